```python
import jax
import jax.numpy as jnp
from jax import lax
import numpy as np

D_MODEL = 1024
BATCH = 8
SEQ = 2048
DEPTH = 4

ATT_HEADS = 8
ATT_KV_HEADS = 2
ATT_HEAD_DIM = 64
ATT_GROUP = ATT_HEADS // ATT_KV_HEADS
ATT_WIDTH = ATT_HEADS * ATT_HEAD_DIM
KV_WIDTH = ATT_KV_HEADS * ATT_HEAD_DIM
WINDOW = 128
ATT_BLOCK = 128
ROPE_THETA = 500000.0
ROPE_DIM = ATT_HEAD_DIM // 4

HG_HEADS = 4
HG_HEAD_DIM = 128
HG_WIDTH = HG_HEADS * HG_HEAD_DIM
HG_CHUNK = 64

MIX_WIDTH = ATT_WIDTH + HG_WIDTH
IN_WIDTH = ATT_WIDTH + 2 * KV_WIDTH + 5 * HG_WIDTH

N_EXPERTS = 32
TOP_K = 4
D_FF = 1024
SWIGLU_ALPHA = 1.702
SWIGLU_LIMIT = 7.0
MOE_BLOCK = 256

N_MOD = 6
EPS = 1e-6
NEG_INF = -1e30
LB_FLOOR = 1e-30

kernel_name = "hymba_swa_hgrn2_moe_adaln_encoder"


def rms_norm(x, gain):
    xf = x.astype(jnp.float32)
    y = xf * lax.rsqrt(jnp.mean(xf * xf, axis=-1, keepdims=True) + EPS)
    return (y * gain.astype(jnp.float32)).astype(x.dtype)


def partial_rope(x, positions):
    half = ROPE_DIM // 2
    inv_freq = ROPE_THETA ** (-(jnp.arange(half, dtype=jnp.float32) * 2.0 / ROPE_DIM))
    ang = positions.astype(jnp.float32)[..., None] * inv_freq
    cos = jnp.cos(ang)[:, :, None, :]
    sin = jnp.sin(ang)[:, :, None, :]
    xr = x[..., :ROPE_DIM].astype(jnp.float32)
    x1, x2 = xr[..., :half], xr[..., half:]
    rot = jnp.concatenate([x1 * cos - x2 * sin, x2 * cos + x1 * sin], axis=-1).astype(x.dtype)
    return jnp.concatenate([rot, x[..., ROPE_DIM:]], axis=-1)


def window_attention(q, k, v, sink):
    B, S = q.shape[0], q.shape[1]
    nb = S // ATT_BLOCK
    qb = q.reshape(B, nb, ATT_BLOCK, ATT_KV_HEADS, ATT_GROUP, ATT_HEAD_DIM)
    pad = ((0, 0), (ATT_BLOCK, ATT_BLOCK), (0, 0), (0, 0))
    kp = jnp.pad(k, pad).reshape(B, nb + 2, ATT_BLOCK, ATT_KV_HEADS, ATT_HEAD_DIM)
    vp = jnp.pad(v, pad).reshape(B, nb + 2, ATT_BLOCK, ATT_KV_HEADS, ATT_HEAD_DIM)
    kb = jnp.concatenate([kp[:, :-2], kp[:, 1:-1], kp[:, 2:]], axis=2)
    vb = jnp.concatenate([vp[:, :-2], vp[:, 1:-1], vp[:, 2:]], axis=2)
    s = jnp.einsum("bnqhgd,bnshd->bnhgqs", qb, kb,
                   preferred_element_type=jnp.float32) * (ATT_HEAD_DIM ** -0.5)
    blk = jnp.arange(nb)[:, None, None]
    qpos = blk * ATT_BLOCK + jnp.arange(ATT_BLOCK)[None, :, None]
    kpos = (blk - 1) * ATT_BLOCK + jnp.arange(3 * ATT_BLOCK)[None, None, :]
    valid = (jnp.abs(qpos - kpos) <= WINDOW) & (kpos >= 0) & (kpos < S)
    s = jnp.where(valid[None, :, None, None], s, NEG_INF)
    sink_col = jnp.broadcast_to(
        sink.astype(jnp.float32).reshape(1, 1, ATT_KV_HEADS, ATT_GROUP, 1, 1), s.shape[:-1] + (1,))
    p = jax.nn.softmax(jnp.concatenate([s, sink_col], axis=-1), axis=-1)[..., :-1]
    o = jnp.einsum("bnhgqs,bnshd->bnqhgd", p.astype(v.dtype), vb)
    return o.reshape(B, S, ATT_WIDTH)


def chunked_gated_scan(q, k, v, log_f):
    N, S, H, DK = q.shape
    DV = v.shape[-1]
    nc = S // HG_CHUNK

    def to_chunks(t):
        return t.reshape(N, nc, HG_CHUNK, H, t.shape[-1]).transpose(1, 0, 3, 2, 4)

    lower_tri = jnp.tril(jnp.ones((HG_CHUNK, HG_CHUNK), dtype=bool))

    def step(state, xs):
        qc, kc, vc, gc = xs
        b = jnp.cumsum(gc, axis=2)
        inter = jnp.einsum("nhcd,nhde->nhce", qc * jnp.exp(b), state)
        rel = jnp.where(lower_tri[:, :, None], b[:, :, :, None, :] - b[:, :, None, :, :], NEG_INF)
        scores = jnp.einsum("nhtd,nhsd,nhtsd->nhts", qc, kc, jnp.exp(rel))
        intra = jnp.einsum("nhts,nhse->nhte", scores, vc)
        b_last = b[:, :, -1:, :]
        new_state = (jnp.exp(b_last[:, :, 0, :])[..., None] * state
                     + jnp.einsum("nhsd,nhse->nhde", kc * jnp.exp(b_last - b), vc))
        return new_state, inter + intra

    state0 = jnp.zeros((N, H, DK, DV), jnp.float32)
    _, out = lax.scan(step, state0, (to_chunks(q), to_chunks(k), to_chunks(v), to_chunks(log_f)))
    return out.transpose(1, 0, 3, 2, 4).reshape(N, S, H, DV)


def hgrn2_bidirectional(q, f_fwd, f_bwd, i, lb_fwd, lb_bwd):
    B, S = q.shape[0], q.shape[1]

    def heads(t):
        return t.astype(jnp.float32).reshape(B, S, HG_HEADS, HG_HEAD_DIM)

    def forget(f, lb):
        f = heads(f)
        lb = lb.astype(jnp.float32).reshape(HG_HEADS, HG_HEAD_DIM)
        log_lb = jnp.log(jnp.maximum(lb, LB_FLOOR))
        log_f = jnp.logaddexp(log_lb, jnp.log1p(-lb) + jax.nn.log_sigmoid(f))
        key = (1.0 - lb) * jax.nn.sigmoid(-f)
        return log_f, key

    qh = jax.nn.silu(heads(q))
    vh = heads(i)
    lf_f, k_f = forget(f_fwd, lb_fwd)
    lf_b, k_b = forget(f_bwd, lb_bwd)
    rev = lambda t: jnp.flip(t, axis=1)
    out = chunked_gated_scan(jnp.concatenate([qh, rev(qh)], axis=0),
                             jnp.concatenate([k_f, rev(k_b)], axis=0),
                             jnp.concatenate([vh, rev(vh)], axis=0),
                             jnp.concatenate([lf_f, rev(lf_b)], axis=0))
    return out[:B] + rev(out[B:])


def hybrid_mixer(h, positions, w_in, sink, attn_gain, lb_fwd, lb_bwd, hg_gain, w_out):
    B, S = h.shape[0], h.shape[1]
    proj = jnp.einsum("bsd,de->bse", h, w_in)
    cuts = np.cumsum([ATT_WIDTH, KV_WIDTH, KV_WIDTH, HG_WIDTH, HG_WIDTH, HG_WIDTH, HG_WIDTH])
    q_a, k_a, v_a, q_h, f_fw, f_bw, i_h, g_h = jnp.split(proj, [int(t) for t in cuts], axis=-1)
    q_a = partial_rope(q_a.reshape(B, S, ATT_HEADS, ATT_HEAD_DIM), positions)
    k_a = partial_rope(k_a.reshape(B, S, ATT_KV_HEADS, ATT_HEAD_DIM), positions)
    v_a = v_a.reshape(B, S, ATT_KV_HEADS, ATT_HEAD_DIM)
    a = rms_norm(window_attention(q_a, k_a, v_a, sink), attn_gain)
    o = hgrn2_bidirectional(q_h, f_fw, f_bw, i_h, lb_fwd, lb_bwd)
    o = rms_norm(o, hg_gain) * jax.nn.silu(g_h.astype(jnp.float32).reshape(B, S, HG_HEADS, HG_HEAD_DIM))
    o = o.reshape(B, S, HG_WIDTH).astype(h.dtype)
    mixed = jnp.concatenate([a.astype(h.dtype), o], axis=-1)
    return jnp.einsum("bse,ed->bsd", mixed, w_out)


def moe_ffn(h, w_router, b_router, w_gu, b_gu, w_down, b_down):
    B, S, D = h.shape
    T = B * S
    xt = h.reshape(T, D)
    logits = (xt @ w_router).astype(jnp.float32) + b_router.astype(jnp.float32)
    top_val, top_idx = lax.top_k(logits, TOP_K)
    gates = jax.nn.softmax(top_val, axis=-1)
    TK = T * TOP_K
    flat_e = top_idx.reshape(TK).astype(jnp.int32)
    flat_tok = jnp.arange(TK, dtype=jnp.int32) // TOP_K
    flat_g = gates.reshape(TK)
    counts = jnp.zeros((N_EXPERTS,), jnp.int32).at[flat_e].add(1)
    padded = (counts + MOE_BLOCK - 1) // MOE_BLOCK * MOE_BLOCK
    start = jnp.cumsum(counts) - counts
    pend = jnp.cumsum(padded)
    pstart = pend - padded
    order = jnp.argsort(flat_e)
    se = flat_e[order]
    dest = pstart[se] + jnp.arange(TK, dtype=jnp.int32) - start[se]
    cap = TK + N_EXPERTS * MOE_BLOCK
    nblk = cap // MOE_BLOCK
    buf_tok = jnp.zeros((cap,), jnp.int32).at[dest].set(flat_tok[order])
    buf_g = jnp.zeros((cap,), jnp.float32).at[dest].set(flat_g[order])
    blk_expert = jnp.clip(jnp.searchsorted(pend, jnp.arange(nblk, dtype=jnp.int32) * MOE_BLOCK,
                                           side="right"), 0, N_EXPERTS - 1).astype(jnp.int32)
    xs = xt[buf_tok].reshape(nblk, MOE_BLOCK, D)

    def expert_block(args):
        xb, e = args
        hu = xb @ w_gu[e] + b_gu[e]
        glu = jnp.minimum(hu[:, :D_FF], SWIGLU_LIMIT)
        lin = jnp.clip(hu[:, D_FF:], -SWIGLU_LIMIT, SWIGLU_LIMIT)
        act = glu * jax.nn.sigmoid(SWIGLU_ALPHA * glu) * (lin + 1.0)
        return act @ w_down[e] + b_down[e]

    ys = lax.map(expert_block, (xs, blk_expert)).reshape(cap, D)
    out = jnp.zeros((T, D), ys.dtype).at[buf_tok].add(ys * buf_g[:, None].astype(ys.dtype))
    return out.reshape(B, S, D).astype(h.dtype)


def setup_inputs(seed: int = 0) -> dict:
    key = jax.random.key(seed)
    ks = jax.random.split(key, 20)
    f32 = jnp.float32

    def nrm(k, shape, scale):
        return scale * jax.random.normal(k, shape, f32)

    def gain(k, shape):
        return 1.0 + 0.02 * jax.random.normal(k, shape, f32)

    return {
        "x": jax.random.normal(ks[0], (BATCH, SEQ, D_MODEL), f32),
        "c": jax.random.normal(ks[1], (BATCH, D_MODEL), f32),
        "positions": jnp.broadcast_to(jnp.arange(SEQ, dtype=jnp.int32), (BATCH, SEQ)),
        "w_ada": nrm(ks[2], (DEPTH, D_MODEL, N_MOD * D_MODEL), 0.5 * D_MODEL ** -0.5),
        "b_ada": nrm(ks[3], (DEPTH, N_MOD * D_MODEL), 0.02),
        "norm1": gain(ks[4], (DEPTH, D_MODEL)),
        "w_in": nrm(ks[5], (DEPTH, D_MODEL, IN_WIDTH), D_MODEL ** -0.5),
        "attn_sink": nrm(ks[6], (DEPTH, ATT_HEADS), 0.5),
        "attn_norm": gain(ks[7], (DEPTH, ATT_WIDTH)),
        "hg_lb_logits": nrm(ks[8], (DEPTH, 2, HG_WIDTH), 0.5),
        "hg_norm": gain(ks[9], (DEPTH, HG_HEAD_DIM)),
        "w_out": nrm(ks[10], (DEPTH, MIX_WIDTH, D_MODEL), MIX_WIDTH ** -0.5),
        "norm2": gain(ks[11], (DEPTH, D_MODEL)),
        "w_router": nrm(ks[12], (DEPTH, D_MODEL, N_EXPERTS), D_MODEL ** -0.5),
        "b_router": nrm(ks[13], (DEPTH, N_EXPERTS), 0.01),
        "w_gu": nrm(ks[14], (DEPTH, N_EXPERTS, D_MODEL, 2 * D_FF), D_MODEL ** -0.5),
        "b_gu": nrm(ks[15], (DEPTH, N_EXPERTS, 2 * D_FF), 0.02),
        "w_down": nrm(ks[16], (DEPTH, N_EXPERTS, D_FF, D_MODEL), D_FF ** -0.5),
        "b_down": nrm(ks[17], (DEPTH, N_EXPERTS, D_MODEL), 0.02),
        "final_norm": gain(ks[18], (D_MODEL,)),
    }


def reference(x, c, positions, w_ada, b_ada, norm1, w_in, attn_sink, attn_norm, hg_lb_logits,
              hg_norm, w_out, norm2, w_router, b_router, w_gu, b_gu, w_down, b_down, final_norm):
    cond = jax.nn.silu(c)
    lb_p = jax.nn.softmax(hg_lb_logits.astype(jnp.float32), axis=0)
    lower = jnp.cumsum(lb_p, axis=0) - lb_p[0]
    for l in range(DEPTH):
        mod = cond @ w_ada[l] + b_ada[l]
        sh1, sc1, g1, sh2, sc2, g2 = jnp.split(mod[:, None, :], N_MOD, axis=-1)
        h = rms_norm(x, norm1[l]) * (1.0 + sc1) + sh1
        x = x + g1 * hybrid_mixer(h, positions, w_in[l], attn_sink[l], attn_norm[l],
                                  lower[l, 0], lower[l, 1], hg_norm[l], w_out[l])
        h = rms_norm(x, norm2[l]) * (1.0 + sc2) + sh2
        x = x + g2 * moe_ffn(h, w_router[l], b_router[l], w_gu[l], b_gu[l], w_down[l], b_down[l])
    return rms_norm(x, final_norm)
```

```python
import functools

import numpy as np
import jax
import jax.numpy as jnp
from jax import lax
from jax.experimental import pallas as pl
from jax.experimental.pallas import tpu as pltpu

F32 = jnp.float32
BF16 = jnp.bfloat16
I32 = jnp.int32

ATT_HEADS = 8
ATT_KV_HEADS = 2
ATT_HEAD_DIM = 64
ATT_WIDTH = ATT_HEADS * ATT_HEAD_DIM
KV_WIDTH = ATT_KV_HEADS * ATT_HEAD_DIM
WINDOW = 128
ATT_BLOCK = 128
ROPE_THETA = 500000.0
ROPE_DIM = ATT_HEAD_DIM // 4
HG_HEADS = 4
HG_HEAD_DIM = 128
HG_WIDTH = HG_HEADS * HG_HEAD_DIM
N_EXPERTS = 32
TOP_K = 4
SWIGLU_ALPHA = 1.702
SWIGLU_LIMIT = 7.0
N_MOD = 6
EPS = 1e-6
NEG_INF = -1e30
LB_FLOOR = 1e-30

LANES = 128
HG_CHUNK = 64
HG_LEVELS = 6
MOE_ROWS = 256
TOK_TILE = 256
PROJ_TILE = 512
RANK_TILE = 512
VMEM_LIMIT = 56 * 1024 * 1024


def _dot(a, b):
    return jnp.dot(a, b, preferred_element_type=F32)


def _dot_nt(a, b):
    return lax.dot_general(a, b, (((1,), (1,)), ((), ())), preferred_element_type=F32)


def _dot_tn(a, b):
    return lax.dot_general(a, b, (((0,), (0,)), ((), ())), preferred_element_type=F32)


def _split3(x):
    hi = x.astype(BF16)
    r1 = x - hi.astype(F32)
    mid = r1.astype(BF16)
    lo = (r1 - mid.astype(F32)).astype(BF16)
    return hi, mid, lo


def _dot_exact_lhs(m_bf16, x):
    hi, mid, lo = _split3(x)
    return _dot(m_bf16, hi) + _dot(m_bf16, mid) + _dot(m_bf16, lo)


def _dot_f32_nt(a, b):
    ah, am, al = _split3(a)
    bh, bm, bl = _split3(b)
    return (_dot_nt(ah, bh) + _dot_nt(ah, bm) + _dot_nt(am, bh)
            + _dot_nt(ah, bl) + _dot_nt(am, bm) + _dot_nt(al, bh))


def _dot_f32(a, b):
    ah, am, al = _split3(a)
    bh, bm, bl = _split3(b)
    return (_dot(ah, bh) + _dot(ah, bm) + _dot(am, bh)
            + _dot(ah, bl) + _dot(am, bm) + _dot(al, bh))


def _sigmoid(x):
    return 1.0 / (1.0 + jnp.exp(-x))


def _params(sem=None):
    return pltpu.CompilerParams(dimension_semantics=sem, vmem_limit_bytes=VMEM_LIMIT)


def _ada_kernel(c_ref, w_ref, b_ref, o_ref):
    c = c_ref[...]
    cond = c * _sigmoid(c)
    o_ref[...] = _dot_f32(cond, w_ref[...]) + b_ref[...]


def _ada_all(c, w_ada, b_ada):
    depth, d, n = w_ada.shape
    b = c.shape[0]
    nt = n // d
    return pl.pallas_call(
        _ada_kernel,
        out_shape=jax.ShapeDtypeStruct((depth, b, n), F32),
        grid=(depth, nt),
        in_specs=[
            pl.BlockSpec((b, d), lambda l, j: (0, 0)),
            pl.BlockSpec((None, d, d), lambda l, j: (l, 0, j)),
            pl.BlockSpec((None, 1, d), lambda l, j: (l, 0, j)),
        ],
        out_specs=pl.BlockSpec((None, b, d), lambda l, j: (l, 0, j)),
        compiler_params=_params(("arbitrary", "arbitrary")),
        name="ada_mod",
    )(c, w_ada, b_ada.reshape(depth, 1, n))


def _lb_kernel(x_ref, o_ref):
    depth = x_ref.shape[0]
    xs = [x_ref[l] for l in range(depth)]
    m = xs[0]
    for l in range(1, depth):
        m = jnp.maximum(m, xs[l])
    es = [jnp.exp(v - m) for v in xs]
    den = es[0]
    for l in range(1, depth):
        den = den + es[l]
    ps = [e / den for e in es]
    run = ps[0]
    o_ref[0] = run - ps[0]
    for l in range(1, depth):
        run = run + ps[l]
        o_ref[l] = run - ps[0]


def _lower_bounds(hg_lb_logits):
    return pl.pallas_call(
        _lb_kernel,
        out_shape=jax.ShapeDtypeStruct(hg_lb_logits.shape, F32),
        name="hg_lower_bounds",
    )(hg_lb_logits.astype(F32))


def _rope_kernel(pos_ref, invf_ref, a_ref, b_ref, c_ref):
    pos = pos_ref[...].astype(F32)
    ang = pos * invf_ref[...]
    cs = jnp.cos(ang)
    sn = jnp.sin(ang)
    lane = lax.broadcasted_iota(I32, ang.shape, 1) & (ATT_HEAD_DIM - 1)
    half = ROPE_DIM // 2
    first = lane < half
    second = (lane >= half) & (lane < ROPE_DIM)
    a_ref[...] = jnp.where(first | second, cs, 1.0)
    b_ref[...] = jnp.where(first, -sn, 0.0)
    c_ref[...] = jnp.where(second, sn, 0.0)


def _rope_tables(positions):
    t = positions.size
    half = ROPE_DIM // 2
    inv = (np.float32(ROPE_THETA) ** (-(np.arange(half, dtype=np.float32) * np.float32(2.0) / np.float32(ROPE_DIM)))).astype(np.float32)
    lane = np.arange(LANES) % ATT_HEAD_DIM
    pat = np.where(lane < ROPE_DIM, inv[lane % half], 0.0).astype(np.float32).reshape(1, LANES)
    tm = min(t, 2048)
    shp = jax.ShapeDtypeStruct((t, LANES), F32)
    spec = pl.BlockSpec((tm, LANES), lambda i: (i, 0))
    return pl.pallas_call(
        _rope_kernel,
        out_shape=(shp, shp, shp),
        grid=(t // tm,),
        in_specs=[pl.BlockSpec((tm, 1), lambda i: (i, 0)), pl.BlockSpec((1, LANES), lambda i: (0, 0))],
        out_specs=(spec, spec, spec),
        compiler_params=_params(("arbitrary",)),
        name="rope_tables",
    )(positions.reshape(t, 1).astype(I32), jnp.asarray(pat))


def _rms_mod(x, gain, scale, shift):
    ms = jnp.mean(x * x, axis=-1, keepdims=True)
    return (x * lax.rsqrt(ms + EPS) * gain) * (1.0 + scale) + shift


def _rope_apply(x, a, b, c):
    half = ROPE_DIM // 2
    return x * a + pltpu.roll(x, LANES - half, 1) * b + pltpu.roll(x, half, 1) * c


def _inproj_kernel(x_ref, mod_ref, n1_ref, w_ref, ra_ref, rb_ref, rc_ref, qa_ref, ka_ref, va_ref, hg_ref, *, d):
    mod = mod_ref[...]
    h = _rms_mod(x_ref[...], n1_ref[...], mod[:, d:2 * d], mod[:, 0:d]).astype(BF16)
    a, b, c = ra_ref[...], rb_ref[...], rc_ref[...]
    kvw = ATT_WIDTH + 2 * KV_WIDTH
    pa = _dot(h, w_ref[:, 0:kvw])
    scale = ATT_HEAD_DIM ** -0.5
    for g in range(ATT_WIDTH // LANES):
        qg = _rope_apply(pa[:, g * LANES:(g + 1) * LANES], a, b, c)
        qa_ref[:, g * LANES:(g + 1) * LANES] = (qg * scale).astype(BF16)
    k = _rope_apply(pa[:, ATT_WIDTH:ATT_WIDTH + KV_WIDTH], a, b, c)
    v = pa[:, ATT_WIDTH + KV_WIDTH:kvw]
    ka_ref[:, 0:LANES] = k.astype(BF16)
    ka_ref[:, LANES:2 * LANES] = pltpu.roll(k, ATT_HEAD_DIM, 1).astype(BF16)
    va_ref[:, 0:LANES] = v.astype(BF16)
    va_ref[:, LANES:2 * LANES] = pltpu.roll(v, ATT_HEAD_DIM, 1).astype(BF16)
    for g in range(5):
        lo = kvw + g * HG_WIDTH
        hg_ref[:, g * HG_WIDTH:(g + 1) * HG_WIDTH] = _dot(h, w_ref[:, lo:lo + HG_WIDTH])


def _inproj(x2, mod3, norm1, w_in_bf, ra, rb, rc, l, seq):
    t, d = x2.shape
    n_in = w_in_bf.shape[-1]
    tm = min(PROJ_TILE, seq)
    nb = mod3.shape[0] // norm1.shape[0]
    tok = lambda i: (i, 0)
    return pl.pallas_call(
        functools.partial(_inproj_kernel, d=d),
        out_shape=(
            jax.ShapeDtypeStruct((t, ATT_WIDTH), BF16),
            jax.ShapeDtypeStruct((t, 2 * KV_WIDTH), BF16),
            jax.ShapeDtypeStruct((t, 2 * KV_WIDTH), BF16),
            jax.ShapeDtypeStruct((t, 5 * HG_WIDTH), F32),
        ),
        grid=(t // tm,),
        in_specs=[
            pl.BlockSpec((tm, d), tok),
            pl.BlockSpec((None, 1, N_MOD * d), lambda i: (l * nb + (i * tm) // seq, 0, 0)),
            pl.BlockSpec((None, 1, d), lambda i: (l, 0, 0)),
            pl.BlockSpec((None, d, n_in), lambda i: (l, 0, 0)),
            pl.BlockSpec((tm, LANES), tok),
            pl.BlockSpec((tm, LANES), tok),
            pl.BlockSpec((tm, LANES), tok),
        ],
        out_specs=(
            pl.BlockSpec((tm, ATT_WIDTH), tok),
            pl.BlockSpec((tm, 2 * KV_WIDTH), tok),
            pl.BlockSpec((tm, 2 * KV_WIDTH), tok),
            pl.BlockSpec((tm, 5 * HG_WIDTH), tok),
        ),
        compiler_params=_params(("arbitrary",)),
        name="in_proj",
    )(x2, mod3, norm1, w_in_bf, ra, rb, rc)


def _attn_kernel(sink_ref, q_ref, kp_ref, kc_ref, kn_ref, vp_ref, vc_ref, vn_ref, gain_ref, o_ref, *, l, seq):
    n = pl.program_id(1)
    blk = ATT_BLOCK
    k2 = jnp.concatenate([kp_ref[...], kc_ref[...], kn_ref[...]], axis=0)
    v2 = jnp.concatenate([vp_ref[...], vc_ref[...], vn_ref[...]], axis=0)
    lane = lax.broadcasted_iota(I32, (3 * blk, LANES), 1)
    lo_half = lane < ATT_HEAD_DIM
    zero = jnp.zeros((3 * blk, LANES), BF16)
    ka, kb = k2[:, 0:LANES], k2[:, LANES:2 * LANES]
    va, vb = v2[:, 0:LANES], v2[:, LANES:2 * LANES]
    kz = [[jnp.where(lo_half, ka, zero), jnp.where(lo_half, zero, kb)],
          [jnp.where(lo_half, kb, zero), jnp.where(lo_half, zero, ka)]]
    vz = [[jnp.where(lo_half, va, zero), jnp.where(lo_half, zero, vb)],
          [jnp.where(lo_half, vb, zero), jnp.where(lo_half, zero, va)]]
    qpos = n * blk + lax.broadcasted_iota(I32, (blk, 3 * blk), 0)
    kpos = (n - 1) * blk + lax.broadcasted_iota(I32, (blk, 3 * blk), 1)
    valid = (jnp.abs(qpos - kpos) <= WINDOW) & (kpos >= 0) & (kpos < seq)
    group = ATT_HEADS // ATT_KV_HEADS
    outs = []
    for m in range(ATT_HEADS // 2):
        qp = q_ref[:, m * LANES:(m + 1) * LANES]
        acc = None
        for half in range(2):
            hh = 2 * m + half
            j = hh // group
            s = _dot_nt(qp, kz[j][half])
            s = jnp.where(valid, s, NEG_INF)
            sink = sink_ref[l, hh]
            mx = jnp.maximum(jnp.max(s, axis=-1, keepdims=True), sink)
            p = jnp.exp(s - mx)
            den = jnp.sum(p, axis=-1, keepdims=True) + jnp.exp(sink - mx)
            p = (p * (1.0 / den)).astype(BF16)
            pv = _dot(p, vz[j][half])
            acc = pv if acc is None else acc + pv
        outs.append(acc)
    o = jnp.concatenate(outs, axis=-1)
    ms = jnp.mean(o * o, axis=-1, keepdims=True)
    o_ref[...] = (o * lax.rsqrt(ms + EPS) * gain_ref[...]).astype(BF16)


def _attention(qa, ka2, va2, attn_sink, attn_norm3, l, batch, seq):
    t = qa.shape[0]
    blk = ATT_BLOCK
    nb = seq // blk
    cur = lambda b, n: (b * nb + n, 0)
    prev = lambda b, n: (b * nb + jnp.maximum(n - 1, 0), 0)
    nxt = lambda b, n: (b * nb + jnp.minimum(n + 1, nb - 1), 0)
    kvspec = lambda f: pl.BlockSpec((blk, 2 * KV_WIDTH), f)
    return pl.pallas_call(
        functools.partial(_attn_kernel, l=l, seq=seq),
        out_shape=jax.ShapeDtypeStruct((t, ATT_WIDTH), BF16),
        grid=(batch, nb),
        in_specs=[
            pl.BlockSpec(memory_space=pltpu.SMEM),
            pl.BlockSpec((blk, ATT_WIDTH), cur),
            kvspec(prev), kvspec(cur), kvspec(nxt),
            kvspec(prev), kvspec(cur), kvspec(nxt),
            pl.BlockSpec((None, 1, ATT_WIDTH), lambda b, n: (l, 0, 0)),
        ],
        out_specs=pl.BlockSpec((blk, ATT_WIDTH), cur),
        compiler_params=_params(("arbitrary", "arbitrary")),
        name="window_attn",
    )(attn_sink, qa, ka2, ka2, ka2, va2, va2, va2, attn_norm3)


def _hgrn_constants():
    c, nl = HG_CHUNK, HG_LEVELS
    r = np.arange(c)
    u = r[None, :]
    rr = r[:, None]
    mats, masks = [], []
    for lev in range(nl):
        m = 1 << lev
        parent = r // (2 * m)
        anchor = parent * 2 * m + m
        upper = r >= anchor
        aa = anchor[:, None]
        mats.append(np.where(upper[:, None], (u > aa) & (u <= rr), (u > rr) & (u <= aa)))
        masks.append((parent[:, None] == parent[None, :]) & upper[:, None] & (~upper)[None, :])
    masks.append(np.eye(c, dtype=bool))
    mats.append(u <= rr)
    mats.append(u > rr)
    mf = np.concatenate(mats, axis=0).astype(np.float32)
    mb = np.concatenate([mt[::-1, ::-1] for mt in mats], axis=0).astype(np.float32)
    kf = np.stack(masks).astype(np.float32)
    kb = np.stack([mk[::-1, ::-1] for mk in masks]).astype(np.float32)
    return mf, mb, kf, kb


def _hgrn_kernel(q_ref, ff_ref, fb_ref, i_ref, g_ref, lb_ref, gn_ref, mf_ref, mb_ref, kf_ref, kb_ref,
                 o_ref, of_scr, ob_scr, st_scr, *, seq):
    c, nl = HG_CHUNK, HG_LEVELS
    nc = seq // c
    lb = lb_ref[...]
    st_scr[...] = jnp.zeros(st_scr.shape, F32)

    def chunk(row0, f_ref, lbrow, m_ref, k_ref, d, o_scr, last_row):
        rows = pl.ds(row0, c)
        lbf = jnp.maximum(lbrow, LB_FLOOR)
        oml = 1.0 - lbrow
        f = f_ref[rows, :]
        e = jnp.exp(-jnp.abs(f))
        r = 1.0 / (1.0 + e)
        er = e * r
        pos = f >= 0.0
        sig = jnp.where(pos, r, er)
        nsig = jnp.where(pos, er, r)
        logf = jnp.log(lbf + oml * sig)
        kk = oml * nsig
        qh = q_ref[rows, :]
        q = qh * _sigmoid(qh)
        v = i_ref[rows, :]
        g_all = jnp.exp(_dot_exact_lhs(m_ref[...], logf))
        a = jnp.zeros((c, c), F32)
        for lev in range(nl):
            gl = g_all[lev * c:(lev + 1) * c, :]
            a = a + k_ref[lev] * _dot_nt((q * gl).astype(BF16), (kk * gl).astype(BF16))
        a = a + k_ref[nl] * _dot_nt(q.astype(BF16), kk.astype(BF16))
        eb = g_all[nl * c:(nl + 1) * c, :]
        erem = g_all[(nl + 1) * c:(nl + 2) * c, :]
        st = st_scr[d]
        vb = v.astype(BF16)
        o = _dot_nt((q * eb).astype(BF16), st.astype(BF16)) + _dot(a.astype(BF16), vb)
        o_scr[rows, :] = o
        st_scr[d] = st * eb[last_row:last_row + 1, :] + _dot_tn(vb, (kk * erem).astype(BF16))

    def body(i, carry):
        chunk(pl.multiple_of(i * c, c), ff_ref, lb[0:1, :], mf_ref, kf_ref, 0, of_scr, c - 1)
        chunk(pl.multiple_of((nc - 1 - i) * c, c), fb_ref, lb[1:2, :], mb_ref, kb_ref, 1, ob_scr, 0)
        return carry

    lax.fori_loop(0, nc, body, 0)

    ep = min(256, seq)
    gn = gn_ref[...]

    def epilogue(j, carry):
        rows = pl.ds(pl.multiple_of(j * ep, ep), ep)
        o = of_scr[rows, :] + ob_scr[rows, :]
        y = o * lax.rsqrt(jnp.mean(o * o, axis=-1, keepdims=True) + EPS) * gn
        g = g_ref[rows, :]
        o_ref[rows, :] = (y * (g * _sigmoid(g))).astype(BF16)
        return carry

    lax.fori_loop(0, seq // ep, epilogue, 0)


def _hgrn(hg, lower, hg_norm3, consts, l, batch, seq):
    t = hg.shape[0]
    mf, mb, kf, kb = consts
    hd = HG_HEAD_DIM

    def col(g):
        return pl.BlockSpec((seq, hd), lambda b, h: (b, g * HG_HEADS + h))

    full2 = lambda a: pl.BlockSpec(a.shape, lambda b, h: (0, 0))
    full3 = lambda a: pl.BlockSpec(a.shape, lambda b, h: (0, 0, 0))
    return pl.pallas_call(
        functools.partial(_hgrn_kernel, seq=seq),
        out_shape=jax.ShapeDtypeStruct((t, HG_WIDTH), BF16),
        grid=(batch, HG_HEADS),
        in_specs=[
            col(0), col(1), col(2), col(3), col(4),
            pl.BlockSpec((None, 2, hd), lambda b, h: (l, 0, h)),
            pl.BlockSpec((None, 1, hd), lambda b, h: (l, 0, 0)),
            full2(mf), full2(mb), full3(kf), full3(kb),
        ],
        out_specs=pl.BlockSpec((seq, hd), lambda b, h: (b, h)),
        scratch_shapes=[
            pltpu.VMEM((seq, hd), F32),
            pltpu.VMEM((seq, hd), F32),
            pltpu.VMEM((2, hd, hd), F32),
        ],
        compiler_params=_params(("arbitrary", "arbitrary")),
        name="hgrn2_scan",
    )(hg, hg, hg, hg, hg, lower, hg_norm3, mf, mb, kf, kb)


def _outproj_kernel(a_ref, o_ref, x_ref, mod_ref, n2_ref, w_ref, wr_ref, br_ref,
                    xo_ref, h2_ref, ti_ref, gt_ref, *, d):
    mod = mod_ref[...]
    y = _dot(a_ref[...], w_ref[0:ATT_WIDTH, :]) + _dot(o_ref[...], w_ref[ATT_WIDTH:ATT_WIDTH + HG_WIDTH, :])
    xn = x_ref[...] + mod[:, 2 * d:3 * d] * y
    xo_ref[...] = xn
    h2 = _rms_mod(xn, n2_ref[...], mod[:, 4 * d:5 * d], mod[:, 3 * d:4 * d])
    h2_ref[...] = h2
    lg = _dot_f32_nt(wr_ref[...], h2) + br_ref[...]
    eidx = lax.broadcasted_iota(I32, lg.shape, 0)
    vals, idxs = [], []
    for _ in range(TOP_K):
        mx = jnp.max(lg, axis=0, keepdims=True)
        sel = jnp.min(jnp.where(lg == mx, eidx, N_EXPERTS), axis=0, keepdims=True)
        vals.append(mx)
        idxs.append(sel)
        lg = jnp.where(eidx == sel, -jnp.inf, lg)
    ex = [jnp.exp(v - vals[0]) for v in vals]
    den = ex[0]
    for e in ex[1:]:
        den = den + e
    inv = 1.0 / den
    ti_ref[...] = jnp.concatenate(idxs, axis=0)
    gt_ref[...] = jnp.concatenate([e * inv for e in ex], axis=0)


def _outproj(a, o, x2, mod3, norm2, w_out_bf, w_router_t, b_router3, l, seq):
    t, d = x2.shape
    tm = min(PROJ_TILE, seq)
    nb = mod3.shape[0] // norm2.shape[0]
    tok = lambda i: (i, 0)
    lane_tok = lambda i: (0, i)
    return pl.pallas_call(
        functools.partial(_outproj_kernel, d=d),
        out_shape=(
            jax.ShapeDtypeStruct((t, d), F32),
            jax.ShapeDtypeStruct((t, d), F32),
            jax.ShapeDtypeStruct((TOP_K, t), I32),
            jax.ShapeDtypeStruct((TOP_K, t), F32),
        ),
        grid=(t // tm,),
        in_specs=[
            pl.BlockSpec((tm, ATT_WIDTH), tok),
            pl.BlockSpec((tm, HG_WIDTH), tok),
            pl.BlockSpec((tm, d), tok),
            pl.BlockSpec((None, 1, N_MOD * d), lambda i: (l * nb + (i * tm) // seq, 0, 0)),
            pl.BlockSpec((None, 1, d), lambda i: (l, 0, 0)),
            pl.BlockSpec((None, ATT_WIDTH + HG_WIDTH, d), lambda i: (l, 0, 0)),
            pl.BlockSpec((None, N_EXPERTS, d), lambda i: (l, 0, 0)),
            pl.BlockSpec((None, N_EXPERTS, 1), lambda i: (l, 0, 0)),
        ],
        out_specs=(
            pl.BlockSpec((tm, d), tok),
            pl.BlockSpec((tm, d), tok),
            pl.BlockSpec((TOP_K, tm), lane_tok),
            pl.BlockSpec((TOP_K, tm), lane_tok),
        ),
        compiler_params=_params(("arbitrary",)),
        name="out_proj_router",
    )(a, o, x2, mod3, norm2, w_out_bf, w_router_t, b_router3)


def _rank_kernel(ti_ref, tri_ref, rank_ref, cnt_ref, carry_scr):
    @pl.when(pl.program_id(0) == 0)
    def _():
        carry_scr[...] = jnp.zeros(carry_scr.shape, F32)

    ti = ti_ref[...]
    tl = ti.shape[1]
    eidx = lax.broadcasted_iota(I32, (N_EXPERTS, tl), 0)
    carry = carry_scr[...]
    rows = []
    for k in range(TOP_K):
        oh = eidx == ti[k:k + 1, :]
        ohf = jnp.where(oh, 1.0, 0.0)
        pre = _dot(ohf.astype(BF16), tri_ref[...])
        rows.append(jnp.sum(jnp.where(oh, carry + pre, 0.0), axis=0, keepdims=True))
        carry = carry + jnp.sum(ohf, axis=1, keepdims=True)
    carry_scr[...] = carry
    rank_ref[...] = jnp.concatenate(rows, axis=0).astype(I32)
    cnt_ref[...] = jnp.broadcast_to(carry, cnt_ref.shape)


def _ranks(topi):
    k, t = topi.shape
    tl = min(RANK_TILE, t)
    tri = np.triu(np.ones((tl, tl), np.float32), 1)
    return pl.pallas_call(
        _rank_kernel,
        out_shape=(jax.ShapeDtypeStruct((k, t), I32), jax.ShapeDtypeStruct((N_EXPERTS, LANES), F32)),
        grid=(t // tl,),
        in_specs=[pl.BlockSpec((k, tl), lambda i: (0, i)), pl.BlockSpec((tl, tl), lambda i: (0, 0))],
        out_specs=(pl.BlockSpec((k, tl), lambda i: (0, i)), pl.BlockSpec((N_EXPERTS, LANES), lambda i: (0, 0))),
        scratch_shapes=[pltpu.VMEM((N_EXPERTS, 1), F32)],
        compiler_params=_params(("arbitrary",)),
        name="route_rank",
    )(topi, jnp.asarray(tri, BF16))


def _dest_kernel(cnt_ref, ltri_ref, ti_ref, rank_ref, dest_ref, bexp_ref, nused_ref):
    cnt = cnt_ref[...]
    nblk = jnp.floor((cnt + (MOE_ROWS - 1)) * (1.0 / MOE_ROWS))
    pstart_b = _dot(ltri_ref[...], nblk.astype(BF16))
    pend_b = pstart_b + nblk
    pstart = (pstart_b[:, 0:1] * MOE_ROWS).astype(I32)
    ti = ti_ref[...]
    tl = ti.shape[1]
    eidx = lax.broadcasted_iota(I32, (N_EXPERTS, tl), 0)
    rows = []
    for k in range(TOP_K):
        oh = eidx == ti[k:k + 1, :]
        rows.append(jnp.sum(jnp.where(oh, pstart, 0), axis=0, keepdims=True))
    dest_ref[...] = jnp.concatenate(rows, axis=0) + rank_ref[...]
    nb = bexp_ref.shape[1]
    bi = lax.broadcasted_iota(I32, (N_EXPERTS, nb), 1).astype(F32)
    be = jnp.sum(jnp.where(pend_b[:, 0:1] <= bi, 1, 0), axis=0, keepdims=True)
    bexp_ref[...] = jnp.minimum(be, N_EXPERTS - 1).astype(I32)
    nused_ref[...] = pend_b[N_EXPERTS - 1:N_EXPERTS, :].astype(I32)


def _destinations(counts, topi, rank, nblocks):
    k, t = topi.shape
    tl = min(2048, t)
    nbp = -(-nblocks // LANES) * LANES
    ltri = np.tril(np.ones((N_EXPERTS, N_EXPERTS), np.float32), -1)
    return pl.pallas_call(
        _dest_kernel,
        out_shape=(
            jax.ShapeDtypeStruct((k, t), I32),
            jax.ShapeDtypeStruct((1, nbp), I32),
            jax.ShapeDtypeStruct((1, LANES), I32),
        ),
        grid=(t // tl,),
        in_specs=[
            pl.BlockSpec((N_EXPERTS, LANES), lambda i: (0, 0)),
            pl.BlockSpec((N_EXPERTS, N_EXPERTS), lambda i: (0, 0)),
            pl.BlockSpec((k, tl), lambda i: (0, i)),
            pl.BlockSpec((k, tl), lambda i: (0, i)),
        ],
        out_specs=(
            pl.BlockSpec((k, tl), lambda i: (0, i)),
            pl.BlockSpec((1, nbp), lambda i: (0, 0)),
            pl.BlockSpec((1, LANES), lambda i: (0, 0)),
        ),
        compiler_params=_params(("arbitrary",)),
        name="route_dest",
    )(counts, jnp.asarray(ltri, BF16), topi, rank)


def _dispatch_kernel(dest_hbm, h2_ref, xs_in, xs_out, dsm, isem, sem):
    del xs_in
    i = pl.program_id(0)
    tm = h2_ref.shape[0]
    icp = pltpu.make_async_copy(dest_hbm.at[i], dsm, isem)
    icp.start()
    icp.wait()

    def row_copy(r, dst):
        return pltpu.make_async_copy(h2_ref.at[pl.ds(r, 1), :], xs_out.at[pl.ds(dst, 1), :], sem)

    def body(r, carry):
        for k in range(TOP_K):
            row_copy(r, dsm[k, r]).start()
        return carry

    lax.fori_loop(0, tm, body, 0)

    def drain(r, carry):
        for k in range(TOP_K):
            row_copy(r, dsm[k, r]).wait()
        return carry

    lax.fori_loop(0, tm, drain, 0)


def _dispatch(dest3, h2, cap):
    t, d = h2.shape
    tm = dest3.shape[2]
    xs0 = jnp.zeros((cap, d), h2.dtype)
    return pl.pallas_call(
        _dispatch_kernel,
        out_shape=jax.ShapeDtypeStruct((cap, d), h2.dtype),
        grid=(t // tm,),
        in_specs=[
            pl.BlockSpec(memory_space=pl.ANY),
            pl.BlockSpec((tm, d), lambda i: (i, 0)),
            pl.BlockSpec(memory_space=pl.ANY),
        ],
        out_specs=pl.BlockSpec(memory_space=pl.ANY),
        scratch_shapes=[
            pltpu.SMEM((TOP_K, tm), I32),
            pltpu.SemaphoreType.DMA(()),
            pltpu.SemaphoreType.DMA(()),
        ],
        input_output_aliases={2: 0},
        compiler_params=_params(("arbitrary",)),
        name="moe_dispatch",
    )(dest3, h2, xs0)


def _expert_kernel(bexp_ref, nused_ref, xs_ref, wgu_ref, bgu_ref, wd_ref, bd_ref, ys_ref, *, dff):
    del bexp_ref
    used = pl.program_id(0) < nused_ref[0]

    @pl.when(jnp.logical_not(used))
    def _():
        ys_ref[...] = jnp.zeros(ys_ref.shape, F32)

    @pl.when(used)
    def _():
        xb = xs_ref[...].astype(BF16)
        hu = _dot(xb, wgu_ref[...]) + bgu_ref[...]
        glu = jnp.minimum(hu[:, 0:dff], SWIGLU_LIMIT)
        lin = jnp.clip(hu[:, dff:2 * dff], -SWIGLU_LIMIT, SWIGLU_LIMIT)
        act = glu * _sigmoid(SWIGLU_ALPHA * glu) * (lin + 1.0)
        ys_ref[...] = _dot(act.astype(BF16), wd_ref[...]) + bd_ref[...]


def _experts(bexp, nused, xs, w_gu_bf, b_gu4, w_down_bf, b_down4, l):
    cap, d = xs.shape
    dff = w_down_bf.shape[2]
    nblocks = cap // MOE_ROWS

    def blk(i, be, nu):
        return jnp.minimum(i, nu[0] - 1)

    rows = lambda i, be, nu: (blk(i, be, nu), 0)
    wsel = lambda i, be, nu: (l, be[blk(i, be, nu)], 0, 0)
    grid_spec = pltpu.PrefetchScalarGridSpec(
        num_scalar_prefetch=2,
        grid=(nblocks,),
        in_specs=[
            pl.BlockSpec((MOE_ROWS, d), rows),
            pl.BlockSpec((None, None, d, 2 * dff), wsel),
            pl.BlockSpec((None, None, 1, 2 * dff), wsel),
            pl.BlockSpec((None, None, dff, d), wsel),
            pl.BlockSpec((None, None, 1, d), wsel),
        ],
        out_specs=pl.BlockSpec((MOE_ROWS, d), lambda i, be, nu: (i, 0)),
    )
    return pl.pallas_call(
        functools.partial(_expert_kernel, dff=dff),
        out_shape=jax.ShapeDtypeStruct((cap, d), F32),
        grid_spec=grid_spec,
        compiler_params=_params(("arbitrary",)),
        name="moe_experts",
    )(bexp, nused, xs, w_gu_bf, b_gu4, w_down_bf, b_down4)


def _combine_kernel(dest_hbm, gt_ref, x_ref, mod_ref, ys_hbm, xo_ref, dsm, rows_scr, isem, sem, *, d):
    i = pl.program_id(0)
    tm = x_ref.shape[0]
    icp = pltpu.make_async_copy(dest_hbm.at[i], dsm, isem)
    icp.start()
    icp.wait()

    def row_copy(k, r, src):
        return pltpu.make_async_copy(ys_hbm.at[pl.ds(src, 1), :], rows_scr.at[k, pl.ds(r, 1), :], sem)

    def body(r, carry):
        for k in range(TOP_K):
            row_copy(k, r, dsm[k, r]).start()
        return carry

    lax.fori_loop(0, tm, body, 0)

    def drain(r, carry):
        for k in range(TOP_K):
            row_copy(k, r, dsm[k, r]).wait()
        return carry

    lax.fori_loop(0, tm, drain, 0)

    gt = gt_ref[...]
    acc = gt[:, 0:1] * rows_scr[0]
    for k in range(1, TOP_K):
        acc = acc + gt[:, k:k + 1] * rows_scr[k]
    xo_ref[...] = x_ref[...] + mod_ref[:, 5 * d:6 * d] * acc


def _combine(dest3, gates_t, x2, mod3, ys, l, seq, depth):
    t, d = x2.shape
    tm = dest3.shape[2]
    nb = mod3.shape[0] // depth
    return pl.pallas_call(
        functools.partial(_combine_kernel, d=d),
        out_shape=jax.ShapeDtypeStruct((t, d), F32),
        grid=(t // tm,),
        in_specs=[
            pl.BlockSpec(memory_space=pl.ANY),
            pl.BlockSpec((tm, TOP_K), lambda i: (i, 0)),
            pl.BlockSpec((tm, d), lambda i: (i, 0)),
            pl.BlockSpec((None, 1, N_MOD * d), lambda i: (l * nb + (i * tm) // seq, 0, 0)),
            pl.BlockSpec(memory_space=pl.ANY),
        ],
        out_specs=pl.BlockSpec((tm, d), lambda i: (i, 0)),
        scratch_shapes=[
            pltpu.SMEM((TOP_K, tm), I32),
            pltpu.VMEM((TOP_K, tm, d), F32),
            pltpu.SemaphoreType.DMA(()),
            pltpu.SemaphoreType.DMA(()),
        ],
        compiler_params=_params(("arbitrary",)),
        name="moe_combine",
    )(dest3, gates_t, x2, mod3, ys)


def _final_kernel(x_ref, g_ref, o_ref):
    x = x_ref[...]
    o_ref[...] = x * lax.rsqrt(jnp.mean(x * x, axis=-1, keepdims=True) + EPS) * g_ref[...]


def _final_norm(x2, gain):
    t, d = x2.shape
    tm = min(PROJ_TILE, t)
    return pl.pallas_call(
        _final_kernel,
        out_shape=jax.ShapeDtypeStruct((t, d), F32),
        grid=(t // tm,),
        in_specs=[pl.BlockSpec((tm, d), lambda i: (i, 0)), pl.BlockSpec((1, d), lambda i: (0, 0))],
        out_specs=pl.BlockSpec((tm, d), lambda i: (i, 0)),
        compiler_params=_params(("arbitrary",)),
        name="final_norm",
    )(x2, gain.reshape(1, d))


def _mixer_layer(x2, mod3, tables, lower, consts, p, l, batch, seq):
    ra, rb, rc = tables
    qa, ka2, va2, hg = _inproj(x2, mod3, p["norm1"], p["w_in"], ra, rb, rc, l, seq)
    a = _attention(qa, ka2, va2, p["attn_sink"], p["attn_norm"], l, batch, seq)
    o = _hgrn(hg, lower, p["hg_norm"], consts, l, batch, seq)
    return a, o


def _moe_layer(a, o, x2, mod3, p, l, seq, depth):
    t, d = x2.shape
    xn, h2, topi, gates = _outproj(a, o, x2, mod3, p["norm2"], p["w_out"], p["w_router_t"], p["b_router"], l, seq)
    rank, counts = _ranks(topi)
    cap = t * TOP_K + N_EXPERTS * MOE_ROWS
    dest, bexp, nused = _destinations(counts, topi, rank, cap // MOE_ROWS)
    tm = min(TOK_TILE, t)
    dest3 = dest.reshape(TOP_K, t // tm, tm).transpose(1, 0, 2)
    xs = _dispatch(dest3, h2, cap)
    ys = _experts(bexp.reshape(-1), nused.reshape(-1), xs, p["w_gu"], p["b_gu"], p["w_down"], p["b_down"], l)
    return _combine(dest3, gates.T, xn, mod3, ys, l, seq, depth)


def kernel(x, c, positions, w_ada, b_ada, norm1, w_in, attn_sink, attn_norm, hg_lb_logits, hg_norm, w_out, norm2,
           w_router, b_router, w_gu, b_gu, w_down, b_down, final_norm):
    batch, seq, d = x.shape
    depth = w_ada.shape[0]
    t = batch * seq
    p = {
        "norm1": norm1.reshape(depth, 1, d),
        "w_in": w_in.astype(BF16),
        "attn_sink": attn_sink.astype(F32),
        "attn_norm": attn_norm.reshape(depth, 1, ATT_WIDTH),
        "hg_norm": hg_norm.reshape(depth, 1, HG_HEAD_DIM),
        "w_out": w_out.astype(BF16),
        "norm2": norm2.reshape(depth, 1, d),
        "w_router_t": jnp.swapaxes(w_router, 1, 2),
        "b_router": b_router.reshape(depth, N_EXPERTS, 1),
        "w_gu": w_gu.astype(BF16),
        "b_gu": b_gu.reshape(depth, N_EXPERTS, 1, b_gu.shape[-1]),
        "w_down": w_down.astype(BF16),
        "b_down": b_down.reshape(depth, N_EXPERTS, 1, d),
    }
    mod3 = _ada_all(c, w_ada, b_ada).reshape(depth * batch, 1, N_MOD * d)
    lower = _lower_bounds(hg_lb_logits)
    tables = _rope_tables(positions)
    consts = _hgrn_constants()
    consts = (jnp.asarray(consts[0], BF16), jnp.asarray(consts[1], BF16), jnp.asarray(consts[2]), jnp.asarray(consts[3]))
    x2 = x.reshape(t, d)
    for l in range(depth):
        a, o = _mixer_layer(x2, mod3, tables, lower, consts, p, l, batch, seq)
        x2 = _moe_layer(a, o, x2, mod3, p, l, seq, depth)
    return _final_norm(x2, final_norm).reshape(batch, seq, d)
```

```python
import functools

import numpy as np
import jax
import jax.numpy as jnp
from jax import lax
from jax.experimental import pallas as pl
from jax.experimental.pallas import tpu as pltpu

F32 = jnp.float32
BF16 = jnp.bfloat16
I32 = jnp.int32

ATT_HEADS = 8
ATT_KV_HEADS = 2
ATT_HEAD_DIM = 64
ATT_WIDTH = ATT_HEADS * ATT_HEAD_DIM
KV_WIDTH = ATT_KV_HEADS * ATT_HEAD_DIM
WINDOW = 128
ATT_BLOCK = 128
ROPE_THETA = 500000.0
ROPE_DIM = ATT_HEAD_DIM // 4
HG_HEADS = 4
HG_HEAD_DIM = 128
HG_WIDTH = HG_HEADS * HG_HEAD_DIM
N_EXPERTS = 32
TOP_K = 4
SWIGLU_ALPHA = 1.702
SWIGLU_LIMIT = 7.0
N_MOD = 6
EPS = 1e-6
NEG_INF = -1e30
LB_FLOOR = 1e-30

LANES = 128
HG_CHUNK = 64
HG_LEVELS = 6
HG_UNROLL = 8
MOE_ROWS = 256
TOK_TILE = 256
PROJ_TILE = 512
RANK_TILE = 512
VMEM_LIMIT = 56 * 1024 * 1024


def _dot(a, b):
    return jnp.dot(a, b, preferred_element_type=F32)


def _dot_nt(a, b):
    return lax.dot_general(a, b, (((1,), (1,)), ((), ())), preferred_element_type=F32)


def _dot_tn(a, b):
    return lax.dot_general(a, b, (((0,), (0,)), ((), ())), preferred_element_type=F32)


def _split3(x):
    hi = x.astype(BF16)
    r1 = x - hi.astype(F32)
    mid = r1.astype(BF16)
    lo = (r1 - mid.astype(F32)).astype(BF16)
    return hi, mid, lo


def _dot_exact_lhs(m_bf16, x):
    hi, mid, lo = _split3(x)
    return _dot(m_bf16, hi) + _dot(m_bf16, mid) + _dot(m_bf16, lo)


def _dot_f32_nt(a, b):
    ah, am, al = _split3(a)
    bh, bm, bl = _split3(b)
    return (_dot_nt(ah, bh) + _dot_nt(ah, bm) + _dot_nt(am, bh)
            + _dot_nt(ah, bl) + _dot_nt(am, bm) + _dot_nt(al, bh))


def _dot_f32(a, b):
    ah, am, al = _split3(a)
    bh, bm, bl = _split3(b)
    return (_dot(ah, bh) + _dot(ah, bm) + _dot(am, bh)
            + _dot(ah, bl) + _dot(am, bm) + _dot(al, bh))


def _sigmoid(x):
    return 1.0 / (1.0 + jnp.exp(-x))


def _params(sem=None):
    return pltpu.CompilerParams(dimension_semantics=sem, vmem_limit_bytes=VMEM_LIMIT)


def _ada_kernel(c_ref, w_ref, b_ref, o_ref):
    c = c_ref[...]
    cond = c * _sigmoid(c)
    o_ref[...] = _dot_f32(cond, w_ref[...]) + b_ref[...]


def _ada_all(c, w_ada, b_ada):
    depth, d, n = w_ada.shape
    b = c.shape[0]
    nt = n // d
    return pl.pallas_call(
        _ada_kernel,
        out_shape=jax.ShapeDtypeStruct((depth, b, n), F32),
        grid=(depth, nt),
        in_specs=[
            pl.BlockSpec((b, d), lambda l, j: (0, 0)),
            pl.BlockSpec((None, d, d), lambda l, j: (l, 0, j)),
            pl.BlockSpec((None, 1, d), lambda l, j: (l, 0, j)),
        ],
        out_specs=pl.BlockSpec((None, b, d), lambda l, j: (l, 0, j)),
        compiler_params=_params(("arbitrary", "arbitrary")),
        name="ada_mod",
    )(c, w_ada, b_ada.reshape(depth, 1, n))


def _lb_kernel(x_ref, o_ref):
    depth = x_ref.shape[0]
    xs = [x_ref[l] for l in range(depth)]
    m = xs[0]
    for l in range(1, depth):
        m = jnp.maximum(m, xs[l])
    es = [jnp.exp(v - m) for v in xs]
    den = es[0]
    for l in range(1, depth):
        den = den + es[l]
    ps = [e / den for e in es]
    run = ps[0]
    o_ref[0] = run - ps[0]
    for l in range(1, depth):
        run = run + ps[l]
        o_ref[l] = run - ps[0]


def _lower_bounds(hg_lb_logits):
    return pl.pallas_call(
        _lb_kernel,
        out_shape=jax.ShapeDtypeStruct(hg_lb_logits.shape, F32),
        name="hg_lower_bounds",
    )(hg_lb_logits.astype(F32))


def _rope_kernel(pos_ref, invf_ref, a_ref, b_ref, c_ref):
    pos = pos_ref[...].astype(F32)
    ang = pos * invf_ref[...]
    cs = jnp.cos(ang)
    sn = jnp.sin(ang)
    lane = lax.broadcasted_iota(I32, ang.shape, 1) & (ATT_HEAD_DIM - 1)
    half = ROPE_DIM // 2
    first = lane < half
    second = (lane >= half) & (lane < ROPE_DIM)
    a_ref[...] = jnp.where(first | second, cs, 1.0)
    b_ref[...] = jnp.where(first, -sn, 0.0)
    c_ref[...] = jnp.where(second, sn, 0.0)


def _rope_tables(positions):
    t = positions.size
    half = ROPE_DIM // 2
    inv = (np.float32(ROPE_THETA) ** (-(np.arange(half, dtype=np.float32) * np.float32(2.0) / np.float32(ROPE_DIM)))).astype(np.float32)
    lane = np.arange(LANES) % ATT_HEAD_DIM
    pat = np.where(lane < ROPE_DIM, inv[lane % half], 0.0).astype(np.float32).reshape(1, LANES)
    tm = min(t, 2048)
    shp = jax.ShapeDtypeStruct((t, LANES), F32)
    spec = pl.BlockSpec((tm, LANES), lambda i: (i, 0))
    return pl.pallas_call(
        _rope_kernel,
        out_shape=(shp, shp, shp),
        grid=(t // tm,),
        in_specs=[pl.BlockSpec((tm, 1), lambda i: (i, 0)), pl.BlockSpec((1, LANES), lambda i: (0, 0))],
        out_specs=(spec, spec, spec),
        compiler_params=_params(("arbitrary",)),
        name="rope_tables",
    )(positions.reshape(t, 1).astype(I32), jnp.asarray(pat))


def _rms_mod(x, gain, scale, shift):
    ms = jnp.mean(x * x, axis=-1, keepdims=True)
    return (x * lax.rsqrt(ms + EPS) * gain) * (1.0 + scale) + shift


def _rope_apply(x, a, b, c):
    half = ROPE_DIM // 2
    return x * a + pltpu.roll(x, LANES - half, 1) * b + pltpu.roll(x, half, 1) * c


def _inproj_kernel(x_ref, mod_ref, n1_ref, w_ref, ra_ref, rb_ref, rc_ref, qa_ref, ka_ref, va_ref, hg_ref, *, d):
    mod = mod_ref[...]
    h = _rms_mod(x_ref[...], n1_ref[...], mod[:, d:2 * d], mod[:, 0:d]).astype(BF16)
    a, b, c = ra_ref[...], rb_ref[...], rc_ref[...]
    kvw = ATT_WIDTH + 2 * KV_WIDTH
    pa = _dot(h, w_ref[:, 0:kvw])
    scale = ATT_HEAD_DIM ** -0.5
    for g in range(ATT_WIDTH // LANES):
        qg = _rope_apply(pa[:, g * LANES:(g + 1) * LANES], a, b, c)
        qa_ref[:, g * LANES:(g + 1) * LANES] = (qg * scale).astype(BF16)
    k = _rope_apply(pa[:, ATT_WIDTH:ATT_WIDTH + KV_WIDTH], a, b, c)
    v = pa[:, ATT_WIDTH + KV_WIDTH:kvw]
    ka_ref[:, 0:LANES] = k.astype(BF16)
    ka_ref[:, LANES:2 * LANES] = pltpu.roll(k, ATT_HEAD_DIM, 1).astype(BF16)
    va_ref[:, 0:LANES] = v.astype(BF16)
    va_ref[:, LANES:2 * LANES] = pltpu.roll(v, ATT_HEAD_DIM, 1).astype(BF16)
    for g in range(5):
        lo = kvw + g * HG_WIDTH
        hg_ref[:, g * HG_WIDTH:(g + 1) * HG_WIDTH] = _dot(h, w_ref[:, lo:lo + HG_WIDTH])


def _inproj(x2, mod3, norm1, w_in_bf, ra, rb, rc, l, seq):
    t, d = x2.shape
    n_in = w_in_bf.shape[-1]
    tm = min(PROJ_TILE, seq)
    nb = mod3.shape[0] // norm1.shape[0]
    tok = lambda i: (i, 0)
    return pl.pallas_call(
        functools.partial(_inproj_kernel, d=d),
        out_shape=(
            jax.ShapeDtypeStruct((t, ATT_WIDTH), BF16),
            jax.ShapeDtypeStruct((t, 2 * KV_WIDTH), BF16),
            jax.ShapeDtypeStruct((t, 2 * KV_WIDTH), BF16),
            jax.ShapeDtypeStruct((t, 5 * HG_WIDTH), F32),
        ),
        grid=(t // tm,),
        in_specs=[
            pl.BlockSpec((tm, d), tok),
            pl.BlockSpec((None, 1, N_MOD * d), lambda i: (l * nb + (i * tm) // seq, 0, 0)),
            pl.BlockSpec((None, 1, d), lambda i: (l, 0, 0)),
            pl.BlockSpec((None, d, n_in), lambda i: (l, 0, 0)),
            pl.BlockSpec((tm, LANES), tok),
            pl.BlockSpec((tm, LANES), tok),
            pl.BlockSpec((tm, LANES), tok),
        ],
        out_specs=(
            pl.BlockSpec((tm, ATT_WIDTH), tok),
            pl.BlockSpec((tm, 2 * KV_WIDTH), tok),
            pl.BlockSpec((tm, 2 * KV_WIDTH), tok),
            pl.BlockSpec((tm, 5 * HG_WIDTH), tok),
        ),
        compiler_params=_params(("arbitrary",)),
        name="in_proj",
    )(x2, mod3, norm1, w_in_bf, ra, rb, rc)


def _attn_kernel(sink_ref, q_ref, kp_ref, kc_ref, kn_ref, vp_ref, vc_ref, vn_ref, gain_ref, o_ref, *, l, seq):
    n = pl.program_id(1)
    blk = ATT_BLOCK
    k2 = jnp.concatenate([kp_ref[...], kc_ref[...], kn_ref[...]], axis=0)
    v2 = jnp.concatenate([vp_ref[...], vc_ref[...], vn_ref[...]], axis=0)
    lane = lax.broadcasted_iota(I32, (3 * blk, LANES), 1)
    lo_half = lane < ATT_HEAD_DIM
    zero = jnp.zeros((3 * blk, LANES), BF16)
    ka, kb = k2[:, 0:LANES], k2[:, LANES:2 * LANES]
    va, vb = v2[:, 0:LANES], v2[:, LANES:2 * LANES]
    kz = [[jnp.where(lo_half, ka, zero), jnp.where(lo_half, zero, kb)],
          [jnp.where(lo_half, kb, zero), jnp.where(lo_half, zero, ka)]]
    vz = [[jnp.where(lo_half, va, zero), jnp.where(lo_half, zero, vb)],
          [jnp.where(lo_half, vb, zero), jnp.where(lo_half, zero, va)]]
    qpos = n * blk + lax.broadcasted_iota(I32, (blk, 3 * blk), 0)
    kpos = (n - 1) * blk + lax.broadcasted_iota(I32, (blk, 3 * blk), 1)
    valid = (jnp.abs(qpos - kpos) <= WINDOW) & (kpos >= 0) & (kpos < seq)
    group = ATT_HEADS // ATT_KV_HEADS
    outs = []
    for m in range(ATT_HEADS // 2):
        qp = q_ref[:, m * LANES:(m + 1) * LANES]
        acc = None
        for half in range(2):
            hh = 2 * m + half
            j = hh // group
            s = _dot_nt(qp, kz[j][half])
            s = jnp.where(valid, s, NEG_INF)
            sink = sink_ref[l, hh]
            mx = jnp.maximum(jnp.max(s, axis=-1, keepdims=True), sink)
            p = jnp.exp(s - mx)
            den = jnp.sum(p, axis=-1, keepdims=True) + jnp.exp(sink - mx)
            p = (p * (1.0 / den)).astype(BF16)
            pv = _dot(p, vz[j][half])
            acc = pv if acc is None else acc + pv
        outs.append(acc)
    o = jnp.concatenate(outs, axis=-1)
    ms = jnp.mean(o * o, axis=-1, keepdims=True)
    o_ref[...] = (o * lax.rsqrt(ms + EPS) * gain_ref[...]).astype(BF16)


def _attention(qa, ka2, va2, attn_sink, attn_norm3, l, batch, seq):
    t = qa.shape[0]
    blk = ATT_BLOCK
    nb = seq // blk
    cur = lambda b, n: (b * nb + n, 0)
    prev = lambda b, n: (b * nb + jnp.maximum(n - 1, 0), 0)
    nxt = lambda b, n: (b * nb + jnp.minimum(n + 1, nb - 1), 0)
    kvspec = lambda f: pl.BlockSpec((blk, 2 * KV_WIDTH), f)
    return pl.pallas_call(
        functools.partial(_attn_kernel, l=l, seq=seq),
        out_shape=jax.ShapeDtypeStruct((t, ATT_WIDTH), BF16),
        grid=(batch, nb),
        in_specs=[
            pl.BlockSpec(memory_space=pltpu.SMEM),
            pl.BlockSpec((blk, ATT_WIDTH), cur),
            kvspec(prev), kvspec(cur), kvspec(nxt),
            kvspec(prev), kvspec(cur), kvspec(nxt),
            pl.BlockSpec((None, 1, ATT_WIDTH), lambda b, n: (l, 0, 0)),
        ],
        out_specs=pl.BlockSpec((blk, ATT_WIDTH), cur),
        compiler_params=_params(("arbitrary", "arbitrary")),
        name="window_attn",
    )(attn_sink, qa, ka2, ka2, ka2, va2, va2, va2, attn_norm3)


def _hgrn_constants():
    c, nl = HG_CHUNK, HG_LEVELS
    r = np.arange(c)
    u = r[None, :]
    rr = r[:, None]
    mats, masks = [], []
    for lev in range(nl):
        m = 1 << lev
        parent = r // (2 * m)
        anchor = parent * 2 * m + m
        upper = r >= anchor
        aa = anchor[:, None]
        mats.append(np.where(upper[:, None], (u > aa) & (u <= rr), (u > rr) & (u <= aa)))
        masks.append((parent[:, None] == parent[None, :]) & upper[:, None] & (~upper)[None, :])
    masks.append(np.eye(c, dtype=bool))
    mats.append(u <= rr)
    mats.append(u > rr)
    mf = np.concatenate(mats, axis=0).astype(np.float32)
    mb = np.concatenate([mt[::-1, ::-1] for mt in mats], axis=0).astype(np.float32)
    kf = np.stack(masks).astype(np.float32)
    kb = np.stack([mk[::-1, ::-1] for mk in masks]).astype(np.float32)
    return mf, mb, kf, kb


def _hgrn_kernel(q_ref, ff_ref, fb_ref, i_ref, g_ref, lb_ref, gn_ref, mf_ref, mb_ref, kf_ref, kb_ref,
                 o_ref, of_scr, ob_scr, st_scr, *, seq):
    c, nl = HG_CHUNK, HG_LEVELS
    nc = seq // c
    lb = lb_ref[...]
    st_scr[...] = jnp.zeros(st_scr.shape, F32)

    unroll = HG_UNROLL if nc % HG_UNROLL == 0 else 1
    dirs = (
        dict(f_ref=ff_ref, lbrow=lb[0:1, :], m_ref=mf_ref, k_ref=kf_ref, o_scr=of_scr, last_row=c - 1, d=0),
        dict(f_ref=fb_ref, lbrow=lb[1:2, :], m_ref=mb_ref, k_ref=kb_ref, o_scr=ob_scr, last_row=0, d=1),
    )

    def body(i, carry):
        work = []
        for u in range(unroll):
            cf = i * unroll + u
            work.append((dirs[0], pl.ds(pl.multiple_of(cf * c, c), c)))
            work.append((dirs[1], pl.ds(pl.multiple_of((nc - 1 - cf) * c, c), c)))
        gates = []
        for dr, rows in work:
            lbf = jnp.maximum(dr["lbrow"], LB_FLOOR)
            oml = 1.0 - dr["lbrow"]
            f = dr["f_ref"][rows, :]
            e = jnp.exp(-jnp.abs(f))
            r = 1.0 / (1.0 + e)
            er = e * r
            pos = f >= 0.0
            logf = jnp.log(lbf + oml * jnp.where(pos, r, er))
            kk = oml * jnp.where(pos, er, r)
            qh = q_ref[rows, :]
            gates.append((logf, kk, qh * _sigmoid(qh), i_ref[rows, :].astype(BF16)))
        gall = [jnp.exp(_dot_exact_lhs(dr["m_ref"][...], g[0])) for (dr, _), g in zip(work, gates)]
        amat = [dr["k_ref"][nl] * _dot_nt(g[2].astype(BF16), g[1].astype(BF16)) for (dr, _), g in zip(work, gates)]
        for lev in range(nl):
            for j, ((dr, _), g) in enumerate(zip(work, gates)):
                gl = gall[j][lev * c:(lev + 1) * c, :]
                amat[j] = amat[j] + dr["k_ref"][lev] * _dot_nt((g[2] * gl).astype(BF16), (g[1] * gl).astype(BF16))
        intra = [_dot(amat[j].astype(BF16), g[3]) for j, g in enumerate(gates)]
        upd = [_dot_tn(g[3], (g[1] * gall[j][(nl + 1) * c:(nl + 2) * c, :]).astype(BF16)) for j, g in enumerate(gates)]
        st = [st_scr[0], st_scr[1]]
        for j, ((dr, rows), g) in enumerate(zip(work, gates)):
            eb = gall[j][nl * c:(nl + 1) * c, :]
            d = dr["d"]
            dr["o_scr"][rows, :] = _dot_nt((g[2] * eb).astype(BF16), st[d].astype(BF16)) + intra[j]
            st[d] = st[d] * eb[dr["last_row"]:dr["last_row"] + 1, :] + upd[j]
        st_scr[0] = st[0]
        st_scr[1] = st[1]
        return carry

    lax.fori_loop(0, nc // unroll, body, 0)

    ep = min(256, seq)
    gn = gn_ref[...]

    def epilogue(j, carry):
        rows = pl.ds(pl.multiple_of(j * ep, ep), ep)
        o = of_scr[rows, :] + ob_scr[rows, :]
        y = o * lax.rsqrt(jnp.mean(o * o, axis=-1, keepdims=True) + EPS) * gn
        g = g_ref[rows, :]
        o_ref[rows, :] = (y * (g * _sigmoid(g))).astype(BF16)
        return carry

    lax.fori_loop(0, seq // ep, epilogue, 0)


def _hgrn(hg, lower, hg_norm3, consts, l, batch, seq):
    t = hg.shape[0]
    mf, mb, kf, kb = consts
    hd = HG_HEAD_DIM

    def col(g):
        return pl.BlockSpec((seq, hd), lambda b, h: (b, g * HG_HEADS + h))

    full2 = lambda a: pl.BlockSpec(a.shape, lambda b, h: (0, 0))
    full3 = lambda a: pl.BlockSpec(a.shape, lambda b, h: (0, 0, 0))
    return pl.pallas_call(
        functools.partial(_hgrn_kernel, seq=seq),
        out_shape=jax.ShapeDtypeStruct((t, HG_WIDTH), BF16),
        grid=(batch, HG_HEADS),
        in_specs=[
            col(0), col(1), col(2), col(3), col(4),
            pl.BlockSpec((None, 2, hd), lambda b, h: (l, 0, h)),
            pl.BlockSpec((None, 1, hd), lambda b, h: (l, 0, 0)),
            full2(mf), full2(mb), full3(kf), full3(kb),
        ],
        out_specs=pl.BlockSpec((seq, hd), lambda b, h: (b, h)),
        scratch_shapes=[
            pltpu.VMEM((seq, hd), F32),
            pltpu.VMEM((seq, hd), F32),
            pltpu.VMEM((2, hd, hd), F32),
        ],
        compiler_params=_params(("arbitrary", "arbitrary")),
        name="hgrn2_scan",
    )(hg, hg, hg, hg, hg, lower, hg_norm3, mf, mb, kf, kb)


def _outproj_kernel(a_ref, o_ref, x_ref, mod_ref, n2_ref, w_ref, wr_ref, br_ref,
                    xo_ref, h2_ref, ti_ref, gt_ref, *, d):
    mod = mod_ref[...]
    y = _dot(a_ref[...], w_ref[0:ATT_WIDTH, :]) + _dot(o_ref[...], w_ref[ATT_WIDTH:ATT_WIDTH + HG_WIDTH, :])
    xn = x_ref[...] + mod[:, 2 * d:3 * d] * y
    xo_ref[...] = xn
    h2 = _rms_mod(xn, n2_ref[...], mod[:, 4 * d:5 * d], mod[:, 3 * d:4 * d])
    h2_ref[...] = h2
    lg = _dot_f32_nt(wr_ref[...], h2) + br_ref[...]
    eidx = lax.broadcasted_iota(I32, lg.shape, 0)
    vals, idxs = [], []
    for _ in range(TOP_K):
        mx = jnp.max(lg, axis=0, keepdims=True)
        sel = jnp.min(jnp.where(lg == mx, eidx, N_EXPERTS), axis=0, keepdims=True)
        vals.append(mx)
        idxs.append(sel)
        lg = jnp.where(eidx == sel, -jnp.inf, lg)
    ex = [jnp.exp(v - vals[0]) for v in vals]
    den = ex[0]
    for e in ex[1:]:
        den = den + e
    inv = 1.0 / den
    ti_ref[...] = jnp.concatenate(idxs, axis=0)
    gt_ref[...] = jnp.concatenate([e * inv for e in ex], axis=0)


def _outproj(a, o, x2, mod3, norm2, w_out_bf, w_router_t, b_router3, l, seq):
    t, d = x2.shape
    tm = min(PROJ_TILE, seq)
    nb = mod3.shape[0] // norm2.shape[0]
    tok = lambda i: (i, 0)
    lane_tok = lambda i: (0, i)
    return pl.pallas_call(
        functools.partial(_outproj_kernel, d=d),
        out_shape=(
            jax.ShapeDtypeStruct((t, d), F32),
            jax.ShapeDtypeStruct((t, d), F32),
            jax.ShapeDtypeStruct((TOP_K, t), I32),
            jax.ShapeDtypeStruct((TOP_K, t), F32),
        ),
        grid=(t // tm,),
        in_specs=[
            pl.BlockSpec((tm, ATT_WIDTH), tok),
            pl.BlockSpec((tm, HG_WIDTH), tok),
            pl.BlockSpec((tm, d), tok),
            pl.BlockSpec((None, 1, N_MOD * d), lambda i: (l * nb + (i * tm) // seq, 0, 0)),
            pl.BlockSpec((None, 1, d), lambda i: (l, 0, 0)),
            pl.BlockSpec((None, ATT_WIDTH + HG_WIDTH, d), lambda i: (l, 0, 0)),
            pl.BlockSpec((None, N_EXPERTS, d), lambda i: (l, 0, 0)),
            pl.BlockSpec((None, N_EXPERTS, 1), lambda i: (l, 0, 0)),
        ],
        out_specs=(
            pl.BlockSpec((tm, d), tok),
            pl.BlockSpec((tm, d), tok),
            pl.BlockSpec((TOP_K, tm), lane_tok),
            pl.BlockSpec((TOP_K, tm), lane_tok),
        ),
        compiler_params=_params(("arbitrary",)),
        name="out_proj_router",
    )(a, o, x2, mod3, norm2, w_out_bf, w_router_t, b_router3)


def _rank_kernel(ti_ref, tri_ref, rank_ref, cnt_ref, carry_scr):
    @pl.when(pl.program_id(0) == 0)
    def _():
        carry_scr[...] = jnp.zeros(carry_scr.shape, F32)

    ti = ti_ref[...]
    tl = ti.shape[1]
    eidx = lax.broadcasted_iota(I32, (N_EXPERTS, tl), 0)
    carry = carry_scr[...]
    rows = []
    for k in range(TOP_K):
        oh = eidx == ti[k:k + 1, :]
        ohf = jnp.where(oh, 1.0, 0.0)
        pre = _dot(ohf.astype(BF16), tri_ref[...])
        rows.append(jnp.sum(jnp.where(oh, carry + pre, 0.0), axis=0, keepdims=True))
        carry = carry + jnp.sum(ohf, axis=1, keepdims=True)
    carry_scr[...] = carry
    rank_ref[...] = jnp.concatenate(rows, axis=0).astype(I32)
    cnt_ref[...] = jnp.broadcast_to(carry, cnt_ref.shape)


def _ranks(topi):
    k, t = topi.shape
    tl = min(RANK_TILE, t)
    tri = np.triu(np.ones((tl, tl), np.float32), 1)
    return pl.pallas_call(
        _rank_kernel,
        out_shape=(jax.ShapeDtypeStruct((k, t), I32), jax.ShapeDtypeStruct((N_EXPERTS, LANES), F32)),
        grid=(t // tl,),
        in_specs=[pl.BlockSpec((k, tl), lambda i: (0, i)), pl.BlockSpec((tl, tl), lambda i: (0, 0))],
        out_specs=(pl.BlockSpec((k, tl), lambda i: (0, i)), pl.BlockSpec((N_EXPERTS, LANES), lambda i: (0, 0))),
        scratch_shapes=[pltpu.VMEM((N_EXPERTS, 1), F32)],
        compiler_params=_params(("arbitrary",)),
        name="route_rank",
    )(topi, jnp.asarray(tri, BF16))


def _dest_kernel(cnt_ref, ltri_ref, ti_ref, rank_ref, dest_ref, bexp_ref, nused_ref):
    cnt = cnt_ref[...]
    nblk = jnp.floor((cnt + (MOE_ROWS - 1)) * (1.0 / MOE_ROWS))
    pstart_b = _dot(ltri_ref[...], nblk.astype(BF16))
    pend_b = pstart_b + nblk
    pstart = (pstart_b[:, 0:1] * MOE_ROWS).astype(I32)
    ti = ti_ref[...]
    tl = ti.shape[1]
    eidx = lax.broadcasted_iota(I32, (N_EXPERTS, tl), 0)
    rows = []
    for k in range(TOP_K):
        oh = eidx == ti[k:k + 1, :]
        rows.append(jnp.sum(jnp.where(oh, pstart, 0), axis=0, keepdims=True))
    dest_ref[...] = jnp.concatenate(rows, axis=0) + rank_ref[...]
    nb = bexp_ref.shape[1]
    bi = lax.broadcasted_iota(I32, (N_EXPERTS, nb), 1).astype(F32)
    be = jnp.sum(jnp.where(pend_b[:, 0:1] <= bi, 1, 0), axis=0, keepdims=True)
    bexp_ref[...] = jnp.minimum(be, N_EXPERTS - 1).astype(I32)
    nused_ref[...] = pend_b[N_EXPERTS - 1:N_EXPERTS, :].astype(I32)


def _destinations(counts, topi, rank, nblocks):
    k, t = topi.shape
    tl = min(2048, t)
    nbp = -(-nblocks // LANES) * LANES
    ltri = np.tril(np.ones((N_EXPERTS, N_EXPERTS), np.float32), -1)
    return pl.pallas_call(
        _dest_kernel,
        out_shape=(
            jax.ShapeDtypeStruct((k, t), I32),
            jax.ShapeDtypeStruct((1, nbp), I32),
            jax.ShapeDtypeStruct((1, LANES), I32),
        ),
        grid=(t // tl,),
        in_specs=[
            pl.BlockSpec((N_EXPERTS, LANES), lambda i: (0, 0)),
            pl.BlockSpec((N_EXPERTS, N_EXPERTS), lambda i: (0, 0)),
            pl.BlockSpec((k, tl), lambda i: (0, i)),
            pl.BlockSpec((k, tl), lambda i: (0, i)),
        ],
        out_specs=(
            pl.BlockSpec((k, tl), lambda i: (0, i)),
            pl.BlockSpec((1, nbp), lambda i: (0, 0)),
            pl.BlockSpec((1, LANES), lambda i: (0, 0)),
        ),
        compiler_params=_params(("arbitrary",)),
        name="route_dest",
    )(counts, jnp.asarray(ltri, BF16), topi, rank)


def _dispatch_kernel(dest_hbm, h2_ref, xs_in, xs_out, dsm, isem, sem):
    del xs_in
    i = pl.program_id(0)
    tm = h2_ref.shape[0]
    icp = pltpu.make_async_copy(dest_hbm.at[i], dsm, isem)
    icp.start()
    icp.wait()

    def row_copy(r, dst):
        return pltpu.make_async_copy(h2_ref.at[pl.ds(r, 1), :], xs_out.at[pl.ds(dst, 1), :], sem)

    def body(r, carry):
        for k in range(TOP_K):
            row_copy(r, dsm[k, r]).start(priority=k % 2)
        return carry

    lax.fori_loop(0, tm, body, 0)

    for k in range(TOP_K):
        pltpu.make_async_copy(h2_ref, xs_out.at[pl.ds(0, tm), :], sem).wait()


def _dispatch(dest3, h2, cap):
    t, d = h2.shape
    tm = dest3.shape[2]
    xs0 = jnp.zeros((cap, d), h2.dtype)
    return pl.pallas_call(
        _dispatch_kernel,
        out_shape=jax.ShapeDtypeStruct((cap, d), h2.dtype),
        grid=(t // tm,),
        in_specs=[
            pl.BlockSpec(memory_space=pl.ANY),
            pl.BlockSpec((tm, d), lambda i: (i, 0)),
            pl.BlockSpec(memory_space=pl.ANY),
        ],
        out_specs=pl.BlockSpec(memory_space=pl.ANY),
        scratch_shapes=[
            pltpu.SMEM((TOP_K, tm), I32),
            pltpu.SemaphoreType.DMA(()),
            pltpu.SemaphoreType.DMA(()),
        ],
        input_output_aliases={2: 0},
        compiler_params=_params(("arbitrary",)),
        name="moe_dispatch",
    )(dest3, h2, xs0)


def _expert_kernel(bexp_ref, nused_ref, xs_ref, wgu_ref, bgu_ref, wd_ref, bd_ref, ys_ref, wgu_bf, wd_bf, *, dff):
    i = pl.program_id(0)
    used = i < nused_ref[0]
    fresh = jnp.logical_or(i == 0, bexp_ref[i] != bexp_ref[jnp.maximum(i - 1, 0)])

    @pl.when(jnp.logical_not(used))
    def _():
        ys_ref[...] = jnp.zeros(ys_ref.shape, F32)

    @pl.when(jnp.logical_and(used, fresh))
    def _():
        wgu_bf[...] = wgu_ref[...].astype(BF16)
        wd_bf[...] = wd_ref[...].astype(BF16)

    @pl.when(used)
    def _():
        xb = xs_ref[...].astype(BF16)
        hu = _dot(xb, wgu_bf[...]) + bgu_ref[...]
        glu = jnp.minimum(hu[:, 0:dff], SWIGLU_LIMIT)
        lin = jnp.clip(hu[:, dff:2 * dff], -SWIGLU_LIMIT, SWIGLU_LIMIT)
        act = glu * _sigmoid(SWIGLU_ALPHA * glu) * (lin + 1.0)
        ys_ref[...] = _dot(act.astype(BF16), wd_bf[...]) + bd_ref[...]


def _experts(bexp, nused, xs, w_gu, b_gu4, w_down, b_down4, l):
    cap, d = xs.shape
    dff = w_down.shape[2]
    nblocks = cap // MOE_ROWS

    def blk(i, be, nu):
        return jnp.minimum(i, nu[0] - 1)

    rows = lambda i, be, nu: (blk(i, be, nu), 0)
    wsel = lambda i, be, nu: (l, be[blk(i, be, nu)], 0, 0)
    grid_spec = pltpu.PrefetchScalarGridSpec(
        num_scalar_prefetch=2,
        grid=(nblocks,),
        in_specs=[
            pl.BlockSpec((MOE_ROWS, d), rows),
            pl.BlockSpec((None, None, d, 2 * dff), wsel),
            pl.BlockSpec((None, None, 1, 2 * dff), wsel),
            pl.BlockSpec((None, None, dff, d), wsel),
            pl.BlockSpec((None, None, 1, d), wsel),
        ],
        out_specs=pl.BlockSpec((MOE_ROWS, d), lambda i, be, nu: (i, 0)),
        scratch_shapes=[pltpu.VMEM((d, 2 * dff), BF16), pltpu.VMEM((dff, d), BF16)],
    )
    return pl.pallas_call(
        functools.partial(_expert_kernel, dff=dff),
        out_shape=jax.ShapeDtypeStruct((cap, d), F32),
        grid_spec=grid_spec,
        compiler_params=_params(("arbitrary",)),
        name="moe_experts",
    )(bexp, nused, xs, w_gu, b_gu4, w_down, b_down4)


def _combine_kernel(dest_hbm, gt_ref, x_ref, mod_ref, ys_hbm, xo_ref, dsm, rows_scr, isem, sem, *, d):
    i = pl.program_id(0)
    tm = x_ref.shape[0]
    icp = pltpu.make_async_copy(dest_hbm.at[i], dsm, isem)
    icp.start()
    icp.wait()

    def row_copy(k, r, src):
        return pltpu.make_async_copy(ys_hbm.at[pl.ds(src, 1), :], rows_scr.at[k, pl.ds(r, 1), :], sem)

    def body(r, carry):
        for k in range(TOP_K):
            row_copy(k, r, dsm[k, r]).start(priority=k % 2)
        return carry

    lax.fori_loop(0, tm, body, 0)

    for k in range(TOP_K):
        pltpu.make_async_copy(ys_hbm.at[pl.ds(0, tm), :], rows_scr.at[k], sem).wait()

    gt = gt_ref[...]
    acc = gt[:, 0:1] * rows_scr[0]
    for k in range(1, TOP_K):
        acc = acc + gt[:, k:k + 1] * rows_scr[k]
    xo_ref[...] = x_ref[...] + mod_ref[:, 5 * d:6 * d] * acc


def _combine(dest3, gates_t, x2, mod3, ys, l, seq, depth):
    t, d = x2.shape
    tm = dest3.shape[2]
    nb = mod3.shape[0] // depth
    return pl.pallas_call(
        functools.partial(_combine_kernel, d=d),
        out_shape=jax.ShapeDtypeStruct((t, d), F32),
        grid=(t // tm,),
        in_specs=[
            pl.BlockSpec(memory_space=pl.ANY),
            pl.BlockSpec((tm, TOP_K), lambda i: (i, 0)),
            pl.BlockSpec((tm, d), lambda i: (i, 0)),
            pl.BlockSpec((None, 1, N_MOD * d), lambda i: (l * nb + (i * tm) // seq, 0, 0)),
            pl.BlockSpec(memory_space=pl.ANY),
        ],
        out_specs=pl.BlockSpec((tm, d), lambda i: (i, 0)),
        scratch_shapes=[
            pltpu.SMEM((TOP_K, tm), I32),
            pltpu.VMEM((TOP_K, tm, d), F32),
            pltpu.SemaphoreType.DMA(()),
            pltpu.SemaphoreType.DMA(()),
        ],
        compiler_params=_params(("arbitrary",)),
        name="moe_combine",
    )(dest3, gates_t, x2, mod3, ys)


def _final_kernel(x_ref, g_ref, o_ref):
    x = x_ref[...]
    o_ref[...] = x * lax.rsqrt(jnp.mean(x * x, axis=-1, keepdims=True) + EPS) * g_ref[...]


def _final_norm(x2, gain):
    t, d = x2.shape
    tm = min(PROJ_TILE, t)
    return pl.pallas_call(
        _final_kernel,
        out_shape=jax.ShapeDtypeStruct((t, d), F32),
        grid=(t // tm,),
        in_specs=[pl.BlockSpec((tm, d), lambda i: (i, 0)), pl.BlockSpec((1, d), lambda i: (0, 0))],
        out_specs=pl.BlockSpec((tm, d), lambda i: (i, 0)),
        compiler_params=_params(("arbitrary",)),
        name="final_norm",
    )(x2, gain.reshape(1, d))


def _mixer_layer(x2, mod3, tables, lower, consts, p, l, batch, seq):
    ra, rb, rc = tables
    qa, ka2, va2, hg = _inproj(x2, mod3, p["norm1"], p["w_in"], ra, rb, rc, l, seq)
    a = _attention(qa, ka2, va2, p["attn_sink"], p["attn_norm"], l, batch, seq)
    o = _hgrn(hg, lower, p["hg_norm"], consts, l, batch, seq)
    return a, o


def _moe_layer(a, o, x2, mod3, p, l, seq, depth):
    t, d = x2.shape
    xn, h2, topi, gates = _outproj(a, o, x2, mod3, p["norm2"], p["w_out"], p["w_router_t"], p["b_router"], l, seq)
    rank, counts = _ranks(topi)
    cap = t * TOP_K + N_EXPERTS * MOE_ROWS
    dest, bexp, nused = _destinations(counts, topi, rank, cap // MOE_ROWS)
    tm = min(TOK_TILE, t)
    dest3 = dest.reshape(TOP_K, t // tm, tm).transpose(1, 0, 2)
    xs = _dispatch(dest3, h2, cap)
    ys = _experts(bexp.reshape(-1), nused.reshape(-1), xs, p["w_gu"], p["b_gu"], p["w_down"], p["b_down"], l)
    return _combine(dest3, gates.T, xn, mod3, ys, l, seq, depth)


def kernel(x, c, positions, w_ada, b_ada, norm1, w_in, attn_sink, attn_norm, hg_lb_logits, hg_norm, w_out, norm2,
           w_router, b_router, w_gu, b_gu, w_down, b_down, final_norm):
    batch, seq, d = x.shape
    depth = w_ada.shape[0]
    t = batch * seq
    p = {
        "norm1": norm1.reshape(depth, 1, d),
        "w_in": w_in.astype(BF16),
        "attn_sink": attn_sink.astype(F32),
        "attn_norm": attn_norm.reshape(depth, 1, ATT_WIDTH),
        "hg_norm": hg_norm.reshape(depth, 1, HG_HEAD_DIM),
        "w_out": w_out.astype(BF16),
        "norm2": norm2.reshape(depth, 1, d),
        "w_router_t": jnp.swapaxes(w_router, 1, 2),
        "b_router": b_router.reshape(depth, N_EXPERTS, 1),
        "w_gu": w_gu,
        "b_gu": b_gu.reshape(depth, N_EXPERTS, 1, b_gu.shape[-1]),
        "w_down": w_down,
        "b_down": b_down.reshape(depth, N_EXPERTS, 1, d),
    }
    mod3 = _ada_all(c, w_ada, b_ada).reshape(depth * batch, 1, N_MOD * d)
    lower = _lower_bounds(hg_lb_logits)
    tables = _rope_tables(positions)
    consts = _hgrn_constants()
    consts = (jnp.asarray(consts[0], BF16), jnp.asarray(consts[1], BF16), jnp.asarray(consts[2]), jnp.asarray(consts[3]))
    x2 = x.reshape(t, d)
    for l in range(depth):
        a, o = _mixer_layer(x2, mod3, tables, lower, consts, p, l, batch, seq)
        x2 = _moe_layer(a, o, x2, mod3, p, l, seq, depth)
    return _final_norm(x2, final_norm).reshape(batch, seq, d)
```

```python
import functools

import numpy as np
import jax
import jax.numpy as jnp
from jax import lax
from jax.experimental import pallas as pl
from jax.experimental.pallas import tpu as pltpu

F32 = jnp.float32
BF16 = jnp.bfloat16
I32 = jnp.int32
U32 = jnp.uint32

ATT_HEADS = 8
ATT_KV_HEADS = 2
ATT_HEAD_DIM = 64
ATT_WIDTH = ATT_HEADS * ATT_HEAD_DIM
KV_WIDTH = ATT_KV_HEADS * ATT_HEAD_DIM
WINDOW = 128
ATT_BLOCK = 128
ROPE_THETA = 500000.0
ROPE_DIM = ATT_HEAD_DIM // 4
HG_HEADS = 4
HG_HEAD_DIM = 128
HG_WIDTH = HG_HEADS * HG_HEAD_DIM
N_EXPERTS = 32
TOP_K = 4
SWIGLU_ALPHA = 1.702
SWIGLU_LIMIT = 7.0
N_MOD = 6
EPS = 1e-6
NEG_INF = -1e30
LB_FLOOR = 1e-30

LANES = 128
HG_CHUNK = 64
HG_LEVELS = 6
HG_UNROLL = 8
MOE_SHIFT = 8
MOE_ROWS = 1 << MOE_SHIFT
TOK_TILE = 256
PROJ_TILE = 512
RANK_TILE = 512
VMEM_LIMIT = 56 * 1024 * 1024


def _dot(a, b):
    return jnp.dot(a, b, preferred_element_type=F32)


def _dot_nt(a, b):
    return lax.dot_general(a, b, (((1,), (1,)), ((), ())), preferred_element_type=F32)


def _dot_tn(a, b):
    return lax.dot_general(a, b, (((0,), (0,)), ((), ())), preferred_element_type=F32)


def _split3(x):
    hi = x.astype(BF16)
    r1 = x - hi.astype(F32)
    mid = r1.astype(BF16)
    lo = (r1 - mid.astype(F32)).astype(BF16)
    return hi, mid, lo


def _dot_exact_lhs(m_bf16, x):
    hi, mid, lo = _split3(x)
    return _dot(m_bf16, hi) + _dot(m_bf16, mid) + _dot(m_bf16, lo)


def _dot_f32_nt(a, b):
    ah, am, al = _split3(a)
    bh, bm, bl = _split3(b)
    return (_dot_nt(ah, bh) + _dot_nt(ah, bm) + _dot_nt(am, bh)
            + _dot_nt(ah, bl) + _dot_nt(am, bm) + _dot_nt(al, bh))


def _dot_f32(a, b):
    ah, am, al = _split3(a)
    bh, bm, bl = _split3(b)
    return (_dot(ah, bh) + _dot(ah, bm) + _dot(am, bh)
            + _dot(ah, bl) + _dot(am, bm) + _dot(al, bh))


def _sigmoid(x):
    return 1.0 / (1.0 + jnp.exp(-x))


def _params(sem=None):
    return pltpu.CompilerParams(dimension_semantics=sem, vmem_limit_bytes=VMEM_LIMIT)


def _ada_kernel(c_ref, w_ref, b_ref, o_ref):
    c = c_ref[...]
    cond = c * _sigmoid(c)
    o_ref[...] = _dot_f32(cond, w_ref[...]) + b_ref[...]


def _ada_all(c, w_ada, b_ada):
    depth, d, n = w_ada.shape
    b = c.shape[0]
    nt = n // d
    return pl.pallas_call(
        _ada_kernel,
        out_shape=jax.ShapeDtypeStruct((depth, b, n), F32),
        grid=(depth, nt),
        in_specs=[
            pl.BlockSpec((b, d), lambda l, j: (0, 0)),
            pl.BlockSpec((None, d, d), lambda l, j: (l, 0, j)),
            pl.BlockSpec((None, 1, d), lambda l, j: (l, 0, j)),
        ],
        out_specs=pl.BlockSpec((None, b, d), lambda l, j: (l, 0, j)),
        compiler_params=_params(("arbitrary", "arbitrary")),
        name="ada_mod",
    )(c, w_ada, b_ada.reshape(depth, 1, n))


def _lb_kernel(x_ref, o_ref):
    depth = x_ref.shape[0]
    xs = [x_ref[l] for l in range(depth)]
    m = xs[0]
    for l in range(1, depth):
        m = jnp.maximum(m, xs[l])
    es = [jnp.exp(v - m) for v in xs]
    den = es[0]
    for l in range(1, depth):
        den = den + es[l]
    ps = [e / den for e in es]
    run = ps[0]
    o_ref[0] = run - ps[0]
    for l in range(1, depth):
        run = run + ps[l]
        o_ref[l] = run - ps[0]


def _lower_bounds(hg_lb_logits):
    return pl.pallas_call(
        _lb_kernel,
        out_shape=jax.ShapeDtypeStruct(hg_lb_logits.shape, F32),
        name="hg_lower_bounds",
    )(hg_lb_logits.astype(F32))


def _rope_kernel(pos_ref, invf_ref, a_ref, b_ref, c_ref):
    pos = pos_ref[...].astype(F32)
    ang = pos * invf_ref[...]
    cs = jnp.cos(ang)
    sn = jnp.sin(ang)
    lane = lax.broadcasted_iota(I32, ang.shape, 1) & (ATT_HEAD_DIM - 1)
    half = ROPE_DIM // 2
    first = lane < half
    second = (lane >= half) & (lane < ROPE_DIM)
    a_ref[...] = jnp.where(first | second, cs, 1.0)
    b_ref[...] = jnp.where(first, -sn, 0.0)
    c_ref[...] = jnp.where(second, sn, 0.0)


def _rope_tables(positions):
    t = positions.size
    half = ROPE_DIM // 2
    inv = (np.float32(ROPE_THETA) ** (-(np.arange(half, dtype=np.float32) * np.float32(2.0) / np.float32(ROPE_DIM)))).astype(np.float32)
    lane = np.arange(LANES) % ATT_HEAD_DIM
    pat = np.where(lane < ROPE_DIM, inv[lane % half], 0.0).astype(np.float32).reshape(1, LANES)
    tm = min(t, 2048)
    shp = jax.ShapeDtypeStruct((t, LANES), F32)
    spec = pl.BlockSpec((tm, LANES), lambda i: (i, 0))
    return pl.pallas_call(
        _rope_kernel,
        out_shape=(shp, shp, shp),
        grid=(t // tm,),
        in_specs=[pl.BlockSpec((tm, 1), lambda i: (i, 0)), pl.BlockSpec((1, LANES), lambda i: (0, 0))],
        out_specs=(spec, spec, spec),
        compiler_params=_params(("arbitrary",)),
        name="rope_tables",
    )(positions.reshape(t, 1).astype(I32), jnp.asarray(pat))


def _rms_mod(x, gain, scale, shift):
    ms = jnp.mean(x * x, axis=-1, keepdims=True)
    return (x * lax.rsqrt(ms + EPS) * gain) * (1.0 + scale) + shift


def _rope_apply(x, a, b, c):
    half = ROPE_DIM // 2
    return x * a + pltpu.roll(x, LANES - half, 1) * b + pltpu.roll(x, half, 1) * c


def _inproj_kernel(x_ref, mod_ref, n1_ref, w_ref, ra_ref, rb_ref, rc_ref, qa_ref, ka_ref, va_ref, hg_ref, *, d):
    mod = mod_ref[...]
    h = _rms_mod(x_ref[...], n1_ref[...], mod[:, d:2 * d], mod[:, 0:d]).astype(BF16)
    a, b, c = ra_ref[...], rb_ref[...], rc_ref[...]
    kvw = ATT_WIDTH + 2 * KV_WIDTH
    pa = _dot(h, w_ref[:, 0:kvw])
    scale = ATT_HEAD_DIM ** -0.5
    for g in range(ATT_WIDTH // LANES):
        qg = _rope_apply(pa[:, g * LANES:(g + 1) * LANES], a, b, c)
        qa_ref[:, g * LANES:(g + 1) * LANES] = (qg * scale).astype(BF16)
    k = _rope_apply(pa[:, ATT_WIDTH:ATT_WIDTH + KV_WIDTH], a, b, c)
    v = pa[:, ATT_WIDTH + KV_WIDTH:kvw]
    ka_ref[:, 0:LANES] = k.astype(BF16)
    ka_ref[:, LANES:2 * LANES] = pltpu.roll(k, ATT_HEAD_DIM, 1).astype(BF16)
    va_ref[:, 0:LANES] = v.astype(BF16)
    va_ref[:, LANES:2 * LANES] = pltpu.roll(v, ATT_HEAD_DIM, 1).astype(BF16)
    for g in range(5):
        lo = kvw + g * HG_WIDTH
        hg_ref[:, g * HG_WIDTH:(g + 1) * HG_WIDTH] = _dot(h, w_ref[:, lo:lo + HG_WIDTH])


def _inproj(x2, mod3, norm1, w_in_bf, ra, rb, rc, l, seq):
    t, d = x2.shape
    n_in = w_in_bf.shape[-1]
    tm = min(PROJ_TILE, seq)
    nb = mod3.shape[0] // norm1.shape[0]
    tok = lambda i: (i, 0)
    return pl.pallas_call(
        functools.partial(_inproj_kernel, d=d),
        out_shape=(
            jax.ShapeDtypeStruct((t, ATT_WIDTH), BF16),
            jax.ShapeDtypeStruct((t, 2 * KV_WIDTH), BF16),
            jax.ShapeDtypeStruct((t, 2 * KV_WIDTH), BF16),
            jax.ShapeDtypeStruct((t, 5 * HG_WIDTH), F32),
        ),
        grid=(t // tm,),
        in_specs=[
            pl.BlockSpec((tm, d), tok),
            pl.BlockSpec((None, 1, N_MOD * d), lambda i: (l * nb + (i * tm) // seq, 0, 0)),
            pl.BlockSpec((None, 1, d), lambda i: (l, 0, 0)),
            pl.BlockSpec((None, d, n_in), lambda i: (l, 0, 0)),
            pl.BlockSpec((tm, LANES), tok),
            pl.BlockSpec((tm, LANES), tok),
            pl.BlockSpec((tm, LANES), tok),
        ],
        out_specs=(
            pl.BlockSpec((tm, ATT_WIDTH), tok),
            pl.BlockSpec((tm, 2 * KV_WIDTH), tok),
            pl.BlockSpec((tm, 2 * KV_WIDTH), tok),
            pl.BlockSpec((tm, 5 * HG_WIDTH), tok),
        ),
        compiler_params=_params(("arbitrary",)),
        name="in_proj",
    )(x2, mod3, norm1, w_in_bf, ra, rb, rc)


def _attn_kernel(sink_ref, q_ref, kp_ref, kc_ref, kn_ref, vp_ref, vc_ref, vn_ref, gain_ref, o_ref, *, l, seq):
    n = pl.program_id(1)
    blk = ATT_BLOCK
    k2 = jnp.concatenate([kp_ref[...], kc_ref[...], kn_ref[...]], axis=0)
    v2 = jnp.concatenate([vp_ref[...], vc_ref[...], vn_ref[...]], axis=0)
    lane = lax.broadcasted_iota(I32, (3 * blk, LANES), 1)
    lo_half = lane < ATT_HEAD_DIM
    zero = jnp.zeros((3 * blk, LANES), BF16)
    ka, kb = k2[:, 0:LANES], k2[:, LANES:2 * LANES]
    va, vb = v2[:, 0:LANES], v2[:, LANES:2 * LANES]
    kz = [[jnp.where(lo_half, ka, zero), jnp.where(lo_half, zero, kb)],
          [jnp.where(lo_half, kb, zero), jnp.where(lo_half, zero, ka)]]
    vz = [[jnp.where(lo_half, va, zero), jnp.where(lo_half, zero, vb)],
          [jnp.where(lo_half, vb, zero), jnp.where(lo_half, zero, va)]]
    qpos = n * blk + lax.broadcasted_iota(I32, (blk, 3 * blk), 0)
    kpos = (n - 1) * blk + lax.broadcasted_iota(I32, (blk, 3 * blk), 1)
    valid = (jnp.abs(qpos - kpos) <= WINDOW) & (kpos >= 0) & (kpos < seq)
    group = ATT_HEADS // ATT_KV_HEADS
    outs = []
    for m in range(ATT_HEADS // 2):
        qp = q_ref[:, m * LANES:(m + 1) * LANES]
        acc = None
        for half in range(2):
            hh = 2 * m + half
            j = hh // group
            s = _dot_nt(qp, kz[j][half])
            s = jnp.where(valid, s, NEG_INF)
            sink = sink_ref[l, hh]
            mx = jnp.maximum(jnp.max(s, axis=-1, keepdims=True), sink)
            p = jnp.exp(s - mx)
            den = jnp.sum(p, axis=-1, keepdims=True) + jnp.exp(sink - mx)
            p = (p * (1.0 / den)).astype(BF16)
            pv = _dot(p, vz[j][half])
            acc = pv if acc is None else acc + pv
        outs.append(acc)
    o = jnp.concatenate(outs, axis=-1)
    ms = jnp.mean(o * o, axis=-1, keepdims=True)
    o_ref[...] = (o * lax.rsqrt(ms + EPS) * gain_ref[...]).astype(BF16)


def _attention(qa, ka2, va2, attn_sink, attn_norm3, l, batch, seq):
    t = qa.shape[0]
    blk = ATT_BLOCK
    nb = seq // blk
    cur = lambda b, n: (b * nb + n, 0)
    prev = lambda b, n: (b * nb + jnp.maximum(n - 1, 0), 0)
    nxt = lambda b, n: (b * nb + jnp.minimum(n + 1, nb - 1), 0)
    kvspec = lambda f: pl.BlockSpec((blk, 2 * KV_WIDTH), f)
    return pl.pallas_call(
        functools.partial(_attn_kernel, l=l, seq=seq),
        out_shape=jax.ShapeDtypeStruct((t, ATT_WIDTH), BF16),
        grid=(batch, nb),
        in_specs=[
            pl.BlockSpec(memory_space=pltpu.SMEM),
            pl.BlockSpec((blk, ATT_WIDTH), cur),
            kvspec(prev), kvspec(cur), kvspec(nxt),
            kvspec(prev), kvspec(cur), kvspec(nxt),
            pl.BlockSpec((None, 1, ATT_WIDTH), lambda b, n: (l, 0, 0)),
        ],
        out_specs=pl.BlockSpec((blk, ATT_WIDTH), cur),
        compiler_params=_params(("arbitrary", "arbitrary")),
        name="window_attn",
    )(attn_sink, qa, ka2, ka2, ka2, va2, va2, va2, attn_norm3)


def _hgrn_constants():
    c, nl = HG_CHUNK, HG_LEVELS
    r = np.arange(c)
    u = r[None, :]
    rr = r[:, None]
    mats, masks = [], []
    for lev in range(nl):
        m = 1 << lev
        parent = r // (2 * m)
        anchor = parent * 2 * m + m
        upper = r >= anchor
        aa = anchor[:, None]
        mats.append(np.where(upper[:, None], (u > aa) & (u <= rr), (u > rr) & (u <= aa)))
        masks.append((parent[:, None] == parent[None, :]) & upper[:, None] & (~upper)[None, :])
    masks.append(np.eye(c, dtype=bool))
    mats.append(u <= rr)
    mats.append(u > rr)
    mf = np.concatenate(mats, axis=0).astype(np.float32)
    mb = np.concatenate([mt[::-1, ::-1] for mt in mats], axis=0).astype(np.float32)
    kf = np.stack(masks).astype(np.float32)
    kb = np.stack([mk[::-1, ::-1] for mk in masks]).astype(np.float32)
    return mf, mb, kf, kb


def _hgrn_kernel(q_ref, ff_ref, fb_ref, i_ref, g_ref, lb_ref, gn_ref, mf_ref, mb_ref, kf_ref, kb_ref,
                 o_ref, of_scr, ob_scr, st_scr, *, seq):
    c, nl = HG_CHUNK, HG_LEVELS
    nc = seq // c
    lb = lb_ref[...]
    st_scr[...] = jnp.zeros(st_scr.shape, F32)

    unroll = HG_UNROLL if nc % HG_UNROLL == 0 else 1
    dirs = (
        dict(f_ref=ff_ref, lbrow=lb[0:1, :], m_ref=mf_ref, k_ref=kf_ref, o_scr=of_scr, last_row=c - 1, d=0),
        dict(f_ref=fb_ref, lbrow=lb[1:2, :], m_ref=mb_ref, k_ref=kb_ref, o_scr=ob_scr, last_row=0, d=1),
    )

    def body(i, carry):
        work = []
        for u in range(unroll):
            cf = i * unroll + u
            work.append((dirs[0], pl.ds(pl.multiple_of(cf * c, c), c)))
            work.append((dirs[1], pl.ds(pl.multiple_of((nc - 1 - cf) * c, c), c)))
        gates = []
        for dr, rows in work:
            lbf = jnp.maximum(dr["lbrow"], LB_FLOOR)
            oml = 1.0 - dr["lbrow"]
            f = dr["f_ref"][rows, :]
            e = jnp.exp(-jnp.abs(f))
            r = 1.0 / (1.0 + e)
            er = e * r
            pos = f >= 0.0
            logf = jnp.log(lbf + oml * jnp.where(pos, r, er))
            kk = oml * jnp.where(pos, er, r)
            qh = q_ref[rows, :]
            gates.append((logf, kk, qh * _sigmoid(qh), i_ref[rows, :].astype(BF16)))
        gall = [jnp.exp(_dot_exact_lhs(dr["m_ref"][...], g[0])) for (dr, _), g in zip(work, gates)]
        amat = [dr["k_ref"][nl] * _dot_nt(g[2].astype(BF16), g[1].astype(BF16)) for (dr, _), g in zip(work, gates)]
        for lev in range(nl):
            for j, ((dr, _), g) in enumerate(zip(work, gates)):
                gl = gall[j][lev * c:(lev + 1) * c, :]
                amat[j] = amat[j] + dr["k_ref"][lev] * _dot_nt((g[2] * gl).astype(BF16), (g[1] * gl).astype(BF16))
        intra = [_dot(amat[j].astype(BF16), g[3]) for j, g in enumerate(gates)]
        upd = [_dot_tn(g[3], (g[1] * gall[j][(nl + 1) * c:(nl + 2) * c, :]).astype(BF16)) for j, g in enumerate(gates)]
        st = [st_scr[0], st_scr[1]]
        for j, ((dr, rows), g) in enumerate(zip(work, gates)):
            eb = gall[j][nl * c:(nl + 1) * c, :]
            d = dr["d"]
            dr["o_scr"][rows, :] = _dot_nt((g[2] * eb).astype(BF16), st[d].astype(BF16)) + intra[j]
            st[d] = st[d] * eb[dr["last_row"]:dr["last_row"] + 1, :] + upd[j]
        st_scr[0] = st[0]
        st_scr[1] = st[1]
        return carry

    lax.fori_loop(0, nc // unroll, body, 0)

    ep = min(256, seq)
    gn = gn_ref[...]

    def epilogue(j, carry):
        rows = pl.ds(pl.multiple_of(j * ep, ep), ep)
        o = of_scr[rows, :] + ob_scr[rows, :]
        y = o * lax.rsqrt(jnp.mean(o * o, axis=-1, keepdims=True) + EPS) * gn
        g = g_ref[rows, :]
        o_ref[rows, :] = (y * (g * _sigmoid(g))).astype(BF16)
        return carry

    lax.fori_loop(0, seq // ep, epilogue, 0)


def _hgrn(hg, lower, hg_norm3, consts, l, batch, seq):
    t = hg.shape[0]
    mf, mb, kf, kb = consts
    hd = HG_HEAD_DIM

    def col(g):
        return pl.BlockSpec((seq, hd), lambda b, h: (b, g * HG_HEADS + h))

    full2 = lambda a: pl.BlockSpec(a.shape, lambda b, h: (0, 0))
    full3 = lambda a: pl.BlockSpec(a.shape, lambda b, h: (0, 0, 0))
    return pl.pallas_call(
        functools.partial(_hgrn_kernel, seq=seq),
        out_shape=jax.ShapeDtypeStruct((t, HG_WIDTH), BF16),
        grid=(batch, HG_HEADS),
        in_specs=[
            col(0), col(1), col(2), col(3), col(4),
            pl.BlockSpec((None, 2, hd), lambda b, h: (l, 0, h)),
            pl.BlockSpec((None, 1, hd), lambda b, h: (l, 0, 0)),
            full2(mf), full2(mb), full3(kf), full3(kb),
        ],
        out_specs=pl.BlockSpec((seq, hd), lambda b, h: (b, h)),
        scratch_shapes=[
            pltpu.VMEM((seq, hd), F32),
            pltpu.VMEM((seq, hd), F32),
            pltpu.VMEM((2, hd, hd), F32),
        ],
        compiler_params=_params(("arbitrary", "arbitrary")),
        name="hgrn2_scan",
    )(hg, hg, hg, hg, hg, lower, hg_norm3, mf, mb, kf, kb)


def _outproj_kernel(a_ref, o_ref, x_ref, mod_ref, n2_ref, w_ref, wr_ref, br_ref,
                    xo_ref, h2_ref, ti_ref, gt_ref, *, d):
    mod = mod_ref[...]
    y = _dot(a_ref[...], w_ref[0:ATT_WIDTH, :]) + _dot(o_ref[...], w_ref[ATT_WIDTH:ATT_WIDTH + HG_WIDTH, :])
    xn = x_ref[...] + mod[:, 2 * d:3 * d] * y
    xo_ref[...] = xn
    h2 = _rms_mod(xn, n2_ref[...], mod[:, 4 * d:5 * d], mod[:, 3 * d:4 * d])
    h2_ref[...] = _pack_bf16_pairs(h2)
    lg = _dot_f32_nt(wr_ref[...], h2) + br_ref[...]
    eidx = lax.broadcasted_iota(I32, lg.shape, 0)
    vals, idxs = [], []
    for _ in range(TOP_K):
        mx = jnp.max(lg, axis=0, keepdims=True)
        sel = jnp.min(jnp.where(lg == mx, eidx, N_EXPERTS), axis=0, keepdims=True)
        vals.append(mx)
        idxs.append(sel)
        lg = jnp.where(eidx == sel, -jnp.inf, lg)
    ex = [jnp.exp(v - vals[0]) for v in vals]
    den = ex[0]
    for e in ex[1:]:
        den = den + e
    inv = 1.0 / den
    ti_ref[...] = jnp.concatenate(idxs, axis=0)
    gt_ref[...] = jnp.concatenate([e * inv for e in ex], axis=0)


def _outproj(a, o, x2, mod3, norm2, w_out_bf, w_router_t, b_router3, l, seq):
    t, d = x2.shape
    tm = min(PROJ_TILE, seq)
    nb = mod3.shape[0] // norm2.shape[0]
    tok = lambda i: (i, 0)
    lane_tok = lambda i: (0, i)
    return pl.pallas_call(
        functools.partial(_outproj_kernel, d=d),
        out_shape=(
            jax.ShapeDtypeStruct((t, d), F32),
            jax.ShapeDtypeStruct((t, d // 2), U32),
            jax.ShapeDtypeStruct((TOP_K, t), I32),
            jax.ShapeDtypeStruct((TOP_K, t), F32),
        ),
        grid=(t // tm,),
        in_specs=[
            pl.BlockSpec((tm, ATT_WIDTH), tok),
            pl.BlockSpec((tm, HG_WIDTH), tok),
            pl.BlockSpec((tm, d), tok),
            pl.BlockSpec((None, 1, N_MOD * d), lambda i: (l * nb + (i * tm) // seq, 0, 0)),
            pl.BlockSpec((None, 1, d), lambda i: (l, 0, 0)),
            pl.BlockSpec((None, ATT_WIDTH + HG_WIDTH, d), lambda i: (l, 0, 0)),
            pl.BlockSpec((None, N_EXPERTS, d), lambda i: (l, 0, 0)),
            pl.BlockSpec((None, N_EXPERTS, 1), lambda i: (l, 0, 0)),
        ],
        out_specs=(
            pl.BlockSpec((tm, d), tok),
            pl.BlockSpec((tm, d // 2), tok),
            pl.BlockSpec((TOP_K, tm), lane_tok),
            pl.BlockSpec((TOP_K, tm), lane_tok),
        ),
        compiler_params=_params(("arbitrary",)),
        name="out_proj_router",
    )(a, o, x2, mod3, norm2, w_out_bf, w_router_t, b_router3)


def _rank_kernel(ti_ref, tri_ref, rank_ref, cnt_ref, carry_scr):
    @pl.when(pl.program_id(0) == 0)
    def _():
        carry_scr[...] = jnp.zeros(carry_scr.shape, F32)

    ti = ti_ref[...]
    tl = ti.shape[1]
    eidx = lax.broadcasted_iota(I32, (N_EXPERTS, tl), 0)
    carry = carry_scr[...]
    rows = []
    for k in range(TOP_K):
        oh = eidx == ti[k:k + 1, :]
        ohf = jnp.where(oh, 1.0, 0.0)
        pre = _dot(ohf.astype(BF16), tri_ref[...])
        rows.append(jnp.sum(jnp.where(oh, carry + pre, 0.0), axis=0, keepdims=True))
        carry = carry + jnp.sum(ohf, axis=1, keepdims=True)
    carry_scr[...] = carry
    rank_ref[...] = jnp.concatenate(rows, axis=0).astype(I32)
    cnt_ref[...] = jnp.broadcast_to(carry, cnt_ref.shape)


def _ranks(topi):
    k, t = topi.shape
    tl = min(RANK_TILE, t)
    tri = np.triu(np.ones((tl, tl), np.float32), 1)
    return pl.pallas_call(
        _rank_kernel,
        out_shape=(jax.ShapeDtypeStruct((k, t), I32), jax.ShapeDtypeStruct((N_EXPERTS, LANES), F32)),
        grid=(t // tl,),
        in_specs=[pl.BlockSpec((k, tl), lambda i: (0, i)), pl.BlockSpec((tl, tl), lambda i: (0, 0))],
        out_specs=(pl.BlockSpec((k, tl), lambda i: (0, i)), pl.BlockSpec((N_EXPERTS, LANES), lambda i: (0, 0))),
        scratch_shapes=[pltpu.VMEM((N_EXPERTS, 1), F32)],
        compiler_params=_params(("arbitrary",)),
        name="route_rank",
    )(topi, jnp.asarray(tri, BF16))


def _dest_kernel(cnt_ref, ltri_ref, ti_ref, rank_ref, dest_ref, bexp_ref, nused_ref):
    cnt = cnt_ref[...]
    nblk = jnp.floor((cnt + (MOE_ROWS - 1)) * (1.0 / MOE_ROWS))
    pstart_b = _dot(ltri_ref[...], nblk.astype(BF16))
    pend_b = pstart_b + nblk
    pstart = (pstart_b[:, 0:1] * MOE_ROWS).astype(I32)
    ti = ti_ref[...]
    tl = ti.shape[1]
    eidx = lax.broadcasted_iota(I32, (N_EXPERTS, tl), 0)
    rows = []
    for k in range(TOP_K):
        oh = eidx == ti[k:k + 1, :]
        rows.append(jnp.sum(jnp.where(oh, pstart, 0), axis=0, keepdims=True))
    dest_ref[...] = jnp.concatenate(rows, axis=0) + rank_ref[...]
    nb = bexp_ref.shape[1]
    bi = lax.broadcasted_iota(I32, (N_EXPERTS, nb), 1).astype(F32)
    be = jnp.sum(jnp.where(pend_b[:, 0:1] <= bi, 1, 0), axis=0, keepdims=True)
    bexp_ref[...] = jnp.minimum(be, N_EXPERTS - 1).astype(I32)
    nused_ref[...] = pend_b[N_EXPERTS - 1:N_EXPERTS, :].astype(I32)


def _destinations(counts, topi, rank, nblocks):
    k, t = topi.shape
    tl = min(2048, t)
    nbp = -(-nblocks // LANES) * LANES
    ltri = np.tril(np.ones((N_EXPERTS, N_EXPERTS), np.float32), -1)
    return pl.pallas_call(
        _dest_kernel,
        out_shape=(
            jax.ShapeDtypeStruct((k, t), I32),
            jax.ShapeDtypeStruct((1, nbp), I32),
            jax.ShapeDtypeStruct((1, LANES), I32),
        ),
        grid=(t // tl,),
        in_specs=[
            pl.BlockSpec((N_EXPERTS, LANES), lambda i: (0, 0)),
            pl.BlockSpec((N_EXPERTS, N_EXPERTS), lambda i: (0, 0)),
            pl.BlockSpec((k, tl), lambda i: (0, i)),
            pl.BlockSpec((k, tl), lambda i: (0, i)),
        ],
        out_specs=(
            pl.BlockSpec((k, tl), lambda i: (0, i)),
            pl.BlockSpec((1, nbp), lambda i: (0, 0)),
            pl.BlockSpec((1, LANES), lambda i: (0, 0)),
        ),
        compiler_params=_params(("arbitrary",)),
        name="route_dest",
    )(counts, jnp.asarray(ltri, BF16), topi, rank)


def _inverse_kernel(bexp_ref, nused_ref, dest_hbm, inv_hbm, dsm0, dsm1, inv_sm, isem, osem, *, t, tm):
    i = pl.program_id(0)
    nt = pl.num_programs(0)
    nblk = inv_sm.shape[0] // MOE_ROWS
    nused = nused_ref[0]
    dsm = (dsm0, dsm1)

    def idx_copy(j, p):
        return pltpu.make_async_copy(dest_hbm.at[j], dsm[p], isem.at[p])

    @pl.when(i == 0)
    def _():
        idx_copy(0, 0).start()

        def init_block(b, carry):
            nxt = jnp.minimum(b + 1, nblk - 1)
            partial = jnp.logical_or(b >= nused - 1, bexp_ref[nxt] != bexp_ref[b])

            @pl.when(partial)
            def _():
                row0 = b * MOE_ROWS
                spare0 = TOP_K * t + (b % 2) * MOE_ROWS

                def init_row(r, c2):
                    inv_sm[row0 + r] = spare0 + r
                    return c2

                lax.fori_loop(0, MOE_ROWS, init_row, 0, unroll=8)

            return carry

        lax.fori_loop(0, nblk, init_block, 0)

    tok0 = i * tm
    for p in range(2):
        @pl.when(i % 2 == p)
        def _():
            @pl.when(i + 1 < nt)
            def _():
                idx_copy(i + 1, 1 - p).start()

            idx_copy(i, p).wait()

            def body(r, carry):
                for k in range(TOP_K):
                    inv_sm[dsm[p][k * tm + r]] = k * t + tok0 + r
                return carry

            lax.fori_loop(0, tm, body, 0, unroll=8)

    @pl.when(i == nt - 1)
    def _():
        out = pltpu.make_async_copy(inv_sm, inv_hbm, osem)
        out.start()
        out.wait()


def _inverse_map(bexp, nused, dest2, nblocks, t):
    nt, ktm = dest2.shape
    grid_spec = pltpu.PrefetchScalarGridSpec(
        num_scalar_prefetch=2,
        grid=(nt,),
        in_specs=[pl.BlockSpec(memory_space=pl.ANY)],
        out_specs=pl.BlockSpec(memory_space=pl.ANY),
        scratch_shapes=[
            pltpu.SMEM((ktm,), I32),
            pltpu.SMEM((ktm,), I32),
            pltpu.SMEM((nblocks * MOE_ROWS,), I32),
            pltpu.SemaphoreType.DMA((2,)),
            pltpu.SemaphoreType.DMA(()),
        ],
    )
    inv = pl.pallas_call(
        functools.partial(_inverse_kernel, t=t, tm=ktm // TOP_K),
        out_shape=jax.ShapeDtypeStruct((nblocks * MOE_ROWS,), I32),
        grid_spec=grid_spec,
        compiler_params=_params(("arbitrary",)),
        name="route_inverse",
    )(bexp, nused, dest2)
    return inv.reshape(nblocks, MOE_ROWS)


def _pack_bf16_pairs(x):
    n = x.shape[1] // 2
    lo = lax.bitcast_convert_type(x[:, :n].astype(BF16).astype(F32), U32)
    hi = lax.bitcast_convert_type(x[:, n:].astype(BF16).astype(F32), U32)
    return hi | (lo >> 16)


def _unpack_bf16_pairs(w):
    lo = lax.bitcast_convert_type(w << 16, F32)
    hi = lax.bitcast_convert_type(w & jnp.uint32(0xFFFF0000), F32)
    return lo, hi


def _expert_kernel(bexp_ref, nused_ref, inv_hbm, h2_hbm, wgu_ref, bgu_ref, wd_ref, bd_ref, y4_hbm,
                   ig0, ig1, is0, is1, xbuf, ybuf, wgu_bf, wd_bf, gisem, sisem, gsem, ssem, *, dff, t):
    i = pl.program_id(0)
    nblk = pl.num_programs(0)
    nused = nused_ref[0]
    rows = xbuf.shape[1]
    half = xbuf.shape[2]
    used = i < nused
    fresh = jnp.logical_or(i == 0, bexp_ref[i] != bexp_ref[jnp.maximum(i - 1, 0)])
    gidx, sidx = (ig0, ig1), (is0, is1)

    def gidx_copy(j, p):
        return pltpu.make_async_copy(inv_hbm.at[j], gidx[p], gisem.at[p])

    def sidx_copy(j, p):
        return pltpu.make_async_copy(inv_hbm.at[j], sidx[p], sisem.at[p])

    def gather_rows(p):
        def body(rr, carry):
            for u in range(2):
                r = rr * 2 + u
                slot_row = gidx[p][r]
                tok = slot_row & (t - 1) if t & (t - 1) == 0 else lax.rem(slot_row, t)
                pltpu.make_async_copy(h2_hbm.at[pl.ds(tok, 1), :], xbuf.at[p, pl.ds(r, 1), :],
                                      gsem.at[p]).start(priority=u)
            return carry

        for rr in range(rows // 2):
            body(rr, 0)

    def gather_wait(p):
        pltpu.make_async_copy(h2_hbm.at[pl.ds(0, rows), :], xbuf.at[p], gsem.at[p]).wait()

    def scatter_rows(p):
        def body(rr, carry):
            for u in range(2):
                r = rr * 2 + u
                pltpu.make_async_copy(ybuf.at[p, pl.ds(r, 1), :], y4_hbm.at[pl.ds(sidx[p][r], 1), :],
                                      ssem.at[p]).start(priority=u)
            return carry

        for rr in range(rows // 2):
            body(rr, 0)

    def scatter_wait(p):
        pltpu.make_async_copy(ybuf.at[p], y4_hbm.at[pl.ds(0, rows), :], ssem.at[p]).wait()

    @pl.when(i == 0)
    def _():
        ybuf[...] = jnp.zeros(ybuf.shape, U32)
        for s in range(2):
            spare = pltpu.make_async_copy(ybuf.at[s], y4_hbm.at[pl.ds(TOP_K * t + s * rows, rows), :], ssem.at[s])
            spare.start()
            spare.wait()
        gidx_copy(0, 0).start()
        sidx_copy(0, 0).start()

        @pl.when(nused > 1)
        def _():
            gidx_copy(1, 1).start()

        gidx_copy(0, 0).wait()
        gather_rows(0)

    @pl.when(jnp.logical_and(used, fresh))
    def _():
        wgu_bf[...] = wgu_ref[...].astype(BF16)
        wd_bf[...] = wd_ref[...].astype(BF16)

    for p in range(2):
        q = 1 - p

        @pl.when(i % 2 == p)
        def _():
            @pl.when(i + 1 < nused)
            def _():
                gidx_copy(i + 1, q).wait()
                gather_rows(q)
                sidx_copy(i + 1, q).start()

            @pl.when(i + 2 < nused)
            def _():
                gidx_copy(i + 2, p).start()

            @pl.when(jnp.logical_and(i >= 2, i - 2 < nused))
            def _():
                scatter_wait(p)

            @pl.when(used)
            def _():
                gather_wait(p)
                lo, hi = _unpack_bf16_pairs(xbuf[p])
                hu = (_dot(lo.astype(BF16), wgu_bf[0:half, :]) + _dot(hi.astype(BF16), wgu_bf[half:2 * half, :])
                      + bgu_ref[...])
                glu = jnp.minimum(hu[:, 0:dff], SWIGLU_LIMIT)
                lin = jnp.clip(hu[:, dff:2 * dff], -SWIGLU_LIMIT, SWIGLU_LIMIT)
                act = glu * _sigmoid(SWIGLU_ALPHA * glu) * (lin + 1.0)
                ybuf[p] = _pack_bf16_pairs(_dot(act.astype(BF16), wd_bf[...]) + bd_ref[...])
                sidx_copy(i, p).wait()
                scatter_rows(p)

            @pl.when(i == nblk - 1)
            def _():
                @pl.when(jnp.logical_and(i >= 1, i - 1 < nused))
                def _():
                    scatter_wait(q)

                @pl.when(used)
                def _():
                    scatter_wait(p)


def _experts(bexp, nused, inv, h2p, w_gu, b_gu4, w_down, b_down4, l, t):
    nblocks = inv.shape[0]
    half = h2p.shape[1]
    d = 2 * half
    dff = w_down.shape[2]

    wsel = lambda i, be, nu: (l, be[jnp.minimum(i, nu[0] - 1)], 0, 0)
    grid_spec = pltpu.PrefetchScalarGridSpec(
        num_scalar_prefetch=2,
        grid=(nblocks,),
        in_specs=[
            pl.BlockSpec(memory_space=pl.ANY),
            pl.BlockSpec(memory_space=pl.ANY),
            pl.BlockSpec((None, None, d, 2 * dff), wsel),
            pl.BlockSpec((None, None, 1, 2 * dff), wsel),
            pl.BlockSpec((None, None, dff, d), wsel),
            pl.BlockSpec((None, None, 1, d), wsel),
        ],
        out_specs=pl.BlockSpec(memory_space=pl.ANY),
        scratch_shapes=[pltpu.SMEM((MOE_ROWS,), I32)] * 4 + [
            pltpu.VMEM((2, MOE_ROWS, half), U32),
            pltpu.VMEM((2, MOE_ROWS, half), U32),
            pltpu.VMEM((d, 2 * dff), BF16),
            pltpu.VMEM((dff, d), BF16),
        ] + [pltpu.SemaphoreType.DMA((2,))] * 4,
    )
    return pl.pallas_call(
        functools.partial(_expert_kernel, dff=dff, t=t),
        out_shape=jax.ShapeDtypeStruct((TOP_K * t + 2 * MOE_ROWS, half), U32),
        grid_spec=grid_spec,
        compiler_params=_params(("arbitrary",)),
        name="moe_experts",
    )(bexp, nused, inv, h2p, w_gu, b_gu4, w_down, b_down4)


def _combine_kernel(*refs, d):
    y_refs = refs[:TOP_K]
    gt_ref, x_ref, mod_ref, xo_ref = refs[TOP_K:]
    half = d // 2
    gt = gt_ref[...]
    acc_lo = acc_hi = None
    for k in range(TOP_K):
        lo, hi = _unpack_bf16_pairs(y_refs[k][...])
        g = gt[:, k:k + 1]
        acc_lo = g * lo if acc_lo is None else acc_lo + g * lo
        acc_hi = g * hi if acc_hi is None else acc_hi + g * hi
    xo_ref[:, 0:half] = x_ref[:, 0:half] + mod_ref[:, 5 * d:5 * d + half] * acc_lo
    xo_ref[:, half:d] = x_ref[:, half:d] + mod_ref[:, 5 * d + half:6 * d] * acc_hi


def _combine(y4p, gates_t, x2, mod3, l, seq, depth):
    t, d = x2.shape
    tm = min(PROJ_TILE, seq)
    nb = mod3.shape[0] // depth
    nt = t // tm

    def slot(k):
        return pl.BlockSpec((tm, d // 2), lambda i: (k * nt + i, 0))

    return pl.pallas_call(
        functools.partial(_combine_kernel, d=d),
        out_shape=jax.ShapeDtypeStruct((t, d), F32),
        grid=(nt,),
        in_specs=[slot(k) for k in range(TOP_K)] + [
            pl.BlockSpec((tm, TOP_K), lambda i: (i, 0)),
            pl.BlockSpec((tm, d), lambda i: (i, 0)),
            pl.BlockSpec((None, 1, N_MOD * d), lambda i: (l * nb + (i * tm) // seq, 0, 0)),
        ],
        out_specs=pl.BlockSpec((tm, d), lambda i: (i, 0)),
        compiler_params=_params(("arbitrary",)),
        name="moe_combine",
    )(*([y4p] * TOP_K), gates_t, x2, mod3)


def _final_kernel(x_ref, g_ref, o_ref):
    x = x_ref[...]
    o_ref[...] = x * lax.rsqrt(jnp.mean(x * x, axis=-1, keepdims=True) + EPS) * g_ref[...]


def _final_norm(x2, gain):
    t, d = x2.shape
    tm = min(PROJ_TILE, t)
    return pl.pallas_call(
        _final_kernel,
        out_shape=jax.ShapeDtypeStruct((t, d), F32),
        grid=(t // tm,),
        in_specs=[pl.BlockSpec((tm, d), lambda i: (i, 0)), pl.BlockSpec((1, d), lambda i: (0, 0))],
        out_specs=pl.BlockSpec((tm, d), lambda i: (i, 0)),
        compiler_params=_params(("arbitrary",)),
        name="final_norm",
    )(x2, gain.reshape(1, d))


def _mixer_layer(x2, mod3, tables, lower, consts, p, l, batch, seq):
    ra, rb, rc = tables
    qa, ka2, va2, hg = _inproj(x2, mod3, p["norm1"], p["w_in"], ra, rb, rc, l, seq)
    a = _attention(qa, ka2, va2, p["attn_sink"], p["attn_norm"], l, batch, seq)
    o = _hgrn(hg, lower, p["hg_norm"], consts, l, batch, seq)
    return a, o


def _moe_layer(a, o, x2, mod3, p, l, seq, depth):
    t, d = x2.shape
    xn, h2p, topi, gates = _outproj(a, o, x2, mod3, p["norm2"], p["w_out"], p["w_router_t"], p["b_router"], l, seq)
    rank, counts = _ranks(topi)
    nblocks = (t * TOP_K + N_EXPERTS * MOE_ROWS) // MOE_ROWS
    dest, bexp, nused = _destinations(counts, topi, rank, nblocks)
    tm = min(TOK_TILE, t)
    dest2 = dest.reshape(TOP_K, t // tm, tm).transpose(1, 0, 2).reshape(t // tm, TOP_K * tm)
    bexp, nused = bexp.reshape(-1), nused.reshape(-1)
    inv = _inverse_map(bexp, nused, dest2, nblocks, t)
    y4p = _experts(bexp, nused, inv, h2p, p["w_gu"], p["b_gu"], p["w_down"], p["b_down"], l, t)
    return _combine(y4p, gates.T, xn, mod3, l, seq, depth)


def kernel(x, c, positions, w_ada, b_ada, norm1, w_in, attn_sink, attn_norm, hg_lb_logits, hg_norm, w_out, norm2,
           w_router, b_router, w_gu, b_gu, w_down, b_down, final_norm):
    batch, seq, d = x.shape
    depth = w_ada.shape[0]
    t = batch * seq
    p = {
        "norm1": norm1.reshape(depth, 1, d),
        "w_in": w_in.astype(BF16),
        "attn_sink": attn_sink.astype(F32),
        "attn_norm": attn_norm.reshape(depth, 1, ATT_WIDTH),
        "hg_norm": hg_norm.reshape(depth, 1, HG_HEAD_DIM),
        "w_out": w_out.astype(BF16),
        "norm2": norm2.reshape(depth, 1, d),
        "w_router_t": jnp.swapaxes(w_router, 1, 2),
        "b_router": b_router.reshape(depth, N_EXPERTS, 1),
        "w_gu": w_gu,
        "b_gu": b_gu.reshape(depth, N_EXPERTS, 1, b_gu.shape[-1]),
        "w_down": w_down,
        "b_down": b_down.reshape(depth, N_EXPERTS, 1, d),
    }
    mod3 = _ada_all(c, w_ada, b_ada).reshape(depth * batch, 1, N_MOD * d)
    lower = _lower_bounds(hg_lb_logits)
    tables = _rope_tables(positions)
    consts = _hgrn_constants()
    consts = (jnp.asarray(consts[0], BF16), jnp.asarray(consts[1], BF16), jnp.asarray(consts[2]), jnp.asarray(consts[3]))
    x2 = x.reshape(t, d)
    for l in range(depth):
        a, o = _mixer_layer(x2, mod3, tables, lower, consts, p, l, batch, seq)
        x2 = _moe_layer(a, o, x2, mod3, p, l, seq, depth)
    return _final_norm(x2, final_norm).reshape(batch, seq, d)
```

```python
import functools

import numpy as np
import jax
import jax.numpy as jnp
from jax import lax
from jax.experimental import pallas as pl
from jax.experimental.pallas import tpu as pltpu
from jax.experimental.pallas import tpu_sc as plsc

F32 = jnp.float32
BF16 = jnp.bfloat16
I32 = jnp.int32
U32 = jnp.uint32

ATT_HEADS = 8
ATT_KV_HEADS = 2
ATT_HEAD_DIM = 64
ATT_WIDTH = ATT_HEADS * ATT_HEAD_DIM
KV_WIDTH = ATT_KV_HEADS * ATT_HEAD_DIM
WINDOW = 128
ATT_BLOCK = 128
ROPE_THETA = 500000.0
ROPE_DIM = ATT_HEAD_DIM // 4
HG_HEADS = 4
HG_HEAD_DIM = 128
HG_WIDTH = HG_HEADS * HG_HEAD_DIM
N_EXPERTS = 32
TOP_K = 4
SWIGLU_ALPHA = 1.702
SWIGLU_LIMIT = 7.0
N_MOD = 6
EPS = 1e-6
NEG_INF = -1e30
LB_FLOOR = 1e-30

LANES = 128
HG_CHUNK = 64
HG_LEVELS = 6
HG_UNROLL = 8
MOE_SHIFT = 8
MOE_ROWS = 1 << MOE_SHIFT
TOK_TILE = 256
PROJ_TILE = 512
RANK_TILE = 512
VMEM_LIMIT = 56 * 1024 * 1024


def _dot(a, b):
    return jnp.dot(a, b, preferred_element_type=F32)


def _dot_nt(a, b):
    return lax.dot_general(a, b, (((1,), (1,)), ((), ())), preferred_element_type=F32)


def _dot_tn(a, b):
    return lax.dot_general(a, b, (((0,), (0,)), ((), ())), preferred_element_type=F32)


def _split3(x):
    hi = x.astype(BF16)
    r1 = x - hi.astype(F32)
    mid = r1.astype(BF16)
    lo = (r1 - mid.astype(F32)).astype(BF16)
    return hi, mid, lo


def _dot_exact_lhs(m_bf16, x):
    hi, mid, lo = _split3(x)
    return _dot(m_bf16, hi) + _dot(m_bf16, mid) + _dot(m_bf16, lo)


def _dot_f32_nt(a, b):
    ah, am, al = _split3(a)
    bh, bm, bl = _split3(b)
    return (_dot_nt(ah, bh) + _dot_nt(ah, bm) + _dot_nt(am, bh)
            + _dot_nt(ah, bl) + _dot_nt(am, bm) + _dot_nt(al, bh))


def _dot_f32(a, b):
    ah, am, al = _split3(a)
    bh, bm, bl = _split3(b)
    return (_dot(ah, bh) + _dot(ah, bm) + _dot(am, bh)
            + _dot(ah, bl) + _dot(am, bm) + _dot(al, bh))


def _sigmoid(x):
    return 1.0 / (1.0 + jnp.exp(-x))


def _params(sem=None):
    return pltpu.CompilerParams(dimension_semantics=sem, vmem_limit_bytes=VMEM_LIMIT)


def _ada_kernel(c_ref, w_ref, b_ref, o_ref):
    c = c_ref[...]
    cond = c * _sigmoid(c)
    o_ref[...] = _dot_f32(cond, w_ref[...]) + b_ref[...]


def _ada_all(c, w_ada, b_ada):
    depth, d, n = w_ada.shape
    b = c.shape[0]
    nt = n // d
    return pl.pallas_call(
        _ada_kernel,
        out_shape=jax.ShapeDtypeStruct((depth, b, n), F32),
        grid=(depth, nt),
        in_specs=[
            pl.BlockSpec((b, d), lambda l, j: (0, 0)),
            pl.BlockSpec((None, d, d), lambda l, j: (l, 0, j)),
            pl.BlockSpec((None, 1, d), lambda l, j: (l, 0, j)),
        ],
        out_specs=pl.BlockSpec((None, b, d), lambda l, j: (l, 0, j)),
        compiler_params=_params(("arbitrary", "arbitrary")),
        name="ada_mod",
    )(c, w_ada, b_ada.reshape(depth, 1, n))


def _lb_kernel(x_ref, o_ref):
    depth = x_ref.shape[0]
    xs = [x_ref[l] for l in range(depth)]
    m = xs[0]
    for l in range(1, depth):
        m = jnp.maximum(m, xs[l])
    es = [jnp.exp(v - m) for v in xs]
    den = es[0]
    for l in range(1, depth):
        den = den + es[l]
    ps = [e / den for e in es]
    run = ps[0]
    o_ref[0] = run - ps[0]
    for l in range(1, depth):
        run = run + ps[l]
        o_ref[l] = run - ps[0]


def _lower_bounds(hg_lb_logits):
    return pl.pallas_call(
        _lb_kernel,
        out_shape=jax.ShapeDtypeStruct(hg_lb_logits.shape, F32),
        name="hg_lower_bounds",
    )(hg_lb_logits.astype(F32))


def _rope_kernel(pos_ref, invf_ref, a_ref, b_ref, c_ref):
    pos = pos_ref[...].astype(F32)
    ang = pos * invf_ref[...]
    cs = jnp.cos(ang)
    sn = jnp.sin(ang)
    lane = lax.broadcasted_iota(I32, ang.shape, 1) & (ATT_HEAD_DIM - 1)
    half = ROPE_DIM // 2
    first = lane < half
    second = (lane >= half) & (lane < ROPE_DIM)
    a_ref[...] = jnp.where(first | second, cs, 1.0)
    b_ref[...] = jnp.where(first, -sn, 0.0)
    c_ref[...] = jnp.where(second, sn, 0.0)


def _rope_tables(positions):
    t = positions.size
    half = ROPE_DIM // 2
    inv = (np.float32(ROPE_THETA) ** (-(np.arange(half, dtype=np.float32) * np.float32(2.0) / np.float32(ROPE_DIM)))).astype(np.float32)
    lane = np.arange(LANES) % ATT_HEAD_DIM
    pat = np.where(lane < ROPE_DIM, inv[lane % half], 0.0).astype(np.float32).reshape(1, LANES)
    tm = min(t, 2048)
    shp = jax.ShapeDtypeStruct((t, LANES), F32)
    spec = pl.BlockSpec((tm, LANES), lambda i: (i, 0))
    return pl.pallas_call(
        _rope_kernel,
        out_shape=(shp, shp, shp),
        grid=(t // tm,),
        in_specs=[pl.BlockSpec((tm, 1), lambda i: (i, 0)), pl.BlockSpec((1, LANES), lambda i: (0, 0))],
        out_specs=(spec, spec, spec),
        compiler_params=_params(("arbitrary",)),
        name="rope_tables",
    )(positions.reshape(t, 1).astype(I32), jnp.asarray(pat))


def _rms_mod(x, gain, scale, shift):
    ms = jnp.mean(x * x, axis=-1, keepdims=True)
    return (x * lax.rsqrt(ms + EPS) * gain) * (1.0 + scale) + shift


def _rope_apply(x, a, b, c):
    half = ROPE_DIM // 2
    return x * a + pltpu.roll(x, LANES - half, 1) * b + pltpu.roll(x, half, 1) * c


def _inproj_kernel(x_ref, mod_ref, n1_ref, w_ref, ra_ref, rb_ref, rc_ref, qa_ref, ka_ref, va_ref, hg_ref, *, d):
    mod = mod_ref[...]
    h = _rms_mod(x_ref[...], n1_ref[...], mod[:, d:2 * d], mod[:, 0:d]).astype(BF16)
    a, b, c = ra_ref[...], rb_ref[...], rc_ref[...]
    kvw = ATT_WIDTH + 2 * KV_WIDTH
    pa = _dot(h, w_ref[:, 0:kvw])
    scale = ATT_HEAD_DIM ** -0.5
    for g in range(ATT_WIDTH // LANES):
        qg = _rope_apply(pa[:, g * LANES:(g + 1) * LANES], a, b, c)
        qa_ref[:, g * LANES:(g + 1) * LANES] = (qg * scale).astype(BF16)
    k = _rope_apply(pa[:, ATT_WIDTH:ATT_WIDTH + KV_WIDTH], a, b, c)
    v = pa[:, ATT_WIDTH + KV_WIDTH:kvw]
    ka_ref[:, 0:LANES] = k.astype(BF16)
    ka_ref[:, LANES:2 * LANES] = pltpu.roll(k, ATT_HEAD_DIM, 1).astype(BF16)
    va_ref[:, 0:LANES] = v.astype(BF16)
    va_ref[:, LANES:2 * LANES] = pltpu.roll(v, ATT_HEAD_DIM, 1).astype(BF16)
    for g in range(5):
        lo = kvw + g * HG_WIDTH
        hg_ref[:, g * HG_WIDTH:(g + 1) * HG_WIDTH] = _dot(h, w_ref[:, lo:lo + HG_WIDTH])


def _inproj(x2, mod3, norm1, w_in_bf, ra, rb, rc, l, seq):
    t, d = x2.shape
    n_in = w_in_bf.shape[-1]
    tm = min(PROJ_TILE, seq)
    nb = mod3.shape[0] // norm1.shape[0]
    tok = lambda i: (i, 0)
    return pl.pallas_call(
        functools.partial(_inproj_kernel, d=d),
        out_shape=(
            jax.ShapeDtypeStruct((t, ATT_WIDTH), BF16),
            jax.ShapeDtypeStruct((t, 2 * KV_WIDTH), BF16),
            jax.ShapeDtypeStruct((t, 2 * KV_WIDTH), BF16),
            jax.ShapeDtypeStruct((t, 5 * HG_WIDTH), F32),
        ),
        grid=(t // tm,),
        in_specs=[
            pl.BlockSpec((tm, d), tok),
            pl.BlockSpec((None, 1, N_MOD * d), lambda i: (l * nb + (i * tm) // seq, 0, 0)),
            pl.BlockSpec((None, 1, d), lambda i: (l, 0, 0)),
            pl.BlockSpec((None, d, n_in), lambda i: (l, 0, 0)),
            pl.BlockSpec((tm, LANES), tok),
            pl.BlockSpec((tm, LANES), tok),
            pl.BlockSpec((tm, LANES), tok),
        ],
        out_specs=(
            pl.BlockSpec((tm, ATT_WIDTH), tok),
            pl.BlockSpec((tm, 2 * KV_WIDTH), tok),
            pl.BlockSpec((tm, 2 * KV_WIDTH), tok),
            pl.BlockSpec((tm, 5 * HG_WIDTH), tok),
        ),
        compiler_params=_params(("arbitrary",)),
        name="in_proj",
    )(x2, mod3, norm1, w_in_bf, ra, rb, rc)


def _attn_kernel(sink_ref, q_ref, kp_ref, kc_ref, kn_ref, vp_ref, vc_ref, vn_ref, gain_ref, o_ref, *, l, seq):
    n = pl.program_id(1)
    blk = ATT_BLOCK
    k2 = jnp.concatenate([kp_ref[...], kc_ref[...], kn_ref[...]], axis=0)
    v2 = jnp.concatenate([vp_ref[...], vc_ref[...], vn_ref[...]], axis=0)
    lane = lax.broadcasted_iota(I32, (3 * blk, LANES), 1)
    lo_half = lane < ATT_HEAD_DIM
    zero = jnp.zeros((3 * blk, LANES), BF16)
    ka, kb = k2[:, 0:LANES], k2[:, LANES:2 * LANES]
    va, vb = v2[:, 0:LANES], v2[:, LANES:2 * LANES]
    kz = [[jnp.where(lo_half, ka, zero), jnp.where(lo_half, zero, kb)],
          [jnp.where(lo_half, kb, zero), jnp.where(lo_half, zero, ka)]]
    vz = [[jnp.where(lo_half, va, zero), jnp.where(lo_half, zero, vb)],
          [jnp.where(lo_half, vb, zero), jnp.where(lo_half, zero, va)]]
    qpos = n * blk + lax.broadcasted_iota(I32, (blk, 3 * blk), 0)
    kpos = (n - 1) * blk + lax.broadcasted_iota(I32, (blk, 3 * blk), 1)
    valid = (jnp.abs(qpos - kpos) <= WINDOW) & (kpos >= 0) & (kpos < seq)
    group = ATT_HEADS // ATT_KV_HEADS
    outs = []
    for m in range(ATT_HEADS // 2):
        qp = q_ref[:, m * LANES:(m + 1) * LANES]
        acc = None
        for half in range(2):
            hh = 2 * m + half
            j = hh // group
            s = _dot_nt(qp, kz[j][half])
            s = jnp.where(valid, s, NEG_INF)
            sink = sink_ref[l, hh]
            mx = jnp.maximum(jnp.max(s, axis=-1, keepdims=True), sink)
            p = jnp.exp(s - mx)
            den = jnp.sum(p, axis=-1, keepdims=True) + jnp.exp(sink - mx)
            p = (p * (1.0 / den)).astype(BF16)
            pv = _dot(p, vz[j][half])
            acc = pv if acc is None else acc + pv
        outs.append(acc)
    o = jnp.concatenate(outs, axis=-1)
    ms = jnp.mean(o * o, axis=-1, keepdims=True)
    o_ref[...] = (o * lax.rsqrt(ms + EPS) * gain_ref[...]).astype(BF16)


def _attention(qa, ka2, va2, attn_sink, attn_norm3, l, batch, seq):
    t = qa.shape[0]
    blk = ATT_BLOCK
    nb = seq // blk
    cur = lambda b, n: (b * nb + n, 0)
    prev = lambda b, n: (b * nb + jnp.maximum(n - 1, 0), 0)
    nxt = lambda b, n: (b * nb + jnp.minimum(n + 1, nb - 1), 0)
    kvspec = lambda f: pl.BlockSpec((blk, 2 * KV_WIDTH), f)
    return pl.pallas_call(
        functools.partial(_attn_kernel, l=l, seq=seq),
        out_shape=jax.ShapeDtypeStruct((t, ATT_WIDTH), BF16),
        grid=(batch, nb),
        in_specs=[
            pl.BlockSpec(memory_space=pltpu.SMEM),
            pl.BlockSpec((blk, ATT_WIDTH), cur),
            kvspec(prev), kvspec(cur), kvspec(nxt),
            kvspec(prev), kvspec(cur), kvspec(nxt),
            pl.BlockSpec((None, 1, ATT_WIDTH), lambda b, n: (l, 0, 0)),
        ],
        out_specs=pl.BlockSpec((blk, ATT_WIDTH), cur),
        compiler_params=_params(("arbitrary", "arbitrary")),
        name="window_attn",
    )(attn_sink, qa, ka2, ka2, ka2, va2, va2, va2, attn_norm3)


def _hgrn_constants():
    c, nl = HG_CHUNK, HG_LEVELS
    r = np.arange(c)
    u = r[None, :]
    rr = r[:, None]
    mats, masks = [], []
    for lev in range(nl):
        m = 1 << lev
        parent = r // (2 * m)
        anchor = parent * 2 * m + m
        upper = r >= anchor
        aa = anchor[:, None]
        mats.append(np.where(upper[:, None], (u > aa) & (u <= rr), (u > rr) & (u <= aa)))
        masks.append((parent[:, None] == parent[None, :]) & upper[:, None] & (~upper)[None, :])
    masks.append(np.eye(c, dtype=bool))
    mats.append(u <= rr)
    mats.append(u > rr)
    mf = np.concatenate(mats, axis=0).astype(np.float32)
    mb = np.concatenate([mt[::-1, ::-1] for mt in mats], axis=0).astype(np.float32)
    kf = np.stack(masks).astype(np.float32)
    kb = np.stack([mk[::-1, ::-1] for mk in masks]).astype(np.float32)
    return mf, mb, kf, kb


def _hgrn_kernel(q_ref, ff_ref, fb_ref, i_ref, g_ref, lb_ref, gn_ref, mf_ref, mb_ref, kf_ref, kb_ref,
                 o_ref, of_scr, ob_scr, st_scr, *, seq):
    c, nl = HG_CHUNK, HG_LEVELS
    nc = seq // c
    lb = lb_ref[...]
    st_scr[...] = jnp.zeros(st_scr.shape, F32)

    unroll = HG_UNROLL if nc % HG_UNROLL == 0 else 1
    dirs = (
        dict(f_ref=ff_ref, lbrow=lb[0:1, :], m_ref=mf_ref, k_ref=kf_ref, o_scr=of_scr, last_row=c - 1, d=0),
        dict(f_ref=fb_ref, lbrow=lb[1:2, :], m_ref=mb_ref, k_ref=kb_ref, o_scr=ob_scr, last_row=0, d=1),
    )

    def body(i, carry):
        work = []
        for u in range(unroll):
            cf = i * unroll + u
            work.append((dirs[0], pl.ds(pl.multiple_of(cf * c, c), c)))
            work.append((dirs[1], pl.ds(pl.multiple_of((nc - 1 - cf) * c, c), c)))
        gates = []
        for dr, rows in work:
            lbf = jnp.maximum(dr["lbrow"], LB_FLOOR)
            oml = 1.0 - dr["lbrow"]
            f = dr["f_ref"][rows, :]
            e = jnp.exp(-jnp.abs(f))
            r = 1.0 / (1.0 + e)
            er = e * r
            pos = f >= 0.0
            logf = jnp.log(lbf + oml * jnp.where(pos, r, er))
            kk = oml * jnp.where(pos, er, r)
            qh = q_ref[rows, :]
            gates.append((logf, kk, qh * _sigmoid(qh), i_ref[rows, :].astype(BF16)))
        gall = [jnp.exp(_dot_exact_lhs(dr["m_ref"][...], g[0])) for (dr, _), g in zip(work, gates)]
        amat = [dr["k_ref"][nl] * _dot_nt(g[2].astype(BF16), g[1].astype(BF16)) for (dr, _), g in zip(work, gates)]
        for lev in range(nl):
            for j, ((dr, _), g) in enumerate(zip(work, gates)):
                gl = gall[j][lev * c:(lev + 1) * c, :]
                amat[j] = amat[j] + dr["k_ref"][lev] * _dot_nt((g[2] * gl).astype(BF16), (g[1] * gl).astype(BF16))
        intra = [_dot(amat[j].astype(BF16), g[3]) for j, g in enumerate(gates)]
        upd = [_dot_tn(g[3], (g[1] * gall[j][(nl + 1) * c:(nl + 2) * c, :]).astype(BF16)) for j, g in enumerate(gates)]
        st = [st_scr[0], st_scr[1]]
        for j, ((dr, rows), g) in enumerate(zip(work, gates)):
            eb = gall[j][nl * c:(nl + 1) * c, :]
            d = dr["d"]
            dr["o_scr"][rows, :] = _dot_nt((g[2] * eb).astype(BF16), st[d].astype(BF16)) + intra[j]
            st[d] = st[d] * eb[dr["last_row"]:dr["last_row"] + 1, :] + upd[j]
        st_scr[0] = st[0]
        st_scr[1] = st[1]
        return carry

    lax.fori_loop(0, nc // unroll, body, 0)

    ep = min(256, seq)
    gn = gn_ref[...]

    def epilogue(j, carry):
        rows = pl.ds(pl.multiple_of(j * ep, ep), ep)
        o = of_scr[rows, :] + ob_scr[rows, :]
        y = o * lax.rsqrt(jnp.mean(o * o, axis=-1, keepdims=True) + EPS) * gn
        g = g_ref[rows, :]
        o_ref[rows, :] = (y * (g * _sigmoid(g))).astype(BF16)
        return carry

    lax.fori_loop(0, seq // ep, epilogue, 0)


def _hgrn(hg, lower, hg_norm3, consts, l, batch, seq):
    t = hg.shape[0]
    mf, mb, kf, kb = consts
    hd = HG_HEAD_DIM

    def col(g):
        return pl.BlockSpec((seq, hd), lambda b, h: (b, g * HG_HEADS + h))

    full2 = lambda a: pl.BlockSpec(a.shape, lambda b, h: (0, 0))
    full3 = lambda a: pl.BlockSpec(a.shape, lambda b, h: (0, 0, 0))
    return pl.pallas_call(
        functools.partial(_hgrn_kernel, seq=seq),
        out_shape=jax.ShapeDtypeStruct((t, HG_WIDTH), BF16),
        grid=(batch, HG_HEADS),
        in_specs=[
            col(0), col(1), col(2), col(3), col(4),
            pl.BlockSpec((None, 2, hd), lambda b, h: (l, 0, h)),
            pl.BlockSpec((None, 1, hd), lambda b, h: (l, 0, 0)),
            full2(mf), full2(mb), full3(kf), full3(kb),
        ],
        out_specs=pl.BlockSpec((seq, hd), lambda b, h: (b, h)),
        scratch_shapes=[
            pltpu.VMEM((seq, hd), F32),
            pltpu.VMEM((seq, hd), F32),
            pltpu.VMEM((2, hd, hd), F32),
        ],
        compiler_params=_params(("arbitrary", "arbitrary")),
        name="hgrn2_scan",
    )(hg, hg, hg, hg, hg, lower, hg_norm3, mf, mb, kf, kb)


def _outproj_kernel(a_ref, o_ref, x_ref, mod_ref, n2_ref, w_ref, wr_ref, br_ref,
                    xo_ref, h2_ref, ti_ref, gt_ref, *, d):
    mod = mod_ref[...]
    y = _dot(a_ref[...], w_ref[0:ATT_WIDTH, :]) + _dot(o_ref[...], w_ref[ATT_WIDTH:ATT_WIDTH + HG_WIDTH, :])
    xn = x_ref[...] + mod[:, 2 * d:3 * d] * y
    xo_ref[...] = xn
    h2 = _rms_mod(xn, n2_ref[...], mod[:, 4 * d:5 * d], mod[:, 3 * d:4 * d])
    h2_ref[...] = _pack_bf16_pairs(h2)
    lg = _dot_f32_nt(wr_ref[...], h2) + br_ref[...]
    eidx = lax.broadcasted_iota(I32, lg.shape, 0)
    vals, idxs = [], []
    for _ in range(TOP_K):
        mx = jnp.max(lg, axis=0, keepdims=True)
        sel = jnp.min(jnp.where(lg == mx, eidx, N_EXPERTS), axis=0, keepdims=True)
        vals.append(mx)
        idxs.append(sel)
        lg = jnp.where(eidx == sel, -jnp.inf, lg)
    ex = [jnp.exp(v - vals[0]) for v in vals]
    den = ex[0]
    for e in ex[1:]:
        den = den + e
    inv = 1.0 / den
    ti_ref[...] = jnp.concatenate(idxs, axis=0)
    gt_ref[...] = jnp.concatenate([e * inv for e in ex], axis=0)


def _outproj(a, o, x2, mod3, norm2, w_out_bf, w_router_t, b_router3, l, seq):
    t, d = x2.shape
    tm = min(PROJ_TILE, seq)
    nb = mod3.shape[0] // norm2.shape[0]
    tok = lambda i: (i, 0)
    lane_tok = lambda i: (0, i)
    return pl.pallas_call(
        functools.partial(_outproj_kernel, d=d),
        out_shape=(
            jax.ShapeDtypeStruct((t, d), F32),
            jax.ShapeDtypeStruct((t, d // 2), U32),
            jax.ShapeDtypeStruct((TOP_K, t), I32),
            jax.ShapeDtypeStruct((TOP_K, t), F32),
        ),
        grid=(t // tm,),
        in_specs=[
            pl.BlockSpec((tm, ATT_WIDTH), tok),
            pl.BlockSpec((tm, HG_WIDTH), tok),
            pl.BlockSpec((tm, d), tok),
            pl.BlockSpec((None, 1, N_MOD * d), lambda i: (l * nb + (i * tm) // seq, 0, 0)),
            pl.BlockSpec((None, 1, d), lambda i: (l, 0, 0)),
            pl.BlockSpec((None, ATT_WIDTH + HG_WIDTH, d), lambda i: (l, 0, 0)),
            pl.BlockSpec((None, N_EXPERTS, d), lambda i: (l, 0, 0)),
            pl.BlockSpec((None, N_EXPERTS, 1), lambda i: (l, 0, 0)),
        ],
        out_specs=(
            pl.BlockSpec((tm, d), tok),
            pl.BlockSpec((tm, d // 2), tok),
            pl.BlockSpec((TOP_K, tm), lane_tok),
            pl.BlockSpec((TOP_K, tm), lane_tok),
        ),
        compiler_params=_params(("arbitrary",)),
        name="out_proj_router",
    )(a, o, x2, mod3, norm2, w_out_bf, w_router_t, b_router3)


def _rank_kernel(ti_ref, tri_ref, rank_ref, cnt_ref, carry_scr):
    @pl.when(pl.program_id(0) == 0)
    def _():
        carry_scr[...] = jnp.zeros(carry_scr.shape, F32)

    ti = ti_ref[...]
    tl = ti.shape[1]
    eidx = lax.broadcasted_iota(I32, (N_EXPERTS, tl), 0)
    carry = carry_scr[...]
    rows = []
    for k in range(TOP_K):
        oh = eidx == ti[k:k + 1, :]
        ohf = jnp.where(oh, 1.0, 0.0)
        pre = _dot(ohf.astype(BF16), tri_ref[...])
        rows.append(jnp.sum(jnp.where(oh, carry + pre, 0.0), axis=0, keepdims=True))
        carry = carry + jnp.sum(ohf, axis=1, keepdims=True)
    carry_scr[...] = carry
    rank_ref[...] = jnp.concatenate(rows, axis=0).astype(I32)
    cnt_ref[...] = jnp.broadcast_to(carry, cnt_ref.shape)


def _ranks(topi):
    k, t = topi.shape
    tl = min(RANK_TILE, t)
    tri = np.triu(np.ones((tl, tl), np.float32), 1)
    return pl.pallas_call(
        _rank_kernel,
        out_shape=(jax.ShapeDtypeStruct((k, t), I32), jax.ShapeDtypeStruct((N_EXPERTS, LANES), F32)),
        grid=(t // tl,),
        in_specs=[pl.BlockSpec((k, tl), lambda i: (0, i)), pl.BlockSpec((tl, tl), lambda i: (0, 0))],
        out_specs=(pl.BlockSpec((k, tl), lambda i: (0, i)), pl.BlockSpec((N_EXPERTS, LANES), lambda i: (0, 0))),
        scratch_shapes=[pltpu.VMEM((N_EXPERTS, 1), F32)],
        compiler_params=_params(("arbitrary",)),
        name="route_rank",
    )(topi, jnp.asarray(tri, BF16))


def _dest_kernel(cnt_ref, ltri_ref, ti_ref, rank_ref, dest_ref, bexp_ref, nused_ref):
    cnt = cnt_ref[...]
    nblk = jnp.floor((cnt + (MOE_ROWS - 1)) * (1.0 / MOE_ROWS))
    pstart_b = _dot(ltri_ref[...], nblk.astype(BF16))
    pend_b = pstart_b + nblk
    pstart = (pstart_b[:, 0:1] * MOE_ROWS).astype(I32)
    ti = ti_ref[...]
    tl = ti.shape[1]
    eidx = lax.broadcasted_iota(I32, (N_EXPERTS, tl), 0)
    rows = []
    for k in range(TOP_K):
        oh = eidx == ti[k:k + 1, :]
        rows.append(jnp.sum(jnp.where(oh, pstart, 0), axis=0, keepdims=True))
    dest_ref[...] = jnp.concatenate(rows, axis=0) + rank_ref[...]
    nb = bexp_ref.shape[1]
    bi = lax.broadcasted_iota(I32, (N_EXPERTS, nb), 1).astype(F32)
    be = jnp.sum(jnp.where(pend_b[:, 0:1] <= bi, 1, 0), axis=0, keepdims=True)
    bexp_ref[...] = jnp.minimum(be, N_EXPERTS - 1).astype(I32)
    nused_ref[...] = pend_b[N_EXPERTS - 1:N_EXPERTS, :].astype(I32)


def _destinations(counts, topi, rank, nblocks):
    k, t = topi.shape
    tl = min(2048, t)
    nbp = -(-nblocks // LANES) * LANES
    ltri = np.tril(np.ones((N_EXPERTS, N_EXPERTS), np.float32), -1)
    return pl.pallas_call(
        _dest_kernel,
        out_shape=(
            jax.ShapeDtypeStruct((k, t), I32),
            jax.ShapeDtypeStruct((1, nbp), I32),
            jax.ShapeDtypeStruct((1, LANES), I32),
        ),
        grid=(t // tl,),
        in_specs=[
            pl.BlockSpec((N_EXPERTS, LANES), lambda i: (0, 0)),
            pl.BlockSpec((N_EXPERTS, N_EXPERTS), lambda i: (0, 0)),
            pl.BlockSpec((k, tl), lambda i: (0, i)),
            pl.BlockSpec((k, tl), lambda i: (0, i)),
        ],
        out_specs=(
            pl.BlockSpec((k, tl), lambda i: (0, i)),
            pl.BlockSpec((1, nbp), lambda i: (0, 0)),
            pl.BlockSpec((1, LANES), lambda i: (0, 0)),
        ),
        compiler_params=_params(("arbitrary",)),
        name="route_dest",
    )(counts, jnp.asarray(ltri, BF16), topi, rank)


def _inverse_kernel(bexp_ref, nused_ref, dest_hbm, inv_hbm, dsm0, dsm1, inv_sm, isem, osem, *, t, tm):
    i = pl.program_id(0)
    nt = pl.num_programs(0)
    nblk = inv_sm.shape[0] // MOE_ROWS
    nused = nused_ref[0]
    dsm = (dsm0, dsm1)

    def idx_copy(j, p):
        return pltpu.make_async_copy(dest_hbm.at[j], dsm[p], isem.at[p])

    @pl.when(i == 0)
    def _():
        idx_copy(0, 0).start()

        def init_block(b, carry):
            nxt = jnp.minimum(b + 1, nblk - 1)
            partial = jnp.logical_or(b >= nused - 1, bexp_ref[nxt] != bexp_ref[b])

            @pl.when(partial)
            def _():
                row0 = b * MOE_ROWS
                spare0 = TOP_K * t + (b % 2) * MOE_ROWS

                def init_row(r, c2):
                    inv_sm[row0 + r] = spare0 + r
                    return c2

                lax.fori_loop(0, MOE_ROWS, init_row, 0, unroll=8)

            return carry

        lax.fori_loop(0, nblk, init_block, 0)

    tok0 = i * tm
    for p in range(2):
        @pl.when(i % 2 == p)
        def _():
            @pl.when(i + 1 < nt)
            def _():
                idx_copy(i + 1, 1 - p).start()

            idx_copy(i, p).wait()

            def body(r, carry):
                for k in range(TOP_K):
                    inv_sm[dsm[p][k * tm + r]] = k * t + tok0 + r
                return carry

            lax.fori_loop(0, tm, body, 0, unroll=8)

    @pl.when(i == nt - 1)
    def _():
        out = pltpu.make_async_copy(inv_sm, inv_hbm, osem)
        out.start()
        out.wait()


def _inverse_map(bexp, nused, dest2, nblocks, t):
    nt, ktm = dest2.shape
    grid_spec = pltpu.PrefetchScalarGridSpec(
        num_scalar_prefetch=2,
        grid=(nt,),
        in_specs=[pl.BlockSpec(memory_space=pl.ANY)],
        out_specs=pl.BlockSpec(memory_space=pl.ANY),
        scratch_shapes=[
            pltpu.SMEM((ktm,), I32),
            pltpu.SMEM((ktm,), I32),
            pltpu.SMEM((nblocks * MOE_ROWS,), I32),
            pltpu.SemaphoreType.DMA((2,)),
            pltpu.SemaphoreType.DMA(()),
        ],
    )
    inv = pl.pallas_call(
        functools.partial(_inverse_kernel, t=t, tm=ktm // TOP_K),
        out_shape=jax.ShapeDtypeStruct((nblocks * MOE_ROWS,), I32),
        grid_spec=grid_spec,
        compiler_params=_params(("arbitrary",)),
        name="route_inverse",
    )(bexp, nused, dest2)
    return inv.reshape(nblocks, MOE_ROWS)


def _pack_bf16_pairs(x):
    n = x.shape[1] // 2
    lo = lax.bitcast_convert_type(x[:, :n].astype(BF16).astype(F32), U32)
    hi = lax.bitcast_convert_type(x[:, n:].astype(BF16).astype(F32), U32)
    return hi | (lo >> 16)


def _unpack_bf16_pairs(w):
    lo = lax.bitcast_convert_type(w << 16, F32)
    hi = lax.bitcast_convert_type(w & jnp.uint32(0xFFFF0000), F32)
    return lo, hi


def _expert_kernel(bexp_ref, nused_ref, inv_hbm, h2_hbm, wgu_ref, bgu_ref, wd_ref, bd_ref, y4_hbm,
                   ig0, ig1, is0, is1, xbuf, ybuf, wgu_bf, wd_bf, gisem, sisem, gsem, ssem, *, dff, t):
    i = pl.program_id(0)
    nblk = pl.num_programs(0)
    nused = nused_ref[0]
    rows = xbuf.shape[1]
    half = xbuf.shape[2]
    used = i < nused
    fresh = jnp.logical_or(i == 0, bexp_ref[i] != bexp_ref[jnp.maximum(i - 1, 0)])
    gidx, sidx = (ig0, ig1), (is0, is1)

    def gidx_copy(j, p):
        return pltpu.make_async_copy(inv_hbm.at[j], gidx[p], gisem.at[p])

    def sidx_copy(j, p):
        return pltpu.make_async_copy(inv_hbm.at[j], sidx[p], sisem.at[p])

    def gather_rows(p):
        def body(rr, carry):
            for u in range(2):
                r = rr * 2 + u
                slot_row = gidx[p][r]
                tok = slot_row & (t - 1) if t & (t - 1) == 0 else lax.rem(slot_row, t)
                pltpu.make_async_copy(h2_hbm.at[pl.ds(tok, 1), :], xbuf.at[p, pl.ds(r, 1), :],
                                      gsem.at[p]).start(priority=u)
            return carry

        for rr in range(rows // 2):
            body(rr, 0)

    def gather_wait(p):
        pltpu.make_async_copy(h2_hbm.at[pl.ds(0, rows), :], xbuf.at[p], gsem.at[p]).wait()

    def scatter_rows(p):
        def body(rr, carry):
            for u in range(2):
                r = rr * 2 + u
                pltpu.make_async_copy(ybuf.at[p, pl.ds(r, 1), :], y4_hbm.at[pl.ds(sidx[p][r], 1), :],
                                      ssem.at[p]).start(priority=u)
            return carry

        for rr in range(rows // 2):
            body(rr, 0)

    def scatter_wait(p):
        pltpu.make_async_copy(ybuf.at[p], y4_hbm.at[pl.ds(0, rows), :], ssem.at[p]).wait()

    @pl.when(i == 0)
    def _():
        ybuf[...] = jnp.zeros(ybuf.shape, U32)
        for s in range(2):
            spare = pltpu.make_async_copy(ybuf.at[s], y4_hbm.at[pl.ds(TOP_K * t + s * rows, rows), :], ssem.at[s])
            spare.start()
            spare.wait()
        gidx_copy(0, 0).start()
        sidx_copy(0, 0).start()

        @pl.when(nused > 1)
        def _():
            gidx_copy(1, 1).start()

        gidx_copy(0, 0).wait()
        gather_rows(0)

    @pl.when(jnp.logical_and(used, fresh))
    def _():
        wgu_bf[...] = wgu_ref[...].astype(BF16)
        wd_bf[...] = wd_ref[...].astype(BF16)

    for p in range(2):
        q = 1 - p

        @pl.when(i % 2 == p)
        def _():
            @pl.when(i + 1 < nused)
            def _():
                gidx_copy(i + 1, q).wait()
                gather_rows(q)
                sidx_copy(i + 1, q).start()

            @pl.when(i + 2 < nused)
            def _():
                gidx_copy(i + 2, p).start()

            @pl.when(jnp.logical_and(i >= 2, i - 2 < nused))
            def _():
                scatter_wait(p)

            @pl.when(used)
            def _():
                gather_wait(p)
                lo, hi = _unpack_bf16_pairs(xbuf[p])
                hu = (_dot(lo.astype(BF16), wgu_bf[0:half, :]) + _dot(hi.astype(BF16), wgu_bf[half:2 * half, :])
                      + bgu_ref[...])
                glu = jnp.minimum(hu[:, 0:dff], SWIGLU_LIMIT)
                lin = jnp.clip(hu[:, dff:2 * dff], -SWIGLU_LIMIT, SWIGLU_LIMIT)
                act = glu * _sigmoid(SWIGLU_ALPHA * glu) * (lin + 1.0)
                ybuf[p] = _pack_bf16_pairs(_dot(act.astype(BF16), wd_bf[...]) + bd_ref[...])
                sidx_copy(i, p).wait()
                scatter_rows(p)

            @pl.when(i == nblk - 1)
            def _():
                @pl.when(jnp.logical_and(i >= 1, i - 1 < nused))
                def _():
                    scatter_wait(q)

                @pl.when(used)
                def _():
                    scatter_wait(p)


def _experts(bexp, nused, inv, h2p, w_gu, b_gu4, w_down, b_down4, l, t):
    nblocks = inv.shape[0]
    half = h2p.shape[1]
    d = 2 * half
    dff = w_down.shape[2]

    wsel = lambda i, be, nu: (l, be[jnp.minimum(i, nu[0] - 1)], 0, 0)
    grid_spec = pltpu.PrefetchScalarGridSpec(
        num_scalar_prefetch=2,
        grid=(nblocks,),
        in_specs=[
            pl.BlockSpec(memory_space=pl.ANY),
            pl.BlockSpec(memory_space=pl.ANY),
            pl.BlockSpec((None, None, d, 2 * dff), wsel),
            pl.BlockSpec((None, None, 1, 2 * dff), wsel),
            pl.BlockSpec((None, None, dff, d), wsel),
            pl.BlockSpec((None, None, 1, d), wsel),
        ],
        out_specs=pl.BlockSpec(memory_space=pl.ANY),
        scratch_shapes=[pltpu.SMEM((MOE_ROWS,), I32)] * 4 + [
            pltpu.VMEM((2, MOE_ROWS, half), U32),
            pltpu.VMEM((2, MOE_ROWS, half), U32),
            pltpu.VMEM((d, 2 * dff), BF16),
            pltpu.VMEM((dff, d), BF16),
        ] + [pltpu.SemaphoreType.DMA((2,))] * 4,
    )
    return pl.pallas_call(
        functools.partial(_expert_kernel, dff=dff, t=t),
        out_shape=jax.ShapeDtypeStruct((TOP_K * t + 2 * MOE_ROWS, half), U32),
        grid_spec=grid_spec,
        compiler_params=_params(("arbitrary",)),
        name="moe_experts",
    )(bexp, nused, inv, h2p, w_gu, b_gu4, w_down, b_down4)


def _combine_kernel(*refs, d):
    y_refs = refs[:TOP_K]
    gt_ref, x_ref, mod_ref, xo_ref = refs[TOP_K:]
    half = d // 2
    gt = gt_ref[...]
    acc_lo = acc_hi = None
    for k in range(TOP_K):
        lo, hi = _unpack_bf16_pairs(y_refs[k][...])
        g = gt[:, k:k + 1]
        acc_lo = g * lo if acc_lo is None else acc_lo + g * lo
        acc_hi = g * hi if acc_hi is None else acc_hi + g * hi
    xo_ref[:, 0:half] = x_ref[:, 0:half] + mod_ref[:, 5 * d:5 * d + half] * acc_lo
    xo_ref[:, half:d] = x_ref[:, half:d] + mod_ref[:, 5 * d + half:6 * d] * acc_hi


def _combine(y4p, gates_t, x2, mod3, l, seq, depth):
    t, d = x2.shape
    tm = min(PROJ_TILE, seq)
    nb = mod3.shape[0] // depth
    nt = t // tm

    def slot(k):
        return pl.BlockSpec((tm, d // 2), lambda i: (k * nt + i, 0))

    return pl.pallas_call(
        functools.partial(_combine_kernel, d=d),
        out_shape=jax.ShapeDtypeStruct((t, d), F32),
        grid=(nt,),
        in_specs=[slot(k) for k in range(TOP_K)] + [
            pl.BlockSpec((tm, TOP_K), lambda i: (i, 0)),
            pl.BlockSpec((tm, d), lambda i: (i, 0)),
            pl.BlockSpec((None, 1, N_MOD * d), lambda i: (l * nb + (i * tm) // seq, 0, 0)),
        ],
        out_specs=pl.BlockSpec((tm, d), lambda i: (i, 0)),
        compiler_params=_params(("arbitrary",)),
        name="moe_combine",
    )(*([y4p] * TOP_K), gates_t, x2, mod3)


def _final_kernel(x_ref, g_ref, o_ref):
    x = x_ref[...]
    o_ref[...] = x * lax.rsqrt(jnp.mean(x * x, axis=-1, keepdims=True) + EPS) * g_ref[...]


def _final_norm(x2, gain):
    t, d = x2.shape
    tm = min(PROJ_TILE, t)
    return pl.pallas_call(
        _final_kernel,
        out_shape=jax.ShapeDtypeStruct((t, d), F32),
        grid=(t // tm,),
        in_specs=[pl.BlockSpec((tm, d), lambda i: (i, 0)), pl.BlockSpec((1, d), lambda i: (0, 0))],
        out_specs=pl.BlockSpec((tm, d), lambda i: (i, 0)),
        compiler_params=_params(("arbitrary",)),
        name="final_norm",
    )(x2, gain.reshape(1, d))


def _mixer_layer(x2, mod3, tables, lower, consts, p, l, batch, seq):
    ra, rb, rc = tables
    qa, ka2, va2, hg = _inproj(x2, mod3, p["norm1"], p["w_in"], ra, rb, rc, l, seq)
    a = _attention(qa, ka2, va2, p["attn_sink"], p["attn_norm"], l, batch, seq)
    o = _hgrn(hg, lower, p["hg_norm"], consts, l, batch, seq)
    return a, o


SC_WINDOW = 128


def _sc_mesh():
    return plsc.VectorSubcoreMesh(core_axis_name="core", subcore_axis_name="subcore")


def _sc_worker(mesh):
    return lax.axis_index("core") * mesh.num_subcores + lax.axis_index("subcore"), mesh.num_cores * mesh.num_subcores


def _sc_scatter_rows(src, dest_rows, n_out):
    n, width = src.shape
    nk = dest_rows.shape[0]
    mesh = _sc_mesh()

    @pl.kernel(out_type=jax.ShapeDtypeStruct((n_out, width), src.dtype), mesh=mesh,
               scratch_types=[pltpu.VMEM((nk, SC_WINDOW), I32), pltpu.VMEM((SC_WINDOW, width), src.dtype)],
               name="sc_scatter_rows")
    def scatter_kernel(src_hbm, idx_hbm, out_hbm, idx_v, rows_v):
        wid, nw = _sc_worker(mesh)
        per = n // SC_WINDOW // nw

        @pl.loop(0, per)
        def _(j):
            base = (wid * per + j) * SC_WINDOW
            pltpu.sync_copy(idx_hbm.at[:, pl.ds(base, SC_WINDOW)], idx_v)
            pltpu.sync_copy(src_hbm.at[pl.ds(base, SC_WINDOW)], rows_v)
            for k in range(nk):
                pltpu.sync_copy(rows_v, out_hbm.at[idx_v.at[k]])

    return scatter_kernel(src, dest_rows)


def _sc_gather_rows(src, rows):
    n = rows.shape[0]
    width = src.shape[1]
    mesh = _sc_mesh()

    @pl.kernel(out_type=jax.ShapeDtypeStruct((n, width), src.dtype), mesh=mesh,
               scratch_types=[pltpu.VMEM((1, SC_WINDOW), I32), pltpu.VMEM((SC_WINDOW, width), src.dtype)],
               name="sc_gather_rows")
    def gather_kernel(src_hbm, idx_hbm, out_hbm, idx_v, rows_v):
        wid, nw = _sc_worker(mesh)
        per = n // SC_WINDOW // nw

        @pl.loop(0, per)
        def _(j):
            base = (wid * per + j) * SC_WINDOW
            pltpu.sync_copy(idx_hbm.at[:, pl.ds(base, SC_WINDOW)], idx_v)
            pltpu.sync_copy(src_hbm.at[idx_v.at[0]], rows_v)
            pltpu.sync_copy(rows_v, out_hbm.at[pl.ds(base, SC_WINDOW)])

    return gather_kernel(src, rows.reshape(1, n))


def _block_expert_kernel(bexp_ref, nused_ref, xs_ref, wgu_ref, bgu_ref, wd_ref, bd_ref, ys_ref, wgu_bf, wd_bf, *, dff):
    i = pl.program_id(0)
    half = xs_ref.shape[1]
    used = i < nused_ref[0]
    fresh = jnp.logical_or(i == 0, bexp_ref[i] != bexp_ref[jnp.maximum(i - 1, 0)])

    @pl.when(jnp.logical_not(used))
    def _():
        ys_ref[...] = jnp.zeros(ys_ref.shape, U32)

    @pl.when(jnp.logical_and(used, fresh))
    def _():
        wgu_bf[...] = wgu_ref[...].astype(BF16)
        wd_bf[...] = wd_ref[...].astype(BF16)

    @pl.when(used)
    def _():
        lo, hi = _unpack_bf16_pairs(xs_ref[...])
        hu = (_dot(lo.astype(BF16), wgu_bf[0:half, :]) + _dot(hi.astype(BF16), wgu_bf[half:2 * half, :])
              + bgu_ref[...])
        glu = jnp.minimum(hu[:, 0:dff], SWIGLU_LIMIT)
        lin = jnp.clip(hu[:, dff:2 * dff], -SWIGLU_LIMIT, SWIGLU_LIMIT)
        act = glu * _sigmoid(SWIGLU_ALPHA * glu) * (lin + 1.0)
        ys_ref[...] = _pack_bf16_pairs(_dot(act.astype(BF16), wd_bf[...]) + bd_ref[...])


def _block_experts(bexp, nused, xs, w_gu, b_gu4, w_down, b_down4, l):
    cap, half = xs.shape
    d = 2 * half
    dff = w_down.shape[2]

    def blk(i, be, nu):
        return jnp.minimum(i, nu[0] - 1)

    wsel = lambda i, be, nu: (l, be[blk(i, be, nu)], 0, 0)
    grid_spec = pltpu.PrefetchScalarGridSpec(
        num_scalar_prefetch=2,
        grid=(cap // MOE_ROWS,),
        in_specs=[
            pl.BlockSpec((MOE_ROWS, half), lambda i, be, nu: (blk(i, be, nu), 0)),
            pl.BlockSpec((None, None, d, 2 * dff), wsel),
            pl.BlockSpec((None, None, 1, 2 * dff), wsel),
            pl.BlockSpec((None, None, dff, d), wsel),
            pl.BlockSpec((None, None, 1, d), wsel),
        ],
        out_specs=pl.BlockSpec((MOE_ROWS, half), lambda i, be, nu: (i, 0)),
        scratch_shapes=[pltpu.VMEM((d, 2 * dff), BF16), pltpu.VMEM((dff, d), BF16)],
    )
    return pl.pallas_call(
        functools.partial(_block_expert_kernel, dff=dff),
        out_shape=jax.ShapeDtypeStruct((cap, half), U32),
        grid_spec=grid_spec,
        compiler_params=_params(("arbitrary",)),
        name="moe_block_experts",
    )(bexp, nused, xs, w_gu, b_gu4, w_down, b_down4)


def _moe_layer_sc(a, o, x2, mod3, p, l, seq, depth):
    t, d = x2.shape
    xn, h2p, topi, gates = _outproj(a, o, x2, mod3, p["norm2"], p["w_out"], p["w_router_t"], p["b_router"], l, seq)
    rank, counts = _ranks(topi)
    nblocks = (t * TOP_K + N_EXPERTS * MOE_ROWS) // MOE_ROWS
    dest, bexp, nused = _destinations(counts, topi, rank, nblocks)
    xs = _sc_scatter_rows(h2p, dest, nblocks * MOE_ROWS)
    ys = _block_experts(bexp.reshape(-1), nused.reshape(-1), xs, p["w_gu"], p["b_gu"], p["w_down"], p["b_down"], l)
    y4p = _sc_gather_rows(ys, dest.reshape(-1))
    return _combine(y4p, gates.T, xn, mod3, l, seq, depth)


def _moe_layer(a, o, x2, mod3, p, l, seq, depth):
    t, d = x2.shape
    xn, h2p, topi, gates = _outproj(a, o, x2, mod3, p["norm2"], p["w_out"], p["w_router_t"], p["b_router"], l, seq)
    rank, counts = _ranks(topi)
    nblocks = (t * TOP_K + N_EXPERTS * MOE_ROWS) // MOE_ROWS
    dest, bexp, nused = _destinations(counts, topi, rank, nblocks)
    tm = min(TOK_TILE, t)
    dest2 = dest.reshape(TOP_K, t // tm, tm).transpose(1, 0, 2).reshape(t // tm, TOP_K * tm)
    bexp, nused = bexp.reshape(-1), nused.reshape(-1)
    inv = _inverse_map(bexp, nused, dest2, nblocks, t)
    y4p = _experts(bexp, nused, inv, h2p, p["w_gu"], p["b_gu"], p["w_down"], p["b_down"], l, t)
    return _combine(y4p, gates.T, xn, mod3, l, seq, depth)


def kernel(x, c, positions, w_ada, b_ada, norm1, w_in, attn_sink, attn_norm, hg_lb_logits, hg_norm, w_out, norm2,
           w_router, b_router, w_gu, b_gu, w_down, b_down, final_norm):
    batch, seq, d = x.shape
    depth = w_ada.shape[0]
    t = batch * seq
    p = {
        "norm1": norm1.reshape(depth, 1, d),
        "w_in": w_in.astype(BF16),
        "attn_sink": attn_sink.astype(F32),
        "attn_norm": attn_norm.reshape(depth, 1, ATT_WIDTH),
        "hg_norm": hg_norm.reshape(depth, 1, HG_HEAD_DIM),
        "w_out": w_out.astype(BF16),
        "norm2": norm2.reshape(depth, 1, d),
        "w_router_t": jnp.swapaxes(w_router, 1, 2),
        "b_router": b_router.reshape(depth, N_EXPERTS, 1),
        "w_gu": w_gu,
        "b_gu": b_gu.reshape(depth, N_EXPERTS, 1, b_gu.shape[-1]),
        "w_down": w_down,
        "b_down": b_down.reshape(depth, N_EXPERTS, 1, d),
    }
    mod3 = _ada_all(c, w_ada, b_ada).reshape(depth * batch, 1, N_MOD * d)
    lower = _lower_bounds(hg_lb_logits)
    tables = _rope_tables(positions)
    consts = _hgrn_constants()
    consts = (jnp.asarray(consts[0], BF16), jnp.asarray(consts[1], BF16), jnp.asarray(consts[2]), jnp.asarray(consts[3]))
    x2 = x.reshape(t, d)
    for l in range(depth):
        a, o = _mixer_layer(x2, mod3, tables, lower, consts, p, l, batch, seq)
        x2 = _moe_layer_sc(a, o, x2, mod3, p, l, seq, depth)
    return _final_norm(x2, final_norm).reshape(batch, seq, d)
```

```python
import functools

import numpy as np
import jax
import jax.numpy as jnp
from jax import lax
from jax.experimental import pallas as pl
from jax.experimental.pallas import tpu as pltpu
from jax.experimental.pallas import tpu_sc as plsc

F32 = jnp.float32
BF16 = jnp.bfloat16
I32 = jnp.int32
U32 = jnp.uint32

ATT_HEADS = 8
ATT_KV_HEADS = 2
ATT_HEAD_DIM = 64
ATT_WIDTH = ATT_HEADS * ATT_HEAD_DIM
KV_WIDTH = ATT_KV_HEADS * ATT_HEAD_DIM
WINDOW = 128
ATT_BLOCK = 128
ROPE_THETA = 500000.0
ROPE_DIM = ATT_HEAD_DIM // 4
HG_HEADS = 4
HG_HEAD_DIM = 128
HG_WIDTH = HG_HEADS * HG_HEAD_DIM
N_EXPERTS = 32
TOP_K = 4
SWIGLU_ALPHA = 1.702
SWIGLU_LIMIT = 7.0
N_MOD = 6
EPS = 1e-6
NEG_INF = -1e30
LB_FLOOR = 1e-30

LANES = 128
HG_CHUNK = 64
HG_LEVELS = 6
HG_UNROLL = 8
MOE_SHIFT = 8
MOE_ROWS = 1 << MOE_SHIFT
TOK_TILE = 256
PROJ_TILE = 512
RANK_TILE = 512
VMEM_LIMIT = 56 * 1024 * 1024


def _dot(a, b):
    return jnp.dot(a, b, preferred_element_type=F32)


def _dot_nt(a, b):
    return lax.dot_general(a, b, (((1,), (1,)), ((), ())), preferred_element_type=F32)


def _dot_tn(a, b):
    return lax.dot_general(a, b, (((0,), (0,)), ((), ())), preferred_element_type=F32)


def _split3(x):
    hi = x.astype(BF16)
    r1 = x - hi.astype(F32)
    mid = r1.astype(BF16)
    lo = (r1 - mid.astype(F32)).astype(BF16)
    return hi, mid, lo


def _dot_exact_lhs(m_bf16, x):
    hi, mid, lo = _split3(x)
    return _dot(m_bf16, hi) + _dot(m_bf16, mid) + _dot(m_bf16, lo)


def _dot_f32_nt(a, b):
    ah, am, _ = _split3(a)
    bh, bm, _ = _split3(b)
    return _dot_nt(ah, bh) + _dot_nt(ah, bm) + _dot_nt(am, bh)


def _dot_f32(a, b):
    ah, am, al = _split3(a)
    bh, bm, bl = _split3(b)
    return (_dot(ah, bh) + _dot(ah, bm) + _dot(am, bh)
            + _dot(ah, bl) + _dot(am, bm) + _dot(al, bh))


def _sigmoid(x):
    return 1.0 / (1.0 + jnp.exp(-x))


def _params(sem=None):
    return pltpu.CompilerParams(dimension_semantics=sem, vmem_limit_bytes=VMEM_LIMIT)


def _ada_kernel(c_ref, w_ref, b_ref, o_ref):
    c = c_ref[...]
    cond = c * _sigmoid(c)
    o_ref[...] = _dot_f32(cond, w_ref[...]) + b_ref[...]


def _ada_all(c, w_ada, b_ada):
    depth, d, n = w_ada.shape
    b = c.shape[0]
    nt = n // d
    return pl.pallas_call(
        _ada_kernel,
        out_shape=jax.ShapeDtypeStruct((depth, b, n), F32),
        grid=(depth, nt),
        in_specs=[
            pl.BlockSpec((b, d), lambda l, j: (0, 0)),
            pl.BlockSpec((None, d, d), lambda l, j: (l, 0, j)),
            pl.BlockSpec((None, 1, d), lambda l, j: (l, 0, j)),
        ],
        out_specs=pl.BlockSpec((None, b, d), lambda l, j: (l, 0, j)),
        compiler_params=_params(("arbitrary", "arbitrary")),
        name="ada_mod",
    )(c, w_ada, b_ada.reshape(depth, 1, n))


def _lb_kernel(x_ref, o_ref):
    depth = x_ref.shape[0]
    xs = [x_ref[l] for l in range(depth)]
    m = xs[0]
    for l in range(1, depth):
        m = jnp.maximum(m, xs[l])
    es = [jnp.exp(v - m) for v in xs]
    den = es[0]
    for l in range(1, depth):
        den = den + es[l]
    ps = [e / den for e in es]
    run = ps[0]
    o_ref[0] = run - ps[0]
    for l in range(1, depth):
        run = run + ps[l]
        o_ref[l] = run - ps[0]


def _lower_bounds(hg_lb_logits):
    return pl.pallas_call(
        _lb_kernel,
        out_shape=jax.ShapeDtypeStruct(hg_lb_logits.shape, F32),
        name="hg_lower_bounds",
    )(hg_lb_logits.astype(F32))


def _rope_kernel(pos_ref, invf_ref, a_ref, b_ref, c_ref):
    pos = pos_ref[...].astype(F32)
    ang = pos * invf_ref[...]
    cs = jnp.cos(ang)
    sn = jnp.sin(ang)
    lane = lax.broadcasted_iota(I32, ang.shape, 1) & (ATT_HEAD_DIM - 1)
    half = ROPE_DIM // 2
    first = lane < half
    second = (lane >= half) & (lane < ROPE_DIM)
    a_ref[...] = jnp.where(first | second, cs, 1.0)
    b_ref[...] = jnp.where(first, -sn, 0.0)
    c_ref[...] = jnp.where(second, sn, 0.0)


def _rope_tables(positions):
    t = positions.size
    half = ROPE_DIM // 2
    inv = (np.float32(ROPE_THETA) ** (-(np.arange(half, dtype=np.float32) * np.float32(2.0) / np.float32(ROPE_DIM)))).astype(np.float32)
    lane = np.arange(LANES) % ATT_HEAD_DIM
    pat = np.where(lane < ROPE_DIM, inv[lane % half], 0.0).astype(np.float32).reshape(1, LANES)
    tm = min(t, 2048)
    shp = jax.ShapeDtypeStruct((t, LANES), F32)
    spec = pl.BlockSpec((tm, LANES), lambda i: (i, 0))
    return pl.pallas_call(
        _rope_kernel,
        out_shape=(shp, shp, shp),
        grid=(t // tm,),
        in_specs=[pl.BlockSpec((tm, 1), lambda i: (i, 0)), pl.BlockSpec((1, LANES), lambda i: (0, 0))],
        out_specs=(spec, spec, spec),
        compiler_params=_params(("arbitrary",)),
        name="rope_tables",
    )(positions.reshape(t, 1).astype(I32), jnp.asarray(pat))


def _rms_mod(x, gain, scale, shift):
    ms = jnp.mean(x * x, axis=-1, keepdims=True)
    return (x * lax.rsqrt(ms + EPS) * gain) * (1.0 + scale) + shift


def _rope_apply(x, a, b, c):
    half = ROPE_DIM // 2
    return x * a + pltpu.roll(x, LANES - half, 1) * b + pltpu.roll(x, half, 1) * c


def _inproj_kernel(x_ref, mod_ref, n1_ref, w_ref, ra_ref, rb_ref, rc_ref, qa_ref, ka_ref, va_ref, hg_ref, *, d):
    mod = mod_ref[...]
    h = _rms_mod(x_ref[...], n1_ref[...], mod[:, d:2 * d], mod[:, 0:d]).astype(BF16)
    a, b, c = ra_ref[...], rb_ref[...], rc_ref[...]
    kvw = ATT_WIDTH + 2 * KV_WIDTH
    pa = _dot(h, w_ref[:, 0:kvw])
    scale = ATT_HEAD_DIM ** -0.5
    for g in range(ATT_WIDTH // LANES):
        qg = _rope_apply(pa[:, g * LANES:(g + 1) * LANES], a, b, c)
        qa_ref[:, g * LANES:(g + 1) * LANES] = (qg * scale).astype(BF16)
    k = _rope_apply(pa[:, ATT_WIDTH:ATT_WIDTH + KV_WIDTH], a, b, c)
    v = pa[:, ATT_WIDTH + KV_WIDTH:kvw]
    ka_ref[:, 0:LANES] = k.astype(BF16)
    ka_ref[:, LANES:2 * LANES] = pltpu.roll(k, ATT_HEAD_DIM, 1).astype(BF16)
    va_ref[:, 0:LANES] = v.astype(BF16)
    va_ref[:, LANES:2 * LANES] = pltpu.roll(v, ATT_HEAD_DIM, 1).astype(BF16)
    for g in range(5):
        lo = kvw + g * HG_WIDTH
        hg_ref[:, g * HG_WIDTH:(g + 1) * HG_WIDTH] = _dot(h, w_ref[:, lo:lo + HG_WIDTH])


def _inproj(x2, mod3, norm1, w_in_bf, ra, rb, rc, l, seq):
    t, d = x2.shape
    n_in = w_in_bf.shape[-1]
    tm = min(PROJ_TILE, seq)
    nb = mod3.shape[0] // norm1.shape[0]
    tok = lambda i: (i, 0)
    return pl.pallas_call(
        functools.partial(_inproj_kernel, d=d),
        out_shape=(
            jax.ShapeDtypeStruct((t, ATT_WIDTH), BF16),
            jax.ShapeDtypeStruct((t, 2 * KV_WIDTH), BF16),
            jax.ShapeDtypeStruct((t, 2 * KV_WIDTH), BF16),
            jax.ShapeDtypeStruct((t, 5 * HG_WIDTH), F32),
        ),
        grid=(t // tm,),
        in_specs=[
            pl.BlockSpec((tm, d), tok),
            pl.BlockSpec((None, 1, N_MOD * d), lambda i: (l * nb + (i * tm) // seq, 0, 0)),
            pl.BlockSpec((None, 1, d), lambda i: (l, 0, 0)),
            pl.BlockSpec((None, d, n_in), lambda i: (l, 0, 0)),
            pl.BlockSpec((tm, LANES), tok),
            pl.BlockSpec((tm, LANES), tok),
            pl.BlockSpec((tm, LANES), tok),
        ],
        out_specs=(
            pl.BlockSpec((tm, ATT_WIDTH), tok),
            pl.BlockSpec((tm, 2 * KV_WIDTH), tok),
            pl.BlockSpec((tm, 2 * KV_WIDTH), tok),
            pl.BlockSpec((tm, 5 * HG_WIDTH), tok),
        ),
        compiler_params=_params(("arbitrary",)),
        name="in_proj",
    )(x2, mod3, norm1, w_in_bf, ra, rb, rc)


def _attn_kernel(sink_ref, q_ref, kp_ref, kc_ref, kn_ref, vp_ref, vc_ref, vn_ref, gain_ref, o_ref, *, l, seq):
    n = pl.program_id(1)
    blk = ATT_BLOCK
    k2 = jnp.concatenate([kp_ref[...], kc_ref[...], kn_ref[...]], axis=0)
    v2 = jnp.concatenate([vp_ref[...], vc_ref[...], vn_ref[...]], axis=0)
    lane = lax.broadcasted_iota(I32, (3 * blk, LANES), 1)
    lo_half = lane < ATT_HEAD_DIM
    zero = jnp.zeros((3 * blk, LANES), BF16)
    ka, kb = k2[:, 0:LANES], k2[:, LANES:2 * LANES]
    va, vb = v2[:, 0:LANES], v2[:, LANES:2 * LANES]
    kz = [[jnp.where(lo_half, ka, zero), jnp.where(lo_half, zero, kb)],
          [jnp.where(lo_half, kb, zero), jnp.where(lo_half, zero, ka)]]
    vz = [[jnp.where(lo_half, va, zero), jnp.where(lo_half, zero, vb)],
          [jnp.where(lo_half, vb, zero), jnp.where(lo_half, zero, va)]]
    qpos = n * blk + lax.broadcasted_iota(I32, (blk, 3 * blk), 0)
    kpos = (n - 1) * blk + lax.broadcasted_iota(I32, (blk, 3 * blk), 1)
    valid = (jnp.abs(qpos - kpos) <= WINDOW) & (kpos >= 0) & (kpos < seq)
    group = ATT_HEADS // ATT_KV_HEADS
    outs = []
    for m in range(ATT_HEADS // 2):
        qp = q_ref[:, m * LANES:(m + 1) * LANES]
        acc = None
        for half in range(2):
            hh = 2 * m + half
            j = hh // group
            s = _dot_nt(qp, kz[j][half])
            s = jnp.where(valid, s, NEG_INF)
            sink = sink_ref[l, hh]
            mx = jnp.maximum(jnp.max(s, axis=-1, keepdims=True), sink)
            p = jnp.exp(s - mx)
            den = jnp.sum(p, axis=-1, keepdims=True) + jnp.exp(sink - mx)
            p = (p * (1.0 / den)).astype(BF16)
            pv = _dot(p, vz[j][half])
            acc = pv if acc is None else acc + pv
        outs.append(acc)
    o = jnp.concatenate(outs, axis=-1)
    ms = jnp.mean(o * o, axis=-1, keepdims=True)
    o_ref[...] = (o * lax.rsqrt(ms + EPS) * gain_ref[...]).astype(BF16)


def _attention(qa, ka2, va2, attn_sink, attn_norm3, l, batch, seq):
    t = qa.shape[0]
    blk = ATT_BLOCK
    nb = seq // blk
    cur = lambda b, n: (b * nb + n, 0)
    prev = lambda b, n: (b * nb + jnp.maximum(n - 1, 0), 0)
    nxt = lambda b, n: (b * nb + jnp.minimum(n + 1, nb - 1), 0)
    kvspec = lambda f: pl.BlockSpec((blk, 2 * KV_WIDTH), f)
    return pl.pallas_call(
        functools.partial(_attn_kernel, l=l, seq=seq),
        out_shape=jax.ShapeDtypeStruct((t, ATT_WIDTH), BF16),
        grid=(batch, nb),
        in_specs=[
            pl.BlockSpec(memory_space=pltpu.SMEM),
            pl.BlockSpec((blk, ATT_WIDTH), cur),
            kvspec(prev), kvspec(cur), kvspec(nxt),
            kvspec(prev), kvspec(cur), kvspec(nxt),
            pl.BlockSpec((None, 1, ATT_WIDTH), lambda b, n: (l, 0, 0)),
        ],
        out_specs=pl.BlockSpec((blk, ATT_WIDTH), cur),
        compiler_params=_params(("arbitrary", "arbitrary")),
        name="window_attn",
    )(attn_sink, qa, ka2, ka2, ka2, va2, va2, va2, attn_norm3)


def _hgrn_constants():
    c, nl = HG_CHUNK, HG_LEVELS
    r = np.arange(c)
    u = r[None, :]
    rr = r[:, None]
    mats, masks = [], []
    for lev in range(nl):
        m = 1 << lev
        parent = r // (2 * m)
        anchor = parent * 2 * m + m
        upper = r >= anchor
        aa = anchor[:, None]
        mats.append(np.where(upper[:, None], (u > aa) & (u <= rr), (u > rr) & (u <= aa)))
        masks.append((parent[:, None] == parent[None, :]) & upper[:, None] & (~upper)[None, :])
    masks.append(np.eye(c, dtype=bool))
    mats.append(u <= rr)
    mats.append(u > rr)
    mf = np.concatenate(mats, axis=0).astype(np.float32)
    mb = np.concatenate([mt[::-1, ::-1] for mt in mats], axis=0).astype(np.float32)
    kf = np.stack(masks).astype(np.float32)
    kb = np.stack([mk[::-1, ::-1] for mk in masks]).astype(np.float32)
    return mf, mb, kf, kb


def _hgrn_kernel(q_ref, ff_ref, fb_ref, i_ref, g_ref, lb_ref, gn_ref, mf_ref, mb_ref, kf_ref, kb_ref,
                 o_ref, of_scr, ob_scr, st_scr, *, seq):
    c, nl = HG_CHUNK, HG_LEVELS
    nc = seq // c
    lb = lb_ref[...]
    st_scr[...] = jnp.zeros(st_scr.shape, F32)

    unroll = HG_UNROLL if nc % HG_UNROLL == 0 else 1
    dirs = (
        dict(f_ref=ff_ref, lbrow=lb[0:1, :], m_ref=mf_ref, k_ref=kf_ref, o_scr=of_scr, last_row=c - 1, d=0),
        dict(f_ref=fb_ref, lbrow=lb[1:2, :], m_ref=mb_ref, k_ref=kb_ref, o_scr=ob_scr, last_row=0, d=1),
    )

    def body(i, carry):
        work = []
        for u in range(unroll):
            cf = i * unroll + u
            work.append((dirs[0], pl.ds(pl.multiple_of(cf * c, c), c)))
            work.append((dirs[1], pl.ds(pl.multiple_of((nc - 1 - cf) * c, c), c)))
        gates = []
        for dr, rows in work:
            lbf = jnp.maximum(dr["lbrow"], LB_FLOOR)
            oml = 1.0 - dr["lbrow"]
            f = dr["f_ref"][rows, :]
            e = jnp.exp(-jnp.abs(f))
            r = 1.0 / (1.0 + e)
            er = e * r
            pos = f >= 0.0
            logf = jnp.log(lbf + oml * jnp.where(pos, r, er))
            kk = oml * jnp.where(pos, er, r)
            qh = q_ref[rows, :]
            gates.append((logf, kk, qh * _sigmoid(qh), i_ref[rows, :].astype(BF16)))
        gall = [jnp.exp(_dot_exact_lhs(dr["m_ref"][...], g[0])) for (dr, _), g in zip(work, gates)]
        amat = [dr["k_ref"][nl] * _dot_nt(g[2].astype(BF16), g[1].astype(BF16)) for (dr, _), g in zip(work, gates)]
        for lev in range(nl):
            for j, ((dr, _), g) in enumerate(zip(work, gates)):
                gl = gall[j][lev * c:(lev + 1) * c, :]
                amat[j] = amat[j] + dr["k_ref"][lev] * _dot_nt((g[2] * gl).astype(BF16), (g[1] * gl).astype(BF16))
        intra = [_dot(amat[j].astype(BF16), g[3]) for j, g in enumerate(gates)]
        upd = [_dot_tn(g[3], (g[1] * gall[j][(nl + 1) * c:(nl + 2) * c, :]).astype(BF16)) for j, g in enumerate(gates)]
        st = [st_scr[0], st_scr[1]]
        for j, ((dr, rows), g) in enumerate(zip(work, gates)):
            eb = gall[j][nl * c:(nl + 1) * c, :]
            d = dr["d"]
            dr["o_scr"][rows, :] = _dot_nt((g[2] * eb).astype(BF16), st[d].astype(BF16)) + intra[j]
            st[d] = st[d] * eb[dr["last_row"]:dr["last_row"] + 1, :] + upd[j]
        st_scr[0] = st[0]
        st_scr[1] = st[1]
        return carry

    lax.fori_loop(0, nc // unroll, body, 0)

    ep = min(256, seq)
    gn = gn_ref[...]

    def epilogue(j, carry):
        rows = pl.ds(pl.multiple_of(j * ep, ep), ep)
        o = of_scr[rows, :] + ob_scr[rows, :]
        y = o * lax.rsqrt(jnp.mean(o * o, axis=-1, keepdims=True) + EPS) * gn
        g = g_ref[rows, :]
        o_ref[rows, :] = (y * (g * _sigmoid(g))).astype(BF16)
        return carry

    lax.fori_loop(0, seq // ep, epilogue, 0)


def _hgrn(hg, lower, hg_norm3, consts, l, batch, seq):
    t = hg.shape[0]
    mf, mb, kf, kb = consts
    hd = HG_HEAD_DIM

    def col(g):
        return pl.BlockSpec((seq, hd), lambda b, h: (b, g * HG_HEADS + h))

    full2 = lambda a: pl.BlockSpec(a.shape, lambda b, h: (0, 0))
    full3 = lambda a: pl.BlockSpec(a.shape, lambda b, h: (0, 0, 0))
    return pl.pallas_call(
        functools.partial(_hgrn_kernel, seq=seq),
        out_shape=jax.ShapeDtypeStruct((t, HG_WIDTH), BF16),
        grid=(batch, HG_HEADS),
        in_specs=[
            col(0), col(1), col(2), col(3), col(4),
            pl.BlockSpec((None, 2, hd), lambda b, h: (l, 0, h)),
            pl.BlockSpec((None, 1, hd), lambda b, h: (l, 0, 0)),
            full2(mf), full2(mb), full3(kf), full3(kb),
        ],
        out_specs=pl.BlockSpec((seq, hd), lambda b, h: (b, h)),
        scratch_shapes=[
            pltpu.VMEM((seq, hd), F32),
            pltpu.VMEM((seq, hd), F32),
            pltpu.VMEM((2, hd, hd), F32),
        ],
        compiler_params=_params(("arbitrary", "arbitrary")),
        name="hgrn2_scan",
    )(hg, hg, hg, hg, hg, lower, hg_norm3, mf, mb, kf, kb)


def _outproj_kernel(a_ref, o_ref, x_ref, mod_ref, n2_ref, w_ref, wr_ref, br_ref,
                    xo_ref, h2_ref, ti_ref, gt_ref, *, d):
    mod = mod_ref[...]
    y = _dot(a_ref[...], w_ref[0:ATT_WIDTH, :]) + _dot(o_ref[...], w_ref[ATT_WIDTH:ATT_WIDTH + HG_WIDTH, :])
    xn = x_ref[...] + mod[:, 2 * d:3 * d] * y
    xo_ref[...] = xn
    h2 = _rms_mod(xn, n2_ref[...], mod[:, 4 * d:5 * d], mod[:, 3 * d:4 * d])
    h2_ref[...] = _pack_bf16_pairs(h2)
    lg = _dot_f32_nt(wr_ref[...], h2) + br_ref[...]
    eidx = lax.broadcasted_iota(I32, lg.shape, 0)
    vals, idxs = [], []
    for _ in range(TOP_K):
        mx = jnp.max(lg, axis=0, keepdims=True)
        sel = jnp.min(jnp.where(lg == mx, eidx, N_EXPERTS), axis=0, keepdims=True)
        vals.append(mx)
        idxs.append(sel)
        lg = jnp.where(eidx == sel, -jnp.inf, lg)
    ex = [jnp.exp(v - vals[0]) for v in vals]
    den = ex[0]
    for e in ex[1:]:
        den = den + e
    inv = 1.0 / den
    ti_ref[...] = jnp.concatenate(idxs, axis=0)
    gt_ref[...] = jnp.concatenate([e * inv for e in ex], axis=0)


def _outproj(a, o, x2, mod3, norm2, w_out_bf, w_router_t, b_router3, l, seq):
    t, d = x2.shape
    tm = min(PROJ_TILE, seq)
    nb = mod3.shape[0] // norm2.shape[0]
    tok = lambda i: (i, 0)
    lane_tok = lambda i: (0, i)
    return pl.pallas_call(
        functools.partial(_outproj_kernel, d=d),
        out_shape=(
            jax.ShapeDtypeStruct((t, d), F32),
            jax.ShapeDtypeStruct((t, d // 2), U32),
            jax.ShapeDtypeStruct((TOP_K, t), I32),
            jax.ShapeDtypeStruct((TOP_K, t), F32),
        ),
        grid=(t // tm,),
        in_specs=[
            pl.BlockSpec((tm, ATT_WIDTH), tok),
            pl.BlockSpec((tm, HG_WIDTH), tok),
            pl.BlockSpec((tm, d), tok),
            pl.BlockSpec((None, 1, N_MOD * d), lambda i: (l * nb + (i * tm) // seq, 0, 0)),
            pl.BlockSpec((None, 1, d), lambda i: (l, 0, 0)),
            pl.BlockSpec((None, ATT_WIDTH + HG_WIDTH, d), lambda i: (l, 0, 0)),
            pl.BlockSpec((None, N_EXPERTS, d), lambda i: (l, 0, 0)),
            pl.BlockSpec((None, N_EXPERTS, 1), lambda i: (l, 0, 0)),
        ],
        out_specs=(
            pl.BlockSpec((tm, d), tok),
            pl.BlockSpec((tm, d // 2), tok),
            pl.BlockSpec((TOP_K, tm), lane_tok),
            pl.BlockSpec((TOP_K, tm), lane_tok),
        ),
        compiler_params=_params(("arbitrary",)),
        name="out_proj_router",
    )(a, o, x2, mod3, norm2, w_out_bf, w_router_t, b_router3)


def _rank_kernel(ti_ref, tri_ref, rank_ref, cnt_ref, carry_scr):
    @pl.when(pl.program_id(0) == 0)
    def _():
        carry_scr[...] = jnp.zeros(carry_scr.shape, F32)

    ti = ti_ref[...]
    tl = ti.shape[1]
    eidx = lax.broadcasted_iota(I32, (N_EXPERTS, tl), 0)
    carry = carry_scr[...]
    rows = []
    for k in range(TOP_K):
        oh = eidx == ti[k:k + 1, :]
        ohf = jnp.where(oh, 1.0, 0.0)
        pre = _dot(ohf.astype(BF16), tri_ref[...])
        rows.append(jnp.sum(jnp.where(oh, carry + pre, 0.0), axis=0, keepdims=True))
        carry = carry + jnp.sum(ohf, axis=1, keepdims=True)
    carry_scr[...] = carry
    rank_ref[...] = jnp.concatenate(rows, axis=0).astype(I32)
    cnt_ref[...] = jnp.broadcast_to(carry, cnt_ref.shape)


def _ranks(topi):
    k, t = topi.shape
    tl = min(RANK_TILE, t)
    tri = np.triu(np.ones((tl, tl), np.float32), 1)
    return pl.pallas_call(
        _rank_kernel,
        out_shape=(jax.ShapeDtypeStruct((k, t), I32), jax.ShapeDtypeStruct((N_EXPERTS, LANES), F32)),
        grid=(t // tl,),
        in_specs=[pl.BlockSpec((k, tl), lambda i: (0, i)), pl.BlockSpec((tl, tl), lambda i: (0, 0))],
        out_specs=(pl.BlockSpec((k, tl), lambda i: (0, i)), pl.BlockSpec((N_EXPERTS, LANES), lambda i: (0, 0))),
        scratch_shapes=[pltpu.VMEM((N_EXPERTS, 1), F32)],
        compiler_params=_params(("arbitrary",)),
        name="route_rank",
    )(topi, jnp.asarray(tri, BF16))


def _dest_kernel(cnt_ref, ltri_ref, ti_ref, rank_ref, dest_ref, bexp_ref, nused_ref, eblk_ref):
    cnt = cnt_ref[...]
    nblk = jnp.floor((cnt + (MOE_ROWS - 1)) * (1.0 / MOE_ROWS))
    pstart_b = _dot(ltri_ref[...], nblk.astype(BF16))
    pend_b = pstart_b + nblk
    pstart = (pstart_b[:, 0:1] * MOE_ROWS).astype(I32)
    ti = ti_ref[...]
    tl = ti.shape[1]
    eidx = lax.broadcasted_iota(I32, (N_EXPERTS, tl), 0)
    rows = []
    for k in range(TOP_K):
        oh = eidx == ti[k:k + 1, :]
        rows.append(jnp.sum(jnp.where(oh, pstart, 0), axis=0, keepdims=True))
    dest_ref[...] = jnp.concatenate(rows, axis=0) + rank_ref[...]
    nb = bexp_ref.shape[1]
    bi = lax.broadcasted_iota(I32, (N_EXPERTS, nb), 1).astype(F32)
    be = jnp.sum(jnp.where(pend_b[:, 0:1] <= bi, 1, 0), axis=0, keepdims=True)
    bexp_ref[...] = jnp.minimum(be, N_EXPERTS - 1).astype(I32)
    nused_ref[...] = pend_b[N_EXPERTS - 1:N_EXPERTS, :].astype(I32)
    eblk_ref[0] = pstart_b.astype(I32)
    eblk_ref[1] = nblk.astype(I32)


def _destinations(counts, topi, rank, nblocks):
    k, t = topi.shape
    tl = min(2048, t)
    nbp = -(-nblocks // LANES) * LANES
    ltri = np.tril(np.ones((N_EXPERTS, N_EXPERTS), np.float32), -1)
    return pl.pallas_call(
        _dest_kernel,
        out_shape=(
            jax.ShapeDtypeStruct((k, t), I32),
            jax.ShapeDtypeStruct((1, nbp), I32),
            jax.ShapeDtypeStruct((1, LANES), I32),
            jax.ShapeDtypeStruct((2, N_EXPERTS, LANES), I32),
        ),
        grid=(t // tl,),
        in_specs=[
            pl.BlockSpec((N_EXPERTS, LANES), lambda i: (0, 0)),
            pl.BlockSpec((N_EXPERTS, N_EXPERTS), lambda i: (0, 0)),
            pl.BlockSpec((k, tl), lambda i: (0, i)),
            pl.BlockSpec((k, tl), lambda i: (0, i)),
        ],
        out_specs=(
            pl.BlockSpec((k, tl), lambda i: (0, i)),
            pl.BlockSpec((1, nbp), lambda i: (0, 0)),
            pl.BlockSpec((1, LANES), lambda i: (0, 0)),
            pl.BlockSpec((2, N_EXPERTS, LANES), lambda i: (0, 0, 0)),
        ),
        compiler_params=_params(("arbitrary",)),
        name="route_dest",
    )(counts, jnp.asarray(ltri, BF16), topi, rank)


def _inverse_kernel(bexp_ref, nused_ref, dest_hbm, inv_hbm, dsm0, dsm1, inv_sm, isem, osem, *, t, tm):
    i = pl.program_id(0)
    nt = pl.num_programs(0)
    nblk = inv_sm.shape[0] // MOE_ROWS
    nused = nused_ref[0]
    dsm = (dsm0, dsm1)

    def idx_copy(j, p):
        return pltpu.make_async_copy(dest_hbm.at[j], dsm[p], isem.at[p])

    @pl.when(i == 0)
    def _():
        idx_copy(0, 0).start()

        def init_block(b, carry):
            nxt = jnp.minimum(b + 1, nblk - 1)
            partial = jnp.logical_or(b >= nused - 1, bexp_ref[nxt] != bexp_ref[b])

            @pl.when(partial)
            def _():
                row0 = b * MOE_ROWS
                spare0 = TOP_K * t + (b % 2) * MOE_ROWS

                def init_row(r, c2):
                    inv_sm[row0 + r] = spare0 + r
                    return c2

                lax.fori_loop(0, MOE_ROWS, init_row, 0, unroll=8)

            return carry

        lax.fori_loop(0, nblk, init_block, 0)

    tok0 = i * tm
    for p in range(2):
        @pl.when(i % 2 == p)
        def _():
            @pl.when(i + 1 < nt)
            def _():
                idx_copy(i + 1, 1 - p).start()

            idx_copy(i, p).wait()

            def body(r, carry):
                for k in range(TOP_K):
                    inv_sm[dsm[p][k * tm + r]] = k * t + tok0 + r
                return carry

            lax.fori_loop(0, tm, body, 0, unroll=8)

    @pl.when(i == nt - 1)
    def _():
        out = pltpu.make_async_copy(inv_sm, inv_hbm, osem)
        out.start()
        out.wait()


def _inverse_map(bexp, nused, dest2, nblocks, t):
    nt, ktm = dest2.shape
    grid_spec = pltpu.PrefetchScalarGridSpec(
        num_scalar_prefetch=2,
        grid=(nt,),
        in_specs=[pl.BlockSpec(memory_space=pl.ANY)],
        out_specs=pl.BlockSpec(memory_space=pl.ANY),
        scratch_shapes=[
            pltpu.SMEM((ktm,), I32),
            pltpu.SMEM((ktm,), I32),
            pltpu.SMEM((nblocks * MOE_ROWS,), I32),
            pltpu.SemaphoreType.DMA((2,)),
            pltpu.SemaphoreType.DMA(()),
        ],
    )
    inv = pl.pallas_call(
        functools.partial(_inverse_kernel, t=t, tm=ktm // TOP_K),
        out_shape=jax.ShapeDtypeStruct((nblocks * MOE_ROWS,), I32),
        grid_spec=grid_spec,
        compiler_params=_params(("arbitrary",)),
        name="route_inverse",
    )(bexp, nused, dest2)
    return inv.reshape(nblocks, MOE_ROWS)


def _pack_bf16_pairs(x):
    n = x.shape[1] // 2
    lo = lax.bitcast_convert_type(x[:, :n].astype(BF16).astype(F32), U32)
    hi = lax.bitcast_convert_type(x[:, n:].astype(BF16).astype(F32), U32)
    return hi | (lo >> 16)


def _unpack_bf16_pairs(w):
    lo = lax.bitcast_convert_type(w << 16, F32)
    hi = lax.bitcast_convert_type(w & jnp.uint32(0xFFFF0000), F32)
    return lo, hi


def _expert_kernel(bexp_ref, nused_ref, inv_hbm, h2_hbm, wgu_ref, bgu_ref, wd_ref, bd_ref, y4_hbm,
                   ig0, ig1, is0, is1, xbuf, ybuf, wgu_bf, wd_bf, gisem, sisem, gsem, ssem, *, dff, t):
    i = pl.program_id(0)
    nblk = pl.num_programs(0)
    nused = nused_ref[0]
    rows = xbuf.shape[1]
    half = xbuf.shape[2]
    used = i < nused
    fresh = jnp.logical_or(i == 0, bexp_ref[i] != bexp_ref[jnp.maximum(i - 1, 0)])
    gidx, sidx = (ig0, ig1), (is0, is1)

    def gidx_copy(j, p):
        return pltpu.make_async_copy(inv_hbm.at[j], gidx[p], gisem.at[p])

    def sidx_copy(j, p):
        return pltpu.make_async_copy(inv_hbm.at[j], sidx[p], sisem.at[p])

    def gather_rows(p):
        def body(rr, carry):
            for u in range(2):
                r = rr * 2 + u
                slot_row = gidx[p][r]
                tok = slot_row & (t - 1) if t & (t - 1) == 0 else lax.rem(slot_row, t)
                pltpu.make_async_copy(h2_hbm.at[pl.ds(tok, 1), :], xbuf.at[p, pl.ds(r, 1), :],
                                      gsem.at[p]).start(priority=u)
            return carry

        for rr in range(rows // 2):
            body(rr, 0)

    def gather_wait(p):
        pltpu.make_async_copy(h2_hbm.at[pl.ds(0, rows), :], xbuf.at[p], gsem.at[p]).wait()

    def scatter_rows(p):
        def body(rr, carry):
            for u in range(2):
                r = rr * 2 + u
                pltpu.make_async_copy(ybuf.at[p, pl.ds(r, 1), :], y4_hbm.at[pl.ds(sidx[p][r], 1), :],
                                      ssem.at[p]).start(priority=u)
            return carry

        for rr in range(rows // 2):
            body(rr, 0)

    def scatter_wait(p):
        pltpu.make_async_copy(ybuf.at[p], y4_hbm.at[pl.ds(0, rows), :], ssem.at[p]).wait()

    @pl.when(i == 0)
    def _():
        ybuf[...] = jnp.zeros(ybuf.shape, U32)
        for s in range(2):
            spare = pltpu.make_async_copy(ybuf.at[s], y4_hbm.at[pl.ds(TOP_K * t + s * rows, rows), :], ssem.at[s])
            spare.start()
            spare.wait()
        gidx_copy(0, 0).start()
        sidx_copy(0, 0).start()

        @pl.when(nused > 1)
        def _():
            gidx_copy(1, 1).start()

        gidx_copy(0, 0).wait()
        gather_rows(0)

    @pl.when(jnp.logical_and(used, fresh))
    def _():
        wgu_bf[...] = wgu_ref[...].astype(BF16)
        wd_bf[...] = wd_ref[...].astype(BF16)

    for p in range(2):
        q = 1 - p

        @pl.when(i % 2 == p)
        def _():
            @pl.when(i + 1 < nused)
            def _():
                gidx_copy(i + 1, q).wait()
                gather_rows(q)
                sidx_copy(i + 1, q).start()

            @pl.when(i + 2 < nused)
            def _():
                gidx_copy(i + 2, p).start()

            @pl.when(jnp.logical_and(i >= 2, i - 2 < nused))
            def _():
                scatter_wait(p)

            @pl.when(used)
            def _():
                gather_wait(p)
                lo, hi = _unpack_bf16_pairs(xbuf[p])
                hu = (_dot(lo.astype(BF16), wgu_bf[0:half, :]) + _dot(hi.astype(BF16), wgu_bf[half:2 * half, :])
                      + bgu_ref[...])
                glu = jnp.minimum(hu[:, 0:dff], SWIGLU_LIMIT)
                lin = jnp.clip(hu[:, dff:2 * dff], -SWIGLU_LIMIT, SWIGLU_LIMIT)
                act = glu * _sigmoid(SWIGLU_ALPHA * glu) * (lin + 1.0)
                ybuf[p] = _pack_bf16_pairs(_dot(act.astype(BF16), wd_bf[...]) + bd_ref[...])
                sidx_copy(i, p).wait()
                scatter_rows(p)

            @pl.when(i == nblk - 1)
            def _():
                @pl.when(jnp.logical_and(i >= 1, i - 1 < nused))
                def _():
                    scatter_wait(q)

                @pl.when(used)
                def _():
                    scatter_wait(p)


def _experts(bexp, nused, inv, h2p, w_gu, b_gu4, w_down, b_down4, l, t):
    nblocks = inv.shape[0]
    half = h2p.shape[1]
    d = 2 * half
    dff = w_down.shape[2]

    wsel = lambda i, be, nu: (l, be[jnp.minimum(i, nu[0] - 1)], 0, 0)
    grid_spec = pltpu.PrefetchScalarGridSpec(
        num_scalar_prefetch=2,
        grid=(nblocks,),
        in_specs=[
            pl.BlockSpec(memory_space=pl.ANY),
            pl.BlockSpec(memory_space=pl.ANY),
            pl.BlockSpec((None, None, d, 2 * dff), wsel),
            pl.BlockSpec((None, None, 1, 2 * dff), wsel),
            pl.BlockSpec((None, None, dff, d), wsel),
            pl.BlockSpec((None, None, 1, d), wsel),
        ],
        out_specs=pl.BlockSpec(memory_space=pl.ANY),
        scratch_shapes=[pltpu.SMEM((MOE_ROWS,), I32)] * 4 + [
            pltpu.VMEM((2, MOE_ROWS, half), U32),
            pltpu.VMEM((2, MOE_ROWS, half), U32),
            pltpu.VMEM((d, 2 * dff), BF16),
            pltpu.VMEM((dff, d), BF16),
        ] + [pltpu.SemaphoreType.DMA((2,))] * 4,
    )
    return pl.pallas_call(
        functools.partial(_expert_kernel, dff=dff, t=t),
        out_shape=jax.ShapeDtypeStruct((TOP_K * t + 2 * MOE_ROWS, half), U32),
        grid_spec=grid_spec,
        compiler_params=_params(("arbitrary",)),
        name="moe_experts",
    )(bexp, nused, inv, h2p, w_gu, b_gu4, w_down, b_down4)


def _combine_kernel(*refs, d, final):
    y_refs = refs[:TOP_K]
    gt_ref, x_ref, mod_ref = refs[TOP_K:TOP_K + 3]
    xo_ref = refs[-1]
    half = d // 2
    gt = gt_ref[...]
    acc_lo = acc_hi = None
    for k in range(TOP_K):
        lo, hi = _unpack_bf16_pairs(y_refs[k][...])
        g = gt[:, k:k + 1]
        acc_lo = g * lo if acc_lo is None else acc_lo + g * lo
        acc_hi = g * hi if acc_hi is None else acc_hi + g * hi
    x_lo = x_ref[:, 0:half] + mod_ref[:, 5 * d:5 * d + half] * acc_lo
    x_hi = x_ref[:, half:d] + mod_ref[:, 5 * d + half:6 * d] * acc_hi
    if final:
        gain = refs[TOP_K + 3][...]
        ssq = jnp.sum(x_lo * x_lo, axis=-1, keepdims=True) + jnp.sum(x_hi * x_hi, axis=-1, keepdims=True)
        inv = lax.rsqrt(ssq * (1.0 / d) + EPS)
        x_lo = x_lo * inv * gain[:, 0:half]
        x_hi = x_hi * inv * gain[:, half:d]
    xo_ref[:, 0:half] = x_lo
    xo_ref[:, half:d] = x_hi


def _combine(y4p, gates_t, x2, mod3, l, seq, depth, final_gain=None):
    t, d = x2.shape
    tm = min(PROJ_TILE, seq)
    nb = mod3.shape[0] // depth
    nt = t // tm
    final = final_gain is not None

    def slot(k):
        return pl.BlockSpec((tm, d // 2), lambda i: (k * nt + i, 0))

    extra_specs = [pl.BlockSpec((1, d), lambda i: (0, 0))] if final else []
    extra_args = [final_gain.reshape(1, d)] if final else []
    return pl.pallas_call(
        functools.partial(_combine_kernel, d=d, final=final),
        out_shape=jax.ShapeDtypeStruct((t, d), F32),
        grid=(nt,),
        in_specs=[slot(k) for k in range(TOP_K)] + [
            pl.BlockSpec((tm, TOP_K), lambda i: (i, 0)),
            pl.BlockSpec((tm, d), lambda i: (i, 0)),
            pl.BlockSpec((None, 1, N_MOD * d), lambda i: (l * nb + (i * tm) // seq, 0, 0)),
        ] + extra_specs,
        out_specs=pl.BlockSpec((tm, d), lambda i: (i, 0)),
        compiler_params=_params(("arbitrary",)),
        name="moe_combine",
    )(*([y4p] * TOP_K), gates_t, x2, mod3, *extra_args)


def _final_kernel(x_ref, g_ref, o_ref):
    x = x_ref[...]
    o_ref[...] = x * lax.rsqrt(jnp.mean(x * x, axis=-1, keepdims=True) + EPS) * g_ref[...]


def _final_norm(x2, gain):
    t, d = x2.shape
    tm = min(PROJ_TILE, t)
    return pl.pallas_call(
        _final_kernel,
        out_shape=jax.ShapeDtypeStruct((t, d), F32),
        grid=(t // tm,),
        in_specs=[pl.BlockSpec((tm, d), lambda i: (i, 0)), pl.BlockSpec((1, d), lambda i: (0, 0))],
        out_specs=pl.BlockSpec((tm, d), lambda i: (i, 0)),
        compiler_params=_params(("arbitrary",)),
        name="final_norm",
    )(x2, gain.reshape(1, d))


def _mixer_layer(x2, mod3, tables, lower, consts, p, l, batch, seq):
    ra, rb, rc = tables
    qa, ka2, va2, hg = _inproj(x2, mod3, p["norm1"], p["w_in"], ra, rb, rc, l, seq)
    a = _attention(qa, ka2, va2, p["attn_sink"], p["attn_norm"], l, batch, seq)
    o = _hgrn(hg, lower, p["hg_norm"], consts, l, batch, seq)
    return a, o


SC_WINDOW = 128


def _sc_mesh():
    return plsc.VectorSubcoreMesh(core_axis_name="core", subcore_axis_name="subcore")


def _sc_worker(mesh):
    return lax.axis_index("core") * mesh.num_subcores + lax.axis_index("subcore"), mesh.num_cores * mesh.num_subcores


def _sc_scatter_rows(src, dest_rows, n_out):
    n, width = src.shape
    nk = dest_rows.shape[0]
    mesh = _sc_mesh()

    @pl.kernel(out_type=jax.ShapeDtypeStruct((n_out, width), src.dtype), mesh=mesh,
               scratch_types=[pltpu.VMEM((nk, SC_WINDOW), I32), pltpu.VMEM((SC_WINDOW, width), src.dtype)],
               name="sc_scatter_rows")
    def scatter_kernel(src_hbm, idx_hbm, out_hbm, idx_v, rows_v):
        wid, nw = _sc_worker(mesh)
        per = n // SC_WINDOW // nw

        @pl.loop(0, per)
        def _(j):
            base = (wid * per + j) * SC_WINDOW
            pltpu.sync_copy(idx_hbm.at[:, pl.ds(base, SC_WINDOW)], idx_v)
            pltpu.sync_copy(src_hbm.at[pl.ds(base, SC_WINDOW)], rows_v)
            for k in range(nk):
                pltpu.sync_copy(rows_v, out_hbm.at[idx_v.at[k]])

    return scatter_kernel(src, dest_rows)


def _sc_gather_rows(src, rows):
    n = rows.shape[0]
    width = src.shape[1]
    mesh = _sc_mesh()

    @pl.kernel(out_type=jax.ShapeDtypeStruct((n, width), src.dtype), mesh=mesh,
               scratch_types=[pltpu.VMEM((1, SC_WINDOW), I32), pltpu.VMEM((SC_WINDOW, width), src.dtype)],
               name="sc_gather_rows")
    def gather_kernel(src_hbm, idx_hbm, out_hbm, idx_v, rows_v):
        wid, nw = _sc_worker(mesh)
        per = n // SC_WINDOW // nw

        @pl.loop(0, per)
        def _(j):
            base = (wid * per + j) * SC_WINDOW
            pltpu.sync_copy(idx_hbm.at[:, pl.ds(base, SC_WINDOW)], idx_v)
            pltpu.sync_copy(src_hbm.at[idx_v.at[0]], rows_v)
            pltpu.sync_copy(rows_v, out_hbm.at[pl.ds(base, SC_WINDOW)])

    return gather_kernel(src, rows.reshape(1, n))


def _expert_group_kernel(eb_ref, xs_hbm, wgu_ref, bgu_ref, wd_ref, bd_ref, ys_hbm,
                         xbuf, ybuf, wgu_bf, wd_bf, xsem, ysem, *, dff):
    e = pl.program_id(0)
    ne = pl.num_programs(0)
    b0 = eb_ref[e]
    nb = eb_ref[ne + e]
    rows, half = xbuf.shape[1], xbuf.shape[2]

    def x_copy(j, slot):
        return pltpu.make_async_copy(xs_hbm.at[pl.ds((b0 + j) * rows, rows), :], xbuf.at[slot], xsem.at[slot])

    def y_copy(j, slot):
        return pltpu.make_async_copy(ybuf.at[slot], ys_hbm.at[pl.ds((b0 + j) * rows, rows), :], ysem.at[slot])

    @pl.when(nb > 0)
    def _():
        x_copy(0, 0).start()
        wgu_bf[...] = wgu_ref[...].astype(BF16)
        wd_bf[...] = wd_ref[...].astype(BF16)

        def block(j, carry):
            slot = j % 2

            @pl.when(j + 1 < nb)
            def _():
                x_copy(j + 1, 1 - slot).start()

            x_copy(j, slot).wait()

            @pl.when(j >= 2)
            def _():
                y_copy(j - 2, slot).wait()

            lo, hi = _unpack_bf16_pairs(xbuf[slot])
            hu = (_dot(lo.astype(BF16), wgu_bf[0:half, :]) + _dot(hi.astype(BF16), wgu_bf[half:2 * half, :])
                  + bgu_ref[...])
            glu = jnp.minimum(hu[:, 0:dff], SWIGLU_LIMIT)
            lin = jnp.clip(hu[:, dff:2 * dff], -SWIGLU_LIMIT, SWIGLU_LIMIT)
            act = glu * _sigmoid(SWIGLU_ALPHA * glu) * (lin + 1.0)
            ybuf[slot] = _pack_bf16_pairs(_dot(act.astype(BF16), wd_bf[...]) + bd_ref[...])
            y_copy(j, slot).start()
            return carry

        lax.fori_loop(0, nb, block, 0)

        @pl.when(nb >= 2)
        def _():
            y_copy(nb - 2, nb % 2).wait()

        y_copy(nb - 1, (nb - 1) % 2).wait()


def _block_experts(eblk, xs, w_gu, b_gu4, w_down, b_down4, l):
    cap, half = xs.shape
    d = 2 * half
    dff = w_down.shape[2]
    wsel = lambda e, eb: (l, e, 0, 0)
    grid_spec = pltpu.PrefetchScalarGridSpec(
        num_scalar_prefetch=1,
        grid=(N_EXPERTS,),
        in_specs=[
            pl.BlockSpec(memory_space=pl.ANY),
            pl.BlockSpec((None, None, d, 2 * dff), wsel),
            pl.BlockSpec((None, None, 1, 2 * dff), wsel),
            pl.BlockSpec((None, None, dff, d), wsel),
            pl.BlockSpec((None, None, 1, d), wsel),
        ],
        out_specs=pl.BlockSpec(memory_space=pl.ANY),
        scratch_shapes=[
            pltpu.VMEM((2, MOE_ROWS, half), U32),
            pltpu.VMEM((2, MOE_ROWS, half), U32),
            pltpu.VMEM((d, 2 * dff), BF16),
            pltpu.VMEM((dff, d), BF16),
            pltpu.SemaphoreType.DMA((2,)),
            pltpu.SemaphoreType.DMA((2,)),
        ],
    )
    return pl.pallas_call(
        functools.partial(_expert_group_kernel, dff=dff),
        out_shape=jax.ShapeDtypeStruct((cap, half), U32),
        grid_spec=grid_spec,
        compiler_params=_params(("arbitrary",)),
        name="moe_expert_groups",
    )(eblk, xs, w_gu, b_gu4, w_down, b_down4)


def _moe_layer_sc(a, o, x2, mod3, p, l, seq, depth, final_gain=None):
    t, d = x2.shape
    xn, h2p, topi, gates = _outproj(a, o, x2, mod3, p["norm2"], p["w_out"], p["w_router_t"], p["b_router"], l, seq)
    rank, counts = _ranks(topi)
    nblocks = (t * TOP_K + N_EXPERTS * MOE_ROWS) // MOE_ROWS
    dest, _, _, eblk = _destinations(counts, topi, rank, nblocks)
    xs = _sc_scatter_rows(h2p, dest, nblocks * MOE_ROWS)
    ys = _block_experts(eblk[:, :, 0].reshape(-1), xs, p["w_gu"], p["b_gu"], p["w_down"], p["b_down"], l)
    y4p = _sc_gather_rows(ys, dest.reshape(-1))
    return _combine(y4p, gates.T, xn, mod3, l, seq, depth, final_gain)


def _moe_layer(a, o, x2, mod3, p, l, seq, depth):
    t, d = x2.shape
    xn, h2p, topi, gates = _outproj(a, o, x2, mod3, p["norm2"], p["w_out"], p["w_router_t"], p["b_router"], l, seq)
    rank, counts = _ranks(topi)
    nblocks = (t * TOP_K + N_EXPERTS * MOE_ROWS) // MOE_ROWS
    dest, bexp, nused, _ = _destinations(counts, topi, rank, nblocks)
    tm = min(TOK_TILE, t)
    dest2 = dest.reshape(TOP_K, t // tm, tm).transpose(1, 0, 2).reshape(t // tm, TOP_K * tm)
    bexp, nused = bexp.reshape(-1), nused.reshape(-1)
    inv = _inverse_map(bexp, nused, dest2, nblocks, t)
    y4p = _experts(bexp, nused, inv, h2p, p["w_gu"], p["b_gu"], p["w_down"], p["b_down"], l, t)
    return _combine(y4p, gates.T, xn, mod3, l, seq, depth)


def kernel(x, c, positions, w_ada, b_ada, norm1, w_in, attn_sink, attn_norm, hg_lb_logits, hg_norm, w_out, norm2,
           w_router, b_router, w_gu, b_gu, w_down, b_down, final_norm):
    batch, seq, d = x.shape
    depth = w_ada.shape[0]
    t = batch * seq
    p = {
        "norm1": norm1.reshape(depth, 1, d),
        "w_in": w_in.astype(BF16),
        "attn_sink": attn_sink.astype(F32),
        "attn_norm": attn_norm.reshape(depth, 1, ATT_WIDTH),
        "hg_norm": hg_norm.reshape(depth, 1, HG_HEAD_DIM),
        "w_out": w_out.astype(BF16),
        "norm2": norm2.reshape(depth, 1, d),
        "w_router_t": jnp.swapaxes(w_router, 1, 2),
        "b_router": b_router.reshape(depth, N_EXPERTS, 1),
        "w_gu": w_gu,
        "b_gu": b_gu.reshape(depth, N_EXPERTS, 1, b_gu.shape[-1]),
        "w_down": w_down,
        "b_down": b_down.reshape(depth, N_EXPERTS, 1, d),
    }
    mod3 = _ada_all(c, w_ada, b_ada).reshape(depth * batch, 1, N_MOD * d)
    lower = _lower_bounds(hg_lb_logits)
    tables = _rope_tables(positions)
    consts = _hgrn_constants()
    consts = (jnp.asarray(consts[0], BF16), jnp.asarray(consts[1], BF16), jnp.asarray(consts[2]), jnp.asarray(consts[3]))
    x2 = x.reshape(t, d)
    for l in range(depth):
        a, o = _mixer_layer(x2, mod3, tables, lower, consts, p, l, batch, seq)
        x2 = _moe_layer_sc(a, o, x2, mod3, p, l, seq, depth, final_norm if l == depth - 1 else None)
    return x2.reshape(batch, seq, d)
```

```python
import functools

import numpy as np
import jax
import jax.numpy as jnp
from jax import lax
from jax.experimental import pallas as pl
from jax.experimental.pallas import tpu as pltpu
from jax.experimental.pallas import tpu_sc as plsc

F32 = jnp.float32
BF16 = jnp.bfloat16
I32 = jnp.int32
U32 = jnp.uint32

ATT_HEADS = 8
ATT_KV_HEADS = 2
ATT_HEAD_DIM = 64
ATT_WIDTH = ATT_HEADS * ATT_HEAD_DIM
KV_WIDTH = ATT_KV_HEADS * ATT_HEAD_DIM
WINDOW = 128
ATT_BLOCK = 128
ROPE_THETA = 500000.0
ROPE_DIM = ATT_HEAD_DIM // 4
HG_HEADS = 4
HG_HEAD_DIM = 128
HG_WIDTH = HG_HEADS * HG_HEAD_DIM
N_EXPERTS = 32
TOP_K = 4
SWIGLU_ALPHA = 1.702
SWIGLU_LIMIT = 7.0
N_MOD = 6
EPS = 1e-6
NEG_INF = -1e30
LB_FLOOR = 1e-30

LANES = 128
HG_CHUNK = 64
HG_LEVELS = 6
HG_UNROLL = 8
MOE_SHIFT = 8
MOE_ROWS = 1 << MOE_SHIFT
EXPERT_GROUP = 2
TOK_TILE = 256
PROJ_TILE = 512
RANK_TILE = 512
VMEM_LIMIT = 56 * 1024 * 1024


def _dot(a, b):
    return jnp.dot(a, b, preferred_element_type=F32)


def _dot_nt(a, b):
    return lax.dot_general(a, b, (((1,), (1,)), ((), ())), preferred_element_type=F32)


def _dot_tn(a, b):
    return lax.dot_general(a, b, (((0,), (0,)), ((), ())), preferred_element_type=F32)


def _split3(x):
    hi = x.astype(BF16)
    r1 = x - hi.astype(F32)
    mid = r1.astype(BF16)
    lo = (r1 - mid.astype(F32)).astype(BF16)
    return hi, mid, lo


def _dot_exact_lhs(m_bf16, x):
    hi, mid, lo = _split3(x)
    return _dot(m_bf16, hi) + _dot(m_bf16, mid) + _dot(m_bf16, lo)


def _dot_f32_nt(a, b):
    ah, am, _ = _split3(a)
    bh, bm, _ = _split3(b)
    return _dot_nt(ah, bh) + _dot_nt(ah, bm) + _dot_nt(am, bh)


def _dot_f32(a, b):
    ah, am, al = _split3(a)
    bh, bm, bl = _split3(b)
    return (_dot(ah, bh) + _dot(ah, bm) + _dot(am, bh)
            + _dot(ah, bl) + _dot(am, bm) + _dot(al, bh))


def _sigmoid(x):
    return 1.0 / (1.0 + jnp.exp(-x))


def _params(sem=None):
    return pltpu.CompilerParams(dimension_semantics=sem, vmem_limit_bytes=VMEM_LIMIT)


def _ada_kernel(c_ref, w_ref, b_ref, o_ref):
    c = c_ref[...]
    cond = c * _sigmoid(c)
    o_ref[...] = _dot_f32(cond, w_ref[...]) + b_ref[...]


def _ada_all(c, w_ada, b_ada):
    depth, d, n = w_ada.shape
    b = c.shape[0]
    nt = n // d
    return pl.pallas_call(
        _ada_kernel,
        out_shape=jax.ShapeDtypeStruct((depth, b, n), F32),
        grid=(depth, nt),
        in_specs=[
            pl.BlockSpec((b, d), lambda l, j: (0, 0)),
            pl.BlockSpec((None, d, d), lambda l, j: (l, 0, j)),
            pl.BlockSpec((None, 1, d), lambda l, j: (l, 0, j)),
        ],
        out_specs=pl.BlockSpec((None, b, d), lambda l, j: (l, 0, j)),
        compiler_params=_params(("arbitrary", "arbitrary")),
        name="ada_mod",
    )(c, w_ada, b_ada.reshape(depth, 1, n))


def _lb_kernel(x_ref, o_ref):
    depth = x_ref.shape[0]
    xs = [x_ref[l] for l in range(depth)]
    m = xs[0]
    for l in range(1, depth):
        m = jnp.maximum(m, xs[l])
    es = [jnp.exp(v - m) for v in xs]
    den = es[0]
    for l in range(1, depth):
        den = den + es[l]
    ps = [e / den for e in es]
    run = ps[0]
    o_ref[0] = run - ps[0]
    for l in range(1, depth):
        run = run + ps[l]
        o_ref[l] = run - ps[0]


def _lower_bounds(hg_lb_logits):
    return pl.pallas_call(
        _lb_kernel,
        out_shape=jax.ShapeDtypeStruct(hg_lb_logits.shape, F32),
        name="hg_lower_bounds",
    )(hg_lb_logits.astype(F32))


def _rope_kernel(pos_ref, invf_ref, a_ref, b_ref, c_ref):
    pos = pos_ref[...].astype(F32)
    ang = pos * invf_ref[...]
    cs = jnp.cos(ang)
    sn = jnp.sin(ang)
    lane = lax.broadcasted_iota(I32, ang.shape, 1) & (ATT_HEAD_DIM - 1)
    half = ROPE_DIM // 2
    first = lane < half
    second = (lane >= half) & (lane < ROPE_DIM)
    a_ref[...] = jnp.where(first | second, cs, 1.0)
    b_ref[...] = jnp.where(first, -sn, 0.0)
    c_ref[...] = jnp.where(second, sn, 0.0)


def _rope_tables(positions):
    t = positions.size
    half = ROPE_DIM // 2
    inv = (np.float32(ROPE_THETA) ** (-(np.arange(half, dtype=np.float32) * np.float32(2.0) / np.float32(ROPE_DIM)))).astype(np.float32)
    lane = np.arange(LANES) % ATT_HEAD_DIM
    pat = np.where(lane < ROPE_DIM, inv[lane % half], 0.0).astype(np.float32).reshape(1, LANES)
    tm = min(t, 2048)
    shp = jax.ShapeDtypeStruct((t, LANES), F32)
    spec = pl.BlockSpec((tm, LANES), lambda i: (i, 0))
    return pl.pallas_call(
        _rope_kernel,
        out_shape=(shp, shp, shp),
        grid=(t // tm,),
        in_specs=[pl.BlockSpec((tm, 1), lambda i: (i, 0)), pl.BlockSpec((1, LANES), lambda i: (0, 0))],
        out_specs=(spec, spec, spec),
        compiler_params=_params(("arbitrary",)),
        name="rope_tables",
    )(positions.reshape(t, 1).astype(I32), jnp.asarray(pat))


def _rms_mod(x, gain, scale, shift):
    ms = jnp.mean(x * x, axis=-1, keepdims=True)
    return (x * lax.rsqrt(ms + EPS) * gain) * (1.0 + scale) + shift


def _rope_apply(x, a, b, c):
    half = ROPE_DIM // 2
    return x * a + pltpu.roll(x, LANES - half, 1) * b + pltpu.roll(x, half, 1) * c


def _inproj_kernel(x_ref, mod_ref, n1_ref, w_ref, ra_ref, rb_ref, rc_ref, qa_ref, ka_ref, va_ref, hg_ref, *, d):
    mod = mod_ref[...]
    h = _rms_mod(x_ref[...], n1_ref[...], mod[:, d:2 * d], mod[:, 0:d]).astype(BF16)
    a, b, c = ra_ref[...], rb_ref[...], rc_ref[...]
    kvw = ATT_WIDTH + 2 * KV_WIDTH
    pa = _dot(h, w_ref[:, 0:kvw])
    scale = ATT_HEAD_DIM ** -0.5
    for g in range(ATT_WIDTH // LANES):
        qg = _rope_apply(pa[:, g * LANES:(g + 1) * LANES], a, b, c)
        qa_ref[:, g * LANES:(g + 1) * LANES] = (qg * scale).astype(BF16)
    k = _rope_apply(pa[:, ATT_WIDTH:ATT_WIDTH + KV_WIDTH], a, b, c)
    v = pa[:, ATT_WIDTH + KV_WIDTH:kvw]
    ka_ref[:, 0:LANES] = k.astype(BF16)
    ka_ref[:, LANES:2 * LANES] = pltpu.roll(k, ATT_HEAD_DIM, 1).astype(BF16)
    va_ref[:, 0:LANES] = v.astype(BF16)
    va_ref[:, LANES:2 * LANES] = pltpu.roll(v, ATT_HEAD_DIM, 1).astype(BF16)
    for g in range(5):
        lo = kvw + g * HG_WIDTH
        hg_ref[:, g * HG_WIDTH:(g + 1) * HG_WIDTH] = _dot(h, w_ref[:, lo:lo + HG_WIDTH])


def _inproj(x2, mod3, norm1, w_in_bf, ra, rb, rc, l, seq):
    t, d = x2.shape
    n_in = w_in_bf.shape[-1]
    tm = min(PROJ_TILE, seq)
    nb = mod3.shape[0] // norm1.shape[0]
    tok = lambda i: (i, 0)
    return pl.pallas_call(
        functools.partial(_inproj_kernel, d=d),
        out_shape=(
            jax.ShapeDtypeStruct((t, ATT_WIDTH), BF16),
            jax.ShapeDtypeStruct((t, 2 * KV_WIDTH), BF16),
            jax.ShapeDtypeStruct((t, 2 * KV_WIDTH), BF16),
            jax.ShapeDtypeStruct((t, 5 * HG_WIDTH), F32),
        ),
        grid=(t // tm,),
        in_specs=[
            pl.BlockSpec((tm, d), tok),
            pl.BlockSpec((None, 1, N_MOD * d), lambda i: (l * nb + (i * tm) // seq, 0, 0)),
            pl.BlockSpec((None, 1, d), lambda i: (l, 0, 0)),
            pl.BlockSpec((None, d, n_in), lambda i: (l, 0, 0)),
            pl.BlockSpec((tm, LANES), tok),
            pl.BlockSpec((tm, LANES), tok),
            pl.BlockSpec((tm, LANES), tok),
        ],
        out_specs=(
            pl.BlockSpec((tm, ATT_WIDTH), tok),
            pl.BlockSpec((tm, 2 * KV_WIDTH), tok),
            pl.BlockSpec((tm, 2 * KV_WIDTH), tok),
            pl.BlockSpec((tm, 5 * HG_WIDTH), tok),
        ),
        compiler_params=_params(("arbitrary",)),
        name="in_proj",
    )(x2, mod3, norm1, w_in_bf, ra, rb, rc)


def _attn_kernel(sink_ref, q_ref, kp_ref, kc_ref, kn_ref, vp_ref, vc_ref, vn_ref, gain_ref, o_ref, *, l, seq):
    n = pl.program_id(1)
    blk = ATT_BLOCK
    k2 = jnp.concatenate([kp_ref[...], kc_ref[...], kn_ref[...]], axis=0)
    v2 = jnp.concatenate([vp_ref[...], vc_ref[...], vn_ref[...]], axis=0)
    lane = lax.broadcasted_iota(I32, (3 * blk, LANES), 1)
    lo_half = lane < ATT_HEAD_DIM
    zero = jnp.zeros((3 * blk, LANES), BF16)
    ka, kb = k2[:, 0:LANES], k2[:, LANES:2 * LANES]
    va, vb = v2[:, 0:LANES], v2[:, LANES:2 * LANES]
    kz = [[jnp.where(lo_half, ka, zero), jnp.where(lo_half, zero, kb)],
          [jnp.where(lo_half, kb, zero), jnp.where(lo_half, zero, ka)]]
    vz = [[jnp.where(lo_half, va, zero), jnp.where(lo_half, zero, vb)],
          [jnp.where(lo_half, vb, zero), jnp.where(lo_half, zero, va)]]
    qpos = n * blk + lax.broadcasted_iota(I32, (blk, 3 * blk), 0)
    kpos = (n - 1) * blk + lax.broadcasted_iota(I32, (blk, 3 * blk), 1)
    valid = (jnp.abs(qpos - kpos) <= WINDOW) & (kpos >= 0) & (kpos < seq)
    valid2 = jnp.concatenate([valid, valid], axis=0)
    upper = lax.broadcasted_iota(I32, (2 * blk, 1), 0) < blk
    outs = []
    for j in range(ATT_KV_HEADS):
        qs = jnp.concatenate([q_ref[:, 2 * j * LANES:(2 * j + 1) * LANES],
                              q_ref[:, (2 * j + 1) * LANES:(2 * j + 2) * LANES]], axis=0)
        acc = None
        for half in range(2):
            s = _dot_nt(qs, kz[j][half])
            s = jnp.where(valid2, s, NEG_INF)
            sink = jnp.where(upper, sink_ref[l, 4 * j + half], sink_ref[l, 4 * j + 2 + half])
            mx = jnp.maximum(jnp.max(s, axis=-1, keepdims=True), sink)
            p = jnp.exp(s - mx)
            den = jnp.sum(p, axis=-1, keepdims=True) + jnp.exp(sink - mx)
            p = (p * (1.0 / den)).astype(BF16)
            pv = _dot(p, vz[j][half])
            acc = pv if acc is None else acc + pv
        outs.append(acc[0:blk])
        outs.append(acc[blk:2 * blk])
    o = jnp.concatenate(outs, axis=-1)
    ms = jnp.mean(o * o, axis=-1, keepdims=True)
    o_ref[...] = (o * lax.rsqrt(ms + EPS) * gain_ref[...]).astype(BF16)


def _attention(qa, ka2, va2, attn_sink, attn_norm3, l, batch, seq):
    t = qa.shape[0]
    blk = ATT_BLOCK
    nb = seq // blk
    cur = lambda b, n: (b * nb + n, 0)
    prev = lambda b, n: (b * nb + jnp.maximum(n - 1, 0), 0)
    nxt = lambda b, n: (b * nb + jnp.minimum(n + 1, nb - 1), 0)
    kvspec = lambda f: pl.BlockSpec((blk, 2 * KV_WIDTH), f)
    return pl.pallas_call(
        functools.partial(_attn_kernel, l=l, seq=seq),
        out_shape=jax.ShapeDtypeStruct((t, ATT_WIDTH), BF16),
        grid=(batch, nb),
        in_specs=[
            pl.BlockSpec(memory_space=pltpu.SMEM),
            pl.BlockSpec((blk, ATT_WIDTH), cur),
            kvspec(prev), kvspec(cur), kvspec(nxt),
            kvspec(prev), kvspec(cur), kvspec(nxt),
            pl.BlockSpec((None, 1, ATT_WIDTH), lambda b, n: (l, 0, 0)),
        ],
        out_specs=pl.BlockSpec((blk, ATT_WIDTH), cur),
        compiler_params=_params(("arbitrary", "arbitrary")),
        name="window_attn",
    )(attn_sink, qa, ka2, ka2, ka2, va2, va2, va2, attn_norm3)


def _hgrn_constants():
    c, nl = HG_CHUNK, HG_LEVELS
    r = np.arange(c)
    u = r[None, :]
    rr = r[:, None]
    mats, masks = [], []
    for lev in range(nl):
        m = 1 << lev
        parent = r // (2 * m)
        anchor = parent * 2 * m + m
        upper = r >= anchor
        aa = anchor[:, None]
        mats.append(np.where(upper[:, None], (u > aa) & (u <= rr), (u > rr) & (u <= aa)))
        masks.append((parent[:, None] == parent[None, :]) & upper[:, None] & (~upper)[None, :])
    masks.append(np.eye(c, dtype=bool))
    mats.append(u <= rr)
    mats.append(u > rr)
    mf = np.concatenate(mats, axis=0).astype(np.float32)
    mb = np.concatenate([mt[::-1, ::-1] for mt in mats], axis=0).astype(np.float32)
    kf = np.stack(masks).astype(np.float32)
    kb = np.stack([mk[::-1, ::-1] for mk in masks]).astype(np.float32)
    return mf, mb, kf, kb


def _hgrn_kernel(q_ref, ff_ref, fb_ref, i_ref, g_ref, lb_ref, gn_ref, mf_ref, mb_ref, kf_ref, kb_ref,
                 o_ref, of_scr, ob_scr, st_scr, *, seq):
    c, nl = HG_CHUNK, HG_LEVELS
    nc = seq // c
    lb = lb_ref[...]
    st_scr[...] = jnp.zeros(st_scr.shape, F32)

    unroll = HG_UNROLL if nc % HG_UNROLL == 0 else 1
    dirs = (
        dict(f_ref=ff_ref, lbrow=lb[0:1, :], m_ref=mf_ref, k_ref=kf_ref, o_scr=of_scr, last_row=c - 1, d=0),
        dict(f_ref=fb_ref, lbrow=lb[1:2, :], m_ref=mb_ref, k_ref=kb_ref, o_scr=ob_scr, last_row=0, d=1),
    )

    def body(i, carry):
        work = []
        for u in range(unroll):
            cf = i * unroll + u
            work.append((dirs[0], pl.ds(pl.multiple_of(cf * c, c), c)))
            work.append((dirs[1], pl.ds(pl.multiple_of((nc - 1 - cf) * c, c), c)))
        gates = []
        for dr, rows in work:
            lbf = jnp.maximum(dr["lbrow"], LB_FLOOR)
            oml = 1.0 - dr["lbrow"]
            f = dr["f_ref"][rows, :]
            e = jnp.exp(-jnp.abs(f))
            r = 1.0 / (1.0 + e)
            er = e * r
            pos = f >= 0.0
            logf = jnp.log(lbf + oml * jnp.where(pos, r, er))
            kk = oml * jnp.where(pos, er, r)
            qh = q_ref[rows, :]
            gates.append((logf, kk, qh * _sigmoid(qh), i_ref[rows, :].astype(BF16)))
        gall = [jnp.exp(_dot_exact_lhs(dr["m_ref"][...], g[0])) for (dr, _), g in zip(work, gates)]
        amat = [dr["k_ref"][nl] * _dot_nt(g[2].astype(BF16), g[1].astype(BF16)) for (dr, _), g in zip(work, gates)]
        for lev in range(nl):
            for j, ((dr, _), g) in enumerate(zip(work, gates)):
                gl = gall[j][lev * c:(lev + 1) * c, :]
                amat[j] = amat[j] + dr["k_ref"][lev] * _dot_nt((g[2] * gl).astype(BF16), (g[1] * gl).astype(BF16))
        intra = [_dot(amat[j].astype(BF16), g[3]) for j, g in enumerate(gates)]
        upd = [_dot_tn(g[3], (g[1] * gall[j][(nl + 1) * c:(nl + 2) * c, :]).astype(BF16)) for j, g in enumerate(gates)]
        st = [st_scr[0], st_scr[1]]
        for j, ((dr, rows), g) in enumerate(zip(work, gates)):
            eb = gall[j][nl * c:(nl + 1) * c, :]
            d = dr["d"]
            dr["o_scr"][rows, :] = _dot_nt((g[2] * eb).astype(BF16), st[d].astype(BF16)) + intra[j]
            st[d] = st[d] * eb[dr["last_row"]:dr["last_row"] + 1, :] + upd[j]
        st_scr[0] = st[0]
        st_scr[1] = st[1]
        return carry

    lax.fori_loop(0, nc // unroll, body, 0)

    ep = min(256, seq)
    gn = gn_ref[...]

    def epilogue(j, carry):
        rows = pl.ds(pl.multiple_of(j * ep, ep), ep)
        o = of_scr[rows, :] + ob_scr[rows, :]
        y = o * lax.rsqrt(jnp.mean(o * o, axis=-1, keepdims=True) + EPS) * gn
        g = g_ref[rows, :]
        o_ref[rows, :] = (y * (g * _sigmoid(g))).astype(BF16)
        return carry

    lax.fori_loop(0, seq // ep, epilogue, 0)


def _hgrn(hg, lower, hg_norm3, consts, l, batch, seq):
    t = hg.shape[0]
    mf, mb, kf, kb = consts
    hd = HG_HEAD_DIM

    def col(g):
        return pl.BlockSpec((seq, hd), lambda b, h: (b, g * HG_HEADS + h))

    full2 = lambda a: pl.BlockSpec(a.shape, lambda b, h: (0, 0))
    full3 = lambda a: pl.BlockSpec(a.shape, lambda b, h: (0, 0, 0))
    return pl.pallas_call(
        functools.partial(_hgrn_kernel, seq=seq),
        out_shape=jax.ShapeDtypeStruct((t, HG_WIDTH), BF16),
        grid=(batch, HG_HEADS),
        in_specs=[
            col(0), col(1), col(2), col(3), col(4),
            pl.BlockSpec((None, 2, hd), lambda b, h: (l, 0, h)),
            pl.BlockSpec((None, 1, hd), lambda b, h: (l, 0, 0)),
            full2(mf), full2(mb), full3(kf), full3(kb),
        ],
        out_specs=pl.BlockSpec((seq, hd), lambda b, h: (b, h)),
        scratch_shapes=[
            pltpu.VMEM((seq, hd), F32),
            pltpu.VMEM((seq, hd), F32),
            pltpu.VMEM((2, hd, hd), F32),
        ],
        compiler_params=_params(("arbitrary", "arbitrary")),
        name="hgrn2_scan",
    )(hg, hg, hg, hg, hg, lower, hg_norm3, mf, mb, kf, kb)


def _outproj_kernel(a_ref, o_ref, x_ref, mod_ref, n2_ref, w_ref, wr_ref, br_ref,
                    xo_ref, h2_ref, ti_ref, gt_ref, *, d):
    mod = mod_ref[...]
    y = _dot(a_ref[...], w_ref[0:ATT_WIDTH, :]) + _dot(o_ref[...], w_ref[ATT_WIDTH:ATT_WIDTH + HG_WIDTH, :])
    xn = x_ref[...] + mod[:, 2 * d:3 * d] * y
    xo_ref[...] = xn
    h2 = _rms_mod(xn, n2_ref[...], mod[:, 4 * d:5 * d], mod[:, 3 * d:4 * d])
    h2_ref[...] = _pack_bf16_pairs(h2)
    lg = _dot_f32_nt(wr_ref[...], h2) + br_ref[...]
    eidx = lax.broadcasted_iota(I32, lg.shape, 0)
    vals, idxs = [], []
    for _ in range(TOP_K):
        mx = jnp.max(lg, axis=0, keepdims=True)
        sel = jnp.min(jnp.where(lg == mx, eidx, N_EXPERTS), axis=0, keepdims=True)
        vals.append(mx)
        idxs.append(sel)
        lg = jnp.where(eidx == sel, -jnp.inf, lg)
    ex = [jnp.exp(v - vals[0]) for v in vals]
    den = ex[0]
    for e in ex[1:]:
        den = den + e
    inv = 1.0 / den
    ti_ref[...] = jnp.concatenate(idxs, axis=0)
    gt_ref[...] = jnp.concatenate([e * inv for e in ex], axis=0)


def _outproj(a, o, x2, mod3, norm2, w_out_bf, w_router_t, b_router3, l, seq):
    t, d = x2.shape
    tm = min(PROJ_TILE, seq)
    nb = mod3.shape[0] // norm2.shape[0]
    tok = lambda i: (i, 0)
    lane_tok = lambda i: (0, i)
    return pl.pallas_call(
        functools.partial(_outproj_kernel, d=d),
        out_shape=(
            jax.ShapeDtypeStruct((t, d), F32),
            jax.ShapeDtypeStruct((t, d // 2), U32),
            jax.ShapeDtypeStruct((TOP_K, t), I32),
            jax.ShapeDtypeStruct((TOP_K, t), F32),
        ),
        grid=(t // tm,),
        in_specs=[
            pl.BlockSpec((tm, ATT_WIDTH), tok),
            pl.BlockSpec((tm, HG_WIDTH), tok),
            pl.BlockSpec((tm, d), tok),
            pl.BlockSpec((None, 1, N_MOD * d), lambda i: (l * nb + (i * tm) // seq, 0, 0)),
            pl.BlockSpec((None, 1, d), lambda i: (l, 0, 0)),
            pl.BlockSpec((None, ATT_WIDTH + HG_WIDTH, d), lambda i: (l, 0, 0)),
            pl.BlockSpec((None, N_EXPERTS, d), lambda i: (l, 0, 0)),
            pl.BlockSpec((None, N_EXPERTS, 1), lambda i: (l, 0, 0)),
        ],
        out_specs=(
            pl.BlockSpec((tm, d), tok),
            pl.BlockSpec((tm, d // 2), tok),
            pl.BlockSpec((TOP_K, tm), lane_tok),
            pl.BlockSpec((TOP_K, tm), lane_tok),
        ),
        compiler_params=_params(("arbitrary",)),
        name="out_proj_router",
    )(a, o, x2, mod3, norm2, w_out_bf, w_router_t, b_router3)


def _rank_kernel(ti_ref, tri_ref, rank_ref, cnt_ref, carry_scr):
    @pl.when(pl.program_id(0) == 0)
    def _():
        carry_scr[...] = jnp.zeros(carry_scr.shape, F32)

    ti = ti_ref[...]
    tl = ti.shape[1]
    eidx = lax.broadcasted_iota(I32, (N_EXPERTS, tl), 0)
    carry = carry_scr[...]
    rows = []
    for k in range(TOP_K):
        oh = eidx == ti[k:k + 1, :]
        ohf = jnp.where(oh, 1.0, 0.0)
        pre = _dot(ohf.astype(BF16), tri_ref[...])
        rows.append(jnp.sum(jnp.where(oh, carry + pre, 0.0), axis=0, keepdims=True))
        carry = carry + jnp.sum(ohf, axis=1, keepdims=True)
    carry_scr[...] = carry
    rank_ref[...] = jnp.concatenate(rows, axis=0).astype(I32)
    cnt_ref[...] = jnp.broadcast_to(carry, cnt_ref.shape)


def _ranks(topi):
    k, t = topi.shape
    tl = min(RANK_TILE, t)
    tri = np.triu(np.ones((tl, tl), np.float32), 1)
    return pl.pallas_call(
        _rank_kernel,
        out_shape=(jax.ShapeDtypeStruct((k, t), I32), jax.ShapeDtypeStruct((N_EXPERTS, LANES), F32)),
        grid=(t // tl,),
        in_specs=[pl.BlockSpec((k, tl), lambda i: (0, i)), pl.BlockSpec((tl, tl), lambda i: (0, 0))],
        out_specs=(pl.BlockSpec((k, tl), lambda i: (0, i)), pl.BlockSpec((N_EXPERTS, LANES), lambda i: (0, 0))),
        scratch_shapes=[pltpu.VMEM((N_EXPERTS, 1), F32)],
        compiler_params=_params(("arbitrary",)),
        name="route_rank",
    )(topi, jnp.asarray(tri, BF16))


def _dest_kernel(cnt_ref, ltri_ref, ti_ref, rank_ref, dest_ref, bexp_ref, nused_ref, eblk_ref):
    cnt = cnt_ref[...]
    nblk = jnp.floor((cnt + (MOE_ROWS - 1)) * (1.0 / MOE_ROWS))
    pstart_b = _dot(ltri_ref[...], nblk.astype(BF16))
    pend_b = pstart_b + nblk
    pstart = (pstart_b[:, 0:1] * MOE_ROWS).astype(I32)
    ti = ti_ref[...]
    tl = ti.shape[1]
    eidx = lax.broadcasted_iota(I32, (N_EXPERTS, tl), 0)
    rows = []
    for k in range(TOP_K):
        oh = eidx == ti[k:k + 1, :]
        rows.append(jnp.sum(jnp.where(oh, pstart, 0), axis=0, keepdims=True))
    dest_ref[...] = jnp.concatenate(rows, axis=0) + rank_ref[...]
    nb = bexp_ref.shape[1]
    bi = lax.broadcasted_iota(I32, (N_EXPERTS, nb), 1).astype(F32)
    be = jnp.sum(jnp.where(pend_b[:, 0:1] <= bi, 1, 0), axis=0, keepdims=True)
    bexp_ref[...] = jnp.minimum(be, N_EXPERTS - 1).astype(I32)
    nused_ref[...] = pend_b[N_EXPERTS - 1:N_EXPERTS, :].astype(I32)
    eblk_ref[0] = pstart_b.astype(I32)
    eblk_ref[1] = nblk.astype(I32)


def _destinations(counts, topi, rank, nblocks):
    k, t = topi.shape
    tl = min(2048, t)
    nbp = -(-nblocks // LANES) * LANES
    ltri = np.tril(np.ones((N_EXPERTS, N_EXPERTS), np.float32), -1)
    return pl.pallas_call(
        _dest_kernel,
        out_shape=(
            jax.ShapeDtypeStruct((k, t), I32),
            jax.ShapeDtypeStruct((1, nbp), I32),
            jax.ShapeDtypeStruct((1, LANES), I32),
            jax.ShapeDtypeStruct((2, N_EXPERTS, LANES), I32),
        ),
        grid=(t // tl,),
        in_specs=[
            pl.BlockSpec((N_EXPERTS, LANES), lambda i: (0, 0)),
            pl.BlockSpec((N_EXPERTS, N_EXPERTS), lambda i: (0, 0)),
            pl.BlockSpec((k, tl), lambda i: (0, i)),
            pl.BlockSpec((k, tl), lambda i: (0, i)),
        ],
        out_specs=(
            pl.BlockSpec((k, tl), lambda i: (0, i)),
            pl.BlockSpec((1, nbp), lambda i: (0, 0)),
            pl.BlockSpec((1, LANES), lambda i: (0, 0)),
            pl.BlockSpec((2, N_EXPERTS, LANES), lambda i: (0, 0, 0)),
        ),
        compiler_params=_params(("arbitrary",)),
        name="route_dest",
    )(counts, jnp.asarray(ltri, BF16), topi, rank)


def _inverse_kernel(bexp_ref, nused_ref, dest_hbm, inv_hbm, dsm0, dsm1, inv_sm, isem, osem, *, t, tm):
    i = pl.program_id(0)
    nt = pl.num_programs(0)
    nblk = inv_sm.shape[0] // MOE_ROWS
    nused = nused_ref[0]
    dsm = (dsm0, dsm1)

    def idx_copy(j, p):
        return pltpu.make_async_copy(dest_hbm.at[j], dsm[p], isem.at[p])

    @pl.when(i == 0)
    def _():
        idx_copy(0, 0).start()

        def init_block(b, carry):
            nxt = jnp.minimum(b + 1, nblk - 1)
            partial = jnp.logical_or(b >= nused - 1, bexp_ref[nxt] != bexp_ref[b])

            @pl.when(partial)
            def _():
                row0 = b * MOE_ROWS
                spare0 = TOP_K * t + (b % 2) * MOE_ROWS

                def init_row(r, c2):
                    inv_sm[row0 + r] = spare0 + r
                    return c2

                lax.fori_loop(0, MOE_ROWS, init_row, 0, unroll=8)

            return carry

        lax.fori_loop(0, nblk, init_block, 0)

    tok0 = i * tm
    for p in range(2):
        @pl.when(i % 2 == p)
        def _():
            @pl.when(i + 1 < nt)
            def _():
                idx_copy(i + 1, 1 - p).start()

            idx_copy(i, p).wait()

            def body(r, carry):
                for k in range(TOP_K):
                    inv_sm[dsm[p][k * tm + r]] = k * t + tok0 + r
                return carry

            lax.fori_loop(0, tm, body, 0, unroll=8)

    @pl.when(i == nt - 1)
    def _():
        out = pltpu.make_async_copy(inv_sm, inv_hbm, osem)
        out.start()
        out.wait()


def _inverse_map(bexp, nused, dest2, nblocks, t):
    nt, ktm = dest2.shape
    grid_spec = pltpu.PrefetchScalarGridSpec(
        num_scalar_prefetch=2,
        grid=(nt,),
        in_specs=[pl.BlockSpec(memory_space=pl.ANY)],
        out_specs=pl.BlockSpec(memory_space=pl.ANY),
        scratch_shapes=[
            pltpu.SMEM((ktm,), I32),
            pltpu.SMEM((ktm,), I32),
            pltpu.SMEM((nblocks * MOE_ROWS,), I32),
            pltpu.SemaphoreType.DMA((2,)),
            pltpu.SemaphoreType.DMA(()),
        ],
    )
    inv = pl.pallas_call(
        functools.partial(_inverse_kernel, t=t, tm=ktm // TOP_K),
        out_shape=jax.ShapeDtypeStruct((nblocks * MOE_ROWS,), I32),
        grid_spec=grid_spec,
        compiler_params=_params(("arbitrary",)),
        name="route_inverse",
    )(bexp, nused, dest2)
    return inv.reshape(nblocks, MOE_ROWS)


def _pack_bf16_pairs(x):
    n = x.shape[1] // 2
    lo = lax.bitcast_convert_type(x[:, :n].astype(BF16).astype(F32), U32)
    hi = lax.bitcast_convert_type(x[:, n:].astype(BF16).astype(F32), U32)
    return hi | (lo >> 16)


def _unpack_bf16_pairs(w):
    lo = lax.bitcast_convert_type(w << 16, F32)
    hi = lax.bitcast_convert_type(w & jnp.uint32(0xFFFF0000), F32)
    return lo, hi


def _expert_kernel(bexp_ref, nused_ref, inv_hbm, h2_hbm, wgu_ref, bgu_ref, wd_ref, bd_ref, y4_hbm,
                   ig0, ig1, is0, is1, xbuf, ybuf, wgu_bf, wd_bf, gisem, sisem, gsem, ssem, *, dff, t):
    i = pl.program_id(0)
    nblk = pl.num_programs(0)
    nused = nused_ref[0]
    rows = xbuf.shape[1]
    half = xbuf.shape[2]
    used = i < nused
    fresh = jnp.logical_or(i == 0, bexp_ref[i] != bexp_ref[jnp.maximum(i - 1, 0)])
    gidx, sidx = (ig0, ig1), (is0, is1)

    def gidx_copy(j, p):
        return pltpu.make_async_copy(inv_hbm.at[j], gidx[p], gisem.at[p])

    def sidx_copy(j, p):
        return pltpu.make_async_copy(inv_hbm.at[j], sidx[p], sisem.at[p])

    def gather_rows(p):
        def body(rr, carry):
            for u in range(2):
                r = rr * 2 + u
                slot_row = gidx[p][r]
                tok = slot_row & (t - 1) if t & (t - 1) == 0 else lax.rem(slot_row, t)
                pltpu.make_async_copy(h2_hbm.at[pl.ds(tok, 1), :], xbuf.at[p, pl.ds(r, 1), :],
                                      gsem.at[p]).start(priority=u)
            return carry

        for rr in range(rows // 2):
            body(rr, 0)

    def gather_wait(p):
        pltpu.make_async_copy(h2_hbm.at[pl.ds(0, rows), :], xbuf.at[p], gsem.at[p]).wait()

    def scatter_rows(p):
        def body(rr, carry):
            for u in range(2):
                r = rr * 2 + u
                pltpu.make_async_copy(ybuf.at[p, pl.ds(r, 1), :], y4_hbm.at[pl.ds(sidx[p][r], 1), :],
                                      ssem.at[p]).start(priority=u)
            return carry

        for rr in range(rows // 2):
            body(rr, 0)

    def scatter_wait(p):
        pltpu.make_async_copy(ybuf.at[p], y4_hbm.at[pl.ds(0, rows), :], ssem.at[p]).wait()

    @pl.when(i == 0)
    def _():
        ybuf[...] = jnp.zeros(ybuf.shape, U32)
        for s in range(2):
            spare = pltpu.make_async_copy(ybuf.at[s], y4_hbm.at[pl.ds(TOP_K * t + s * rows, rows), :], ssem.at[s])
            spare.start()
            spare.wait()
        gidx_copy(0, 0).start()
        sidx_copy(0, 0).start()

        @pl.when(nused > 1)
        def _():
            gidx_copy(1, 1).start()

        gidx_copy(0, 0).wait()
        gather_rows(0)

    @pl.when(jnp.logical_and(used, fresh))
    def _():
        wgu_bf[...] = wgu_ref[...].astype(BF16)
        wd_bf[...] = wd_ref[...].astype(BF16)

    for p in range(2):
        q = 1 - p

        @pl.when(i % 2 == p)
        def _():
            @pl.when(i + 1 < nused)
            def _():
                gidx_copy(i + 1, q).wait()
                gather_rows(q)
                sidx_copy(i + 1, q).start()

            @pl.when(i + 2 < nused)
            def _():
                gidx_copy(i + 2, p).start()

            @pl.when(jnp.logical_and(i >= 2, i - 2 < nused))
            def _():
                scatter_wait(p)

            @pl.when(used)
            def _():
                gather_wait(p)
                lo, hi = _unpack_bf16_pairs(xbuf[p])
                hu = (_dot(lo.astype(BF16), wgu_bf[0:half, :]) + _dot(hi.astype(BF16), wgu_bf[half:2 * half, :])
                      + bgu_ref[...])
                glu = jnp.minimum(hu[:, 0:dff], SWIGLU_LIMIT)
                lin = jnp.clip(hu[:, dff:2 * dff], -SWIGLU_LIMIT, SWIGLU_LIMIT)
                act = glu * _sigmoid(SWIGLU_ALPHA * glu) * (lin + 1.0)
                ybuf[p] = _pack_bf16_pairs(_dot(act.astype(BF16), wd_bf[...]) + bd_ref[...])
                sidx_copy(i, p).wait()
                scatter_rows(p)

            @pl.when(i == nblk - 1)
            def _():
                @pl.when(jnp.logical_and(i >= 1, i - 1 < nused))
                def _():
                    scatter_wait(q)

                @pl.when(used)
                def _():
                    scatter_wait(p)


def _experts(bexp, nused, inv, h2p, w_gu, b_gu4, w_down, b_down4, l, t):
    nblocks = inv.shape[0]
    half = h2p.shape[1]
    d = 2 * half
    dff = w_down.shape[2]

    wsel = lambda i, be, nu: (l, be[jnp.minimum(i, nu[0] - 1)], 0, 0)
    grid_spec = pltpu.PrefetchScalarGridSpec(
        num_scalar_prefetch=2,
        grid=(nblocks,),
        in_specs=[
            pl.BlockSpec(memory_space=pl.ANY),
            pl.BlockSpec(memory_space=pl.ANY),
            pl.BlockSpec((None, None, d, 2 * dff), wsel),
            pl.BlockSpec((None, None, 1, 2 * dff), wsel),
            pl.BlockSpec((None, None, dff, d), wsel),
            pl.BlockSpec((None, None, 1, d), wsel),
        ],
        out_specs=pl.BlockSpec(memory_space=pl.ANY),
        scratch_shapes=[pltpu.SMEM((MOE_ROWS,), I32)] * 4 + [
            pltpu.VMEM((2, MOE_ROWS, half), U32),
            pltpu.VMEM((2, MOE_ROWS, half), U32),
            pltpu.VMEM((d, 2 * dff), BF16),
            pltpu.VMEM((dff, d), BF16),
        ] + [pltpu.SemaphoreType.DMA((2,))] * 4,
    )
    return pl.pallas_call(
        functools.partial(_expert_kernel, dff=dff, t=t),
        out_shape=jax.ShapeDtypeStruct((TOP_K * t + 2 * MOE_ROWS, half), U32),
        grid_spec=grid_spec,
        compiler_params=_params(("arbitrary",)),
        name="moe_experts",
    )(bexp, nused, inv, h2p, w_gu, b_gu4, w_down, b_down4)


def _combine_kernel(*refs, d, final):
    y_refs = refs[:TOP_K]
    gt_ref, x_ref, mod_ref = refs[TOP_K:TOP_K + 3]
    xo_ref = refs[-1]
    half = d // 2
    gt = gt_ref[...]
    acc_lo = acc_hi = None
    for k in range(TOP_K):
        lo, hi = _unpack_bf16_pairs(y_refs[k][...])
        g = gt[:, k:k + 1]
        acc_lo = g * lo if acc_lo is None else acc_lo + g * lo
        acc_hi = g * hi if acc_hi is None else acc_hi + g * hi
    x_lo = x_ref[:, 0:half] + mod_ref[:, 5 * d:5 * d + half] * acc_lo
    x_hi = x_ref[:, half:d] + mod_ref[:, 5 * d + half:6 * d] * acc_hi
    if final:
        gain = refs[TOP_K + 3][...]
        ssq = jnp.sum(x_lo * x_lo, axis=-1, keepdims=True) + jnp.sum(x_hi * x_hi, axis=-1, keepdims=True)
        inv = lax.rsqrt(ssq * (1.0 / d) + EPS)
        x_lo = x_lo * inv * gain[:, 0:half]
        x_hi = x_hi * inv * gain[:, half:d]
    xo_ref[:, 0:half] = x_lo
    xo_ref[:, half:d] = x_hi


def _combine(y4p, gates_t, x2, mod3, l, seq, depth, final_gain=None):
    t, d = x2.shape
    tm = min(PROJ_TILE, seq)
    nb = mod3.shape[0] // depth
    nt = t // tm
    final = final_gain is not None

    def slot(k):
        return pl.BlockSpec((tm, d // 2), lambda i: (k * nt + i, 0))

    extra_specs = [pl.BlockSpec((1, d), lambda i: (0, 0))] if final else []
    extra_args = [final_gain.reshape(1, d)] if final else []
    return pl.pallas_call(
        functools.partial(_combine_kernel, d=d, final=final),
        out_shape=jax.ShapeDtypeStruct((t, d), F32),
        grid=(nt,),
        in_specs=[slot(k) for k in range(TOP_K)] + [
            pl.BlockSpec((tm, TOP_K), lambda i: (i, 0)),
            pl.BlockSpec((tm, d), lambda i: (i, 0)),
            pl.BlockSpec((None, 1, N_MOD * d), lambda i: (l * nb + (i * tm) // seq, 0, 0)),
        ] + extra_specs,
        out_specs=pl.BlockSpec((tm, d), lambda i: (i, 0)),
        compiler_params=_params(("arbitrary",)),
        name="moe_combine",
    )(*([y4p] * TOP_K), gates_t, x2, mod3, *extra_args)


def _final_kernel(x_ref, g_ref, o_ref):
    x = x_ref[...]
    o_ref[...] = x * lax.rsqrt(jnp.mean(x * x, axis=-1, keepdims=True) + EPS) * g_ref[...]


def _final_norm(x2, gain):
    t, d = x2.shape
    tm = min(PROJ_TILE, t)
    return pl.pallas_call(
        _final_kernel,
        out_shape=jax.ShapeDtypeStruct((t, d), F32),
        grid=(t // tm,),
        in_specs=[pl.BlockSpec((tm, d), lambda i: (i, 0)), pl.BlockSpec((1, d), lambda i: (0, 0))],
        out_specs=pl.BlockSpec((tm, d), lambda i: (i, 0)),
        compiler_params=_params(("arbitrary",)),
        name="final_norm",
    )(x2, gain.reshape(1, d))


def _mixer_layer(x2, mod3, tables, lower, consts, p, l, batch, seq):
    ra, rb, rc = tables
    qa, ka2, va2, hg = _inproj(x2, mod3, p["norm1"], p["w_in"], ra, rb, rc, l, seq)
    a = _attention(qa, ka2, va2, p["attn_sink"], p["attn_norm"], l, batch, seq)
    o = _hgrn(hg, lower, p["hg_norm"], consts, l, batch, seq)
    return a, o


SC_WINDOW = 128


def _sc_mesh():
    return plsc.VectorSubcoreMesh(core_axis_name="core", subcore_axis_name="subcore")


def _sc_worker(mesh):
    return lax.axis_index("core") * mesh.num_subcores + lax.axis_index("subcore"), mesh.num_cores * mesh.num_subcores


def _sc_scatter_rows(src, dest_rows, n_out):
    n, width = src.shape
    nk = dest_rows.shape[0]
    mesh = _sc_mesh()

    @pl.kernel(out_type=jax.ShapeDtypeStruct((n_out, width), src.dtype), mesh=mesh,
               scratch_types=[pltpu.VMEM((nk, SC_WINDOW), I32), pltpu.VMEM((SC_WINDOW, width), src.dtype)],
               name="sc_scatter_rows")
    def scatter_kernel(src_hbm, idx_hbm, out_hbm, idx_v, rows_v):
        wid, nw = _sc_worker(mesh)
        per = n // SC_WINDOW // nw

        @pl.loop(0, per)
        def _(j):
            base = (wid * per + j) * SC_WINDOW
            pltpu.sync_copy(idx_hbm.at[:, pl.ds(base, SC_WINDOW)], idx_v)
            pltpu.sync_copy(src_hbm.at[pl.ds(base, SC_WINDOW)], rows_v)
            for k in range(nk):
                pltpu.sync_copy(rows_v, out_hbm.at[idx_v.at[k]])

    return scatter_kernel(src, dest_rows)


def _sc_gather_rows(src, rows):
    n = rows.shape[0]
    width = src.shape[1]
    mesh = _sc_mesh()

    @pl.kernel(out_type=jax.ShapeDtypeStruct((n, width), src.dtype), mesh=mesh,
               scratch_types=[pltpu.VMEM((1, SC_WINDOW), I32), pltpu.VMEM((SC_WINDOW, width), src.dtype)],
               name="sc_gather_rows")
    def gather_kernel(src_hbm, idx_hbm, out_hbm, idx_v, rows_v):
        wid, nw = _sc_worker(mesh)
        per = n // SC_WINDOW // nw

        @pl.loop(0, per)
        def _(j):
            base = (wid * per + j) * SC_WINDOW
            pltpu.sync_copy(idx_hbm.at[:, pl.ds(base, SC_WINDOW)], idx_v)
            pltpu.sync_copy(src_hbm.at[idx_v.at[0]], rows_v)
            pltpu.sync_copy(rows_v, out_hbm.at[pl.ds(base, SC_WINDOW)])

    return gather_kernel(src, rows.reshape(1, n))


def _expert_group_kernel(eb_ref, xs_hbm, wgu_ref, bgu_ref, wd_ref, bd_ref, ys_hbm,
                         xbuf, ybuf, wgu_bf, wd_bf, xsem, ysem, *, dff):
    e = pl.program_id(0)
    ne = pl.num_programs(0)
    b0 = eb_ref[e]
    nb = eb_ref[ne + e]
    rows, half = xbuf.shape[1], xbuf.shape[2]
    group = rows // MOE_ROWS
    nch = (nb + group - 1) // group

    def x_copy(j, slot):
        start = (b0 + j * group) * MOE_ROWS
        return pltpu.make_async_copy(xs_hbm.at[pl.ds(start, rows), :], xbuf.at[slot], xsem.at[slot])

    def y_copy(j, slot, g):
        start = (b0 + j * group + g) * MOE_ROWS
        return pltpu.make_async_copy(ybuf.at[slot, pl.ds(g * MOE_ROWS, MOE_ROWS), :],
                                     ys_hbm.at[pl.ds(start, MOE_ROWS), :], ysem.at[slot, g])

    def y_each(j, slot, fn):
        for g in range(group):
            @pl.when(j * group + g < nb)
            def _():
                fn(y_copy(j, slot, g))

    @pl.when(nb > 0)
    def _():
        x_copy(0, 0).start()
        wgu_bf[...] = wgu_ref[...].astype(BF16)
        wd_bf[...] = wd_ref[...].astype(BF16)

        def chunk(j, carry):
            slot = j % 2

            @pl.when(j + 1 < nch)
            def _():
                x_copy(j + 1, 1 - slot).start()

            x_copy(j, slot).wait()

            @pl.when(j >= 2)
            def _():
                y_each(j - 2, slot, lambda cp: cp.wait())

            lo, hi = _unpack_bf16_pairs(xbuf[slot])
            hu = (_dot(lo.astype(BF16), wgu_bf[0:half, :]) + _dot(hi.astype(BF16), wgu_bf[half:2 * half, :])
                  + bgu_ref[...])
            glu = jnp.minimum(hu[:, 0:dff], SWIGLU_LIMIT)
            lin = jnp.clip(hu[:, dff:2 * dff], -SWIGLU_LIMIT, SWIGLU_LIMIT)
            act = glu * _sigmoid(SWIGLU_ALPHA * glu) * (lin + 1.0)
            ybuf[slot] = _pack_bf16_pairs(_dot(act.astype(BF16), wd_bf[...]) + bd_ref[...])
            y_each(j, slot, lambda cp: cp.start())
            return carry

        lax.fori_loop(0, nch, chunk, 0)

        @pl.when(nch >= 2)
        def _():
            y_each(nch - 2, nch % 2, lambda cp: cp.wait())

        y_each(nch - 1, (nch - 1) % 2, lambda cp: cp.wait())


def _block_experts(eblk, xs, w_gu, b_gu4, w_down, b_down4, l):
    cap, half = xs.shape
    d = 2 * half
    dff = w_down.shape[2]
    wsel = lambda e, eb: (l, e, 0, 0)
    grid_spec = pltpu.PrefetchScalarGridSpec(
        num_scalar_prefetch=1,
        grid=(N_EXPERTS,),
        in_specs=[
            pl.BlockSpec(memory_space=pl.ANY),
            pl.BlockSpec((None, None, d, 2 * dff), wsel),
            pl.BlockSpec((None, None, 1, 2 * dff), wsel),
            pl.BlockSpec((None, None, dff, d), wsel),
            pl.BlockSpec((None, None, 1, d), wsel),
        ],
        out_specs=pl.BlockSpec(memory_space=pl.ANY),
        scratch_shapes=[
            pltpu.VMEM((2, EXPERT_GROUP * MOE_ROWS, half), U32),
            pltpu.VMEM((2, EXPERT_GROUP * MOE_ROWS, half), U32),
            pltpu.VMEM((d, 2 * dff), BF16),
            pltpu.VMEM((dff, d), BF16),
            pltpu.SemaphoreType.DMA((2,)),
            pltpu.SemaphoreType.DMA((2, EXPERT_GROUP)),
        ],
    )
    return pl.pallas_call(
        functools.partial(_expert_group_kernel, dff=dff),
        out_shape=jax.ShapeDtypeStruct((cap - (EXPERT_GROUP - 1) * MOE_ROWS, half), U32),
        grid_spec=grid_spec,
        compiler_params=_params(("arbitrary",)),
        name="moe_expert_groups",
    )(eblk, xs, w_gu, b_gu4, w_down, b_down4)


def _moe_layer_sc(a, o, x2, mod3, p, l, seq, depth, final_gain=None):
    t, d = x2.shape
    xn, h2p, topi, gates = _outproj(a, o, x2, mod3, p["norm2"], p["w_out"], p["w_router_t"], p["b_router"], l, seq)
    rank, counts = _ranks(topi)
    nblocks = (t * TOP_K + N_EXPERTS * MOE_ROWS) // MOE_ROWS
    dest, _, _, eblk = _destinations(counts, topi, rank, nblocks)
    xs = _sc_scatter_rows(h2p, dest, (nblocks + EXPERT_GROUP - 1) * MOE_ROWS)
    ys = _block_experts(eblk[:, :, 0].reshape(-1), xs, p["w_gu"], p["b_gu"], p["w_down"], p["b_down"], l)
    y4p = _sc_gather_rows(ys, dest.reshape(-1))
    return _combine(y4p, gates.T, xn, mod3, l, seq, depth, final_gain)


def _moe_layer(a, o, x2, mod3, p, l, seq, depth):
    t, d = x2.shape
    xn, h2p, topi, gates = _outproj(a, o, x2, mod3, p["norm2"], p["w_out"], p["w_router_t"], p["b_router"], l, seq)
    rank, counts = _ranks(topi)
    nblocks = (t * TOP_K + N_EXPERTS * MOE_ROWS) // MOE_ROWS
    dest, bexp, nused, _ = _destinations(counts, topi, rank, nblocks)
    tm = min(TOK_TILE, t)
    dest2 = dest.reshape(TOP_K, t // tm, tm).transpose(1, 0, 2).reshape(t // tm, TOP_K * tm)
    bexp, nused = bexp.reshape(-1), nused.reshape(-1)
    inv = _inverse_map(bexp, nused, dest2, nblocks, t)
    y4p = _experts(bexp, nused, inv, h2p, p["w_gu"], p["b_gu"], p["w_down"], p["b_down"], l, t)
    return _combine(y4p, gates.T, xn, mod3, l, seq, depth)


def kernel(x, c, positions, w_ada, b_ada, norm1, w_in, attn_sink, attn_norm, hg_lb_logits, hg_norm, w_out, norm2,
           w_router, b_router, w_gu, b_gu, w_down, b_down, final_norm):
    batch, seq, d = x.shape
    depth = w_ada.shape[0]
    t = batch * seq
    p = {
        "norm1": norm1.reshape(depth, 1, d),
        "w_in": w_in.astype(BF16),
        "attn_sink": attn_sink.astype(F32),
        "attn_norm": attn_norm.reshape(depth, 1, ATT_WIDTH),
        "hg_norm": hg_norm.reshape(depth, 1, HG_HEAD_DIM),
        "w_out": w_out.astype(BF16),
        "norm2": norm2.reshape(depth, 1, d),
        "w_router_t": jnp.swapaxes(w_router, 1, 2),
        "b_router": b_router.reshape(depth, N_EXPERTS, 1),
        "w_gu": w_gu,
        "b_gu": b_gu.reshape(depth, N_EXPERTS, 1, b_gu.shape[-1]),
        "w_down": w_down,
        "b_down": b_down.reshape(depth, N_EXPERTS, 1, d),
    }
    mod3 = _ada_all(c, w_ada, b_ada).reshape(depth * batch, 1, N_MOD * d)
    lower = _lower_bounds(hg_lb_logits)
    tables = _rope_tables(positions)
    consts = _hgrn_constants()
    consts = (jnp.asarray(consts[0], BF16), jnp.asarray(consts[1], BF16), jnp.asarray(consts[2]), jnp.asarray(consts[3]))
    x2 = x.reshape(t, d)
    for l in range(depth):
        a, o = _mixer_layer(x2, mod3, tables, lower, consts, p, l, batch, seq)
        x2 = _moe_layer_sc(a, o, x2, mod3, p, l, seq, depth, final_norm if l == depth - 1 else None)
    return x2.reshape(batch, seq, d)
```

```python
import functools

import numpy as np
import jax
import jax.numpy as jnp
from jax import lax
from jax.experimental import pallas as pl
from jax.experimental.pallas import tpu as pltpu
from jax.experimental.pallas import tpu_sc as plsc

F32 = jnp.float32
BF16 = jnp.bfloat16
I32 = jnp.int32
U32 = jnp.uint32

ATT_HEADS = 8
ATT_KV_HEADS = 2
ATT_HEAD_DIM = 64
ATT_WIDTH = ATT_HEADS * ATT_HEAD_DIM
KV_WIDTH = ATT_KV_HEADS * ATT_HEAD_DIM
WINDOW = 128
ATT_BLOCK = 128
ROPE_THETA = 500000.0
ROPE_DIM = ATT_HEAD_DIM // 4
HG_HEADS = 4
HG_HEAD_DIM = 128
HG_WIDTH = HG_HEADS * HG_HEAD_DIM
N_EXPERTS = 32
TOP_K = 4
SWIGLU_ALPHA = 1.702
SWIGLU_LIMIT = 7.0
N_MOD = 6
EPS = 1e-6
NEG_INF = -1e30
LB_FLOOR = 1e-30

LANES = 128
HG_CHUNK = 64
HG_LEVELS = 6
HG_UNROLL = 8
MOE_SHIFT = 8
MOE_ROWS = 1 << MOE_SHIFT
EXPERT_GROUP = 1
TOK_TILE = 256
PROJ_TILE = 512
RANK_TILE = 512
VMEM_LIMIT = 56 * 1024 * 1024


def _dot(a, b):
    return jnp.dot(a, b, preferred_element_type=F32)


def _dot_nt(a, b):
    return lax.dot_general(a, b, (((1,), (1,)), ((), ())), preferred_element_type=F32)


def _dot_tn(a, b):
    return lax.dot_general(a, b, (((0,), (0,)), ((), ())), preferred_element_type=F32)


def _split3(x):
    hi = x.astype(BF16)
    r1 = x - hi.astype(F32)
    mid = r1.astype(BF16)
    lo = (r1 - mid.astype(F32)).astype(BF16)
    return hi, mid, lo


def _dot_exact_lhs(m_bf16, x):
    hi, mid, lo = _split3(x)
    return _dot(m_bf16, hi) + _dot(m_bf16, mid) + _dot(m_bf16, lo)


def _dot_f32_nt(a, b):
    ah, am, _ = _split3(a)
    bh, bm, _ = _split3(b)
    return _dot_nt(ah, bh) + _dot_nt(ah, bm) + _dot_nt(am, bh)


def _dot_f32(a, b):
    ah, am, al = _split3(a)
    bh, bm, bl = _split3(b)
    return (_dot(ah, bh) + _dot(ah, bm) + _dot(am, bh)
            + _dot(ah, bl) + _dot(am, bm) + _dot(al, bh))


def _sigmoid(x):
    return 1.0 / (1.0 + jnp.exp(-x))


def _params(sem=None):
    return pltpu.CompilerParams(dimension_semantics=sem, vmem_limit_bytes=VMEM_LIMIT)


def _ada_kernel(c_ref, w_ref, b_ref, o_ref):
    c = c_ref[...]
    cond = c * _sigmoid(c)
    o_ref[...] = _dot_f32(cond, w_ref[...]) + b_ref[...]


def _ada_all(c, w_ada, b_ada):
    depth, d, n = w_ada.shape
    b = c.shape[0]
    nt = n // d
    return pl.pallas_call(
        _ada_kernel,
        out_shape=jax.ShapeDtypeStruct((depth, b, n), F32),
        grid=(depth, nt),
        in_specs=[
            pl.BlockSpec((b, d), lambda l, j: (0, 0)),
            pl.BlockSpec((None, d, d), lambda l, j: (l, 0, j)),
            pl.BlockSpec((None, 1, d), lambda l, j: (l, 0, j)),
        ],
        out_specs=pl.BlockSpec((None, b, d), lambda l, j: (l, 0, j)),
        compiler_params=_params(("arbitrary", "arbitrary")),
        name="ada_mod",
    )(c, w_ada, b_ada.reshape(depth, 1, n))


def _lb_kernel(x_ref, o_ref):
    depth = x_ref.shape[0]
    xs = [x_ref[l] for l in range(depth)]
    m = xs[0]
    for l in range(1, depth):
        m = jnp.maximum(m, xs[l])
    es = [jnp.exp(v - m) for v in xs]
    den = es[0]
    for l in range(1, depth):
        den = den + es[l]
    ps = [e / den for e in es]
    run = ps[0]
    o_ref[0] = run - ps[0]
    for l in range(1, depth):
        run = run + ps[l]
        o_ref[l] = run - ps[0]


def _lower_bounds(hg_lb_logits):
    return pl.pallas_call(
        _lb_kernel,
        out_shape=jax.ShapeDtypeStruct(hg_lb_logits.shape, F32),
        name="hg_lower_bounds",
    )(hg_lb_logits.astype(F32))


def _rope_kernel(pos_ref, invf_ref, a_ref, b_ref, c_ref):
    pos = pos_ref[...].astype(F32)
    ang = pos * invf_ref[...]
    cs = jnp.cos(ang)
    sn = jnp.sin(ang)
    lane = lax.broadcasted_iota(I32, ang.shape, 1) & (ATT_HEAD_DIM - 1)
    half = ROPE_DIM // 2
    first = lane < half
    second = (lane >= half) & (lane < ROPE_DIM)
    a_ref[...] = jnp.where(first | second, cs, 1.0)
    b_ref[...] = jnp.where(first, -sn, 0.0)
    c_ref[...] = jnp.where(second, sn, 0.0)


def _rope_tables(positions):
    t = positions.size
    half = ROPE_DIM // 2
    inv = (np.float32(ROPE_THETA) ** (-(np.arange(half, dtype=np.float32) * np.float32(2.0) / np.float32(ROPE_DIM)))).astype(np.float32)
    lane = np.arange(LANES) % ATT_HEAD_DIM
    pat = np.where(lane < ROPE_DIM, inv[lane % half], 0.0).astype(np.float32).reshape(1, LANES)
    tm = min(t, 2048)
    shp = jax.ShapeDtypeStruct((t, LANES), F32)
    spec = pl.BlockSpec((tm, LANES), lambda i: (i, 0))
    return pl.pallas_call(
        _rope_kernel,
        out_shape=(shp, shp, shp),
        grid=(t // tm,),
        in_specs=[pl.BlockSpec((tm, 1), lambda i: (i, 0)), pl.BlockSpec((1, LANES), lambda i: (0, 0))],
        out_specs=(spec, spec, spec),
        compiler_params=_params(("arbitrary",)),
        name="rope_tables",
    )(positions.reshape(t, 1).astype(I32), jnp.asarray(pat))


def _rms_mod(x, gain, scale, shift):
    ms = jnp.mean(x * x, axis=-1, keepdims=True)
    return (x * lax.rsqrt(ms + EPS) * gain) * (1.0 + scale) + shift


def _rope_apply(x, a, b, c):
    half = ROPE_DIM // 2
    return x * a + pltpu.roll(x, LANES - half, 1) * b + pltpu.roll(x, half, 1) * c


def _inproj_kernel(x_ref, mod_ref, n1_ref, w_ref, ra_ref, rb_ref, rc_ref, qa_ref, ka_ref, va_ref, hg_ref, *, d):
    mod = mod_ref[...]
    h = _rms_mod(x_ref[...], n1_ref[...], mod[:, d:2 * d], mod[:, 0:d]).astype(BF16)
    a, b, c = ra_ref[...], rb_ref[...], rc_ref[...]
    kvw = ATT_WIDTH + 2 * KV_WIDTH
    pa = _dot(h, w_ref[:, 0:kvw])
    scale = ATT_HEAD_DIM ** -0.5
    for g in range(ATT_WIDTH // LANES):
        qg = _rope_apply(pa[:, g * LANES:(g + 1) * LANES], a, b, c)
        qa_ref[:, g * LANES:(g + 1) * LANES] = (qg * scale).astype(BF16)
    k = _rope_apply(pa[:, ATT_WIDTH:ATT_WIDTH + KV_WIDTH], a, b, c)
    v = pa[:, ATT_WIDTH + KV_WIDTH:kvw]
    ka_ref[:, 0:LANES] = k.astype(BF16)
    ka_ref[:, LANES:2 * LANES] = pltpu.roll(k, ATT_HEAD_DIM, 1).astype(BF16)
    va_ref[:, 0:LANES] = v.astype(BF16)
    va_ref[:, LANES:2 * LANES] = pltpu.roll(v, ATT_HEAD_DIM, 1).astype(BF16)
    for g in range(5):
        lo = kvw + g * HG_WIDTH
        hg_ref[:, g * HG_WIDTH:(g + 1) * HG_WIDTH] = _dot(h, w_ref[:, lo:lo + HG_WIDTH])


def _inproj(x2, mod3, norm1, w_in_bf, ra, rb, rc, l, seq):
    t, d = x2.shape
    n_in = w_in_bf.shape[-1]
    tm = min(PROJ_TILE, seq)
    nb = mod3.shape[0] // norm1.shape[0]
    tok = lambda i: (i, 0)
    return pl.pallas_call(
        functools.partial(_inproj_kernel, d=d),
        out_shape=(
            jax.ShapeDtypeStruct((t, ATT_WIDTH), BF16),
            jax.ShapeDtypeStruct((t, 2 * KV_WIDTH), BF16),
            jax.ShapeDtypeStruct((t, 2 * KV_WIDTH), BF16),
            jax.ShapeDtypeStruct((t, 5 * HG_WIDTH), F32),
        ),
        grid=(t // tm,),
        in_specs=[
            pl.BlockSpec((tm, d), tok),
            pl.BlockSpec((None, 1, N_MOD * d), lambda i: (l * nb + (i * tm) // seq, 0, 0)),
            pl.BlockSpec((None, 1, d), lambda i: (l, 0, 0)),
            pl.BlockSpec((None, d, n_in), lambda i: (l, 0, 0)),
            pl.BlockSpec((tm, LANES), tok),
            pl.BlockSpec((tm, LANES), tok),
            pl.BlockSpec((tm, LANES), tok),
        ],
        out_specs=(
            pl.BlockSpec((tm, ATT_WIDTH), tok),
            pl.BlockSpec((tm, 2 * KV_WIDTH), tok),
            pl.BlockSpec((tm, 2 * KV_WIDTH), tok),
            pl.BlockSpec((tm, 5 * HG_WIDTH), tok),
        ),
        compiler_params=_params(("arbitrary",)),
        name="in_proj",
    )(x2, mod3, norm1, w_in_bf, ra, rb, rc)


def _attn_kernel(sink_ref, q_ref, kp_ref, kc_ref, kn_ref, vp_ref, vc_ref, vn_ref, gain_ref, o_ref, *, l, seq):
    n = pl.program_id(1)
    blk = ATT_BLOCK
    k2 = jnp.concatenate([kp_ref[...], kc_ref[...], kn_ref[...]], axis=0)
    v2 = jnp.concatenate([vp_ref[...], vc_ref[...], vn_ref[...]], axis=0)
    lane = lax.broadcasted_iota(I32, (3 * blk, LANES), 1)
    lo_half = lane < ATT_HEAD_DIM
    zero = jnp.zeros((3 * blk, LANES), BF16)
    ka, kb = k2[:, 0:LANES], k2[:, LANES:2 * LANES]
    va, vb = v2[:, 0:LANES], v2[:, LANES:2 * LANES]
    kz = [[jnp.where(lo_half, ka, zero), jnp.where(lo_half, zero, kb)],
          [jnp.where(lo_half, kb, zero), jnp.where(lo_half, zero, ka)]]
    vz = [[jnp.where(lo_half, va, zero), jnp.where(lo_half, zero, vb)],
          [jnp.where(lo_half, vb, zero), jnp.where(lo_half, zero, va)]]
    qpos = n * blk + lax.broadcasted_iota(I32, (blk, 3 * blk), 0)
    kpos = (n - 1) * blk + lax.broadcasted_iota(I32, (blk, 3 * blk), 1)
    valid = (jnp.abs(qpos - kpos) <= WINDOW) & (kpos >= 0) & (kpos < seq)
    valid2 = jnp.concatenate([valid, valid], axis=0)
    upper = lax.broadcasted_iota(I32, (2 * blk, 1), 0) < blk
    outs = []
    for j in range(ATT_KV_HEADS):
        qs = jnp.concatenate([q_ref[:, 2 * j * LANES:(2 * j + 1) * LANES],
                              q_ref[:, (2 * j + 1) * LANES:(2 * j + 2) * LANES]], axis=0)
        acc = None
        for half in range(2):
            s = _dot_nt(qs, kz[j][half])
            s = jnp.where(valid2, s, NEG_INF)
            sink = jnp.where(upper, sink_ref[l, 4 * j + half], sink_ref[l, 4 * j + 2 + half])
            mx = jnp.maximum(jnp.max(s, axis=-1, keepdims=True), sink)
            p = jnp.exp(s - mx)
            den = jnp.sum(p, axis=-1, keepdims=True) + jnp.exp(sink - mx)
            p = (p * (1.0 / den)).astype(BF16)
            pv = _dot(p, vz[j][half])
            acc = pv if acc is None else acc + pv
        outs.append(acc[0:blk])
        outs.append(acc[blk:2 * blk])
    o = jnp.concatenate(outs, axis=-1)
    ms = jnp.mean(o * o, axis=-1, keepdims=True)
    o_ref[...] = (o * lax.rsqrt(ms + EPS) * gain_ref[...]).astype(BF16)


def _attention(qa, ka2, va2, attn_sink, attn_norm3, l, batch, seq):
    t = qa.shape[0]
    blk = ATT_BLOCK
    nb = seq // blk
    cur = lambda b, n: (b * nb + n, 0)
    prev = lambda b, n: (b * nb + jnp.maximum(n - 1, 0), 0)
    nxt = lambda b, n: (b * nb + jnp.minimum(n + 1, nb - 1), 0)
    kvspec = lambda f: pl.BlockSpec((blk, 2 * KV_WIDTH), f)
    return pl.pallas_call(
        functools.partial(_attn_kernel, l=l, seq=seq),
        out_shape=jax.ShapeDtypeStruct((t, ATT_WIDTH), BF16),
        grid=(batch, nb),
        in_specs=[
            pl.BlockSpec(memory_space=pltpu.SMEM),
            pl.BlockSpec((blk, ATT_WIDTH), cur),
            kvspec(prev), kvspec(cur), kvspec(nxt),
            kvspec(prev), kvspec(cur), kvspec(nxt),
            pl.BlockSpec((None, 1, ATT_WIDTH), lambda b, n: (l, 0, 0)),
        ],
        out_specs=pl.BlockSpec((blk, ATT_WIDTH), cur),
        compiler_params=_params(("arbitrary", "arbitrary")),
        name="window_attn",
    )(attn_sink, qa, ka2, ka2, ka2, va2, va2, va2, attn_norm3)


def _hgrn_constants():
    c, nl = HG_CHUNK, HG_LEVELS
    r = np.arange(c)
    u = r[None, :]
    rr = r[:, None]
    mats, masks = [], []
    for lev in range(nl):
        m = 1 << lev
        parent = r // (2 * m)
        anchor = parent * 2 * m + m
        upper = r >= anchor
        aa = anchor[:, None]
        mats.append(np.where(upper[:, None], (u > aa) & (u <= rr), (u > rr) & (u <= aa)))
        masks.append((parent[:, None] == parent[None, :]) & upper[:, None] & (~upper)[None, :])
    masks.append(np.eye(c, dtype=bool))
    mats.append(u <= rr)
    mats.append(u > rr)
    mf = np.concatenate(mats, axis=0).astype(np.float32)
    mb = np.concatenate([mt[::-1, ::-1] for mt in mats], axis=0).astype(np.float32)
    kf = np.stack(masks).astype(np.float32)
    kb = np.stack([mk[::-1, ::-1] for mk in masks]).astype(np.float32)
    return mf, mb, kf, kb


def _hgrn_kernel(q_ref, ff_ref, fb_ref, i_ref, g_ref, lb_ref, gn_ref, mf_ref, mb_ref, kf_ref, kb_ref,
                 o_ref, of_scr, ob_scr, st_scr, *, seq):
    c, nl = HG_CHUNK, HG_LEVELS
    nc = seq // c
    lb = lb_ref[...]
    st_scr[...] = jnp.zeros(st_scr.shape, F32)

    unroll = HG_UNROLL if nc % HG_UNROLL == 0 else 1
    dirs = (
        dict(f_ref=ff_ref, lbrow=lb[0:1, :], m_ref=mf_ref, k_ref=kf_ref, o_scr=of_scr, last_row=c - 1, d=0),
        dict(f_ref=fb_ref, lbrow=lb[1:2, :], m_ref=mb_ref, k_ref=kb_ref, o_scr=ob_scr, last_row=0, d=1),
    )

    def body(i, carry):
        work = []
        for u in range(unroll):
            cf = i * unroll + u
            work.append((dirs[0], pl.ds(pl.multiple_of(cf * c, c), c)))
            work.append((dirs[1], pl.ds(pl.multiple_of((nc - 1 - cf) * c, c), c)))
        gates = []
        for dr, rows in work:
            lbf = jnp.maximum(dr["lbrow"], LB_FLOOR)
            oml = 1.0 - dr["lbrow"]
            f = dr["f_ref"][rows, :]
            e = jnp.exp(-jnp.abs(f))
            r = 1.0 / (1.0 + e)
            er = e * r
            pos = f >= 0.0
            logf = jnp.log(lbf + oml * jnp.where(pos, r, er))
            kk = oml * jnp.where(pos, er, r)
            qh = q_ref[rows, :]
            gates.append((logf, kk, qh * _sigmoid(qh), i_ref[rows, :].astype(BF16)))
        gall = [jnp.exp(_dot_exact_lhs(dr["m_ref"][...], g[0])) for (dr, _), g in zip(work, gates)]
        amat = [dr["k_ref"][nl] * _dot_nt(g[2].astype(BF16), g[1].astype(BF16)) for (dr, _), g in zip(work, gates)]
        for lev in range(nl):
            for j, ((dr, _), g) in enumerate(zip(work, gates)):
                gl = gall[j][lev * c:(lev + 1) * c, :]
                amat[j] = amat[j] + dr["k_ref"][lev] * _dot_nt((g[2] * gl).astype(BF16), (g[1] * gl).astype(BF16))
        intra = [_dot(amat[j].astype(BF16), g[3]) for j, g in enumerate(gates)]
        upd = [_dot_tn(g[3], (g[1] * gall[j][(nl + 1) * c:(nl + 2) * c, :]).astype(BF16)) for j, g in enumerate(gates)]
        st = [st_scr[0], st_scr[1]]
        for j, ((dr, rows), g) in enumerate(zip(work, gates)):
            eb = gall[j][nl * c:(nl + 1) * c, :]
            d = dr["d"]
            dr["o_scr"][rows, :] = _dot_nt((g[2] * eb).astype(BF16), st[d].astype(BF16)) + intra[j]
            st[d] = st[d] * eb[dr["last_row"]:dr["last_row"] + 1, :] + upd[j]
        st_scr[0] = st[0]
        st_scr[1] = st[1]
        return carry

    lax.fori_loop(0, nc // unroll, body, 0)

    ep = min(256, seq)
    gn = gn_ref[...]

    def epilogue(j, carry):
        rows = pl.ds(pl.multiple_of(j * ep, ep), ep)
        o = of_scr[rows, :] + ob_scr[rows, :]
        y = o * lax.rsqrt(jnp.mean(o * o, axis=-1, keepdims=True) + EPS) * gn
        g = g_ref[rows, :]
        o_ref[rows, :] = (y * (g * _sigmoid(g))).astype(BF16)
        return carry

    lax.fori_loop(0, seq // ep, epilogue, 0)


def _hgrn(hg, lower, hg_norm3, consts, l, batch, seq):
    t = hg.shape[0]
    mf, mb, kf, kb = consts
    hd = HG_HEAD_DIM

    def col(g):
        return pl.BlockSpec((seq, hd), lambda b, h: (b, g * HG_HEADS + h))

    full2 = lambda a: pl.BlockSpec(a.shape, lambda b, h: (0, 0))
    full3 = lambda a: pl.BlockSpec(a.shape, lambda b, h: (0, 0, 0))
    return pl.pallas_call(
        functools.partial(_hgrn_kernel, seq=seq),
        out_shape=jax.ShapeDtypeStruct((t, HG_WIDTH), BF16),
        grid=(batch, HG_HEADS),
        in_specs=[
            col(0), col(1), col(2), col(3), col(4),
            pl.BlockSpec((None, 2, hd), lambda b, h: (l, 0, h)),
            pl.BlockSpec((None, 1, hd), lambda b, h: (l, 0, 0)),
            full2(mf), full2(mb), full3(kf), full3(kb),
        ],
        out_specs=pl.BlockSpec((seq, hd), lambda b, h: (b, h)),
        scratch_shapes=[
            pltpu.VMEM((seq, hd), F32),
            pltpu.VMEM((seq, hd), F32),
            pltpu.VMEM((2, hd, hd), F32),
        ],
        compiler_params=_params(("arbitrary", "arbitrary")),
        name="hgrn2_scan",
    )(hg, hg, hg, hg, hg, lower, hg_norm3, mf, mb, kf, kb)


def _outproj_kernel(a_ref, o_ref, x_ref, mod_ref, n2_ref, w_ref, wr_ref, br_ref,
                    xo_ref, h2_ref, ti_ref, gt_ref, *, d):
    mod = mod_ref[...]
    y = _dot(a_ref[...], w_ref[0:ATT_WIDTH, :]) + _dot(o_ref[...], w_ref[ATT_WIDTH:ATT_WIDTH + HG_WIDTH, :])
    xn = x_ref[...] + mod[:, 2 * d:3 * d] * y
    xo_ref[...] = xn
    h2 = _rms_mod(xn, n2_ref[...], mod[:, 4 * d:5 * d], mod[:, 3 * d:4 * d])
    h2_ref[...] = _pack_bf16_pairs(h2)
    lg = _dot_f32_nt(wr_ref[...], h2) + br_ref[...]
    eidx = lax.broadcasted_iota(I32, lg.shape, 0)
    vals, idxs = [], []
    for _ in range(TOP_K):
        mx = jnp.max(lg, axis=0, keepdims=True)
        sel = jnp.min(jnp.where(lg == mx, eidx, N_EXPERTS), axis=0, keepdims=True)
        vals.append(mx)
        idxs.append(sel)
        lg = jnp.where(eidx == sel, -jnp.inf, lg)
    ex = [jnp.exp(v - vals[0]) for v in vals]
    den = ex[0]
    for e in ex[1:]:
        den = den + e
    inv = 1.0 / den
    ti_ref[...] = jnp.concatenate(idxs, axis=0)
    gt_ref[...] = jnp.concatenate([e * inv for e in ex], axis=0)


def _outproj(a, o, x2, mod3, norm2, w_out_bf, w_router_t, b_router3, l, seq):
    t, d = x2.shape
    tm = min(PROJ_TILE, seq)
    nb = mod3.shape[0] // norm2.shape[0]
    tok = lambda i: (i, 0)
    lane_tok = lambda i: (0, i)
    return pl.pallas_call(
        functools.partial(_outproj_kernel, d=d),
        out_shape=(
            jax.ShapeDtypeStruct((t, d), F32),
            jax.ShapeDtypeStruct((t, d // 2), U32),
            jax.ShapeDtypeStruct((TOP_K, t), I32),
            jax.ShapeDtypeStruct((TOP_K, t), F32),
        ),
        grid=(t // tm,),
        in_specs=[
            pl.BlockSpec((tm, ATT_WIDTH), tok),
            pl.BlockSpec((tm, HG_WIDTH), tok),
            pl.BlockSpec((tm, d), tok),
            pl.BlockSpec((None, 1, N_MOD * d), lambda i: (l * nb + (i * tm) // seq, 0, 0)),
            pl.BlockSpec((None, 1, d), lambda i: (l, 0, 0)),
            pl.BlockSpec((None, ATT_WIDTH + HG_WIDTH, d), lambda i: (l, 0, 0)),
            pl.BlockSpec((None, N_EXPERTS, d), lambda i: (l, 0, 0)),
            pl.BlockSpec((None, N_EXPERTS, 1), lambda i: (l, 0, 0)),
        ],
        out_specs=(
            pl.BlockSpec((tm, d), tok),
            pl.BlockSpec((tm, d // 2), tok),
            pl.BlockSpec((TOP_K, tm), lane_tok),
            pl.BlockSpec((TOP_K, tm), lane_tok),
        ),
        compiler_params=_params(("arbitrary",)),
        name="out_proj_router",
    )(a, o, x2, mod3, norm2, w_out_bf, w_router_t, b_router3)


def _rank_kernel(ti_ref, tri_ref, rank_ref, cnt_ref, carry_scr):
    @pl.when(pl.program_id(0) == 0)
    def _():
        carry_scr[...] = jnp.zeros(carry_scr.shape, F32)

    ti = ti_ref[...]
    tl = ti.shape[1]
    eidx = lax.broadcasted_iota(I32, (N_EXPERTS, tl), 0)
    carry = carry_scr[...]
    rows = []
    for k in range(TOP_K):
        oh = eidx == ti[k:k + 1, :]
        ohf = jnp.where(oh, 1.0, 0.0)
        pre = _dot(ohf.astype(BF16), tri_ref[...])
        rows.append(jnp.sum(jnp.where(oh, carry + pre, 0.0), axis=0, keepdims=True))
        carry = carry + jnp.sum(ohf, axis=1, keepdims=True)
    carry_scr[...] = carry
    rank_ref[...] = jnp.concatenate(rows, axis=0).astype(I32)
    cnt_ref[...] = jnp.broadcast_to(carry, cnt_ref.shape)


def _ranks(topi):
    k, t = topi.shape
    tl = min(RANK_TILE, t)
    tri = np.triu(np.ones((tl, tl), np.float32), 1)
    return pl.pallas_call(
        _rank_kernel,
        out_shape=(jax.ShapeDtypeStruct((k, t), I32), jax.ShapeDtypeStruct((N_EXPERTS, LANES), F32)),
        grid=(t // tl,),
        in_specs=[pl.BlockSpec((k, tl), lambda i: (0, i)), pl.BlockSpec((tl, tl), lambda i: (0, 0))],
        out_specs=(pl.BlockSpec((k, tl), lambda i: (0, i)), pl.BlockSpec((N_EXPERTS, LANES), lambda i: (0, 0))),
        scratch_shapes=[pltpu.VMEM((N_EXPERTS, 1), F32)],
        compiler_params=_params(("arbitrary",)),
        name="route_rank",
    )(topi, jnp.asarray(tri, BF16))


def _dest_kernel(cnt_ref, ltri_ref, ti_ref, rank_ref, dest_ref, bexp_ref, nused_ref, eblk_ref):
    cnt = cnt_ref[...]
    nblk = jnp.floor((cnt + (MOE_ROWS - 1)) * (1.0 / MOE_ROWS))
    pstart_b = _dot(ltri_ref[...], nblk.astype(BF16))
    pend_b = pstart_b + nblk
    pstart = (pstart_b[:, 0:1] * MOE_ROWS).astype(I32)
    ti = ti_ref[...]
    tl = ti.shape[1]
    eidx = lax.broadcasted_iota(I32, (N_EXPERTS, tl), 0)
    rows = []
    for k in range(TOP_K):
        oh = eidx == ti[k:k + 1, :]
        rows.append(jnp.sum(jnp.where(oh, pstart, 0), axis=0, keepdims=True))
    dest_ref[...] = jnp.concatenate(rows, axis=0) + rank_ref[...]
    nb = bexp_ref.shape[1]
    bi = lax.broadcasted_iota(I32, (N_EXPERTS, nb), 1).astype(F32)
    be = jnp.sum(jnp.where(pend_b[:, 0:1] <= bi, 1, 0), axis=0, keepdims=True)
    bexp_ref[...] = jnp.minimum(be, N_EXPERTS - 1).astype(I32)
    nused_ref[...] = pend_b[N_EXPERTS - 1:N_EXPERTS, :].astype(I32)
    eblk_ref[0] = pstart_b.astype(I32)
    eblk_ref[1] = nblk.astype(I32)


def _destinations(counts, topi, rank, nblocks):
    k, t = topi.shape
    tl = min(2048, t)
    nbp = -(-nblocks // LANES) * LANES
    ltri = np.tril(np.ones((N_EXPERTS, N_EXPERTS), np.float32), -1)
    return pl.pallas_call(
        _dest_kernel,
        out_shape=(
            jax.ShapeDtypeStruct((k, t), I32),
            jax.ShapeDtypeStruct((1, nbp), I32),
            jax.ShapeDtypeStruct((1, LANES), I32),
            jax.ShapeDtypeStruct((2, N_EXPERTS, LANES), I32),
        ),
        grid=(t // tl,),
        in_specs=[
            pl.BlockSpec((N_EXPERTS, LANES), lambda i: (0, 0)),
            pl.BlockSpec((N_EXPERTS, N_EXPERTS), lambda i: (0, 0)),
            pl.BlockSpec((k, tl), lambda i: (0, i)),
            pl.BlockSpec((k, tl), lambda i: (0, i)),
        ],
        out_specs=(
            pl.BlockSpec((k, tl), lambda i: (0, i)),
            pl.BlockSpec((1, nbp), lambda i: (0, 0)),
            pl.BlockSpec((1, LANES), lambda i: (0, 0)),
            pl.BlockSpec((2, N_EXPERTS, LANES), lambda i: (0, 0, 0)),
        ),
        compiler_params=_params(("arbitrary",)),
        name="route_dest",
    )(counts, jnp.asarray(ltri, BF16), topi, rank)


def _inverse_kernel(bexp_ref, nused_ref, dest_hbm, inv_hbm, dsm0, dsm1, inv_sm, isem, osem, *, t, tm):
    i = pl.program_id(0)
    nt = pl.num_programs(0)
    nblk = inv_sm.shape[0] // MOE_ROWS
    nused = nused_ref[0]
    dsm = (dsm0, dsm1)

    def idx_copy(j, p):
        return pltpu.make_async_copy(dest_hbm.at[j], dsm[p], isem.at[p])

    @pl.when(i == 0)
    def _():
        idx_copy(0, 0).start()

        def init_block(b, carry):
            nxt = jnp.minimum(b + 1, nblk - 1)
            partial = jnp.logical_or(b >= nused - 1, bexp_ref[nxt] != bexp_ref[b])

            @pl.when(partial)
            def _():
                row0 = b * MOE_ROWS
                spare0 = TOP_K * t + (b % 2) * MOE_ROWS

                def init_row(r, c2):
                    inv_sm[row0 + r] = spare0 + r
                    return c2

                lax.fori_loop(0, MOE_ROWS, init_row, 0, unroll=8)

            return carry

        lax.fori_loop(0, nblk, init_block, 0)

    tok0 = i * tm
    for p in range(2):
        @pl.when(i % 2 == p)
        def _():
            @pl.when(i + 1 < nt)
            def _():
                idx_copy(i + 1, 1 - p).start()

            idx_copy(i, p).wait()

            def body(r, carry):
                for k in range(TOP_K):
                    inv_sm[dsm[p][k * tm + r]] = k * t + tok0 + r
                return carry

            lax.fori_loop(0, tm, body, 0, unroll=8)

    @pl.when(i == nt - 1)
    def _():
        out = pltpu.make_async_copy(inv_sm, inv_hbm, osem)
        out.start()
        out.wait()


def _inverse_map(bexp, nused, dest2, nblocks, t):
    nt, ktm = dest2.shape
    grid_spec = pltpu.PrefetchScalarGridSpec(
        num_scalar_prefetch=2,
        grid=(nt,),
        in_specs=[pl.BlockSpec(memory_space=pl.ANY)],
        out_specs=pl.BlockSpec(memory_space=pl.ANY),
        scratch_shapes=[
            pltpu.SMEM((ktm,), I32),
            pltpu.SMEM((ktm,), I32),
            pltpu.SMEM((nblocks * MOE_ROWS,), I32),
            pltpu.SemaphoreType.DMA((2,)),
            pltpu.SemaphoreType.DMA(()),
        ],
    )
    inv = pl.pallas_call(
        functools.partial(_inverse_kernel, t=t, tm=ktm // TOP_K),
        out_shape=jax.ShapeDtypeStruct((nblocks * MOE_ROWS,), I32),
        grid_spec=grid_spec,
        compiler_params=_params(("arbitrary",)),
        name="route_inverse",
    )(bexp, nused, dest2)
    return inv.reshape(nblocks, MOE_ROWS)


def _pack_bf16_pairs(x):
    n = x.shape[1] // 2
    lo = lax.bitcast_convert_type(x[:, :n].astype(BF16).astype(F32), U32)
    hi = lax.bitcast_convert_type(x[:, n:].astype(BF16).astype(F32), U32)
    return hi | (lo >> 16)


def _unpack_bf16_pairs(w):
    lo = lax.bitcast_convert_type(w << 16, F32)
    hi = lax.bitcast_convert_type(w & jnp.uint32(0xFFFF0000), F32)
    return lo, hi


def _expert_kernel(bexp_ref, nused_ref, inv_hbm, h2_hbm, wgu_ref, bgu_ref, wd_ref, bd_ref, y4_hbm,
                   ig0, ig1, is0, is1, xbuf, ybuf, wgu_bf, wd_bf, gisem, sisem, gsem, ssem, *, dff, t):
    i = pl.program_id(0)
    nblk = pl.num_programs(0)
    nused = nused_ref[0]
    rows = xbuf.shape[1]
    half = xbuf.shape[2]
    used = i < nused
    fresh = jnp.logical_or(i == 0, bexp_ref[i] != bexp_ref[jnp.maximum(i - 1, 0)])
    gidx, sidx = (ig0, ig1), (is0, is1)

    def gidx_copy(j, p):
        return pltpu.make_async_copy(inv_hbm.at[j], gidx[p], gisem.at[p])

    def sidx_copy(j, p):
        return pltpu.make_async_copy(inv_hbm.at[j], sidx[p], sisem.at[p])

    def gather_rows(p):
        def body(rr, carry):
            for u in range(2):
                r = rr * 2 + u
                slot_row = gidx[p][r]
                tok = slot_row & (t - 1) if t & (t - 1) == 0 else lax.rem(slot_row, t)
                pltpu.make_async_copy(h2_hbm.at[pl.ds(tok, 1), :], xbuf.at[p, pl.ds(r, 1), :],
                                      gsem.at[p]).start(priority=u)
            return carry

        for rr in range(rows // 2):
            body(rr, 0)

    def gather_wait(p):
        pltpu.make_async_copy(h2_hbm.at[pl.ds(0, rows), :], xbuf.at[p], gsem.at[p]).wait()

    def scatter_rows(p):
        def body(rr, carry):
            for u in range(2):
                r = rr * 2 + u
                pltpu.make_async_copy(ybuf.at[p, pl.ds(r, 1), :], y4_hbm.at[pl.ds(sidx[p][r], 1), :],
                                      ssem.at[p]).start(priority=u)
            return carry

        for rr in range(rows // 2):
            body(rr, 0)

    def scatter_wait(p):
        pltpu.make_async_copy(ybuf.at[p], y4_hbm.at[pl.ds(0, rows), :], ssem.at[p]).wait()

    @pl.when(i == 0)
    def _():
        ybuf[...] = jnp.zeros(ybuf.shape, U32)
        for s in range(2):
            spare = pltpu.make_async_copy(ybuf.at[s], y4_hbm.at[pl.ds(TOP_K * t + s * rows, rows), :], ssem.at[s])
            spare.start()
            spare.wait()
        gidx_copy(0, 0).start()
        sidx_copy(0, 0).start()

        @pl.when(nused > 1)
        def _():
            gidx_copy(1, 1).start()

        gidx_copy(0, 0).wait()
        gather_rows(0)

    @pl.when(jnp.logical_and(used, fresh))
    def _():
        wgu_bf[...] = wgu_ref[...].astype(BF16)
        wd_bf[...] = wd_ref[...].astype(BF16)

    for p in range(2):
        q = 1 - p

        @pl.when(i % 2 == p)
        def _():
            @pl.when(i + 1 < nused)
            def _():
                gidx_copy(i + 1, q).wait()
                gather_rows(q)
                sidx_copy(i + 1, q).start()

            @pl.when(i + 2 < nused)
            def _():
                gidx_copy(i + 2, p).start()

            @pl.when(jnp.logical_and(i >= 2, i - 2 < nused))
            def _():
                scatter_wait(p)

            @pl.when(used)
            def _():
                gather_wait(p)
                lo, hi = _unpack_bf16_pairs(xbuf[p])
                hu = (_dot(lo.astype(BF16), wgu_bf[0:half, :]) + _dot(hi.astype(BF16), wgu_bf[half:2 * half, :])
                      + bgu_ref[...])
                glu = jnp.minimum(hu[:, 0:dff], SWIGLU_LIMIT)
                lin = jnp.clip(hu[:, dff:2 * dff], -SWIGLU_LIMIT, SWIGLU_LIMIT)
                act = glu * _sigmoid(SWIGLU_ALPHA * glu) * (lin + 1.0)
                ybuf[p] = _pack_bf16_pairs(_dot(act.astype(BF16), wd_bf[...]) + bd_ref[...])
                sidx_copy(i, p).wait()
                scatter_rows(p)

            @pl.when(i == nblk - 1)
            def _():
                @pl.when(jnp.logical_and(i >= 1, i - 1 < nused))
                def _():
                    scatter_wait(q)

                @pl.when(used)
                def _():
                    scatter_wait(p)


def _experts(bexp, nused, inv, h2p, w_gu, b_gu4, w_down, b_down4, l, t):
    nblocks = inv.shape[0]
    half = h2p.shape[1]
    d = 2 * half
    dff = w_down.shape[2]

    wsel = lambda i, be, nu: (l, be[jnp.minimum(i, nu[0] - 1)], 0, 0)
    grid_spec = pltpu.PrefetchScalarGridSpec(
        num_scalar_prefetch=2,
        grid=(nblocks,),
        in_specs=[
            pl.BlockSpec(memory_space=pl.ANY),
            pl.BlockSpec(memory_space=pl.ANY),
            pl.BlockSpec((None, None, d, 2 * dff), wsel),
            pl.BlockSpec((None, None, 1, 2 * dff), wsel),
            pl.BlockSpec((None, None, dff, d), wsel),
            pl.BlockSpec((None, None, 1, d), wsel),
        ],
        out_specs=pl.BlockSpec(memory_space=pl.ANY),
        scratch_shapes=[pltpu.SMEM((MOE_ROWS,), I32)] * 4 + [
            pltpu.VMEM((2, MOE_ROWS, half), U32),
            pltpu.VMEM((2, MOE_ROWS, half), U32),
            pltpu.VMEM((d, 2 * dff), BF16),
            pltpu.VMEM((dff, d), BF16),
        ] + [pltpu.SemaphoreType.DMA((2,))] * 4,
    )
    return pl.pallas_call(
        functools.partial(_expert_kernel, dff=dff, t=t),
        out_shape=jax.ShapeDtypeStruct((TOP_K * t + 2 * MOE_ROWS, half), U32),
        grid_spec=grid_spec,
        compiler_params=_params(("arbitrary",)),
        name="moe_experts",
    )(bexp, nused, inv, h2p, w_gu, b_gu4, w_down, b_down4)


def _combine_kernel(*refs, d, final):
    y_refs = refs[:TOP_K]
    gt_ref, x_ref, mod_ref = refs[TOP_K:TOP_K + 3]
    xo_ref = refs[-1]
    half = d // 2
    gt = gt_ref[...]
    acc_lo = acc_hi = None
    for k in range(TOP_K):
        lo, hi = _unpack_bf16_pairs(y_refs[k][...])
        g = gt[:, k:k + 1]
        acc_lo = g * lo if acc_lo is None else acc_lo + g * lo
        acc_hi = g * hi if acc_hi is None else acc_hi + g * hi
    x_lo = x_ref[:, 0:half] + mod_ref[:, 5 * d:5 * d + half] * acc_lo
    x_hi = x_ref[:, half:d] + mod_ref[:, 5 * d + half:6 * d] * acc_hi
    if final:
        gain = refs[TOP_K + 3][...]
        ssq = jnp.sum(x_lo * x_lo, axis=-1, keepdims=True) + jnp.sum(x_hi * x_hi, axis=-1, keepdims=True)
        inv = lax.rsqrt(ssq * (1.0 / d) + EPS)
        x_lo = x_lo * inv * gain[:, 0:half]
        x_hi = x_hi * inv * gain[:, half:d]
    xo_ref[:, 0:half] = x_lo
    xo_ref[:, half:d] = x_hi


def _combine(y4p, gates_t, x2, mod3, l, seq, depth, final_gain=None):
    t, d = x2.shape
    tm = min(PROJ_TILE, seq)
    nb = mod3.shape[0] // depth
    nt = t // tm
    final = final_gain is not None

    def slot(k):
        return pl.BlockSpec((tm, d // 2), lambda i: (k * nt + i, 0))

    extra_specs = [pl.BlockSpec((1, d), lambda i: (0, 0))] if final else []
    extra_args = [final_gain.reshape(1, d)] if final else []
    return pl.pallas_call(
        functools.partial(_combine_kernel, d=d, final=final),
        out_shape=jax.ShapeDtypeStruct((t, d), F32),
        grid=(nt,),
        in_specs=[slot(k) for k in range(TOP_K)] + [
            pl.BlockSpec((tm, TOP_K), lambda i: (i, 0)),
            pl.BlockSpec((tm, d), lambda i: (i, 0)),
            pl.BlockSpec((None, 1, N_MOD * d), lambda i: (l * nb + (i * tm) // seq, 0, 0)),
        ] + extra_specs,
        out_specs=pl.BlockSpec((tm, d), lambda i: (i, 0)),
        compiler_params=_params(("arbitrary",)),
        name="moe_combine",
    )(*([y4p] * TOP_K), gates_t, x2, mod3, *extra_args)


def _final_kernel(x_ref, g_ref, o_ref):
    x = x_ref[...]
    o_ref[...] = x * lax.rsqrt(jnp.mean(x * x, axis=-1, keepdims=True) + EPS) * g_ref[...]


def _final_norm(x2, gain):
    t, d = x2.shape
    tm = min(PROJ_TILE, t)
    return pl.pallas_call(
        _final_kernel,
        out_shape=jax.ShapeDtypeStruct((t, d), F32),
        grid=(t // tm,),
        in_specs=[pl.BlockSpec((tm, d), lambda i: (i, 0)), pl.BlockSpec((1, d), lambda i: (0, 0))],
        out_specs=pl.BlockSpec((tm, d), lambda i: (i, 0)),
        compiler_params=_params(("arbitrary",)),
        name="final_norm",
    )(x2, gain.reshape(1, d))


def _mixer_layer(x2, mod3, tables, lower, consts, p, l, batch, seq):
    ra, rb, rc = tables
    qa, ka2, va2, hg = _inproj(x2, mod3, p["norm1"], p["w_in"], ra, rb, rc, l, seq)
    a = _attention(qa, ka2, va2, p["attn_sink"], p["attn_norm"], l, batch, seq)
    o = _hgrn(hg, lower, p["hg_norm"], consts, l, batch, seq)
    return a, o


SC_WINDOW = 128


def _sc_mesh():
    return plsc.VectorSubcoreMesh(core_axis_name="core", subcore_axis_name="subcore")


def _sc_worker(mesh):
    return lax.axis_index("core") * mesh.num_subcores + lax.axis_index("subcore"), mesh.num_cores * mesh.num_subcores


def _sc_scatter_rows(src, dest_rows, n_out):
    n, width = src.shape
    nk = dest_rows.shape[0]
    mesh = _sc_mesh()

    half = SC_WINDOW // 2

    @pl.kernel(out_type=jax.ShapeDtypeStruct((n_out, width), src.dtype), mesh=mesh,
               scratch_types=[pltpu.VMEM((nk, SC_WINDOW), I32), pltpu.VMEM((2, half, width), src.dtype),
                              pltpu.SemaphoreType.DMA((2,))],
               name="sc_scatter_rows")
    def scatter_kernel(src_hbm, idx_hbm, out_hbm, idx_v, rows_v, sem):
        wid, nw = _sc_worker(mesh)
        per = n // SC_WINDOW // nw
        row0 = wid * per * SC_WINDOW

        def load(s, buf):
            return pltpu.make_async_copy(src_hbm.at[pl.ds(row0 + s * half, half)], rows_v.at[buf], sem.at[buf])

        load(0, 0).start()

        @pl.loop(0, per)
        def _(j):
            pltpu.sync_copy(idx_hbm.at[:, pl.ds(row0 + j * SC_WINDOW, SC_WINDOW)], idx_v)
            for h in range(2):
                if h == 0:
                    load(2 * j + 1, 1).start()
                else:
                    @pl.when(j + 1 < per)
                    def _():
                        load(2 * j + 2, 0).start()

                load(2 * j + h, h).wait()
                for k in range(nk):
                    pltpu.sync_copy(rows_v.at[h], out_hbm.at[idx_v.at[k, pl.ds(h * half, half)]])

    return scatter_kernel(src, dest_rows)


def _sc_gather_rows(src, rows):
    n = rows.shape[0]
    width = src.shape[1]
    mesh = _sc_mesh()

    half = SC_WINDOW // 2

    @pl.kernel(out_type=jax.ShapeDtypeStruct((n, width), src.dtype), mesh=mesh,
               scratch_types=[pltpu.VMEM((1, SC_WINDOW), I32), pltpu.VMEM((2, half, width), src.dtype),
                              pltpu.SemaphoreType.DMA((2,))],
               name="sc_gather_rows")
    def gather_kernel(src_hbm, idx_hbm, out_hbm, idx_v, rows_v, sem):
        wid, nw = _sc_worker(mesh)
        per = n // SC_WINDOW // nw
        row0 = wid * per * SC_WINDOW

        def store(s, buf):
            return pltpu.make_async_copy(rows_v.at[buf], out_hbm.at[pl.ds(row0 + s * half, half)], sem.at[buf])

        @pl.loop(0, per)
        def _(j):
            pltpu.sync_copy(idx_hbm.at[:, pl.ds(row0 + j * SC_WINDOW, SC_WINDOW)], idx_v)
            for h in range(2):
                @pl.when(j >= 1)
                def _():
                    store(2 * j + h - 2, h).wait()

                pltpu.sync_copy(src_hbm.at[idx_v.at[0, pl.ds(h * half, half)]], rows_v.at[h])
                store(2 * j + h, h).start()

        for h in range(2):
            store(2 * per - 2 + h, h).wait()

    return gather_kernel(src, rows.reshape(1, n))


def _expert_group_kernel(eb_ref, xs_hbm, wgu_ref, bgu_ref, wd_ref, bd_ref, ys_hbm,
                         xbuf, ybuf, wgu_bf, wd_bf, xsem, ysem, *, dff):
    e = pl.program_id(0)
    ne = pl.num_programs(0)
    b0 = eb_ref[e]
    nb = eb_ref[ne + e]
    rows, half = xbuf.shape[1], xbuf.shape[2]
    group = rows // MOE_ROWS
    nch = (nb + group - 1) // group

    def x_copy(j, slot):
        start = (b0 + j * group) * MOE_ROWS
        return pltpu.make_async_copy(xs_hbm.at[pl.ds(start, rows), :], xbuf.at[slot], xsem.at[slot])

    def y_copy(j, slot, g):
        start = (b0 + j * group + g) * MOE_ROWS
        return pltpu.make_async_copy(ybuf.at[slot, pl.ds(g * MOE_ROWS, MOE_ROWS), :],
                                     ys_hbm.at[pl.ds(start, MOE_ROWS), :], ysem.at[slot, g])

    def y_each(j, slot, fn):
        for g in range(group):
            @pl.when(j * group + g < nb)
            def _():
                fn(y_copy(j, slot, g))

    @pl.when(nb > 0)
    def _():
        x_copy(0, 0).start(priority=1)
        wgu_bf[...] = wgu_ref[...].astype(BF16)
        wd_bf[...] = wd_ref[...].astype(BF16)

        def chunk(j, carry):
            slot = j % 2

            @pl.when(j + 1 < nch)
            def _():
                x_copy(j + 1, 1 - slot).start(priority=1)

            x_copy(j, slot).wait()

            @pl.when(j >= 2)
            def _():
                y_each(j - 2, slot, lambda cp: cp.wait())

            lo, hi = _unpack_bf16_pairs(xbuf[slot])
            hu = (_dot(lo.astype(BF16), wgu_bf[0:half, :]) + _dot(hi.astype(BF16), wgu_bf[half:2 * half, :])
                  + bgu_ref[...])
            glu = jnp.minimum(hu[:, 0:dff], SWIGLU_LIMIT)
            lin = jnp.clip(hu[:, dff:2 * dff], -SWIGLU_LIMIT, SWIGLU_LIMIT)
            act = glu * _sigmoid(SWIGLU_ALPHA * glu) * (lin + 1.0)
            ybuf[slot] = _pack_bf16_pairs(_dot(act.astype(BF16), wd_bf[...]) + bd_ref[...])
            y_each(j, slot, lambda cp: cp.start(priority=1))
            return carry

        lax.fori_loop(0, nch, chunk, 0)

        @pl.when(nch >= 2)
        def _():
            y_each(nch - 2, nch % 2, lambda cp: cp.wait())

        y_each(nch - 1, (nch - 1) % 2, lambda cp: cp.wait())


def _block_experts(eblk, xs, w_gu, b_gu4, w_down, b_down4, l):
    cap, half = xs.shape
    d = 2 * half
    dff = w_down.shape[2]
    wsel = lambda e, eb: (l, e, 0, 0)
    grid_spec = pltpu.PrefetchScalarGridSpec(
        num_scalar_prefetch=1,
        grid=(N_EXPERTS,),
        in_specs=[
            pl.BlockSpec(memory_space=pl.ANY),
            pl.BlockSpec((None, None, d, 2 * dff), wsel),
            pl.BlockSpec((None, None, 1, 2 * dff), wsel),
            pl.BlockSpec((None, None, dff, d), wsel),
            pl.BlockSpec((None, None, 1, d), wsel),
        ],
        out_specs=pl.BlockSpec(memory_space=pl.ANY),
        scratch_shapes=[
            pltpu.VMEM((2, EXPERT_GROUP * MOE_ROWS, half), U32),
            pltpu.VMEM((2, EXPERT_GROUP * MOE_ROWS, half), U32),
            pltpu.VMEM((d, 2 * dff), BF16),
            pltpu.VMEM((dff, d), BF16),
            pltpu.SemaphoreType.DMA((2,)),
            pltpu.SemaphoreType.DMA((2, EXPERT_GROUP)),
        ],
    )
    return pl.pallas_call(
        functools.partial(_expert_group_kernel, dff=dff),
        out_shape=jax.ShapeDtypeStruct((cap - (EXPERT_GROUP - 1) * MOE_ROWS, half), U32),
        grid_spec=grid_spec,
        compiler_params=_params(("arbitrary",)),
        name="moe_expert_groups",
    )(eblk, xs, w_gu, b_gu4, w_down, b_down4)


def _moe_layer_sc(a, o, x2, mod3, p, l, seq, depth, final_gain=None):
    t, d = x2.shape
    xn, h2p, topi, gates = _outproj(a, o, x2, mod3, p["norm2"], p["w_out"], p["w_router_t"], p["b_router"], l, seq)
    rank, counts = _ranks(topi)
    nblocks = (t * TOP_K + N_EXPERTS * MOE_ROWS) // MOE_ROWS
    dest, _, _, eblk = _destinations(counts, topi, rank, nblocks)
    xs = _sc_scatter_rows(h2p, dest, (nblocks + EXPERT_GROUP - 1) * MOE_ROWS)
    ys = _block_experts(eblk[:, :, 0].reshape(-1), xs, p["w_gu"], p["b_gu"], p["w_down"], p["b_down"], l)
    y4p = _sc_gather_rows(ys, dest.reshape(-1))
    return _combine(y4p, gates.T, xn, mod3, l, seq, depth, final_gain)


def _moe_layer(a, o, x2, mod3, p, l, seq, depth):
    t, d = x2.shape
    xn, h2p, topi, gates = _outproj(a, o, x2, mod3, p["norm2"], p["w_out"], p["w_router_t"], p["b_router"], l, seq)
    rank, counts = _ranks(topi)
    nblocks = (t * TOP_K + N_EXPERTS * MOE_ROWS) // MOE_ROWS
    dest, bexp, nused, _ = _destinations(counts, topi, rank, nblocks)
    tm = min(TOK_TILE, t)
    dest2 = dest.reshape(TOP_K, t // tm, tm).transpose(1, 0, 2).reshape(t // tm, TOP_K * tm)
    bexp, nused = bexp.reshape(-1), nused.reshape(-1)
    inv = _inverse_map(bexp, nused, dest2, nblocks, t)
    y4p = _experts(bexp, nused, inv, h2p, p["w_gu"], p["b_gu"], p["w_down"], p["b_down"], l, t)
    return _combine(y4p, gates.T, xn, mod3, l, seq, depth)


def kernel(x, c, positions, w_ada, b_ada, norm1, w_in, attn_sink, attn_norm, hg_lb_logits, hg_norm, w_out, norm2,
           w_router, b_router, w_gu, b_gu, w_down, b_down, final_norm):
    batch, seq, d = x.shape
    depth = w_ada.shape[0]
    t = batch * seq
    p = {
        "norm1": norm1.reshape(depth, 1, d),
        "w_in": w_in.astype(BF16),
        "attn_sink": attn_sink.astype(F32),
        "attn_norm": attn_norm.reshape(depth, 1, ATT_WIDTH),
        "hg_norm": hg_norm.reshape(depth, 1, HG_HEAD_DIM),
        "w_out": w_out.astype(BF16),
        "norm2": norm2.reshape(depth, 1, d),
        "w_router_t": jnp.swapaxes(w_router, 1, 2),
        "b_router": b_router.reshape(depth, N_EXPERTS, 1),
        "w_gu": w_gu,
        "b_gu": b_gu.reshape(depth, N_EXPERTS, 1, b_gu.shape[-1]),
        "w_down": w_down,
        "b_down": b_down.reshape(depth, N_EXPERTS, 1, d),
    }
    mod3 = _ada_all(c, w_ada, b_ada).reshape(depth * batch, 1, N_MOD * d)
    lower = _lower_bounds(hg_lb_logits)
    tables = _rope_tables(positions)
    consts = _hgrn_constants()
    consts = (jnp.asarray(consts[0], BF16), jnp.asarray(consts[1], BF16), jnp.asarray(consts[2]), jnp.asarray(consts[3]))
    x2 = x.reshape(t, d)
    for l in range(depth):
        a, o = _mixer_layer(x2, mod3, tables, lower, consts, p, l, batch, seq)
        x2 = _moe_layer_sc(a, o, x2, mod3, p, l, seq, depth, final_norm if l == depth - 1 else None)
    return x2.reshape(batch, seq, d)
```

```python
import functools

import numpy as np
import jax
import jax.numpy as jnp
from jax import lax
from jax.experimental import pallas as pl
from jax.experimental.pallas import tpu as pltpu
from jax.experimental.pallas import tpu_sc as plsc

F32 = jnp.float32
BF16 = jnp.bfloat16
I32 = jnp.int32
U32 = jnp.uint32

ATT_HEADS = 8
ATT_KV_HEADS = 2
ATT_HEAD_DIM = 64
ATT_WIDTH = ATT_HEADS * ATT_HEAD_DIM
KV_WIDTH = ATT_KV_HEADS * ATT_HEAD_DIM
WINDOW = 128
ATT_BLOCK = 128
ROPE_THETA = 500000.0
ROPE_DIM = ATT_HEAD_DIM // 4
HG_HEADS = 4
HG_HEAD_DIM = 128
HG_WIDTH = HG_HEADS * HG_HEAD_DIM
N_EXPERTS = 32
TOP_K = 4
SWIGLU_ALPHA = 1.702
SWIGLU_LIMIT = 7.0
N_MOD = 6
EPS = 1e-6
NEG_INF = -1e30
LB_FLOOR = 1e-30

LANES = 128
HG_CHUNK = 64
HG_LEVELS = 6
HG_UNROLL = 8
MOE_SHIFT = 8
MOE_ROWS = 1 << MOE_SHIFT
EXPERT_GROUP = 1
WEIGHT_CHUNKS = 4
TOK_TILE = 256
PROJ_TILE = 512
RANK_TILE = 512
VMEM_LIMIT = 56 * 1024 * 1024


def _dot(a, b):
    return jnp.dot(a, b, preferred_element_type=F32)


def _dot_nt(a, b):
    return lax.dot_general(a, b, (((1,), (1,)), ((), ())), preferred_element_type=F32)


def _dot_tn(a, b):
    return lax.dot_general(a, b, (((0,), (0,)), ((), ())), preferred_element_type=F32)


def _split3(x):
    hi = x.astype(BF16)
    r1 = x - hi.astype(F32)
    mid = r1.astype(BF16)
    lo = (r1 - mid.astype(F32)).astype(BF16)
    return hi, mid, lo


def _dot_exact_lhs(m_bf16, x):
    hi, mid, lo = _split3(x)
    return _dot(m_bf16, hi) + _dot(m_bf16, mid) + _dot(m_bf16, lo)


def _dot_f32_nt(a, b):
    ah, am, _ = _split3(a)
    bh, bm, _ = _split3(b)
    return _dot_nt(ah, bh) + _dot_nt(ah, bm) + _dot_nt(am, bh)


def _dot_f32(a, b):
    ah, am, al = _split3(a)
    bh, bm, bl = _split3(b)
    return (_dot(ah, bh) + _dot(ah, bm) + _dot(am, bh)
            + _dot(ah, bl) + _dot(am, bm) + _dot(al, bh))


def _sigmoid(x):
    return 1.0 / (1.0 + jnp.exp(-x))


def _params(sem=None):
    return pltpu.CompilerParams(dimension_semantics=sem, vmem_limit_bytes=VMEM_LIMIT)


def _ada_kernel(c_ref, w_ref, b_ref, o_ref):
    c = c_ref[...]
    cond = c * _sigmoid(c)
    o_ref[...] = _dot_f32(cond, w_ref[...]) + b_ref[...]


def _ada_all(c, w_ada, b_ada):
    depth, d, n = w_ada.shape
    b = c.shape[0]
    nt = n // d
    return pl.pallas_call(
        _ada_kernel,
        out_shape=jax.ShapeDtypeStruct((depth, b, n), F32),
        grid=(depth, nt),
        in_specs=[
            pl.BlockSpec((b, d), lambda l, j: (0, 0)),
            pl.BlockSpec((None, d, d), lambda l, j: (l, 0, j)),
            pl.BlockSpec((None, 1, d), lambda l, j: (l, 0, j)),
        ],
        out_specs=pl.BlockSpec((None, b, d), lambda l, j: (l, 0, j)),
        compiler_params=_params(("arbitrary", "arbitrary")),
        name="ada_mod",
    )(c, w_ada, b_ada.reshape(depth, 1, n))


def _lb_kernel(x_ref, o_ref):
    depth = x_ref.shape[0]
    xs = [x_ref[l] for l in range(depth)]
    m = xs[0]
    for l in range(1, depth):
        m = jnp.maximum(m, xs[l])
    es = [jnp.exp(v - m) for v in xs]
    den = es[0]
    for l in range(1, depth):
        den = den + es[l]
    ps = [e / den for e in es]
    run = ps[0]
    o_ref[0] = run - ps[0]
    for l in range(1, depth):
        run = run + ps[l]
        o_ref[l] = run - ps[0]


def _lower_bounds(hg_lb_logits):
    return pl.pallas_call(
        _lb_kernel,
        out_shape=jax.ShapeDtypeStruct(hg_lb_logits.shape, F32),
        name="hg_lower_bounds",
    )(hg_lb_logits.astype(F32))


def _rope_kernel(pos_ref, invf_ref, a_ref, b_ref, c_ref):
    pos = pos_ref[...].astype(F32)
    ang = pos * invf_ref[...]
    cs = jnp.cos(ang)
    sn = jnp.sin(ang)
    lane = lax.broadcasted_iota(I32, ang.shape, 1) & (ATT_HEAD_DIM - 1)
    half = ROPE_DIM // 2
    first = lane < half
    second = (lane >= half) & (lane < ROPE_DIM)
    a_ref[...] = jnp.where(first | second, cs, 1.0)
    b_ref[...] = jnp.where(first, -sn, 0.0)
    c_ref[...] = jnp.where(second, sn, 0.0)


def _rope_tables(positions):
    t = positions.size
    half = ROPE_DIM // 2
    inv = (np.float32(ROPE_THETA) ** (-(np.arange(half, dtype=np.float32) * np.float32(2.0) / np.float32(ROPE_DIM)))).astype(np.float32)
    lane = np.arange(LANES) % ATT_HEAD_DIM
    pat = np.where(lane < ROPE_DIM, inv[lane % half], 0.0).astype(np.float32).reshape(1, LANES)
    tm = min(t, 2048)
    shp = jax.ShapeDtypeStruct((t, LANES), F32)
    spec = pl.BlockSpec((tm, LANES), lambda i: (i, 0))
    return pl.pallas_call(
        _rope_kernel,
        out_shape=(shp, shp, shp),
        grid=(t // tm,),
        in_specs=[pl.BlockSpec((tm, 1), lambda i: (i, 0)), pl.BlockSpec((1, LANES), lambda i: (0, 0))],
        out_specs=(spec, spec, spec),
        compiler_params=_params(("arbitrary",)),
        name="rope_tables",
    )(positions.reshape(t, 1).astype(I32), jnp.asarray(pat))


def _rms_mod(x, gain, scale, shift):
    ms = jnp.mean(x * x, axis=-1, keepdims=True)
    return (x * lax.rsqrt(ms + EPS) * gain) * (1.0 + scale) + shift


def _rope_apply(x, a, b, c):
    half = ROPE_DIM // 2
    return x * a + pltpu.roll(x, LANES - half, 1) * b + pltpu.roll(x, half, 1) * c


def _inproj_kernel(x_ref, mod_ref, n1_ref, w_ref, ra_ref, rb_ref, rc_ref, qa_ref, ka_ref, va_ref, hg_ref, *, d):
    mod = mod_ref[...]
    h = _rms_mod(x_ref[...], n1_ref[...], mod[:, d:2 * d], mod[:, 0:d]).astype(BF16)
    a, b, c = ra_ref[...], rb_ref[...], rc_ref[...]
    kvw = ATT_WIDTH + 2 * KV_WIDTH
    pa = _dot(h, w_ref[:, 0:kvw])
    scale = ATT_HEAD_DIM ** -0.5
    for g in range(ATT_WIDTH // LANES):
        qg = _rope_apply(pa[:, g * LANES:(g + 1) * LANES], a, b, c)
        qa_ref[:, g * LANES:(g + 1) * LANES] = (qg * scale).astype(BF16)
    k = _rope_apply(pa[:, ATT_WIDTH:ATT_WIDTH + KV_WIDTH], a, b, c)
    v = pa[:, ATT_WIDTH + KV_WIDTH:kvw]
    ka_ref[:, 0:LANES] = k.astype(BF16)
    ka_ref[:, LANES:2 * LANES] = pltpu.roll(k, ATT_HEAD_DIM, 1).astype(BF16)
    va_ref[:, 0:LANES] = v.astype(BF16)
    va_ref[:, LANES:2 * LANES] = pltpu.roll(v, ATT_HEAD_DIM, 1).astype(BF16)
    for g in range(5):
        lo = kvw + g * HG_WIDTH
        hg_ref[:, g * HG_WIDTH:(g + 1) * HG_WIDTH] = _dot(h, w_ref[:, lo:lo + HG_WIDTH])


def _inproj(x2, mod3, norm1, w_in_bf, ra, rb, rc, l, seq):
    t, d = x2.shape
    n_in = w_in_bf.shape[-1]
    tm = min(PROJ_TILE, seq)
    nb = mod3.shape[0] // norm1.shape[0]
    tok = lambda i: (i, 0)
    return pl.pallas_call(
        functools.partial(_inproj_kernel, d=d),
        out_shape=(
            jax.ShapeDtypeStruct((t, ATT_WIDTH), BF16),
            jax.ShapeDtypeStruct((t, 2 * KV_WIDTH), BF16),
            jax.ShapeDtypeStruct((t, 2 * KV_WIDTH), BF16),
            jax.ShapeDtypeStruct((t, 5 * HG_WIDTH), F32),
        ),
        grid=(t // tm,),
        in_specs=[
            pl.BlockSpec((tm, d), tok),
            pl.BlockSpec((None, 1, N_MOD * d), lambda i: (l * nb + (i * tm) // seq, 0, 0)),
            pl.BlockSpec((None, 1, d), lambda i: (l, 0, 0)),
            pl.BlockSpec((None, d, n_in), lambda i: (l, 0, 0)),
            pl.BlockSpec((tm, LANES), tok),
            pl.BlockSpec((tm, LANES), tok),
            pl.BlockSpec((tm, LANES), tok),
        ],
        out_specs=(
            pl.BlockSpec((tm, ATT_WIDTH), tok),
            pl.BlockSpec((tm, 2 * KV_WIDTH), tok),
            pl.BlockSpec((tm, 2 * KV_WIDTH), tok),
            pl.BlockSpec((tm, 5 * HG_WIDTH), tok),
        ),
        compiler_params=_params(("arbitrary",)),
        name="in_proj",
    )(x2, mod3, norm1, w_in_bf, ra, rb, rc)


def _attn_kernel(sink_ref, q_ref, kp_ref, kc_ref, kn_ref, vp_ref, vc_ref, vn_ref, gain_ref, o_ref, *, l, seq):
    n = pl.program_id(1)
    blk = ATT_BLOCK
    k2 = jnp.concatenate([kp_ref[...], kc_ref[...], kn_ref[...]], axis=0)
    v2 = jnp.concatenate([vp_ref[...], vc_ref[...], vn_ref[...]], axis=0)
    lane = lax.broadcasted_iota(I32, (3 * blk, LANES), 1)
    lo_half = lane < ATT_HEAD_DIM
    zero = jnp.zeros((3 * blk, LANES), BF16)
    ka, kb = k2[:, 0:LANES], k2[:, LANES:2 * LANES]
    va, vb = v2[:, 0:LANES], v2[:, LANES:2 * LANES]
    kz = [[jnp.where(lo_half, ka, zero), jnp.where(lo_half, zero, kb)],
          [jnp.where(lo_half, kb, zero), jnp.where(lo_half, zero, ka)]]
    vz = [[jnp.where(lo_half, va, zero), jnp.where(lo_half, zero, vb)],
          [jnp.where(lo_half, vb, zero), jnp.where(lo_half, zero, va)]]
    qpos = n * blk + lax.broadcasted_iota(I32, (blk, 3 * blk), 0)
    kpos = (n - 1) * blk + lax.broadcasted_iota(I32, (blk, 3 * blk), 1)
    valid = (jnp.abs(qpos - kpos) <= WINDOW) & (kpos >= 0) & (kpos < seq)
    valid2 = jnp.concatenate([valid, valid], axis=0)
    upper = lax.broadcasted_iota(I32, (2 * blk, 1), 0) < blk
    outs = []
    for j in range(ATT_KV_HEADS):
        qs = jnp.concatenate([q_ref[:, 2 * j * LANES:(2 * j + 1) * LANES],
                              q_ref[:, (2 * j + 1) * LANES:(2 * j + 2) * LANES]], axis=0)
        acc = None
        for half in range(2):
            s = _dot_nt(qs, kz[j][half])
            s = jnp.where(valid2, s, NEG_INF)
            sink = jnp.where(upper, sink_ref[l, 4 * j + half], sink_ref[l, 4 * j + 2 + half])
            mx = jnp.maximum(jnp.max(s, axis=-1, keepdims=True), sink)
            p = jnp.exp(s - mx)
            den = jnp.sum(p, axis=-1, keepdims=True) + jnp.exp(sink - mx)
            p = (p * (1.0 / den)).astype(BF16)
            pv = _dot(p, vz[j][half])
            acc = pv if acc is None else acc + pv
        outs.append(acc[0:blk])
        outs.append(acc[blk:2 * blk])
    o = jnp.concatenate(outs, axis=-1)
    ms = jnp.mean(o * o, axis=-1, keepdims=True)
    o_ref[...] = (o * lax.rsqrt(ms + EPS) * gain_ref[...]).astype(BF16)


def _attention(qa, ka2, va2, attn_sink, attn_norm3, l, batch, seq):
    t = qa.shape[0]
    blk = ATT_BLOCK
    nb = seq // blk
    cur = lambda b, n: (b * nb + n, 0)
    prev = lambda b, n: (b * nb + jnp.maximum(n - 1, 0), 0)
    nxt = lambda b, n: (b * nb + jnp.minimum(n + 1, nb - 1), 0)
    kvspec = lambda f: pl.BlockSpec((blk, 2 * KV_WIDTH), f)
    return pl.pallas_call(
        functools.partial(_attn_kernel, l=l, seq=seq),
        out_shape=jax.ShapeDtypeStruct((t, ATT_WIDTH), BF16),
        grid=(batch, nb),
        in_specs=[
            pl.BlockSpec(memory_space=pltpu.SMEM),
            pl.BlockSpec((blk, ATT_WIDTH), cur),
            kvspec(prev), kvspec(cur), kvspec(nxt),
            kvspec(prev), kvspec(cur), kvspec(nxt),
            pl.BlockSpec((None, 1, ATT_WIDTH), lambda b, n: (l, 0, 0)),
        ],
        out_specs=pl.BlockSpec((blk, ATT_WIDTH), cur),
        compiler_params=_params(("arbitrary", "arbitrary")),
        name="window_attn",
    )(attn_sink, qa, ka2, ka2, ka2, va2, va2, va2, attn_norm3)


def _hgrn_constants():
    c, nl = HG_CHUNK, HG_LEVELS
    r = np.arange(c)
    u = r[None, :]
    rr = r[:, None]
    mats, masks = [], []
    for lev in range(nl):
        m = 1 << lev
        parent = r // (2 * m)
        anchor = parent * 2 * m + m
        upper = r >= anchor
        aa = anchor[:, None]
        mats.append(np.where(upper[:, None], (u > aa) & (u <= rr), (u > rr) & (u <= aa)))
        masks.append((parent[:, None] == parent[None, :]) & upper[:, None] & (~upper)[None, :])
    masks.append(np.eye(c, dtype=bool))
    mats.append(u <= rr)
    mats.append(u > rr)
    mf = np.concatenate(mats, axis=0).astype(np.float32)
    mb = np.concatenate([mt[::-1, ::-1] for mt in mats], axis=0).astype(np.float32)
    kf = np.stack(masks).astype(np.float32)
    kb = np.stack([mk[::-1, ::-1] for mk in masks]).astype(np.float32)
    return mf, mb, kf, kb


def _hgrn_kernel(q_ref, ff_ref, fb_ref, i_ref, g_ref, lb_ref, gn_ref, mf_ref, mb_ref, kf_ref, kb_ref,
                 o_ref, of_scr, ob_scr, st_scr, *, seq):
    c, nl = HG_CHUNK, HG_LEVELS
    nc = seq // c
    lb = lb_ref[...]
    st_scr[...] = jnp.zeros(st_scr.shape, F32)

    unroll = HG_UNROLL if nc % HG_UNROLL == 0 else 1
    dirs = (
        dict(f_ref=ff_ref, lbrow=lb[0:1, :], m_ref=mf_ref, k_ref=kf_ref, o_scr=of_scr, last_row=c - 1, d=0),
        dict(f_ref=fb_ref, lbrow=lb[1:2, :], m_ref=mb_ref, k_ref=kb_ref, o_scr=ob_scr, last_row=0, d=1),
    )

    def body(i, carry):
        work = []
        for u in range(unroll):
            cf = i * unroll + u
            work.append((dirs[0], pl.ds(pl.multiple_of(cf * c, c), c)))
            work.append((dirs[1], pl.ds(pl.multiple_of((nc - 1 - cf) * c, c), c)))
        gates = []
        for dr, rows in work:
            lbf = jnp.maximum(dr["lbrow"], LB_FLOOR)
            oml = 1.0 - dr["lbrow"]
            f = dr["f_ref"][rows, :]
            e = jnp.exp(-jnp.abs(f))
            r = 1.0 / (1.0 + e)
            er = e * r
            pos = f >= 0.0
            logf = jnp.log(lbf + oml * jnp.where(pos, r, er))
            kk = oml * jnp.where(pos, er, r)
            qh = q_ref[rows, :]
            gates.append((logf, kk, qh * _sigmoid(qh), i_ref[rows, :].astype(BF16)))
        gall = [jnp.exp(_dot_exact_lhs(dr["m_ref"][...], g[0])) for (dr, _), g in zip(work, gates)]
        amat = [dr["k_ref"][nl] * _dot_nt(g[2].astype(BF16), g[1].astype(BF16)) for (dr, _), g in zip(work, gates)]
        for lev in range(nl):
            for j, ((dr, _), g) in enumerate(zip(work, gates)):
                gl = gall[j][lev * c:(lev + 1) * c, :]
                amat[j] = amat[j] + dr["k_ref"][lev] * _dot_nt((g[2] * gl).astype(BF16), (g[1] * gl).astype(BF16))
        intra = [_dot(amat[j].astype(BF16), g[3]) for j, g in enumerate(gates)]
        upd = [_dot_tn(g[3], (g[1] * gall[j][(nl + 1) * c:(nl + 2) * c, :]).astype(BF16)) for j, g in enumerate(gates)]
        st = [st_scr[0], st_scr[1]]
        for j, ((dr, rows), g) in enumerate(zip(work, gates)):
            eb = gall[j][nl * c:(nl + 1) * c, :]
            d = dr["d"]
            dr["o_scr"][rows, :] = _dot_nt((g[2] * eb).astype(BF16), st[d].astype(BF16)) + intra[j]
            st[d] = st[d] * eb[dr["last_row"]:dr["last_row"] + 1, :] + upd[j]
        st_scr[0] = st[0]
        st_scr[1] = st[1]
        return carry

    lax.fori_loop(0, nc // unroll, body, 0)

    ep = min(256, seq)
    gn = gn_ref[...]

    def epilogue(j, carry):
        rows = pl.ds(pl.multiple_of(j * ep, ep), ep)
        o = of_scr[rows, :] + ob_scr[rows, :]
        y = o * lax.rsqrt(jnp.mean(o * o, axis=-1, keepdims=True) + EPS) * gn
        g = g_ref[rows, :]
        o_ref[rows, :] = (y * (g * _sigmoid(g))).astype(BF16)
        return carry

    lax.fori_loop(0, seq // ep, epilogue, 0)


def _hgrn(hg, lower, hg_norm3, consts, l, batch, seq):
    t = hg.shape[0]
    mf, mb, kf, kb = consts
    hd = HG_HEAD_DIM

    def col(g):
        return pl.BlockSpec((seq, hd), lambda b, h: (b, g * HG_HEADS + h))

    full2 = lambda a: pl.BlockSpec(a.shape, lambda b, h: (0, 0))
    full3 = lambda a: pl.BlockSpec(a.shape, lambda b, h: (0, 0, 0))
    return pl.pallas_call(
        functools.partial(_hgrn_kernel, seq=seq),
        out_shape=jax.ShapeDtypeStruct((t, HG_WIDTH), BF16),
        grid=(batch, HG_HEADS),
        in_specs=[
            col(0), col(1), col(2), col(3), col(4),
            pl.BlockSpec((None, 2, hd), lambda b, h: (l, 0, h)),
            pl.BlockSpec((None, 1, hd), lambda b, h: (l, 0, 0)),
            full2(mf), full2(mb), full3(kf), full3(kb),
        ],
        out_specs=pl.BlockSpec((seq, hd), lambda b, h: (b, h)),
        scratch_shapes=[
            pltpu.VMEM((seq, hd), F32),
            pltpu.VMEM((seq, hd), F32),
            pltpu.VMEM((2, hd, hd), F32),
        ],
        compiler_params=_params(("arbitrary", "arbitrary")),
        name="hgrn2_scan",
    )(hg, hg, hg, hg, hg, lower, hg_norm3, mf, mb, kf, kb)


def _outproj_kernel(a_ref, o_ref, x_ref, mod_ref, n2_ref, w_ref, wr_ref, br_ref,
                    xo_ref, h2_ref, ti_ref, gt_ref, *, d):
    mod = mod_ref[...]
    y = _dot(a_ref[...], w_ref[0:ATT_WIDTH, :]) + _dot(o_ref[...], w_ref[ATT_WIDTH:ATT_WIDTH + HG_WIDTH, :])
    xn = x_ref[...] + mod[:, 2 * d:3 * d] * y
    xo_ref[...] = xn
    h2 = _rms_mod(xn, n2_ref[...], mod[:, 4 * d:5 * d], mod[:, 3 * d:4 * d])
    h2_ref[...] = _pack_bf16_pairs(h2)
    lg = _dot_f32_nt(wr_ref[...], h2) + br_ref[...]
    eidx = lax.broadcasted_iota(I32, lg.shape, 0)
    vals, idxs = [], []
    for _ in range(TOP_K):
        mx = jnp.max(lg, axis=0, keepdims=True)
        sel = jnp.min(jnp.where(lg == mx, eidx, N_EXPERTS), axis=0, keepdims=True)
        vals.append(mx)
        idxs.append(sel)
        lg = jnp.where(eidx == sel, -jnp.inf, lg)
    ex = [jnp.exp(v - vals[0]) for v in vals]
    den = ex[0]
    for e in ex[1:]:
        den = den + e
    inv = 1.0 / den
    ti_ref[...] = jnp.concatenate(idxs, axis=0)
    gt_ref[...] = jnp.concatenate([e * inv for e in ex], axis=0)


def _outproj(a, o, x2, mod3, norm2, w_out_bf, w_router_t, b_router3, l, seq):
    t, d = x2.shape
    tm = min(PROJ_TILE, seq)
    nb = mod3.shape[0] // norm2.shape[0]
    tok = lambda i: (i, 0)
    lane_tok = lambda i: (0, i)
    return pl.pallas_call(
        functools.partial(_outproj_kernel, d=d),
        out_shape=(
            jax.ShapeDtypeStruct((t, d), F32),
            jax.ShapeDtypeStruct((t, d // 2), U32),
            jax.ShapeDtypeStruct((TOP_K, t), I32),
            jax.ShapeDtypeStruct((TOP_K, t), F32),
        ),
        grid=(t // tm,),
        in_specs=[
            pl.BlockSpec((tm, ATT_WIDTH), tok),
            pl.BlockSpec((tm, HG_WIDTH), tok),
            pl.BlockSpec((tm, d), tok),
            pl.BlockSpec((None, 1, N_MOD * d), lambda i: (l * nb + (i * tm) // seq, 0, 0)),
            pl.BlockSpec((None, 1, d), lambda i: (l, 0, 0)),
            pl.BlockSpec((None, ATT_WIDTH + HG_WIDTH, d), lambda i: (l, 0, 0)),
            pl.BlockSpec((None, N_EXPERTS, d), lambda i: (l, 0, 0)),
            pl.BlockSpec((None, N_EXPERTS, 1), lambda i: (l, 0, 0)),
        ],
        out_specs=(
            pl.BlockSpec((tm, d), tok),
            pl.BlockSpec((tm, d // 2), tok),
            pl.BlockSpec((TOP_K, tm), lane_tok),
            pl.BlockSpec((TOP_K, tm), lane_tok),
        ),
        compiler_params=_params(("arbitrary",)),
        name="out_proj_router",
    )(a, o, x2, mod3, norm2, w_out_bf, w_router_t, b_router3)


def _rank_kernel(ti_ref, tri_ref, rank_ref, cnt_ref, carry_scr):
    @pl.when(pl.program_id(0) == 0)
    def _():
        carry_scr[...] = jnp.zeros(carry_scr.shape, F32)

    ti = ti_ref[...]
    tl = ti.shape[1]
    eidx = lax.broadcasted_iota(I32, (N_EXPERTS, tl), 0)
    carry = carry_scr[...]
    rows = []
    for k in range(TOP_K):
        oh = eidx == ti[k:k + 1, :]
        ohf = jnp.where(oh, 1.0, 0.0)
        pre = _dot(ohf.astype(BF16), tri_ref[...])
        rows.append(jnp.sum(jnp.where(oh, carry + pre, 0.0), axis=0, keepdims=True))
        carry = carry + jnp.sum(ohf, axis=1, keepdims=True)
    carry_scr[...] = carry
    rank_ref[...] = jnp.concatenate(rows, axis=0).astype(I32)
    cnt_ref[...] = jnp.broadcast_to(carry, cnt_ref.shape)


def _ranks(topi):
    k, t = topi.shape
    tl = min(RANK_TILE, t)
    tri = np.triu(np.ones((tl, tl), np.float32), 1)
    return pl.pallas_call(
        _rank_kernel,
        out_shape=(jax.ShapeDtypeStruct((k, t), I32), jax.ShapeDtypeStruct((N_EXPERTS, LANES), F32)),
        grid=(t // tl,),
        in_specs=[pl.BlockSpec((k, tl), lambda i: (0, i)), pl.BlockSpec((tl, tl), lambda i: (0, 0))],
        out_specs=(pl.BlockSpec((k, tl), lambda i: (0, i)), pl.BlockSpec((N_EXPERTS, LANES), lambda i: (0, 0))),
        scratch_shapes=[pltpu.VMEM((N_EXPERTS, 1), F32)],
        compiler_params=_params(("arbitrary",)),
        name="route_rank",
    )(topi, jnp.asarray(tri, BF16))


def _dest_kernel(cnt_ref, ltri_ref, ti_ref, rank_ref, dest_ref, bexp_ref, nused_ref, eblk_ref):
    cnt = cnt_ref[...]
    nblk = jnp.floor((cnt + (MOE_ROWS - 1)) * (1.0 / MOE_ROWS))
    pstart_b = _dot(ltri_ref[...], nblk.astype(BF16))
    pend_b = pstart_b + nblk
    pstart = (pstart_b[:, 0:1] * MOE_ROWS).astype(I32)
    ti = ti_ref[...]
    tl = ti.shape[1]
    eidx = lax.broadcasted_iota(I32, (N_EXPERTS, tl), 0)
    rows = []
    for k in range(TOP_K):
        oh = eidx == ti[k:k + 1, :]
        rows.append(jnp.sum(jnp.where(oh, pstart, 0), axis=0, keepdims=True))
    dest_ref[...] = jnp.concatenate(rows, axis=0) + rank_ref[...]
    nb = bexp_ref.shape[1]
    bi = lax.broadcasted_iota(I32, (N_EXPERTS, nb), 1).astype(F32)
    be = jnp.sum(jnp.where(pend_b[:, 0:1] <= bi, 1, 0), axis=0, keepdims=True)
    bexp_ref[...] = jnp.minimum(be, N_EXPERTS - 1).astype(I32)
    nused_ref[...] = pend_b[N_EXPERTS - 1:N_EXPERTS, :].astype(I32)
    eblk_ref[0] = pstart_b.astype(I32)
    eblk_ref[1] = nblk.astype(I32)


def _destinations(counts, topi, rank, nblocks):
    k, t = topi.shape
    tl = min(2048, t)
    nbp = -(-nblocks // LANES) * LANES
    ltri = np.tril(np.ones((N_EXPERTS, N_EXPERTS), np.float32), -1)
    return pl.pallas_call(
        _dest_kernel,
        out_shape=(
            jax.ShapeDtypeStruct((k, t), I32),
            jax.ShapeDtypeStruct((1, nbp), I32),
            jax.ShapeDtypeStruct((1, LANES), I32),
            jax.ShapeDtypeStruct((2, N_EXPERTS, LANES), I32),
        ),
        grid=(t // tl,),
        in_specs=[
            pl.BlockSpec((N_EXPERTS, LANES), lambda i: (0, 0)),
            pl.BlockSpec((N_EXPERTS, N_EXPERTS), lambda i: (0, 0)),
            pl.BlockSpec((k, tl), lambda i: (0, i)),
            pl.BlockSpec((k, tl), lambda i: (0, i)),
        ],
        out_specs=(
            pl.BlockSpec((k, tl), lambda i: (0, i)),
            pl.BlockSpec((1, nbp), lambda i: (0, 0)),
            pl.BlockSpec((1, LANES), lambda i: (0, 0)),
            pl.BlockSpec((2, N_EXPERTS, LANES), lambda i: (0, 0, 0)),
        ),
        compiler_params=_params(("arbitrary",)),
        name="route_dest",
    )(counts, jnp.asarray(ltri, BF16), topi, rank)


def _inverse_kernel(bexp_ref, nused_ref, dest_hbm, inv_hbm, dsm0, dsm1, inv_sm, isem, osem, *, t, tm):
    i = pl.program_id(0)
    nt = pl.num_programs(0)
    nblk = inv_sm.shape[0] // MOE_ROWS
    nused = nused_ref[0]
    dsm = (dsm0, dsm1)

    def idx_copy(j, p):
        return pltpu.make_async_copy(dest_hbm.at[j], dsm[p], isem.at[p])

    @pl.when(i == 0)
    def _():
        idx_copy(0, 0).start()

        def init_block(b, carry):
            nxt = jnp.minimum(b + 1, nblk - 1)
            partial = jnp.logical_or(b >= nused - 1, bexp_ref[nxt] != bexp_ref[b])

            @pl.when(partial)
            def _():
                row0 = b * MOE_ROWS
                spare0 = TOP_K * t + (b % 2) * MOE_ROWS

                def init_row(r, c2):
                    inv_sm[row0 + r] = spare0 + r
                    return c2

                lax.fori_loop(0, MOE_ROWS, init_row, 0, unroll=8)

            return carry

        lax.fori_loop(0, nblk, init_block, 0)

    tok0 = i * tm
    for p in range(2):
        @pl.when(i % 2 == p)
        def _():
            @pl.when(i + 1 < nt)
            def _():
                idx_copy(i + 1, 1 - p).start()

            idx_copy(i, p).wait()

            def body(r, carry):
                for k in range(TOP_K):
                    inv_sm[dsm[p][k * tm + r]] = k * t + tok0 + r
                return carry

            lax.fori_loop(0, tm, body, 0, unroll=8)

    @pl.when(i == nt - 1)
    def _():
        out = pltpu.make_async_copy(inv_sm, inv_hbm, osem)
        out.start()
        out.wait()


def _inverse_map(bexp, nused, dest2, nblocks, t):
    nt, ktm = dest2.shape
    grid_spec = pltpu.PrefetchScalarGridSpec(
        num_scalar_prefetch=2,
        grid=(nt,),
        in_specs=[pl.BlockSpec(memory_space=pl.ANY)],
        out_specs=pl.BlockSpec(memory_space=pl.ANY),
        scratch_shapes=[
            pltpu.SMEM((ktm,), I32),
            pltpu.SMEM((ktm,), I32),
            pltpu.SMEM((nblocks * MOE_ROWS,), I32),
            pltpu.SemaphoreType.DMA((2,)),
            pltpu.SemaphoreType.DMA(()),
        ],
    )
    inv = pl.pallas_call(
        functools.partial(_inverse_kernel, t=t, tm=ktm // TOP_K),
        out_shape=jax.ShapeDtypeStruct((nblocks * MOE_ROWS,), I32),
        grid_spec=grid_spec,
        compiler_params=_params(("arbitrary",)),
        name="route_inverse",
    )(bexp, nused, dest2)
    return inv.reshape(nblocks, MOE_ROWS)


def _pack_bf16_pairs(x):
    n = x.shape[1] // 2
    lo = lax.bitcast_convert_type(x[:, :n].astype(BF16).astype(F32), U32)
    hi = lax.bitcast_convert_type(x[:, n:].astype(BF16).astype(F32), U32)
    return hi | (lo >> 16)


def _unpack_bf16_pairs(w):
    lo = lax.bitcast_convert_type(w << 16, F32)
    hi = lax.bitcast_convert_type(w & jnp.uint32(0xFFFF0000), F32)
    return lo, hi


def _expert_kernel(bexp_ref, nused_ref, inv_hbm, h2_hbm, wgu_ref, bgu_ref, wd_ref, bd_ref, y4_hbm,
                   ig0, ig1, is0, is1, xbuf, ybuf, wgu_bf, wd_bf, gisem, sisem, gsem, ssem, *, dff, t):
    i = pl.program_id(0)
    nblk = pl.num_programs(0)
    nused = nused_ref[0]
    rows = xbuf.shape[1]
    half = xbuf.shape[2]
    used = i < nused
    fresh = jnp.logical_or(i == 0, bexp_ref[i] != bexp_ref[jnp.maximum(i - 1, 0)])
    gidx, sidx = (ig0, ig1), (is0, is1)

    def gidx_copy(j, p):
        return pltpu.make_async_copy(inv_hbm.at[j], gidx[p], gisem.at[p])

    def sidx_copy(j, p):
        return pltpu.make_async_copy(inv_hbm.at[j], sidx[p], sisem.at[p])

    def gather_rows(p):
        def body(rr, carry):
            for u in range(2):
                r = rr * 2 + u
                slot_row = gidx[p][r]
                tok = slot_row & (t - 1) if t & (t - 1) == 0 else lax.rem(slot_row, t)
                pltpu.make_async_copy(h2_hbm.at[pl.ds(tok, 1), :], xbuf.at[p, pl.ds(r, 1), :],
                                      gsem.at[p]).start(priority=u)
            return carry

        for rr in range(rows // 2):
            body(rr, 0)

    def gather_wait(p):
        pltpu.make_async_copy(h2_hbm.at[pl.ds(0, rows), :], xbuf.at[p], gsem.at[p]).wait()

    def scatter_rows(p):
        def body(rr, carry):
            for u in range(2):
                r = rr * 2 + u
                pltpu.make_async_copy(ybuf.at[p, pl.ds(r, 1), :], y4_hbm.at[pl.ds(sidx[p][r], 1), :],
                                      ssem.at[p]).start(priority=u)
            return carry

        for rr in range(rows // 2):
            body(rr, 0)

    def scatter_wait(p):
        pltpu.make_async_copy(ybuf.at[p], y4_hbm.at[pl.ds(0, rows), :], ssem.at[p]).wait()

    @pl.when(i == 0)
    def _():
        ybuf[...] = jnp.zeros(ybuf.shape, U32)
        for s in range(2):
            spare = pltpu.make_async_copy(ybuf.at[s], y4_hbm.at[pl.ds(TOP_K * t + s * rows, rows), :], ssem.at[s])
            spare.start()
            spare.wait()
        gidx_copy(0, 0).start()
        sidx_copy(0, 0).start()

        @pl.when(nused > 1)
        def _():
            gidx_copy(1, 1).start()

        gidx_copy(0, 0).wait()
        gather_rows(0)

    @pl.when(jnp.logical_and(used, fresh))
    def _():
        wgu_bf[...] = wgu_ref[...].astype(BF16)
        wd_bf[...] = wd_ref[...].astype(BF16)

    for p in range(2):
        q = 1 - p

        @pl.when(i % 2 == p)
        def _():
            @pl.when(i + 1 < nused)
            def _():
                gidx_copy(i + 1, q).wait()
                gather_rows(q)
                sidx_copy(i + 1, q).start()

            @pl.when(i + 2 < nused)
            def _():
                gidx_copy(i + 2, p).start()

            @pl.when(jnp.logical_and(i >= 2, i - 2 < nused))
            def _():
                scatter_wait(p)

            @pl.when(used)
            def _():
                gather_wait(p)
                lo, hi = _unpack_bf16_pairs(xbuf[p])
                hu = (_dot(lo.astype(BF16), wgu_bf[0:half, :]) + _dot(hi.astype(BF16), wgu_bf[half:2 * half, :])
                      + bgu_ref[...])
                glu = jnp.minimum(hu[:, 0:dff], SWIGLU_LIMIT)
                lin = jnp.clip(hu[:, dff:2 * dff], -SWIGLU_LIMIT, SWIGLU_LIMIT)
                act = glu * _sigmoid(SWIGLU_ALPHA * glu) * (lin + 1.0)
                ybuf[p] = _pack_bf16_pairs(_dot(act.astype(BF16), wd_bf[...]) + bd_ref[...])
                sidx_copy(i, p).wait()
                scatter_rows(p)

            @pl.when(i == nblk - 1)
            def _():
                @pl.when(jnp.logical_and(i >= 1, i - 1 < nused))
                def _():
                    scatter_wait(q)

                @pl.when(used)
                def _():
                    scatter_wait(p)


def _experts(bexp, nused, inv, h2p, w_gu, b_gu4, w_down, b_down4, l, t):
    nblocks = inv.shape[0]
    half = h2p.shape[1]
    d = 2 * half
    dff = w_down.shape[2]

    wsel = lambda i, be, nu: (l, be[jnp.minimum(i, nu[0] - 1)], 0, 0)
    grid_spec = pltpu.PrefetchScalarGridSpec(
        num_scalar_prefetch=2,
        grid=(nblocks,),
        in_specs=[
            pl.BlockSpec(memory_space=pl.ANY),
            pl.BlockSpec(memory_space=pl.ANY),
            pl.BlockSpec((None, None, d, 2 * dff), wsel),
            pl.BlockSpec((None, None, 1, 2 * dff), wsel),
            pl.BlockSpec((None, None, dff, d), wsel),
            pl.BlockSpec((None, None, 1, d), wsel),
        ],
        out_specs=pl.BlockSpec(memory_space=pl.ANY),
        scratch_shapes=[pltpu.SMEM((MOE_ROWS,), I32)] * 4 + [
            pltpu.VMEM((2, MOE_ROWS, half), U32),
            pltpu.VMEM((2, MOE_ROWS, half), U32),
            pltpu.VMEM((d, 2 * dff), BF16),
            pltpu.VMEM((dff, d), BF16),
        ] + [pltpu.SemaphoreType.DMA((2,))] * 4,
    )
    return pl.pallas_call(
        functools.partial(_expert_kernel, dff=dff, t=t),
        out_shape=jax.ShapeDtypeStruct((TOP_K * t + 2 * MOE_ROWS, half), U32),
        grid_spec=grid_spec,
        compiler_params=_params(("arbitrary",)),
        name="moe_experts",
    )(bexp, nused, inv, h2p, w_gu, b_gu4, w_down, b_down4)


def _combine_kernel(*refs, d, final):
    y_refs = refs[:TOP_K]
    gt_ref, x_ref, mod_ref = refs[TOP_K:TOP_K + 3]
    xo_ref = refs[-1]
    half = d // 2
    gt = gt_ref[...]
    acc_lo = acc_hi = None
    for k in range(TOP_K):
        lo, hi = _unpack_bf16_pairs(y_refs[k][...])
        g = gt[:, k:k + 1]
        acc_lo = g * lo if acc_lo is None else acc_lo + g * lo
        acc_hi = g * hi if acc_hi is None else acc_hi + g * hi
    x_lo = x_ref[:, 0:half] + mod_ref[:, 5 * d:5 * d + half] * acc_lo
    x_hi = x_ref[:, half:d] + mod_ref[:, 5 * d + half:6 * d] * acc_hi
    if final:
        gain = refs[TOP_K + 3][...]
        ssq = jnp.sum(x_lo * x_lo, axis=-1, keepdims=True) + jnp.sum(x_hi * x_hi, axis=-1, keepdims=True)
        inv = lax.rsqrt(ssq * (1.0 / d) + EPS)
        x_lo = x_lo * inv * gain[:, 0:half]
        x_hi = x_hi * inv * gain[:, half:d]
    xo_ref[:, 0:half] = x_lo
    xo_ref[:, half:d] = x_hi


def _combine(y4p, gates_t, x2, mod3, l, seq, depth, final_gain=None):
    t, d = x2.shape
    tm = min(PROJ_TILE, seq)
    nb = mod3.shape[0] // depth
    nt = t // tm
    final = final_gain is not None

    def slot(k):
        return pl.BlockSpec((tm, d // 2), lambda i: (k * nt + i, 0))

    extra_specs = [pl.BlockSpec((1, d), lambda i: (0, 0))] if final else []
    extra_args = [final_gain.reshape(1, d)] if final else []
    return pl.pallas_call(
        functools.partial(_combine_kernel, d=d, final=final),
        out_shape=jax.ShapeDtypeStruct((t, d), F32),
        grid=(nt,),
        in_specs=[slot(k) for k in range(TOP_K)] + [
            pl.BlockSpec((tm, TOP_K), lambda i: (i, 0)),
            pl.BlockSpec((tm, d), lambda i: (i, 0)),
            pl.BlockSpec((None, 1, N_MOD * d), lambda i: (l * nb + (i * tm) // seq, 0, 0)),
        ] + extra_specs,
        out_specs=pl.BlockSpec((tm, d), lambda i: (i, 0)),
        compiler_params=_params(("arbitrary",)),
        name="moe_combine",
    )(*([y4p] * TOP_K), gates_t, x2, mod3, *extra_args)


def _final_kernel(x_ref, g_ref, o_ref):
    x = x_ref[...]
    o_ref[...] = x * lax.rsqrt(jnp.mean(x * x, axis=-1, keepdims=True) + EPS) * g_ref[...]


def _final_norm(x2, gain):
    t, d = x2.shape
    tm = min(PROJ_TILE, t)
    return pl.pallas_call(
        _final_kernel,
        out_shape=jax.ShapeDtypeStruct((t, d), F32),
        grid=(t // tm,),
        in_specs=[pl.BlockSpec((tm, d), lambda i: (i, 0)), pl.BlockSpec((1, d), lambda i: (0, 0))],
        out_specs=pl.BlockSpec((tm, d), lambda i: (i, 0)),
        compiler_params=_params(("arbitrary",)),
        name="final_norm",
    )(x2, gain.reshape(1, d))


def _mixer_layer(x2, mod3, tables, lower, consts, p, l, batch, seq):
    ra, rb, rc = tables
    qa, ka2, va2, hg = _inproj(x2, mod3, p["norm1"], p["w_in"], ra, rb, rc, l, seq)
    a = _attention(qa, ka2, va2, p["attn_sink"], p["attn_norm"], l, batch, seq)
    o = _hgrn(hg, lower, p["hg_norm"], consts, l, batch, seq)
    return a, o


SC_WINDOW = 128


def _sc_mesh():
    return plsc.VectorSubcoreMesh(core_axis_name="core", subcore_axis_name="subcore")


def _sc_worker(mesh):
    return lax.axis_index("core") * mesh.num_subcores + lax.axis_index("subcore"), mesh.num_cores * mesh.num_subcores


def _sc_scatter_rows(src, dest_rows, n_out):
    n, width = src.shape
    nk = dest_rows.shape[0]
    mesh = _sc_mesh()

    half = SC_WINDOW // 2

    @pl.kernel(out_type=jax.ShapeDtypeStruct((n_out, width), src.dtype), mesh=mesh,
               scratch_types=[pltpu.VMEM((nk, SC_WINDOW), I32), pltpu.VMEM((2, half, width), src.dtype),
                              pltpu.SemaphoreType.DMA((2,))],
               name="sc_scatter_rows")
    def scatter_kernel(src_hbm, idx_hbm, out_hbm, idx_v, rows_v, sem):
        wid, nw = _sc_worker(mesh)
        per = n // SC_WINDOW // nw
        row0 = wid * per * SC_WINDOW

        def load(s, buf):
            return pltpu.make_async_copy(src_hbm.at[pl.ds(row0 + s * half, half)], rows_v.at[buf], sem.at[buf])

        load(0, 0).start()

        @pl.loop(0, per)
        def _(j):
            pltpu.sync_copy(idx_hbm.at[:, pl.ds(row0 + j * SC_WINDOW, SC_WINDOW)], idx_v)
            for h in range(2):
                if h == 0:
                    load(2 * j + 1, 1).start()
                else:
                    @pl.when(j + 1 < per)
                    def _():
                        load(2 * j + 2, 0).start()

                load(2 * j + h, h).wait()
                for k in range(nk):
                    pltpu.sync_copy(rows_v.at[h], out_hbm.at[idx_v.at[k, pl.ds(h * half, half)]])

    return scatter_kernel(src, dest_rows)


def _sc_gather_rows(src, rows):
    n = rows.shape[0]
    width = src.shape[1]
    mesh = _sc_mesh()

    half = SC_WINDOW // 2

    @pl.kernel(out_type=jax.ShapeDtypeStruct((n, width), src.dtype), mesh=mesh,
               scratch_types=[pltpu.VMEM((1, SC_WINDOW), I32), pltpu.VMEM((2, half, width), src.dtype),
                              pltpu.SemaphoreType.DMA((2,))],
               name="sc_gather_rows")
    def gather_kernel(src_hbm, idx_hbm, out_hbm, idx_v, rows_v, sem):
        wid, nw = _sc_worker(mesh)
        per = n // SC_WINDOW // nw
        row0 = wid * per * SC_WINDOW

        def store(s, buf):
            return pltpu.make_async_copy(rows_v.at[buf], out_hbm.at[pl.ds(row0 + s * half, half)], sem.at[buf])

        @pl.loop(0, per)
        def _(j):
            pltpu.sync_copy(idx_hbm.at[:, pl.ds(row0 + j * SC_WINDOW, SC_WINDOW)], idx_v)
            for h in range(2):
                @pl.when(j >= 1)
                def _():
                    store(2 * j + h - 2, h).wait()

                pltpu.sync_copy(src_hbm.at[idx_v.at[0, pl.ds(h * half, half)]], rows_v.at[h])
                store(2 * j + h, h).start()

        for h in range(2):
            store(2 * per - 2 + h, h).wait()

    return gather_kernel(src, rows.reshape(1, n))


def _expert_group_kernel(eb_ref, xs_hbm, wgu_hbm, bgu_ref, wd_hbm, bd_ref, ys_hbm,
                         xbuf, ybuf, wgu_f, wd_f, wgu_bf, wd_bf, xsem, ysem, wsem, *, dff, l):
    e = pl.program_id(0)
    ne = pl.num_programs(0)
    b0 = eb_ref[e]
    nb = eb_ref[ne + e]
    rows, half = xbuf.shape[1], xbuf.shape[2]
    group = rows // MOE_ROWS
    nch = (nb + group - 1) // group
    wslot = e % 2

    def w_copies(ex, slot):
        cps = []
        for src, dst, first in ((wgu_hbm, wgu_f, 0), (wd_hbm, wd_f, WEIGHT_CHUNKS)):
            step = dst.shape[1] // WEIGHT_CHUNKS
            for c in range(WEIGHT_CHUNKS):
                cps.append(pltpu.make_async_copy(src.at[l, ex, pl.ds(c * step, step), :],
                                                 dst.at[slot, pl.ds(c * step, step), :], wsem.at[slot, first + c]))
        return cps

    @pl.when(e == 0)
    def _():
        for c, cp in enumerate(w_copies(0, 0)):
            cp.start(priority=c % 2)

    @pl.when(e + 1 < ne)
    def _():
        for c, cp in enumerate(w_copies(e + 1, 1 - wslot)):
            cp.start(priority=c % 2)

    for cp in w_copies(e, wslot):
        cp.wait()

    def x_copy(j, slot):
        start = (b0 + j * group) * MOE_ROWS
        return pltpu.make_async_copy(xs_hbm.at[pl.ds(start, rows), :], xbuf.at[slot], xsem.at[slot])

    def y_copy(j, slot, g):
        start = (b0 + j * group + g) * MOE_ROWS
        return pltpu.make_async_copy(ybuf.at[slot, pl.ds(g * MOE_ROWS, MOE_ROWS), :],
                                     ys_hbm.at[pl.ds(start, MOE_ROWS), :], ysem.at[slot, g])

    def y_each(j, slot, fn):
        for g in range(group):
            @pl.when(j * group + g < nb)
            def _():
                fn(y_copy(j, slot, g))

    @pl.when(nb > 0)
    def _():
        x_copy(0, 0).start(priority=1)
        wgu_bf[...] = wgu_f[wslot].astype(BF16)
        wd_bf[...] = wd_f[wslot].astype(BF16)

        def chunk(j, carry):
            slot = j % 2

            @pl.when(j + 1 < nch)
            def _():
                x_copy(j + 1, 1 - slot).start(priority=1)

            x_copy(j, slot).wait()

            @pl.when(j >= 2)
            def _():
                y_each(j - 2, slot, lambda cp: cp.wait())

            lo, hi = _unpack_bf16_pairs(xbuf[slot])
            hu = (_dot(lo.astype(BF16), wgu_bf[0:half, :]) + _dot(hi.astype(BF16), wgu_bf[half:2 * half, :])
                  + bgu_ref[...])
            glu = jnp.minimum(hu[:, 0:dff], SWIGLU_LIMIT)
            lin = jnp.clip(hu[:, dff:2 * dff], -SWIGLU_LIMIT, SWIGLU_LIMIT)
            act = glu * _sigmoid(SWIGLU_ALPHA * glu) * (lin + 1.0)
            ybuf[slot] = _pack_bf16_pairs(_dot(act.astype(BF16), wd_bf[...]) + bd_ref[...])
            y_each(j, slot, lambda cp: cp.start(priority=1))
            return carry

        lax.fori_loop(0, nch, chunk, 0)

        @pl.when(nch >= 2)
        def _():
            y_each(nch - 2, nch % 2, lambda cp: cp.wait())

        y_each(nch - 1, (nch - 1) % 2, lambda cp: cp.wait())


def _block_experts(eblk, xs, w_gu, b_gu4, w_down, b_down4, l):
    cap, half = xs.shape
    d = 2 * half
    dff = w_down.shape[2]
    wsel = lambda e, eb: (l, e, 0, 0)
    grid_spec = pltpu.PrefetchScalarGridSpec(
        num_scalar_prefetch=1,
        grid=(N_EXPERTS,),
        in_specs=[
            pl.BlockSpec(memory_space=pl.ANY),
            pl.BlockSpec(memory_space=pl.ANY),
            pl.BlockSpec((None, None, 1, 2 * dff), wsel),
            pl.BlockSpec(memory_space=pl.ANY),
            pl.BlockSpec((None, None, 1, d), wsel),
        ],
        out_specs=pl.BlockSpec(memory_space=pl.ANY),
        scratch_shapes=[
            pltpu.VMEM((2, EXPERT_GROUP * MOE_ROWS, half), U32),
            pltpu.VMEM((2, EXPERT_GROUP * MOE_ROWS, half), U32),
            pltpu.VMEM((2, d, 2 * dff), F32),
            pltpu.VMEM((2, dff, d), F32),
            pltpu.VMEM((d, 2 * dff), BF16),
            pltpu.VMEM((dff, d), BF16),
            pltpu.SemaphoreType.DMA((2,)),
            pltpu.SemaphoreType.DMA((2, EXPERT_GROUP)),
            pltpu.SemaphoreType.DMA((2, 2 * WEIGHT_CHUNKS)),
        ],
    )
    return pl.pallas_call(
        functools.partial(_expert_group_kernel, dff=dff, l=l),
        out_shape=jax.ShapeDtypeStruct((cap - (EXPERT_GROUP - 1) * MOE_ROWS, half), U32),
        grid_spec=grid_spec,
        compiler_params=_params(("arbitrary",)),
        name="moe_expert_groups",
    )(eblk, xs, w_gu, b_gu4, w_down, b_down4)


def _moe_layer_sc(a, o, x2, mod3, p, l, seq, depth, final_gain=None):
    t, d = x2.shape
    xn, h2p, topi, gates = _outproj(a, o, x2, mod3, p["norm2"], p["w_out"], p["w_router_t"], p["b_router"], l, seq)
    rank, counts = _ranks(topi)
    nblocks = (t * TOP_K + N_EXPERTS * MOE_ROWS) // MOE_ROWS
    dest, _, _, eblk = _destinations(counts, topi, rank, nblocks)
    xs = _sc_scatter_rows(h2p, dest, (nblocks + EXPERT_GROUP - 1) * MOE_ROWS)
    ys = _block_experts(eblk[:, :, 0].reshape(-1), xs, p["w_gu"], p["b_gu"], p["w_down"], p["b_down"], l)
    y4p = _sc_gather_rows(ys, dest.reshape(-1))
    return _combine(y4p, gates.T, xn, mod3, l, seq, depth, final_gain)


def _moe_layer(a, o, x2, mod3, p, l, seq, depth):
    t, d = x2.shape
    xn, h2p, topi, gates = _outproj(a, o, x2, mod3, p["norm2"], p["w_out"], p["w_router_t"], p["b_router"], l, seq)
    rank, counts = _ranks(topi)
    nblocks = (t * TOP_K + N_EXPERTS * MOE_ROWS) // MOE_ROWS
    dest, bexp, nused, _ = _destinations(counts, topi, rank, nblocks)
    tm = min(TOK_TILE, t)
    dest2 = dest.reshape(TOP_K, t // tm, tm).transpose(1, 0, 2).reshape(t // tm, TOP_K * tm)
    bexp, nused = bexp.reshape(-1), nused.reshape(-1)
    inv = _inverse_map(bexp, nused, dest2, nblocks, t)
    y4p = _experts(bexp, nused, inv, h2p, p["w_gu"], p["b_gu"], p["w_down"], p["b_down"], l, t)
    return _combine(y4p, gates.T, xn, mod3, l, seq, depth)


def kernel(x, c, positions, w_ada, b_ada, norm1, w_in, attn_sink, attn_norm, hg_lb_logits, hg_norm, w_out, norm2,
           w_router, b_router, w_gu, b_gu, w_down, b_down, final_norm):
    batch, seq, d = x.shape
    depth = w_ada.shape[0]
    t = batch * seq
    p = {
        "norm1": norm1.reshape(depth, 1, d),
        "w_in": w_in.astype(BF16),
        "attn_sink": attn_sink.astype(F32),
        "attn_norm": attn_norm.reshape(depth, 1, ATT_WIDTH),
        "hg_norm": hg_norm.reshape(depth, 1, HG_HEAD_DIM),
        "w_out": w_out.astype(BF16),
        "norm2": norm2.reshape(depth, 1, d),
        "w_router_t": jnp.swapaxes(w_router, 1, 2),
        "b_router": b_router.reshape(depth, N_EXPERTS, 1),
        "w_gu": w_gu,
        "b_gu": b_gu.reshape(depth, N_EXPERTS, 1, b_gu.shape[-1]),
        "w_down": w_down,
        "b_down": b_down.reshape(depth, N_EXPERTS, 1, d),
    }
    mod3 = _ada_all(c, w_ada, b_ada).reshape(depth * batch, 1, N_MOD * d)
    lower = _lower_bounds(hg_lb_logits)
    tables = _rope_tables(positions)
    consts = _hgrn_constants()
    consts = (jnp.asarray(consts[0], BF16), jnp.asarray(consts[1], BF16), jnp.asarray(consts[2]), jnp.asarray(consts[3]))
    x2 = x.reshape(t, d)
    for l in range(depth):
        a, o = _mixer_layer(x2, mod3, tables, lower, consts, p, l, batch, seq)
        x2 = _moe_layer_sc(a, o, x2, mod3, p, l, seq, depth, final_norm if l == depth - 1 else None)
    return x2.reshape(batch, seq, d)
```

```python
import functools

import numpy as np
import jax
import jax.numpy as jnp
from jax import lax
from jax.experimental import pallas as pl
from jax.experimental.pallas import tpu as pltpu
from jax.experimental.pallas import tpu_sc as plsc

F32 = jnp.float32
BF16 = jnp.bfloat16
I32 = jnp.int32
U32 = jnp.uint32

ATT_HEADS = 8
ATT_KV_HEADS = 2
ATT_HEAD_DIM = 64
ATT_WIDTH = ATT_HEADS * ATT_HEAD_DIM
KV_WIDTH = ATT_KV_HEADS * ATT_HEAD_DIM
WINDOW = 128
ATT_BLOCK = 128
ROPE_THETA = 500000.0
ROPE_DIM = ATT_HEAD_DIM // 4
HG_HEADS = 4
HG_HEAD_DIM = 128
HG_WIDTH = HG_HEADS * HG_HEAD_DIM
N_EXPERTS = 32
TOP_K = 4
SWIGLU_ALPHA = 1.702
SWIGLU_LIMIT = 7.0
N_MOD = 6
EPS = 1e-6
NEG_INF = -1e30
LB_FLOOR = 1e-30

LANES = 128
HG_CHUNK = 64
HG_LEVELS = 6
HG_UNROLL = 8
MOE_SHIFT = 8
MOE_ROWS = 1 << MOE_SHIFT
EXPERT_GROUP = 1
WEIGHT_CHUNKS = 4
TOK_TILE = 256
PROJ_TILE = 512
RANK_TILE = 512
VMEM_LIMIT = 56 * 1024 * 1024


def _dot(a, b):
    return jnp.dot(a, b, preferred_element_type=F32)


def _dot_nt(a, b):
    return lax.dot_general(a, b, (((1,), (1,)), ((), ())), preferred_element_type=F32)


def _dot_tn(a, b):
    return lax.dot_general(a, b, (((0,), (0,)), ((), ())), preferred_element_type=F32)


def _split3(x):
    hi = x.astype(BF16)
    r1 = x - hi.astype(F32)
    mid = r1.astype(BF16)
    lo = (r1 - mid.astype(F32)).astype(BF16)
    return hi, mid, lo


def _dot_exact_lhs(m_bf16, x):
    hi, mid, lo = _split3(x)
    return _dot(m_bf16, hi) + _dot(m_bf16, mid) + _dot(m_bf16, lo)


def _dot_f32_nt(a, b):
    ah, am, _ = _split3(a)
    bh, bm, _ = _split3(b)
    return _dot_nt(ah, bh) + _dot_nt(ah, bm) + _dot_nt(am, bh)


def _dot_f32(a, b):
    ah, am, _ = _split3(a)
    bh, bm, _ = _split3(b)
    return _dot(ah, bh) + _dot(ah, bm) + _dot(am, bh)


def _sigmoid(x):
    return 1.0 / (1.0 + jnp.exp(-x))


def _params(sem=None):
    return pltpu.CompilerParams(dimension_semantics=sem, vmem_limit_bytes=VMEM_LIMIT)


def _ada_kernel(c_ref, w_ref, b_ref, o_ref):
    c = c_ref[...]
    cond = c * _sigmoid(c)
    o_ref[...] = _dot_f32(cond, w_ref[...]) + b_ref[...]


def _ada_all(c, w_ada, b_ada):
    depth, d, n = w_ada.shape
    b = c.shape[0]
    nt = n // d
    return pl.pallas_call(
        _ada_kernel,
        out_shape=jax.ShapeDtypeStruct((depth, b, n), F32),
        grid=(depth, nt),
        in_specs=[
            pl.BlockSpec((b, d), lambda l, j: (0, 0)),
            pl.BlockSpec((None, d, d), lambda l, j: (l, 0, j)),
            pl.BlockSpec((None, 1, d), lambda l, j: (l, 0, j)),
        ],
        out_specs=pl.BlockSpec((None, b, d), lambda l, j: (l, 0, j)),
        compiler_params=_params(("arbitrary", "arbitrary")),
        name="ada_mod",
    )(c, w_ada, b_ada.reshape(depth, 1, n))


def _lb_kernel(x_ref, o_ref):
    depth = x_ref.shape[0]
    xs = [x_ref[l] for l in range(depth)]
    m = xs[0]
    for l in range(1, depth):
        m = jnp.maximum(m, xs[l])
    es = [jnp.exp(v - m) for v in xs]
    den = es[0]
    for l in range(1, depth):
        den = den + es[l]
    ps = [e / den for e in es]
    run = ps[0]
    o_ref[0] = run - ps[0]
    for l in range(1, depth):
        run = run + ps[l]
        o_ref[l] = run - ps[0]


def _lower_bounds(hg_lb_logits):
    return pl.pallas_call(
        _lb_kernel,
        out_shape=jax.ShapeDtypeStruct(hg_lb_logits.shape, F32),
        name="hg_lower_bounds",
    )(hg_lb_logits.astype(F32))


def _rope_kernel(pos_ref, invf_ref, a_ref, b_ref, c_ref):
    pos = pos_ref[...].astype(F32)
    ang = pos * invf_ref[...]
    cs = jnp.cos(ang)
    sn = jnp.sin(ang)
    lane = lax.broadcasted_iota(I32, ang.shape, 1) & (ATT_HEAD_DIM - 1)
    half = ROPE_DIM // 2
    first = lane < half
    second = (lane >= half) & (lane < ROPE_DIM)
    a_ref[...] = jnp.where(first | second, cs, 1.0)
    b_ref[...] = jnp.where(first, -sn, 0.0)
    c_ref[...] = jnp.where(second, sn, 0.0)


def _rope_tables(positions):
    t = positions.size
    half = ROPE_DIM // 2
    inv = (np.float32(ROPE_THETA) ** (-(np.arange(half, dtype=np.float32) * np.float32(2.0) / np.float32(ROPE_DIM)))).astype(np.float32)
    lane = np.arange(LANES) % ATT_HEAD_DIM
    pat = np.where(lane < ROPE_DIM, inv[lane % half], 0.0).astype(np.float32).reshape(1, LANES)
    tm = min(t, 2048)
    shp = jax.ShapeDtypeStruct((t, LANES), F32)
    spec = pl.BlockSpec((tm, LANES), lambda i: (i, 0))
    return pl.pallas_call(
        _rope_kernel,
        out_shape=(shp, shp, shp),
        grid=(t // tm,),
        in_specs=[pl.BlockSpec((tm, 1), lambda i: (i, 0)), pl.BlockSpec((1, LANES), lambda i: (0, 0))],
        out_specs=(spec, spec, spec),
        compiler_params=_params(("arbitrary",)),
        name="rope_tables",
    )(positions.reshape(t, 1).astype(I32), jnp.asarray(pat))


def _rms_mod(x, gain, scale, shift):
    ms = jnp.mean(x * x, axis=-1, keepdims=True)
    return (x * lax.rsqrt(ms + EPS) * gain) * (1.0 + scale) + shift


def _rope_apply(x, a, b, c):
    half = ROPE_DIM // 2
    return x * a + pltpu.roll(x, LANES - half, 1) * b + pltpu.roll(x, half, 1) * c


def _inproj_kernel(x_ref, mod_ref, n1_ref, w_ref, ra_ref, rb_ref, rc_ref, qa_ref, ka_ref, va_ref, hg_ref, *, d):
    mod = mod_ref[...]
    h = _rms_mod(x_ref[...], n1_ref[...], mod[:, d:2 * d], mod[:, 0:d]).astype(BF16)
    a, b, c = ra_ref[...], rb_ref[...], rc_ref[...]
    kvw = ATT_WIDTH + 2 * KV_WIDTH
    pa = _dot(h, w_ref[:, 0:kvw])
    scale = ATT_HEAD_DIM ** -0.5
    for g in range(ATT_WIDTH // LANES):
        qg = _rope_apply(pa[:, g * LANES:(g + 1) * LANES], a, b, c)
        qa_ref[:, g * LANES:(g + 1) * LANES] = (qg * scale).astype(BF16)
    k = _rope_apply(pa[:, ATT_WIDTH:ATT_WIDTH + KV_WIDTH], a, b, c)
    v = pa[:, ATT_WIDTH + KV_WIDTH:kvw]
    ka_ref[:, 0:LANES] = k.astype(BF16)
    ka_ref[:, LANES:2 * LANES] = pltpu.roll(k, ATT_HEAD_DIM, 1).astype(BF16)
    va_ref[:, 0:LANES] = v.astype(BF16)
    va_ref[:, LANES:2 * LANES] = pltpu.roll(v, ATT_HEAD_DIM, 1).astype(BF16)
    for g in range(5):
        lo = kvw + g * HG_WIDTH
        hg_ref[:, g * HG_WIDTH:(g + 1) * HG_WIDTH] = _dot(h, w_ref[:, lo:lo + HG_WIDTH])


def _inproj(x2, mod3, norm1, w_in_bf, ra, rb, rc, l, seq):
    t, d = x2.shape
    n_in = w_in_bf.shape[-1]
    tm = min(PROJ_TILE, seq)
    nb = mod3.shape[0] // norm1.shape[0]
    tok = lambda i: (i, 0)
    return pl.pallas_call(
        functools.partial(_inproj_kernel, d=d),
        out_shape=(
            jax.ShapeDtypeStruct((t, ATT_WIDTH), BF16),
            jax.ShapeDtypeStruct((t, 2 * KV_WIDTH), BF16),
            jax.ShapeDtypeStruct((t, 2 * KV_WIDTH), BF16),
            jax.ShapeDtypeStruct((t, 5 * HG_WIDTH), F32),
        ),
        grid=(t // tm,),
        in_specs=[
            pl.BlockSpec((tm, d), tok),
            pl.BlockSpec((None, 1, N_MOD * d), lambda i: (l * nb + (i * tm) // seq, 0, 0)),
            pl.BlockSpec((None, 1, d), lambda i: (l, 0, 0)),
            pl.BlockSpec((None, d, n_in), lambda i: (l, 0, 0)),
            pl.BlockSpec((tm, LANES), tok),
            pl.BlockSpec((tm, LANES), tok),
            pl.BlockSpec((tm, LANES), tok),
        ],
        out_specs=(
            pl.BlockSpec((tm, ATT_WIDTH), tok),
            pl.BlockSpec((tm, 2 * KV_WIDTH), tok),
            pl.BlockSpec((tm, 2 * KV_WIDTH), tok),
            pl.BlockSpec((tm, 5 * HG_WIDTH), tok),
        ),
        compiler_params=_params(("arbitrary",)),
        name="in_proj",
    )(x2, mod3, norm1, w_in_bf, ra, rb, rc)


def _attn_kernel(sink_ref, q_ref, kp_ref, kc_ref, kn_ref, vp_ref, vc_ref, vn_ref, gain_ref, o_ref, *, l, seq):
    n = pl.program_id(1)
    blk = ATT_BLOCK
    k2 = jnp.concatenate([kp_ref[...], kc_ref[...], kn_ref[...]], axis=0)
    v2 = jnp.concatenate([vp_ref[...], vc_ref[...], vn_ref[...]], axis=0)
    lane = lax.broadcasted_iota(I32, (3 * blk, LANES), 1)
    lo_half = lane < ATT_HEAD_DIM
    zero = jnp.zeros((3 * blk, LANES), BF16)
    ka, kb = k2[:, 0:LANES], k2[:, LANES:2 * LANES]
    va, vb = v2[:, 0:LANES], v2[:, LANES:2 * LANES]
    kz = [[jnp.where(lo_half, ka, zero), jnp.where(lo_half, zero, kb)],
          [jnp.where(lo_half, kb, zero), jnp.where(lo_half, zero, ka)]]
    vz = [[jnp.where(lo_half, va, zero), jnp.where(lo_half, zero, vb)],
          [jnp.where(lo_half, vb, zero), jnp.where(lo_half, zero, va)]]
    qpos = n * blk + lax.broadcasted_iota(I32, (blk, 3 * blk), 0)
    kpos = (n - 1) * blk + lax.broadcasted_iota(I32, (blk, 3 * blk), 1)
    valid = (jnp.abs(qpos - kpos) <= WINDOW) & (kpos >= 0) & (kpos < seq)
    valid2 = jnp.concatenate([valid, valid], axis=0)
    upper = lax.broadcasted_iota(I32, (2 * blk, 1), 0) < blk
    outs = []
    for j in range(ATT_KV_HEADS):
        qs = jnp.concatenate([q_ref[:, 2 * j * LANES:(2 * j + 1) * LANES],
                              q_ref[:, (2 * j + 1) * LANES:(2 * j + 2) * LANES]], axis=0)
        acc = None
        for half in range(2):
            s = _dot_nt(qs, kz[j][half])
            s = jnp.where(valid2, s, NEG_INF)
            sink = jnp.where(upper, sink_ref[l, 4 * j + half], sink_ref[l, 4 * j + 2 + half])
            mx = jnp.maximum(jnp.max(s, axis=-1, keepdims=True), sink)
            p = jnp.exp(s - mx)
            den = jnp.sum(p, axis=-1, keepdims=True) + jnp.exp(sink - mx)
            p = (p * (1.0 / den)).astype(BF16)
            pv = _dot(p, vz[j][half])
            acc = pv if acc is None else acc + pv
        outs.append(acc[0:blk])
        outs.append(acc[blk:2 * blk])
    o = jnp.concatenate(outs, axis=-1)
    ms = jnp.mean(o * o, axis=-1, keepdims=True)
    o_ref[...] = (o * lax.rsqrt(ms + EPS) * gain_ref[...]).astype(BF16)


def _attention(qa, ka2, va2, attn_sink, attn_norm3, l, batch, seq):
    t = qa.shape[0]
    blk = ATT_BLOCK
    nb = seq // blk
    cur = lambda b, n: (b * nb + n, 0)
    prev = lambda b, n: (b * nb + jnp.maximum(n - 1, 0), 0)
    nxt = lambda b, n: (b * nb + jnp.minimum(n + 1, nb - 1), 0)
    kvspec = lambda f: pl.BlockSpec((blk, 2 * KV_WIDTH), f)
    return pl.pallas_call(
        functools.partial(_attn_kernel, l=l, seq=seq),
        out_shape=jax.ShapeDtypeStruct((t, ATT_WIDTH), BF16),
        grid=(batch, nb),
        in_specs=[
            pl.BlockSpec(memory_space=pltpu.SMEM),
            pl.BlockSpec((blk, ATT_WIDTH), cur),
            kvspec(prev), kvspec(cur), kvspec(nxt),
            kvspec(prev), kvspec(cur), kvspec(nxt),
            pl.BlockSpec((None, 1, ATT_WIDTH), lambda b, n: (l, 0, 0)),
        ],
        out_specs=pl.BlockSpec((blk, ATT_WIDTH), cur),
        compiler_params=_params(("arbitrary", "arbitrary")),
        name="window_attn",
    )(attn_sink, qa, ka2, ka2, ka2, va2, va2, va2, attn_norm3)


def _hgrn_constants():
    c, nl = HG_CHUNK, HG_LEVELS
    r = np.arange(c)
    masks = []
    for lev in range(nl):
        m = 1 << lev
        parent = r // (2 * m)
        upper = r >= parent * 2 * m + m
        masks.append((parent[:, None] == parent[None, :]) & upper[:, None] & (~upper)[None, :])
    masks.append(np.eye(c, dtype=bool))
    kf = np.stack(masks).astype(np.float32)
    kb = np.stack([mk[::-1, ::-1] for mk in masks]).astype(np.float32)
    return kf, kb


def _chunk_decays(logf, reverse):
    c = logf.shape[0]
    sub = 8
    nv = c // sub
    row = lax.broadcasted_iota(I32, (sub, LANES), 0)
    parts = []
    for v in range(nv):
        x = logf[sub * v:sub * (v + 1), :]
        for s in (1, 2, 4):
            if reverse:
                x = x + jnp.where(row < sub - s, pltpu.roll(x, sub - s, 0), 0.0)
            else:
                x = x + jnp.where(row >= s, pltpu.roll(x, s, 0), 0.0)
        parts.append(x)
    b = [None] * nv
    order = list(reversed(range(nv))) if reverse else list(range(nv))
    edge = 0 if reverse else sub - 1
    carry = None
    for v in order:
        b[v] = parts[v] if carry is None else parts[v] + carry
        carry = b[v][edge:edge + 1, :]
    b_last = carry

    def anchor_row(v, r):
        return jnp.broadcast_to(b[v][r:r + 1, :], (sub, LANES))

    odd = (row & 1) == 1
    levels = []
    for lev in range(HG_LEVELS):
        m = 1 << lev
        pieces = []
        for v in range(nv):
            if m == 1:
                a = jnp.where(odd, pltpu.roll(b[v], 1, 0), b[v]) if reverse else jnp.where(odd, b[v], pltpu.roll(b[v], sub - 1, 0))
            elif m == 2:
                lo, hi = (1, 5) if reverse else (2, 6)
                a = jnp.where(row < 4, anchor_row(v, lo), anchor_row(v, hi))
            elif m == 4:
                a = anchor_row(v, 3 if reverse else 4)
            else:
                mv = m // sub
                first = (v // (2 * mv)) * 2 * mv
                a = anchor_row(first + mv - 1, sub - 1) if reverse else anchor_row(first + mv, 0)
            pieces.append(jnp.exp(-jnp.abs(b[v] - a)))
        levels.append(jnp.concatenate(pieces, axis=0))
    eb = jnp.concatenate([jnp.exp(bv) for bv in b], axis=0)
    erem = jnp.concatenate([jnp.exp(b_last - bv) for bv in b], axis=0)
    return levels, eb, erem


def _hgrn_kernel(q_ref, ff_ref, fb_ref, i_ref, g_ref, lb_ref, gn_ref, kf_ref, kb_ref,
                 o_ref, of_scr, ob_scr, st_scr, *, seq):
    c, nl = HG_CHUNK, HG_LEVELS
    nc = seq // c
    lb = lb_ref[...]
    st_scr[...] = jnp.zeros(st_scr.shape, F32)

    unroll = HG_UNROLL if nc % HG_UNROLL == 0 else 1
    dirs = (
        dict(f_ref=ff_ref, lbrow=lb[0:1, :], k_ref=kf_ref, o_scr=of_scr, last_row=c - 1, d=0),
        dict(f_ref=fb_ref, lbrow=lb[1:2, :], k_ref=kb_ref, o_scr=ob_scr, last_row=0, d=1),
    )

    def body(i, carry):
        work = []
        for u in range(unroll):
            cf = i * unroll + u
            work.append((dirs[0], pl.ds(pl.multiple_of(cf * c, c), c)))
            work.append((dirs[1], pl.ds(pl.multiple_of((nc - 1 - cf) * c, c), c)))
        gates = []
        for dr, rows in work:
            lbf = jnp.maximum(dr["lbrow"], LB_FLOOR)
            oml = 1.0 - dr["lbrow"]
            f = dr["f_ref"][rows, :]
            e = jnp.exp(-jnp.abs(f))
            r = 1.0 / (1.0 + e)
            er = e * r
            pos = f >= 0.0
            logf = jnp.log(lbf + oml * jnp.where(pos, r, er))
            kk = oml * jnp.where(pos, er, r)
            qh = q_ref[rows, :]
            gates.append((logf, kk, qh * _sigmoid(qh), i_ref[rows, :].astype(BF16)))
        decays = [_chunk_decays(g[0], dr["d"] == 1) for (dr, _), g in zip(work, gates)]
        amat = [dr["k_ref"][nl] * _dot_nt(g[2].astype(BF16), g[1].astype(BF16)) for (dr, _), g in zip(work, gates)]
        for lev in range(nl):
            for j, ((dr, _), g) in enumerate(zip(work, gates)):
                gl = decays[j][0][lev]
                amat[j] = amat[j] + dr["k_ref"][lev] * _dot_nt((g[2] * gl).astype(BF16), (g[1] * gl).astype(BF16))
        intra = [_dot(amat[j].astype(BF16), g[3]) for j, g in enumerate(gates)]
        upd = [_dot_tn(g[3], (g[1] * decays[j][2]).astype(BF16)) for j, g in enumerate(gates)]
        st = [st_scr[0], st_scr[1]]
        for j, ((dr, rows), g) in enumerate(zip(work, gates)):
            eb = decays[j][1]
            d = dr["d"]
            dr["o_scr"][rows, :] = _dot_nt((g[2] * eb).astype(BF16), st[d].astype(BF16)) + intra[j]
            st[d] = st[d] * eb[dr["last_row"]:dr["last_row"] + 1, :] + upd[j]
        st_scr[0] = st[0]
        st_scr[1] = st[1]
        return carry

    lax.fori_loop(0, nc // unroll, body, 0)

    ep = min(256, seq)
    gn = gn_ref[...]

    def epilogue(j, carry):
        rows = pl.ds(pl.multiple_of(j * ep, ep), ep)
        o = of_scr[rows, :] + ob_scr[rows, :]
        y = o * lax.rsqrt(jnp.mean(o * o, axis=-1, keepdims=True) + EPS) * gn
        g = g_ref[rows, :]
        o_ref[rows, :] = (y * (g * _sigmoid(g))).astype(BF16)
        return carry

    lax.fori_loop(0, seq // ep, epilogue, 0)


def _hgrn(hg, lower, hg_norm3, consts, l, batch, seq):
    t = hg.shape[0]
    kf, kb = consts
    hd = HG_HEAD_DIM

    def col(g):
        return pl.BlockSpec((seq, hd), lambda b, h: (b, g * HG_HEADS + h))

    full2 = lambda a: pl.BlockSpec(a.shape, lambda b, h: (0, 0))
    full3 = lambda a: pl.BlockSpec(a.shape, lambda b, h: (0, 0, 0))
    return pl.pallas_call(
        functools.partial(_hgrn_kernel, seq=seq),
        out_shape=jax.ShapeDtypeStruct((t, HG_WIDTH), BF16),
        grid=(batch, HG_HEADS),
        in_specs=[
            col(0), col(1), col(2), col(3), col(4),
            pl.BlockSpec((None, 2, hd), lambda b, h: (l, 0, h)),
            pl.BlockSpec((None, 1, hd), lambda b, h: (l, 0, 0)),
            full3(kf), full3(kb),
        ],
        out_specs=pl.BlockSpec((seq, hd), lambda b, h: (b, h)),
        scratch_shapes=[
            pltpu.VMEM((seq, hd), F32),
            pltpu.VMEM((seq, hd), F32),
            pltpu.VMEM((2, hd, hd), F32),
        ],
        compiler_params=_params(("arbitrary", "arbitrary")),
        name="hgrn2_scan",
    )(hg, hg, hg, hg, hg, lower, hg_norm3, kf, kb)


def _outproj_kernel(a_ref, o_ref, x_ref, mod_ref, n2_ref, w_ref, wr_ref, br_ref,
                    xo_ref, h2_ref, ti_ref, gt_ref, *, d):
    mod = mod_ref[...]
    y = _dot(a_ref[...], w_ref[0:ATT_WIDTH, :]) + _dot(o_ref[...], w_ref[ATT_WIDTH:ATT_WIDTH + HG_WIDTH, :])
    xn = x_ref[...] + mod[:, 2 * d:3 * d] * y
    xo_ref[...] = xn
    h2 = _rms_mod(xn, n2_ref[...], mod[:, 4 * d:5 * d], mod[:, 3 * d:4 * d])
    h2_ref[...] = _pack_bf16_pairs(h2)
    lg = _dot_f32_nt(wr_ref[...], h2) + br_ref[...]
    eidx = lax.broadcasted_iota(I32, lg.shape, 0)
    vals, idxs = [], []
    for _ in range(TOP_K):
        mx = jnp.max(lg, axis=0, keepdims=True)
        sel = jnp.min(jnp.where(lg == mx, eidx, N_EXPERTS), axis=0, keepdims=True)
        vals.append(mx)
        idxs.append(sel)
        lg = jnp.where(eidx == sel, -jnp.inf, lg)
    ex = [jnp.exp(v - vals[0]) for v in vals]
    den = ex[0]
    for e in ex[1:]:
        den = den + e
    inv = 1.0 / den
    ti_ref[...] = jnp.concatenate(idxs, axis=0)
    gt_ref[...] = jnp.concatenate([e * inv for e in ex], axis=0)


def _outproj(a, o, x2, mod3, norm2, w_out_bf, w_router_t, b_router3, l, seq):
    t, d = x2.shape
    tm = min(PROJ_TILE, seq)
    nb = mod3.shape[0] // norm2.shape[0]
    tok = lambda i: (i, 0)
    lane_tok = lambda i: (0, i)
    return pl.pallas_call(
        functools.partial(_outproj_kernel, d=d),
        out_shape=(
            jax.ShapeDtypeStruct((t, d), F32),
            jax.ShapeDtypeStruct((t, d // 2), U32),
            jax.ShapeDtypeStruct((TOP_K, t), I32),
            jax.ShapeDtypeStruct((TOP_K, t), F32),
        ),
        grid=(t // tm,),
        in_specs=[
            pl.BlockSpec((tm, ATT_WIDTH), tok),
            pl.BlockSpec((tm, HG_WIDTH), tok),
            pl.BlockSpec((tm, d), tok),
            pl.BlockSpec((None, 1, N_MOD * d), lambda i: (l * nb + (i * tm) // seq, 0, 0)),
            pl.BlockSpec((None, 1, d), lambda i: (l, 0, 0)),
            pl.BlockSpec((None, ATT_WIDTH + HG_WIDTH, d), lambda i: (l, 0, 0)),
            pl.BlockSpec((None, N_EXPERTS, d), lambda i: (l, 0, 0)),
            pl.BlockSpec((None, N_EXPERTS, 1), lambda i: (l, 0, 0)),
        ],
        out_specs=(
            pl.BlockSpec((tm, d), tok),
            pl.BlockSpec((tm, d // 2), tok),
            pl.BlockSpec((TOP_K, tm), lane_tok),
            pl.BlockSpec((TOP_K, tm), lane_tok),
        ),
        compiler_params=_params(("arbitrary",)),
        name="out_proj_router",
    )(a, o, x2, mod3, norm2, w_out_bf, w_router_t, b_router3)


def _rank_kernel(ti_ref, tri_ref, rank_ref, cnt_ref, carry_scr):
    @pl.when(pl.program_id(0) == 0)
    def _():
        carry_scr[...] = jnp.zeros(carry_scr.shape, F32)

    ti = ti_ref[...]
    tl = ti.shape[1]
    eidx = lax.broadcasted_iota(I32, (N_EXPERTS, tl), 0)
    carry = carry_scr[...]
    rows = []
    for k in range(TOP_K):
        oh = eidx == ti[k:k + 1, :]
        ohf = jnp.where(oh, 1.0, 0.0)
        pre = _dot(ohf.astype(BF16), tri_ref[...])
        rows.append(jnp.sum(jnp.where(oh, carry + pre, 0.0), axis=0, keepdims=True))
        carry = carry + jnp.sum(ohf, axis=1, keepdims=True)
    carry_scr[...] = carry
    rank_ref[...] = jnp.concatenate(rows, axis=0).astype(I32)
    cnt_ref[...] = jnp.broadcast_to(carry, cnt_ref.shape)


def _ranks(topi):
    k, t = topi.shape
    tl = min(RANK_TILE, t)
    tri = np.triu(np.ones((tl, tl), np.float32), 1)
    return pl.pallas_call(
        _rank_kernel,
        out_shape=(jax.ShapeDtypeStruct((k, t), I32), jax.ShapeDtypeStruct((N_EXPERTS, LANES), F32)),
        grid=(t // tl,),
        in_specs=[pl.BlockSpec((k, tl), lambda i: (0, i)), pl.BlockSpec((tl, tl), lambda i: (0, 0))],
        out_specs=(pl.BlockSpec((k, tl), lambda i: (0, i)), pl.BlockSpec((N_EXPERTS, LANES), lambda i: (0, 0))),
        scratch_shapes=[pltpu.VMEM((N_EXPERTS, 1), F32)],
        compiler_params=_params(("arbitrary",)),
        name="route_rank",
    )(topi, jnp.asarray(tri, BF16))


def _dest_kernel(cnt_ref, ltri_ref, ti_ref, rank_ref, dest_ref, bexp_ref, nused_ref, eblk_ref):
    cnt = cnt_ref[...]
    nblk = jnp.floor((cnt + (MOE_ROWS - 1)) * (1.0 / MOE_ROWS))
    pstart_b = _dot(ltri_ref[...], nblk.astype(BF16))
    pend_b = pstart_b + nblk
    pstart = (pstart_b[:, 0:1] * MOE_ROWS).astype(I32)
    ti = ti_ref[...]
    tl = ti.shape[1]
    eidx = lax.broadcasted_iota(I32, (N_EXPERTS, tl), 0)
    rows = []
    for k in range(TOP_K):
        oh = eidx == ti[k:k + 1, :]
        rows.append(jnp.sum(jnp.where(oh, pstart, 0), axis=0, keepdims=True))
    dest_ref[...] = jnp.concatenate(rows, axis=0) + rank_ref[...]
    nb = bexp_ref.shape[1]
    bi = lax.broadcasted_iota(I32, (N_EXPERTS, nb), 1).astype(F32)
    be = jnp.sum(jnp.where(pend_b[:, 0:1] <= bi, 1, 0), axis=0, keepdims=True)
    bexp_ref[...] = jnp.minimum(be, N_EXPERTS - 1).astype(I32)
    nused_ref[...] = pend_b[N_EXPERTS - 1:N_EXPERTS, :].astype(I32)
    eblk_ref[0] = pstart_b.astype(I32)
    eblk_ref[1] = nblk.astype(I32)


def _destinations(counts, topi, rank, nblocks):
    k, t = topi.shape
    tl = min(2048, t)
    nbp = -(-nblocks // LANES) * LANES
    ltri = np.tril(np.ones((N_EXPERTS, N_EXPERTS), np.float32), -1)
    return pl.pallas_call(
        _dest_kernel,
        out_shape=(
            jax.ShapeDtypeStruct((k, t), I32),
            jax.ShapeDtypeStruct((1, nbp), I32),
            jax.ShapeDtypeStruct((1, LANES), I32),
            jax.ShapeDtypeStruct((2, N_EXPERTS, LANES), I32),
        ),
        grid=(t // tl,),
        in_specs=[
            pl.BlockSpec((N_EXPERTS, LANES), lambda i: (0, 0)),
            pl.BlockSpec((N_EXPERTS, N_EXPERTS), lambda i: (0, 0)),
            pl.BlockSpec((k, tl), lambda i: (0, i)),
            pl.BlockSpec((k, tl), lambda i: (0, i)),
        ],
        out_specs=(
            pl.BlockSpec((k, tl), lambda i: (0, i)),
            pl.BlockSpec((1, nbp), lambda i: (0, 0)),
            pl.BlockSpec((1, LANES), lambda i: (0, 0)),
            pl.BlockSpec((2, N_EXPERTS, LANES), lambda i: (0, 0, 0)),
        ),
        compiler_params=_params(("arbitrary",)),
        name="route_dest",
    )(counts, jnp.asarray(ltri, BF16), topi, rank)


def _inverse_kernel(bexp_ref, nused_ref, dest_hbm, inv_hbm, dsm0, dsm1, inv_sm, isem, osem, *, t, tm):
    i = pl.program_id(0)
    nt = pl.num_programs(0)
    nblk = inv_sm.shape[0] // MOE_ROWS
    nused = nused_ref[0]
    dsm = (dsm0, dsm1)

    def idx_copy(j, p):
        return pltpu.make_async_copy(dest_hbm.at[j], dsm[p], isem.at[p])

    @pl.when(i == 0)
    def _():
        idx_copy(0, 0).start()

        def init_block(b, carry):
            nxt = jnp.minimum(b + 1, nblk - 1)
            partial = jnp.logical_or(b >= nused - 1, bexp_ref[nxt] != bexp_ref[b])

            @pl.when(partial)
            def _():
                row0 = b * MOE_ROWS
                spare0 = TOP_K * t + (b % 2) * MOE_ROWS

                def init_row(r, c2):
                    inv_sm[row0 + r] = spare0 + r
                    return c2

                lax.fori_loop(0, MOE_ROWS, init_row, 0, unroll=8)

            return carry

        lax.fori_loop(0, nblk, init_block, 0)

    tok0 = i * tm
    for p in range(2):
        @pl.when(i % 2 == p)
        def _():
            @pl.when(i + 1 < nt)
            def _():
                idx_copy(i + 1, 1 - p).start()

            idx_copy(i, p).wait()

            def body(r, carry):
                for k in range(TOP_K):
                    inv_sm[dsm[p][k * tm + r]] = k * t + tok0 + r
                return carry

            lax.fori_loop(0, tm, body, 0, unroll=8)

    @pl.when(i == nt - 1)
    def _():
        out = pltpu.make_async_copy(inv_sm, inv_hbm, osem)
        out.start()
        out.wait()


def _inverse_map(bexp, nused, dest2, nblocks, t):
    nt, ktm = dest2.shape
    grid_spec = pltpu.PrefetchScalarGridSpec(
        num_scalar_prefetch=2,
        grid=(nt,),
        in_specs=[pl.BlockSpec(memory_space=pl.ANY)],
        out_specs=pl.BlockSpec(memory_space=pl.ANY),
        scratch_shapes=[
            pltpu.SMEM((ktm,), I32),
            pltpu.SMEM((ktm,), I32),
            pltpu.SMEM((nblocks * MOE_ROWS,), I32),
            pltpu.SemaphoreType.DMA((2,)),
            pltpu.SemaphoreType.DMA(()),
        ],
    )
    inv = pl.pallas_call(
        functools.partial(_inverse_kernel, t=t, tm=ktm // TOP_K),
        out_shape=jax.ShapeDtypeStruct((nblocks * MOE_ROWS,), I32),
        grid_spec=grid_spec,
        compiler_params=_params(("arbitrary",)),
        name="route_inverse",
    )(bexp, nused, dest2)
    return inv.reshape(nblocks, MOE_ROWS)


def _pack_bf16_pairs(x):
    n = x.shape[1] // 2
    lo = lax.bitcast_convert_type(x[:, :n].astype(BF16).astype(F32), U32)
    hi = lax.bitcast_convert_type(x[:, n:].astype(BF16).astype(F32), U32)
    return hi | (lo >> 16)


def _unpack_bf16_pairs(w):
    lo = lax.bitcast_convert_type(w << 16, F32)
    hi = lax.bitcast_convert_type(w & jnp.uint32(0xFFFF0000), F32)
    return lo, hi


def _expert_kernel(bexp_ref, nused_ref, inv_hbm, h2_hbm, wgu_ref, bgu_ref, wd_ref, bd_ref, y4_hbm,
                   ig0, ig1, is0, is1, xbuf, ybuf, wgu_bf, wd_bf, gisem, sisem, gsem, ssem, *, dff, t):
    i = pl.program_id(0)
    nblk = pl.num_programs(0)
    nused = nused_ref[0]
    rows = xbuf.shape[1]
    half = xbuf.shape[2]
    used = i < nused
    fresh = jnp.logical_or(i == 0, bexp_ref[i] != bexp_ref[jnp.maximum(i - 1, 0)])
    gidx, sidx = (ig0, ig1), (is0, is1)

    def gidx_copy(j, p):
        return pltpu.make_async_copy(inv_hbm.at[j], gidx[p], gisem.at[p])

    def sidx_copy(j, p):
        return pltpu.make_async_copy(inv_hbm.at[j], sidx[p], sisem.at[p])

    def gather_rows(p):
        def body(rr, carry):
            for u in range(2):
                r = rr * 2 + u
                slot_row = gidx[p][r]
                tok = slot_row & (t - 1) if t & (t - 1) == 0 else lax.rem(slot_row, t)
                pltpu.make_async_copy(h2_hbm.at[pl.ds(tok, 1), :], xbuf.at[p, pl.ds(r, 1), :],
                                      gsem.at[p]).start(priority=u)
            return carry

        for rr in range(rows // 2):
            body(rr, 0)

    def gather_wait(p):
        pltpu.make_async_copy(h2_hbm.at[pl.ds(0, rows), :], xbuf.at[p], gsem.at[p]).wait()

    def scatter_rows(p):
        def body(rr, carry):
            for u in range(2):
                r = rr * 2 + u
                pltpu.make_async_copy(ybuf.at[p, pl.ds(r, 1), :], y4_hbm.at[pl.ds(sidx[p][r], 1), :],
                                      ssem.at[p]).start(priority=u)
            return carry

        for rr in range(rows // 2):
            body(rr, 0)

    def scatter_wait(p):
        pltpu.make_async_copy(ybuf.at[p], y4_hbm.at[pl.ds(0, rows), :], ssem.at[p]).wait()

    @pl.when(i == 0)
    def _():
        ybuf[...] = jnp.zeros(ybuf.shape, U32)
        for s in range(2):
            spare = pltpu.make_async_copy(ybuf.at[s], y4_hbm.at[pl.ds(TOP_K * t + s * rows, rows), :], ssem.at[s])
            spare.start()
            spare.wait()
        gidx_copy(0, 0).start()
        sidx_copy(0, 0).start()

        @pl.when(nused > 1)
        def _():
            gidx_copy(1, 1).start()

        gidx_copy(0, 0).wait()
        gather_rows(0)

    @pl.when(jnp.logical_and(used, fresh))
    def _():
        wgu_bf[...] = wgu_ref[...].astype(BF16)
        wd_bf[...] = wd_ref[...].astype(BF16)

    for p in range(2):
        q = 1 - p

        @pl.when(i % 2 == p)
        def _():
            @pl.when(i + 1 < nused)
            def _():
                gidx_copy(i + 1, q).wait()
                gather_rows(q)
                sidx_copy(i + 1, q).start()

            @pl.when(i + 2 < nused)
            def _():
                gidx_copy(i + 2, p).start()

            @pl.when(jnp.logical_and(i >= 2, i - 2 < nused))
            def _():
                scatter_wait(p)

            @pl.when(used)
            def _():
                gather_wait(p)
                lo, hi = _unpack_bf16_pairs(xbuf[p])
                hu = (_dot(lo.astype(BF16), wgu_bf[0:half, :]) + _dot(hi.astype(BF16), wgu_bf[half:2 * half, :])
                      + bgu_ref[...])
                glu = jnp.minimum(hu[:, 0:dff], SWIGLU_LIMIT)
                lin = jnp.clip(hu[:, dff:2 * dff], -SWIGLU_LIMIT, SWIGLU_LIMIT)
                act = glu * _sigmoid(SWIGLU_ALPHA * glu) * (lin + 1.0)
                ybuf[p] = _pack_bf16_pairs(_dot(act.astype(BF16), wd_bf[...]) + bd_ref[...])
                sidx_copy(i, p).wait()
                scatter_rows(p)

            @pl.when(i == nblk - 1)
            def _():
                @pl.when(jnp.logical_and(i >= 1, i - 1 < nused))
                def _():
                    scatter_wait(q)

                @pl.when(used)
                def _():
                    scatter_wait(p)


def _experts(bexp, nused, inv, h2p, w_gu, b_gu4, w_down, b_down4, l, t):
    nblocks = inv.shape[0]
    half = h2p.shape[1]
    d = 2 * half
    dff = w_down.shape[2]

    wsel = lambda i, be, nu: (l, be[jnp.minimum(i, nu[0] - 1)], 0, 0)
    grid_spec = pltpu.PrefetchScalarGridSpec(
        num_scalar_prefetch=2,
        grid=(nblocks,),
        in_specs=[
            pl.BlockSpec(memory_space=pl.ANY),
            pl.BlockSpec(memory_space=pl.ANY),
            pl.BlockSpec((None, None, d, 2 * dff), wsel),
            pl.BlockSpec((None, None, 1, 2 * dff), wsel),
            pl.BlockSpec((None, None, dff, d), wsel),
            pl.BlockSpec((None, None, 1, d), wsel),
        ],
        out_specs=pl.BlockSpec(memory_space=pl.ANY),
        scratch_shapes=[pltpu.SMEM((MOE_ROWS,), I32)] * 4 + [
            pltpu.VMEM((2, MOE_ROWS, half), U32),
            pltpu.VMEM((2, MOE_ROWS, half), U32),
            pltpu.VMEM((d, 2 * dff), BF16),
            pltpu.VMEM((dff, d), BF16),
        ] + [pltpu.SemaphoreType.DMA((2,))] * 4,
    )
    return pl.pallas_call(
        functools.partial(_expert_kernel, dff=dff, t=t),
        out_shape=jax.ShapeDtypeStruct((TOP_K * t + 2 * MOE_ROWS, half), U32),
        grid_spec=grid_spec,
        compiler_params=_params(("arbitrary",)),
        name="moe_experts",
    )(bexp, nused, inv, h2p, w_gu, b_gu4, w_down, b_down4)


def _combine_kernel(*refs, d, final):
    y_refs = refs[:TOP_K]
    gt_ref, x_ref, mod_ref = refs[TOP_K:TOP_K + 3]
    xo_ref = refs[-1]
    half = d // 2
    gt = gt_ref[...]
    acc_lo = acc_hi = None
    for k in range(TOP_K):
        lo, hi = _unpack_bf16_pairs(y_refs[k][...])
        g = gt[:, k:k + 1]
        acc_lo = g * lo if acc_lo is None else acc_lo + g * lo
        acc_hi = g * hi if acc_hi is None else acc_hi + g * hi
    x_lo = x_ref[:, 0:half] + mod_ref[:, 5 * d:5 * d + half] * acc_lo
    x_hi = x_ref[:, half:d] + mod_ref[:, 5 * d + half:6 * d] * acc_hi
    if final:
        gain = refs[TOP_K + 3][...]
        ssq = jnp.sum(x_lo * x_lo, axis=-1, keepdims=True) + jnp.sum(x_hi * x_hi, axis=-1, keepdims=True)
        inv = lax.rsqrt(ssq * (1.0 / d) + EPS)
        x_lo = x_lo * inv * gain[:, 0:half]
        x_hi = x_hi * inv * gain[:, half:d]
    xo_ref[:, 0:half] = x_lo
    xo_ref[:, half:d] = x_hi


def _combine(y4p, gates_t, x2, mod3, l, seq, depth, final_gain=None):
    t, d = x2.shape
    tm = min(PROJ_TILE, seq)
    nb = mod3.shape[0] // depth
    nt = t // tm
    final = final_gain is not None

    def slot(k):
        return pl.BlockSpec((tm, d // 2), lambda i: (k * nt + i, 0))

    extra_specs = [pl.BlockSpec((1, d), lambda i: (0, 0))] if final else []
    extra_args = [final_gain.reshape(1, d)] if final else []
    return pl.pallas_call(
        functools.partial(_combine_kernel, d=d, final=final),
        out_shape=jax.ShapeDtypeStruct((t, d), F32),
        grid=(nt,),
        in_specs=[slot(k) for k in range(TOP_K)] + [
            pl.BlockSpec((tm, TOP_K), lambda i: (i, 0)),
            pl.BlockSpec((tm, d), lambda i: (i, 0)),
            pl.BlockSpec((None, 1, N_MOD * d), lambda i: (l * nb + (i * tm) // seq, 0, 0)),
        ] + extra_specs,
        out_specs=pl.BlockSpec((tm, d), lambda i: (i, 0)),
        compiler_params=_params(("arbitrary",)),
        name="moe_combine",
    )(*([y4p] * TOP_K), gates_t, x2, mod3, *extra_args)


def _final_kernel(x_ref, g_ref, o_ref):
    x = x_ref[...]
    o_ref[...] = x * lax.rsqrt(jnp.mean(x * x, axis=-1, keepdims=True) + EPS) * g_ref[...]


def _final_norm(x2, gain):
    t, d = x2.shape
    tm = min(PROJ_TILE, t)
    return pl.pallas_call(
        _final_kernel,
        out_shape=jax.ShapeDtypeStruct((t, d), F32),
        grid=(t // tm,),
        in_specs=[pl.BlockSpec((tm, d), lambda i: (i, 0)), pl.BlockSpec((1, d), lambda i: (0, 0))],
        out_specs=pl.BlockSpec((tm, d), lambda i: (i, 0)),
        compiler_params=_params(("arbitrary",)),
        name="final_norm",
    )(x2, gain.reshape(1, d))


def _mixer_layer(x2, mod3, tables, lower, consts, p, l, batch, seq):
    ra, rb, rc = tables
    qa, ka2, va2, hg = _inproj(x2, mod3, p["norm1"], p["w_in"], ra, rb, rc, l, seq)
    a = _attention(qa, ka2, va2, p["attn_sink"], p["attn_norm"], l, batch, seq)
    o = _hgrn(hg, lower, p["hg_norm"], consts, l, batch, seq)
    return a, o


SC_WINDOW = 128


def _sc_mesh():
    return plsc.VectorSubcoreMesh(core_axis_name="core", subcore_axis_name="subcore")


def _sc_worker(mesh):
    return lax.axis_index("core") * mesh.num_subcores + lax.axis_index("subcore"), mesh.num_cores * mesh.num_subcores


def _sc_scatter_rows(src, dest_rows, n_out):
    n, width = src.shape
    nk = dest_rows.shape[0]
    mesh = _sc_mesh()

    half = SC_WINDOW // 2

    @pl.kernel(out_type=jax.ShapeDtypeStruct((n_out, width), src.dtype), mesh=mesh,
               scratch_types=[pltpu.VMEM((nk, SC_WINDOW), I32), pltpu.VMEM((2, half, width), src.dtype),
                              pltpu.SemaphoreType.DMA((2,))],
               name="sc_scatter_rows")
    def scatter_kernel(src_hbm, idx_hbm, out_hbm, idx_v, rows_v, sem):
        wid, nw = _sc_worker(mesh)
        per = n // SC_WINDOW // nw
        row0 = wid * per * SC_WINDOW

        def load(s, buf):
            return pltpu.make_async_copy(src_hbm.at[pl.ds(row0 + s * half, half)], rows_v.at[buf], sem.at[buf])

        load(0, 0).start()

        @pl.loop(0, per)
        def _(j):
            pltpu.sync_copy(idx_hbm.at[:, pl.ds(row0 + j * SC_WINDOW, SC_WINDOW)], idx_v)
            for h in range(2):
                if h == 0:
                    load(2 * j + 1, 1).start()
                else:
                    @pl.when(j + 1 < per)
                    def _():
                        load(2 * j + 2, 0).start()

                load(2 * j + h, h).wait()
                for k in range(nk):
                    pltpu.sync_copy(rows_v.at[h], out_hbm.at[idx_v.at[k, pl.ds(h * half, half)]])

    return scatter_kernel(src, dest_rows)


def _sc_gather_rows(src, rows):
    n = rows.shape[0]
    width = src.shape[1]
    mesh = _sc_mesh()

    half = SC_WINDOW // 2

    @pl.kernel(out_type=jax.ShapeDtypeStruct((n, width), src.dtype), mesh=mesh,
               scratch_types=[pltpu.VMEM((1, SC_WINDOW), I32), pltpu.VMEM((2, half, width), src.dtype),
                              pltpu.SemaphoreType.DMA((2,))],
               name="sc_gather_rows")
    def gather_kernel(src_hbm, idx_hbm, out_hbm, idx_v, rows_v, sem):
        wid, nw = _sc_worker(mesh)
        per = n // SC_WINDOW // nw
        row0 = wid * per * SC_WINDOW

        def store(s, buf):
            return pltpu.make_async_copy(rows_v.at[buf], out_hbm.at[pl.ds(row0 + s * half, half)], sem.at[buf])

        @pl.loop(0, per)
        def _(j):
            pltpu.sync_copy(idx_hbm.at[:, pl.ds(row0 + j * SC_WINDOW, SC_WINDOW)], idx_v)
            for h in range(2):
                @pl.when(j >= 1)
                def _():
                    store(2 * j + h - 2, h).wait()

                pltpu.sync_copy(src_hbm.at[idx_v.at[0, pl.ds(h * half, half)]], rows_v.at[h])
                store(2 * j + h, h).start()

        for h in range(2):
            store(2 * per - 2 + h, h).wait()

    return gather_kernel(src, rows.reshape(1, n))


def _expert_group_kernel(eb_ref, xs_hbm, wgu_hbm, bgu_ref, wd_hbm, bd_ref, ys_hbm,
                         xbuf, ybuf, wgu_f, wd_f, wgu_bf, wd_bf, xsem, ysem, wsem, *, dff, l):
    e = pl.program_id(0)
    ne = pl.num_programs(0)
    b0 = eb_ref[e]
    nb = eb_ref[ne + e]
    rows, half = xbuf.shape[1], xbuf.shape[2]
    group = rows // MOE_ROWS
    nch = (nb + group - 1) // group
    wslot = e % 2

    def w_copies(ex, slot):
        cps = []
        for src, dst, first in ((wgu_hbm, wgu_f, 0), (wd_hbm, wd_f, WEIGHT_CHUNKS)):
            step = dst.shape[1] // WEIGHT_CHUNKS
            for c in range(WEIGHT_CHUNKS):
                cps.append(pltpu.make_async_copy(src.at[l, ex, pl.ds(c * step, step), :],
                                                 dst.at[slot, pl.ds(c * step, step), :], wsem.at[slot, first + c]))
        return cps

    @pl.when(e == 0)
    def _():
        for c, cp in enumerate(w_copies(0, 0)):
            cp.start(priority=c % 2)

    @pl.when(e + 1 < ne)
    def _():
        for c, cp in enumerate(w_copies(e + 1, 1 - wslot)):
            cp.start(priority=c % 2)

    for cp in w_copies(e, wslot):
        cp.wait()

    def x_copy(j, slot):
        start = (b0 + j * group) * MOE_ROWS
        return pltpu.make_async_copy(xs_hbm.at[pl.ds(start, rows), :], xbuf.at[slot], xsem.at[slot])

    def y_copy(j, slot, g):
        start = (b0 + j * group + g) * MOE_ROWS
        return pltpu.make_async_copy(ybuf.at[slot, pl.ds(g * MOE_ROWS, MOE_ROWS), :],
                                     ys_hbm.at[pl.ds(start, MOE_ROWS), :], ysem.at[slot, g])

    def y_each(j, slot, fn):
        for g in range(group):
            @pl.when(j * group + g < nb)
            def _():
                fn(y_copy(j, slot, g))

    @pl.when(nb > 0)
    def _():
        x_copy(0, 0).start(priority=1)
        wgu_bf[...] = wgu_f[wslot].astype(BF16)
        wd_bf[...] = wd_f[wslot].astype(BF16)

        def chunk(j, carry):
            slot = j % 2

            @pl.when(j + 1 < nch)
            def _():
                x_copy(j + 1, 1 - slot).start(priority=1)

            x_copy(j, slot).wait()

            @pl.when(j >= 2)
            def _():
                y_each(j - 2, slot, lambda cp: cp.wait())

            lo, hi = _unpack_bf16_pairs(xbuf[slot])
            hu = (_dot(lo.astype(BF16), wgu_bf[0:half, :]) + _dot(hi.astype(BF16), wgu_bf[half:2 * half, :])
                  + bgu_ref[...])
            glu = jnp.minimum(hu[:, 0:dff], SWIGLU_LIMIT)
            lin = jnp.clip(hu[:, dff:2 * dff], -SWIGLU_LIMIT, SWIGLU_LIMIT)
            act = glu * _sigmoid(SWIGLU_ALPHA * glu) * (lin + 1.0)
            ybuf[slot] = _pack_bf16_pairs(_dot(act.astype(BF16), wd_bf[...]) + bd_ref[...])
            y_each(j, slot, lambda cp: cp.start(priority=1))
            return carry

        lax.fori_loop(0, nch, chunk, 0)

        @pl.when(nch >= 2)
        def _():
            y_each(nch - 2, nch % 2, lambda cp: cp.wait())

        y_each(nch - 1, (nch - 1) % 2, lambda cp: cp.wait())


def _block_experts(eblk, xs, w_gu, b_gu4, w_down, b_down4, l):
    cap, half = xs.shape
    d = 2 * half
    dff = w_down.shape[2]
    wsel = lambda e, eb: (l, e, 0, 0)
    grid_spec = pltpu.PrefetchScalarGridSpec(
        num_scalar_prefetch=1,
        grid=(N_EXPERTS,),
        in_specs=[
            pl.BlockSpec(memory_space=pl.ANY),
            pl.BlockSpec(memory_space=pl.ANY),
            pl.BlockSpec((None, None, 1, 2 * dff), wsel),
            pl.BlockSpec(memory_space=pl.ANY),
            pl.BlockSpec((None, None, 1, d), wsel),
        ],
        out_specs=pl.BlockSpec(memory_space=pl.ANY),
        scratch_shapes=[
            pltpu.VMEM((2, EXPERT_GROUP * MOE_ROWS, half), U32),
            pltpu.VMEM((2, EXPERT_GROUP * MOE_ROWS, half), U32),
            pltpu.VMEM((2, d, 2 * dff), F32),
            pltpu.VMEM((2, dff, d), F32),
            pltpu.VMEM((d, 2 * dff), BF16),
            pltpu.VMEM((dff, d), BF16),
            pltpu.SemaphoreType.DMA((2,)),
            pltpu.SemaphoreType.DMA((2, EXPERT_GROUP)),
            pltpu.SemaphoreType.DMA((2, 2 * WEIGHT_CHUNKS)),
        ],
    )
    return pl.pallas_call(
        functools.partial(_expert_group_kernel, dff=dff, l=l),
        out_shape=jax.ShapeDtypeStruct((cap - (EXPERT_GROUP - 1) * MOE_ROWS, half), U32),
        grid_spec=grid_spec,
        compiler_params=_params(("arbitrary",)),
        name="moe_expert_groups",
    )(eblk, xs, w_gu, b_gu4, w_down, b_down4)


def _moe_layer_sc(a, o, x2, mod3, p, l, seq, depth, final_gain=None):
    t, d = x2.shape
    xn, h2p, topi, gates = _outproj(a, o, x2, mod3, p["norm2"], p["w_out"], p["w_router_t"], p["b_router"], l, seq)
    rank, counts = _ranks(topi)
    nblocks = (t * TOP_K + N_EXPERTS * MOE_ROWS) // MOE_ROWS
    dest, _, _, eblk = _destinations(counts, topi, rank, nblocks)
    xs = _sc_scatter_rows(h2p, dest, (nblocks + EXPERT_GROUP - 1) * MOE_ROWS)
    ys = _block_experts(eblk[:, :, 0].reshape(-1), xs, p["w_gu"], p["b_gu"], p["w_down"], p["b_down"], l)
    y4p = _sc_gather_rows(ys, dest.reshape(-1))
    return _combine(y4p, gates.T, xn, mod3, l, seq, depth, final_gain)


def _moe_layer(a, o, x2, mod3, p, l, seq, depth):
    t, d = x2.shape
    xn, h2p, topi, gates = _outproj(a, o, x2, mod3, p["norm2"], p["w_out"], p["w_router_t"], p["b_router"], l, seq)
    rank, counts = _ranks(topi)
    nblocks = (t * TOP_K + N_EXPERTS * MOE_ROWS) // MOE_ROWS
    dest, bexp, nused, _ = _destinations(counts, topi, rank, nblocks)
    tm = min(TOK_TILE, t)
    dest2 = dest.reshape(TOP_K, t // tm, tm).transpose(1, 0, 2).reshape(t // tm, TOP_K * tm)
    bexp, nused = bexp.reshape(-1), nused.reshape(-1)
    inv = _inverse_map(bexp, nused, dest2, nblocks, t)
    y4p = _experts(bexp, nused, inv, h2p, p["w_gu"], p["b_gu"], p["w_down"], p["b_down"], l, t)
    return _combine(y4p, gates.T, xn, mod3, l, seq, depth)


def kernel(x, c, positions, w_ada, b_ada, norm1, w_in, attn_sink, attn_norm, hg_lb_logits, hg_norm, w_out, norm2,
           w_router, b_router, w_gu, b_gu, w_down, b_down, final_norm):
    batch, seq, d = x.shape
    depth = w_ada.shape[0]
    t = batch * seq
    p = {
        "norm1": norm1.reshape(depth, 1, d),
        "w_in": w_in.astype(BF16),
        "attn_sink": attn_sink.astype(F32),
        "attn_norm": attn_norm.reshape(depth, 1, ATT_WIDTH),
        "hg_norm": hg_norm.reshape(depth, 1, HG_HEAD_DIM),
        "w_out": w_out.astype(BF16),
        "norm2": norm2.reshape(depth, 1, d),
        "w_router_t": jnp.swapaxes(w_router, 1, 2),
        "b_router": b_router.reshape(depth, N_EXPERTS, 1),
        "w_gu": w_gu,
        "b_gu": b_gu.reshape(depth, N_EXPERTS, 1, b_gu.shape[-1]),
        "w_down": w_down,
        "b_down": b_down.reshape(depth, N_EXPERTS, 1, d),
    }
    mod3 = _ada_all(c, w_ada, b_ada).reshape(depth * batch, 1, N_MOD * d)
    lower = _lower_bounds(hg_lb_logits)
    tables = _rope_tables(positions)
    consts = tuple(jnp.asarray(m) for m in _hgrn_constants())
    x2 = x.reshape(t, d)
    for l in range(depth):
        a, o = _mixer_layer(x2, mod3, tables, lower, consts, p, l, batch, seq)
        x2 = _moe_layer_sc(a, o, x2, mod3, p, l, seq, depth, final_norm if l == depth - 1 else None)
    return x2.reshape(batch, seq, d)
```

```python
import functools

import numpy as np
import jax
import jax.numpy as jnp
from jax import lax
from jax.experimental import pallas as pl
from jax.experimental.pallas import tpu as pltpu
from jax.experimental.pallas import tpu_sc as plsc

F32 = jnp.float32
BF16 = jnp.bfloat16
I32 = jnp.int32
U32 = jnp.uint32

ATT_HEADS = 8
ATT_KV_HEADS = 2
ATT_HEAD_DIM = 64
ATT_WIDTH = ATT_HEADS * ATT_HEAD_DIM
KV_WIDTH = ATT_KV_HEADS * ATT_HEAD_DIM
WINDOW = 128
ATT_BLOCK = 128
ROPE_THETA = 500000.0
ROPE_DIM = ATT_HEAD_DIM // 4
HG_HEADS = 4
HG_HEAD_DIM = 128
HG_WIDTH = HG_HEADS * HG_HEAD_DIM
N_EXPERTS = 32
TOP_K = 4
SWIGLU_ALPHA = 1.702
SWIGLU_LIMIT = 7.0
N_MOD = 6
EPS = 1e-6
NEG_INF = -1e30
LB_FLOOR = 1e-30

LANES = 128
HG_CHUNK = 64
HG_LEVELS = 6
HG_UNROLL = 8
MOE_SHIFT = 8
MOE_ROWS = 1 << MOE_SHIFT
EXPERT_GROUP = 1
WEIGHT_CHUNKS = 4
EXPERT_COLS = 512
TOK_TILE = 256
PROJ_TILE = 512
RANK_TILE = 512
VMEM_LIMIT = 56 * 1024 * 1024


def _dot(a, b):
    return jnp.dot(a, b, preferred_element_type=F32)


def _dot_nt(a, b):
    return lax.dot_general(a, b, (((1,), (1,)), ((), ())), preferred_element_type=F32)


def _dot_tn(a, b):
    return lax.dot_general(a, b, (((0,), (0,)), ((), ())), preferred_element_type=F32)


def _split3(x):
    hi = x.astype(BF16)
    r1 = x - hi.astype(F32)
    mid = r1.astype(BF16)
    lo = (r1 - mid.astype(F32)).astype(BF16)
    return hi, mid, lo


def _dot_exact_lhs(m_bf16, x):
    hi, mid, lo = _split3(x)
    return _dot(m_bf16, hi) + _dot(m_bf16, mid) + _dot(m_bf16, lo)


def _dot_f32_nt(a, b):
    ah, am, _ = _split3(a)
    bh, bm, _ = _split3(b)
    return _dot_nt(ah, bh) + _dot_nt(ah, bm) + _dot_nt(am, bh)


def _dot_f32(a, b):
    ah, am, _ = _split3(a)
    bh, bm, _ = _split3(b)
    return _dot(ah, bh) + _dot(ah, bm) + _dot(am, bh)


def _sigmoid(x):
    return 1.0 / (1.0 + jnp.exp(-x))


def _params(sem=None):
    return pltpu.CompilerParams(dimension_semantics=sem, vmem_limit_bytes=VMEM_LIMIT)


def _ada_kernel(c_ref, w_ref, b_ref, o_ref):
    c = c_ref[...]
    cond = c * _sigmoid(c)
    o_ref[...] = _dot_f32(cond, w_ref[...]) + b_ref[...]


def _ada_all(c, w_ada, b_ada):
    depth, d, n = w_ada.shape
    b = c.shape[0]
    nt = n // d
    return pl.pallas_call(
        _ada_kernel,
        out_shape=jax.ShapeDtypeStruct((depth, b, n), F32),
        grid=(depth, nt),
        in_specs=[
            pl.BlockSpec((b, d), lambda l, j: (0, 0)),
            pl.BlockSpec((None, d, d), lambda l, j: (l, 0, j)),
            pl.BlockSpec((None, 1, d), lambda l, j: (l, 0, j)),
        ],
        out_specs=pl.BlockSpec((None, b, d), lambda l, j: (l, 0, j)),
        compiler_params=_params(("arbitrary", "arbitrary")),
        name="ada_mod",
    )(c, w_ada, b_ada.reshape(depth, 1, n))


def _lb_kernel(x_ref, o_ref):
    depth = x_ref.shape[0]
    xs = [x_ref[l] for l in range(depth)]
    m = xs[0]
    for l in range(1, depth):
        m = jnp.maximum(m, xs[l])
    es = [jnp.exp(v - m) for v in xs]
    den = es[0]
    for l in range(1, depth):
        den = den + es[l]
    ps = [e / den for e in es]
    run = ps[0]
    o_ref[0] = run - ps[0]
    for l in range(1, depth):
        run = run + ps[l]
        o_ref[l] = run - ps[0]


def _lower_bounds(hg_lb_logits):
    return pl.pallas_call(
        _lb_kernel,
        out_shape=jax.ShapeDtypeStruct(hg_lb_logits.shape, F32),
        name="hg_lower_bounds",
    )(hg_lb_logits.astype(F32))


def _rope_kernel(pos_ref, invf_ref, a_ref, b_ref, c_ref):
    pos = pos_ref[...].astype(F32)
    ang = pos * invf_ref[...]
    cs = jnp.cos(ang)
    sn = jnp.sin(ang)
    lane = lax.broadcasted_iota(I32, ang.shape, 1) & (ATT_HEAD_DIM - 1)
    half = ROPE_DIM // 2
    first = lane < half
    second = (lane >= half) & (lane < ROPE_DIM)
    a_ref[...] = jnp.where(first | second, cs, 1.0)
    b_ref[...] = jnp.where(first, -sn, 0.0)
    c_ref[...] = jnp.where(second, sn, 0.0)


def _rope_tables(positions):
    t = positions.size
    half = ROPE_DIM // 2
    inv = (np.float32(ROPE_THETA) ** (-(np.arange(half, dtype=np.float32) * np.float32(2.0) / np.float32(ROPE_DIM)))).astype(np.float32)
    lane = np.arange(LANES) % ATT_HEAD_DIM
    pat = np.where(lane < ROPE_DIM, inv[lane % half], 0.0).astype(np.float32).reshape(1, LANES)
    tm = min(t, 2048)
    shp = jax.ShapeDtypeStruct((t, LANES), F32)
    spec = pl.BlockSpec((tm, LANES), lambda i: (i, 0))
    return pl.pallas_call(
        _rope_kernel,
        out_shape=(shp, shp, shp),
        grid=(t // tm,),
        in_specs=[pl.BlockSpec((tm, 1), lambda i: (i, 0)), pl.BlockSpec((1, LANES), lambda i: (0, 0))],
        out_specs=(spec, spec, spec),
        compiler_params=_params(("arbitrary",)),
        name="rope_tables",
    )(positions.reshape(t, 1).astype(I32), jnp.asarray(pat))


def _rms_mod(x, gain, scale, shift):
    ms = jnp.mean(x * x, axis=-1, keepdims=True)
    return (x * lax.rsqrt(ms + EPS) * gain) * (1.0 + scale) + shift


def _rope_apply(x, a, b, c):
    half = ROPE_DIM // 2
    return x * a + pltpu.roll(x, LANES - half, 1) * b + pltpu.roll(x, half, 1) * c


def _inproj_kernel(x_ref, mod_ref, n1_ref, w_ref, ra_ref, rb_ref, rc_ref, qa_ref, ka_ref, va_ref, hg_ref, *, d):
    mod = mod_ref[...]
    h = _rms_mod(x_ref[...], n1_ref[...], mod[:, d:2 * d], mod[:, 0:d]).astype(BF16)
    a, b, c = ra_ref[...], rb_ref[...], rc_ref[...]
    kvw = ATT_WIDTH + 2 * KV_WIDTH
    pa = _dot(h, w_ref[:, 0:kvw])
    scale = ATT_HEAD_DIM ** -0.5
    for g in range(ATT_WIDTH // LANES):
        qg = _rope_apply(pa[:, g * LANES:(g + 1) * LANES], a, b, c)
        qa_ref[:, g * LANES:(g + 1) * LANES] = (qg * scale).astype(BF16)
    k = _rope_apply(pa[:, ATT_WIDTH:ATT_WIDTH + KV_WIDTH], a, b, c)
    v = pa[:, ATT_WIDTH + KV_WIDTH:kvw]
    ka_ref[:, 0:LANES] = k.astype(BF16)
    ka_ref[:, LANES:2 * LANES] = pltpu.roll(k, ATT_HEAD_DIM, 1).astype(BF16)
    va_ref[:, 0:LANES] = v.astype(BF16)
    va_ref[:, LANES:2 * LANES] = pltpu.roll(v, ATT_HEAD_DIM, 1).astype(BF16)
    for g in range(5):
        lo = kvw + g * HG_WIDTH
        hg_ref[:, g * HG_WIDTH:(g + 1) * HG_WIDTH] = _dot(h, w_ref[:, lo:lo + HG_WIDTH])


def _inproj(x2, mod3, norm1, w_in_bf, ra, rb, rc, l, seq):
    t, d = x2.shape
    n_in = w_in_bf.shape[-1]
    tm = min(PROJ_TILE, seq)
    nb = mod3.shape[0] // norm1.shape[0]
    tok = lambda i: (i, 0)
    return pl.pallas_call(
        functools.partial(_inproj_kernel, d=d),
        out_shape=(
            jax.ShapeDtypeStruct((t, ATT_WIDTH), BF16),
            jax.ShapeDtypeStruct((t, 2 * KV_WIDTH), BF16),
            jax.ShapeDtypeStruct((t, 2 * KV_WIDTH), BF16),
            jax.ShapeDtypeStruct((t, 5 * HG_WIDTH), F32),
        ),
        grid=(t // tm,),
        in_specs=[
            pl.BlockSpec((tm, d), tok),
            pl.BlockSpec((None, 1, N_MOD * d), lambda i: (l * nb + (i * tm) // seq, 0, 0)),
            pl.BlockSpec((None, 1, d), lambda i: (l, 0, 0)),
            pl.BlockSpec((None, d, n_in), lambda i: (l, 0, 0)),
            pl.BlockSpec((tm, LANES), tok),
            pl.BlockSpec((tm, LANES), tok),
            pl.BlockSpec((tm, LANES), tok),
        ],
        out_specs=(
            pl.BlockSpec((tm, ATT_WIDTH), tok),
            pl.BlockSpec((tm, 2 * KV_WIDTH), tok),
            pl.BlockSpec((tm, 2 * KV_WIDTH), tok),
            pl.BlockSpec((tm, 5 * HG_WIDTH), tok),
        ),
        compiler_params=_params(("arbitrary",)),
        name="in_proj",
    )(x2, mod3, norm1, w_in_bf, ra, rb, rc)


def _attn_kernel(sink_ref, q_ref, kp_ref, kc_ref, kn_ref, vp_ref, vc_ref, vn_ref, gain_ref, o_ref, *, l, seq):
    n = pl.program_id(1)
    blk = ATT_BLOCK
    k2 = jnp.concatenate([kp_ref[...], kc_ref[...], kn_ref[...]], axis=0)
    v2 = jnp.concatenate([vp_ref[...], vc_ref[...], vn_ref[...]], axis=0)
    lane = lax.broadcasted_iota(I32, (3 * blk, LANES), 1)
    lo_half = lane < ATT_HEAD_DIM
    zero = jnp.zeros((3 * blk, LANES), BF16)
    ka, kb = k2[:, 0:LANES], k2[:, LANES:2 * LANES]
    va, vb = v2[:, 0:LANES], v2[:, LANES:2 * LANES]
    kz = [[jnp.where(lo_half, ka, zero), jnp.where(lo_half, zero, kb)],
          [jnp.where(lo_half, kb, zero), jnp.where(lo_half, zero, ka)]]
    vz = [[jnp.where(lo_half, va, zero), jnp.where(lo_half, zero, vb)],
          [jnp.where(lo_half, vb, zero), jnp.where(lo_half, zero, va)]]
    qpos = n * blk + lax.broadcasted_iota(I32, (blk, 3 * blk), 0)
    kpos = (n - 1) * blk + lax.broadcasted_iota(I32, (blk, 3 * blk), 1)
    valid = (jnp.abs(qpos - kpos) <= WINDOW) & (kpos >= 0) & (kpos < seq)
    valid2 = jnp.concatenate([valid, valid], axis=0)
    upper = lax.broadcasted_iota(I32, (2 * blk, 1), 0) < blk
    outs = []
    for j in range(ATT_KV_HEADS):
        qs = jnp.concatenate([q_ref[:, 2 * j * LANES:(2 * j + 1) * LANES],
                              q_ref[:, (2 * j + 1) * LANES:(2 * j + 2) * LANES]], axis=0)
        acc = None
        for half in range(2):
            s = _dot_nt(qs, kz[j][half])
            s = jnp.where(valid2, s, NEG_INF)
            sink = jnp.where(upper, sink_ref[l, 4 * j + half], sink_ref[l, 4 * j + 2 + half])
            mx = jnp.maximum(jnp.max(s, axis=-1, keepdims=True), sink)
            p = jnp.exp(s - mx)
            den = jnp.sum(p, axis=-1, keepdims=True) + jnp.exp(sink - mx)
            p = (p * (1.0 / den)).astype(BF16)
            pv = _dot(p, vz[j][half])
            acc = pv if acc is None else acc + pv
        outs.append(acc[0:blk])
        outs.append(acc[blk:2 * blk])
    o = jnp.concatenate(outs, axis=-1)
    ms = jnp.mean(o * o, axis=-1, keepdims=True)
    o_ref[...] = (o * lax.rsqrt(ms + EPS) * gain_ref[...]).astype(BF16)


def _attention(qa, ka2, va2, attn_sink, attn_norm3, l, batch, seq):
    t = qa.shape[0]
    blk = ATT_BLOCK
    nb = seq // blk
    cur = lambda b, n: (b * nb + n, 0)
    prev = lambda b, n: (b * nb + jnp.maximum(n - 1, 0), 0)
    nxt = lambda b, n: (b * nb + jnp.minimum(n + 1, nb - 1), 0)
    kvspec = lambda f: pl.BlockSpec((blk, 2 * KV_WIDTH), f)
    return pl.pallas_call(
        functools.partial(_attn_kernel, l=l, seq=seq),
        out_shape=jax.ShapeDtypeStruct((t, ATT_WIDTH), BF16),
        grid=(batch, nb),
        in_specs=[
            pl.BlockSpec(memory_space=pltpu.SMEM),
            pl.BlockSpec((blk, ATT_WIDTH), cur),
            kvspec(prev), kvspec(cur), kvspec(nxt),
            kvspec(prev), kvspec(cur), kvspec(nxt),
            pl.BlockSpec((None, 1, ATT_WIDTH), lambda b, n: (l, 0, 0)),
        ],
        out_specs=pl.BlockSpec((blk, ATT_WIDTH), cur),
        compiler_params=_params(("arbitrary", "arbitrary")),
        name="window_attn",
    )(attn_sink, qa, ka2, ka2, ka2, va2, va2, va2, attn_norm3)


def _hgrn_constants():
    c, nl = HG_CHUNK, HG_LEVELS
    r = np.arange(c)
    masks = []
    for lev in range(nl):
        m = 1 << lev
        parent = r // (2 * m)
        upper = r >= parent * 2 * m + m
        masks.append((parent[:, None] == parent[None, :]) & upper[:, None] & (~upper)[None, :])
    masks.append(np.eye(c, dtype=bool))
    kf = np.stack(masks).astype(np.float32)
    kb = np.stack([mk[::-1, ::-1] for mk in masks]).astype(np.float32)
    return kf, kb


def _chunk_decays(logf, reverse):
    c = logf.shape[0]
    sub = 8
    nv = c // sub
    row = lax.broadcasted_iota(I32, (sub, LANES), 0)
    parts = []
    for v in range(nv):
        x = logf[sub * v:sub * (v + 1), :]
        for s in (1, 2, 4):
            if reverse:
                x = x + jnp.where(row < sub - s, pltpu.roll(x, sub - s, 0), 0.0)
            else:
                x = x + jnp.where(row >= s, pltpu.roll(x, s, 0), 0.0)
        parts.append(x)
    b = [None] * nv
    order = list(reversed(range(nv))) if reverse else list(range(nv))
    edge = 0 if reverse else sub - 1
    carry = None
    for v in order:
        b[v] = parts[v] if carry is None else parts[v] + carry
        carry = b[v][edge:edge + 1, :]
    b_last = carry

    def anchor_row(v, r):
        return jnp.broadcast_to(b[v][r:r + 1, :], (sub, LANES))

    odd = (row & 1) == 1
    levels = []
    for lev in range(HG_LEVELS):
        m = 1 << lev
        pieces = []
        for v in range(nv):
            if m == 1:
                a = jnp.where(odd, pltpu.roll(b[v], 1, 0), b[v]) if reverse else jnp.where(odd, b[v], pltpu.roll(b[v], sub - 1, 0))
            elif m == 2:
                lo, hi = (1, 5) if reverse else (2, 6)
                a = jnp.where(row < 4, anchor_row(v, lo), anchor_row(v, hi))
            elif m == 4:
                a = anchor_row(v, 3 if reverse else 4)
            else:
                mv = m // sub
                first = (v // (2 * mv)) * 2 * mv
                a = anchor_row(first + mv - 1, sub - 1) if reverse else anchor_row(first + mv, 0)
            pieces.append(jnp.exp(-jnp.abs(b[v] - a)))
        levels.append(jnp.concatenate(pieces, axis=0))
    eb = jnp.concatenate([jnp.exp(bv) for bv in b], axis=0)
    erem = jnp.concatenate([jnp.exp(b_last - bv) for bv in b], axis=0)
    return levels, eb, erem


def _hgrn_kernel(q_ref, ff_ref, fb_ref, i_ref, g_ref, lb_ref, gn_ref, kf_ref, kb_ref,
                 o_ref, of_scr, ob_scr, st_scr, *, seq):
    c, nl = HG_CHUNK, HG_LEVELS
    nc = seq // c
    lb = lb_ref[...]
    st_scr[...] = jnp.zeros(st_scr.shape, F32)

    unroll = HG_UNROLL if nc % HG_UNROLL == 0 else 1
    dirs = (
        dict(f_ref=ff_ref, lbrow=lb[0:1, :], k_ref=kf_ref, o_scr=of_scr, last_row=c - 1, d=0),
        dict(f_ref=fb_ref, lbrow=lb[1:2, :], k_ref=kb_ref, o_scr=ob_scr, last_row=0, d=1),
    )

    def body(i, carry):
        work = []
        for u in range(unroll):
            cf = i * unroll + u
            work.append((dirs[0], pl.ds(pl.multiple_of(cf * c, c), c)))
            work.append((dirs[1], pl.ds(pl.multiple_of((nc - 1 - cf) * c, c), c)))
        gates = []
        for dr, rows in work:
            lbf = jnp.maximum(dr["lbrow"], LB_FLOOR)
            oml = 1.0 - dr["lbrow"]
            f = dr["f_ref"][rows, :]
            e = jnp.exp(-jnp.abs(f))
            r = 1.0 / (1.0 + e)
            er = e * r
            pos = f >= 0.0
            logf = jnp.log(lbf + oml * jnp.where(pos, r, er))
            kk = oml * jnp.where(pos, er, r)
            qh = q_ref[rows, :]
            gates.append((logf, kk, qh * _sigmoid(qh), i_ref[rows, :].astype(BF16)))
        decays = [_chunk_decays(g[0], dr["d"] == 1) for (dr, _), g in zip(work, gates)]
        amat = [dr["k_ref"][nl] * _dot_nt(g[2].astype(BF16), g[1].astype(BF16)) for (dr, _), g in zip(work, gates)]
        for lev in range(nl):
            for j, ((dr, _), g) in enumerate(zip(work, gates)):
                gl = decays[j][0][lev]
                amat[j] = amat[j] + dr["k_ref"][lev] * _dot_nt((g[2] * gl).astype(BF16), (g[1] * gl).astype(BF16))
        intra = [_dot(amat[j].astype(BF16), g[3]) for j, g in enumerate(gates)]
        upd = [_dot_tn(g[3], (g[1] * decays[j][2]).astype(BF16)) for j, g in enumerate(gates)]
        st = [st_scr[0], st_scr[1]]
        for j, ((dr, rows), g) in enumerate(zip(work, gates)):
            eb = decays[j][1]
            d = dr["d"]
            dr["o_scr"][rows, :] = _dot_nt((g[2] * eb).astype(BF16), st[d].astype(BF16)) + intra[j]
            st[d] = st[d] * eb[dr["last_row"]:dr["last_row"] + 1, :] + upd[j]
        st_scr[0] = st[0]
        st_scr[1] = st[1]
        return carry

    lax.fori_loop(0, nc // unroll, body, 0)

    ep = min(256, seq)
    gn = gn_ref[...]

    def epilogue(j, carry):
        rows = pl.ds(pl.multiple_of(j * ep, ep), ep)
        o = of_scr[rows, :] + ob_scr[rows, :]
        y = o * lax.rsqrt(jnp.mean(o * o, axis=-1, keepdims=True) + EPS) * gn
        g = g_ref[rows, :]
        o_ref[rows, :] = (y * (g * _sigmoid(g))).astype(BF16)
        return carry

    lax.fori_loop(0, seq // ep, epilogue, 0)


def _hgrn(hg, lower, hg_norm3, consts, l, batch, seq):
    t = hg.shape[0]
    kf, kb = consts
    hd = HG_HEAD_DIM

    def col(g):
        return pl.BlockSpec((seq, hd), lambda b, h: (b, g * HG_HEADS + h))

    full2 = lambda a: pl.BlockSpec(a.shape, lambda b, h: (0, 0))
    full3 = lambda a: pl.BlockSpec(a.shape, lambda b, h: (0, 0, 0))
    return pl.pallas_call(
        functools.partial(_hgrn_kernel, seq=seq),
        out_shape=jax.ShapeDtypeStruct((t, HG_WIDTH), BF16),
        grid=(batch, HG_HEADS),
        in_specs=[
            col(0), col(1), col(2), col(3), col(4),
            pl.BlockSpec((None, 2, hd), lambda b, h: (l, 0, h)),
            pl.BlockSpec((None, 1, hd), lambda b, h: (l, 0, 0)),
            full3(kf), full3(kb),
        ],
        out_specs=pl.BlockSpec((seq, hd), lambda b, h: (b, h)),
        scratch_shapes=[
            pltpu.VMEM((seq, hd), F32),
            pltpu.VMEM((seq, hd), F32),
            pltpu.VMEM((2, hd, hd), F32),
        ],
        compiler_params=_params(("arbitrary", "arbitrary")),
        name="hgrn2_scan",
    )(hg, hg, hg, hg, hg, lower, hg_norm3, kf, kb)


def _outproj_kernel(a_ref, o_ref, x_ref, mod_ref, n2_ref, w_ref, wr_ref, br_ref,
                    xo_ref, h2_ref, ti_ref, gt_ref, *, d):
    mod = mod_ref[...]
    y = _dot(a_ref[...], w_ref[0:ATT_WIDTH, :]) + _dot(o_ref[...], w_ref[ATT_WIDTH:ATT_WIDTH + HG_WIDTH, :])
    xn = x_ref[...] + mod[:, 2 * d:3 * d] * y
    xo_ref[...] = xn
    h2 = _rms_mod(xn, n2_ref[...], mod[:, 4 * d:5 * d], mod[:, 3 * d:4 * d])
    h2_ref[...] = _pack_bf16_pairs(h2)
    lg = _dot_f32_nt(wr_ref[...], h2) + br_ref[...]
    eidx = lax.broadcasted_iota(I32, lg.shape, 0)
    vals, idxs = [], []
    for _ in range(TOP_K):
        mx = jnp.max(lg, axis=0, keepdims=True)
        sel = jnp.min(jnp.where(lg == mx, eidx, N_EXPERTS), axis=0, keepdims=True)
        vals.append(mx)
        idxs.append(sel)
        lg = jnp.where(eidx == sel, -jnp.inf, lg)
    ex = [jnp.exp(v - vals[0]) for v in vals]
    den = ex[0]
    for e in ex[1:]:
        den = den + e
    inv = 1.0 / den
    ti_ref[...] = jnp.concatenate(idxs, axis=0)
    gt_ref[...] = jnp.concatenate([e * inv for e in ex], axis=0)


def _outproj(a, o, x2, mod3, norm2, w_out_bf, w_router_t, b_router3, l, seq):
    t, d = x2.shape
    tm = min(PROJ_TILE, seq)
    nb = mod3.shape[0] // norm2.shape[0]
    tok = lambda i: (i, 0)
    lane_tok = lambda i: (0, i)
    return pl.pallas_call(
        functools.partial(_outproj_kernel, d=d),
        out_shape=(
            jax.ShapeDtypeStruct((t, d), F32),
            jax.ShapeDtypeStruct((t, d // 2), U32),
            jax.ShapeDtypeStruct((TOP_K, t), I32),
            jax.ShapeDtypeStruct((TOP_K, t), F32),
        ),
        grid=(t // tm,),
        in_specs=[
            pl.BlockSpec((tm, ATT_WIDTH), tok),
            pl.BlockSpec((tm, HG_WIDTH), tok),
            pl.BlockSpec((tm, d), tok),
            pl.BlockSpec((None, 1, N_MOD * d), lambda i: (l * nb + (i * tm) // seq, 0, 0)),
            pl.BlockSpec((None, 1, d), lambda i: (l, 0, 0)),
            pl.BlockSpec((None, ATT_WIDTH + HG_WIDTH, d), lambda i: (l, 0, 0)),
            pl.BlockSpec((None, N_EXPERTS, d), lambda i: (l, 0, 0)),
            pl.BlockSpec((None, N_EXPERTS, 1), lambda i: (l, 0, 0)),
        ],
        out_specs=(
            pl.BlockSpec((tm, d), tok),
            pl.BlockSpec((tm, d // 2), tok),
            pl.BlockSpec((TOP_K, tm), lane_tok),
            pl.BlockSpec((TOP_K, tm), lane_tok),
        ),
        compiler_params=_params(("arbitrary",)),
        name="out_proj_router",
    )(a, o, x2, mod3, norm2, w_out_bf, w_router_t, b_router3)


def _rank_kernel(ti_ref, tri_ref, rank_ref, cnt_ref, carry_scr):
    @pl.when(pl.program_id(0) == 0)
    def _():
        carry_scr[...] = jnp.zeros(carry_scr.shape, F32)

    ti = ti_ref[...]
    tl = ti.shape[1]
    eidx = lax.broadcasted_iota(I32, (N_EXPERTS, tl), 0)
    carry = carry_scr[...]
    rows = []
    for k in range(TOP_K):
        oh = eidx == ti[k:k + 1, :]
        ohf = jnp.where(oh, 1.0, 0.0)
        pre = _dot(ohf.astype(BF16), tri_ref[...])
        rows.append(jnp.sum(jnp.where(oh, carry + pre, 0.0), axis=0, keepdims=True))
        carry = carry + jnp.sum(ohf, axis=1, keepdims=True)
    carry_scr[...] = carry
    rank_ref[...] = jnp.concatenate(rows, axis=0).astype(I32)
    cnt_ref[...] = jnp.broadcast_to(carry, cnt_ref.shape)


def _ranks(topi):
    k, t = topi.shape
    tl = min(RANK_TILE, t)
    tri = np.triu(np.ones((tl, tl), np.float32), 1)
    return pl.pallas_call(
        _rank_kernel,
        out_shape=(jax.ShapeDtypeStruct((k, t), I32), jax.ShapeDtypeStruct((N_EXPERTS, LANES), F32)),
        grid=(t // tl,),
        in_specs=[pl.BlockSpec((k, tl), lambda i: (0, i)), pl.BlockSpec((tl, tl), lambda i: (0, 0))],
        out_specs=(pl.BlockSpec((k, tl), lambda i: (0, i)), pl.BlockSpec((N_EXPERTS, LANES), lambda i: (0, 0))),
        scratch_shapes=[pltpu.VMEM((N_EXPERTS, 1), F32)],
        compiler_params=_params(("arbitrary",)),
        name="route_rank",
    )(topi, jnp.asarray(tri, BF16))


def _dest_kernel(cnt_ref, ltri_ref, ti_ref, rank_ref, dest_ref, bexp_ref, nused_ref, eblk_ref):
    cnt = cnt_ref[...]
    nblk = jnp.floor((cnt + (MOE_ROWS - 1)) * (1.0 / MOE_ROWS))
    pstart_b = _dot(ltri_ref[...], nblk.astype(BF16))
    pend_b = pstart_b + nblk
    pstart = (pstart_b[:, 0:1] * MOE_ROWS).astype(I32)
    ti = ti_ref[...]
    tl = ti.shape[1]
    eidx = lax.broadcasted_iota(I32, (N_EXPERTS, tl), 0)
    rows = []
    for k in range(TOP_K):
        oh = eidx == ti[k:k + 1, :]
        rows.append(jnp.sum(jnp.where(oh, pstart, 0), axis=0, keepdims=True))
    dest_ref[...] = jnp.concatenate(rows, axis=0) + rank_ref[...]
    nb = bexp_ref.shape[1]
    bi = lax.broadcasted_iota(I32, (N_EXPERTS, nb), 1).astype(F32)
    be = jnp.sum(jnp.where(pend_b[:, 0:1] <= bi, 1, 0), axis=0, keepdims=True)
    bexp_ref[...] = jnp.minimum(be, N_EXPERTS - 1).astype(I32)
    nused_ref[...] = pend_b[N_EXPERTS - 1:N_EXPERTS, :].astype(I32)
    eblk_ref[0] = pstart_b.astype(I32)
    eblk_ref[1] = nblk.astype(I32)


def _destinations(counts, topi, rank, nblocks):
    k, t = topi.shape
    tl = min(2048, t)
    nbp = -(-nblocks // LANES) * LANES
    ltri = np.tril(np.ones((N_EXPERTS, N_EXPERTS), np.float32), -1)
    return pl.pallas_call(
        _dest_kernel,
        out_shape=(
            jax.ShapeDtypeStruct((k, t), I32),
            jax.ShapeDtypeStruct((1, nbp), I32),
            jax.ShapeDtypeStruct((1, LANES), I32),
            jax.ShapeDtypeStruct((2, N_EXPERTS, LANES), I32),
        ),
        grid=(t // tl,),
        in_specs=[
            pl.BlockSpec((N_EXPERTS, LANES), lambda i: (0, 0)),
            pl.BlockSpec((N_EXPERTS, N_EXPERTS), lambda i: (0, 0)),
            pl.BlockSpec((k, tl), lambda i: (0, i)),
            pl.BlockSpec((k, tl), lambda i: (0, i)),
        ],
        out_specs=(
            pl.BlockSpec((k, tl), lambda i: (0, i)),
            pl.BlockSpec((1, nbp), lambda i: (0, 0)),
            pl.BlockSpec((1, LANES), lambda i: (0, 0)),
            pl.BlockSpec((2, N_EXPERTS, LANES), lambda i: (0, 0, 0)),
        ),
        compiler_params=_params(("arbitrary",)),
        name="route_dest",
    )(counts, jnp.asarray(ltri, BF16), topi, rank)


def _inverse_kernel(bexp_ref, nused_ref, dest_hbm, inv_hbm, dsm0, dsm1, inv_sm, isem, osem, *, t, tm):
    i = pl.program_id(0)
    nt = pl.num_programs(0)
    nblk = inv_sm.shape[0] // MOE_ROWS
    nused = nused_ref[0]
    dsm = (dsm0, dsm1)

    def idx_copy(j, p):
        return pltpu.make_async_copy(dest_hbm.at[j], dsm[p], isem.at[p])

    @pl.when(i == 0)
    def _():
        idx_copy(0, 0).start()

        def init_block(b, carry):
            nxt = jnp.minimum(b + 1, nblk - 1)
            partial = jnp.logical_or(b >= nused - 1, bexp_ref[nxt] != bexp_ref[b])

            @pl.when(partial)
            def _():
                row0 = b * MOE_ROWS
                spare0 = TOP_K * t + (b % 2) * MOE_ROWS

                def init_row(r, c2):
                    inv_sm[row0 + r] = spare0 + r
                    return c2

                lax.fori_loop(0, MOE_ROWS, init_row, 0, unroll=8)

            return carry

        lax.fori_loop(0, nblk, init_block, 0)

    tok0 = i * tm
    for p in range(2):
        @pl.when(i % 2 == p)
        def _():
            @pl.when(i + 1 < nt)
            def _():
                idx_copy(i + 1, 1 - p).start()

            idx_copy(i, p).wait()

            def body(r, carry):
                for k in range(TOP_K):
                    inv_sm[dsm[p][k * tm + r]] = k * t + tok0 + r
                return carry

            lax.fori_loop(0, tm, body, 0, unroll=8)

    @pl.when(i == nt - 1)
    def _():
        out = pltpu.make_async_copy(inv_sm, inv_hbm, osem)
        out.start()
        out.wait()


def _inverse_map(bexp, nused, dest2, nblocks, t):
    nt, ktm = dest2.shape
    grid_spec = pltpu.PrefetchScalarGridSpec(
        num_scalar_prefetch=2,
        grid=(nt,),
        in_specs=[pl.BlockSpec(memory_space=pl.ANY)],
        out_specs=pl.BlockSpec(memory_space=pl.ANY),
        scratch_shapes=[
            pltpu.SMEM((ktm,), I32),
            pltpu.SMEM((ktm,), I32),
            pltpu.SMEM((nblocks * MOE_ROWS,), I32),
            pltpu.SemaphoreType.DMA((2,)),
            pltpu.SemaphoreType.DMA(()),
        ],
    )
    inv = pl.pallas_call(
        functools.partial(_inverse_kernel, t=t, tm=ktm // TOP_K),
        out_shape=jax.ShapeDtypeStruct((nblocks * MOE_ROWS,), I32),
        grid_spec=grid_spec,
        compiler_params=_params(("arbitrary",)),
        name="route_inverse",
    )(bexp, nused, dest2)
    return inv.reshape(nblocks, MOE_ROWS)


def _pack_bf16_pairs(x):
    n = x.shape[1] // 2
    lo = lax.bitcast_convert_type(x[:, :n].astype(BF16).astype(F32), U32)
    hi = lax.bitcast_convert_type(x[:, n:].astype(BF16).astype(F32), U32)
    return hi | (lo >> 16)


def _unpack_bf16_pairs(w):
    lo = lax.bitcast_convert_type(w << 16, F32)
    hi = lax.bitcast_convert_type(w & jnp.uint32(0xFFFF0000), F32)
    return lo, hi


def _expert_kernel(bexp_ref, nused_ref, inv_hbm, h2_hbm, wgu_ref, bgu_ref, wd_ref, bd_ref, y4_hbm,
                   ig0, ig1, is0, is1, xbuf, ybuf, wgu_bf, wd_bf, gisem, sisem, gsem, ssem, *, dff, t):
    i = pl.program_id(0)
    nblk = pl.num_programs(0)
    nused = nused_ref[0]
    rows = xbuf.shape[1]
    half = xbuf.shape[2]
    used = i < nused
    fresh = jnp.logical_or(i == 0, bexp_ref[i] != bexp_ref[jnp.maximum(i - 1, 0)])
    gidx, sidx = (ig0, ig1), (is0, is1)

    def gidx_copy(j, p):
        return pltpu.make_async_copy(inv_hbm.at[j], gidx[p], gisem.at[p])

    def sidx_copy(j, p):
        return pltpu.make_async_copy(inv_hbm.at[j], sidx[p], sisem.at[p])

    def gather_rows(p):
        def body(rr, carry):
            for u in range(2):
                r = rr * 2 + u
                slot_row = gidx[p][r]
                tok = slot_row & (t - 1) if t & (t - 1) == 0 else lax.rem(slot_row, t)
                pltpu.make_async_copy(h2_hbm.at[pl.ds(tok, 1), :], xbuf.at[p, pl.ds(r, 1), :],
                                      gsem.at[p]).start(priority=u)
            return carry

        for rr in range(rows // 2):
            body(rr, 0)

    def gather_wait(p):
        pltpu.make_async_copy(h2_hbm.at[pl.ds(0, rows), :], xbuf.at[p], gsem.at[p]).wait()

    def scatter_rows(p):
        def body(rr, carry):
            for u in range(2):
                r = rr * 2 + u
                pltpu.make_async_copy(ybuf.at[p, pl.ds(r, 1), :], y4_hbm.at[pl.ds(sidx[p][r], 1), :],
                                      ssem.at[p]).start(priority=u)
            return carry

        for rr in range(rows // 2):
            body(rr, 0)

    def scatter_wait(p):
        pltpu.make_async_copy(ybuf.at[p], y4_hbm.at[pl.ds(0, rows), :], ssem.at[p]).wait()

    @pl.when(i == 0)
    def _():
        ybuf[...] = jnp.zeros(ybuf.shape, U32)
        for s in range(2):
            spare = pltpu.make_async_copy(ybuf.at[s], y4_hbm.at[pl.ds(TOP_K * t + s * rows, rows), :], ssem.at[s])
            spare.start()
            spare.wait()
        gidx_copy(0, 0).start()
        sidx_copy(0, 0).start()

        @pl.when(nused > 1)
        def _():
            gidx_copy(1, 1).start()

        gidx_copy(0, 0).wait()
        gather_rows(0)

    @pl.when(jnp.logical_and(used, fresh))
    def _():
        wgu_bf[...] = wgu_ref[...].astype(BF16)
        wd_bf[...] = wd_ref[...].astype(BF16)

    for p in range(2):
        q = 1 - p

        @pl.when(i % 2 == p)
        def _():
            @pl.when(i + 1 < nused)
            def _():
                gidx_copy(i + 1, q).wait()
                gather_rows(q)
                sidx_copy(i + 1, q).start()

            @pl.when(i + 2 < nused)
            def _():
                gidx_copy(i + 2, p).start()

            @pl.when(jnp.logical_and(i >= 2, i - 2 < nused))
            def _():
                scatter_wait(p)

            @pl.when(used)
            def _():
                gather_wait(p)
                lo, hi = _unpack_bf16_pairs(xbuf[p])
                hu = (_dot(lo.astype(BF16), wgu_bf[0:half, :]) + _dot(hi.astype(BF16), wgu_bf[half:2 * half, :])
                      + bgu_ref[...])
                glu = jnp.minimum(hu[:, 0:dff], SWIGLU_LIMIT)
                lin = jnp.clip(hu[:, dff:2 * dff], -SWIGLU_LIMIT, SWIGLU_LIMIT)
                act = glu * _sigmoid(SWIGLU_ALPHA * glu) * (lin + 1.0)
                ybuf[p] = _pack_bf16_pairs(_dot(act.astype(BF16), wd_bf[...]) + bd_ref[...])
                sidx_copy(i, p).wait()
                scatter_rows(p)

            @pl.when(i == nblk - 1)
            def _():
                @pl.when(jnp.logical_and(i >= 1, i - 1 < nused))
                def _():
                    scatter_wait(q)

                @pl.when(used)
                def _():
                    scatter_wait(p)


def _experts(bexp, nused, inv, h2p, w_gu, b_gu4, w_down, b_down4, l, t):
    nblocks = inv.shape[0]
    half = h2p.shape[1]
    d = 2 * half
    dff = w_down.shape[2]

    wsel = lambda i, be, nu: (l, be[jnp.minimum(i, nu[0] - 1)], 0, 0)
    grid_spec = pltpu.PrefetchScalarGridSpec(
        num_scalar_prefetch=2,
        grid=(nblocks,),
        in_specs=[
            pl.BlockSpec(memory_space=pl.ANY),
            pl.BlockSpec(memory_space=pl.ANY),
            pl.BlockSpec((None, None, d, 2 * dff), wsel),
            pl.BlockSpec((None, None, 1, 2 * dff), wsel),
            pl.BlockSpec((None, None, dff, d), wsel),
            pl.BlockSpec((None, None, 1, d), wsel),
        ],
        out_specs=pl.BlockSpec(memory_space=pl.ANY),
        scratch_shapes=[pltpu.SMEM((MOE_ROWS,), I32)] * 4 + [
            pltpu.VMEM((2, MOE_ROWS, half), U32),
            pltpu.VMEM((2, MOE_ROWS, half), U32),
            pltpu.VMEM((d, 2 * dff), BF16),
            pltpu.VMEM((dff, d), BF16),
        ] + [pltpu.SemaphoreType.DMA((2,))] * 4,
    )
    return pl.pallas_call(
        functools.partial(_expert_kernel, dff=dff, t=t),
        out_shape=jax.ShapeDtypeStruct((TOP_K * t + 2 * MOE_ROWS, half), U32),
        grid_spec=grid_spec,
        compiler_params=_params(("arbitrary",)),
        name="moe_experts",
    )(bexp, nused, inv, h2p, w_gu, b_gu4, w_down, b_down4)


def _combine_kernel(*refs, d, final):
    y_refs = refs[:TOP_K]
    gt_ref, x_ref, mod_ref = refs[TOP_K:TOP_K + 3]
    xo_ref = refs[-1]
    half = d // 2
    gt = gt_ref[...]
    acc_lo = acc_hi = None
    for k in range(TOP_K):
        lo, hi = _unpack_bf16_pairs(y_refs[k][...])
        g = gt[:, k:k + 1]
        acc_lo = g * lo if acc_lo is None else acc_lo + g * lo
        acc_hi = g * hi if acc_hi is None else acc_hi + g * hi
    x_lo = x_ref[:, 0:half] + mod_ref[:, 5 * d:5 * d + half] * acc_lo
    x_hi = x_ref[:, half:d] + mod_ref[:, 5 * d + half:6 * d] * acc_hi
    if final:
        gain = refs[TOP_K + 3][...]
        ssq = jnp.sum(x_lo * x_lo, axis=-1, keepdims=True) + jnp.sum(x_hi * x_hi, axis=-1, keepdims=True)
        inv = lax.rsqrt(ssq * (1.0 / d) + EPS)
        x_lo = x_lo * inv * gain[:, 0:half]
        x_hi = x_hi * inv * gain[:, half:d]
    xo_ref[:, 0:half] = x_lo
    xo_ref[:, half:d] = x_hi


def _combine(y4p, gates_t, x2, mod3, l, seq, depth, final_gain=None):
    t, d = x2.shape
    tm = min(PROJ_TILE, seq)
    nb = mod3.shape[0] // depth
    nt = t // tm
    final = final_gain is not None

    def slot(k):
        return pl.BlockSpec((tm, d // 2), lambda i: (k * nt + i, 0))

    extra_specs = [pl.BlockSpec((1, d), lambda i: (0, 0))] if final else []
    extra_args = [final_gain.reshape(1, d)] if final else []
    return pl.pallas_call(
        functools.partial(_combine_kernel, d=d, final=final),
        out_shape=jax.ShapeDtypeStruct((t, d), F32),
        grid=(nt,),
        in_specs=[slot(k) for k in range(TOP_K)] + [
            pl.BlockSpec((tm, TOP_K), lambda i: (i, 0)),
            pl.BlockSpec((tm, d), lambda i: (i, 0)),
            pl.BlockSpec((None, 1, N_MOD * d), lambda i: (l * nb + (i * tm) // seq, 0, 0)),
        ] + extra_specs,
        out_specs=pl.BlockSpec((tm, d), lambda i: (i, 0)),
        compiler_params=_params(("arbitrary",)),
        name="moe_combine",
    )(*([y4p] * TOP_K), gates_t, x2, mod3, *extra_args)


def _final_kernel(x_ref, g_ref, o_ref):
    x = x_ref[...]
    o_ref[...] = x * lax.rsqrt(jnp.mean(x * x, axis=-1, keepdims=True) + EPS) * g_ref[...]


def _final_norm(x2, gain):
    t, d = x2.shape
    tm = min(PROJ_TILE, t)
    return pl.pallas_call(
        _final_kernel,
        out_shape=jax.ShapeDtypeStruct((t, d), F32),
        grid=(t // tm,),
        in_specs=[pl.BlockSpec((tm, d), lambda i: (i, 0)), pl.BlockSpec((1, d), lambda i: (0, 0))],
        out_specs=pl.BlockSpec((tm, d), lambda i: (i, 0)),
        compiler_params=_params(("arbitrary",)),
        name="final_norm",
    )(x2, gain.reshape(1, d))


def _mixer_layer(x2, mod3, tables, lower, consts, p, l, batch, seq):
    ra, rb, rc = tables
    qa, ka2, va2, hg = _inproj(x2, mod3, p["norm1"], p["w_in"], ra, rb, rc, l, seq)
    a = _attention(qa, ka2, va2, p["attn_sink"], p["attn_norm"], l, batch, seq)
    o = _hgrn(hg, lower, p["hg_norm"], consts, l, batch, seq)
    return a, o


SC_WINDOW = 128


def _sc_mesh():
    return plsc.VectorSubcoreMesh(core_axis_name="core", subcore_axis_name="subcore")


def _sc_worker(mesh):
    return lax.axis_index("core") * mesh.num_subcores + lax.axis_index("subcore"), mesh.num_cores * mesh.num_subcores


def _sc_scatter_rows(src, dest_rows, n_out):
    n, width = src.shape
    nk = dest_rows.shape[0]
    mesh = _sc_mesh()

    half = SC_WINDOW // 2

    @pl.kernel(out_type=jax.ShapeDtypeStruct((n_out, width), src.dtype), mesh=mesh,
               scratch_types=[pltpu.VMEM((nk, SC_WINDOW), I32), pltpu.VMEM((2, half, width), src.dtype),
                              pltpu.SemaphoreType.DMA((2,))],
               name="sc_scatter_rows")
    def scatter_kernel(src_hbm, idx_hbm, out_hbm, idx_v, rows_v, sem):
        wid, nw = _sc_worker(mesh)
        per = n // SC_WINDOW // nw
        row0 = wid * per * SC_WINDOW

        def load(s, buf):
            return pltpu.make_async_copy(src_hbm.at[pl.ds(row0 + s * half, half)], rows_v.at[buf], sem.at[buf])

        load(0, 0).start()

        @pl.loop(0, per)
        def _(j):
            pltpu.sync_copy(idx_hbm.at[:, pl.ds(row0 + j * SC_WINDOW, SC_WINDOW)], idx_v)
            for h in range(2):
                if h == 0:
                    load(2 * j + 1, 1).start()
                else:
                    @pl.when(j + 1 < per)
                    def _():
                        load(2 * j + 2, 0).start()

                load(2 * j + h, h).wait()
                for k in range(nk):
                    pltpu.sync_copy(rows_v.at[h], out_hbm.at[idx_v.at[k, pl.ds(h * half, half)]])

    return scatter_kernel(src, dest_rows)


def _sc_gather_rows(src, rows):
    n = rows.shape[0]
    width = src.shape[1]
    mesh = _sc_mesh()

    half = SC_WINDOW // 2

    @pl.kernel(out_type=jax.ShapeDtypeStruct((n, width), src.dtype), mesh=mesh,
               scratch_types=[pltpu.VMEM((1, SC_WINDOW), I32), pltpu.VMEM((2, half, width), src.dtype),
                              pltpu.SemaphoreType.DMA((2,))],
               name="sc_gather_rows")
    def gather_kernel(src_hbm, idx_hbm, out_hbm, idx_v, rows_v, sem):
        wid, nw = _sc_worker(mesh)
        per = n // SC_WINDOW // nw
        row0 = wid * per * SC_WINDOW

        def store(s, buf):
            return pltpu.make_async_copy(rows_v.at[buf], out_hbm.at[pl.ds(row0 + s * half, half)], sem.at[buf])

        @pl.loop(0, per)
        def _(j):
            pltpu.sync_copy(idx_hbm.at[:, pl.ds(row0 + j * SC_WINDOW, SC_WINDOW)], idx_v)
            for h in range(2):
                @pl.when(j >= 1)
                def _():
                    store(2 * j + h - 2, h).wait()

                pltpu.sync_copy(src_hbm.at[idx_v.at[0, pl.ds(h * half, half)]], rows_v.at[h])
                store(2 * j + h, h).start()

        for h in range(2):
            store(2 * per - 2 + h, h).wait()

    return gather_kernel(src, rows.reshape(1, n))


def _expert_group_kernel(eb_ref, xs_hbm, wgu_hbm, bgu_ref, wd_hbm, bd_ref, ys_hbm,
                         xbuf, ybuf, wgu_f, wd_f, wgu_bf, wd_bf, act_scr, xsem, ysem, wsem, *, dff, l):
    e = pl.program_id(0)
    ne = pl.num_programs(0)
    b0 = eb_ref[e]
    nb = eb_ref[ne + e]
    rows, half = xbuf.shape[1], xbuf.shape[2]
    group = rows // MOE_ROWS
    nch = (nb + group - 1) // group
    wslot = e % 2

    def w_copies(ex, slot):
        cps = []
        for src, dst, first in ((wgu_hbm, wgu_f, 0), (wd_hbm, wd_f, WEIGHT_CHUNKS)):
            step = dst.shape[1] // WEIGHT_CHUNKS
            for c in range(WEIGHT_CHUNKS):
                cps.append(pltpu.make_async_copy(src.at[l, ex, pl.ds(c * step, step), :],
                                                 dst.at[slot, pl.ds(c * step, step), :], wsem.at[slot, first + c]))
        return cps

    @pl.when(e == 0)
    def _():
        for c, cp in enumerate(w_copies(0, 0)):
            cp.start(priority=c % 2)

    @pl.when(e + 1 < ne)
    def _():
        for c, cp in enumerate(w_copies(e + 1, 1 - wslot)):
            cp.start(priority=c % 2)

    for cp in w_copies(e, wslot):
        cp.wait()

    def x_copy(j, slot):
        start = (b0 + j * group) * MOE_ROWS
        return pltpu.make_async_copy(xs_hbm.at[pl.ds(start, rows), :], xbuf.at[slot], xsem.at[slot])

    def y_copy(j, slot, g):
        start = (b0 + j * group + g) * MOE_ROWS
        return pltpu.make_async_copy(ybuf.at[slot, pl.ds(g * MOE_ROWS, MOE_ROWS), :],
                                     ys_hbm.at[pl.ds(start, MOE_ROWS), :], ysem.at[slot, g])

    def y_each(j, slot, fn):
        for g in range(group):
            @pl.when(j * group + g < nb)
            def _():
                fn(y_copy(j, slot, g))

    @pl.when(nb > 0)
    def _():
        x_copy(0, 0).start(priority=1)
        wgu_bf[...] = wgu_f[wslot].astype(BF16)
        wd_bf[...] = wd_f[wslot].astype(BF16)

        def chunk(j, carry):
            slot = j % 2

            @pl.when(j + 1 < nch)
            def _():
                x_copy(j + 1, 1 - slot).start(priority=1)

            x_copy(j, slot).wait()

            @pl.when(j >= 2)
            def _():
                y_each(j - 2, slot, lambda cp: cp.wait())

            lo, hi = _unpack_bf16_pairs(xbuf[slot])
            xb = jnp.concatenate([lo.astype(BF16), hi.astype(BF16)], axis=1)
            for c0 in range(0, dff, EXPERT_COLS):
                gate = _dot(xb, wgu_bf[:, c0:c0 + EXPERT_COLS]) + bgu_ref[:, c0:c0 + EXPERT_COLS]
                up = (_dot(xb, wgu_bf[:, dff + c0:dff + c0 + EXPERT_COLS])
                      + bgu_ref[:, dff + c0:dff + c0 + EXPERT_COLS])
                glu = jnp.minimum(gate, SWIGLU_LIMIT)
                lin = jnp.clip(up, -SWIGLU_LIMIT, SWIGLU_LIMIT)
                act_scr[:, c0:c0 + EXPERT_COLS] = (glu * _sigmoid(SWIGLU_ALPHA * glu) * (lin + 1.0)).astype(BF16)
            ybuf[slot] = _pack_bf16_pairs(_dot(act_scr[...], wd_bf[...]) + bd_ref[...])
            y_each(j, slot, lambda cp: cp.start(priority=1))
            return carry

        lax.fori_loop(0, nch, chunk, 0)

        @pl.when(nch >= 2)
        def _():
            y_each(nch - 2, nch % 2, lambda cp: cp.wait())

        y_each(nch - 1, (nch - 1) % 2, lambda cp: cp.wait())


def _block_experts(eblk, xs, w_gu, b_gu4, w_down, b_down4, l):
    cap, half = xs.shape
    d = 2 * half
    dff = w_down.shape[2]
    wsel = lambda e, eb: (l, e, 0, 0)
    grid_spec = pltpu.PrefetchScalarGridSpec(
        num_scalar_prefetch=1,
        grid=(N_EXPERTS,),
        in_specs=[
            pl.BlockSpec(memory_space=pl.ANY),
            pl.BlockSpec(memory_space=pl.ANY),
            pl.BlockSpec((None, None, 1, 2 * dff), wsel),
            pl.BlockSpec(memory_space=pl.ANY),
            pl.BlockSpec((None, None, 1, d), wsel),
        ],
        out_specs=pl.BlockSpec(memory_space=pl.ANY),
        scratch_shapes=[
            pltpu.VMEM((2, EXPERT_GROUP * MOE_ROWS, half), U32),
            pltpu.VMEM((2, EXPERT_GROUP * MOE_ROWS, half), U32),
            pltpu.VMEM((2, d, 2 * dff), F32),
            pltpu.VMEM((2, dff, d), F32),
            pltpu.VMEM((d, 2 * dff), BF16),
            pltpu.VMEM((dff, d), BF16),
            pltpu.VMEM((EXPERT_GROUP * MOE_ROWS, dff), BF16),
            pltpu.SemaphoreType.DMA((2,)),
            pltpu.SemaphoreType.DMA((2, EXPERT_GROUP)),
            pltpu.SemaphoreType.DMA((2, 2 * WEIGHT_CHUNKS)),
        ],
    )
    return pl.pallas_call(
        functools.partial(_expert_group_kernel, dff=dff, l=l),
        out_shape=jax.ShapeDtypeStruct((cap - (EXPERT_GROUP - 1) * MOE_ROWS, half), U32),
        grid_spec=grid_spec,
        compiler_params=_params(("arbitrary",)),
        name="moe_expert_groups",
    )(eblk, xs, w_gu, b_gu4, w_down, b_down4)


def _moe_layer_sc(a, o, x2, mod3, p, l, seq, depth, final_gain=None):
    t, d = x2.shape
    xn, h2p, topi, gates = _outproj(a, o, x2, mod3, p["norm2"], p["w_out"], p["w_router_t"], p["b_router"], l, seq)
    rank, counts = _ranks(topi)
    nblocks = (t * TOP_K + N_EXPERTS * MOE_ROWS) // MOE_ROWS
    dest, _, _, eblk = _destinations(counts, topi, rank, nblocks)
    xs = _sc_scatter_rows(h2p, dest, (nblocks + EXPERT_GROUP - 1) * MOE_ROWS)
    ys = _block_experts(eblk[:, :, 0].reshape(-1), xs, p["w_gu"], p["b_gu"], p["w_down"], p["b_down"], l)
    y4p = _sc_gather_rows(ys, dest.reshape(-1))
    return _combine(y4p, gates.T, xn, mod3, l, seq, depth, final_gain)


def _moe_layer(a, o, x2, mod3, p, l, seq, depth):
    t, d = x2.shape
    xn, h2p, topi, gates = _outproj(a, o, x2, mod3, p["norm2"], p["w_out"], p["w_router_t"], p["b_router"], l, seq)
    rank, counts = _ranks(topi)
    nblocks = (t * TOP_K + N_EXPERTS * MOE_ROWS) // MOE_ROWS
    dest, bexp, nused, _ = _destinations(counts, topi, rank, nblocks)
    tm = min(TOK_TILE, t)
    dest2 = dest.reshape(TOP_K, t // tm, tm).transpose(1, 0, 2).reshape(t // tm, TOP_K * tm)
    bexp, nused = bexp.reshape(-1), nused.reshape(-1)
    inv = _inverse_map(bexp, nused, dest2, nblocks, t)
    y4p = _experts(bexp, nused, inv, h2p, p["w_gu"], p["b_gu"], p["w_down"], p["b_down"], l, t)
    return _combine(y4p, gates.T, xn, mod3, l, seq, depth)


def kernel(x, c, positions, w_ada, b_ada, norm1, w_in, attn_sink, attn_norm, hg_lb_logits, hg_norm, w_out, norm2,
           w_router, b_router, w_gu, b_gu, w_down, b_down, final_norm):
    batch, seq, d = x.shape
    depth = w_ada.shape[0]
    t = batch * seq
    p = {
        "norm1": norm1.reshape(depth, 1, d),
        "w_in": w_in.astype(BF16),
        "attn_sink": attn_sink.astype(F32),
        "attn_norm": attn_norm.reshape(depth, 1, ATT_WIDTH),
        "hg_norm": hg_norm.reshape(depth, 1, HG_HEAD_DIM),
        "w_out": w_out.astype(BF16),
        "norm2": norm2.reshape(depth, 1, d),
        "w_router_t": jnp.swapaxes(w_router, 1, 2),
        "b_router": b_router.reshape(depth, N_EXPERTS, 1),
        "w_gu": w_gu,
        "b_gu": b_gu.reshape(depth, N_EXPERTS, 1, b_gu.shape[-1]),
        "w_down": w_down,
        "b_down": b_down.reshape(depth, N_EXPERTS, 1, d),
    }
    mod3 = _ada_all(c, w_ada, b_ada).reshape(depth * batch, 1, N_MOD * d)
    lower = _lower_bounds(hg_lb_logits)
    tables = _rope_tables(positions)
    consts = tuple(jnp.asarray(m) for m in _hgrn_constants())
    x2 = x.reshape(t, d)
    for l in range(depth):
        a, o = _mixer_layer(x2, mod3, tables, lower, consts, p, l, batch, seq)
        x2 = _moe_layer_sc(a, o, x2, mod3, p, l, seq, depth, final_norm if l == depth - 1 else None)
    return x2.reshape(batch, seq, d)
```

```python
import functools

import numpy as np
import jax
import jax.numpy as jnp
from jax import lax
from jax.experimental import pallas as pl
from jax.experimental.pallas import tpu as pltpu
from jax.experimental.pallas import tpu_sc as plsc

F32 = jnp.float32
BF16 = jnp.bfloat16
I32 = jnp.int32
U32 = jnp.uint32

ATT_HEADS = 8
ATT_KV_HEADS = 2
ATT_HEAD_DIM = 64
ATT_WIDTH = ATT_HEADS * ATT_HEAD_DIM
KV_WIDTH = ATT_KV_HEADS * ATT_HEAD_DIM
WINDOW = 128
ATT_BLOCK = 128
ROPE_THETA = 500000.0
ROPE_DIM = ATT_HEAD_DIM // 4
HG_HEADS = 4
HG_HEAD_DIM = 128
HG_WIDTH = HG_HEADS * HG_HEAD_DIM
N_EXPERTS = 32
TOP_K = 4
SWIGLU_ALPHA = 1.702
SWIGLU_LIMIT = 7.0
N_MOD = 6
EPS = 1e-6
NEG_INF = -1e30
LB_FLOOR = 1e-30

LANES = 128
HG_CHUNK = 64
HG_LEVELS = 6
HG_UNROLL = 8
MOE_ROWS = 256
EXPERT_GROUP = 1
WEIGHT_CHUNKS = 4
EXPERT_COLS = 512
MOE_COMBINE_PARTS = 2
PROJ_TILE = 512
RANK_TILE = 512
VMEM_LIMIT = 56 * 1024 * 1024


def _dot(a, b):
    return jnp.dot(a, b, preferred_element_type=F32)


def _dot_nt(a, b):
    return lax.dot_general(a, b, (((1,), (1,)), ((), ())), preferred_element_type=F32)


def _dot_tn(a, b):
    return lax.dot_general(a, b, (((0,), (0,)), ((), ())), preferred_element_type=F32)


def _split3(x):
    hi = x.astype(BF16)
    r1 = x - hi.astype(F32)
    mid = r1.astype(BF16)
    lo = (r1 - mid.astype(F32)).astype(BF16)
    return hi, mid, lo


def _dot_f32_nt(a, b):
    ah, am, _ = _split3(a)
    bh, bm, _ = _split3(b)
    return _dot_nt(ah, bh) + _dot_nt(ah, bm) + _dot_nt(am, bh)


def _dot_f32(a, b):
    ah, am, _ = _split3(a)
    bh, bm, _ = _split3(b)
    return _dot(ah, bh) + _dot(ah, bm) + _dot(am, bh)


def _sigmoid(x):
    return 1.0 / (1.0 + jnp.exp(-x))


def _params(sem=None):
    return pltpu.CompilerParams(dimension_semantics=sem, vmem_limit_bytes=VMEM_LIMIT)


def _ada_kernel(c_ref, w_ref, b_ref, o_ref):
    c = c_ref[...]
    cond = c * _sigmoid(c)
    o_ref[...] = _dot_f32(cond, w_ref[...]) + b_ref[...]


def _ada_all(c, w_ada, b_ada):
    depth, d, n = w_ada.shape
    b = c.shape[0]
    nt = n // d
    return pl.pallas_call(
        _ada_kernel,
        out_shape=jax.ShapeDtypeStruct((depth, b, n), F32),
        grid=(depth, nt),
        in_specs=[
            pl.BlockSpec((b, d), lambda l, j: (0, 0)),
            pl.BlockSpec((None, d, d), lambda l, j: (l, 0, j)),
            pl.BlockSpec((None, 1, d), lambda l, j: (l, 0, j)),
        ],
        out_specs=pl.BlockSpec((None, b, d), lambda l, j: (l, 0, j)),
        compiler_params=_params(("arbitrary", "arbitrary")),
        name="ada_mod",
    )(c, w_ada, b_ada.reshape(depth, 1, n))


def _lb_kernel(x_ref, o_ref):
    depth = x_ref.shape[0]
    xs = [x_ref[l] for l in range(depth)]
    m = xs[0]
    for l in range(1, depth):
        m = jnp.maximum(m, xs[l])
    es = [jnp.exp(v - m) for v in xs]
    den = es[0]
    for l in range(1, depth):
        den = den + es[l]
    ps = [e / den for e in es]
    run = ps[0]
    o_ref[0] = run - ps[0]
    for l in range(1, depth):
        run = run + ps[l]
        o_ref[l] = run - ps[0]


def _lower_bounds(hg_lb_logits):
    return pl.pallas_call(
        _lb_kernel,
        out_shape=jax.ShapeDtypeStruct(hg_lb_logits.shape, F32),
        name="hg_lower_bounds",
    )(hg_lb_logits.astype(F32))


def _rope_kernel(pos_ref, invf_ref, a_ref, b_ref, c_ref):
    pos = pos_ref[...].astype(F32)
    ang = pos * invf_ref[...]
    cs = jnp.cos(ang)
    sn = jnp.sin(ang)
    lane = lax.broadcasted_iota(I32, ang.shape, 1) & (ATT_HEAD_DIM - 1)
    half = ROPE_DIM // 2
    first = lane < half
    second = (lane >= half) & (lane < ROPE_DIM)
    a_ref[...] = jnp.where(first | second, cs, 1.0)
    b_ref[...] = jnp.where(first, -sn, 0.0)
    c_ref[...] = jnp.where(second, sn, 0.0)


def _rope_tables(positions):
    t = positions.size
    half = ROPE_DIM // 2
    inv = (np.float32(ROPE_THETA) ** (-(np.arange(half, dtype=np.float32) * np.float32(2.0) / np.float32(ROPE_DIM)))).astype(np.float32)
    lane = np.arange(LANES) % ATT_HEAD_DIM
    pat = np.where(lane < ROPE_DIM, inv[lane % half], 0.0).astype(np.float32).reshape(1, LANES)
    tm = min(t, 2048)
    shp = jax.ShapeDtypeStruct((t, LANES), F32)
    spec = pl.BlockSpec((tm, LANES), lambda i: (i, 0))
    return pl.pallas_call(
        _rope_kernel,
        out_shape=(shp, shp, shp),
        grid=(t // tm,),
        in_specs=[pl.BlockSpec((tm, 1), lambda i: (i, 0)), pl.BlockSpec((1, LANES), lambda i: (0, 0))],
        out_specs=(spec, spec, spec),
        compiler_params=_params(("arbitrary",)),
        name="rope_tables",
    )(positions.reshape(t, 1).astype(I32), jnp.asarray(pat))


def _rms_mod(x, gain, scale, shift):
    ms = jnp.mean(x * x, axis=-1, keepdims=True)
    return (x * lax.rsqrt(ms + EPS) * gain) * (1.0 + scale) + shift


def _rope_apply(x, a, b, c):
    half = ROPE_DIM // 2
    return x * a + pltpu.roll(x, LANES - half, 1) * b + pltpu.roll(x, half, 1) * c


def _inproj_kernel(x_ref, mod_ref, n1_ref, w_ref, ra_ref, rb_ref, rc_ref, qa_ref, ka_ref, va_ref, hg_ref, *, d):
    mod = mod_ref[...]
    h = _rms_mod(x_ref[...], n1_ref[...], mod[:, d:2 * d], mod[:, 0:d]).astype(BF16)
    a, b, c = ra_ref[...], rb_ref[...], rc_ref[...]
    kvw = ATT_WIDTH + 2 * KV_WIDTH
    pa = _dot(h, w_ref[:, 0:kvw])
    scale = ATT_HEAD_DIM ** -0.5
    for g in range(ATT_WIDTH // LANES):
        qg = _rope_apply(pa[:, g * LANES:(g + 1) * LANES], a, b, c)
        qa_ref[:, g * LANES:(g + 1) * LANES] = (qg * scale).astype(BF16)
    k = _rope_apply(pa[:, ATT_WIDTH:ATT_WIDTH + KV_WIDTH], a, b, c)
    v = pa[:, ATT_WIDTH + KV_WIDTH:kvw]
    ka_ref[:, 0:LANES] = k.astype(BF16)
    ka_ref[:, LANES:2 * LANES] = pltpu.roll(k, ATT_HEAD_DIM, 1).astype(BF16)
    va_ref[:, 0:LANES] = v.astype(BF16)
    va_ref[:, LANES:2 * LANES] = pltpu.roll(v, ATT_HEAD_DIM, 1).astype(BF16)
    for g in range(5):
        lo = kvw + g * HG_WIDTH
        hg_ref[:, g * HG_WIDTH:(g + 1) * HG_WIDTH] = _dot(h, w_ref[:, lo:lo + HG_WIDTH])


def _inproj(x2, mod3, norm1, w_in_bf, ra, rb, rc, l, seq):
    t, d = x2.shape
    n_in = w_in_bf.shape[-1]
    tm = min(PROJ_TILE, seq)
    nb = mod3.shape[0] // norm1.shape[0]
    tok = lambda i: (i, 0)
    return pl.pallas_call(
        functools.partial(_inproj_kernel, d=d),
        out_shape=(
            jax.ShapeDtypeStruct((t, ATT_WIDTH), BF16),
            jax.ShapeDtypeStruct((t, 2 * KV_WIDTH), BF16),
            jax.ShapeDtypeStruct((t, 2 * KV_WIDTH), BF16),
            jax.ShapeDtypeStruct((t, 5 * HG_WIDTH), F32),
        ),
        grid=(t // tm,),
        in_specs=[
            pl.BlockSpec((tm, d), tok),
            pl.BlockSpec((None, 1, N_MOD * d), lambda i: (l * nb + (i * tm) // seq, 0, 0)),
            pl.BlockSpec((None, 1, d), lambda i: (l, 0, 0)),
            pl.BlockSpec((None, d, n_in), lambda i: (l, 0, 0)),
            pl.BlockSpec((tm, LANES), tok),
            pl.BlockSpec((tm, LANES), tok),
            pl.BlockSpec((tm, LANES), tok),
        ],
        out_specs=(
            pl.BlockSpec((tm, ATT_WIDTH), tok),
            pl.BlockSpec((tm, 2 * KV_WIDTH), tok),
            pl.BlockSpec((tm, 2 * KV_WIDTH), tok),
            pl.BlockSpec((tm, 5 * HG_WIDTH), tok),
        ),
        compiler_params=_params(("arbitrary",)),
        name="in_proj",
    )(x2, mod3, norm1, w_in_bf, ra, rb, rc)


def _attn_kernel(sink_ref, q_ref, kp_ref, kc_ref, kn_ref, vp_ref, vc_ref, vn_ref, gain_ref, o_ref, *, l, seq):
    n = pl.program_id(1)
    blk = ATT_BLOCK
    k2 = jnp.concatenate([kp_ref[...], kc_ref[...], kn_ref[...]], axis=0)
    v2 = jnp.concatenate([vp_ref[...], vc_ref[...], vn_ref[...]], axis=0)
    lane = lax.broadcasted_iota(I32, (3 * blk, LANES), 1)
    lo_half = lane < ATT_HEAD_DIM
    zero = jnp.zeros((3 * blk, LANES), BF16)
    ka, kb = k2[:, 0:LANES], k2[:, LANES:2 * LANES]
    va, vb = v2[:, 0:LANES], v2[:, LANES:2 * LANES]
    kz = [[jnp.where(lo_half, ka, zero), jnp.where(lo_half, zero, kb)],
          [jnp.where(lo_half, kb, zero), jnp.where(lo_half, zero, ka)]]
    vz = [[jnp.where(lo_half, va, zero), jnp.where(lo_half, zero, vb)],
          [jnp.where(lo_half, vb, zero), jnp.where(lo_half, zero, va)]]
    qpos = n * blk + lax.broadcasted_iota(I32, (blk, 3 * blk), 0)
    kpos = (n - 1) * blk + lax.broadcasted_iota(I32, (blk, 3 * blk), 1)
    valid = (jnp.abs(qpos - kpos) <= WINDOW) & (kpos >= 0) & (kpos < seq)
    valid2 = jnp.concatenate([valid, valid], axis=0)
    upper = lax.broadcasted_iota(I32, (2 * blk, 1), 0) < blk
    outs = []
    for j in range(ATT_KV_HEADS):
        qs = jnp.concatenate([q_ref[:, 2 * j * LANES:(2 * j + 1) * LANES],
                              q_ref[:, (2 * j + 1) * LANES:(2 * j + 2) * LANES]], axis=0)
        acc = None
        for half in range(2):
            s = _dot_nt(qs, kz[j][half])
            s = jnp.where(valid2, s, NEG_INF)
            sink = jnp.where(upper, sink_ref[l, 4 * j + half], sink_ref[l, 4 * j + 2 + half])
            mx = jnp.maximum(jnp.max(s, axis=-1, keepdims=True), sink)
            p = jnp.exp(s - mx)
            den = jnp.sum(p, axis=-1, keepdims=True) + jnp.exp(sink - mx)
            p = (p * (1.0 / den)).astype(BF16)
            pv = _dot(p, vz[j][half])
            acc = pv if acc is None else acc + pv
        outs.append(acc[0:blk])
        outs.append(acc[blk:2 * blk])
    o = jnp.concatenate(outs, axis=-1)
    ms = jnp.mean(o * o, axis=-1, keepdims=True)
    o_ref[...] = (o * lax.rsqrt(ms + EPS) * gain_ref[...]).astype(BF16)


def _attention(qa, ka2, va2, attn_sink, attn_norm3, l, batch, seq):
    t = qa.shape[0]
    blk = ATT_BLOCK
    nb = seq // blk
    cur = lambda b, n: (b * nb + n, 0)
    prev = lambda b, n: (b * nb + jnp.maximum(n - 1, 0), 0)
    nxt = lambda b, n: (b * nb + jnp.minimum(n + 1, nb - 1), 0)
    kvspec = lambda f: pl.BlockSpec((blk, 2 * KV_WIDTH), f)
    return pl.pallas_call(
        functools.partial(_attn_kernel, l=l, seq=seq),
        out_shape=jax.ShapeDtypeStruct((t, ATT_WIDTH), BF16),
        grid=(batch, nb),
        in_specs=[
            pl.BlockSpec(memory_space=pltpu.SMEM),
            pl.BlockSpec((blk, ATT_WIDTH), cur),
            kvspec(prev), kvspec(cur), kvspec(nxt),
            kvspec(prev), kvspec(cur), kvspec(nxt),
            pl.BlockSpec((None, 1, ATT_WIDTH), lambda b, n: (l, 0, 0)),
        ],
        out_specs=pl.BlockSpec((blk, ATT_WIDTH), cur),
        compiler_params=_params(("arbitrary", "arbitrary")),
        name="window_attn",
    )(attn_sink, qa, ka2, ka2, ka2, va2, va2, va2, attn_norm3)


def _hgrn_constants():
    c, nl = HG_CHUNK, HG_LEVELS
    r = np.arange(c)
    masks = []
    for lev in range(nl):
        m = 1 << lev
        parent = r // (2 * m)
        upper = r >= parent * 2 * m + m
        masks.append((parent[:, None] == parent[None, :]) & upper[:, None] & (~upper)[None, :])
    masks.append(np.eye(c, dtype=bool))
    kf = np.stack(masks).astype(np.float32)
    kb = np.stack([mk[::-1, ::-1] for mk in masks]).astype(np.float32)
    return kf, kb


def _chunk_decays(logf, reverse):
    c = logf.shape[0]
    sub = 8
    nv = c // sub
    row = lax.broadcasted_iota(I32, (sub, LANES), 0)
    parts = []
    for v in range(nv):
        x = logf[sub * v:sub * (v + 1), :]
        for s in (1, 2, 4):
            if reverse:
                x = x + jnp.where(row < sub - s, pltpu.roll(x, sub - s, 0), 0.0)
            else:
                x = x + jnp.where(row >= s, pltpu.roll(x, s, 0), 0.0)
        parts.append(x)
    b = [None] * nv
    order = list(reversed(range(nv))) if reverse else list(range(nv))
    edge = 0 if reverse else sub - 1
    carry = None
    for v in order:
        b[v] = parts[v] if carry is None else parts[v] + carry
        carry = b[v][edge:edge + 1, :]
    b_last = carry

    def anchor_row(v, r):
        return jnp.broadcast_to(b[v][r:r + 1, :], (sub, LANES))

    odd = (row & 1) == 1
    levels = []
    for lev in range(HG_LEVELS):
        m = 1 << lev
        pieces = []
        for v in range(nv):
            if m == 1:
                a = jnp.where(odd, pltpu.roll(b[v], 1, 0), b[v]) if reverse else jnp.where(odd, b[v], pltpu.roll(b[v], sub - 1, 0))
            elif m == 2:
                lo, hi = (1, 5) if reverse else (2, 6)
                a = jnp.where(row < 4, anchor_row(v, lo), anchor_row(v, hi))
            elif m == 4:
                a = anchor_row(v, 3 if reverse else 4)
            else:
                mv = m // sub
                first = (v // (2 * mv)) * 2 * mv
                a = anchor_row(first + mv - 1, sub - 1) if reverse else anchor_row(first + mv, 0)
            pieces.append(jnp.exp(-jnp.abs(b[v] - a)))
        levels.append(jnp.concatenate(pieces, axis=0))
    eb = jnp.concatenate([jnp.exp(bv) for bv in b], axis=0)
    erem = jnp.concatenate([jnp.exp(b_last - bv) for bv in b], axis=0)
    return levels, eb, erem


def _hgrn_kernel(q_ref, ff_ref, fb_ref, i_ref, g_ref, lb_ref, gn_ref, kf_ref, kb_ref,
                 o_ref, of_scr, ob_scr, st_scr, *, seq):
    c, nl = HG_CHUNK, HG_LEVELS
    nc = seq // c
    lb = lb_ref[...]
    st_scr[...] = jnp.zeros(st_scr.shape, F32)

    unroll = HG_UNROLL if nc % HG_UNROLL == 0 else 1
    dirs = (
        dict(f_ref=ff_ref, lbrow=lb[0:1, :], k_ref=kf_ref, o_scr=of_scr, last_row=c - 1, d=0),
        dict(f_ref=fb_ref, lbrow=lb[1:2, :], k_ref=kb_ref, o_scr=ob_scr, last_row=0, d=1),
    )

    def body(i, carry):
        work = []
        for u in range(unroll):
            cf = i * unroll + u
            work.append((dirs[0], pl.ds(pl.multiple_of(cf * c, c), c)))
            work.append((dirs[1], pl.ds(pl.multiple_of((nc - 1 - cf) * c, c), c)))
        gates = []
        for dr, rows in work:
            lbf = jnp.maximum(dr["lbrow"], LB_FLOOR)
            oml = 1.0 - dr["lbrow"]
            f = dr["f_ref"][rows, :]
            e = jnp.exp(-jnp.abs(f))
            r = 1.0 / (1.0 + e)
            er = e * r
            pos = f >= 0.0
            logf = jnp.log(lbf + oml * jnp.where(pos, r, er))
            kk = oml * jnp.where(pos, er, r)
            qh = q_ref[rows, :]
            gates.append((logf, kk, qh * _sigmoid(qh), i_ref[rows, :].astype(BF16)))
        decays = [_chunk_decays(g[0], dr["d"] == 1) for (dr, _), g in zip(work, gates)]
        amat = [dr["k_ref"][nl] * _dot_nt(g[2].astype(BF16), g[1].astype(BF16)) for (dr, _), g in zip(work, gates)]
        for lev in range(nl):
            for j, ((dr, _), g) in enumerate(zip(work, gates)):
                gl = decays[j][0][lev]
                amat[j] = amat[j] + dr["k_ref"][lev] * _dot_nt((g[2] * gl).astype(BF16), (g[1] * gl).astype(BF16))
        intra = [_dot(amat[j].astype(BF16), g[3]) for j, g in enumerate(gates)]
        upd = [_dot_tn(g[3], (g[1] * decays[j][2]).astype(BF16)) for j, g in enumerate(gates)]
        st = [st_scr[0], st_scr[1]]
        for j, ((dr, rows), g) in enumerate(zip(work, gates)):
            eb = decays[j][1]
            d = dr["d"]
            dr["o_scr"][rows, :] = _dot_nt((g[2] * eb).astype(BF16), st[d].astype(BF16)) + intra[j]
            st[d] = st[d] * eb[dr["last_row"]:dr["last_row"] + 1, :] + upd[j]
        st_scr[0] = st[0]
        st_scr[1] = st[1]
        return carry

    lax.fori_loop(0, nc // unroll, body, 0)

    ep = min(256, seq)
    gn = gn_ref[...]

    def epilogue(j, carry):
        rows = pl.ds(pl.multiple_of(j * ep, ep), ep)
        o = of_scr[rows, :] + ob_scr[rows, :]
        y = o * lax.rsqrt(jnp.mean(o * o, axis=-1, keepdims=True) + EPS) * gn
        g = g_ref[rows, :]
        o_ref[rows, :] = (y * (g * _sigmoid(g))).astype(BF16)
        return carry

    lax.fori_loop(0, seq // ep, epilogue, 0)


def _hgrn(hg, lower, hg_norm3, consts, l, batch, seq):
    t = hg.shape[0]
    kf, kb = consts
    hd = HG_HEAD_DIM

    def col(g):
        return pl.BlockSpec((seq, hd), lambda b, h: (b, g * HG_HEADS + h))

    full3 = lambda a: pl.BlockSpec(a.shape, lambda b, h: (0, 0, 0))
    return pl.pallas_call(
        functools.partial(_hgrn_kernel, seq=seq),
        out_shape=jax.ShapeDtypeStruct((t, HG_WIDTH), BF16),
        grid=(batch, HG_HEADS),
        in_specs=[
            col(0), col(1), col(2), col(3), col(4),
            pl.BlockSpec((None, 2, hd), lambda b, h: (l, 0, h)),
            pl.BlockSpec((None, 1, hd), lambda b, h: (l, 0, 0)),
            full3(kf), full3(kb),
        ],
        out_specs=pl.BlockSpec((seq, hd), lambda b, h: (b, h)),
        scratch_shapes=[
            pltpu.VMEM((seq, hd), F32),
            pltpu.VMEM((seq, hd), F32),
            pltpu.VMEM((2, hd, hd), F32),
        ],
        compiler_params=_params(("arbitrary", "arbitrary")),
        name="hgrn2_scan",
    )(hg, hg, hg, hg, hg, lower, hg_norm3, kf, kb)


def _outproj_kernel(a_ref, o_ref, x_ref, mod_ref, n2_ref, w_ref, wr_ref, br_ref,
                    xo_ref, h2_ref, ti_ref, gt_ref, *, d):
    mod = mod_ref[...]
    y = _dot(a_ref[...], w_ref[0:ATT_WIDTH, :]) + _dot(o_ref[...], w_ref[ATT_WIDTH:ATT_WIDTH + HG_WIDTH, :])
    xn = x_ref[...] + mod[:, 2 * d:3 * d] * y
    xo_ref[...] = xn
    h2 = _rms_mod(xn, n2_ref[...], mod[:, 4 * d:5 * d], mod[:, 3 * d:4 * d])
    h2_ref[...] = _pack_bf16_pairs(h2)
    lg = _dot_f32_nt(wr_ref[...], h2) + br_ref[...]
    eidx = lax.broadcasted_iota(I32, lg.shape, 0)
    vals, idxs = [], []
    for _ in range(TOP_K):
        mx = jnp.max(lg, axis=0, keepdims=True)
        sel = jnp.min(jnp.where(lg == mx, eidx, N_EXPERTS), axis=0, keepdims=True)
        vals.append(mx)
        idxs.append(sel)
        lg = jnp.where(eidx == sel, -jnp.inf, lg)
    ex = [jnp.exp(v - vals[0]) for v in vals]
    den = ex[0]
    for e in ex[1:]:
        den = den + e
    inv = 1.0 / den
    ti_ref[...] = jnp.concatenate(idxs, axis=0)
    gt_ref[...] = jnp.concatenate([e * inv for e in ex], axis=0)


def _outproj(a, o, x2, mod3, norm2, w_out_bf, w_router_t, b_router3, l, seq):
    t, d = x2.shape
    tm = min(PROJ_TILE, seq)
    nb = mod3.shape[0] // norm2.shape[0]
    tok = lambda i: (i, 0)
    lane_tok = lambda i: (0, i)
    return pl.pallas_call(
        functools.partial(_outproj_kernel, d=d),
        out_shape=(
            jax.ShapeDtypeStruct((t, d), F32),
            jax.ShapeDtypeStruct((t, d // 2), U32),
            jax.ShapeDtypeStruct((TOP_K, t), I32),
            jax.ShapeDtypeStruct((TOP_K, t), F32),
        ),
        grid=(t // tm,),
        in_specs=[
            pl.BlockSpec((tm, ATT_WIDTH), tok),
            pl.BlockSpec((tm, HG_WIDTH), tok),
            pl.BlockSpec((tm, d), tok),
            pl.BlockSpec((None, 1, N_MOD * d), lambda i: (l * nb + (i * tm) // seq, 0, 0)),
            pl.BlockSpec((None, 1, d), lambda i: (l, 0, 0)),
            pl.BlockSpec((None, ATT_WIDTH + HG_WIDTH, d), lambda i: (l, 0, 0)),
            pl.BlockSpec((None, N_EXPERTS, d), lambda i: (l, 0, 0)),
            pl.BlockSpec((None, N_EXPERTS, 1), lambda i: (l, 0, 0)),
        ],
        out_specs=(
            pl.BlockSpec((tm, d), tok),
            pl.BlockSpec((tm, d // 2), tok),
            pl.BlockSpec((TOP_K, tm), lane_tok),
            pl.BlockSpec((TOP_K, tm), lane_tok),
        ),
        compiler_params=_params(("arbitrary",)),
        name="out_proj_router",
    )(a, o, x2, mod3, norm2, w_out_bf, w_router_t, b_router3)


def _rank_kernel(ti_ref, tri_ref, rank_ref, cnt_ref, carry_scr):
    @pl.when(pl.program_id(0) == 0)
    def _():
        carry_scr[...] = jnp.zeros(carry_scr.shape, F32)

    ti = ti_ref[...]
    tl = ti.shape[1]
    eidx = lax.broadcasted_iota(I32, (N_EXPERTS, tl), 0)
    carry = carry_scr[...]
    rows = []
    for k in range(TOP_K):
        oh = eidx == ti[k:k + 1, :]
        ohf = jnp.where(oh, 1.0, 0.0)
        pre = _dot(ohf.astype(BF16), tri_ref[...])
        rows.append(jnp.sum(jnp.where(oh, carry + pre, 0.0), axis=0, keepdims=True))
        carry = carry + jnp.sum(ohf, axis=1, keepdims=True)
    carry_scr[...] = carry
    rank_ref[...] = jnp.concatenate(rows, axis=0).astype(I32)
    cnt_ref[...] = jnp.broadcast_to(carry, cnt_ref.shape)


def _ranks(topi):
    k, t = topi.shape
    tl = min(RANK_TILE, t)
    tri = np.triu(np.ones((tl, tl), np.float32), 1)
    return pl.pallas_call(
        _rank_kernel,
        out_shape=(jax.ShapeDtypeStruct((k, t), I32), jax.ShapeDtypeStruct((N_EXPERTS, LANES), F32)),
        grid=(t // tl,),
        in_specs=[pl.BlockSpec((k, tl), lambda i: (0, i)), pl.BlockSpec((tl, tl), lambda i: (0, 0))],
        out_specs=(pl.BlockSpec((k, tl), lambda i: (0, i)), pl.BlockSpec((N_EXPERTS, LANES), lambda i: (0, 0))),
        scratch_shapes=[pltpu.VMEM((N_EXPERTS, 1), F32)],
        compiler_params=_params(("arbitrary",)),
        name="route_rank",
    )(topi, jnp.asarray(tri, BF16))


def _dest_kernel(cnt_ref, ltri_ref, ti_ref, rank_ref, dest_ref, eblk_ref):
    cnt = cnt_ref[...]
    nblk = jnp.floor((cnt + (MOE_ROWS - 1)) * (1.0 / MOE_ROWS))
    pstart_b = _dot(ltri_ref[...], nblk.astype(BF16))
    pstart = (pstart_b[:, 0:1] * MOE_ROWS).astype(I32)
    ti = ti_ref[...]
    tl = ti.shape[1]
    eidx = lax.broadcasted_iota(I32, (N_EXPERTS, tl), 0)
    rows = []
    for k in range(TOP_K):
        oh = eidx == ti[k:k + 1, :]
        rows.append(jnp.sum(jnp.where(oh, pstart, 0), axis=0, keepdims=True))
    dest_ref[...] = jnp.concatenate(rows, axis=0) + rank_ref[...]
    eblk_ref[0] = pstart_b.astype(I32)
    eblk_ref[1] = nblk.astype(I32)


def _destinations(counts, topi, rank):
    k, t = topi.shape
    tl = min(2048, t)
    ltri = np.tril(np.ones((N_EXPERTS, N_EXPERTS), np.float32), -1)
    return pl.pallas_call(
        _dest_kernel,
        out_shape=(
            jax.ShapeDtypeStruct((k, t), I32),
            jax.ShapeDtypeStruct((2, N_EXPERTS, LANES), I32),
        ),
        grid=(t // tl,),
        in_specs=[
            pl.BlockSpec((N_EXPERTS, LANES), lambda i: (0, 0)),
            pl.BlockSpec((N_EXPERTS, N_EXPERTS), lambda i: (0, 0)),
            pl.BlockSpec((k, tl), lambda i: (0, i)),
            pl.BlockSpec((k, tl), lambda i: (0, i)),
        ],
        out_specs=(
            pl.BlockSpec((k, tl), lambda i: (0, i)),
            pl.BlockSpec((2, N_EXPERTS, LANES), lambda i: (0, 0, 0)),
        ),
        compiler_params=_params(("arbitrary",)),
        name="route_dest",
    )(counts, jnp.asarray(ltri, BF16), topi, rank)


def _pack_bf16_pairs(x):
    n = x.shape[1] // 2
    lo = lax.bitcast_convert_type(x[:, :n].astype(BF16).astype(F32), U32)
    hi = lax.bitcast_convert_type(x[:, n:].astype(BF16).astype(F32), U32)
    return hi | (lo >> 16)


def _unpack_bf16_pairs(w):
    lo = lax.bitcast_convert_type(w << 16, F32)
    hi = lax.bitcast_convert_type(w & jnp.uint32(0xFFFF0000), F32)
    return lo, hi


SC_WINDOW = 128


def _sc_mesh():
    return plsc.VectorSubcoreMesh(core_axis_name="core", subcore_axis_name="subcore")


def _sc_worker(mesh):
    return lax.axis_index("core") * mesh.num_subcores + lax.axis_index("subcore"), mesh.num_cores * mesh.num_subcores


def _sc_scatter_rows(src, dest_rows, n_out):
    n, width = src.shape
    nk = dest_rows.shape[0]
    mesh = _sc_mesh()
    half = SC_WINDOW // 2

    @pl.kernel(out_type=jax.ShapeDtypeStruct((n_out, width), src.dtype), mesh=mesh,
               scratch_types=[pltpu.VMEM((nk, SC_WINDOW), I32), pltpu.VMEM((2, half, width), src.dtype),
                              pltpu.SemaphoreType.DMA((2,))],
               name="sc_scatter_rows")
    def scatter_kernel(src_hbm, idx_hbm, out_hbm, idx_v, rows_v, sem):
        wid, nw = _sc_worker(mesh)
        per = n // SC_WINDOW // nw
        row0 = wid * per * SC_WINDOW

        def load(s, buf):
            return pltpu.make_async_copy(src_hbm.at[pl.ds(row0 + s * half, half)], rows_v.at[buf], sem.at[buf])

        load(0, 0).start()

        @pl.loop(0, per)
        def _(j):
            pltpu.sync_copy(idx_hbm.at[:, pl.ds(row0 + j * SC_WINDOW, SC_WINDOW)], idx_v)
            for h in range(2):
                if h == 0:
                    load(2 * j + 1, 1).start()
                else:
                    @pl.when(j + 1 < per)
                    def _():
                        load(2 * j + 2, 0).start()

                load(2 * j + h, h).wait()
                for k in range(nk):
                    pltpu.sync_copy(rows_v.at[h], out_hbm.at[idx_v.at[k, pl.ds(h * half, half)]])

    return scatter_kernel(src, dest_rows)


def _sc_gather_rows(src, rows):
    n = rows.shape[0]
    width = src.shape[1]
    mesh = _sc_mesh()
    half = SC_WINDOW // 2

    @pl.kernel(out_type=jax.ShapeDtypeStruct((n, width), src.dtype), mesh=mesh,
               scratch_types=[pltpu.VMEM((1, SC_WINDOW), I32), pltpu.VMEM((2, half, width), src.dtype),
                              pltpu.SemaphoreType.DMA((2,))],
               name="sc_gather_rows")
    def gather_kernel(src_hbm, idx_hbm, out_hbm, idx_v, rows_v, sem):
        wid, nw = _sc_worker(mesh)
        per = n // SC_WINDOW // nw
        row0 = wid * per * SC_WINDOW

        def store(s, buf):
            return pltpu.make_async_copy(rows_v.at[buf], out_hbm.at[pl.ds(row0 + s * half, half)], sem.at[buf])

        @pl.loop(0, per)
        def _(j):
            pltpu.sync_copy(idx_hbm.at[:, pl.ds(row0 + j * SC_WINDOW, SC_WINDOW)], idx_v)
            for h in range(2):
                @pl.when(j >= 1)
                def _():
                    store(2 * j + h - 2, h).wait()

                pltpu.sync_copy(src_hbm.at[idx_v.at[0, pl.ds(h * half, half)]], rows_v.at[h])
                store(2 * j + h, h).start()

        for h in range(2):
            store(2 * per - 2 + h, h).wait()

    return gather_kernel(src, rows.reshape(1, n))


def _expert_group_kernel(eb_ref, xs_hbm, wgu_hbm, bgu_ref, wd_hbm, bd_ref, ys_hbm,
                         xbuf, ybuf, wgu_f, wd_f, wgu_bf, wd_bf, act_scr, xsem, ysem, wsem, *, dff, l):
    e = pl.program_id(0)
    ne = pl.num_programs(0)
    b0 = eb_ref[e]
    nb = eb_ref[ne + e]
    rows, half = xbuf.shape[1], xbuf.shape[2]
    group = rows // MOE_ROWS
    nch = (nb + group - 1) // group
    wslot = e % 2

    def w_copies(ex, slot):
        cps = []
        for src, dst, first in ((wgu_hbm, wgu_f, 0), (wd_hbm, wd_f, WEIGHT_CHUNKS)):
            step = dst.shape[1] // WEIGHT_CHUNKS
            for c in range(WEIGHT_CHUNKS):
                cps.append(pltpu.make_async_copy(src.at[l, ex, pl.ds(c * step, step), :],
                                                 dst.at[slot, pl.ds(c * step, step), :], wsem.at[slot, first + c]))
        return cps

    @pl.when(e == 0)
    def _():
        for c, cp in enumerate(w_copies(0, 0)):
            cp.start(priority=c % 2)

    @pl.when(e + 1 < ne)
    def _():
        for c, cp in enumerate(w_copies(e + 1, 1 - wslot)):
            cp.start(priority=c % 2)

    for cp in w_copies(e, wslot):
        cp.wait()

    def x_copy(j, slot):
        start = (b0 + j * group) * MOE_ROWS
        return pltpu.make_async_copy(xs_hbm.at[pl.ds(start, rows), :], xbuf.at[slot], xsem.at[slot])

    def y_copy(j, slot, g):
        start = (b0 + j * group + g) * MOE_ROWS
        return pltpu.make_async_copy(ybuf.at[slot, pl.ds(g * MOE_ROWS, MOE_ROWS), :],
                                     ys_hbm.at[pl.ds(start, MOE_ROWS), :], ysem.at[slot, g])

    def y_each(j, slot, fn):
        for g in range(group):
            @pl.when(j * group + g < nb)
            def _():
                fn(y_copy(j, slot, g))

    @pl.when(nb > 0)
    def _():
        x_copy(0, 0).start(priority=1)
        wgu_bf[...] = wgu_f[wslot].astype(BF16)
        wd_bf[...] = wd_f[wslot].astype(BF16)

        def chunk(j, carry):
            slot = j % 2

            @pl.when(j + 1 < nch)
            def _():
                x_copy(j + 1, 1 - slot).start(priority=1)

            x_copy(j, slot).wait()

            @pl.when(j >= 2)
            def _():
                y_each(j - 2, slot, lambda cp: cp.wait())

            lo, hi = _unpack_bf16_pairs(xbuf[slot])
            xb = jnp.concatenate([lo.astype(BF16), hi.astype(BF16)], axis=1)
            for c0 in range(0, dff, EXPERT_COLS):
                gate = _dot(xb, wgu_bf[:, c0:c0 + EXPERT_COLS]) + bgu_ref[:, c0:c0 + EXPERT_COLS]
                up = (_dot(xb, wgu_bf[:, dff + c0:dff + c0 + EXPERT_COLS])
                      + bgu_ref[:, dff + c0:dff + c0 + EXPERT_COLS])
                glu = jnp.minimum(gate, SWIGLU_LIMIT)
                lin = jnp.clip(up, -SWIGLU_LIMIT, SWIGLU_LIMIT)
                act_scr[:, c0:c0 + EXPERT_COLS] = (glu * _sigmoid(SWIGLU_ALPHA * glu) * (lin + 1.0)).astype(BF16)
            ybuf[slot] = _pack_bf16_pairs(_dot(act_scr[...], wd_bf[...]) + bd_ref[...])
            y_each(j, slot, lambda cp: cp.start(priority=1))
            return carry

        lax.fori_loop(0, nch, chunk, 0)

        @pl.when(nch >= 2)
        def _():
            y_each(nch - 2, nch % 2, lambda cp: cp.wait())

        y_each(nch - 1, (nch - 1) % 2, lambda cp: cp.wait())


def _block_experts(eblk, xs, w_gu, b_gu4, w_down, b_down4, l):
    cap, half = xs.shape
    d = 2 * half
    dff = w_down.shape[2]
    wsel = lambda e, eb: (l, e, 0, 0)
    grid_spec = pltpu.PrefetchScalarGridSpec(
        num_scalar_prefetch=1,
        grid=(N_EXPERTS,),
        in_specs=[
            pl.BlockSpec(memory_space=pl.ANY),
            pl.BlockSpec(memory_space=pl.ANY),
            pl.BlockSpec((None, None, 1, 2 * dff), wsel),
            pl.BlockSpec(memory_space=pl.ANY),
            pl.BlockSpec((None, None, 1, d), wsel),
        ],
        out_specs=pl.BlockSpec(memory_space=pl.ANY),
        scratch_shapes=[
            pltpu.VMEM((2, EXPERT_GROUP * MOE_ROWS, half), U32),
            pltpu.VMEM((2, EXPERT_GROUP * MOE_ROWS, half), U32),
            pltpu.VMEM((2, d, 2 * dff), F32),
            pltpu.VMEM((2, dff, d), F32),
            pltpu.VMEM((d, 2 * dff), BF16),
            pltpu.VMEM((dff, d), BF16),
            pltpu.VMEM((EXPERT_GROUP * MOE_ROWS, dff), BF16),
            pltpu.SemaphoreType.DMA((2,)),
            pltpu.SemaphoreType.DMA((2, EXPERT_GROUP)),
            pltpu.SemaphoreType.DMA((2, 2 * WEIGHT_CHUNKS)),
        ],
    )
    return pl.pallas_call(
        functools.partial(_expert_group_kernel, dff=dff, l=l),
        out_shape=jax.ShapeDtypeStruct((cap - (EXPERT_GROUP - 1) * MOE_ROWS, half), U32),
        grid_spec=grid_spec,
        compiler_params=_params(("arbitrary",)),
        name="moe_expert_groups",
    )(eblk, xs, w_gu, b_gu4, w_down, b_down4)


def _combine_kernel(*refs, d, final):
    y_refs = refs[:TOP_K]
    gt_ref, x_ref, mod_ref = refs[TOP_K:TOP_K + 3]
    xo_ref = refs[-1]
    half = d // 2
    gt = gt_ref[...]
    acc_lo = acc_hi = None
    for k in range(TOP_K):
        lo, hi = _unpack_bf16_pairs(y_refs[k][...])
        g = gt[:, k:k + 1]
        acc_lo = g * lo if acc_lo is None else acc_lo + g * lo
        acc_hi = g * hi if acc_hi is None else acc_hi + g * hi
    x_lo = x_ref[:, 0:half] + mod_ref[:, 5 * d:5 * d + half] * acc_lo
    x_hi = x_ref[:, half:d] + mod_ref[:, 5 * d + half:6 * d] * acc_hi
    if final:
        gain = refs[TOP_K + 3][...]
        ssq = jnp.sum(x_lo * x_lo, axis=-1, keepdims=True) + jnp.sum(x_hi * x_hi, axis=-1, keepdims=True)
        inv = lax.rsqrt(ssq * (1.0 / d) + EPS)
        x_lo = x_lo * inv * gain[:, 0:half]
        x_hi = x_hi * inv * gain[:, half:d]
    xo_ref[:, 0:half] = x_lo
    xo_ref[:, half:d] = x_hi


def _combine(y4p, gates_t, x2, mod3, l, seq, depth, row0, prev, final_gain=None):
    t, d = x2.shape
    n = gates_t.shape[0]
    tm = min(PROJ_TILE, seq)
    nb = mod3.shape[0] // depth
    nt = n // tm
    t0 = row0 // tm
    final = final_gain is not None

    def slot(k):
        return pl.BlockSpec((tm, d // 2), lambda i: (k * nt + i, 0))

    specs = [slot(k) for k in range(TOP_K)] + [
        pl.BlockSpec((tm, TOP_K), lambda i: (i, 0)),
        pl.BlockSpec((tm, d), lambda i: (t0 + i, 0)),
        pl.BlockSpec((None, 1, N_MOD * d), lambda i: (l * nb + (row0 + i * tm) // seq, 0, 0)),
    ]
    args = [y4p] * TOP_K + [gates_t, x2, mod3]
    if final:
        specs.append(pl.BlockSpec((1, d), lambda i: (0, 0)))
        args.append(final_gain.reshape(1, d))
    aliases = {}
    if prev is not None:
        specs.append(pl.BlockSpec(memory_space=pl.ANY))
        aliases = {len(args): 0}
        args.append(prev)
    return pl.pallas_call(
        functools.partial(_combine_kernel, d=d, final=final),
        out_shape=jax.ShapeDtypeStruct((t, d), F32),
        grid=(nt,),
        in_specs=specs,
        out_specs=pl.BlockSpec((tm, d), lambda i: (t0 + i, 0)),
        input_output_aliases=aliases,
        compiler_params=_params(("arbitrary",)),
        name="moe_combine",
    )(*args)


def _mixer_layer(x2, mod3, tables, lower, consts, p, l, batch, seq):
    ra, rb, rc = tables
    qa, ka2, va2, hg = _inproj(x2, mod3, p["norm1"], p["w_in"], ra, rb, rc, l, seq)
    a = _attention(qa, ka2, va2, p["attn_sink"], p["attn_norm"], l, batch, seq)
    o = _hgrn(hg, lower, p["hg_norm"], consts, l, batch, seq)
    return a, o


def _moe_layer(a, o, x2, mod3, p, l, seq, depth, final_gain=None):
    t, d = x2.shape
    xn, h2p, topi, gates = _outproj(a, o, x2, mod3, p["norm2"], p["w_out"], p["w_router_t"], p["b_router"], l, seq)
    rank, counts = _ranks(topi)
    nblocks = (t * TOP_K + N_EXPERTS * MOE_ROWS) // MOE_ROWS
    dest, eblk = _destinations(counts, topi, rank)
    xs = _sc_scatter_rows(h2p, dest, (nblocks + EXPERT_GROUP - 1) * MOE_ROWS)
    ys = _block_experts(eblk[:, :, 0].reshape(-1), xs, p["w_gu"], p["b_gu"], p["w_down"], p["b_down"], l)
    gates_t = gates.T
    out = None
    for part in range(MOE_COMBINE_PARTS):
        rows = slice(part * (t // MOE_COMBINE_PARTS), (part + 1) * (t // MOE_COMBINE_PARTS))
        y4p = _sc_gather_rows(ys, dest[:, rows].reshape(-1))
        out = _combine(y4p, gates_t[rows], xn, mod3, l, seq, depth, rows.start, out, final_gain)
    return out


def kernel(x, c, positions, w_ada, b_ada, norm1, w_in, attn_sink, attn_norm, hg_lb_logits, hg_norm, w_out, norm2,
           w_router, b_router, w_gu, b_gu, w_down, b_down, final_norm):
    batch, seq, d = x.shape
    depth = w_ada.shape[0]
    t = batch * seq
    p = {
        "norm1": norm1.reshape(depth, 1, d),
        "w_in": w_in.astype(BF16),
        "attn_sink": attn_sink.astype(F32),
        "attn_norm": attn_norm.reshape(depth, 1, ATT_WIDTH),
        "hg_norm": hg_norm.reshape(depth, 1, HG_HEAD_DIM),
        "w_out": w_out.astype(BF16),
        "norm2": norm2.reshape(depth, 1, d),
        "w_router_t": jnp.swapaxes(w_router, 1, 2),
        "b_router": b_router.reshape(depth, N_EXPERTS, 1),
        "w_gu": w_gu,
        "b_gu": b_gu.reshape(depth, N_EXPERTS, 1, b_gu.shape[-1]),
        "w_down": w_down,
        "b_down": b_down.reshape(depth, N_EXPERTS, 1, d),
    }
    mod3 = _ada_all(c, w_ada, b_ada).reshape(depth * batch, 1, N_MOD * d)
    lower = _lower_bounds(hg_lb_logits)
    tables = _rope_tables(positions)
    consts = tuple(jnp.asarray(m) for m in _hgrn_constants())
    x2 = x.reshape(t, d)
    for l in range(depth):
        a, o = _mixer_layer(x2, mod3, tables, lower, consts, p, l, batch, seq)
        x2 = _moe_layer(a, o, x2, mod3, p, l, seq, depth, final_norm if l == depth - 1 else None)
    return x2.reshape(batch, seq, d)
```

```python
import functools

import numpy as np
import jax
import jax.numpy as jnp
from jax import lax
from jax.experimental import pallas as pl
from jax.experimental.pallas import tpu as pltpu
from jax.experimental.pallas import tpu_sc as plsc

F32 = jnp.float32
BF16 = jnp.bfloat16
I32 = jnp.int32
U32 = jnp.uint32

ATT_HEADS = 8
ATT_KV_HEADS = 2
ATT_HEAD_DIM = 64
ATT_WIDTH = ATT_HEADS * ATT_HEAD_DIM
KV_WIDTH = ATT_KV_HEADS * ATT_HEAD_DIM
WINDOW = 128
ATT_BLOCK = 128
ROPE_THETA = 500000.0
ROPE_DIM = ATT_HEAD_DIM // 4
HG_HEADS = 4
HG_HEAD_DIM = 128
HG_WIDTH = HG_HEADS * HG_HEAD_DIM
N_EXPERTS = 32
TOP_K = 4
SWIGLU_ALPHA = 1.702
SWIGLU_LIMIT = 7.0
N_MOD = 6
EPS = 1e-6
NEG_INF = -1e30
LB_FLOOR = 1e-30

LANES = 128
HG_CHUNK = 64
HG_LEVELS = 6
HG_UNROLL = 8
MOE_ROWS = 256
EXPERT_GROUP = 1
WEIGHT_CHUNKS = 4
EXPERT_COLS = 512
MOE_COMBINE_PARTS = 4
PROJ_TILE = 512
RANK_TILE = 1024
VMEM_LIMIT = 56 * 1024 * 1024


def _dot(a, b):
    return jnp.dot(a, b, preferred_element_type=F32)


def _dot_nt(a, b):
    return lax.dot_general(a, b, (((1,), (1,)), ((), ())), preferred_element_type=F32)


def _dot_tn(a, b):
    return lax.dot_general(a, b, (((0,), (0,)), ((), ())), preferred_element_type=F32)


def _split3(x):
    hi = x.astype(BF16)
    r1 = x - hi.astype(F32)
    mid = r1.astype(BF16)
    lo = (r1 - mid.astype(F32)).astype(BF16)
    return hi, mid, lo


def _dot_f32_nt(a, b):
    ah, am, _ = _split3(a)
    bh, bm, _ = _split3(b)
    return _dot_nt(ah, bh) + _dot_nt(ah, bm) + _dot_nt(am, bh)


def _dot_f32(a, b):
    ah, am, _ = _split3(a)
    bh, bm, _ = _split3(b)
    return _dot(ah, bh) + _dot(ah, bm) + _dot(am, bh)


def _sigmoid(x):
    return 1.0 / (1.0 + jnp.exp(-x))


def _params(sem=None):
    return pltpu.CompilerParams(dimension_semantics=sem, vmem_limit_bytes=VMEM_LIMIT)


def _ada_kernel(c_ref, w_ref, b_ref, o_ref):
    c = c_ref[...]
    cond = c * _sigmoid(c)
    o_ref[...] = _dot_f32(cond, w_ref[...]) + b_ref[...]


def _ada_all(c, w_ada, b_ada):
    depth, d, n = w_ada.shape
    b = c.shape[0]
    nt = n // d
    return pl.pallas_call(
        _ada_kernel,
        out_shape=jax.ShapeDtypeStruct((depth, b, n), F32),
        grid=(depth, nt),
        in_specs=[
            pl.BlockSpec((b, d), lambda l, j: (0, 0)),
            pl.BlockSpec((None, d, d), lambda l, j: (l, 0, j)),
            pl.BlockSpec((None, 1, d), lambda l, j: (l, 0, j)),
        ],
        out_specs=pl.BlockSpec((None, b, d), lambda l, j: (l, 0, j)),
        compiler_params=_params(("arbitrary", "arbitrary")),
        name="ada_mod",
    )(c, w_ada, b_ada.reshape(depth, 1, n))


def _lb_kernel(x_ref, o_ref):
    depth = x_ref.shape[0]
    xs = [x_ref[l] for l in range(depth)]
    m = xs[0]
    for l in range(1, depth):
        m = jnp.maximum(m, xs[l])
    es = [jnp.exp(v - m) for v in xs]
    den = es[0]
    for l in range(1, depth):
        den = den + es[l]
    ps = [e / den for e in es]
    run = ps[0]
    o_ref[0] = run - ps[0]
    for l in range(1, depth):
        run = run + ps[l]
        o_ref[l] = run - ps[0]


def _lower_bounds(hg_lb_logits):
    return pl.pallas_call(
        _lb_kernel,
        out_shape=jax.ShapeDtypeStruct(hg_lb_logits.shape, F32),
        name="hg_lower_bounds",
    )(hg_lb_logits.astype(F32))


def _rope_kernel(pos_ref, invf_ref, a_ref, b_ref, c_ref):
    pos = pos_ref[...].astype(F32)
    ang = pos * invf_ref[...]
    cs = jnp.cos(ang)
    sn = jnp.sin(ang)
    lane = lax.broadcasted_iota(I32, ang.shape, 1) & (ATT_HEAD_DIM - 1)
    half = ROPE_DIM // 2
    first = lane < half
    second = (lane >= half) & (lane < ROPE_DIM)
    a_ref[...] = jnp.where(first | second, cs, 1.0)
    b_ref[...] = jnp.where(first, -sn, 0.0)
    c_ref[...] = jnp.where(second, sn, 0.0)


def _rope_tables(positions):
    t = positions.size
    half = ROPE_DIM // 2
    inv = (np.float32(ROPE_THETA) ** (-(np.arange(half, dtype=np.float32) * np.float32(2.0) / np.float32(ROPE_DIM)))).astype(np.float32)
    lane = np.arange(LANES) % ATT_HEAD_DIM
    pat = np.where(lane < ROPE_DIM, inv[lane % half], 0.0).astype(np.float32).reshape(1, LANES)
    tm = min(t, 2048)
    shp = jax.ShapeDtypeStruct((t, LANES), F32)
    spec = pl.BlockSpec((tm, LANES), lambda i: (i, 0))
    return pl.pallas_call(
        _rope_kernel,
        out_shape=(shp, shp, shp),
        grid=(t // tm,),
        in_specs=[pl.BlockSpec((tm, 1), lambda i: (i, 0)), pl.BlockSpec((1, LANES), lambda i: (0, 0))],
        out_specs=(spec, spec, spec),
        compiler_params=_params(("arbitrary",)),
        name="rope_tables",
    )(positions.reshape(t, 1).astype(I32), jnp.asarray(pat))


def _rms_mod(x, gain, scale, shift):
    ms = jnp.mean(x * x, axis=-1, keepdims=True)
    return (x * lax.rsqrt(ms + EPS) * gain) * (1.0 + scale) + shift


def _rope_apply(x, a, b, c):
    half = ROPE_DIM // 2
    return x * a + pltpu.roll(x, LANES - half, 1) * b + pltpu.roll(x, half, 1) * c


def _inproj_kernel(x_ref, mod_ref, n1_ref, w_ref, ra_ref, rb_ref, rc_ref, qa_ref, ka_ref, va_ref, hg_ref, *, d):
    mod = mod_ref[...]
    h = _rms_mod(x_ref[...], n1_ref[...], mod[:, d:2 * d], mod[:, 0:d]).astype(BF16)
    a, b, c = ra_ref[...], rb_ref[...], rc_ref[...]
    kvw = ATT_WIDTH + 2 * KV_WIDTH
    pa = _dot(h, w_ref[:, 0:kvw])
    scale = ATT_HEAD_DIM ** -0.5
    for g in range(ATT_WIDTH // LANES):
        qg = _rope_apply(pa[:, g * LANES:(g + 1) * LANES], a, b, c)
        qa_ref[:, g * LANES:(g + 1) * LANES] = (qg * scale).astype(BF16)
    k = _rope_apply(pa[:, ATT_WIDTH:ATT_WIDTH + KV_WIDTH], a, b, c)
    v = pa[:, ATT_WIDTH + KV_WIDTH:kvw]
    ka_ref[:, 0:LANES] = k.astype(BF16)
    ka_ref[:, LANES:2 * LANES] = pltpu.roll(k, ATT_HEAD_DIM, 1).astype(BF16)
    va_ref[:, 0:LANES] = v.astype(BF16)
    va_ref[:, LANES:2 * LANES] = pltpu.roll(v, ATT_HEAD_DIM, 1).astype(BF16)
    for g in range(5):
        lo = kvw + g * HG_WIDTH
        hg_ref[:, g * HG_WIDTH:(g + 1) * HG_WIDTH] = _dot(h, w_ref[:, lo:lo + HG_WIDTH])


def _inproj(x2, mod3, norm1, w_in_bf, ra, rb, rc, l, seq):
    t, d = x2.shape
    n_in = w_in_bf.shape[-1]
    tm = min(PROJ_TILE, seq)
    nb = mod3.shape[0] // norm1.shape[0]
    tok = lambda i: (i, 0)
    return pl.pallas_call(
        functools.partial(_inproj_kernel, d=d),
        out_shape=(
            jax.ShapeDtypeStruct((t, ATT_WIDTH), BF16),
            jax.ShapeDtypeStruct((t, 2 * KV_WIDTH), BF16),
            jax.ShapeDtypeStruct((t, 2 * KV_WIDTH), BF16),
            jax.ShapeDtypeStruct((t, 5 * HG_WIDTH), F32),
        ),
        grid=(t // tm,),
        in_specs=[
            pl.BlockSpec((tm, d), tok),
            pl.BlockSpec((None, 1, N_MOD * d), lambda i: (l * nb + (i * tm) // seq, 0, 0)),
            pl.BlockSpec((None, 1, d), lambda i: (l, 0, 0)),
            pl.BlockSpec((None, d, n_in), lambda i: (l, 0, 0)),
            pl.BlockSpec((tm, LANES), tok),
            pl.BlockSpec((tm, LANES), tok),
            pl.BlockSpec((tm, LANES), tok),
        ],
        out_specs=(
            pl.BlockSpec((tm, ATT_WIDTH), tok),
            pl.BlockSpec((tm, 2 * KV_WIDTH), tok),
            pl.BlockSpec((tm, 2 * KV_WIDTH), tok),
            pl.BlockSpec((tm, 5 * HG_WIDTH), tok),
        ),
        compiler_params=_params(("arbitrary",)),
        name="in_proj",
    )(x2, mod3, norm1, w_in_bf, ra, rb, rc)


def _attn_kernel(sink_ref, q_ref, kp_ref, kc_ref, kn_ref, vp_ref, vc_ref, vn_ref, gain_ref, o_ref, *, l, seq):
    n = pl.program_id(1)
    blk = ATT_BLOCK
    k2 = jnp.concatenate([kp_ref[...], kc_ref[...], kn_ref[...]], axis=0)
    v2 = jnp.concatenate([vp_ref[...], vc_ref[...], vn_ref[...]], axis=0)
    lane = lax.broadcasted_iota(I32, (3 * blk, LANES), 1)
    lo_half = lane < ATT_HEAD_DIM
    zero = jnp.zeros((3 * blk, LANES), BF16)
    ka, kb = k2[:, 0:LANES], k2[:, LANES:2 * LANES]
    va, vb = v2[:, 0:LANES], v2[:, LANES:2 * LANES]
    kz = [[jnp.where(lo_half, ka, zero), jnp.where(lo_half, zero, kb)],
          [jnp.where(lo_half, kb, zero), jnp.where(lo_half, zero, ka)]]
    vz = [[jnp.where(lo_half, va, zero), jnp.where(lo_half, zero, vb)],
          [jnp.where(lo_half, vb, zero), jnp.where(lo_half, zero, va)]]
    qpos = n * blk + lax.broadcasted_iota(I32, (blk, 3 * blk), 0)
    kpos = (n - 1) * blk + lax.broadcasted_iota(I32, (blk, 3 * blk), 1)
    valid = (jnp.abs(qpos - kpos) <= WINDOW) & (kpos >= 0) & (kpos < seq)
    valid2 = jnp.concatenate([valid, valid], axis=0)
    upper = lax.broadcasted_iota(I32, (2 * blk, 1), 0) < blk
    outs = []
    for j in range(ATT_KV_HEADS):
        qs = jnp.concatenate([q_ref[:, 2 * j * LANES:(2 * j + 1) * LANES],
                              q_ref[:, (2 * j + 1) * LANES:(2 * j + 2) * LANES]], axis=0)
        acc = None
        for half in range(2):
            s = _dot_nt(qs, kz[j][half])
            s = jnp.where(valid2, s, NEG_INF)
            sink = jnp.where(upper, sink_ref[l, 4 * j + half], sink_ref[l, 4 * j + 2 + half])
            mx = jnp.maximum(jnp.max(s, axis=-1, keepdims=True), sink)
            p = jnp.exp(s - mx)
            den = jnp.sum(p, axis=-1, keepdims=True) + jnp.exp(sink - mx)
            p = (p * (1.0 / den)).astype(BF16)
            pv = _dot(p, vz[j][half])
            acc = pv if acc is None else acc + pv
        outs.append(acc[0:blk])
        outs.append(acc[blk:2 * blk])
    o = jnp.concatenate(outs, axis=-1)
    ms = jnp.mean(o * o, axis=-1, keepdims=True)
    o_ref[...] = (o * lax.rsqrt(ms + EPS) * gain_ref[...]).astype(BF16)


def _attention(qa, ka2, va2, attn_sink, attn_norm3, l, batch, seq):
    t = qa.shape[0]
    blk = ATT_BLOCK
    nb = seq // blk
    cur = lambda b, n: (b * nb + n, 0)
    prev = lambda b, n: (b * nb + jnp.maximum(n - 1, 0), 0)
    nxt = lambda b, n: (b * nb + jnp.minimum(n + 1, nb - 1), 0)
    kvspec = lambda f: pl.BlockSpec((blk, 2 * KV_WIDTH), f)
    return pl.pallas_call(
        functools.partial(_attn_kernel, l=l, seq=seq),
        out_shape=jax.ShapeDtypeStruct((t, ATT_WIDTH), BF16),
        grid=(batch, nb),
        in_specs=[
            pl.BlockSpec(memory_space=pltpu.SMEM),
            pl.BlockSpec((blk, ATT_WIDTH), cur),
            kvspec(prev), kvspec(cur), kvspec(nxt),
            kvspec(prev), kvspec(cur), kvspec(nxt),
            pl.BlockSpec((None, 1, ATT_WIDTH), lambda b, n: (l, 0, 0)),
        ],
        out_specs=pl.BlockSpec((blk, ATT_WIDTH), cur),
        compiler_params=_params(("arbitrary", "arbitrary")),
        name="window_attn",
    )(attn_sink, qa, ka2, ka2, ka2, va2, va2, va2, attn_norm3)


def _hgrn_constants():
    c, nl = HG_CHUNK, HG_LEVELS
    r = np.arange(c)
    masks = []
    for lev in range(nl):
        m = 1 << lev
        parent = r // (2 * m)
        upper = r >= parent * 2 * m + m
        masks.append((parent[:, None] == parent[None, :]) & upper[:, None] & (~upper)[None, :])
    masks.append(np.eye(c, dtype=bool))
    kf = np.stack(masks).astype(np.float32)
    kb = np.stack([mk[::-1, ::-1] for mk in masks]).astype(np.float32)
    return kf, kb


def _chunk_decays(logf, reverse):
    c = logf.shape[0]
    sub = 8
    nv = c // sub
    row = lax.broadcasted_iota(I32, (sub, LANES), 0)
    parts = []
    for v in range(nv):
        x = logf[sub * v:sub * (v + 1), :]
        for s in (1, 2, 4):
            if reverse:
                x = x + jnp.where(row < sub - s, pltpu.roll(x, sub - s, 0), 0.0)
            else:
                x = x + jnp.where(row >= s, pltpu.roll(x, s, 0), 0.0)
        parts.append(x)
    b = [None] * nv
    order = list(reversed(range(nv))) if reverse else list(range(nv))
    edge = 0 if reverse else sub - 1
    carry = None
    for v in order:
        b[v] = parts[v] if carry is None else parts[v] + carry
        carry = b[v][edge:edge + 1, :]
    b_last = carry

    def anchor_row(v, r):
        return jnp.broadcast_to(b[v][r:r + 1, :], (sub, LANES))

    odd = (row & 1) == 1
    levels = []
    for lev in range(HG_LEVELS):
        m = 1 << lev
        pieces = []
        for v in range(nv):
            if m == 1:
                a = jnp.where(odd, pltpu.roll(b[v], 1, 0), b[v]) if reverse else jnp.where(odd, b[v], pltpu.roll(b[v], sub - 1, 0))
            elif m == 2:
                lo, hi = (1, 5) if reverse else (2, 6)
                a = jnp.where(row < 4, anchor_row(v, lo), anchor_row(v, hi))
            elif m == 4:
                a = anchor_row(v, 3 if reverse else 4)
            else:
                mv = m // sub
                first = (v // (2 * mv)) * 2 * mv
                a = anchor_row(first + mv - 1, sub - 1) if reverse else anchor_row(first + mv, 0)
            pieces.append(jnp.exp(-jnp.abs(b[v] - a)))
        levels.append(jnp.concatenate(pieces, axis=0))
    eb = jnp.concatenate([jnp.exp(bv) for bv in b], axis=0)
    erem = jnp.concatenate([jnp.exp(b_last - bv) for bv in b], axis=0)
    return levels, eb, erem


def _hgrn_kernel(q_ref, ff_ref, fb_ref, i_ref, g_ref, lb_ref, gn_ref, kf_ref, kb_ref,
                 o_ref, of_scr, ob_scr, st_scr, *, seq):
    c, nl = HG_CHUNK, HG_LEVELS
    nc = seq // c
    lb = lb_ref[...]
    st_scr[...] = jnp.zeros(st_scr.shape, F32)

    unroll = HG_UNROLL if nc % HG_UNROLL == 0 else 1
    dirs = (
        dict(f_ref=ff_ref, lbrow=lb[0:1, :], k_ref=kf_ref, o_scr=of_scr, last_row=c - 1, d=0),
        dict(f_ref=fb_ref, lbrow=lb[1:2, :], k_ref=kb_ref, o_scr=ob_scr, last_row=0, d=1),
    )

    def body(i, carry):
        work = []
        for u in range(unroll):
            cf = i * unroll + u
            work.append((dirs[0], pl.ds(pl.multiple_of(cf * c, c), c)))
            work.append((dirs[1], pl.ds(pl.multiple_of((nc - 1 - cf) * c, c), c)))
        gates = []
        for dr, rows in work:
            lbf = jnp.maximum(dr["lbrow"], LB_FLOOR)
            oml = 1.0 - dr["lbrow"]
            f = dr["f_ref"][rows, :]
            e = jnp.exp(-jnp.abs(f))
            r = 1.0 / (1.0 + e)
            er = e * r
            pos = f >= 0.0
            logf = jnp.log(lbf + oml * jnp.where(pos, r, er))
            kk = oml * jnp.where(pos, er, r)
            qh = q_ref[rows, :]
            gates.append((logf, kk, qh * _sigmoid(qh), i_ref[rows, :].astype(BF16)))
        decays = [_chunk_decays(g[0], dr["d"] == 1) for (dr, _), g in zip(work, gates)]
        amat = [dr["k_ref"][nl] * _dot_nt(g[2].astype(BF16), g[1].astype(BF16)) for (dr, _), g in zip(work, gates)]
        for lev in range(nl):
            for j, ((dr, _), g) in enumerate(zip(work, gates)):
                gl = decays[j][0][lev]
                amat[j] = amat[j] + dr["k_ref"][lev] * _dot_nt((g[2] * gl).astype(BF16), (g[1] * gl).astype(BF16))
        intra = [_dot(amat[j].astype(BF16), g[3]) for j, g in enumerate(gates)]
        upd = [_dot_tn(g[3], (g[1] * decays[j][2]).astype(BF16)) for j, g in enumerate(gates)]
        st = [st_scr[0], st_scr[1]]
        for j, ((dr, rows), g) in enumerate(zip(work, gates)):
            eb = decays[j][1]
            d = dr["d"]
            dr["o_scr"][rows, :] = _dot_nt((g[2] * eb).astype(BF16), st[d].astype(BF16)) + intra[j]
            st[d] = st[d] * eb[dr["last_row"]:dr["last_row"] + 1, :] + upd[j]
        st_scr[0] = st[0]
        st_scr[1] = st[1]
        return carry

    lax.fori_loop(0, nc // unroll, body, 0)

    ep = min(256, seq)
    gn = gn_ref[...]

    def epilogue(j, carry):
        rows = pl.ds(pl.multiple_of(j * ep, ep), ep)
        o = of_scr[rows, :] + ob_scr[rows, :]
        y = o * lax.rsqrt(jnp.mean(o * o, axis=-1, keepdims=True) + EPS) * gn
        g = g_ref[rows, :]
        o_ref[rows, :] = (y * (g * _sigmoid(g))).astype(BF16)
        return carry

    lax.fori_loop(0, seq // ep, epilogue, 0)


def _hgrn(hg, lower, hg_norm3, consts, l, batch, seq):
    t = hg.shape[0]
    kf, kb = consts
    hd = HG_HEAD_DIM

    def col(g):
        return pl.BlockSpec((seq, hd), lambda b, h: (b, g * HG_HEADS + h))

    full3 = lambda a: pl.BlockSpec(a.shape, lambda b, h: (0, 0, 0))
    return pl.pallas_call(
        functools.partial(_hgrn_kernel, seq=seq),
        out_shape=jax.ShapeDtypeStruct((t, HG_WIDTH), BF16),
        grid=(batch, HG_HEADS),
        in_specs=[
            col(0), col(1), col(2), col(3), col(4),
            pl.BlockSpec((None, 2, hd), lambda b, h: (l, 0, h)),
            pl.BlockSpec((None, 1, hd), lambda b, h: (l, 0, 0)),
            full3(kf), full3(kb),
        ],
        out_specs=pl.BlockSpec((seq, hd), lambda b, h: (b, h)),
        scratch_shapes=[
            pltpu.VMEM((seq, hd), F32),
            pltpu.VMEM((seq, hd), F32),
            pltpu.VMEM((2, hd, hd), F32),
        ],
        compiler_params=_params(("arbitrary", "arbitrary")),
        name="hgrn2_scan",
    )(hg, hg, hg, hg, hg, lower, hg_norm3, kf, kb)


def _outproj_kernel(a_ref, o_ref, x_ref, mod_ref, n2_ref, w_ref, wr_ref, br_ref,
                    xo_ref, h2_ref, ti_ref, gt_ref, *, d):
    mod = mod_ref[...]
    y = _dot(a_ref[...], w_ref[0:ATT_WIDTH, :]) + _dot(o_ref[...], w_ref[ATT_WIDTH:ATT_WIDTH + HG_WIDTH, :])
    xn = x_ref[...] + mod[:, 2 * d:3 * d] * y
    xo_ref[...] = xn
    h2 = _rms_mod(xn, n2_ref[...], mod[:, 4 * d:5 * d], mod[:, 3 * d:4 * d])
    h2_ref[...] = _pack_bf16_pairs(h2)
    lg = _dot_f32_nt(wr_ref[...], h2) + br_ref[...]
    eidx = lax.broadcasted_iota(I32, lg.shape, 0)
    vals, idxs = [], []
    for _ in range(TOP_K):
        mx = jnp.max(lg, axis=0, keepdims=True)
        sel = jnp.min(jnp.where(lg == mx, eidx, N_EXPERTS), axis=0, keepdims=True)
        vals.append(mx)
        idxs.append(sel)
        lg = jnp.where(eidx == sel, -jnp.inf, lg)
    ex = [jnp.exp(v - vals[0]) for v in vals]
    den = ex[0]
    for e in ex[1:]:
        den = den + e
    inv = 1.0 / den
    ti_ref[...] = jnp.concatenate(idxs, axis=0)
    gt_ref[...] = jnp.concatenate([e * inv for e in ex], axis=0)


def _outproj(a, o, x2, mod3, norm2, w_out_bf, w_router_t, b_router3, l, seq):
    t, d = x2.shape
    tm = min(PROJ_TILE, seq)
    nb = mod3.shape[0] // norm2.shape[0]
    tok = lambda i: (i, 0)
    lane_tok = lambda i: (0, i)
    return pl.pallas_call(
        functools.partial(_outproj_kernel, d=d),
        out_shape=(
            jax.ShapeDtypeStruct((t, d), F32),
            jax.ShapeDtypeStruct((t, d // 2), U32),
            jax.ShapeDtypeStruct((TOP_K, t), I32),
            jax.ShapeDtypeStruct((TOP_K, t), F32),
        ),
        grid=(t // tm,),
        in_specs=[
            pl.BlockSpec((tm, ATT_WIDTH), tok),
            pl.BlockSpec((tm, HG_WIDTH), tok),
            pl.BlockSpec((tm, d), tok),
            pl.BlockSpec((None, 1, N_MOD * d), lambda i: (l * nb + (i * tm) // seq, 0, 0)),
            pl.BlockSpec((None, 1, d), lambda i: (l, 0, 0)),
            pl.BlockSpec((None, ATT_WIDTH + HG_WIDTH, d), lambda i: (l, 0, 0)),
            pl.BlockSpec((None, N_EXPERTS, d), lambda i: (l, 0, 0)),
            pl.BlockSpec((None, N_EXPERTS, 1), lambda i: (l, 0, 0)),
        ],
        out_specs=(
            pl.BlockSpec((tm, d), tok),
            pl.BlockSpec((tm, d // 2), tok),
            pl.BlockSpec((TOP_K, tm), lane_tok),
            pl.BlockSpec((TOP_K, tm), lane_tok),
        ),
        compiler_params=_params(("arbitrary",)),
        name="out_proj_router",
    )(a, o, x2, mod3, norm2, w_out_bf, w_router_t, b_router3)


def _rank_kernel(ti_ref, tri_ref, rank_ref, cnt_ref, carry_scr):
    @pl.when(pl.program_id(0) == 0)
    def _():
        carry_scr[...] = jnp.zeros(carry_scr.shape, F32)

    ti = ti_ref[...]
    tl = ti.shape[1]
    eidx = lax.broadcasted_iota(I32, (N_EXPERTS, tl), 0)
    carry = carry_scr[...]
    rows = []
    for k in range(TOP_K):
        oh = eidx == ti[k:k + 1, :]
        ohf = jnp.where(oh, 1.0, 0.0)
        pre = _dot(ohf.astype(BF16), tri_ref[...])
        rows.append(jnp.sum(jnp.where(oh, carry + pre, 0.0), axis=0, keepdims=True))
        carry = carry + jnp.sum(ohf, axis=1, keepdims=True)
    carry_scr[...] = carry
    rank_ref[...] = jnp.concatenate(rows, axis=0).astype(I32)
    cnt_ref[...] = jnp.broadcast_to(carry, cnt_ref.shape)


def _ranks(topi):
    k, t = topi.shape
    tl = min(RANK_TILE, t)
    tri = np.triu(np.ones((tl, tl), np.float32), 1)
    return pl.pallas_call(
        _rank_kernel,
        out_shape=(jax.ShapeDtypeStruct((k, t), I32), jax.ShapeDtypeStruct((N_EXPERTS, LANES), F32)),
        grid=(t // tl,),
        in_specs=[pl.BlockSpec((k, tl), lambda i: (0, i)), pl.BlockSpec((tl, tl), lambda i: (0, 0))],
        out_specs=(pl.BlockSpec((k, tl), lambda i: (0, i)), pl.BlockSpec((N_EXPERTS, LANES), lambda i: (0, 0))),
        scratch_shapes=[pltpu.VMEM((N_EXPERTS, 1), F32)],
        compiler_params=_params(("arbitrary",)),
        name="route_rank",
    )(topi, jnp.asarray(tri, BF16))


def _dest_kernel(cnt_ref, ltri_ref, ti_ref, rank_ref, dest_ref, eblk_ref):
    cnt = cnt_ref[...]
    nblk = jnp.floor((cnt + (MOE_ROWS - 1)) * (1.0 / MOE_ROWS))
    pstart_b = _dot(ltri_ref[...], nblk.astype(BF16))
    pstart = (pstart_b[:, 0:1] * MOE_ROWS).astype(I32)
    ti = ti_ref[...]
    tl = ti.shape[1]
    eidx = lax.broadcasted_iota(I32, (N_EXPERTS, tl), 0)
    rows = []
    for k in range(TOP_K):
        oh = eidx == ti[k:k + 1, :]
        rows.append(jnp.sum(jnp.where(oh, pstart, 0), axis=0, keepdims=True))
    dest_ref[...] = jnp.concatenate(rows, axis=0) + rank_ref[...]
    eblk_ref[0] = pstart_b.astype(I32)
    eblk_ref[1] = nblk.astype(I32)


def _destinations(counts, topi, rank):
    k, t = topi.shape
    tl = min(2048, t)
    ltri = np.tril(np.ones((N_EXPERTS, N_EXPERTS), np.float32), -1)
    return pl.pallas_call(
        _dest_kernel,
        out_shape=(
            jax.ShapeDtypeStruct((k, t), I32),
            jax.ShapeDtypeStruct((2, N_EXPERTS, LANES), I32),
        ),
        grid=(t // tl,),
        in_specs=[
            pl.BlockSpec((N_EXPERTS, LANES), lambda i: (0, 0)),
            pl.BlockSpec((N_EXPERTS, N_EXPERTS), lambda i: (0, 0)),
            pl.BlockSpec((k, tl), lambda i: (0, i)),
            pl.BlockSpec((k, tl), lambda i: (0, i)),
        ],
        out_specs=(
            pl.BlockSpec((k, tl), lambda i: (0, i)),
            pl.BlockSpec((2, N_EXPERTS, LANES), lambda i: (0, 0, 0)),
        ),
        compiler_params=_params(("arbitrary",)),
        name="route_dest",
    )(counts, jnp.asarray(ltri, BF16), topi, rank)


def _pack_bf16_pairs(x):
    n = x.shape[1] // 2
    lo = lax.bitcast_convert_type(x[:, :n].astype(BF16).astype(F32), U32)
    hi = lax.bitcast_convert_type(x[:, n:].astype(BF16).astype(F32), U32)
    return hi | (lo >> 16)


def _unpack_bf16_pairs(w):
    lo = lax.bitcast_convert_type(w << 16, F32)
    hi = lax.bitcast_convert_type(w & jnp.uint32(0xFFFF0000), F32)
    return lo, hi


SC_WINDOW = 128


def _sc_mesh():
    return plsc.VectorSubcoreMesh(core_axis_name="core", subcore_axis_name="subcore")


def _sc_worker(mesh):
    return lax.axis_index("core") * mesh.num_subcores + lax.axis_index("subcore"), mesh.num_cores * mesh.num_subcores


def _sc_scatter_rows(src, dest_rows, n_out):
    n, width = src.shape
    nk = dest_rows.shape[0]
    mesh = _sc_mesh()
    half = SC_WINDOW // 2

    @pl.kernel(out_type=jax.ShapeDtypeStruct((n_out, width), src.dtype), mesh=mesh,
               scratch_types=[pltpu.VMEM((nk, SC_WINDOW), I32), pltpu.VMEM((2, half, width), src.dtype),
                              pltpu.SemaphoreType.DMA((2,))],
               name="sc_scatter_rows")
    def scatter_kernel(src_hbm, idx_hbm, out_hbm, idx_v, rows_v, sem):
        wid, nw = _sc_worker(mesh)
        per = n // SC_WINDOW // nw
        row0 = wid * per * SC_WINDOW

        def load(s, buf):
            return pltpu.make_async_copy(src_hbm.at[pl.ds(row0 + s * half, half)], rows_v.at[buf], sem.at[buf])

        load(0, 0).start()

        @pl.loop(0, per)
        def _(j):
            pltpu.sync_copy(idx_hbm.at[:, pl.ds(row0 + j * SC_WINDOW, SC_WINDOW)], idx_v)
            for h in range(2):
                if h == 0:
                    load(2 * j + 1, 1).start()
                else:
                    @pl.when(j + 1 < per)
                    def _():
                        load(2 * j + 2, 0).start()

                load(2 * j + h, h).wait()
                for k in range(nk):
                    pltpu.sync_copy(rows_v.at[h], out_hbm.at[idx_v.at[k, pl.ds(h * half, half)]])

    return scatter_kernel(src, dest_rows)


def _sc_gather_rows(src, rows):
    n = rows.shape[0]
    width = src.shape[1]
    mesh = _sc_mesh()
    half = SC_WINDOW // 2

    @pl.kernel(out_type=jax.ShapeDtypeStruct((n, width), src.dtype), mesh=mesh,
               scratch_types=[pltpu.VMEM((1, SC_WINDOW), I32), pltpu.VMEM((2, half, width), src.dtype),
                              pltpu.SemaphoreType.DMA((2,))],
               name="sc_gather_rows")
    def gather_kernel(src_hbm, idx_hbm, out_hbm, idx_v, rows_v, sem):
        wid, nw = _sc_worker(mesh)
        per = n // SC_WINDOW // nw
        row0 = wid * per * SC_WINDOW

        def store(s, buf):
            return pltpu.make_async_copy(rows_v.at[buf], out_hbm.at[pl.ds(row0 + s * half, half)], sem.at[buf])

        @pl.loop(0, per)
        def _(j):
            pltpu.sync_copy(idx_hbm.at[:, pl.ds(row0 + j * SC_WINDOW, SC_WINDOW)], idx_v)
            for h in range(2):
                @pl.when(j >= 1)
                def _():
                    store(2 * j + h - 2, h).wait()

                pltpu.sync_copy(src_hbm.at[idx_v.at[0, pl.ds(h * half, half)]], rows_v.at[h])
                store(2 * j + h, h).start()

        for h in range(2):
            store(2 * per - 2 + h, h).wait()

    return gather_kernel(src, rows.reshape(1, n))


def _expert_group_kernel(eb_ref, xs_hbm, wgu_hbm, bgu_ref, wd_hbm, bd_ref, ys_hbm,
                         xbuf, ybuf, wgu_f, wd_f, wgu_bf, wd_bf, act_scr, xsem, ysem, wsem, *, dff, l):
    e = pl.program_id(0)
    ne = pl.num_programs(0)
    b0 = eb_ref[e]
    nb = eb_ref[ne + e]
    rows, half = xbuf.shape[1], xbuf.shape[2]
    group = rows // MOE_ROWS
    nch = (nb + group - 1) // group
    wslot = e % 2

    def w_copies(ex, slot):
        cps = []
        for src, dst, first in ((wgu_hbm, wgu_f, 0), (wd_hbm, wd_f, WEIGHT_CHUNKS)):
            step = dst.shape[1] // WEIGHT_CHUNKS
            for c in range(WEIGHT_CHUNKS):
                cps.append(pltpu.make_async_copy(src.at[l, ex, pl.ds(c * step, step), :],
                                                 dst.at[slot, pl.ds(c * step, step), :], wsem.at[slot, first + c]))
        return cps

    @pl.when(e == 0)
    def _():
        for c, cp in enumerate(w_copies(0, 0)):
            cp.start(priority=c % 2)

    @pl.when(e + 1 < ne)
    def _():
        for c, cp in enumerate(w_copies(e + 1, 1 - wslot)):
            cp.start(priority=c % 2)

    for cp in w_copies(e, wslot):
        cp.wait()

    def x_copy(j, slot):
        start = (b0 + j * group) * MOE_ROWS
        return pltpu.make_async_copy(xs_hbm.at[pl.ds(start, rows), :], xbuf.at[slot], xsem.at[slot])

    def y_copy(j, slot, g):
        start = (b0 + j * group + g) * MOE_ROWS
        return pltpu.make_async_copy(ybuf.at[slot, pl.ds(g * MOE_ROWS, MOE_ROWS), :],
                                     ys_hbm.at[pl.ds(start, MOE_ROWS), :], ysem.at[slot, g])

    def y_each(j, slot, fn):
        for g in range(group):
            @pl.when(j * group + g < nb)
            def _():
                fn(y_copy(j, slot, g))

    @pl.when(nb > 0)
    def _():
        x_copy(0, 0).start(priority=1)
        wgu_bf[...] = wgu_f[wslot].astype(BF16)
        wd_bf[...] = wd_f[wslot].astype(BF16)

        def chunk(j, carry):
            slot = j % 2

            @pl.when(j + 1 < nch)
            def _():
                x_copy(j + 1, 1 - slot).start(priority=1)

            x_copy(j, slot).wait()

            @pl.when(j >= 2)
            def _():
                y_each(j - 2, slot, lambda cp: cp.wait())

            lo, hi = _unpack_bf16_pairs(xbuf[slot])
            xb = jnp.concatenate([lo.astype(BF16), hi.astype(BF16)], axis=1)
            for c0 in range(0, dff, EXPERT_COLS):
                gate = _dot(xb, wgu_bf[:, c0:c0 + EXPERT_COLS]) + bgu_ref[:, c0:c0 + EXPERT_COLS]
                up = (_dot(xb, wgu_bf[:, dff + c0:dff + c0 + EXPERT_COLS])
                      + bgu_ref[:, dff + c0:dff + c0 + EXPERT_COLS])
                glu = jnp.minimum(gate, SWIGLU_LIMIT)
                lin = jnp.clip(up, -SWIGLU_LIMIT, SWIGLU_LIMIT)
                act_scr[:, c0:c0 + EXPERT_COLS] = (glu * _sigmoid(SWIGLU_ALPHA * glu) * (lin + 1.0)).astype(BF16)
            ybuf[slot] = _pack_bf16_pairs(_dot(act_scr[...], wd_bf[...]) + bd_ref[...])
            y_each(j, slot, lambda cp: cp.start(priority=1))
            return carry

        lax.fori_loop(0, nch, chunk, 0)

        @pl.when(nch >= 2)
        def _():
            y_each(nch - 2, nch % 2, lambda cp: cp.wait())

        y_each(nch - 1, (nch - 1) % 2, lambda cp: cp.wait())


def _block_experts(eblk, xs, w_gu, b_gu4, w_down, b_down4, l):
    cap, half = xs.shape
    d = 2 * half
    dff = w_down.shape[2]
    wsel = lambda e, eb: (l, e, 0, 0)
    grid_spec = pltpu.PrefetchScalarGridSpec(
        num_scalar_prefetch=1,
        grid=(N_EXPERTS,),
        in_specs=[
            pl.BlockSpec(memory_space=pl.ANY),
            pl.BlockSpec(memory_space=pl.ANY),
            pl.BlockSpec((None, None, 1, 2 * dff), wsel),
            pl.BlockSpec(memory_space=pl.ANY),
            pl.BlockSpec((None, None, 1, d), wsel),
        ],
        out_specs=pl.BlockSpec(memory_space=pl.ANY),
        scratch_shapes=[
            pltpu.VMEM((2, EXPERT_GROUP * MOE_ROWS, half), U32),
            pltpu.VMEM((2, EXPERT_GROUP * MOE_ROWS, half), U32),
            pltpu.VMEM((2, d, 2 * dff), F32),
            pltpu.VMEM((2, dff, d), F32),
            pltpu.VMEM((d, 2 * dff), BF16),
            pltpu.VMEM((dff, d), BF16),
            pltpu.VMEM((EXPERT_GROUP * MOE_ROWS, dff), BF16),
            pltpu.SemaphoreType.DMA((2,)),
            pltpu.SemaphoreType.DMA((2, EXPERT_GROUP)),
            pltpu.SemaphoreType.DMA((2, 2 * WEIGHT_CHUNKS)),
        ],
    )
    return pl.pallas_call(
        functools.partial(_expert_group_kernel, dff=dff, l=l),
        out_shape=jax.ShapeDtypeStruct((cap - (EXPERT_GROUP - 1) * MOE_ROWS, half), U32),
        grid_spec=grid_spec,
        compiler_params=_params(("arbitrary",)),
        name="moe_expert_groups",
    )(eblk, xs, w_gu, b_gu4, w_down, b_down4)


def _combine_kernel(*refs, d, final):
    y_refs = refs[:TOP_K]
    gt_ref, x_ref, mod_ref = refs[TOP_K:TOP_K + 3]
    xo_ref = refs[-1]
    half = d // 2
    gt = gt_ref[...]
    acc_lo = acc_hi = None
    for k in range(TOP_K):
        lo, hi = _unpack_bf16_pairs(y_refs[k][...])
        g = gt[:, k:k + 1]
        acc_lo = g * lo if acc_lo is None else acc_lo + g * lo
        acc_hi = g * hi if acc_hi is None else acc_hi + g * hi
    x_lo = x_ref[:, 0:half] + mod_ref[:, 5 * d:5 * d + half] * acc_lo
    x_hi = x_ref[:, half:d] + mod_ref[:, 5 * d + half:6 * d] * acc_hi
    if final:
        gain = refs[TOP_K + 3][...]
        ssq = jnp.sum(x_lo * x_lo, axis=-1, keepdims=True) + jnp.sum(x_hi * x_hi, axis=-1, keepdims=True)
        inv = lax.rsqrt(ssq * (1.0 / d) + EPS)
        x_lo = x_lo * inv * gain[:, 0:half]
        x_hi = x_hi * inv * gain[:, half:d]
    xo_ref[:, 0:half] = x_lo
    xo_ref[:, half:d] = x_hi


def _combine(y4p, gates_t, x2, mod3, l, seq, depth, row0, prev, final_gain=None):
    t, d = x2.shape
    n = gates_t.shape[0]
    tm = min(PROJ_TILE, seq)
    nb = mod3.shape[0] // depth
    nt = n // tm
    t0 = row0 // tm
    final = final_gain is not None

    def slot(k):
        return pl.BlockSpec((tm, d // 2), lambda i: (k * nt + i, 0))

    specs = [slot(k) for k in range(TOP_K)] + [
        pl.BlockSpec((tm, TOP_K), lambda i: (i, 0)),
        pl.BlockSpec((tm, d), lambda i: (t0 + i, 0)),
        pl.BlockSpec((None, 1, N_MOD * d), lambda i: (l * nb + (row0 + i * tm) // seq, 0, 0)),
    ]
    args = [y4p] * TOP_K + [gates_t, x2, mod3]
    if final:
        specs.append(pl.BlockSpec((1, d), lambda i: (0, 0)))
        args.append(final_gain.reshape(1, d))
    aliases = {}
    if prev is not None:
        specs.append(pl.BlockSpec(memory_space=pl.ANY))
        aliases = {len(args): 0}
        args.append(prev)
    return pl.pallas_call(
        functools.partial(_combine_kernel, d=d, final=final),
        out_shape=jax.ShapeDtypeStruct((t, d), F32),
        grid=(nt,),
        in_specs=specs,
        out_specs=pl.BlockSpec((tm, d), lambda i: (t0 + i, 0)),
        input_output_aliases=aliases,
        compiler_params=_params(("arbitrary",)),
        name="moe_combine",
    )(*args)


def _mixer_layer(x2, mod3, tables, lower, consts, p, l, batch, seq):
    ra, rb, rc = tables
    qa, ka2, va2, hg = _inproj(x2, mod3, p["norm1"], p["w_in"], ra, rb, rc, l, seq)
    a = _attention(qa, ka2, va2, p["attn_sink"], p["attn_norm"], l, batch, seq)
    o = _hgrn(hg, lower, p["hg_norm"], consts, l, batch, seq)
    return a, o


def _moe_layer(a, o, x2, mod3, p, l, seq, depth, final_gain=None):
    t, d = x2.shape
    xn, h2p, topi, gates = _outproj(a, o, x2, mod3, p["norm2"], p["w_out"], p["w_router_t"], p["b_router"], l, seq)
    rank, counts = _ranks(topi)
    nblocks = (t * TOP_K + N_EXPERTS * MOE_ROWS) // MOE_ROWS
    dest, eblk = _destinations(counts, topi, rank)
    xs = _sc_scatter_rows(h2p, dest, (nblocks + EXPERT_GROUP - 1) * MOE_ROWS)
    ys = _block_experts(eblk[:, :, 0].reshape(-1), xs, p["w_gu"], p["b_gu"], p["w_down"], p["b_down"], l)
    gates_t = gates.T
    out = None
    for part in range(MOE_COMBINE_PARTS):
        rows = slice(part * (t // MOE_COMBINE_PARTS), (part + 1) * (t // MOE_COMBINE_PARTS))
        y4p = _sc_gather_rows(ys, dest[:, rows].reshape(-1))
        out = _combine(y4p, gates_t[rows], xn, mod3, l, seq, depth, rows.start, out, final_gain)
    return out


def kernel(x, c, positions, w_ada, b_ada, norm1, w_in, attn_sink, attn_norm, hg_lb_logits, hg_norm, w_out, norm2,
           w_router, b_router, w_gu, b_gu, w_down, b_down, final_norm):
    batch, seq, d = x.shape
    depth = w_ada.shape[0]
    t = batch * seq
    p = {
        "norm1": norm1.reshape(depth, 1, d),
        "w_in": w_in.astype(BF16),
        "attn_sink": attn_sink.astype(F32),
        "attn_norm": attn_norm.reshape(depth, 1, ATT_WIDTH),
        "hg_norm": hg_norm.reshape(depth, 1, HG_HEAD_DIM),
        "w_out": w_out.astype(BF16),
        "norm2": norm2.reshape(depth, 1, d),
        "w_router_t": jnp.swapaxes(w_router, 1, 2),
        "b_router": b_router.reshape(depth, N_EXPERTS, 1),
        "w_gu": w_gu,
        "b_gu": b_gu.reshape(depth, N_EXPERTS, 1, b_gu.shape[-1]),
        "w_down": w_down,
        "b_down": b_down.reshape(depth, N_EXPERTS, 1, d),
    }
    mod3 = _ada_all(c, w_ada, b_ada).reshape(depth * batch, 1, N_MOD * d)
    lower = _lower_bounds(hg_lb_logits)
    tables = _rope_tables(positions)
    consts = tuple(jnp.asarray(m) for m in _hgrn_constants())
    x2 = x.reshape(t, d)
    for l in range(depth):
        a, o = _mixer_layer(x2, mod3, tables, lower, consts, p, l, batch, seq)
        x2 = _moe_layer(a, o, x2, mod3, p, l, seq, depth, final_norm if l == depth - 1 else None)
    return x2.reshape(batch, seq, d)
```

```python
import functools

import numpy as np
import jax
import jax.numpy as jnp
from jax import lax
from jax.experimental import pallas as pl
from jax.experimental.pallas import tpu as pltpu
from jax.experimental.pallas import tpu_sc as plsc

F32 = jnp.float32
BF16 = jnp.bfloat16
I32 = jnp.int32
U32 = jnp.uint32

ATT_HEADS = 8
ATT_KV_HEADS = 2
ATT_HEAD_DIM = 64
ATT_WIDTH = ATT_HEADS * ATT_HEAD_DIM
KV_WIDTH = ATT_KV_HEADS * ATT_HEAD_DIM
WINDOW = 128
ATT_BLOCK = 128
ROPE_THETA = 500000.0
ROPE_DIM = ATT_HEAD_DIM // 4
HG_HEADS = 4
HG_HEAD_DIM = 128
HG_WIDTH = HG_HEADS * HG_HEAD_DIM
N_EXPERTS = 32
TOP_K = 4
SWIGLU_ALPHA = 1.702
SWIGLU_LIMIT = 7.0
N_MOD = 6
EPS = 1e-6
NEG_INF = -1e30
LB_FLOOR = 1e-30

LANES = 128
HG_CHUNK = 64
HG_LEVELS = 6
HG_UNROLL = 8
MOE_ROWS = 256
EXPERT_GROUP = 1
WEIGHT_CHUNKS = 4
EXPERT_COLS = 512
MOE_COMBINE_PARTS = 2
PROJ_TILE = 512
RANK_TILE = 512
VMEM_LIMIT = 56 * 1024 * 1024


def _dot(a, b):
    return jnp.dot(a, b, preferred_element_type=F32)


def _dot_nt(a, b):
    return lax.dot_general(a, b, (((1,), (1,)), ((), ())), preferred_element_type=F32)


def _dot_tn(a, b):
    return lax.dot_general(a, b, (((0,), (0,)), ((), ())), preferred_element_type=F32)


def _split3(x):
    hi = x.astype(BF16)
    r1 = x - hi.astype(F32)
    mid = r1.astype(BF16)
    lo = (r1 - mid.astype(F32)).astype(BF16)
    return hi, mid, lo


def _dot_f32_nt(a, b):
    ah, am, _ = _split3(a)
    bh, bm, _ = _split3(b)
    return _dot_nt(ah, bh) + _dot_nt(ah, bm) + _dot_nt(am, bh)


def _dot_f32(a, b):
    ah, am, _ = _split3(a)
    bh, bm, _ = _split3(b)
    return _dot(ah, bh) + _dot(ah, bm) + _dot(am, bh)


def _sigmoid(x):
    return 1.0 / (1.0 + jnp.exp(-x))


def _params(sem=None):
    return pltpu.CompilerParams(dimension_semantics=sem, vmem_limit_bytes=VMEM_LIMIT)


def _ada_kernel(c_ref, w_ref, b_ref, o_ref):
    c = c_ref[...]
    cond = c * _sigmoid(c)
    o_ref[...] = _dot_f32(cond, w_ref[...]) + b_ref[...]


def _ada_all(c, w_ada, b_ada):
    depth, d, n = w_ada.shape
    b = c.shape[0]
    nt = n // d
    return pl.pallas_call(
        _ada_kernel,
        out_shape=jax.ShapeDtypeStruct((depth, b, n), F32),
        grid=(depth, nt),
        in_specs=[
            pl.BlockSpec((b, d), lambda l, j: (0, 0)),
            pl.BlockSpec((None, d, d), lambda l, j: (l, 0, j)),
            pl.BlockSpec((None, 1, d), lambda l, j: (l, 0, j)),
        ],
        out_specs=pl.BlockSpec((None, b, d), lambda l, j: (l, 0, j)),
        compiler_params=_params(("arbitrary", "arbitrary")),
        name="ada_mod",
    )(c, w_ada, b_ada.reshape(depth, 1, n))


def _lb_kernel(x_ref, o_ref):
    depth = x_ref.shape[0]
    xs = [x_ref[l] for l in range(depth)]
    m = xs[0]
    for l in range(1, depth):
        m = jnp.maximum(m, xs[l])
    es = [jnp.exp(v - m) for v in xs]
    den = es[0]
    for l in range(1, depth):
        den = den + es[l]
    ps = [e / den for e in es]
    run = ps[0]
    o_ref[0] = run - ps[0]
    for l in range(1, depth):
        run = run + ps[l]
        o_ref[l] = run - ps[0]


def _lower_bounds(hg_lb_logits):
    return pl.pallas_call(
        _lb_kernel,
        out_shape=jax.ShapeDtypeStruct(hg_lb_logits.shape, F32),
        name="hg_lower_bounds",
    )(hg_lb_logits.astype(F32))


def _rope_kernel(pos_ref, invf_ref, a_ref, b_ref, c_ref):
    pos = pos_ref[...].astype(F32)
    ang = pos * invf_ref[...]
    cs = jnp.cos(ang)
    sn = jnp.sin(ang)
    lane = lax.broadcasted_iota(I32, ang.shape, 1) & (ATT_HEAD_DIM - 1)
    half = ROPE_DIM // 2
    first = lane < half
    second = (lane >= half) & (lane < ROPE_DIM)
    a_ref[...] = jnp.where(first | second, cs, 1.0)
    b_ref[...] = jnp.where(first, -sn, 0.0)
    c_ref[...] = jnp.where(second, sn, 0.0)


def _rope_tables(positions):
    t = positions.size
    half = ROPE_DIM // 2
    inv = (np.float32(ROPE_THETA) ** (-(np.arange(half, dtype=np.float32) * np.float32(2.0) / np.float32(ROPE_DIM)))).astype(np.float32)
    lane = np.arange(LANES) % ATT_HEAD_DIM
    pat = np.where(lane < ROPE_DIM, inv[lane % half], 0.0).astype(np.float32).reshape(1, LANES)
    tm = min(t, 2048)
    shp = jax.ShapeDtypeStruct((t, LANES), F32)
    spec = pl.BlockSpec((tm, LANES), lambda i: (i, 0))
    return pl.pallas_call(
        _rope_kernel,
        out_shape=(shp, shp, shp),
        grid=(t // tm,),
        in_specs=[pl.BlockSpec((tm, 1), lambda i: (i, 0)), pl.BlockSpec((1, LANES), lambda i: (0, 0))],
        out_specs=(spec, spec, spec),
        compiler_params=_params(("arbitrary",)),
        name="rope_tables",
    )(positions.reshape(t, 1).astype(I32), jnp.asarray(pat))


def _rms_mod(x, gain, scale, shift):
    ms = jnp.mean(x * x, axis=-1, keepdims=True)
    return (x * lax.rsqrt(ms + EPS) * gain) * (1.0 + scale) + shift


def _rope_apply(x, a, b, c):
    half = ROPE_DIM // 2
    return x * a + pltpu.roll(x, LANES - half, 1) * b + pltpu.roll(x, half, 1) * c


def _inproj_kernel(x_ref, mod_ref, n1_ref, w_ref, ra_ref, rb_ref, rc_ref, qa_ref, ka_ref, va_ref, hg_ref, *, d):
    mod = mod_ref[...]
    h = _rms_mod(x_ref[...], n1_ref[...], mod[:, d:2 * d], mod[:, 0:d]).astype(BF16)
    a, b, c = ra_ref[...], rb_ref[...], rc_ref[...]
    kvw = ATT_WIDTH + 2 * KV_WIDTH
    pa = _dot(h, w_ref[:, 0:kvw])
    scale = ATT_HEAD_DIM ** -0.5
    for g in range(ATT_WIDTH // LANES):
        qg = _rope_apply(pa[:, g * LANES:(g + 1) * LANES], a, b, c)
        qa_ref[:, g * LANES:(g + 1) * LANES] = (qg * scale).astype(BF16)
    k = _rope_apply(pa[:, ATT_WIDTH:ATT_WIDTH + KV_WIDTH], a, b, c)
    v = pa[:, ATT_WIDTH + KV_WIDTH:kvw]
    ka_ref[:, 0:LANES] = k.astype(BF16)
    ka_ref[:, LANES:2 * LANES] = pltpu.roll(k, ATT_HEAD_DIM, 1).astype(BF16)
    va_ref[:, 0:LANES] = v.astype(BF16)
    va_ref[:, LANES:2 * LANES] = pltpu.roll(v, ATT_HEAD_DIM, 1).astype(BF16)
    for g in range(5):
        lo = kvw + g * HG_WIDTH
        hg_ref[:, g * HG_WIDTH:(g + 1) * HG_WIDTH] = _dot(h, w_ref[:, lo:lo + HG_WIDTH])


def _inproj(x2, mod3, norm1, w_in_bf, ra, rb, rc, l, seq):
    t, d = x2.shape
    n_in = w_in_bf.shape[-1]
    tm = min(PROJ_TILE, seq)
    nb = mod3.shape[0] // norm1.shape[0]
    tok = lambda i: (i, 0)
    return pl.pallas_call(
        functools.partial(_inproj_kernel, d=d),
        out_shape=(
            jax.ShapeDtypeStruct((t, ATT_WIDTH), BF16),
            jax.ShapeDtypeStruct((t, 2 * KV_WIDTH), BF16),
            jax.ShapeDtypeStruct((t, 2 * KV_WIDTH), BF16),
            jax.ShapeDtypeStruct((t, 5 * HG_WIDTH), F32),
        ),
        grid=(t // tm,),
        in_specs=[
            pl.BlockSpec((tm, d), tok),
            pl.BlockSpec((None, 1, N_MOD * d), lambda i: (l * nb + (i * tm) // seq, 0, 0)),
            pl.BlockSpec((None, 1, d), lambda i: (l, 0, 0)),
            pl.BlockSpec((None, d, n_in), lambda i: (l, 0, 0)),
            pl.BlockSpec((tm, LANES), tok),
            pl.BlockSpec((tm, LANES), tok),
            pl.BlockSpec((tm, LANES), tok),
        ],
        out_specs=(
            pl.BlockSpec((tm, ATT_WIDTH), tok),
            pl.BlockSpec((tm, 2 * KV_WIDTH), tok),
            pl.BlockSpec((tm, 2 * KV_WIDTH), tok),
            pl.BlockSpec((tm, 5 * HG_WIDTH), tok),
        ),
        compiler_params=_params(("arbitrary",)),
        name="in_proj",
    )(x2, mod3, norm1, w_in_bf, ra, rb, rc)


def _attn_kernel(sink_ref, q_ref, kp_ref, kc_ref, kn_ref, vp_ref, vc_ref, vn_ref, gain_ref, o_ref, *, l, seq):
    n = pl.program_id(1)
    blk = ATT_BLOCK
    k2 = jnp.concatenate([kp_ref[...], kc_ref[...], kn_ref[...]], axis=0)
    v2 = jnp.concatenate([vp_ref[...], vc_ref[...], vn_ref[...]], axis=0)
    lane = lax.broadcasted_iota(I32, (3 * blk, LANES), 1)
    lo_half = lane < ATT_HEAD_DIM
    zero = jnp.zeros((3 * blk, LANES), BF16)
    ka, kb = k2[:, 0:LANES], k2[:, LANES:2 * LANES]
    va, vb = v2[:, 0:LANES], v2[:, LANES:2 * LANES]
    kz = [[jnp.where(lo_half, ka, zero), jnp.where(lo_half, zero, kb)],
          [jnp.where(lo_half, kb, zero), jnp.where(lo_half, zero, ka)]]
    vz = [[jnp.where(lo_half, va, zero), jnp.where(lo_half, zero, vb)],
          [jnp.where(lo_half, vb, zero), jnp.where(lo_half, zero, va)]]
    qpos = n * blk + lax.broadcasted_iota(I32, (blk, 3 * blk), 0)
    kpos = (n - 1) * blk + lax.broadcasted_iota(I32, (blk, 3 * blk), 1)
    valid = (jnp.abs(qpos - kpos) <= WINDOW) & (kpos >= 0) & (kpos < seq)
    valid2 = jnp.concatenate([valid, valid], axis=0)
    upper = lax.broadcasted_iota(I32, (2 * blk, 1), 0) < blk
    outs = []
    for j in range(ATT_KV_HEADS):
        qs = jnp.concatenate([q_ref[:, 2 * j * LANES:(2 * j + 1) * LANES],
                              q_ref[:, (2 * j + 1) * LANES:(2 * j + 2) * LANES]], axis=0)
        acc = None
        for half in range(2):
            s = _dot_nt(qs, kz[j][half])
            s = jnp.where(valid2, s, NEG_INF)
            sink = jnp.where(upper, sink_ref[l, 4 * j + half], sink_ref[l, 4 * j + 2 + half])
            mx = jnp.maximum(jnp.max(s, axis=-1, keepdims=True), sink)
            p = jnp.exp(s - mx)
            den = jnp.sum(p, axis=-1, keepdims=True) + jnp.exp(sink - mx)
            p = (p * (1.0 / den)).astype(BF16)
            pv = _dot(p, vz[j][half])
            acc = pv if acc is None else acc + pv
        outs.append(acc[0:blk])
        outs.append(acc[blk:2 * blk])
    o = jnp.concatenate(outs, axis=-1)
    ms = jnp.mean(o * o, axis=-1, keepdims=True)
    o_ref[...] = (o * lax.rsqrt(ms + EPS) * gain_ref[...]).astype(BF16)


def _attention(qa, ka2, va2, attn_sink, attn_norm3, l, batch, seq):
    t = qa.shape[0]
    blk = ATT_BLOCK
    nb = seq // blk
    cur = lambda b, n: (b * nb + n, 0)
    prev = lambda b, n: (b * nb + jnp.maximum(n - 1, 0), 0)
    nxt = lambda b, n: (b * nb + jnp.minimum(n + 1, nb - 1), 0)
    kvspec = lambda f: pl.BlockSpec((blk, 2 * KV_WIDTH), f)
    return pl.pallas_call(
        functools.partial(_attn_kernel, l=l, seq=seq),
        out_shape=jax.ShapeDtypeStruct((t, ATT_WIDTH), BF16),
        grid=(batch, nb),
        in_specs=[
            pl.BlockSpec(memory_space=pltpu.SMEM),
            pl.BlockSpec((blk, ATT_WIDTH), cur),
            kvspec(prev), kvspec(cur), kvspec(nxt),
            kvspec(prev), kvspec(cur), kvspec(nxt),
            pl.BlockSpec((None, 1, ATT_WIDTH), lambda b, n: (l, 0, 0)),
        ],
        out_specs=pl.BlockSpec((blk, ATT_WIDTH), cur),
        compiler_params=_params(("arbitrary", "arbitrary")),
        name="window_attn",
    )(attn_sink, qa, ka2, ka2, ka2, va2, va2, va2, attn_norm3)


def _hgrn_constants():
    c, nl = HG_CHUNK, HG_LEVELS
    r = np.arange(c)
    masks = []
    for lev in range(nl):
        m = 1 << lev
        parent = r // (2 * m)
        upper = r >= parent * 2 * m + m
        masks.append((parent[:, None] == parent[None, :]) & upper[:, None] & (~upper)[None, :])
    masks.append(np.eye(c, dtype=bool))
    kf = np.stack(masks).astype(np.float32)
    kb = np.stack([mk[::-1, ::-1] for mk in masks]).astype(np.float32)
    return kf, kb


def _chunk_decays(logf, reverse):
    c = logf.shape[0]
    sub = 8
    nv = c // sub
    row = lax.broadcasted_iota(I32, (sub, LANES), 0)
    parts = []
    for v in range(nv):
        x = logf[sub * v:sub * (v + 1), :]
        for s in (1, 2, 4):
            if reverse:
                x = x + jnp.where(row < sub - s, pltpu.roll(x, sub - s, 0), 0.0)
            else:
                x = x + jnp.where(row >= s, pltpu.roll(x, s, 0), 0.0)
        parts.append(x)
    b = [None] * nv
    order = list(reversed(range(nv))) if reverse else list(range(nv))
    edge = 0 if reverse else sub - 1
    carry = None
    for v in order:
        b[v] = parts[v] if carry is None else parts[v] + carry
        carry = b[v][edge:edge + 1, :]
    b_last = carry

    def anchor_row(v, r):
        return jnp.broadcast_to(b[v][r:r + 1, :], (sub, LANES))

    odd = (row & 1) == 1
    levels = []
    for lev in range(HG_LEVELS):
        m = 1 << lev
        pieces = []
        for v in range(nv):
            if m == 1:
                a = jnp.where(odd, pltpu.roll(b[v], 1, 0), b[v]) if reverse else jnp.where(odd, b[v], pltpu.roll(b[v], sub - 1, 0))
            elif m == 2:
                lo, hi = (1, 5) if reverse else (2, 6)
                a = jnp.where(row < 4, anchor_row(v, lo), anchor_row(v, hi))
            elif m == 4:
                a = anchor_row(v, 3 if reverse else 4)
            else:
                mv = m // sub
                first = (v // (2 * mv)) * 2 * mv
                a = anchor_row(first + mv - 1, sub - 1) if reverse else anchor_row(first + mv, 0)
            pieces.append(jnp.exp(-jnp.abs(b[v] - a)))
        levels.append(jnp.concatenate(pieces, axis=0))
    eb = jnp.concatenate([jnp.exp(bv) for bv in b], axis=0)
    erem = jnp.concatenate([jnp.exp(b_last - bv) for bv in b], axis=0)
    return levels, eb, erem


def _hgrn_kernel(q_ref, ff_ref, fb_ref, i_ref, g_ref, lb_ref, gn_ref, kf_ref, kb_ref,
                 o_ref, of_scr, ob_scr, st_scr, *, seq):
    c, nl = HG_CHUNK, HG_LEVELS
    nc = seq // c
    lb = lb_ref[...]
    st_scr[...] = jnp.zeros(st_scr.shape, F32)

    unroll = HG_UNROLL if nc % HG_UNROLL == 0 else 1
    dirs = (
        dict(f_ref=ff_ref, lbrow=lb[0:1, :], k_ref=kf_ref, o_scr=of_scr, last_row=c - 1, d=0),
        dict(f_ref=fb_ref, lbrow=lb[1:2, :], k_ref=kb_ref, o_scr=ob_scr, last_row=0, d=1),
    )

    def body(i, carry):
        work = []
        for u in range(unroll):
            cf = i * unroll + u
            work.append((dirs[0], pl.ds(pl.multiple_of(cf * c, c), c)))
            work.append((dirs[1], pl.ds(pl.multiple_of((nc - 1 - cf) * c, c), c)))
        gates = []
        for dr, rows in work:
            lbf = jnp.maximum(dr["lbrow"], LB_FLOOR)
            oml = 1.0 - dr["lbrow"]
            f = dr["f_ref"][rows, :]
            e = jnp.exp(-jnp.abs(f))
            r = 1.0 / (1.0 + e)
            er = e * r
            pos = f >= 0.0
            logf = jnp.log(lbf + oml * jnp.where(pos, r, er))
            kk = oml * jnp.where(pos, er, r)
            qh = q_ref[rows, :]
            gates.append((logf, kk.astype(BF16), (qh * _sigmoid(qh)).astype(BF16), i_ref[rows, :].astype(BF16)))
        decays = [_chunk_decays(g[0], dr["d"] == 1) for (dr, _), g in zip(work, gates)]
        amat = [dr["k_ref"][nl] * _dot_nt(g[2], g[1]) for (dr, _), g in zip(work, gates)]
        for lev in range(nl):
            for j, ((dr, _), g) in enumerate(zip(work, gates)):
                gl = decays[j][0][lev].astype(BF16)
                amat[j] = amat[j] + dr["k_ref"][lev] * _dot_nt(g[2] * gl, g[1] * gl)
        intra = [_dot(amat[j].astype(BF16), g[3]) for j, g in enumerate(gates)]
        upd = [_dot_tn(g[3], g[1] * decays[j][2].astype(BF16)) for j, g in enumerate(gates)]
        st = [st_scr[0], st_scr[1]]
        for j, ((dr, rows), g) in enumerate(zip(work, gates)):
            eb = decays[j][1]
            d = dr["d"]
            dr["o_scr"][rows, :] = _dot_nt(g[2] * eb.astype(BF16), st[d].astype(BF16)) + intra[j]
            st[d] = st[d] * eb[dr["last_row"]:dr["last_row"] + 1, :] + upd[j]
        st_scr[0] = st[0]
        st_scr[1] = st[1]
        return carry

    lax.fori_loop(0, nc // unroll, body, 0)

    ep = min(256, seq)
    gn = gn_ref[...]

    def epilogue(j, carry):
        rows = pl.ds(pl.multiple_of(j * ep, ep), ep)
        o = of_scr[rows, :] + ob_scr[rows, :]
        y = o * lax.rsqrt(jnp.mean(o * o, axis=-1, keepdims=True) + EPS) * gn
        g = g_ref[rows, :]
        o_ref[rows, :] = (y * (g * _sigmoid(g))).astype(BF16)
        return carry

    lax.fori_loop(0, seq // ep, epilogue, 0)


def _hgrn(hg, lower, hg_norm3, consts, l, batch, seq):
    t = hg.shape[0]
    kf, kb = consts
    hd = HG_HEAD_DIM

    def col(g):
        return pl.BlockSpec((seq, hd), lambda b, h: (b, g * HG_HEADS + h))

    full3 = lambda a: pl.BlockSpec(a.shape, lambda b, h: (0, 0, 0))
    return pl.pallas_call(
        functools.partial(_hgrn_kernel, seq=seq),
        out_shape=jax.ShapeDtypeStruct((t, HG_WIDTH), BF16),
        grid=(batch, HG_HEADS),
        in_specs=[
            col(0), col(1), col(2), col(3), col(4),
            pl.BlockSpec((None, 2, hd), lambda b, h: (l, 0, h)),
            pl.BlockSpec((None, 1, hd), lambda b, h: (l, 0, 0)),
            full3(kf), full3(kb),
        ],
        out_specs=pl.BlockSpec((seq, hd), lambda b, h: (b, h)),
        scratch_shapes=[
            pltpu.VMEM((seq, hd), F32),
            pltpu.VMEM((seq, hd), F32),
            pltpu.VMEM((2, hd, hd), F32),
        ],
        compiler_params=_params(("arbitrary", "arbitrary")),
        name="hgrn2_scan",
    )(hg, hg, hg, hg, hg, lower, hg_norm3, kf, kb)


def _outproj_kernel(a_ref, o_ref, x_ref, mod_ref, n2_ref, w_ref, wr_ref, br_ref,
                    xo_ref, h2_ref, ti_ref, gt_ref, *, d):
    mod = mod_ref[...]
    y = _dot(a_ref[...], w_ref[0:ATT_WIDTH, :]) + _dot(o_ref[...], w_ref[ATT_WIDTH:ATT_WIDTH + HG_WIDTH, :])
    xn = x_ref[...] + mod[:, 2 * d:3 * d] * y
    xo_ref[...] = xn
    h2 = _rms_mod(xn, n2_ref[...], mod[:, 4 * d:5 * d], mod[:, 3 * d:4 * d])
    h2_ref[...] = _pack_bf16_pairs(h2)
    lg = _dot_f32_nt(wr_ref[...], h2) + br_ref[...]
    eidx = lax.broadcasted_iota(I32, lg.shape, 0)
    vals, idxs = [], []
    for _ in range(TOP_K):
        mx = jnp.max(lg, axis=0, keepdims=True)
        sel = jnp.min(jnp.where(lg == mx, eidx, N_EXPERTS), axis=0, keepdims=True)
        vals.append(mx)
        idxs.append(sel)
        lg = jnp.where(eidx == sel, -jnp.inf, lg)
    ex = [jnp.exp(v - vals[0]) for v in vals]
    den = ex[0]
    for e in ex[1:]:
        den = den + e
    inv = 1.0 / den
    ti_ref[...] = jnp.concatenate(idxs, axis=0)
    gt_ref[...] = jnp.concatenate([e * inv for e in ex], axis=0)


def _outproj(a, o, x2, mod3, norm2, w_out_bf, w_router_t, b_router3, l, seq):
    t, d = x2.shape
    tm = min(PROJ_TILE, seq)
    nb = mod3.shape[0] // norm2.shape[0]
    tok = lambda i: (i, 0)
    lane_tok = lambda i: (0, i)
    return pl.pallas_call(
        functools.partial(_outproj_kernel, d=d),
        out_shape=(
            jax.ShapeDtypeStruct((t, d), F32),
            jax.ShapeDtypeStruct((t, d // 2), U32),
            jax.ShapeDtypeStruct((TOP_K, t), I32),
            jax.ShapeDtypeStruct((TOP_K, t), F32),
        ),
        grid=(t // tm,),
        in_specs=[
            pl.BlockSpec((tm, ATT_WIDTH), tok),
            pl.BlockSpec((tm, HG_WIDTH), tok),
            pl.BlockSpec((tm, d), tok),
            pl.BlockSpec((None, 1, N_MOD * d), lambda i: (l * nb + (i * tm) // seq, 0, 0)),
            pl.BlockSpec((None, 1, d), lambda i: (l, 0, 0)),
            pl.BlockSpec((None, ATT_WIDTH + HG_WIDTH, d), lambda i: (l, 0, 0)),
            pl.BlockSpec((None, N_EXPERTS, d), lambda i: (l, 0, 0)),
            pl.BlockSpec((None, N_EXPERTS, 1), lambda i: (l, 0, 0)),
        ],
        out_specs=(
            pl.BlockSpec((tm, d), tok),
            pl.BlockSpec((tm, d // 2), tok),
            pl.BlockSpec((TOP_K, tm), lane_tok),
            pl.BlockSpec((TOP_K, tm), lane_tok),
        ),
        compiler_params=_params(("arbitrary",)),
        name="out_proj_router",
    )(a, o, x2, mod3, norm2, w_out_bf, w_router_t, b_router3)


def _rank_kernel(ti_ref, tri_ref, rank_ref, cnt_ref, carry_scr):
    @pl.when(pl.program_id(0) == 0)
    def _():
        carry_scr[...] = jnp.zeros(carry_scr.shape, F32)

    ti = ti_ref[...]
    tl = ti.shape[1]
    eidx = lax.broadcasted_iota(I32, (N_EXPERTS, tl), 0)
    carry = carry_scr[...]
    rows = []
    for k in range(TOP_K):
        oh = eidx == ti[k:k + 1, :]
        ohf = jnp.where(oh, 1.0, 0.0)
        pre = _dot(ohf.astype(BF16), tri_ref[...])
        rows.append(jnp.sum(jnp.where(oh, carry + pre, 0.0), axis=0, keepdims=True))
        carry = carry + jnp.sum(ohf, axis=1, keepdims=True)
    carry_scr[...] = carry
    rank_ref[...] = jnp.concatenate(rows, axis=0).astype(I32)
    cnt_ref[...] = jnp.broadcast_to(carry, cnt_ref.shape)


def _ranks(topi):
    k, t = topi.shape
    tl = min(RANK_TILE, t)
    tri = np.triu(np.ones((tl, tl), np.float32), 1)
    return pl.pallas_call(
        _rank_kernel,
        out_shape=(jax.ShapeDtypeStruct((k, t), I32), jax.ShapeDtypeStruct((N_EXPERTS, LANES), F32)),
        grid=(t // tl,),
        in_specs=[pl.BlockSpec((k, tl), lambda i: (0, i)), pl.BlockSpec((tl, tl), lambda i: (0, 0))],
        out_specs=(pl.BlockSpec((k, tl), lambda i: (0, i)), pl.BlockSpec((N_EXPERTS, LANES), lambda i: (0, 0))),
        scratch_shapes=[pltpu.VMEM((N_EXPERTS, 1), F32)],
        compiler_params=_params(("arbitrary",)),
        name="route_rank",
    )(topi, jnp.asarray(tri, BF16))


def _dest_kernel(cnt_ref, ltri_ref, ti_ref, rank_ref, dest_ref, eblk_ref):
    cnt = cnt_ref[...]
    nblk = jnp.floor((cnt + (MOE_ROWS - 1)) * (1.0 / MOE_ROWS))
    pstart_b = _dot(ltri_ref[...], nblk.astype(BF16))
    pstart = (pstart_b[:, 0:1] * MOE_ROWS).astype(I32)
    ti = ti_ref[...]
    tl = ti.shape[1]
    eidx = lax.broadcasted_iota(I32, (N_EXPERTS, tl), 0)
    rows = []
    for k in range(TOP_K):
        oh = eidx == ti[k:k + 1, :]
        rows.append(jnp.sum(jnp.where(oh, pstart, 0), axis=0, keepdims=True))
    dest_ref[...] = jnp.concatenate(rows, axis=0) + rank_ref[...]
    eblk_ref[0] = pstart_b.astype(I32)
    eblk_ref[1] = nblk.astype(I32)


def _destinations(counts, topi, rank):
    k, t = topi.shape
    tl = min(2048, t)
    ltri = np.tril(np.ones((N_EXPERTS, N_EXPERTS), np.float32), -1)
    return pl.pallas_call(
        _dest_kernel,
        out_shape=(
            jax.ShapeDtypeStruct((k, t), I32),
            jax.ShapeDtypeStruct((2, N_EXPERTS, LANES), I32),
        ),
        grid=(t // tl,),
        in_specs=[
            pl.BlockSpec((N_EXPERTS, LANES), lambda i: (0, 0)),
            pl.BlockSpec((N_EXPERTS, N_EXPERTS), lambda i: (0, 0)),
            pl.BlockSpec((k, tl), lambda i: (0, i)),
            pl.BlockSpec((k, tl), lambda i: (0, i)),
        ],
        out_specs=(
            pl.BlockSpec((k, tl), lambda i: (0, i)),
            pl.BlockSpec((2, N_EXPERTS, LANES), lambda i: (0, 0, 0)),
        ),
        compiler_params=_params(("arbitrary",)),
        name="route_dest",
    )(counts, jnp.asarray(ltri, BF16), topi, rank)


def _pack_bf16_pairs(x):
    n = x.shape[1] // 2
    lo = lax.bitcast_convert_type(x[:, :n].astype(BF16).astype(F32), U32)
    hi = lax.bitcast_convert_type(x[:, n:].astype(BF16).astype(F32), U32)
    return hi | (lo >> 16)


def _unpack_bf16_pairs(w):
    lo = lax.bitcast_convert_type(w << 16, F32)
    hi = lax.bitcast_convert_type(w & jnp.uint32(0xFFFF0000), F32)
    return lo, hi


SC_WINDOW = 128


def _sc_mesh():
    return plsc.VectorSubcoreMesh(core_axis_name="core", subcore_axis_name="subcore")


def _sc_worker(mesh):
    return lax.axis_index("core") * mesh.num_subcores + lax.axis_index("subcore"), mesh.num_cores * mesh.num_subcores


def _sc_scatter_rows(src, dest_rows, n_out):
    n, width = src.shape
    nk = dest_rows.shape[0]
    mesh = _sc_mesh()
    half = SC_WINDOW // 2

    @pl.kernel(out_type=jax.ShapeDtypeStruct((n_out, width), src.dtype), mesh=mesh,
               scratch_types=[pltpu.VMEM((nk, SC_WINDOW), I32), pltpu.VMEM((2, half, width), src.dtype),
                              pltpu.SemaphoreType.DMA((2,))],
               name="sc_scatter_rows")
    def scatter_kernel(src_hbm, idx_hbm, out_hbm, idx_v, rows_v, sem):
        wid, nw = _sc_worker(mesh)
        per = n // SC_WINDOW // nw
        row0 = wid * per * SC_WINDOW

        def load(s, buf):
            return pltpu.make_async_copy(src_hbm.at[pl.ds(row0 + s * half, half)], rows_v.at[buf], sem.at[buf])

        load(0, 0).start()

        @pl.loop(0, per)
        def _(j):
            pltpu.sync_copy(idx_hbm.at[:, pl.ds(row0 + j * SC_WINDOW, SC_WINDOW)], idx_v)
            for h in range(2):
                if h == 0:
                    load(2 * j + 1, 1).start()
                else:
                    @pl.when(j + 1 < per)
                    def _():
                        load(2 * j + 2, 0).start()

                load(2 * j + h, h).wait()
                for k in range(nk):
                    pltpu.sync_copy(rows_v.at[h], out_hbm.at[idx_v.at[k, pl.ds(h * half, half)]])

    return scatter_kernel(src, dest_rows)


def _sc_gather_rows(src, rows):
    n = rows.shape[0]
    width = src.shape[1]
    mesh = _sc_mesh()
    half = SC_WINDOW // 2

    @pl.kernel(out_type=jax.ShapeDtypeStruct((n, width), src.dtype), mesh=mesh,
               scratch_types=[pltpu.VMEM((1, SC_WINDOW), I32), pltpu.VMEM((2, half, width), src.dtype),
                              pltpu.SemaphoreType.DMA((2,))],
               name="sc_gather_rows")
    def gather_kernel(src_hbm, idx_hbm, out_hbm, idx_v, rows_v, sem):
        wid, nw = _sc_worker(mesh)
        per = n // SC_WINDOW // nw
        row0 = wid * per * SC_WINDOW

        def store(s, buf):
            return pltpu.make_async_copy(rows_v.at[buf], out_hbm.at[pl.ds(row0 + s * half, half)], sem.at[buf])

        @pl.loop(0, per)
        def _(j):
            pltpu.sync_copy(idx_hbm.at[:, pl.ds(row0 + j * SC_WINDOW, SC_WINDOW)], idx_v)
            for h in range(2):
                @pl.when(j >= 1)
                def _():
                    store(2 * j + h - 2, h).wait()

                pltpu.sync_copy(src_hbm.at[idx_v.at[0, pl.ds(h * half, half)]], rows_v.at[h])
                store(2 * j + h, h).start()

        for h in range(2):
            store(2 * per - 2 + h, h).wait()

    return gather_kernel(src, rows.reshape(1, n))


def _expert_group_kernel(eb_ref, xs_hbm, wgu_hbm, bgu_ref, wd_hbm, bd_ref, ys_hbm,
                         xbuf, ybuf, wgu_f, wd_f, wgu_bf, wd_bf, act_scr, xsem, ysem, wsem, *, dff, l):
    e = pl.program_id(0)
    ne = pl.num_programs(0)
    b0 = eb_ref[e]
    nb = eb_ref[ne + e]
    rows, half = xbuf.shape[1], xbuf.shape[2]
    group = rows // MOE_ROWS
    nch = (nb + group - 1) // group
    wslot = e % 2

    def w_copies(ex, slot):
        cps = []
        for src, dst, first in ((wgu_hbm, wgu_f, 0), (wd_hbm, wd_f, WEIGHT_CHUNKS)):
            step = dst.shape[1] // WEIGHT_CHUNKS
            for c in range(WEIGHT_CHUNKS):
                cps.append(pltpu.make_async_copy(src.at[l, ex, pl.ds(c * step, step), :],
                                                 dst.at[slot, pl.ds(c * step, step), :], wsem.at[slot, first + c]))
        return cps

    @pl.when(e == 0)
    def _():
        for c, cp in enumerate(w_copies(0, 0)):
            cp.start(priority=c % 2)

    @pl.when(e + 1 < ne)
    def _():
        for c, cp in enumerate(w_copies(e + 1, 1 - wslot)):
            cp.start(priority=c % 2)

    for cp in w_copies(e, wslot):
        cp.wait()

    def x_copy(j, slot):
        start = (b0 + j * group) * MOE_ROWS
        return pltpu.make_async_copy(xs_hbm.at[pl.ds(start, rows), :], xbuf.at[slot], xsem.at[slot])

    def y_copy(j, slot, g):
        start = (b0 + j * group + g) * MOE_ROWS
        return pltpu.make_async_copy(ybuf.at[slot, pl.ds(g * MOE_ROWS, MOE_ROWS), :],
                                     ys_hbm.at[pl.ds(start, MOE_ROWS), :], ysem.at[slot, g])

    def y_each(j, slot, fn):
        for g in range(group):
            @pl.when(j * group + g < nb)
            def _():
                fn(y_copy(j, slot, g))

    @pl.when(nb > 0)
    def _():
        x_copy(0, 0).start(priority=1)
        wgu_bf[:, 0:2 * dff] = wgu_f[wslot].astype(BF16)
        wd_bf[:, 0:2 * half] = wd_f[wslot].astype(BF16)

        def chunk(j, carry):
            slot = j % 2

            @pl.when(j + 1 < nch)
            def _():
                x_copy(j + 1, 1 - slot).start(priority=1)

            x_copy(j, slot).wait()

            @pl.when(j >= 2)
            def _():
                y_each(j - 2, slot, lambda cp: cp.wait())

            lo, hi = _unpack_bf16_pairs(xbuf[slot])
            xb = jnp.concatenate([lo.astype(BF16), hi.astype(BF16)], axis=1)
            for c0 in range(0, dff, EXPERT_COLS):
                gate = _dot(xb, wgu_bf[:, c0:c0 + EXPERT_COLS]) + bgu_ref[:, c0:c0 + EXPERT_COLS]
                up = (_dot(xb, wgu_bf[:, dff + c0:dff + c0 + EXPERT_COLS])
                      + bgu_ref[:, dff + c0:dff + c0 + EXPERT_COLS])
                glu = jnp.minimum(gate, SWIGLU_LIMIT)
                lin = jnp.clip(up, -SWIGLU_LIMIT, SWIGLU_LIMIT)
                act_scr[:, c0:c0 + EXPERT_COLS] = (glu * _sigmoid(SWIGLU_ALPHA * glu) * (lin + 1.0)).astype(BF16)
            ybuf[slot] = _pack_bf16_pairs(_dot(act_scr[...], wd_bf[:, 0:2 * half]) + bd_ref[...])
            y_each(j, slot, lambda cp: cp.start(priority=1))
            return carry

        lax.fori_loop(0, nch, chunk, 0)

        @pl.when(nch >= 2)
        def _():
            y_each(nch - 2, nch % 2, lambda cp: cp.wait())

        y_each(nch - 1, (nch - 1) % 2, lambda cp: cp.wait())


def _block_experts(eblk, xs, w_gu, b_gu4, w_down, b_down4, l):
    cap, half = xs.shape
    d = 2 * half
    dff = w_down.shape[2]
    wsel = lambda e, eb: (l, e, 0, 0)
    grid_spec = pltpu.PrefetchScalarGridSpec(
        num_scalar_prefetch=1,
        grid=(N_EXPERTS,),
        in_specs=[
            pl.BlockSpec(memory_space=pl.ANY),
            pl.BlockSpec(memory_space=pl.ANY),
            pl.BlockSpec((None, None, 1, 2 * dff), wsel),
            pl.BlockSpec(memory_space=pl.ANY),
            pl.BlockSpec((None, None, 1, d), wsel),
        ],
        out_specs=pl.BlockSpec(memory_space=pl.ANY),
        scratch_shapes=[
            pltpu.VMEM((2, EXPERT_GROUP * MOE_ROWS, half), U32),
            pltpu.VMEM((2, EXPERT_GROUP * MOE_ROWS, half), U32),
            pltpu.VMEM((2, d, 2 * dff), F32),
            pltpu.VMEM((2, dff, d), F32),
            pltpu.VMEM((d, 2 * dff + LANES), BF16),
            pltpu.VMEM((dff, d + LANES), BF16),
            pltpu.VMEM((EXPERT_GROUP * MOE_ROWS, dff), BF16),
            pltpu.SemaphoreType.DMA((2,)),
            pltpu.SemaphoreType.DMA((2, EXPERT_GROUP)),
            pltpu.SemaphoreType.DMA((2, 2 * WEIGHT_CHUNKS)),
        ],
    )
    return pl.pallas_call(
        functools.partial(_expert_group_kernel, dff=dff, l=l),
        out_shape=jax.ShapeDtypeStruct((cap - (EXPERT_GROUP - 1) * MOE_ROWS, half), U32),
        grid_spec=grid_spec,
        compiler_params=_params(("arbitrary",)),
        name="moe_expert_groups",
    )(eblk, xs, w_gu, b_gu4, w_down, b_down4)


def _combine_kernel(*refs, d, final):
    y_refs = refs[:TOP_K]
    gt_ref, x_ref, mod_ref = refs[TOP_K:TOP_K + 3]
    xo_ref = refs[-1]
    half = d // 2
    gt = gt_ref[...]
    acc_lo = acc_hi = None
    for k in range(TOP_K):
        lo, hi = _unpack_bf16_pairs(y_refs[k][...])
        g = gt[:, k:k + 1]
        acc_lo = g * lo if acc_lo is None else acc_lo + g * lo
        acc_hi = g * hi if acc_hi is None else acc_hi + g * hi
    x_lo = x_ref[:, 0:half] + mod_ref[:, 5 * d:5 * d + half] * acc_lo
    x_hi = x_ref[:, half:d] + mod_ref[:, 5 * d + half:6 * d] * acc_hi
    if final:
        gain = refs[TOP_K + 3][...]
        ssq = jnp.sum(x_lo * x_lo, axis=-1, keepdims=True) + jnp.sum(x_hi * x_hi, axis=-1, keepdims=True)
        inv = lax.rsqrt(ssq * (1.0 / d) + EPS)
        x_lo = x_lo * inv * gain[:, 0:half]
        x_hi = x_hi * inv * gain[:, half:d]
    xo_ref[:, 0:half] = x_lo
    xo_ref[:, half:d] = x_hi


def _combine(y4p, gates_t, x2, mod3, l, seq, depth, row0, prev, final_gain=None):
    t, d = x2.shape
    n = gates_t.shape[0]
    tm = min(PROJ_TILE, seq)
    nb = mod3.shape[0] // depth
    nt = n // tm
    t0 = row0 // tm
    final = final_gain is not None

    def slot(k):
        return pl.BlockSpec((tm, d // 2), lambda i: (k * nt + i, 0))

    specs = [slot(k) for k in range(TOP_K)] + [
        pl.BlockSpec((tm, TOP_K), lambda i: (i, 0)),
        pl.BlockSpec((tm, d), lambda i: (t0 + i, 0)),
        pl.BlockSpec((None, 1, N_MOD * d), lambda i: (l * nb + (row0 + i * tm) // seq, 0, 0)),
    ]
    args = [y4p] * TOP_K + [gates_t, x2, mod3]
    if final:
        specs.append(pl.BlockSpec((1, d), lambda i: (0, 0)))
        args.append(final_gain.reshape(1, d))
    aliases = {}
    if prev is not None:
        specs.append(pl.BlockSpec(memory_space=pl.ANY))
        aliases = {len(args): 0}
        args.append(prev)
    return pl.pallas_call(
        functools.partial(_combine_kernel, d=d, final=final),
        out_shape=jax.ShapeDtypeStruct((t, d), F32),
        grid=(nt,),
        in_specs=specs,
        out_specs=pl.BlockSpec((tm, d), lambda i: (t0 + i, 0)),
        input_output_aliases=aliases,
        compiler_params=_params(("arbitrary",)),
        name="moe_combine",
    )(*args)


def _mixer_layer(x2, mod3, tables, lower, consts, p, l, batch, seq):
    ra, rb, rc = tables
    qa, ka2, va2, hg = _inproj(x2, mod3, p["norm1"], p["w_in"], ra, rb, rc, l, seq)
    a = _attention(qa, ka2, va2, p["attn_sink"], p["attn_norm"], l, batch, seq)
    o = _hgrn(hg, lower, p["hg_norm"], consts, l, batch, seq)
    return a, o


def _moe_layer(a, o, x2, mod3, p, l, seq, depth, final_gain=None):
    t, d = x2.shape
    xn, h2p, topi, gates = _outproj(a, o, x2, mod3, p["norm2"], p["w_out"], p["w_router_t"], p["b_router"], l, seq)
    rank, counts = _ranks(topi)
    nblocks = (t * TOP_K + N_EXPERTS * MOE_ROWS) // MOE_ROWS
    dest, eblk = _destinations(counts, topi, rank)
    xs = _sc_scatter_rows(h2p, dest, (nblocks + EXPERT_GROUP - 1) * MOE_ROWS)
    ys = _block_experts(eblk[:, :, 0].reshape(-1), xs, p["w_gu"], p["b_gu"], p["w_down"], p["b_down"], l)
    gates_t = gates.T
    out = None
    for part in range(MOE_COMBINE_PARTS):
        rows = slice(part * (t // MOE_COMBINE_PARTS), (part + 1) * (t // MOE_COMBINE_PARTS))
        y4p = _sc_gather_rows(ys, dest[:, rows].reshape(-1))
        out = _combine(y4p, gates_t[rows], xn, mod3, l, seq, depth, rows.start, out, final_gain)
    return out


def kernel(x, c, positions, w_ada, b_ada, norm1, w_in, attn_sink, attn_norm, hg_lb_logits, hg_norm, w_out, norm2,
           w_router, b_router, w_gu, b_gu, w_down, b_down, final_norm):
    batch, seq, d = x.shape
    depth = w_ada.shape[0]
    t = batch * seq
    p = {
        "norm1": norm1.reshape(depth, 1, d),
        "w_in": w_in.astype(BF16),
        "attn_sink": attn_sink.astype(F32),
        "attn_norm": attn_norm.reshape(depth, 1, ATT_WIDTH),
        "hg_norm": hg_norm.reshape(depth, 1, HG_HEAD_DIM),
        "w_out": w_out.astype(BF16),
        "norm2": norm2.reshape(depth, 1, d),
        "w_router_t": jnp.swapaxes(w_router, 1, 2),
        "b_router": b_router.reshape(depth, N_EXPERTS, 1),
        "w_gu": w_gu,
        "b_gu": b_gu.reshape(depth, N_EXPERTS, 1, b_gu.shape[-1]),
        "w_down": w_down,
        "b_down": b_down.reshape(depth, N_EXPERTS, 1, d),
    }
    mod3 = _ada_all(c, w_ada, b_ada).reshape(depth * batch, 1, N_MOD * d)
    lower = _lower_bounds(hg_lb_logits)
    tables = _rope_tables(positions)
    consts = tuple(jnp.asarray(m) for m in _hgrn_constants())
    x2 = x.reshape(t, d)
    for l in range(depth):
        a, o = _mixer_layer(x2, mod3, tables, lower, consts, p, l, batch, seq)
        x2 = _moe_layer(a, o, x2, mod3, p, l, seq, depth, final_norm if l == depth - 1 else None)
    return x2.reshape(batch, seq, d)
```

```python
import functools

import numpy as np
import jax
import jax.numpy as jnp
from jax import lax
from jax.experimental import pallas as pl
from jax.experimental.pallas import tpu as pltpu
from jax.experimental.pallas import tpu_sc as plsc

F32 = jnp.float32
BF16 = jnp.bfloat16
I32 = jnp.int32
U32 = jnp.uint32

ATT_HEADS = 8
ATT_KV_HEADS = 2
ATT_HEAD_DIM = 64
ATT_WIDTH = ATT_HEADS * ATT_HEAD_DIM
KV_WIDTH = ATT_KV_HEADS * ATT_HEAD_DIM
WINDOW = 128
ATT_BLOCK = 128
ROPE_THETA = 500000.0
ROPE_DIM = ATT_HEAD_DIM // 4
HG_HEADS = 4
HG_HEAD_DIM = 128
HG_WIDTH = HG_HEADS * HG_HEAD_DIM
N_EXPERTS = 32
TOP_K = 4
SWIGLU_ALPHA = 1.702
SWIGLU_LIMIT = 7.0
N_MOD = 6
EPS = 1e-6
NEG_INF = -1e30
LB_FLOOR = 1e-30

LANES = 128
HG_CHUNK = 64
HG_LEVELS = 6
HG_UNROLL = 8
MOE_ROWS = 256
EXPERT_GROUP = 1
WEIGHT_CHUNKS = 4
EXPERT_COLS = 512
MOE_COMBINE_PARTS = 2
PROJ_TILE = 512
RANK_TILE = 512
VMEM_LIMIT = 56 * 1024 * 1024


def _dot(a, b):
    return jnp.dot(a, b, preferred_element_type=F32)


def _dot_nt(a, b):
    return lax.dot_general(a, b, (((1,), (1,)), ((), ())), preferred_element_type=F32)


def _dot_tn(a, b):
    return lax.dot_general(a, b, (((0,), (0,)), ((), ())), preferred_element_type=F32)


def _split3(x):
    hi = x.astype(BF16)
    r1 = x - hi.astype(F32)
    mid = r1.astype(BF16)
    lo = (r1 - mid.astype(F32)).astype(BF16)
    return hi, mid, lo


def _dot_f32_nt(a, b):
    ah, am, _ = _split3(a)
    bh, bm, _ = _split3(b)
    return _dot_nt(ah, bh) + _dot_nt(ah, bm) + _dot_nt(am, bh)


def _dot_f32(a, b):
    ah, am, _ = _split3(a)
    bh, bm, _ = _split3(b)
    return _dot(ah, bh) + _dot(ah, bm) + _dot(am, bh)


def _sigmoid(x):
    return 1.0 / (1.0 + jnp.exp(-x))


def _params(sem=None):
    return pltpu.CompilerParams(dimension_semantics=sem, vmem_limit_bytes=VMEM_LIMIT)


def _ada_kernel(c_ref, w_ref, b_ref, o_ref):
    c = c_ref[...]
    cond = c * _sigmoid(c)
    o_ref[...] = _dot_f32(cond, w_ref[...]) + b_ref[...]


def _ada_all(c, w_ada, b_ada):
    depth, d, n = w_ada.shape
    b = c.shape[0]
    nt = n // d
    return pl.pallas_call(
        _ada_kernel,
        out_shape=jax.ShapeDtypeStruct((depth, b, n), F32),
        grid=(depth, nt),
        in_specs=[
            pl.BlockSpec((b, d), lambda l, j: (0, 0)),
            pl.BlockSpec((None, d, d), lambda l, j: (l, 0, j)),
            pl.BlockSpec((None, 1, d), lambda l, j: (l, 0, j)),
        ],
        out_specs=pl.BlockSpec((None, b, d), lambda l, j: (l, 0, j)),
        compiler_params=_params(("arbitrary", "arbitrary")),
        name="ada_mod",
    )(c, w_ada, b_ada.reshape(depth, 1, n))


def _lb_kernel(x_ref, o_ref):
    depth = x_ref.shape[0]
    xs = [x_ref[l] for l in range(depth)]
    m = xs[0]
    for l in range(1, depth):
        m = jnp.maximum(m, xs[l])
    es = [jnp.exp(v - m) for v in xs]
    den = es[0]
    for l in range(1, depth):
        den = den + es[l]
    ps = [e / den for e in es]
    run = ps[0]
    o_ref[0] = run - ps[0]
    for l in range(1, depth):
        run = run + ps[l]
        o_ref[l] = run - ps[0]


def _lower_bounds(hg_lb_logits):
    return pl.pallas_call(
        _lb_kernel,
        out_shape=jax.ShapeDtypeStruct(hg_lb_logits.shape, F32),
        name="hg_lower_bounds",
    )(hg_lb_logits.astype(F32))


def _rope_kernel(pos_ref, invf_ref, a_ref, b_ref, c_ref):
    pos = pos_ref[...].astype(F32)
    ang = pos * invf_ref[...]
    cs = jnp.cos(ang)
    sn = jnp.sin(ang)
    lane = lax.broadcasted_iota(I32, ang.shape, 1) & (ATT_HEAD_DIM - 1)
    half = ROPE_DIM // 2
    first = lane < half
    second = (lane >= half) & (lane < ROPE_DIM)
    a_ref[...] = jnp.where(first | second, cs, 1.0)
    b_ref[...] = jnp.where(first, -sn, 0.0)
    c_ref[...] = jnp.where(second, sn, 0.0)


def _rope_tables(positions):
    t = positions.size
    half = ROPE_DIM // 2
    inv = (np.float32(ROPE_THETA) ** (-(np.arange(half, dtype=np.float32) * np.float32(2.0) / np.float32(ROPE_DIM)))).astype(np.float32)
    lane = np.arange(LANES) % ATT_HEAD_DIM
    pat = np.where(lane < ROPE_DIM, inv[lane % half], 0.0).astype(np.float32).reshape(1, LANES)
    tm = min(t, 2048)
    shp = jax.ShapeDtypeStruct((t, LANES), F32)
    spec = pl.BlockSpec((tm, LANES), lambda i: (i, 0))
    return pl.pallas_call(
        _rope_kernel,
        out_shape=(shp, shp, shp),
        grid=(t // tm,),
        in_specs=[pl.BlockSpec((tm, 1), lambda i: (i, 0)), pl.BlockSpec((1, LANES), lambda i: (0, 0))],
        out_specs=(spec, spec, spec),
        compiler_params=_params(("arbitrary",)),
        name="rope_tables",
    )(positions.reshape(t, 1).astype(I32), jnp.asarray(pat))


def _rms_mod(x, gain, scale, shift):
    ms = jnp.mean(x * x, axis=-1, keepdims=True)
    return (x * lax.rsqrt(ms + EPS) * gain) * (1.0 + scale) + shift


def _rope_apply(x, a, b, c):
    half = ROPE_DIM // 2
    return x * a + pltpu.roll(x, LANES - half, 1) * b + pltpu.roll(x, half, 1) * c


def _inproj_kernel(x_ref, mod_ref, n1_ref, w_ref, ra_ref, rb_ref, rc_ref, qa_ref, ka_ref, va_ref, hg_ref, *, d):
    mod = mod_ref[...]
    h = _rms_mod(x_ref[...], n1_ref[...], mod[:, d:2 * d], mod[:, 0:d]).astype(BF16)
    a, b, c = ra_ref[...], rb_ref[...], rc_ref[...]
    kvw = ATT_WIDTH + 2 * KV_WIDTH
    pa = _dot(h, w_ref[:, 0:kvw])
    scale = ATT_HEAD_DIM ** -0.5
    for g in range(ATT_WIDTH // LANES):
        qg = _rope_apply(pa[:, g * LANES:(g + 1) * LANES], a, b, c)
        qa_ref[:, g * LANES:(g + 1) * LANES] = (qg * scale).astype(BF16)
    k = _rope_apply(pa[:, ATT_WIDTH:ATT_WIDTH + KV_WIDTH], a, b, c)
    v = pa[:, ATT_WIDTH + KV_WIDTH:kvw]
    ka_ref[:, 0:LANES] = k.astype(BF16)
    ka_ref[:, LANES:2 * LANES] = pltpu.roll(k, ATT_HEAD_DIM, 1).astype(BF16)
    va_ref[:, 0:LANES] = v.astype(BF16)
    va_ref[:, LANES:2 * LANES] = pltpu.roll(v, ATT_HEAD_DIM, 1).astype(BF16)
    for g in range(5):
        lo = kvw + g * HG_WIDTH
        hg_ref[:, g * HG_WIDTH:(g + 1) * HG_WIDTH] = _dot(h, w_ref[:, lo:lo + HG_WIDTH])


def _inproj(x2, mod3, norm1, w_in_bf, ra, rb, rc, l, seq):
    t, d = x2.shape
    n_in = w_in_bf.shape[-1]
    tm = min(PROJ_TILE, seq)
    nb = mod3.shape[0] // norm1.shape[0]
    tok = lambda i: (i, 0)
    return pl.pallas_call(
        functools.partial(_inproj_kernel, d=d),
        out_shape=(
            jax.ShapeDtypeStruct((t, ATT_WIDTH), BF16),
            jax.ShapeDtypeStruct((t, 2 * KV_WIDTH), BF16),
            jax.ShapeDtypeStruct((t, 2 * KV_WIDTH), BF16),
            jax.ShapeDtypeStruct((t, 5 * HG_WIDTH), F32),
        ),
        grid=(t // tm,),
        in_specs=[
            pl.BlockSpec((tm, d), tok),
            pl.BlockSpec((None, 1, N_MOD * d), lambda i: (l * nb + (i * tm) // seq, 0, 0)),
            pl.BlockSpec((None, 1, d), lambda i: (l, 0, 0)),
            pl.BlockSpec((None, d, n_in), lambda i: (l, 0, 0)),
            pl.BlockSpec((tm, LANES), tok),
            pl.BlockSpec((tm, LANES), tok),
            pl.BlockSpec((tm, LANES), tok),
        ],
        out_specs=(
            pl.BlockSpec((tm, ATT_WIDTH), tok),
            pl.BlockSpec((tm, 2 * KV_WIDTH), tok),
            pl.BlockSpec((tm, 2 * KV_WIDTH), tok),
            pl.BlockSpec((tm, 5 * HG_WIDTH), tok),
        ),
        compiler_params=_params(("arbitrary",)),
        name="in_proj",
    )(x2, mod3, norm1, w_in_bf, ra, rb, rc)


def _attn_kernel(sink_ref, q_ref, kp_ref, kc_ref, kn_ref, vp_ref, vc_ref, vn_ref, gain_ref, o_ref, *, l, seq):
    n = pl.program_id(1)
    blk = ATT_BLOCK
    k2 = jnp.concatenate([kp_ref[...], kc_ref[...], kn_ref[...]], axis=0)
    v2 = jnp.concatenate([vp_ref[...], vc_ref[...], vn_ref[...]], axis=0)
    lane = lax.broadcasted_iota(I32, (3 * blk, LANES), 1)
    lo_half = lane < ATT_HEAD_DIM
    zero = jnp.zeros((3 * blk, LANES), BF16)
    ka, kb = k2[:, 0:LANES], k2[:, LANES:2 * LANES]
    va, vb = v2[:, 0:LANES], v2[:, LANES:2 * LANES]
    kz = [[jnp.where(lo_half, ka, zero), jnp.where(lo_half, zero, kb)],
          [jnp.where(lo_half, kb, zero), jnp.where(lo_half, zero, ka)]]
    vz = [[jnp.where(lo_half, va, zero), jnp.where(lo_half, zero, vb)],
          [jnp.where(lo_half, vb, zero), jnp.where(lo_half, zero, va)]]
    qpos = n * blk + lax.broadcasted_iota(I32, (blk, 3 * blk), 0)
    kpos = (n - 1) * blk + lax.broadcasted_iota(I32, (blk, 3 * blk), 1)
    valid = (jnp.abs(qpos - kpos) <= WINDOW) & (kpos >= 0) & (kpos < seq)
    valid2 = jnp.concatenate([valid, valid], axis=0)
    upper = lax.broadcasted_iota(I32, (2 * blk, 1), 0) < blk
    outs = []
    for j in range(ATT_KV_HEADS):
        qs = jnp.concatenate([q_ref[:, 2 * j * LANES:(2 * j + 1) * LANES],
                              q_ref[:, (2 * j + 1) * LANES:(2 * j + 2) * LANES]], axis=0)
        acc = None
        for half in range(2):
            s = _dot_nt(qs, kz[j][half])
            s = jnp.where(valid2, s, NEG_INF)
            sink = jnp.where(upper, sink_ref[l, 4 * j + half], sink_ref[l, 4 * j + 2 + half])
            mx = jnp.maximum(jnp.max(s, axis=-1, keepdims=True), sink)
            p = jnp.exp(s - mx)
            den = jnp.sum(p, axis=-1, keepdims=True) + jnp.exp(sink - mx)
            p = (p * (1.0 / den)).astype(BF16)
            pv = _dot(p, vz[j][half])
            acc = pv if acc is None else acc + pv
        outs.append(acc[0:blk])
        outs.append(acc[blk:2 * blk])
    o = jnp.concatenate(outs, axis=-1)
    ms = jnp.mean(o * o, axis=-1, keepdims=True)
    o_ref[...] = (o * lax.rsqrt(ms + EPS) * gain_ref[...]).astype(BF16)


def _attention(qa, ka2, va2, attn_sink, attn_norm3, l, batch, seq):
    t = qa.shape[0]
    blk = ATT_BLOCK
    nb = seq // blk
    cur = lambda b, n: (b * nb + n, 0)
    prev = lambda b, n: (b * nb + jnp.maximum(n - 1, 0), 0)
    nxt = lambda b, n: (b * nb + jnp.minimum(n + 1, nb - 1), 0)
    kvspec = lambda f: pl.BlockSpec((blk, 2 * KV_WIDTH), f)
    return pl.pallas_call(
        functools.partial(_attn_kernel, l=l, seq=seq),
        out_shape=jax.ShapeDtypeStruct((t, ATT_WIDTH), BF16),
        grid=(batch, nb),
        in_specs=[
            pl.BlockSpec(memory_space=pltpu.SMEM),
            pl.BlockSpec((blk, ATT_WIDTH), cur),
            kvspec(prev), kvspec(cur), kvspec(nxt),
            kvspec(prev), kvspec(cur), kvspec(nxt),
            pl.BlockSpec((None, 1, ATT_WIDTH), lambda b, n: (l, 0, 0)),
        ],
        out_specs=pl.BlockSpec((blk, ATT_WIDTH), cur),
        compiler_params=_params(("arbitrary", "arbitrary")),
        name="window_attn",
    )(attn_sink, qa, ka2, ka2, ka2, va2, va2, va2, attn_norm3)


def _hgrn_constants():
    c, nl = HG_CHUNK, HG_LEVELS
    r = np.arange(c)
    masks = []
    for lev in range(nl):
        m = 1 << lev
        parent = r // (2 * m)
        upper = r >= parent * 2 * m + m
        masks.append((parent[:, None] == parent[None, :]) & upper[:, None] & (~upper)[None, :])
    masks.append(np.eye(c, dtype=bool))
    kf = np.stack(masks).astype(np.float32)
    kb = np.stack([mk[::-1, ::-1] for mk in masks]).astype(np.float32)
    return kf, kb


def _chunk_decays(logf, reverse):
    c = logf.shape[0]
    sub = 8
    nv = c // sub
    row = lax.broadcasted_iota(I32, (sub, LANES), 0)
    parts = []
    for v in range(nv):
        x = logf[sub * v:sub * (v + 1), :]
        for s in (1, 2, 4):
            if reverse:
                x = x + jnp.where(row < sub - s, pltpu.roll(x, sub - s, 0), 0.0)
            else:
                x = x + jnp.where(row >= s, pltpu.roll(x, s, 0), 0.0)
        parts.append(x)
    b = [None] * nv
    order = list(reversed(range(nv))) if reverse else list(range(nv))
    edge = 0 if reverse else sub - 1
    carry = None
    for v in order:
        b[v] = parts[v] if carry is None else parts[v] + carry
        carry = b[v][edge:edge + 1, :]
    b_last = carry

    def anchor_row(v, r):
        return jnp.broadcast_to(b[v][r:r + 1, :], (sub, LANES))

    odd = (row & 1) == 1
    levels = []
    for lev in range(HG_LEVELS):
        m = 1 << lev
        pieces = []
        for v in range(nv):
            if m == 1:
                a = jnp.where(odd, pltpu.roll(b[v], 1, 0), b[v]) if reverse else jnp.where(odd, b[v], pltpu.roll(b[v], sub - 1, 0))
            elif m == 2:
                lo, hi = (1, 5) if reverse else (2, 6)
                a = jnp.where(row < 4, anchor_row(v, lo), anchor_row(v, hi))
            elif m == 4:
                a = anchor_row(v, 3 if reverse else 4)
            else:
                mv = m // sub
                first = (v // (2 * mv)) * 2 * mv
                a = anchor_row(first + mv - 1, sub - 1) if reverse else anchor_row(first + mv, 0)
            pieces.append(jnp.exp(-jnp.abs(b[v] - a)))
        levels.append(jnp.concatenate(pieces, axis=0))
    eb = jnp.concatenate([jnp.exp(bv) for bv in b], axis=0)
    erem = jnp.concatenate([jnp.exp(b_last - bv) for bv in b], axis=0)
    return levels, eb, erem


def _hgrn_kernel(q_ref, ff_ref, fb_ref, i_ref, g_ref, lb_ref, gn_ref, kf_ref, kb_ref,
                 o_ref, of_scr, ob_scr, st_scr, *, seq):
    c, nl = HG_CHUNK, HG_LEVELS
    nc = seq // c
    lb = lb_ref[...]
    st_scr[...] = jnp.zeros(st_scr.shape, F32)

    unroll = HG_UNROLL if nc % HG_UNROLL == 0 else 1
    dirs = (
        dict(f_ref=ff_ref, lbrow=lb[0:1, :], k_ref=kf_ref, o_scr=of_scr, last_row=c - 1, d=0),
        dict(f_ref=fb_ref, lbrow=lb[1:2, :], k_ref=kb_ref, o_scr=ob_scr, last_row=0, d=1),
    )

    def body(i, carry):
        work = []
        for u in range(unroll):
            cf = i * unroll + u
            work.append((dirs[0], pl.ds(pl.multiple_of(cf * c, c), c)))
            work.append((dirs[1], pl.ds(pl.multiple_of((nc - 1 - cf) * c, c), c)))
        gates = []
        for dr, rows in work:
            lbf = jnp.maximum(dr["lbrow"], LB_FLOOR)
            oml = 1.0 - dr["lbrow"]
            f = dr["f_ref"][rows, :]
            e = jnp.exp(-jnp.abs(f))
            r = 1.0 / (1.0 + e)
            er = e * r
            pos = f >= 0.0
            logf = jnp.log(lbf + oml * jnp.where(pos, r, er))
            kk = oml * jnp.where(pos, er, r)
            qh = q_ref[rows, :]
            gates.append((logf, kk.astype(BF16), (qh * _sigmoid(qh)).astype(BF16), i_ref[rows, :].astype(BF16)))
        decays = [_chunk_decays(g[0], dr["d"] == 1) for (dr, _), g in zip(work, gates)]
        amat = [dr["k_ref"][nl] * _dot_nt(g[2], g[1]) for (dr, _), g in zip(work, gates)]
        for lev in range(nl):
            for j, ((dr, _), g) in enumerate(zip(work, gates)):
                gl = decays[j][0][lev].astype(BF16)
                amat[j] = amat[j] + dr["k_ref"][lev] * _dot_nt(g[2] * gl, g[1] * gl)
        intra = [_dot(amat[j].astype(BF16), g[3]) for j, g in enumerate(gates)]
        upd = [_dot_tn(g[3], g[1] * decays[j][2].astype(BF16)) for j, g in enumerate(gates)]
        st = [st_scr[0], st_scr[1]]
        for j, ((dr, rows), g) in enumerate(zip(work, gates)):
            eb = decays[j][1]
            d = dr["d"]
            dr["o_scr"][rows, :] = _dot_nt(g[2] * eb.astype(BF16), st[d].astype(BF16)) + intra[j]
            st[d] = st[d] * eb[dr["last_row"]:dr["last_row"] + 1, :] + upd[j]
        st_scr[0] = st[0]
        st_scr[1] = st[1]
        return carry

    lax.fori_loop(0, nc // unroll, body, 0)

    ep = min(256, seq)
    gn = gn_ref[...]

    def epilogue(j, carry):
        rows = pl.ds(pl.multiple_of(j * ep, ep), ep)
        o = of_scr[rows, :] + ob_scr[rows, :]
        y = o * lax.rsqrt(jnp.mean(o * o, axis=-1, keepdims=True) + EPS) * gn
        g = g_ref[rows, :]
        o_ref[rows, :] = (y * (g * _sigmoid(g))).astype(BF16)
        return carry

    lax.fori_loop(0, seq // ep, epilogue, 0)


def _hgrn(hg, lower, hg_norm3, consts, l, batch, seq):
    t = hg.shape[0]
    kf, kb = consts
    hd = HG_HEAD_DIM

    def col(g):
        return pl.BlockSpec((seq, hd), lambda b, h: (b, g * HG_HEADS + h))

    full3 = lambda a: pl.BlockSpec(a.shape, lambda b, h: (0, 0, 0))
    return pl.pallas_call(
        functools.partial(_hgrn_kernel, seq=seq),
        out_shape=jax.ShapeDtypeStruct((t, HG_WIDTH), BF16),
        grid=(batch, HG_HEADS),
        in_specs=[
            col(0), col(1), col(2), col(3), col(4),
            pl.BlockSpec((None, 2, hd), lambda b, h: (l, 0, h)),
            pl.BlockSpec((None, 1, hd), lambda b, h: (l, 0, 0)),
            full3(kf), full3(kb),
        ],
        out_specs=pl.BlockSpec((seq, hd), lambda b, h: (b, h)),
        scratch_shapes=[
            pltpu.VMEM((seq, hd), F32),
            pltpu.VMEM((seq, hd), F32),
            pltpu.VMEM((2, hd, hd), F32),
        ],
        compiler_params=_params(("arbitrary", "arbitrary")),
        name="hgrn2_scan",
    )(hg, hg, hg, hg, hg, lower, hg_norm3, kf, kb)


def _outproj_kernel(a_ref, o_ref, x_ref, mod_ref, n2_ref, w_ref, wr_ref, br_ref,
                    xo_ref, h2_ref, ti_ref, gt_ref, w_pad, *, d):
    @pl.when(pl.program_id(0) == 0)
    def _():
        w_pad[:, 0:d] = w_ref[...]

    mod = mod_ref[...]
    y = (_dot(a_ref[...], w_pad[0:ATT_WIDTH, 0:d])
         + _dot(o_ref[...], w_pad[ATT_WIDTH:ATT_WIDTH + HG_WIDTH, 0:d]))
    xn = x_ref[...] + mod[:, 2 * d:3 * d] * y
    xo_ref[...] = xn
    h2 = _rms_mod(xn, n2_ref[...], mod[:, 4 * d:5 * d], mod[:, 3 * d:4 * d])
    h2_ref[...] = _pack_bf16_pairs(h2)
    lg = _dot_f32_nt(wr_ref[...], h2) + br_ref[...]
    eidx = lax.broadcasted_iota(I32, lg.shape, 0)
    vals, idxs = [], []
    for _ in range(TOP_K):
        mx = jnp.max(lg, axis=0, keepdims=True)
        sel = jnp.min(jnp.where(lg == mx, eidx, N_EXPERTS), axis=0, keepdims=True)
        vals.append(mx)
        idxs.append(sel)
        lg = jnp.where(eidx == sel, -jnp.inf, lg)
    ex = [jnp.exp(v - vals[0]) for v in vals]
    den = ex[0]
    for e in ex[1:]:
        den = den + e
    inv = 1.0 / den
    ti_ref[...] = jnp.concatenate(idxs, axis=0)
    gt_ref[...] = jnp.concatenate([e * inv for e in ex], axis=0)


def _outproj(a, o, x2, mod3, norm2, w_out_bf, w_router_t, b_router3, l, seq):
    t, d = x2.shape
    tm = min(PROJ_TILE, seq)
    nb = mod3.shape[0] // norm2.shape[0]
    tok = lambda i: (i, 0)
    lane_tok = lambda i: (0, i)
    return pl.pallas_call(
        functools.partial(_outproj_kernel, d=d),
        out_shape=(
            jax.ShapeDtypeStruct((t, d), F32),
            jax.ShapeDtypeStruct((t, d // 2), U32),
            jax.ShapeDtypeStruct((TOP_K, t), I32),
            jax.ShapeDtypeStruct((TOP_K, t), F32),
        ),
        grid=(t // tm,),
        in_specs=[
            pl.BlockSpec((tm, ATT_WIDTH), tok),
            pl.BlockSpec((tm, HG_WIDTH), tok),
            pl.BlockSpec((tm, d), tok),
            pl.BlockSpec((None, 1, N_MOD * d), lambda i: (l * nb + (i * tm) // seq, 0, 0)),
            pl.BlockSpec((None, 1, d), lambda i: (l, 0, 0)),
            pl.BlockSpec((None, ATT_WIDTH + HG_WIDTH, d), lambda i: (l, 0, 0)),
            pl.BlockSpec((None, N_EXPERTS, d), lambda i: (l, 0, 0)),
            pl.BlockSpec((None, N_EXPERTS, 1), lambda i: (l, 0, 0)),
        ],
        out_specs=(
            pl.BlockSpec((tm, d), tok),
            pl.BlockSpec((tm, d // 2), tok),
            pl.BlockSpec((TOP_K, tm), lane_tok),
            pl.BlockSpec((TOP_K, tm), lane_tok),
        ),
        scratch_shapes=[pltpu.VMEM((ATT_WIDTH + HG_WIDTH, d + LANES), BF16)],
        compiler_params=_params(("arbitrary",)),
        name="out_proj_router",
    )(a, o, x2, mod3, norm2, w_out_bf, w_router_t, b_router3)


def _rank_kernel(ti_ref, tri_ref, rank_ref, cnt_ref, carry_scr):
    @pl.when(pl.program_id(0) == 0)
    def _():
        carry_scr[...] = jnp.zeros(carry_scr.shape, F32)

    ti = ti_ref[...]
    tl = ti.shape[1]
    eidx = lax.broadcasted_iota(I32, (N_EXPERTS, tl), 0)
    carry = carry_scr[...]
    rows = []
    for k in range(TOP_K):
        oh = eidx == ti[k:k + 1, :]
        ohf = jnp.where(oh, 1.0, 0.0)
        pre = _dot(ohf.astype(BF16), tri_ref[...])
        rows.append(jnp.sum(jnp.where(oh, carry + pre, 0.0), axis=0, keepdims=True))
        carry = carry + jnp.sum(ohf, axis=1, keepdims=True)
    carry_scr[...] = carry
    rank_ref[...] = jnp.concatenate(rows, axis=0).astype(I32)
    cnt_ref[...] = jnp.broadcast_to(carry, cnt_ref.shape)


def _ranks(topi):
    k, t = topi.shape
    tl = min(RANK_TILE, t)
    tri = np.triu(np.ones((tl, tl), np.float32), 1)
    return pl.pallas_call(
        _rank_kernel,
        out_shape=(jax.ShapeDtypeStruct((k, t), I32), jax.ShapeDtypeStruct((N_EXPERTS, LANES), F32)),
        grid=(t // tl,),
        in_specs=[pl.BlockSpec((k, tl), lambda i: (0, i)), pl.BlockSpec((tl, tl), lambda i: (0, 0))],
        out_specs=(pl.BlockSpec((k, tl), lambda i: (0, i)), pl.BlockSpec((N_EXPERTS, LANES), lambda i: (0, 0))),
        scratch_shapes=[pltpu.VMEM((N_EXPERTS, 1), F32)],
        compiler_params=_params(("arbitrary",)),
        name="route_rank",
    )(topi, jnp.asarray(tri, BF16))


def _dest_kernel(cnt_ref, ltri_ref, ti_ref, rank_ref, dest_ref, eblk_ref):
    cnt = cnt_ref[...]
    nblk = jnp.floor((cnt + (MOE_ROWS - 1)) * (1.0 / MOE_ROWS))
    pstart_b = _dot(ltri_ref[...], nblk.astype(BF16))
    pstart = (pstart_b[:, 0:1] * MOE_ROWS).astype(I32)
    ti = ti_ref[...]
    tl = ti.shape[1]
    eidx = lax.broadcasted_iota(I32, (N_EXPERTS, tl), 0)
    rows = []
    for k in range(TOP_K):
        oh = eidx == ti[k:k + 1, :]
        rows.append(jnp.sum(jnp.where(oh, pstart, 0), axis=0, keepdims=True))
    dest_ref[...] = jnp.concatenate(rows, axis=0) + rank_ref[...]
    eblk_ref[0] = pstart_b.astype(I32)
    eblk_ref[1] = nblk.astype(I32)


def _destinations(counts, topi, rank):
    k, t = topi.shape
    tl = min(2048, t)
    ltri = np.tril(np.ones((N_EXPERTS, N_EXPERTS), np.float32), -1)
    return pl.pallas_call(
        _dest_kernel,
        out_shape=(
            jax.ShapeDtypeStruct((k, t), I32),
            jax.ShapeDtypeStruct((2, N_EXPERTS, LANES), I32),
        ),
        grid=(t // tl,),
        in_specs=[
            pl.BlockSpec((N_EXPERTS, LANES), lambda i: (0, 0)),
            pl.BlockSpec((N_EXPERTS, N_EXPERTS), lambda i: (0, 0)),
            pl.BlockSpec((k, tl), lambda i: (0, i)),
            pl.BlockSpec((k, tl), lambda i: (0, i)),
        ],
        out_specs=(
            pl.BlockSpec((k, tl), lambda i: (0, i)),
            pl.BlockSpec((2, N_EXPERTS, LANES), lambda i: (0, 0, 0)),
        ),
        compiler_params=_params(("arbitrary",)),
        name="route_dest",
    )(counts, jnp.asarray(ltri, BF16), topi, rank)


def _pack_bf16_pairs(x):
    n = x.shape[1] // 2
    lo = lax.bitcast_convert_type(x[:, :n].astype(BF16).astype(F32), U32)
    hi = lax.bitcast_convert_type(x[:, n:].astype(BF16).astype(F32), U32)
    return hi | (lo >> 16)


def _unpack_bf16_pairs(w):
    lo = lax.bitcast_convert_type(w << 16, F32)
    hi = lax.bitcast_convert_type(w & jnp.uint32(0xFFFF0000), F32)
    return lo, hi


SC_WINDOW = 128


def _sc_mesh():
    return plsc.VectorSubcoreMesh(core_axis_name="core", subcore_axis_name="subcore")


def _sc_worker(mesh):
    return lax.axis_index("core") * mesh.num_subcores + lax.axis_index("subcore"), mesh.num_cores * mesh.num_subcores


def _sc_scatter_rows(src, dest_rows, n_out):
    n, width = src.shape
    nk = dest_rows.shape[0]
    mesh = _sc_mesh()
    half = SC_WINDOW // 2

    @pl.kernel(out_type=jax.ShapeDtypeStruct((n_out, width), src.dtype), mesh=mesh,
               scratch_types=[pltpu.VMEM((nk, SC_WINDOW), I32), pltpu.VMEM((2, half, width), src.dtype),
                              pltpu.SemaphoreType.DMA((2,))],
               name="sc_scatter_rows")
    def scatter_kernel(src_hbm, idx_hbm, out_hbm, idx_v, rows_v, sem):
        wid, nw = _sc_worker(mesh)
        per = n // SC_WINDOW // nw
        row0 = wid * per * SC_WINDOW

        def load(s, buf):
            return pltpu.make_async_copy(src_hbm.at[pl.ds(row0 + s * half, half)], rows_v.at[buf], sem.at[buf])

        load(0, 0).start()

        @pl.loop(0, per)
        def _(j):
            pltpu.sync_copy(idx_hbm.at[:, pl.ds(row0 + j * SC_WINDOW, SC_WINDOW)], idx_v)
            for h in range(2):
                if h == 0:
                    load(2 * j + 1, 1).start()
                else:
                    @pl.when(j + 1 < per)
                    def _():
                        load(2 * j + 2, 0).start()

                load(2 * j + h, h).wait()
                for k in range(nk):
                    pltpu.sync_copy(rows_v.at[h], out_hbm.at[idx_v.at[k, pl.ds(h * half, half)]])

    return scatter_kernel(src, dest_rows)


def _sc_gather_rows(src, rows):
    n = rows.shape[0]
    width = src.shape[1]
    mesh = _sc_mesh()
    half = SC_WINDOW // 2

    @pl.kernel(out_type=jax.ShapeDtypeStruct((n, width), src.dtype), mesh=mesh,
               scratch_types=[pltpu.VMEM((1, SC_WINDOW), I32), pltpu.VMEM((2, half, width), src.dtype),
                              pltpu.SemaphoreType.DMA((2,))],
               name="sc_gather_rows")
    def gather_kernel(src_hbm, idx_hbm, out_hbm, idx_v, rows_v, sem):
        wid, nw = _sc_worker(mesh)
        per = n // SC_WINDOW // nw
        row0 = wid * per * SC_WINDOW

        def store(s, buf):
            return pltpu.make_async_copy(rows_v.at[buf], out_hbm.at[pl.ds(row0 + s * half, half)], sem.at[buf])

        @pl.loop(0, per)
        def _(j):
            pltpu.sync_copy(idx_hbm.at[:, pl.ds(row0 + j * SC_WINDOW, SC_WINDOW)], idx_v)
            for h in range(2):
                @pl.when(j >= 1)
                def _():
                    store(2 * j + h - 2, h).wait()

                pltpu.sync_copy(src_hbm.at[idx_v.at[0, pl.ds(h * half, half)]], rows_v.at[h])
                store(2 * j + h, h).start()

        for h in range(2):
            store(2 * per - 2 + h, h).wait()

    return gather_kernel(src, rows.reshape(1, n))


def _expert_group_kernel(eb_ref, xs_hbm, wgu_hbm, bgu_ref, wd_hbm, bd_ref, ys_hbm,
                         xbuf, ybuf, wgu_f, wd_f, wgu_bf, wd_bf, act_scr, xsem, ysem, wsem, *, dff, l):
    e = pl.program_id(0)
    ne = pl.num_programs(0)
    b0 = eb_ref[e]
    nb = eb_ref[ne + e]
    rows, half = xbuf.shape[1], xbuf.shape[2]
    group = rows // MOE_ROWS
    nch = (nb + group - 1) // group
    wslot = e % 2

    def w_copies(ex, slot):
        cps = []
        for src, dst, first in ((wgu_hbm, wgu_f, 0), (wd_hbm, wd_f, WEIGHT_CHUNKS)):
            step = dst.shape[1] // WEIGHT_CHUNKS
            for c in range(WEIGHT_CHUNKS):
                cps.append(pltpu.make_async_copy(src.at[l, ex, pl.ds(c * step, step), :],
                                                 dst.at[slot, pl.ds(c * step, step), :], wsem.at[slot, first + c]))
        return cps

    @pl.when(e == 0)
    def _():
        for c, cp in enumerate(w_copies(0, 0)):
            cp.start(priority=c % 2)

    @pl.when(e + 1 < ne)
    def _():
        for c, cp in enumerate(w_copies(e + 1, 1 - wslot)):
            cp.start(priority=c % 2)

    for cp in w_copies(e, wslot):
        cp.wait()

    def x_copy(j, slot):
        start = (b0 + j * group) * MOE_ROWS
        return pltpu.make_async_copy(xs_hbm.at[pl.ds(start, rows), :], xbuf.at[slot], xsem.at[slot])

    def y_copy(j, slot, g):
        start = (b0 + j * group + g) * MOE_ROWS
        return pltpu.make_async_copy(ybuf.at[slot, pl.ds(g * MOE_ROWS, MOE_ROWS), :],
                                     ys_hbm.at[pl.ds(start, MOE_ROWS), :], ysem.at[slot, g])

    def y_each(j, slot, fn):
        for g in range(group):
            @pl.when(j * group + g < nb)
            def _():
                fn(y_copy(j, slot, g))

    @pl.when(nb > 0)
    def _():
        x_copy(0, 0).start(priority=1)
        wgu_bf[:, 0:2 * dff] = wgu_f[wslot].astype(BF16)
        wd_bf[:, 0:2 * half] = wd_f[wslot].astype(BF16)

        def chunk(j, carry):
            slot = j % 2

            @pl.when(j + 1 < nch)
            def _():
                x_copy(j + 1, 1 - slot).start(priority=1)

            x_copy(j, slot).wait()

            @pl.when(j >= 2)
            def _():
                y_each(j - 2, slot, lambda cp: cp.wait())

            lo, hi = _unpack_bf16_pairs(xbuf[slot])
            xb = jnp.concatenate([lo.astype(BF16), hi.astype(BF16)], axis=1)
            for c0 in range(0, dff, EXPERT_COLS):
                gate = _dot(xb, wgu_bf[:, c0:c0 + EXPERT_COLS]) + bgu_ref[:, c0:c0 + EXPERT_COLS]
                up = (_dot(xb, wgu_bf[:, dff + c0:dff + c0 + EXPERT_COLS])
                      + bgu_ref[:, dff + c0:dff + c0 + EXPERT_COLS])
                glu = jnp.minimum(gate, SWIGLU_LIMIT)
                lin = jnp.clip(up, -SWIGLU_LIMIT, SWIGLU_LIMIT)
                act_scr[:, c0:c0 + EXPERT_COLS] = (glu * _sigmoid(SWIGLU_ALPHA * glu) * (lin + 1.0)).astype(BF16)
            ybuf[slot] = _pack_bf16_pairs(_dot(act_scr[:, 0:dff], wd_bf[:, 0:2 * half]) + bd_ref[...])
            y_each(j, slot, lambda cp: cp.start(priority=1))
            return carry

        lax.fori_loop(0, nch, chunk, 0)

        @pl.when(nch >= 2)
        def _():
            y_each(nch - 2, nch % 2, lambda cp: cp.wait())

        y_each(nch - 1, (nch - 1) % 2, lambda cp: cp.wait())


def _block_experts(eblk, xs, w_gu, b_gu4, w_down, b_down4, l):
    cap, half = xs.shape
    d = 2 * half
    dff = w_down.shape[2]
    wsel = lambda e, eb: (l, e, 0, 0)
    grid_spec = pltpu.PrefetchScalarGridSpec(
        num_scalar_prefetch=1,
        grid=(N_EXPERTS,),
        in_specs=[
            pl.BlockSpec(memory_space=pl.ANY),
            pl.BlockSpec(memory_space=pl.ANY),
            pl.BlockSpec((None, None, 1, 2 * dff), wsel),
            pl.BlockSpec(memory_space=pl.ANY),
            pl.BlockSpec((None, None, 1, d), wsel),
        ],
        out_specs=pl.BlockSpec(memory_space=pl.ANY),
        scratch_shapes=[
            pltpu.VMEM((2, EXPERT_GROUP * MOE_ROWS, half), U32),
            pltpu.VMEM((2, EXPERT_GROUP * MOE_ROWS, half), U32),
            pltpu.VMEM((2, d, 2 * dff), F32),
            pltpu.VMEM((2, dff, d), F32),
            pltpu.VMEM((d, 2 * dff + LANES), BF16),
            pltpu.VMEM((dff, d + LANES), BF16),
            pltpu.VMEM((EXPERT_GROUP * MOE_ROWS, dff + LANES), BF16),
            pltpu.SemaphoreType.DMA((2,)),
            pltpu.SemaphoreType.DMA((2, EXPERT_GROUP)),
            pltpu.SemaphoreType.DMA((2, 2 * WEIGHT_CHUNKS)),
        ],
    )
    return pl.pallas_call(
        functools.partial(_expert_group_kernel, dff=dff, l=l),
        out_shape=jax.ShapeDtypeStruct((cap - (EXPERT_GROUP - 1) * MOE_ROWS, half), U32),
        grid_spec=grid_spec,
        compiler_params=_params(("arbitrary",)),
        name="moe_expert_groups",
    )(eblk, xs, w_gu, b_gu4, w_down, b_down4)


def _combine_kernel(*refs, d, final):
    y_refs = refs[:TOP_K]
    gt_ref, x_ref, mod_ref = refs[TOP_K:TOP_K + 3]
    xo_ref = refs[-1]
    half = d // 2
    gt = gt_ref[...]
    acc_lo = acc_hi = None
    for k in range(TOP_K):
        lo, hi = _unpack_bf16_pairs(y_refs[k][...])
        g = gt[:, k:k + 1]
        acc_lo = g * lo if acc_lo is None else acc_lo + g * lo
        acc_hi = g * hi if acc_hi is None else acc_hi + g * hi
    x_lo = x_ref[:, 0:half] + mod_ref[:, 5 * d:5 * d + half] * acc_lo
    x_hi = x_ref[:, half:d] + mod_ref[:, 5 * d + half:6 * d] * acc_hi
    if final:
        gain = refs[TOP_K + 3][...]
        ssq = jnp.sum(x_lo * x_lo, axis=-1, keepdims=True) + jnp.sum(x_hi * x_hi, axis=-1, keepdims=True)
        inv = lax.rsqrt(ssq * (1.0 / d) + EPS)
        x_lo = x_lo * inv * gain[:, 0:half]
        x_hi = x_hi * inv * gain[:, half:d]
    xo_ref[:, 0:half] = x_lo
    xo_ref[:, half:d] = x_hi


def _combine(y4p, gates_t, x2, mod3, l, seq, depth, row0, prev, final_gain=None):
    t, d = x2.shape
    n = gates_t.shape[0]
    tm = min(PROJ_TILE, seq)
    nb = mod3.shape[0] // depth
    nt = n // tm
    t0 = row0 // tm
    final = final_gain is not None

    def slot(k):
        return pl.BlockSpec((tm, d // 2), lambda i: (k * nt + i, 0))

    specs = [slot(k) for k in range(TOP_K)] + [
        pl.BlockSpec((tm, TOP_K), lambda i: (i, 0)),
        pl.BlockSpec((tm, d), lambda i: (t0 + i, 0)),
        pl.BlockSpec((None, 1, N_MOD * d), lambda i: (l * nb + (row0 + i * tm) // seq, 0, 0)),
    ]
    args = [y4p] * TOP_K + [gates_t, x2, mod3]
    if final:
        specs.append(pl.BlockSpec((1, d), lambda i: (0, 0)))
        args.append(final_gain.reshape(1, d))
    aliases = {}
    if prev is not None:
        specs.append(pl.BlockSpec(memory_space=pl.ANY))
        aliases = {len(args): 0}
        args.append(prev)
    return pl.pallas_call(
        functools.partial(_combine_kernel, d=d, final=final),
        out_shape=jax.ShapeDtypeStruct((t, d), F32),
        grid=(nt,),
        in_specs=specs,
        out_specs=pl.BlockSpec((tm, d), lambda i: (t0 + i, 0)),
        input_output_aliases=aliases,
        compiler_params=_params(("arbitrary",)),
        name="moe_combine",
    )(*args)


def _mixer_layer(x2, mod3, tables, lower, consts, p, l, batch, seq):
    ra, rb, rc = tables
    qa, ka2, va2, hg = _inproj(x2, mod3, p["norm1"], p["w_in"], ra, rb, rc, l, seq)
    a = _attention(qa, ka2, va2, p["attn_sink"], p["attn_norm"], l, batch, seq)
    o = _hgrn(hg, lower, p["hg_norm"], consts, l, batch, seq)
    return a, o


def _moe_layer(a, o, x2, mod3, p, l, seq, depth, final_gain=None):
    t, d = x2.shape
    xn, h2p, topi, gates = _outproj(a, o, x2, mod3, p["norm2"], p["w_out"], p["w_router_t"], p["b_router"], l, seq)
    rank, counts = _ranks(topi)
    nblocks = (t * TOP_K + N_EXPERTS * MOE_ROWS) // MOE_ROWS
    dest, eblk = _destinations(counts, topi, rank)
    xs = _sc_scatter_rows(h2p, dest, (nblocks + EXPERT_GROUP - 1) * MOE_ROWS)
    ys = _block_experts(eblk[:, :, 0].reshape(-1), xs, p["w_gu"], p["b_gu"], p["w_down"], p["b_down"], l)
    gates_t = gates.T
    out = None
    for part in range(MOE_COMBINE_PARTS):
        rows = slice(part * (t // MOE_COMBINE_PARTS), (part + 1) * (t // MOE_COMBINE_PARTS))
        y4p = _sc_gather_rows(ys, dest[:, rows].reshape(-1))
        out = _combine(y4p, gates_t[rows], xn, mod3, l, seq, depth, rows.start, out, final_gain)
    return out


def kernel(x, c, positions, w_ada, b_ada, norm1, w_in, attn_sink, attn_norm, hg_lb_logits, hg_norm, w_out, norm2,
           w_router, b_router, w_gu, b_gu, w_down, b_down, final_norm):
    batch, seq, d = x.shape
    depth = w_ada.shape[0]
    t = batch * seq
    p = {
        "norm1": norm1.reshape(depth, 1, d),
        "w_in": w_in.astype(BF16),
        "attn_sink": attn_sink.astype(F32),
        "attn_norm": attn_norm.reshape(depth, 1, ATT_WIDTH),
        "hg_norm": hg_norm.reshape(depth, 1, HG_HEAD_DIM),
        "w_out": w_out.astype(BF16),
        "norm2": norm2.reshape(depth, 1, d),
        "w_router_t": jnp.swapaxes(w_router, 1, 2),
        "b_router": b_router.reshape(depth, N_EXPERTS, 1),
        "w_gu": w_gu,
        "b_gu": b_gu.reshape(depth, N_EXPERTS, 1, b_gu.shape[-1]),
        "w_down": w_down,
        "b_down": b_down.reshape(depth, N_EXPERTS, 1, d),
    }
    mod3 = _ada_all(c, w_ada, b_ada).reshape(depth * batch, 1, N_MOD * d)
    lower = _lower_bounds(hg_lb_logits)
    tables = _rope_tables(positions)
    consts = tuple(jnp.asarray(m) for m in _hgrn_constants())
    x2 = x.reshape(t, d)
    for l in range(depth):
        a, o = _mixer_layer(x2, mod3, tables, lower, consts, p, l, batch, seq)
        x2 = _moe_layer(a, o, x2, mod3, p, l, seq, depth, final_norm if l == depth - 1 else None)
    return x2.reshape(batch, seq, d)
```

```python
import functools

import numpy as np
import jax
import jax.numpy as jnp
from jax import lax
from jax.experimental import pallas as pl
from jax.experimental.pallas import tpu as pltpu
from jax.experimental.pallas import tpu_sc as plsc

F32 = jnp.float32
BF16 = jnp.bfloat16
I32 = jnp.int32
U32 = jnp.uint32

ATT_HEADS = 8
ATT_KV_HEADS = 2
ATT_HEAD_DIM = 64
ATT_WIDTH = ATT_HEADS * ATT_HEAD_DIM
KV_WIDTH = ATT_KV_HEADS * ATT_HEAD_DIM
WINDOW = 128
ATT_BLOCK = 128
ROPE_THETA = 500000.0
ROPE_DIM = ATT_HEAD_DIM // 4
HG_HEADS = 4
HG_HEAD_DIM = 128
HG_WIDTH = HG_HEADS * HG_HEAD_DIM
N_EXPERTS = 32
TOP_K = 4
SWIGLU_ALPHA = 1.702
SWIGLU_LIMIT = 7.0
N_MOD = 6
EPS = 1e-6
NEG_INF = -1e30
LB_FLOOR = 1e-30

LANES = 128
HG_CHUNK = 64
HG_LEVELS = 6
HG_UNROLL = 8
MOE_ROWS = 256
EXPERT_GROUP = 2
WEIGHT_CHUNKS = 4
EXPERT_COLS = 512
MOE_COMBINE_PARTS = 2
PROJ_TILE = 512
RANK_TILE = 512
VMEM_LIMIT = 56 * 1024 * 1024


def _dot(a, b):
    return jnp.dot(a, b, preferred_element_type=F32)


def _dot_nt(a, b):
    return lax.dot_general(a, b, (((1,), (1,)), ((), ())), preferred_element_type=F32)


def _dot_tn(a, b):
    return lax.dot_general(a, b, (((0,), (0,)), ((), ())), preferred_element_type=F32)


def _split3(x):
    hi = x.astype(BF16)
    r1 = x - hi.astype(F32)
    mid = r1.astype(BF16)
    lo = (r1 - mid.astype(F32)).astype(BF16)
    return hi, mid, lo


def _dot_f32_nt(a, b):
    ah, am, _ = _split3(a)
    bh, bm, _ = _split3(b)
    return _dot_nt(ah, bh) + _dot_nt(ah, bm) + _dot_nt(am, bh)


def _dot_f32(a, b):
    ah, am, _ = _split3(a)
    bh, bm, _ = _split3(b)
    return _dot(ah, bh) + _dot(ah, bm) + _dot(am, bh)


def _sigmoid(x):
    return 1.0 / (1.0 + jnp.exp(-x))


def _params(sem=None):
    return pltpu.CompilerParams(dimension_semantics=sem, vmem_limit_bytes=VMEM_LIMIT)


def _ada_kernel(c_ref, w_ref, b_ref, o_ref):
    c = c_ref[...]
    cond = c * _sigmoid(c)
    o_ref[...] = _dot_f32(cond, w_ref[...]) + b_ref[...]


def _ada_all(c, w_ada, b_ada):
    depth, d, n = w_ada.shape
    b = c.shape[0]
    nt = n // d
    return pl.pallas_call(
        _ada_kernel,
        out_shape=jax.ShapeDtypeStruct((depth, b, n), F32),
        grid=(depth, nt),
        in_specs=[
            pl.BlockSpec((b, d), lambda l, j: (0, 0)),
            pl.BlockSpec((None, d, d), lambda l, j: (l, 0, j)),
            pl.BlockSpec((None, 1, d), lambda l, j: (l, 0, j)),
        ],
        out_specs=pl.BlockSpec((None, b, d), lambda l, j: (l, 0, j)),
        compiler_params=_params(("arbitrary", "arbitrary")),
        name="ada_mod",
    )(c, w_ada, b_ada.reshape(depth, 1, n))


def _lb_kernel(x_ref, o_ref):
    depth = x_ref.shape[0]
    xs = [x_ref[l] for l in range(depth)]
    m = xs[0]
    for l in range(1, depth):
        m = jnp.maximum(m, xs[l])
    es = [jnp.exp(v - m) for v in xs]
    den = es[0]
    for l in range(1, depth):
        den = den + es[l]
    ps = [e / den for e in es]
    run = ps[0]
    o_ref[0] = run - ps[0]
    for l in range(1, depth):
        run = run + ps[l]
        o_ref[l] = run - ps[0]


def _lower_bounds(hg_lb_logits):
    return pl.pallas_call(
        _lb_kernel,
        out_shape=jax.ShapeDtypeStruct(hg_lb_logits.shape, F32),
        name="hg_lower_bounds",
    )(hg_lb_logits.astype(F32))


def _rope_kernel(pos_ref, invf_ref, a_ref, b_ref, c_ref):
    pos = pos_ref[...].astype(F32)
    ang = pos * invf_ref[...]
    cs = jnp.cos(ang)
    sn = jnp.sin(ang)
    lane = lax.broadcasted_iota(I32, ang.shape, 1) & (ATT_HEAD_DIM - 1)
    half = ROPE_DIM // 2
    first = lane < half
    second = (lane >= half) & (lane < ROPE_DIM)
    a_ref[...] = jnp.where(first | second, cs, 1.0)
    b_ref[...] = jnp.where(first, -sn, 0.0)
    c_ref[...] = jnp.where(second, sn, 0.0)


def _rope_tables(positions):
    t = positions.size
    half = ROPE_DIM // 2
    inv = (np.float32(ROPE_THETA) ** (-(np.arange(half, dtype=np.float32) * np.float32(2.0) / np.float32(ROPE_DIM)))).astype(np.float32)
    lane = np.arange(LANES) % ATT_HEAD_DIM
    pat = np.where(lane < ROPE_DIM, inv[lane % half], 0.0).astype(np.float32).reshape(1, LANES)
    tm = min(t, 2048)
    shp = jax.ShapeDtypeStruct((t, LANES), F32)
    spec = pl.BlockSpec((tm, LANES), lambda i: (i, 0))
    return pl.pallas_call(
        _rope_kernel,
        out_shape=(shp, shp, shp),
        grid=(t // tm,),
        in_specs=[pl.BlockSpec((tm, 1), lambda i: (i, 0)), pl.BlockSpec((1, LANES), lambda i: (0, 0))],
        out_specs=(spec, spec, spec),
        compiler_params=_params(("arbitrary",)),
        name="rope_tables",
    )(positions.reshape(t, 1).astype(I32), jnp.asarray(pat))


def _rms_mod(x, gain, scale, shift):
    ms = jnp.mean(x * x, axis=-1, keepdims=True)
    return (x * lax.rsqrt(ms + EPS) * gain) * (1.0 + scale) + shift


def _rope_apply(x, a, b, c):
    half = ROPE_DIM // 2
    return x * a + pltpu.roll(x, LANES - half, 1) * b + pltpu.roll(x, half, 1) * c


def _inproj_kernel(x_ref, mod_ref, n1_ref, w_ref, ra_ref, rb_ref, rc_ref, qa_ref, ka_ref, va_ref, hg_ref, *, d):
    mod = mod_ref[...]
    h = _rms_mod(x_ref[...], n1_ref[...], mod[:, d:2 * d], mod[:, 0:d]).astype(BF16)
    a, b, c = ra_ref[...], rb_ref[...], rc_ref[...]
    kvw = ATT_WIDTH + 2 * KV_WIDTH
    pa = _dot(h, w_ref[:, 0:kvw])
    scale = ATT_HEAD_DIM ** -0.5
    for g in range(ATT_WIDTH // LANES):
        qg = _rope_apply(pa[:, g * LANES:(g + 1) * LANES], a, b, c)
        qa_ref[:, g * LANES:(g + 1) * LANES] = (qg * scale).astype(BF16)
    k = _rope_apply(pa[:, ATT_WIDTH:ATT_WIDTH + KV_WIDTH], a, b, c)
    v = pa[:, ATT_WIDTH + KV_WIDTH:kvw]
    ka_ref[:, 0:LANES] = k.astype(BF16)
    ka_ref[:, LANES:2 * LANES] = pltpu.roll(k, ATT_HEAD_DIM, 1).astype(BF16)
    va_ref[:, 0:LANES] = v.astype(BF16)
    va_ref[:, LANES:2 * LANES] = pltpu.roll(v, ATT_HEAD_DIM, 1).astype(BF16)
    for g in range(5):
        lo = kvw + g * HG_WIDTH
        hg_ref[:, g * HG_WIDTH:(g + 1) * HG_WIDTH] = _dot(h, w_ref[:, lo:lo + HG_WIDTH])


def _inproj(x2, mod3, norm1, w_in_bf, ra, rb, rc, l, seq):
    t, d = x2.shape
    n_in = w_in_bf.shape[-1]
    tm = min(PROJ_TILE, seq)
    nb = mod3.shape[0] // norm1.shape[0]
    tok = lambda i: (i, 0)
    return pl.pallas_call(
        functools.partial(_inproj_kernel, d=d),
        out_shape=(
            jax.ShapeDtypeStruct((t, ATT_WIDTH), BF16),
            jax.ShapeDtypeStruct((t, 2 * KV_WIDTH), BF16),
            jax.ShapeDtypeStruct((t, 2 * KV_WIDTH), BF16),
            jax.ShapeDtypeStruct((t, 5 * HG_WIDTH), F32),
        ),
        grid=(t // tm,),
        in_specs=[
            pl.BlockSpec((tm, d), tok),
            pl.BlockSpec((None, 1, N_MOD * d), lambda i: (l * nb + (i * tm) // seq, 0, 0)),
            pl.BlockSpec((None, 1, d), lambda i: (l, 0, 0)),
            pl.BlockSpec((None, d, n_in), lambda i: (l, 0, 0)),
            pl.BlockSpec((tm, LANES), tok),
            pl.BlockSpec((tm, LANES), tok),
            pl.BlockSpec((tm, LANES), tok),
        ],
        out_specs=(
            pl.BlockSpec((tm, ATT_WIDTH), tok),
            pl.BlockSpec((tm, 2 * KV_WIDTH), tok),
            pl.BlockSpec((tm, 2 * KV_WIDTH), tok),
            pl.BlockSpec((tm, 5 * HG_WIDTH), tok),
        ),
        compiler_params=_params(("arbitrary",)),
        name="in_proj",
    )(x2, mod3, norm1, w_in_bf, ra, rb, rc)


def _attn_kernel(sink_ref, q_ref, kp_ref, kc_ref, kn_ref, vp_ref, vc_ref, vn_ref, gain_ref, o_ref, *, l, seq):
    n = pl.program_id(1)
    blk = ATT_BLOCK
    k2 = jnp.concatenate([kp_ref[...], kc_ref[...], kn_ref[...]], axis=0)
    v2 = jnp.concatenate([vp_ref[...], vc_ref[...], vn_ref[...]], axis=0)
    lane = lax.broadcasted_iota(I32, (3 * blk, LANES), 1)
    lo_half = lane < ATT_HEAD_DIM
    zero = jnp.zeros((3 * blk, LANES), BF16)
    ka, kb = k2[:, 0:LANES], k2[:, LANES:2 * LANES]
    va, vb = v2[:, 0:LANES], v2[:, LANES:2 * LANES]
    kz = [[jnp.where(lo_half, ka, zero), jnp.where(lo_half, zero, kb)],
          [jnp.where(lo_half, kb, zero), jnp.where(lo_half, zero, ka)]]
    vz = [[jnp.where(lo_half, va, zero), jnp.where(lo_half, zero, vb)],
          [jnp.where(lo_half, vb, zero), jnp.where(lo_half, zero, va)]]
    qpos = n * blk + lax.broadcasted_iota(I32, (blk, 3 * blk), 0)
    kpos = (n - 1) * blk + lax.broadcasted_iota(I32, (blk, 3 * blk), 1)
    valid = (jnp.abs(qpos - kpos) <= WINDOW) & (kpos >= 0) & (kpos < seq)
    valid2 = jnp.concatenate([valid, valid], axis=0)
    upper = lax.broadcasted_iota(I32, (2 * blk, 1), 0) < blk
    outs = []
    for j in range(ATT_KV_HEADS):
        qs = jnp.concatenate([q_ref[:, 2 * j * LANES:(2 * j + 1) * LANES],
                              q_ref[:, (2 * j + 1) * LANES:(2 * j + 2) * LANES]], axis=0)
        acc = None
        for half in range(2):
            s = _dot_nt(qs, kz[j][half])
            s = jnp.where(valid2, s, NEG_INF)
            sink = jnp.where(upper, sink_ref[l, 4 * j + half], sink_ref[l, 4 * j + 2 + half])
            mx = jnp.maximum(jnp.max(s, axis=-1, keepdims=True), sink)
            p = jnp.exp(s - mx)
            den = jnp.sum(p, axis=-1, keepdims=True) + jnp.exp(sink - mx)
            p = (p * (1.0 / den)).astype(BF16)
            pv = _dot(p, vz[j][half])
            acc = pv if acc is None else acc + pv
        outs.append(acc[0:blk])
        outs.append(acc[blk:2 * blk])
    o = jnp.concatenate(outs, axis=-1)
    ms = jnp.mean(o * o, axis=-1, keepdims=True)
    o_ref[...] = (o * lax.rsqrt(ms + EPS) * gain_ref[...]).astype(BF16)


def _attention(qa, ka2, va2, attn_sink, attn_norm3, l, batch, seq):
    t = qa.shape[0]
    blk = ATT_BLOCK
    nb = seq // blk
    cur = lambda b, n: (b * nb + n, 0)
    prev = lambda b, n: (b * nb + jnp.maximum(n - 1, 0), 0)
    nxt = lambda b, n: (b * nb + jnp.minimum(n + 1, nb - 1), 0)
    kvspec = lambda f: pl.BlockSpec((blk, 2 * KV_WIDTH), f)
    return pl.pallas_call(
        functools.partial(_attn_kernel, l=l, seq=seq),
        out_shape=jax.ShapeDtypeStruct((t, ATT_WIDTH), BF16),
        grid=(batch, nb),
        in_specs=[
            pl.BlockSpec(memory_space=pltpu.SMEM),
            pl.BlockSpec((blk, ATT_WIDTH), cur),
            kvspec(prev), kvspec(cur), kvspec(nxt),
            kvspec(prev), kvspec(cur), kvspec(nxt),
            pl.BlockSpec((None, 1, ATT_WIDTH), lambda b, n: (l, 0, 0)),
        ],
        out_specs=pl.BlockSpec((blk, ATT_WIDTH), cur),
        compiler_params=_params(("arbitrary", "arbitrary")),
        name="window_attn",
    )(attn_sink, qa, ka2, ka2, ka2, va2, va2, va2, attn_norm3)


def _hgrn_constants():
    c, nl = HG_CHUNK, HG_LEVELS
    r = np.arange(c)
    masks = []
    for lev in range(nl):
        m = 1 << lev
        parent = r // (2 * m)
        upper = r >= parent * 2 * m + m
        masks.append((parent[:, None] == parent[None, :]) & upper[:, None] & (~upper)[None, :])
    masks.append(np.eye(c, dtype=bool))
    kf = np.stack(masks).astype(np.float32)
    kb = np.stack([mk[::-1, ::-1] for mk in masks]).astype(np.float32)
    return kf, kb


def _chunk_decays(logf, reverse):
    c = logf.shape[0]
    sub = 8
    nv = c // sub
    row = lax.broadcasted_iota(I32, (sub, LANES), 0)
    parts = []
    for v in range(nv):
        x = logf[sub * v:sub * (v + 1), :]
        for s in (1, 2, 4):
            if reverse:
                x = x + jnp.where(row < sub - s, pltpu.roll(x, sub - s, 0), 0.0)
            else:
                x = x + jnp.where(row >= s, pltpu.roll(x, s, 0), 0.0)
        parts.append(x)
    b = [None] * nv
    order = list(reversed(range(nv))) if reverse else list(range(nv))
    edge = 0 if reverse else sub - 1
    carry = None
    for v in order:
        b[v] = parts[v] if carry is None else parts[v] + carry
        carry = b[v][edge:edge + 1, :]
    b_last = carry

    def anchor_row(v, r):
        return jnp.broadcast_to(b[v][r:r + 1, :], (sub, LANES))

    odd = (row & 1) == 1
    levels = []
    for lev in range(HG_LEVELS):
        m = 1 << lev
        pieces = []
        for v in range(nv):
            if m == 1:
                a = jnp.where(odd, pltpu.roll(b[v], 1, 0), b[v]) if reverse else jnp.where(odd, b[v], pltpu.roll(b[v], sub - 1, 0))
            elif m == 2:
                lo, hi = (1, 5) if reverse else (2, 6)
                a = jnp.where(row < 4, anchor_row(v, lo), anchor_row(v, hi))
            elif m == 4:
                a = anchor_row(v, 3 if reverse else 4)
            else:
                mv = m // sub
                first = (v // (2 * mv)) * 2 * mv
                a = anchor_row(first + mv - 1, sub - 1) if reverse else anchor_row(first + mv, 0)
            pieces.append(jnp.exp(-jnp.abs(b[v] - a)))
        levels.append(jnp.concatenate(pieces, axis=0))
    eb = jnp.concatenate([jnp.exp(bv) for bv in b], axis=0)
    erem = jnp.concatenate([jnp.exp(b_last - bv) for bv in b], axis=0)
    return levels, eb, erem


def _hgrn_kernel(q_ref, ff_ref, fb_ref, i_ref, g_ref, lb_ref, gn_ref, kf_ref, kb_ref,
                 o_ref, of_scr, ob_scr, st_scr, *, seq):
    c, nl = HG_CHUNK, HG_LEVELS
    nc = seq // c
    lb = lb_ref[...]
    st_scr[...] = jnp.zeros(st_scr.shape, F32)

    unroll = HG_UNROLL if nc % HG_UNROLL == 0 else 1
    dirs = (
        dict(f_ref=ff_ref, lbrow=lb[0:1, :], k_ref=kf_ref, o_scr=of_scr, last_row=c - 1, d=0),
        dict(f_ref=fb_ref, lbrow=lb[1:2, :], k_ref=kb_ref, o_scr=ob_scr, last_row=0, d=1),
    )

    def body(i, carry):
        work = []
        for u in range(unroll):
            cf = i * unroll + u
            work.append((dirs[0], pl.ds(pl.multiple_of(cf * c, c), c)))
            work.append((dirs[1], pl.ds(pl.multiple_of((nc - 1 - cf) * c, c), c)))
        gates = []
        for dr, rows in work:
            lbf = jnp.maximum(dr["lbrow"], LB_FLOOR)
            oml = 1.0 - dr["lbrow"]
            f = dr["f_ref"][rows, :]
            e = jnp.exp(-jnp.abs(f))
            r = 1.0 / (1.0 + e)
            er = e * r
            pos = f >= 0.0
            logf = jnp.log(lbf + oml * jnp.where(pos, r, er))
            kk = oml * jnp.where(pos, er, r)
            qh = q_ref[rows, :]
            gates.append((logf, kk.astype(BF16), (qh * _sigmoid(qh)).astype(BF16), i_ref[rows, :].astype(BF16)))
        decays = [_chunk_decays(g[0], dr["d"] == 1) for (dr, _), g in zip(work, gates)]
        amat = [dr["k_ref"][nl] * _dot_nt(g[2], g[1]) for (dr, _), g in zip(work, gates)]
        for lev in range(nl):
            for j, ((dr, _), g) in enumerate(zip(work, gates)):
                gl = decays[j][0][lev].astype(BF16)
                amat[j] = amat[j] + dr["k_ref"][lev] * _dot_nt(g[2] * gl, g[1] * gl)
        intra = [_dot(amat[j].astype(BF16), g[3]) for j, g in enumerate(gates)]
        upd = [_dot_tn(g[3], g[1] * decays[j][2].astype(BF16)) for j, g in enumerate(gates)]
        st = [st_scr[0], st_scr[1]]
        for j, ((dr, rows), g) in enumerate(zip(work, gates)):
            eb = decays[j][1]
            d = dr["d"]
            dr["o_scr"][rows, :] = _dot_nt(g[2] * eb.astype(BF16), st[d].astype(BF16)) + intra[j]
            st[d] = st[d] * eb[dr["last_row"]:dr["last_row"] + 1, :] + upd[j]
        st_scr[0] = st[0]
        st_scr[1] = st[1]
        return carry

    lax.fori_loop(0, nc // unroll, body, 0)

    ep = min(256, seq)
    gn = gn_ref[...]

    def epilogue(j, carry):
        rows = pl.ds(pl.multiple_of(j * ep, ep), ep)
        o = of_scr[rows, :] + ob_scr[rows, :]
        y = o * lax.rsqrt(jnp.mean(o * o, axis=-1, keepdims=True) + EPS) * gn
        g = g_ref[rows, :]
        o_ref[rows, :] = (y * (g * _sigmoid(g))).astype(BF16)
        return carry

    lax.fori_loop(0, seq // ep, epilogue, 0)


def _hgrn(hg, lower, hg_norm3, consts, l, batch, seq):
    t = hg.shape[0]
    kf, kb = consts
    hd = HG_HEAD_DIM

    def col(g):
        return pl.BlockSpec((seq, hd), lambda b, h: (b, g * HG_HEADS + h))

    full3 = lambda a: pl.BlockSpec(a.shape, lambda b, h: (0, 0, 0))
    return pl.pallas_call(
        functools.partial(_hgrn_kernel, seq=seq),
        out_shape=jax.ShapeDtypeStruct((t, HG_WIDTH), BF16),
        grid=(batch, HG_HEADS),
        in_specs=[
            col(0), col(1), col(2), col(3), col(4),
            pl.BlockSpec((None, 2, hd), lambda b, h: (l, 0, h)),
            pl.BlockSpec((None, 1, hd), lambda b, h: (l, 0, 0)),
            full3(kf), full3(kb),
        ],
        out_specs=pl.BlockSpec((seq, hd), lambda b, h: (b, h)),
        scratch_shapes=[
            pltpu.VMEM((seq, hd), F32),
            pltpu.VMEM((seq, hd), F32),
            pltpu.VMEM((2, hd, hd), F32),
        ],
        compiler_params=_params(("arbitrary", "arbitrary")),
        name="hgrn2_scan",
    )(hg, hg, hg, hg, hg, lower, hg_norm3, kf, kb)


def _outproj_kernel(a_ref, o_ref, x_ref, mod_ref, n2_ref, w_ref, wr_ref, br_ref,
                    xo_ref, h2_ref, ti_ref, gt_ref, *, d):
    mod = mod_ref[...]
    y = _dot(a_ref[...], w_ref[0:ATT_WIDTH, :]) + _dot(o_ref[...], w_ref[ATT_WIDTH:ATT_WIDTH + HG_WIDTH, :])
    xn = x_ref[...] + mod[:, 2 * d:3 * d] * y
    xo_ref[...] = xn
    h2 = _rms_mod(xn, n2_ref[...], mod[:, 4 * d:5 * d], mod[:, 3 * d:4 * d])
    h2_ref[...] = _pack_bf16_pairs(h2)
    lg = _dot_f32_nt(wr_ref[...], h2) + br_ref[...]
    eidx = lax.broadcasted_iota(I32, lg.shape, 0)
    vals, idxs = [], []
    for _ in range(TOP_K):
        mx = jnp.max(lg, axis=0, keepdims=True)
        sel = jnp.min(jnp.where(lg == mx, eidx, N_EXPERTS), axis=0, keepdims=True)
        vals.append(mx)
        idxs.append(sel)
        lg = jnp.where(eidx == sel, -jnp.inf, lg)
    ex = [jnp.exp(v - vals[0]) for v in vals]
    den = ex[0]
    for e in ex[1:]:
        den = den + e
    inv = 1.0 / den
    ti_ref[...] = jnp.concatenate(idxs, axis=0)
    gt_ref[...] = jnp.concatenate([e * inv for e in ex], axis=0)


def _outproj(a, o, x2, mod3, norm2, w_out_bf, w_router_t, b_router3, l, seq):
    t, d = x2.shape
    tm = min(PROJ_TILE, seq)
    nb = mod3.shape[0] // norm2.shape[0]
    tok = lambda i: (i, 0)
    lane_tok = lambda i: (0, i)
    return pl.pallas_call(
        functools.partial(_outproj_kernel, d=d),
        out_shape=(
            jax.ShapeDtypeStruct((t, d), F32),
            jax.ShapeDtypeStruct((t, d // 2), U32),
            jax.ShapeDtypeStruct((TOP_K, t), I32),
            jax.ShapeDtypeStruct((TOP_K, t), F32),
        ),
        grid=(t // tm,),
        in_specs=[
            pl.BlockSpec((tm, ATT_WIDTH), tok),
            pl.BlockSpec((tm, HG_WIDTH), tok),
            pl.BlockSpec((tm, d), tok),
            pl.BlockSpec((None, 1, N_MOD * d), lambda i: (l * nb + (i * tm) // seq, 0, 0)),
            pl.BlockSpec((None, 1, d), lambda i: (l, 0, 0)),
            pl.BlockSpec((None, ATT_WIDTH + HG_WIDTH, d), lambda i: (l, 0, 0)),
            pl.BlockSpec((None, N_EXPERTS, d), lambda i: (l, 0, 0)),
            pl.BlockSpec((None, N_EXPERTS, 1), lambda i: (l, 0, 0)),
        ],
        out_specs=(
            pl.BlockSpec((tm, d), tok),
            pl.BlockSpec((tm, d // 2), tok),
            pl.BlockSpec((TOP_K, tm), lane_tok),
            pl.BlockSpec((TOP_K, tm), lane_tok),
        ),
        compiler_params=_params(("arbitrary",)),
        name="out_proj_router",
    )(a, o, x2, mod3, norm2, w_out_bf, w_router_t, b_router3)


def _rank_kernel(ti_ref, tri_ref, rank_ref, cnt_ref, carry_scr):
    @pl.when(pl.program_id(0) == 0)
    def _():
        carry_scr[...] = jnp.zeros(carry_scr.shape, F32)

    ti = ti_ref[...]
    tl = ti.shape[1]
    eidx = lax.broadcasted_iota(I32, (N_EXPERTS, tl), 0)
    carry = carry_scr[...]
    rows = []
    for k in range(TOP_K):
        oh = eidx == ti[k:k + 1, :]
        ohf = jnp.where(oh, 1.0, 0.0)
        pre = _dot(ohf.astype(BF16), tri_ref[...])
        rows.append(jnp.sum(jnp.where(oh, carry + pre, 0.0), axis=0, keepdims=True))
        carry = carry + jnp.sum(ohf, axis=1, keepdims=True)
    carry_scr[...] = carry
    rank_ref[...] = jnp.concatenate(rows, axis=0).astype(I32)
    cnt_ref[...] = jnp.broadcast_to(carry, cnt_ref.shape)


def _ranks(topi):
    k, t = topi.shape
    tl = min(RANK_TILE, t)
    tri = np.triu(np.ones((tl, tl), np.float32), 1)
    return pl.pallas_call(
        _rank_kernel,
        out_shape=(jax.ShapeDtypeStruct((k, t), I32), jax.ShapeDtypeStruct((N_EXPERTS, LANES), F32)),
        grid=(t // tl,),
        in_specs=[pl.BlockSpec((k, tl), lambda i: (0, i)), pl.BlockSpec((tl, tl), lambda i: (0, 0))],
        out_specs=(pl.BlockSpec((k, tl), lambda i: (0, i)), pl.BlockSpec((N_EXPERTS, LANES), lambda i: (0, 0))),
        scratch_shapes=[pltpu.VMEM((N_EXPERTS, 1), F32)],
        compiler_params=_params(("arbitrary",)),
        name="route_rank",
    )(topi, jnp.asarray(tri, BF16))


def _dest_kernel(cnt_ref, ltri_ref, ti_ref, rank_ref, dest_ref, eblk_ref):
    cnt = cnt_ref[...]
    nblk = jnp.floor((cnt + (MOE_ROWS - 1)) * (1.0 / MOE_ROWS))
    pstart_b = _dot(ltri_ref[...], nblk.astype(BF16))
    pstart = (pstart_b[:, 0:1] * MOE_ROWS).astype(I32)
    ti = ti_ref[...]
    tl = ti.shape[1]
    eidx = lax.broadcasted_iota(I32, (N_EXPERTS, tl), 0)
    rows = []
    for k in range(TOP_K):
        oh = eidx == ti[k:k + 1, :]
        rows.append(jnp.sum(jnp.where(oh, pstart, 0), axis=0, keepdims=True))
    dest_ref[...] = jnp.concatenate(rows, axis=0) + rank_ref[...]
    eblk_ref[0] = pstart_b.astype(I32)
    eblk_ref[1] = nblk.astype(I32)


def _destinations(counts, topi, rank):
    k, t = topi.shape
    tl = min(2048, t)
    ltri = np.tril(np.ones((N_EXPERTS, N_EXPERTS), np.float32), -1)
    return pl.pallas_call(
        _dest_kernel,
        out_shape=(
            jax.ShapeDtypeStruct((k, t), I32),
            jax.ShapeDtypeStruct((2, N_EXPERTS, LANES), I32),
        ),
        grid=(t // tl,),
        in_specs=[
            pl.BlockSpec((N_EXPERTS, LANES), lambda i: (0, 0)),
            pl.BlockSpec((N_EXPERTS, N_EXPERTS), lambda i: (0, 0)),
            pl.BlockSpec((k, tl), lambda i: (0, i)),
            pl.BlockSpec((k, tl), lambda i: (0, i)),
        ],
        out_specs=(
            pl.BlockSpec((k, tl), lambda i: (0, i)),
            pl.BlockSpec((2, N_EXPERTS, LANES), lambda i: (0, 0, 0)),
        ),
        compiler_params=_params(("arbitrary",)),
        name="route_dest",
    )(counts, jnp.asarray(ltri, BF16), topi, rank)


def _pack_bf16_pairs(x):
    n = x.shape[1] // 2
    lo = lax.bitcast_convert_type(x[:, :n].astype(BF16).astype(F32), U32)
    hi = lax.bitcast_convert_type(x[:, n:].astype(BF16).astype(F32), U32)
    return hi | (lo >> 16)


def _unpack_bf16_pairs(w):
    lo = lax.bitcast_convert_type(w << 16, F32)
    hi = lax.bitcast_convert_type(w & jnp.uint32(0xFFFF0000), F32)
    return lo, hi


SC_WINDOW = 128


def _sc_mesh():
    return plsc.VectorSubcoreMesh(core_axis_name="core", subcore_axis_name="subcore")


def _sc_worker(mesh):
    return lax.axis_index("core") * mesh.num_subcores + lax.axis_index("subcore"), mesh.num_cores * mesh.num_subcores


def _sc_scatter_rows(src, dest_rows, n_out):
    n, width = src.shape
    nk = dest_rows.shape[0]
    mesh = _sc_mesh()
    half = SC_WINDOW // 2

    @pl.kernel(out_type=jax.ShapeDtypeStruct((n_out, width), src.dtype), mesh=mesh,
               scratch_types=[pltpu.VMEM((nk, SC_WINDOW), I32), pltpu.VMEM((2, half, width), src.dtype),
                              pltpu.SemaphoreType.DMA((2,))],
               name="sc_scatter_rows")
    def scatter_kernel(src_hbm, idx_hbm, out_hbm, idx_v, rows_v, sem):
        wid, nw = _sc_worker(mesh)
        per = n // SC_WINDOW // nw
        row0 = wid * per * SC_WINDOW

        def load(s, buf):
            return pltpu.make_async_copy(src_hbm.at[pl.ds(row0 + s * half, half)], rows_v.at[buf], sem.at[buf])

        load(0, 0).start()

        @pl.loop(0, per)
        def _(j):
            pltpu.sync_copy(idx_hbm.at[:, pl.ds(row0 + j * SC_WINDOW, SC_WINDOW)], idx_v)
            for h in range(2):
                if h == 0:
                    load(2 * j + 1, 1).start()
                else:
                    @pl.when(j + 1 < per)
                    def _():
                        load(2 * j + 2, 0).start()

                load(2 * j + h, h).wait()
                for k in range(nk):
                    pltpu.sync_copy(rows_v.at[h], out_hbm.at[idx_v.at[k, pl.ds(h * half, half)]])

    return scatter_kernel(src, dest_rows)


def _sc_gather_rows(src, rows):
    n = rows.shape[0]
    width = src.shape[1]
    mesh = _sc_mesh()
    half = SC_WINDOW // 2

    @pl.kernel(out_type=jax.ShapeDtypeStruct((n, width), src.dtype), mesh=mesh,
               scratch_types=[pltpu.VMEM((1, SC_WINDOW), I32), pltpu.VMEM((2, half, width), src.dtype),
                              pltpu.SemaphoreType.DMA((2,))],
               name="sc_gather_rows")
    def gather_kernel(src_hbm, idx_hbm, out_hbm, idx_v, rows_v, sem):
        wid, nw = _sc_worker(mesh)
        per = n // SC_WINDOW // nw
        row0 = wid * per * SC_WINDOW

        def store(s, buf):
            return pltpu.make_async_copy(rows_v.at[buf], out_hbm.at[pl.ds(row0 + s * half, half)], sem.at[buf])

        @pl.loop(0, per)
        def _(j):
            pltpu.sync_copy(idx_hbm.at[:, pl.ds(row0 + j * SC_WINDOW, SC_WINDOW)], idx_v)
            for h in range(2):
                @pl.when(j >= 1)
                def _():
                    store(2 * j + h - 2, h).wait()

                pltpu.sync_copy(src_hbm.at[idx_v.at[0, pl.ds(h * half, half)]], rows_v.at[h])
                store(2 * j + h, h).start()

        for h in range(2):
            store(2 * per - 2 + h, h).wait()

    return gather_kernel(src, rows.reshape(1, n))


def _expert_group_kernel(eb_ref, xs_hbm, wgu_hbm, bgu_ref, wd_hbm, bd_ref, ys_hbm,
                         xbuf, ybuf, wgu_f, wd_f, wgu_bf, wd_bf, act_scr, xsem, ysem, wsem, *, dff, l):
    e = pl.program_id(0)
    ne = pl.num_programs(0)
    b0 = eb_ref[e]
    nb = eb_ref[ne + e]
    rows, half = xbuf.shape[1], xbuf.shape[2]
    group = rows // MOE_ROWS
    nch = (nb + group - 1) // group
    wslot = e % 2

    def w_copies(ex, slot):
        cps = []
        for src, dst, first in ((wgu_hbm, wgu_f, 0), (wd_hbm, wd_f, WEIGHT_CHUNKS)):
            step = dst.shape[1] // WEIGHT_CHUNKS
            for c in range(WEIGHT_CHUNKS):
                cps.append(pltpu.make_async_copy(src.at[l, ex, pl.ds(c * step, step), :],
                                                 dst.at[slot, pl.ds(c * step, step), :], wsem.at[slot, first + c]))
        return cps

    @pl.when(e == 0)
    def _():
        for c, cp in enumerate(w_copies(0, 0)):
            cp.start(priority=c % 2)

    @pl.when(e + 1 < ne)
    def _():
        for c, cp in enumerate(w_copies(e + 1, 1 - wslot)):
            cp.start(priority=c % 2)

    for cp in w_copies(e, wslot):
        cp.wait()

    def x_copy(j, slot):
        start = (b0 + j * group) * MOE_ROWS
        return pltpu.make_async_copy(xs_hbm.at[pl.ds(start, rows), :], xbuf.at[slot], xsem.at[slot])

    def y_copy(j, slot, g):
        start = (b0 + j * group + g) * MOE_ROWS
        return pltpu.make_async_copy(ybuf.at[slot, pl.ds(g * MOE_ROWS, MOE_ROWS), :],
                                     ys_hbm.at[pl.ds(start, MOE_ROWS), :], ysem.at[slot, g])

    def y_each(j, slot, fn):
        for g in range(group):
            @pl.when(j * group + g < nb)
            def _():
                fn(y_copy(j, slot, g))

    @pl.when(nb > 0)
    def _():
        x_copy(0, 0).start(priority=1)
        wgu_bf[:, 0:2 * dff] = wgu_f[wslot].astype(BF16)
        wd_bf[:, 0:2 * half] = wd_f[wslot].astype(BF16)

        def chunk(j, carry):
            slot = j % 2

            @pl.when(j + 1 < nch)
            def _():
                x_copy(j + 1, 1 - slot).start(priority=1)

            x_copy(j, slot).wait()

            @pl.when(j >= 2)
            def _():
                y_each(j - 2, slot, lambda cp: cp.wait())

            lo, hi = _unpack_bf16_pairs(xbuf[slot])
            xb = jnp.concatenate([lo.astype(BF16), hi.astype(BF16)], axis=1)
            for c0 in range(0, dff, EXPERT_COLS):
                gate = _dot(xb, wgu_bf[:, c0:c0 + EXPERT_COLS]) + bgu_ref[:, c0:c0 + EXPERT_COLS]
                up = (_dot(xb, wgu_bf[:, dff + c0:dff + c0 + EXPERT_COLS])
                      + bgu_ref[:, dff + c0:dff + c0 + EXPERT_COLS])
                glu = jnp.minimum(gate, SWIGLU_LIMIT)
                lin = jnp.clip(up, -SWIGLU_LIMIT, SWIGLU_LIMIT)
                act_scr[:, c0:c0 + EXPERT_COLS] = (glu * _sigmoid(SWIGLU_ALPHA * glu) * (lin + 1.0)).astype(BF16)
            ybuf[slot] = _pack_bf16_pairs(_dot(act_scr[...], wd_bf[:, 0:2 * half]) + bd_ref[...])
            y_each(j, slot, lambda cp: cp.start(priority=1))
            return carry

        lax.fori_loop(0, nch, chunk, 0)

        @pl.when(nch >= 2)
        def _():
            y_each(nch - 2, nch % 2, lambda cp: cp.wait())

        y_each(nch - 1, (nch - 1) % 2, lambda cp: cp.wait())


def _block_experts(eblk, xs, w_gu, b_gu4, w_down, b_down4, l):
    cap, half = xs.shape
    d = 2 * half
    dff = w_down.shape[2]
    wsel = lambda e, eb: (l, e, 0, 0)
    grid_spec = pltpu.PrefetchScalarGridSpec(
        num_scalar_prefetch=1,
        grid=(N_EXPERTS,),
        in_specs=[
            pl.BlockSpec(memory_space=pl.ANY),
            pl.BlockSpec(memory_space=pl.ANY),
            pl.BlockSpec((None, None, 1, 2 * dff), wsel),
            pl.BlockSpec(memory_space=pl.ANY),
            pl.BlockSpec((None, None, 1, d), wsel),
        ],
        out_specs=pl.BlockSpec(memory_space=pl.ANY),
        scratch_shapes=[
            pltpu.VMEM((2, EXPERT_GROUP * MOE_ROWS, half), U32),
            pltpu.VMEM((2, EXPERT_GROUP * MOE_ROWS, half), U32),
            pltpu.VMEM((2, d, 2 * dff), F32),
            pltpu.VMEM((2, dff, d), F32),
            pltpu.VMEM((d, 2 * dff + LANES), BF16),
            pltpu.VMEM((dff, d + LANES), BF16),
            pltpu.VMEM((EXPERT_GROUP * MOE_ROWS, dff), BF16),
            pltpu.SemaphoreType.DMA((2,)),
            pltpu.SemaphoreType.DMA((2, EXPERT_GROUP)),
            pltpu.SemaphoreType.DMA((2, 2 * WEIGHT_CHUNKS)),
        ],
    )
    return pl.pallas_call(
        functools.partial(_expert_group_kernel, dff=dff, l=l),
        out_shape=jax.ShapeDtypeStruct((cap - (EXPERT_GROUP - 1) * MOE_ROWS, half), U32),
        grid_spec=grid_spec,
        compiler_params=_params(("arbitrary",)),
        name="moe_expert_groups",
    )(eblk, xs, w_gu, b_gu4, w_down, b_down4)


def _combine_kernel(*refs, d, final):
    y_refs = refs[:TOP_K]
    gt_ref, x_ref, mod_ref = refs[TOP_K:TOP_K + 3]
    xo_ref = refs[-1]
    half = d // 2
    gt = gt_ref[...]
    acc_lo = acc_hi = None
    for k in range(TOP_K):
        lo, hi = _unpack_bf16_pairs(y_refs[k][...])
        g = gt[:, k:k + 1]
        acc_lo = g * lo if acc_lo is None else acc_lo + g * lo
        acc_hi = g * hi if acc_hi is None else acc_hi + g * hi
    x_lo = x_ref[:, 0:half] + mod_ref[:, 5 * d:5 * d + half] * acc_lo
    x_hi = x_ref[:, half:d] + mod_ref[:, 5 * d + half:6 * d] * acc_hi
    if final:
        gain = refs[TOP_K + 3][...]
        ssq = jnp.sum(x_lo * x_lo, axis=-1, keepdims=True) + jnp.sum(x_hi * x_hi, axis=-1, keepdims=True)
        inv = lax.rsqrt(ssq * (1.0 / d) + EPS)
        x_lo = x_lo * inv * gain[:, 0:half]
        x_hi = x_hi * inv * gain[:, half:d]
    xo_ref[:, 0:half] = x_lo
    xo_ref[:, half:d] = x_hi


def _combine(y4p, gates_t, x2, mod3, l, seq, depth, row0, prev, final_gain=None):
    t, d = x2.shape
    n = gates_t.shape[0]
    tm = min(PROJ_TILE, seq)
    nb = mod3.shape[0] // depth
    nt = n // tm
    t0 = row0 // tm
    final = final_gain is not None

    def slot(k):
        return pl.BlockSpec((tm, d // 2), lambda i: (k * nt + i, 0))

    specs = [slot(k) for k in range(TOP_K)] + [
        pl.BlockSpec((tm, TOP_K), lambda i: (i, 0)),
        pl.BlockSpec((tm, d), lambda i: (t0 + i, 0)),
        pl.BlockSpec((None, 1, N_MOD * d), lambda i: (l * nb + (row0 + i * tm) // seq, 0, 0)),
    ]
    args = [y4p] * TOP_K + [gates_t, x2, mod3]
    if final:
        specs.append(pl.BlockSpec((1, d), lambda i: (0, 0)))
        args.append(final_gain.reshape(1, d))
    aliases = {}
    if prev is not None:
        specs.append(pl.BlockSpec(memory_space=pl.ANY))
        aliases = {len(args): 0}
        args.append(prev)
    return pl.pallas_call(
        functools.partial(_combine_kernel, d=d, final=final),
        out_shape=jax.ShapeDtypeStruct((t, d), F32),
        grid=(nt,),
        in_specs=specs,
        out_specs=pl.BlockSpec((tm, d), lambda i: (t0 + i, 0)),
        input_output_aliases=aliases,
        compiler_params=_params(("arbitrary",)),
        name="moe_combine",
    )(*args)


def _mixer_layer(x2, mod3, tables, lower, consts, p, l, batch, seq):
    ra, rb, rc = tables
    qa, ka2, va2, hg = _inproj(x2, mod3, p["norm1"], p["w_in"], ra, rb, rc, l, seq)
    a = _attention(qa, ka2, va2, p["attn_sink"], p["attn_norm"], l, batch, seq)
    o = _hgrn(hg, lower, p["hg_norm"], consts, l, batch, seq)
    return a, o


def _moe_layer(a, o, x2, mod3, p, l, seq, depth, final_gain=None):
    t, d = x2.shape
    xn, h2p, topi, gates = _outproj(a, o, x2, mod3, p["norm2"], p["w_out"], p["w_router_t"], p["b_router"], l, seq)
    rank, counts = _ranks(topi)
    nblocks = (t * TOP_K + N_EXPERTS * MOE_ROWS) // MOE_ROWS
    dest, eblk = _destinations(counts, topi, rank)
    xs = _sc_scatter_rows(h2p, dest, (nblocks + EXPERT_GROUP - 1) * MOE_ROWS)
    ys = _block_experts(eblk[:, :, 0].reshape(-1), xs, p["w_gu"], p["b_gu"], p["w_down"], p["b_down"], l)
    gates_t = gates.T
    out = None
    for part in range(MOE_COMBINE_PARTS):
        rows = slice(part * (t // MOE_COMBINE_PARTS), (part + 1) * (t // MOE_COMBINE_PARTS))
        y4p = _sc_gather_rows(ys, dest[:, rows].reshape(-1))
        out = _combine(y4p, gates_t[rows], xn, mod3, l, seq, depth, rows.start, out, final_gain)
    return out


def kernel(x, c, positions, w_ada, b_ada, norm1, w_in, attn_sink, attn_norm, hg_lb_logits, hg_norm, w_out, norm2,
           w_router, b_router, w_gu, b_gu, w_down, b_down, final_norm):
    batch, seq, d = x.shape
    depth = w_ada.shape[0]
    t = batch * seq
    p = {
        "norm1": norm1.reshape(depth, 1, d),
        "w_in": w_in.astype(BF16),
        "attn_sink": attn_sink.astype(F32),
        "attn_norm": attn_norm.reshape(depth, 1, ATT_WIDTH),
        "hg_norm": hg_norm.reshape(depth, 1, HG_HEAD_DIM),
        "w_out": w_out.astype(BF16),
        "norm2": norm2.reshape(depth, 1, d),
        "w_router_t": jnp.swapaxes(w_router, 1, 2),
        "b_router": b_router.reshape(depth, N_EXPERTS, 1),
        "w_gu": w_gu,
        "b_gu": b_gu.reshape(depth, N_EXPERTS, 1, b_gu.shape[-1]),
        "w_down": w_down,
        "b_down": b_down.reshape(depth, N_EXPERTS, 1, d),
    }
    mod3 = _ada_all(c, w_ada, b_ada).reshape(depth * batch, 1, N_MOD * d)
    lower = _lower_bounds(hg_lb_logits)
    tables = _rope_tables(positions)
    consts = tuple(jnp.asarray(m) for m in _hgrn_constants())
    x2 = x.reshape(t, d)
    for l in range(depth):
        a, o = _mixer_layer(x2, mod3, tables, lower, consts, p, l, batch, seq)
        x2 = _moe_layer(a, o, x2, mod3, p, l, seq, depth, final_norm if l == depth - 1 else None)
    return x2.reshape(batch, seq, d)
```

```python
import functools

import numpy as np
import jax
import jax.numpy as jnp
from jax import lax
from jax.experimental import pallas as pl
from jax.experimental.pallas import tpu as pltpu
from jax.experimental.pallas import tpu_sc as plsc

F32 = jnp.float32
BF16 = jnp.bfloat16
I32 = jnp.int32
U32 = jnp.uint32

ATT_HEADS = 8
ATT_KV_HEADS = 2
ATT_HEAD_DIM = 64
ATT_WIDTH = ATT_HEADS * ATT_HEAD_DIM
KV_WIDTH = ATT_KV_HEADS * ATT_HEAD_DIM
WINDOW = 128
ATT_BLOCK = 128
ROPE_THETA = 500000.0
ROPE_DIM = ATT_HEAD_DIM // 4
HG_HEADS = 4
HG_HEAD_DIM = 128
HG_WIDTH = HG_HEADS * HG_HEAD_DIM
N_EXPERTS = 32
TOP_K = 4
SWIGLU_ALPHA = 1.702
SWIGLU_LIMIT = 7.0
N_MOD = 6
EPS = 1e-6
NEG_INF = -1e30
LB_FLOOR = 1e-30

LANES = 128
HG_CHUNK = 64
HG_LEVELS = 6
HG_UNROLL = 8
MOE_ROWS = 256
EXPERT_GROUP = 1
WEIGHT_CHUNKS = 4
EXPERT_COLS = 512
MOE_COMBINE_PARTS = 2
PROJ_TILE = 512
RANK_TILE = 512
VMEM_LIMIT = 56 * 1024 * 1024


def _dot(a, b):
    return jnp.dot(a, b, preferred_element_type=F32)


def _dot_nt(a, b):
    return lax.dot_general(a, b, (((1,), (1,)), ((), ())), preferred_element_type=F32)


def _dot_tn(a, b):
    return lax.dot_general(a, b, (((0,), (0,)), ((), ())), preferred_element_type=F32)


def _split3(x):
    hi = x.astype(BF16)
    r1 = x - hi.astype(F32)
    mid = r1.astype(BF16)
    lo = (r1 - mid.astype(F32)).astype(BF16)
    return hi, mid, lo


def _dot_f32_nt(a, b):
    ah, am, _ = _split3(a)
    bh, bm, _ = _split3(b)
    return _dot_nt(ah, bh) + _dot_nt(ah, bm) + _dot_nt(am, bh)


def _dot_f32(a, b):
    ah, am, _ = _split3(a)
    bh, bm, _ = _split3(b)
    return _dot(ah, bh) + _dot(ah, bm) + _dot(am, bh)


def _sigmoid(x):
    return 1.0 / (1.0 + jnp.exp(-x))


def _params(sem=None):
    return pltpu.CompilerParams(dimension_semantics=sem, vmem_limit_bytes=VMEM_LIMIT)


def _ada_kernel(c_ref, w_ref, b_ref, o_ref):
    c = c_ref[...]
    cond = c * _sigmoid(c)
    o_ref[...] = _dot_f32(cond, w_ref[...]) + b_ref[...]


def _ada_all(c, w_ada, b_ada):
    depth, d, n = w_ada.shape
    b = c.shape[0]
    nt = n // d
    return pl.pallas_call(
        _ada_kernel,
        out_shape=jax.ShapeDtypeStruct((depth, b, n), F32),
        grid=(depth, nt),
        in_specs=[
            pl.BlockSpec((b, d), lambda l, j: (0, 0)),
            pl.BlockSpec((None, d, d), lambda l, j: (l, 0, j)),
            pl.BlockSpec((None, 1, d), lambda l, j: (l, 0, j)),
        ],
        out_specs=pl.BlockSpec((None, b, d), lambda l, j: (l, 0, j)),
        compiler_params=_params(("arbitrary", "arbitrary")),
        name="ada_mod",
    )(c, w_ada, b_ada.reshape(depth, 1, n))


def _lb_kernel(x_ref, o_ref):
    depth = x_ref.shape[0]
    xs = [x_ref[l] for l in range(depth)]
    m = xs[0]
    for l in range(1, depth):
        m = jnp.maximum(m, xs[l])
    es = [jnp.exp(v - m) for v in xs]
    den = es[0]
    for l in range(1, depth):
        den = den + es[l]
    ps = [e / den for e in es]
    run = ps[0]
    o_ref[0] = run - ps[0]
    for l in range(1, depth):
        run = run + ps[l]
        o_ref[l] = run - ps[0]


def _lower_bounds(hg_lb_logits):
    return pl.pallas_call(
        _lb_kernel,
        out_shape=jax.ShapeDtypeStruct(hg_lb_logits.shape, F32),
        name="hg_lower_bounds",
    )(hg_lb_logits.astype(F32))


def _rope_kernel(pos_ref, invf_ref, a_ref, b_ref, c_ref):
    pos = pos_ref[...].astype(F32)
    ang = pos * invf_ref[...]
    cs = jnp.cos(ang)
    sn = jnp.sin(ang)
    lane = lax.broadcasted_iota(I32, ang.shape, 1) & (ATT_HEAD_DIM - 1)
    half = ROPE_DIM // 2
    first = lane < half
    second = (lane >= half) & (lane < ROPE_DIM)
    a_ref[...] = jnp.where(first | second, cs, 1.0)
    b_ref[...] = jnp.where(first, -sn, 0.0)
    c_ref[...] = jnp.where(second, sn, 0.0)


def _rope_tables(positions):
    t = positions.size
    half = ROPE_DIM // 2
    inv = (np.float32(ROPE_THETA) ** (-(np.arange(half, dtype=np.float32) * np.float32(2.0) / np.float32(ROPE_DIM)))).astype(np.float32)
    lane = np.arange(LANES) % ATT_HEAD_DIM
    pat = np.where(lane < ROPE_DIM, inv[lane % half], 0.0).astype(np.float32).reshape(1, LANES)
    tm = min(t, 2048)
    shp = jax.ShapeDtypeStruct((t, LANES), F32)
    spec = pl.BlockSpec((tm, LANES), lambda i: (i, 0))
    return pl.pallas_call(
        _rope_kernel,
        out_shape=(shp, shp, shp),
        grid=(t // tm,),
        in_specs=[pl.BlockSpec((tm, 1), lambda i: (i, 0)), pl.BlockSpec((1, LANES), lambda i: (0, 0))],
        out_specs=(spec, spec, spec),
        compiler_params=_params(("arbitrary",)),
        name="rope_tables",
    )(positions.reshape(t, 1).astype(I32), jnp.asarray(pat))


def _rms_mod(x, gain, scale, shift):
    ms = jnp.mean(x * x, axis=-1, keepdims=True)
    return (x * lax.rsqrt(ms + EPS) * gain) * (1.0 + scale) + shift


def _rope_apply(x, a, b, c):
    half = ROPE_DIM // 2
    return x * a + pltpu.roll(x, LANES - half, 1) * b + pltpu.roll(x, half, 1) * c


def _inproj_kernel(x_ref, mod_ref, n1_ref, w_ref, ra_ref, rb_ref, rc_ref, qa_ref, ka_ref, va_ref, hg_ref, *, d):
    mod = mod_ref[...]
    h = _rms_mod(x_ref[...], n1_ref[...], mod[:, d:2 * d], mod[:, 0:d]).astype(BF16)
    a, b, c = ra_ref[...], rb_ref[...], rc_ref[...]
    kvw = ATT_WIDTH + 2 * KV_WIDTH
    pa = _dot(h, w_ref[:, 0:kvw])
    scale = ATT_HEAD_DIM ** -0.5
    for g in range(ATT_WIDTH // LANES):
        qg = _rope_apply(pa[:, g * LANES:(g + 1) * LANES], a, b, c)
        qa_ref[:, g * LANES:(g + 1) * LANES] = (qg * scale).astype(BF16)
    k = _rope_apply(pa[:, ATT_WIDTH:ATT_WIDTH + KV_WIDTH], a, b, c)
    v = pa[:, ATT_WIDTH + KV_WIDTH:kvw]
    ka_ref[:, 0:LANES] = k.astype(BF16)
    ka_ref[:, LANES:2 * LANES] = pltpu.roll(k, ATT_HEAD_DIM, 1).astype(BF16)
    va_ref[:, 0:LANES] = v.astype(BF16)
    va_ref[:, LANES:2 * LANES] = pltpu.roll(v, ATT_HEAD_DIM, 1).astype(BF16)
    for g in range(5):
        lo = kvw + g * HG_WIDTH
        hg_ref[:, g * HG_WIDTH:(g + 1) * HG_WIDTH] = _dot(h, w_ref[:, lo:lo + HG_WIDTH])


def _inproj(x2, mod3, norm1, w_in_bf, ra, rb, rc, l, seq):
    t, d = x2.shape
    n_in = w_in_bf.shape[-1]
    tm = min(PROJ_TILE, seq)
    nb = mod3.shape[0] // norm1.shape[0]
    tok = lambda i: (i, 0)
    return pl.pallas_call(
        functools.partial(_inproj_kernel, d=d),
        out_shape=(
            jax.ShapeDtypeStruct((t, ATT_WIDTH), BF16),
            jax.ShapeDtypeStruct((t, 2 * KV_WIDTH), BF16),
            jax.ShapeDtypeStruct((t, 2 * KV_WIDTH), BF16),
            jax.ShapeDtypeStruct((t, 5 * HG_WIDTH), F32),
        ),
        grid=(t // tm,),
        in_specs=[
            pl.BlockSpec((tm, d), tok),
            pl.BlockSpec((None, 1, N_MOD * d), lambda i: (l * nb + (i * tm) // seq, 0, 0)),
            pl.BlockSpec((None, 1, d), lambda i: (l, 0, 0)),
            pl.BlockSpec((None, d, n_in), lambda i: (l, 0, 0)),
            pl.BlockSpec((tm, LANES), tok),
            pl.BlockSpec((tm, LANES), tok),
            pl.BlockSpec((tm, LANES), tok),
        ],
        out_specs=(
            pl.BlockSpec((tm, ATT_WIDTH), tok),
            pl.BlockSpec((tm, 2 * KV_WIDTH), tok),
            pl.BlockSpec((tm, 2 * KV_WIDTH), tok),
            pl.BlockSpec((tm, 5 * HG_WIDTH), tok),
        ),
        compiler_params=_params(("arbitrary",)),
        name="in_proj",
    )(x2, mod3, norm1, w_in_bf, ra, rb, rc)


def _attn_kernel(sink_ref, q_ref, kp_ref, kc_ref, kn_ref, vp_ref, vc_ref, vn_ref, gain_ref, o_ref, *, l, seq):
    n = pl.program_id(1)
    blk = ATT_BLOCK
    k2 = jnp.concatenate([kp_ref[...], kc_ref[...], kn_ref[...]], axis=0)
    v2 = jnp.concatenate([vp_ref[...], vc_ref[...], vn_ref[...]], axis=0)
    lane = lax.broadcasted_iota(I32, (3 * blk, LANES), 1)
    lo_half = lane < ATT_HEAD_DIM
    zero = jnp.zeros((3 * blk, LANES), BF16)
    ka, kb = k2[:, 0:LANES], k2[:, LANES:2 * LANES]
    va, vb = v2[:, 0:LANES], v2[:, LANES:2 * LANES]
    kz = [[jnp.where(lo_half, ka, zero), jnp.where(lo_half, zero, kb)],
          [jnp.where(lo_half, kb, zero), jnp.where(lo_half, zero, ka)]]
    vz = [[jnp.where(lo_half, va, zero), jnp.where(lo_half, zero, vb)],
          [jnp.where(lo_half, vb, zero), jnp.where(lo_half, zero, va)]]
    qpos = n * blk + lax.broadcasted_iota(I32, (blk, 3 * blk), 0)
    kpos = (n - 1) * blk + lax.broadcasted_iota(I32, (blk, 3 * blk), 1)
    valid = (jnp.abs(qpos - kpos) <= WINDOW) & (kpos >= 0) & (kpos < seq)
    valid2 = jnp.concatenate([valid, valid], axis=0)
    upper = lax.broadcasted_iota(I32, (2 * blk, 1), 0) < blk
    outs = []
    for j in range(ATT_KV_HEADS):
        qs = jnp.concatenate([q_ref[:, 2 * j * LANES:(2 * j + 1) * LANES],
                              q_ref[:, (2 * j + 1) * LANES:(2 * j + 2) * LANES]], axis=0)
        acc = None
        for half in range(2):
            s = _dot_nt(qs, kz[j][half])
            s = jnp.where(valid2, s, NEG_INF)
            sink = jnp.where(upper, sink_ref[l, 4 * j + half], sink_ref[l, 4 * j + 2 + half])
            mx = jnp.maximum(jnp.max(s, axis=-1, keepdims=True), sink)
            p = jnp.exp(s - mx)
            den = jnp.sum(p, axis=-1, keepdims=True) + jnp.exp(sink - mx)
            p = (p * (1.0 / den)).astype(BF16)
            pv = _dot(p, vz[j][half])
            acc = pv if acc is None else acc + pv
        outs.append(acc[0:blk])
        outs.append(acc[blk:2 * blk])
    o = jnp.concatenate(outs, axis=-1)
    ms = jnp.mean(o * o, axis=-1, keepdims=True)
    o_ref[...] = (o * lax.rsqrt(ms + EPS) * gain_ref[...]).astype(BF16)


def _attention(qa, ka2, va2, attn_sink, attn_norm3, l, batch, seq):
    t = qa.shape[0]
    blk = ATT_BLOCK
    nb = seq // blk
    cur = lambda b, n: (b * nb + n, 0)
    prev = lambda b, n: (b * nb + jnp.maximum(n - 1, 0), 0)
    nxt = lambda b, n: (b * nb + jnp.minimum(n + 1, nb - 1), 0)
    kvspec = lambda f: pl.BlockSpec((blk, 2 * KV_WIDTH), f)
    return pl.pallas_call(
        functools.partial(_attn_kernel, l=l, seq=seq),
        out_shape=jax.ShapeDtypeStruct((t, ATT_WIDTH), BF16),
        grid=(batch, nb),
        in_specs=[
            pl.BlockSpec(memory_space=pltpu.SMEM),
            pl.BlockSpec((blk, ATT_WIDTH), cur),
            kvspec(prev), kvspec(cur), kvspec(nxt),
            kvspec(prev), kvspec(cur), kvspec(nxt),
            pl.BlockSpec((None, 1, ATT_WIDTH), lambda b, n: (l, 0, 0)),
        ],
        out_specs=pl.BlockSpec((blk, ATT_WIDTH), cur),
        compiler_params=_params(("arbitrary", "arbitrary")),
        name="window_attn",
    )(attn_sink, qa, ka2, ka2, ka2, va2, va2, va2, attn_norm3)


def _hgrn_constants():
    c, nl = HG_CHUNK, HG_LEVELS
    r = np.arange(c)
    masks = []
    for lev in range(nl):
        m = 1 << lev
        parent = r // (2 * m)
        upper = r >= parent * 2 * m + m
        masks.append((parent[:, None] == parent[None, :]) & upper[:, None] & (~upper)[None, :])
    masks.append(np.eye(c, dtype=bool))
    kf = np.stack(masks).astype(np.float32)
    kb = np.stack([mk[::-1, ::-1] for mk in masks]).astype(np.float32)
    return kf, kb


def _chunk_decays(logf, reverse):
    c = logf.shape[0]
    sub = 8
    nv = c // sub
    row = lax.broadcasted_iota(I32, (sub, LANES), 0)
    parts = []
    for v in range(nv):
        x = logf[sub * v:sub * (v + 1), :]
        for s in (1, 2, 4):
            if reverse:
                x = x + jnp.where(row < sub - s, pltpu.roll(x, sub - s, 0), 0.0)
            else:
                x = x + jnp.where(row >= s, pltpu.roll(x, s, 0), 0.0)
        parts.append(x)
    b = [None] * nv
    order = list(reversed(range(nv))) if reverse else list(range(nv))
    edge = 0 if reverse else sub - 1
    carry = None
    for v in order:
        b[v] = parts[v] if carry is None else parts[v] + carry
        carry = b[v][edge:edge + 1, :]
    b_last = carry

    def anchor_row(v, r):
        return jnp.broadcast_to(b[v][r:r + 1, :], (sub, LANES))

    odd = (row & 1) == 1
    levels = []
    for lev in range(HG_LEVELS):
        m = 1 << lev
        pieces = []
        for v in range(nv):
            if m == 1:
                a = jnp.where(odd, pltpu.roll(b[v], 1, 0), b[v]) if reverse else jnp.where(odd, b[v], pltpu.roll(b[v], sub - 1, 0))
            elif m == 2:
                lo, hi = (1, 5) if reverse else (2, 6)
                a = jnp.where(row < 4, anchor_row(v, lo), anchor_row(v, hi))
            elif m == 4:
                a = anchor_row(v, 3 if reverse else 4)
            else:
                mv = m // sub
                first = (v // (2 * mv)) * 2 * mv
                a = anchor_row(first + mv - 1, sub - 1) if reverse else anchor_row(first + mv, 0)
            pieces.append(jnp.exp(-jnp.abs(b[v] - a)))
        levels.append(jnp.concatenate(pieces, axis=0))
    eb = jnp.concatenate([jnp.exp(bv) for bv in b], axis=0)
    erem = jnp.concatenate([jnp.exp(b_last - bv) for bv in b], axis=0)
    return levels, eb, erem


def _hgrn_kernel(q_ref, ff_ref, fb_ref, i_ref, g_ref, lb_ref, gn_ref, kf_ref, kb_ref,
                 o_ref, of_scr, ob_scr, st_scr, *, seq):
    c, nl = HG_CHUNK, HG_LEVELS
    nc = seq // c
    lb = lb_ref[...]
    st_scr[...] = jnp.zeros(st_scr.shape, F32)

    unroll = HG_UNROLL if nc % HG_UNROLL == 0 else 1
    dirs = (
        dict(f_ref=ff_ref, lbrow=lb[0:1, :], k_ref=kf_ref, o_scr=of_scr, last_row=c - 1, d=0),
        dict(f_ref=fb_ref, lbrow=lb[1:2, :], k_ref=kb_ref, o_scr=ob_scr, last_row=0, d=1),
    )

    def body(i, carry):
        work = []
        for u in range(unroll):
            cf = i * unroll + u
            work.append((dirs[0], pl.ds(pl.multiple_of(cf * c, c), c)))
            work.append((dirs[1], pl.ds(pl.multiple_of((nc - 1 - cf) * c, c), c)))
        gates = []
        for dr, rows in work:
            lbf = jnp.maximum(dr["lbrow"], LB_FLOOR)
            oml = 1.0 - dr["lbrow"]
            f = dr["f_ref"][rows, :]
            e = jnp.exp(-jnp.abs(f))
            r = 1.0 / (1.0 + e)
            er = e * r
            pos = f >= 0.0
            logf = jnp.log(lbf + oml * jnp.where(pos, r, er))
            kk = oml * jnp.where(pos, er, r)
            qh = q_ref[rows, :]
            gates.append((logf, kk.astype(BF16), (qh * _sigmoid(qh)).astype(BF16), i_ref[rows, :].astype(BF16)))
        decays = [_chunk_decays(g[0], dr["d"] == 1) for (dr, _), g in zip(work, gates)]
        amat = [dr["k_ref"][nl] * _dot_nt(g[2], g[1]) for (dr, _), g in zip(work, gates)]
        for lev in range(nl):
            for j, ((dr, _), g) in enumerate(zip(work, gates)):
                gl = decays[j][0][lev].astype(BF16)
                amat[j] = amat[j] + dr["k_ref"][lev] * _dot_nt(g[2] * gl, g[1] * gl)
        intra = [_dot(amat[j].astype(BF16), g[3]) for j, g in enumerate(gates)]
        upd = [_dot_tn(g[3], g[1] * decays[j][2].astype(BF16)) for j, g in enumerate(gates)]
        st = [st_scr[0], st_scr[1]]
        for j, ((dr, rows), g) in enumerate(zip(work, gates)):
            eb = decays[j][1]
            d = dr["d"]
            dr["o_scr"][rows, :] = _dot_nt(g[2] * eb.astype(BF16), st[d].astype(BF16)) + intra[j]
            st[d] = st[d] * eb[dr["last_row"]:dr["last_row"] + 1, :] + upd[j]
        st_scr[0] = st[0]
        st_scr[1] = st[1]
        return carry

    lax.fori_loop(0, nc // unroll, body, 0)

    ep = min(256, seq)
    gn = gn_ref[...]

    def epilogue(j, carry):
        rows = pl.ds(pl.multiple_of(j * ep, ep), ep)
        o = of_scr[rows, :] + ob_scr[rows, :]
        y = o * lax.rsqrt(jnp.mean(o * o, axis=-1, keepdims=True) + EPS) * gn
        g = g_ref[rows, :]
        o_ref[rows, :] = (y * (g * _sigmoid(g))).astype(BF16)
        return carry

    lax.fori_loop(0, seq // ep, epilogue, 0)


def _hgrn(hg, lower, hg_norm3, consts, l, batch, seq):
    t = hg.shape[0]
    kf, kb = consts
    hd = HG_HEAD_DIM

    def col(g):
        return pl.BlockSpec((seq, hd), lambda b, h: (b, g * HG_HEADS + h))

    full3 = lambda a: pl.BlockSpec(a.shape, lambda b, h: (0, 0, 0))
    return pl.pallas_call(
        functools.partial(_hgrn_kernel, seq=seq),
        out_shape=jax.ShapeDtypeStruct((t, HG_WIDTH), BF16),
        grid=(batch, HG_HEADS),
        in_specs=[
            col(0), col(1), col(2), col(3), col(4),
            pl.BlockSpec((None, 2, hd), lambda b, h: (l, 0, h)),
            pl.BlockSpec((None, 1, hd), lambda b, h: (l, 0, 0)),
            full3(kf), full3(kb),
        ],
        out_specs=pl.BlockSpec((seq, hd), lambda b, h: (b, h)),
        scratch_shapes=[
            pltpu.VMEM((seq, hd), F32),
            pltpu.VMEM((seq, hd), F32),
            pltpu.VMEM((2, hd, hd), F32),
        ],
        compiler_params=_params(("arbitrary", "arbitrary")),
        name="hgrn2_scan",
    )(hg, hg, hg, hg, hg, lower, hg_norm3, kf, kb)


def _outproj_kernel(a_ref, o_ref, x_ref, mod_ref, n2_ref, w_ref, wr_ref, br_ref,
                    xo_ref, h2_ref, ti_ref, gt_ref, *, d):
    mod = mod_ref[...]
    y = _dot(a_ref[...], w_ref[0:ATT_WIDTH, :]) + _dot(o_ref[...], w_ref[ATT_WIDTH:ATT_WIDTH + HG_WIDTH, :])
    xn = x_ref[...] + mod[:, 2 * d:3 * d] * y
    xo_ref[...] = xn
    h2 = _rms_mod(xn, n2_ref[...], mod[:, 4 * d:5 * d], mod[:, 3 * d:4 * d])
    h2_ref[...] = _pack_bf16_pairs(h2)
    lg = _dot_f32_nt(wr_ref[...], h2) + br_ref[...]
    eidx = lax.broadcasted_iota(I32, lg.shape, 0)
    vals, idxs = [], []
    for _ in range(TOP_K):
        mx = jnp.max(lg, axis=0, keepdims=True)
        sel = jnp.min(jnp.where(lg == mx, eidx, N_EXPERTS), axis=0, keepdims=True)
        vals.append(mx)
        idxs.append(sel)
        lg = jnp.where(eidx == sel, -jnp.inf, lg)
    ex = [jnp.exp(v - vals[0]) for v in vals]
    den = ex[0]
    for e in ex[1:]:
        den = den + e
    inv = 1.0 / den
    ti_ref[...] = jnp.concatenate(idxs, axis=0)
    gt_ref[...] = jnp.concatenate([e * inv for e in ex], axis=0)


def _outproj(a, o, x2, mod3, norm2, w_out_bf, w_router_t, b_router3, l, seq):
    t, d = x2.shape
    tm = min(PROJ_TILE, seq)
    nb = mod3.shape[0] // norm2.shape[0]
    tok = lambda i: (i, 0)
    lane_tok = lambda i: (0, i)
    return pl.pallas_call(
        functools.partial(_outproj_kernel, d=d),
        out_shape=(
            jax.ShapeDtypeStruct((t, d), F32),
            jax.ShapeDtypeStruct((t, d // 2), U32),
            jax.ShapeDtypeStruct((TOP_K, t), I32),
            jax.ShapeDtypeStruct((TOP_K, t), F32),
        ),
        grid=(t // tm,),
        in_specs=[
            pl.BlockSpec((tm, ATT_WIDTH), tok),
            pl.BlockSpec((tm, HG_WIDTH), tok),
            pl.BlockSpec((tm, d), tok),
            pl.BlockSpec((None, 1, N_MOD * d), lambda i: (l * nb + (i * tm) // seq, 0, 0)),
            pl.BlockSpec((None, 1, d), lambda i: (l, 0, 0)),
            pl.BlockSpec((None, ATT_WIDTH + HG_WIDTH, d), lambda i: (l, 0, 0)),
            pl.BlockSpec((None, N_EXPERTS, d), lambda i: (l, 0, 0)),
            pl.BlockSpec((None, N_EXPERTS, 1), lambda i: (l, 0, 0)),
        ],
        out_specs=(
            pl.BlockSpec((tm, d), tok),
            pl.BlockSpec((tm, d // 2), tok),
            pl.BlockSpec((TOP_K, tm), lane_tok),
            pl.BlockSpec((TOP_K, tm), lane_tok),
        ),
        compiler_params=_params(("arbitrary",)),
        name="out_proj_router",
    )(a, o, x2, mod3, norm2, w_out_bf, w_router_t, b_router3)


def _rank_kernel(ti_ref, tri_ref, rank_ref, cnt_ref, carry_scr):
    @pl.when(pl.program_id(0) == 0)
    def _():
        carry_scr[...] = jnp.zeros(carry_scr.shape, F32)

    ti = ti_ref[...]
    tl = ti.shape[1]
    eidx = lax.broadcasted_iota(I32, (N_EXPERTS, tl), 0)
    carry = carry_scr[...]
    rows = []
    for k in range(TOP_K):
        oh = eidx == ti[k:k + 1, :]
        ohf = jnp.where(oh, 1.0, 0.0)
        pre = _dot(ohf.astype(BF16), tri_ref[...])
        rows.append(jnp.sum(jnp.where(oh, carry + pre, 0.0), axis=0, keepdims=True))
        carry = carry + jnp.sum(ohf, axis=1, keepdims=True)
    carry_scr[...] = carry
    rank_ref[...] = jnp.concatenate(rows, axis=0).astype(I32)
    cnt_ref[...] = jnp.broadcast_to(carry, cnt_ref.shape)


def _ranks(topi):
    k, t = topi.shape
    tl = min(RANK_TILE, t)
    tri = np.triu(np.ones((tl, tl), np.float32), 1)
    return pl.pallas_call(
        _rank_kernel,
        out_shape=(jax.ShapeDtypeStruct((k, t), I32), jax.ShapeDtypeStruct((N_EXPERTS, LANES), F32)),
        grid=(t // tl,),
        in_specs=[pl.BlockSpec((k, tl), lambda i: (0, i)), pl.BlockSpec((tl, tl), lambda i: (0, 0))],
        out_specs=(pl.BlockSpec((k, tl), lambda i: (0, i)), pl.BlockSpec((N_EXPERTS, LANES), lambda i: (0, 0))),
        scratch_shapes=[pltpu.VMEM((N_EXPERTS, 1), F32)],
        compiler_params=_params(("arbitrary",)),
        name="route_rank",
    )(topi, jnp.asarray(tri, BF16))


def _dest_kernel(cnt_ref, ltri_ref, ti_ref, rank_ref, dest_ref, eblk_ref):
    cnt = cnt_ref[...]
    nblk = jnp.floor((cnt + (MOE_ROWS - 1)) * (1.0 / MOE_ROWS))
    pstart_b = _dot(ltri_ref[...], nblk.astype(BF16))
    pstart = (pstart_b[:, 0:1] * MOE_ROWS).astype(I32)
    ti = ti_ref[...]
    tl = ti.shape[1]
    eidx = lax.broadcasted_iota(I32, (N_EXPERTS, tl), 0)
    rows = []
    for k in range(TOP_K):
        oh = eidx == ti[k:k + 1, :]
        rows.append(jnp.sum(jnp.where(oh, pstart, 0), axis=0, keepdims=True))
    dest_ref[...] = jnp.concatenate(rows, axis=0) + rank_ref[...]
    eblk_ref[0] = pstart_b.astype(I32)
    eblk_ref[1] = nblk.astype(I32)


def _destinations(counts, topi, rank):
    k, t = topi.shape
    tl = min(2048, t)
    ltri = np.tril(np.ones((N_EXPERTS, N_EXPERTS), np.float32), -1)
    return pl.pallas_call(
        _dest_kernel,
        out_shape=(
            jax.ShapeDtypeStruct((k, t), I32),
            jax.ShapeDtypeStruct((2, N_EXPERTS, LANES), I32),
        ),
        grid=(t // tl,),
        in_specs=[
            pl.BlockSpec((N_EXPERTS, LANES), lambda i: (0, 0)),
            pl.BlockSpec((N_EXPERTS, N_EXPERTS), lambda i: (0, 0)),
            pl.BlockSpec((k, tl), lambda i: (0, i)),
            pl.BlockSpec((k, tl), lambda i: (0, i)),
        ],
        out_specs=(
            pl.BlockSpec((k, tl), lambda i: (0, i)),
            pl.BlockSpec((2, N_EXPERTS, LANES), lambda i: (0, 0, 0)),
        ),
        compiler_params=_params(("arbitrary",)),
        name="route_dest",
    )(counts, jnp.asarray(ltri, BF16), topi, rank)


def _pack_bf16_pairs(x):
    n = x.shape[1] // 2
    lo = lax.bitcast_convert_type(x[:, :n].astype(BF16).astype(F32), U32)
    hi = lax.bitcast_convert_type(x[:, n:].astype(BF16).astype(F32), U32)
    return hi | (lo >> 16)


def _unpack_bf16_pairs(w):
    lo = lax.bitcast_convert_type(w << 16, F32)
    hi = lax.bitcast_convert_type(w & jnp.uint32(0xFFFF0000), F32)
    return lo, hi


SC_WINDOW = 128


def _sc_mesh():
    return plsc.VectorSubcoreMesh(core_axis_name="core", subcore_axis_name="subcore")


def _sc_worker(mesh):
    return lax.axis_index("core") * mesh.num_subcores + lax.axis_index("subcore"), mesh.num_cores * mesh.num_subcores


def _sc_scatter_rows(src, dest_rows, n_out):
    n, width = src.shape
    nk = dest_rows.shape[0]
    mesh = _sc_mesh()
    half = SC_WINDOW // 2

    @pl.kernel(out_type=jax.ShapeDtypeStruct((n_out, width), src.dtype), mesh=mesh,
               scratch_types=[pltpu.VMEM((nk, SC_WINDOW), I32), pltpu.VMEM((2, half, width), src.dtype),
                              pltpu.SemaphoreType.DMA((2,)), pltpu.SemaphoreType.DMA((nk,))],
               name="sc_scatter_rows")
    def scatter_kernel(src_hbm, idx_hbm, out_hbm, idx_v, rows_v, sem, psem):
        wid, nw = _sc_worker(mesh)
        per = n // SC_WINDOW // nw
        row0 = wid * per * SC_WINDOW

        def load(s, buf):
            return pltpu.make_async_copy(src_hbm.at[pl.ds(row0 + s * half, half)], rows_v.at[buf], sem.at[buf])

        load(0, 0).start()

        @pl.loop(0, per)
        def _(j):
            pltpu.sync_copy(idx_hbm.at[:, pl.ds(row0 + j * SC_WINDOW, SC_WINDOW)], idx_v)
            for h in range(2):
                if h == 0:
                    load(2 * j + 1, 1).start()
                else:
                    @pl.when(j + 1 < per)
                    def _():
                        load(2 * j + 2, 0).start()

                load(2 * j + h, h).wait()
                puts = [pltpu.make_async_copy(rows_v.at[h], out_hbm.at[idx_v.at[k, pl.ds(h * half, half)]], psem.at[k])
                        for k in range(nk)]
                for cp in puts:
                    cp.start()
                for cp in puts:
                    cp.wait()

    return scatter_kernel(src, dest_rows)


def _sc_gather_rows(src, rows):
    n = rows.shape[0]
    width = src.shape[1]
    mesh = _sc_mesh()
    half = SC_WINDOW // 2

    @pl.kernel(out_type=jax.ShapeDtypeStruct((n, width), src.dtype), mesh=mesh,
               scratch_types=[pltpu.VMEM((1, SC_WINDOW), I32), pltpu.VMEM((2, half, width), src.dtype),
                              pltpu.SemaphoreType.DMA((2,)), pltpu.SemaphoreType.DMA((2,))],
               name="sc_gather_rows")
    def gather_kernel(src_hbm, idx_hbm, out_hbm, idx_v, rows_v, sem, gsem):
        wid, nw = _sc_worker(mesh)
        per = n // SC_WINDOW // nw
        row0 = wid * per * SC_WINDOW

        def store(s, buf):
            return pltpu.make_async_copy(rows_v.at[buf], out_hbm.at[pl.ds(row0 + s * half, half)], sem.at[buf])

        def fetch(buf):
            return pltpu.make_async_copy(src_hbm.at[idx_v.at[0, pl.ds(buf * half, half)]], rows_v.at[buf], gsem.at[buf])

        @pl.loop(0, per)
        def _(j):
            pltpu.sync_copy(idx_hbm.at[:, pl.ds(row0 + j * SC_WINDOW, SC_WINDOW)], idx_v)
            for h in range(2):
                @pl.when(j >= 1)
                def _():
                    store(2 * j + h - 2, h).wait()

                fetch(h).start()
            for h in range(2):
                fetch(h).wait()
                store(2 * j + h, h).start()

        for h in range(2):
            store(2 * per - 2 + h, h).wait()

    return gather_kernel(src, rows.reshape(1, n))


def _expert_group_kernel(eb_ref, xs_hbm, wgu_hbm, bgu_ref, wd_hbm, bd_ref, ys_hbm,
                         xbuf, ybuf, wgu_f, wd_f, wgu_bf, wd_bf, act_scr, xsem, ysem, wsem, *, dff, l):
    e = pl.program_id(0)
    ne = pl.num_programs(0)
    b0 = eb_ref[e]
    nb = eb_ref[ne + e]
    rows, half = xbuf.shape[1], xbuf.shape[2]
    group = rows // MOE_ROWS
    nch = (nb + group - 1) // group
    wslot = e % 2

    def w_copies(ex, slot):
        cps = []
        for src, dst, first in ((wgu_hbm, wgu_f, 0), (wd_hbm, wd_f, WEIGHT_CHUNKS)):
            step = dst.shape[1] // WEIGHT_CHUNKS
            for c in range(WEIGHT_CHUNKS):
                cps.append(pltpu.make_async_copy(src.at[l, ex, pl.ds(c * step, step), :],
                                                 dst.at[slot, pl.ds(c * step, step), :], wsem.at[slot, first + c]))
        return cps

    @pl.when(e == 0)
    def _():
        for c, cp in enumerate(w_copies(0, 0)):
            cp.start(priority=c % 2)

    @pl.when(e + 1 < ne)
    def _():
        for c, cp in enumerate(w_copies(e + 1, 1 - wslot)):
            cp.start(priority=c % 2)

    for cp in w_copies(e, wslot):
        cp.wait()

    def x_copy(j, slot):
        start = (b0 + j * group) * MOE_ROWS
        return pltpu.make_async_copy(xs_hbm.at[pl.ds(start, rows), :], xbuf.at[slot], xsem.at[slot])

    def y_copy(j, slot, g):
        start = (b0 + j * group + g) * MOE_ROWS
        return pltpu.make_async_copy(ybuf.at[slot, pl.ds(g * MOE_ROWS, MOE_ROWS), :],
                                     ys_hbm.at[pl.ds(start, MOE_ROWS), :], ysem.at[slot, g])

    def y_each(j, slot, fn):
        for g in range(group):
            @pl.when(j * group + g < nb)
            def _():
                fn(y_copy(j, slot, g))

    @pl.when(nb > 0)
    def _():
        x_copy(0, 0).start(priority=1)
        wgu_bf[:, 0:2 * dff] = wgu_f[wslot].astype(BF16)
        wd_bf[:, 0:2 * half] = wd_f[wslot].astype(BF16)

        def chunk(j, carry):
            slot = j % 2

            @pl.when(j + 1 < nch)
            def _():
                x_copy(j + 1, 1 - slot).start(priority=1)

            x_copy(j, slot).wait()

            @pl.when(j >= 2)
            def _():
                y_each(j - 2, slot, lambda cp: cp.wait())

            lo, hi = _unpack_bf16_pairs(xbuf[slot])
            xb = jnp.concatenate([lo.astype(BF16), hi.astype(BF16)], axis=1)
            for c0 in range(0, dff, EXPERT_COLS):
                gate = _dot(xb, wgu_bf[:, c0:c0 + EXPERT_COLS]) + bgu_ref[:, c0:c0 + EXPERT_COLS]
                up = (_dot(xb, wgu_bf[:, dff + c0:dff + c0 + EXPERT_COLS])
                      + bgu_ref[:, dff + c0:dff + c0 + EXPERT_COLS])
                glu = jnp.minimum(gate, SWIGLU_LIMIT)
                lin = jnp.clip(up, -SWIGLU_LIMIT, SWIGLU_LIMIT)
                act_scr[:, c0:c0 + EXPERT_COLS] = (glu * _sigmoid(SWIGLU_ALPHA * glu) * (lin + 1.0)).astype(BF16)
            ybuf[slot] = _pack_bf16_pairs(_dot(act_scr[...], wd_bf[:, 0:2 * half]) + bd_ref[...])
            y_each(j, slot, lambda cp: cp.start(priority=1))
            return carry

        lax.fori_loop(0, nch, chunk, 0)

        @pl.when(nch >= 2)
        def _():
            y_each(nch - 2, nch % 2, lambda cp: cp.wait())

        y_each(nch - 1, (nch - 1) % 2, lambda cp: cp.wait())


def _block_experts(eblk, xs, w_gu, b_gu4, w_down, b_down4, l):
    cap, half = xs.shape
    d = 2 * half
    dff = w_down.shape[2]
    wsel = lambda e, eb: (l, e, 0, 0)
    grid_spec = pltpu.PrefetchScalarGridSpec(
        num_scalar_prefetch=1,
        grid=(N_EXPERTS,),
        in_specs=[
            pl.BlockSpec(memory_space=pl.ANY),
            pl.BlockSpec(memory_space=pl.ANY),
            pl.BlockSpec((None, None, 1, 2 * dff), wsel),
            pl.BlockSpec(memory_space=pl.ANY),
            pl.BlockSpec((None, None, 1, d), wsel),
        ],
        out_specs=pl.BlockSpec(memory_space=pl.ANY),
        scratch_shapes=[
            pltpu.VMEM((2, EXPERT_GROUP * MOE_ROWS, half), U32),
            pltpu.VMEM((2, EXPERT_GROUP * MOE_ROWS, half), U32),
            pltpu.VMEM((2, d, 2 * dff), F32),
            pltpu.VMEM((2, dff, d), F32),
            pltpu.VMEM((d, 2 * dff + LANES), BF16),
            pltpu.VMEM((dff, d + LANES), BF16),
            pltpu.VMEM((EXPERT_GROUP * MOE_ROWS, dff), BF16),
            pltpu.SemaphoreType.DMA((2,)),
            pltpu.SemaphoreType.DMA((2, EXPERT_GROUP)),
            pltpu.SemaphoreType.DMA((2, 2 * WEIGHT_CHUNKS)),
        ],
    )
    return pl.pallas_call(
        functools.partial(_expert_group_kernel, dff=dff, l=l),
        out_shape=jax.ShapeDtypeStruct((cap - (EXPERT_GROUP - 1) * MOE_ROWS, half), U32),
        grid_spec=grid_spec,
        compiler_params=_params(("arbitrary",)),
        name="moe_expert_groups",
    )(eblk, xs, w_gu, b_gu4, w_down, b_down4)


def _combine_kernel(*refs, d, final):
    y_refs = refs[:TOP_K]
    gt_ref, x_ref, mod_ref = refs[TOP_K:TOP_K + 3]
    xo_ref = refs[-1]
    half = d // 2
    gt = gt_ref[...]
    acc_lo = acc_hi = None
    for k in range(TOP_K):
        lo, hi = _unpack_bf16_pairs(y_refs[k][...])
        g = gt[:, k:k + 1]
        acc_lo = g * lo if acc_lo is None else acc_lo + g * lo
        acc_hi = g * hi if acc_hi is None else acc_hi + g * hi
    x_lo = x_ref[:, 0:half] + mod_ref[:, 5 * d:5 * d + half] * acc_lo
    x_hi = x_ref[:, half:d] + mod_ref[:, 5 * d + half:6 * d] * acc_hi
    if final:
        gain = refs[TOP_K + 3][...]
        ssq = jnp.sum(x_lo * x_lo, axis=-1, keepdims=True) + jnp.sum(x_hi * x_hi, axis=-1, keepdims=True)
        inv = lax.rsqrt(ssq * (1.0 / d) + EPS)
        x_lo = x_lo * inv * gain[:, 0:half]
        x_hi = x_hi * inv * gain[:, half:d]
    xo_ref[:, 0:half] = x_lo
    xo_ref[:, half:d] = x_hi


def _combine(y4p, gates_t, x2, mod3, l, seq, depth, row0, prev, final_gain=None):
    t, d = x2.shape
    n = gates_t.shape[0]
    tm = min(PROJ_TILE, seq)
    nb = mod3.shape[0] // depth
    nt = n // tm
    t0 = row0 // tm
    final = final_gain is not None

    def slot(k):
        return pl.BlockSpec((tm, d // 2), lambda i: (k * nt + i, 0))

    specs = [slot(k) for k in range(TOP_K)] + [
        pl.BlockSpec((tm, TOP_K), lambda i: (i, 0)),
        pl.BlockSpec((tm, d), lambda i: (t0 + i, 0)),
        pl.BlockSpec((None, 1, N_MOD * d), lambda i: (l * nb + (row0 + i * tm) // seq, 0, 0)),
    ]
    args = [y4p] * TOP_K + [gates_t, x2, mod3]
    if final:
        specs.append(pl.BlockSpec((1, d), lambda i: (0, 0)))
        args.append(final_gain.reshape(1, d))
    aliases = {}
    if prev is not None:
        specs.append(pl.BlockSpec(memory_space=pl.ANY))
        aliases = {len(args): 0}
        args.append(prev)
    return pl.pallas_call(
        functools.partial(_combine_kernel, d=d, final=final),
        out_shape=jax.ShapeDtypeStruct((t, d), F32),
        grid=(nt,),
        in_specs=specs,
        out_specs=pl.BlockSpec((tm, d), lambda i: (t0 + i, 0)),
        input_output_aliases=aliases,
        compiler_params=_params(("arbitrary",)),
        name="moe_combine",
    )(*args)


def _mixer_layer(x2, mod3, tables, lower, consts, p, l, batch, seq):
    ra, rb, rc = tables
    qa, ka2, va2, hg = _inproj(x2, mod3, p["norm1"], p["w_in"], ra, rb, rc, l, seq)
    a = _attention(qa, ka2, va2, p["attn_sink"], p["attn_norm"], l, batch, seq)
    o = _hgrn(hg, lower, p["hg_norm"], consts, l, batch, seq)
    return a, o


def _moe_layer(a, o, x2, mod3, p, l, seq, depth, final_gain=None):
    t, d = x2.shape
    xn, h2p, topi, gates = _outproj(a, o, x2, mod3, p["norm2"], p["w_out"], p["w_router_t"], p["b_router"], l, seq)
    rank, counts = _ranks(topi)
    nblocks = (t * TOP_K + N_EXPERTS * MOE_ROWS) // MOE_ROWS
    dest, eblk = _destinations(counts, topi, rank)
    xs = _sc_scatter_rows(h2p, dest, (nblocks + EXPERT_GROUP - 1) * MOE_ROWS)
    ys = _block_experts(eblk[:, :, 0].reshape(-1), xs, p["w_gu"], p["b_gu"], p["w_down"], p["b_down"], l)
    gates_t = gates.T
    out = None
    for part in range(MOE_COMBINE_PARTS):
        rows = slice(part * (t // MOE_COMBINE_PARTS), (part + 1) * (t // MOE_COMBINE_PARTS))
        y4p = _sc_gather_rows(ys, dest[:, rows].reshape(-1))
        out = _combine(y4p, gates_t[rows], xn, mod3, l, seq, depth, rows.start, out, final_gain)
    return out


def kernel(x, c, positions, w_ada, b_ada, norm1, w_in, attn_sink, attn_norm, hg_lb_logits, hg_norm, w_out, norm2,
           w_router, b_router, w_gu, b_gu, w_down, b_down, final_norm):
    batch, seq, d = x.shape
    depth = w_ada.shape[0]
    t = batch * seq
    p = {
        "norm1": norm1.reshape(depth, 1, d),
        "w_in": w_in.astype(BF16),
        "attn_sink": attn_sink.astype(F32),
        "attn_norm": attn_norm.reshape(depth, 1, ATT_WIDTH),
        "hg_norm": hg_norm.reshape(depth, 1, HG_HEAD_DIM),
        "w_out": w_out.astype(BF16),
        "norm2": norm2.reshape(depth, 1, d),
        "w_router_t": jnp.swapaxes(w_router, 1, 2),
        "b_router": b_router.reshape(depth, N_EXPERTS, 1),
        "w_gu": w_gu,
        "b_gu": b_gu.reshape(depth, N_EXPERTS, 1, b_gu.shape[-1]),
        "w_down": w_down,
        "b_down": b_down.reshape(depth, N_EXPERTS, 1, d),
    }
    mod3 = _ada_all(c, w_ada, b_ada).reshape(depth * batch, 1, N_MOD * d)
    lower = _lower_bounds(hg_lb_logits)
    tables = _rope_tables(positions)
    consts = tuple(jnp.asarray(m) for m in _hgrn_constants())
    x2 = x.reshape(t, d)
    for l in range(depth):
        a, o = _mixer_layer(x2, mod3, tables, lower, consts, p, l, batch, seq)
        x2 = _moe_layer(a, o, x2, mod3, p, l, seq, depth, final_norm if l == depth - 1 else None)
    return x2.reshape(batch, seq, d)
```

```python
import functools

import numpy as np
import jax
import jax.numpy as jnp
from jax import lax
from jax.experimental import pallas as pl
from jax.experimental.pallas import tpu as pltpu
from jax.experimental.pallas import tpu_sc as plsc

F32 = jnp.float32
BF16 = jnp.bfloat16
I32 = jnp.int32
U32 = jnp.uint32

ATT_HEADS = 8
ATT_KV_HEADS = 2
ATT_HEAD_DIM = 64
ATT_WIDTH = ATT_HEADS * ATT_HEAD_DIM
KV_WIDTH = ATT_KV_HEADS * ATT_HEAD_DIM
WINDOW = 128
ATT_BLOCK = 128
ROPE_THETA = 500000.0
ROPE_DIM = ATT_HEAD_DIM // 4
HG_HEADS = 4
HG_HEAD_DIM = 128
HG_WIDTH = HG_HEADS * HG_HEAD_DIM
N_EXPERTS = 32
TOP_K = 4
SWIGLU_ALPHA = 1.702
SWIGLU_LIMIT = 7.0
N_MOD = 6
EPS = 1e-6
NEG_INF = -1e30
LB_FLOOR = 1e-30

LANES = 128
HG_CHUNK = 64
HG_LEVELS = 6
HG_UNROLL = 8
MOE_ROWS = 256
EXPERT_GROUP = 1
WEIGHT_CHUNKS = 4
EXPERT_COLS = 512
MOE_COMBINE_PARTS = 2
PROJ_TILE = 512
RANK_TILE = 512
VMEM_LIMIT = 56 * 1024 * 1024


def _dot(a, b):
    return jnp.dot(a, b, preferred_element_type=F32)


def _dot_nt(a, b):
    return lax.dot_general(a, b, (((1,), (1,)), ((), ())), preferred_element_type=F32)


def _dot_tn(a, b):
    return lax.dot_general(a, b, (((0,), (0,)), ((), ())), preferred_element_type=F32)


def _split3(x):
    hi = x.astype(BF16)
    r1 = x - hi.astype(F32)
    mid = r1.astype(BF16)
    lo = (r1 - mid.astype(F32)).astype(BF16)
    return hi, mid, lo


def _dot_f32_nt(a, b):
    ah, am, _ = _split3(a)
    bh, bm, _ = _split3(b)
    return _dot_nt(ah, bh) + _dot_nt(ah, bm) + _dot_nt(am, bh)


def _dot_f32(a, b):
    ah, am, _ = _split3(a)
    bh, bm, _ = _split3(b)
    return _dot(ah, bh) + _dot(ah, bm) + _dot(am, bh)


def _sigmoid(x):
    return 1.0 / (1.0 + jnp.exp(-x))


def _params(sem=None):
    return pltpu.CompilerParams(dimension_semantics=sem, vmem_limit_bytes=VMEM_LIMIT)


def _ada_kernel(c_ref, w_ref, b_ref, o_ref):
    c = c_ref[...]
    cond = c * _sigmoid(c)
    o_ref[...] = _dot_f32(cond, w_ref[...]) + b_ref[...]


def _ada_all(c, w_ada, b_ada):
    depth, d, n = w_ada.shape
    b = c.shape[0]
    nt = n // d
    return pl.pallas_call(
        _ada_kernel,
        out_shape=jax.ShapeDtypeStruct((depth, b, n), F32),
        grid=(depth, nt),
        in_specs=[
            pl.BlockSpec((b, d), lambda l, j: (0, 0)),
            pl.BlockSpec((None, d, d), lambda l, j: (l, 0, j)),
            pl.BlockSpec((None, 1, d), lambda l, j: (l, 0, j)),
        ],
        out_specs=pl.BlockSpec((None, b, d), lambda l, j: (l, 0, j)),
        compiler_params=_params(("arbitrary", "arbitrary")),
        name="ada_mod",
    )(c, w_ada, b_ada.reshape(depth, 1, n))


def _lb_kernel(x_ref, o_ref):
    depth = x_ref.shape[0]
    xs = [x_ref[l] for l in range(depth)]
    m = xs[0]
    for l in range(1, depth):
        m = jnp.maximum(m, xs[l])
    es = [jnp.exp(v - m) for v in xs]
    den = es[0]
    for l in range(1, depth):
        den = den + es[l]
    ps = [e / den for e in es]
    run = ps[0]
    o_ref[0] = run - ps[0]
    for l in range(1, depth):
        run = run + ps[l]
        o_ref[l] = run - ps[0]


def _lower_bounds(hg_lb_logits):
    return pl.pallas_call(
        _lb_kernel,
        out_shape=jax.ShapeDtypeStruct(hg_lb_logits.shape, F32),
        name="hg_lower_bounds",
    )(hg_lb_logits.astype(F32))


def _rope_kernel(pos_ref, invf_ref, a_ref, b_ref, c_ref):
    pos = pos_ref[...].astype(F32)
    ang = pos * invf_ref[...]
    cs = jnp.cos(ang)
    sn = jnp.sin(ang)
    lane = lax.broadcasted_iota(I32, ang.shape, 1) & (ATT_HEAD_DIM - 1)
    half = ROPE_DIM // 2
    first = lane < half
    second = (lane >= half) & (lane < ROPE_DIM)
    a_ref[...] = jnp.where(first | second, cs, 1.0)
    b_ref[...] = jnp.where(first, -sn, 0.0)
    c_ref[...] = jnp.where(second, sn, 0.0)


def _rope_tables(positions):
    t = positions.size
    half = ROPE_DIM // 2
    inv = (np.float32(ROPE_THETA) ** (-(np.arange(half, dtype=np.float32) * np.float32(2.0) / np.float32(ROPE_DIM)))).astype(np.float32)
    lane = np.arange(LANES) % ATT_HEAD_DIM
    pat = np.where(lane < ROPE_DIM, inv[lane % half], 0.0).astype(np.float32).reshape(1, LANES)
    tm = min(t, 2048)
    shp = jax.ShapeDtypeStruct((t, LANES), F32)
    spec = pl.BlockSpec((tm, LANES), lambda i: (i, 0))
    return pl.pallas_call(
        _rope_kernel,
        out_shape=(shp, shp, shp),
        grid=(t // tm,),
        in_specs=[pl.BlockSpec((tm, 1), lambda i: (i, 0)), pl.BlockSpec((1, LANES), lambda i: (0, 0))],
        out_specs=(spec, spec, spec),
        compiler_params=_params(("arbitrary",)),
        name="rope_tables",
    )(positions.reshape(t, 1).astype(I32), jnp.asarray(pat))


def _rms_mod(x, gain, scale, shift):
    ms = jnp.mean(x * x, axis=-1, keepdims=True)
    return (x * lax.rsqrt(ms + EPS) * gain) * (1.0 + scale) + shift


def _rope_apply(x, a, b, c):
    half = ROPE_DIM // 2
    return x * a + pltpu.roll(x, LANES - half, 1) * b + pltpu.roll(x, half, 1) * c


def _inproj_kernel(x_ref, mod_ref, n1_ref, w_ref, ra_ref, rb_ref, rc_ref, qa_ref, ka_ref, va_ref, hg_ref, *, d):
    mod = mod_ref[...]
    h = _rms_mod(x_ref[...], n1_ref[...], mod[:, d:2 * d], mod[:, 0:d]).astype(BF16)
    a, b, c = ra_ref[...], rb_ref[...], rc_ref[...]
    kvw = ATT_WIDTH + 2 * KV_WIDTH
    pa = _dot(h, w_ref[:, 0:kvw])
    scale = ATT_HEAD_DIM ** -0.5
    for g in range(ATT_WIDTH // LANES):
        qg = _rope_apply(pa[:, g * LANES:(g + 1) * LANES], a, b, c)
        qa_ref[:, g * LANES:(g + 1) * LANES] = (qg * scale).astype(BF16)
    k = _rope_apply(pa[:, ATT_WIDTH:ATT_WIDTH + KV_WIDTH], a, b, c)
    v = pa[:, ATT_WIDTH + KV_WIDTH:kvw]
    ka_ref[:, 0:LANES] = k.astype(BF16)
    ka_ref[:, LANES:2 * LANES] = pltpu.roll(k, ATT_HEAD_DIM, 1).astype(BF16)
    va_ref[:, 0:LANES] = v.astype(BF16)
    va_ref[:, LANES:2 * LANES] = pltpu.roll(v, ATT_HEAD_DIM, 1).astype(BF16)
    for g in range(5):
        lo = kvw + g * HG_WIDTH
        hg_ref[:, g * HG_WIDTH:(g + 1) * HG_WIDTH] = _dot(h, w_ref[:, lo:lo + HG_WIDTH])


def _inproj(x2, mod3, norm1, w_in_bf, ra, rb, rc, l, seq):
    t, d = x2.shape
    n_in = w_in_bf.shape[-1]
    tm = min(PROJ_TILE, seq)
    nb = mod3.shape[0] // norm1.shape[0]
    tok = lambda i: (i, 0)
    return pl.pallas_call(
        functools.partial(_inproj_kernel, d=d),
        out_shape=(
            jax.ShapeDtypeStruct((t, ATT_WIDTH), BF16),
            jax.ShapeDtypeStruct((t, 2 * KV_WIDTH), BF16),
            jax.ShapeDtypeStruct((t, 2 * KV_WIDTH), BF16),
            jax.ShapeDtypeStruct((t, 5 * HG_WIDTH), F32),
        ),
        grid=(t // tm,),
        in_specs=[
            pl.BlockSpec((tm, d), tok),
            pl.BlockSpec((None, 1, N_MOD * d), lambda i: (l * nb + (i * tm) // seq, 0, 0)),
            pl.BlockSpec((None, 1, d), lambda i: (l, 0, 0)),
            pl.BlockSpec((None, d, n_in), lambda i: (l, 0, 0)),
            pl.BlockSpec((tm, LANES), tok),
            pl.BlockSpec((tm, LANES), tok),
            pl.BlockSpec((tm, LANES), tok),
        ],
        out_specs=(
            pl.BlockSpec((tm, ATT_WIDTH), tok),
            pl.BlockSpec((tm, 2 * KV_WIDTH), tok),
            pl.BlockSpec((tm, 2 * KV_WIDTH), tok),
            pl.BlockSpec((tm, 5 * HG_WIDTH), tok),
        ),
        compiler_params=_params(("arbitrary",)),
        name="in_proj",
    )(x2, mod3, norm1, w_in_bf, ra, rb, rc)


def _attn_kernel(sink_ref, q_ref, kp_ref, kc_ref, kn_ref, vp_ref, vc_ref, vn_ref, gain_ref, o_ref, *, l, seq):
    n = pl.program_id(1)
    blk = ATT_BLOCK
    k2 = jnp.concatenate([kp_ref[...], kc_ref[...], kn_ref[...]], axis=0)
    v2 = jnp.concatenate([vp_ref[...], vc_ref[...], vn_ref[...]], axis=0)
    lane = lax.broadcasted_iota(I32, (3 * blk, LANES), 1)
    lo_half = lane < ATT_HEAD_DIM
    zero = jnp.zeros((3 * blk, LANES), BF16)
    ka, kb = k2[:, 0:LANES], k2[:, LANES:2 * LANES]
    va, vb = v2[:, 0:LANES], v2[:, LANES:2 * LANES]
    kz = [[jnp.where(lo_half, ka, zero), jnp.where(lo_half, zero, kb)],
          [jnp.where(lo_half, kb, zero), jnp.where(lo_half, zero, ka)]]
    vz = [[jnp.where(lo_half, va, zero), jnp.where(lo_half, zero, vb)],
          [jnp.where(lo_half, vb, zero), jnp.where(lo_half, zero, va)]]
    qpos = n * blk + lax.broadcasted_iota(I32, (blk, 3 * blk), 0)
    kpos = (n - 1) * blk + lax.broadcasted_iota(I32, (blk, 3 * blk), 1)
    valid = (jnp.abs(qpos - kpos) <= WINDOW) & (kpos >= 0) & (kpos < seq)
    valid2 = jnp.concatenate([valid, valid], axis=0)
    upper = lax.broadcasted_iota(I32, (2 * blk, 1), 0) < blk
    outs = []
    for j in range(ATT_KV_HEADS):
        qs = jnp.concatenate([q_ref[:, 2 * j * LANES:(2 * j + 1) * LANES],
                              q_ref[:, (2 * j + 1) * LANES:(2 * j + 2) * LANES]], axis=0)
        acc = None
        for half in range(2):
            s = _dot_nt(qs, kz[j][half])
            s = jnp.where(valid2, s, NEG_INF)
            sink = jnp.where(upper, sink_ref[l, 4 * j + half], sink_ref[l, 4 * j + 2 + half])
            mx = jnp.maximum(jnp.max(s, axis=-1, keepdims=True), sink)
            p = jnp.exp(s - mx)
            den = jnp.sum(p, axis=-1, keepdims=True) + jnp.exp(sink - mx)
            p = (p * (1.0 / den)).astype(BF16)
            pv = _dot(p, vz[j][half])
            acc = pv if acc is None else acc + pv
        outs.append(acc[0:blk])
        outs.append(acc[blk:2 * blk])
    o = jnp.concatenate(outs, axis=-1)
    ms = jnp.mean(o * o, axis=-1, keepdims=True)
    o_ref[...] = (o * lax.rsqrt(ms + EPS) * gain_ref[...]).astype(BF16)


def _attention(qa, ka2, va2, attn_sink, attn_norm3, l, batch, seq):
    t = qa.shape[0]
    blk = ATT_BLOCK
    nb = seq // blk
    cur = lambda b, n: (b * nb + n, 0)
    prev = lambda b, n: (b * nb + jnp.maximum(n - 1, 0), 0)
    nxt = lambda b, n: (b * nb + jnp.minimum(n + 1, nb - 1), 0)
    kvspec = lambda f: pl.BlockSpec((blk, 2 * KV_WIDTH), f)
    return pl.pallas_call(
        functools.partial(_attn_kernel, l=l, seq=seq),
        out_shape=jax.ShapeDtypeStruct((t, ATT_WIDTH), BF16),
        grid=(batch, nb),
        in_specs=[
            pl.BlockSpec(memory_space=pltpu.SMEM),
            pl.BlockSpec((blk, ATT_WIDTH), cur),
            kvspec(prev), kvspec(cur), kvspec(nxt),
            kvspec(prev), kvspec(cur), kvspec(nxt),
            pl.BlockSpec((None, 1, ATT_WIDTH), lambda b, n: (l, 0, 0)),
        ],
        out_specs=pl.BlockSpec((blk, ATT_WIDTH), cur),
        compiler_params=_params(("arbitrary", "arbitrary")),
        name="window_attn",
    )(attn_sink, qa, ka2, ka2, ka2, va2, va2, va2, attn_norm3)


def _hgrn_constants():
    c, nl = HG_CHUNK, HG_LEVELS
    r = np.arange(c)
    masks = []
    for lev in range(nl):
        m = 1 << lev
        parent = r // (2 * m)
        upper = r >= parent * 2 * m + m
        masks.append((parent[:, None] == parent[None, :]) & upper[:, None] & (~upper)[None, :])
    masks.append(np.eye(c, dtype=bool))
    kf = np.stack(masks).astype(np.float32)
    kb = np.stack([mk[::-1, ::-1] for mk in masks]).astype(np.float32)
    return kf, kb


def _chunk_decays(logf, reverse):
    c = logf.shape[0]
    sub = 8
    nv = c // sub
    row = lax.broadcasted_iota(I32, (sub, LANES), 0)
    parts = []
    for v in range(nv):
        x = logf[sub * v:sub * (v + 1), :]
        for s in (1, 2, 4):
            if reverse:
                x = x + jnp.where(row < sub - s, pltpu.roll(x, sub - s, 0), 0.0)
            else:
                x = x + jnp.where(row >= s, pltpu.roll(x, s, 0), 0.0)
        parts.append(x)
    b = [None] * nv
    order = list(reversed(range(nv))) if reverse else list(range(nv))
    edge = 0 if reverse else sub - 1
    carry = None
    for v in order:
        b[v] = parts[v] if carry is None else parts[v] + carry
        carry = b[v][edge:edge + 1, :]
    b_last = carry

    def anchor_row(v, r):
        return jnp.broadcast_to(b[v][r:r + 1, :], (sub, LANES))

    odd = (row & 1) == 1
    levels = []
    for lev in range(HG_LEVELS):
        m = 1 << lev
        pieces = []
        for v in range(nv):
            if m == 1:
                a = jnp.where(odd, pltpu.roll(b[v], 1, 0), b[v]) if reverse else jnp.where(odd, b[v], pltpu.roll(b[v], sub - 1, 0))
            elif m == 2:
                lo, hi = (1, 5) if reverse else (2, 6)
                a = jnp.where(row < 4, anchor_row(v, lo), anchor_row(v, hi))
            elif m == 4:
                a = anchor_row(v, 3 if reverse else 4)
            else:
                mv = m // sub
                first = (v // (2 * mv)) * 2 * mv
                a = anchor_row(first + mv - 1, sub - 1) if reverse else anchor_row(first + mv, 0)
            pieces.append(jnp.exp(-jnp.abs(b[v] - a)))
        levels.append(jnp.concatenate(pieces, axis=0))
    eb = jnp.concatenate([jnp.exp(bv) for bv in b], axis=0)
    erem = jnp.concatenate([jnp.exp(b_last - bv) for bv in b], axis=0)
    return levels, eb, erem


def _hgrn_kernel(q_ref, ff_ref, fb_ref, i_ref, g_ref, lb_ref, gn_ref, kf_ref, kb_ref,
                 o_ref, of_scr, ob_scr, st_scr, *, seq):
    c, nl = HG_CHUNK, HG_LEVELS
    nc = seq // c
    lb = lb_ref[...]
    st_scr[...] = jnp.zeros(st_scr.shape, F32)

    unroll = HG_UNROLL if nc % HG_UNROLL == 0 else 1
    dirs = (
        dict(f_ref=ff_ref, lbrow=lb[0:1, :], k_ref=kf_ref, o_scr=of_scr, last_row=c - 1, d=0),
        dict(f_ref=fb_ref, lbrow=lb[1:2, :], k_ref=kb_ref, o_scr=ob_scr, last_row=0, d=1),
    )

    def body(i, carry):
        work = []
        for u in range(unroll):
            cf = i * unroll + u
            work.append((dirs[0], pl.ds(pl.multiple_of(cf * c, c), c)))
            work.append((dirs[1], pl.ds(pl.multiple_of((nc - 1 - cf) * c, c), c)))
        gates = []
        for dr, rows in work:
            lbf = jnp.maximum(dr["lbrow"], LB_FLOOR)
            oml = 1.0 - dr["lbrow"]
            f = dr["f_ref"][rows, :]
            e = jnp.exp(-jnp.abs(f))
            r = 1.0 / (1.0 + e)
            er = e * r
            pos = f >= 0.0
            logf = jnp.log(lbf + oml * jnp.where(pos, r, er))
            kk = oml * jnp.where(pos, er, r)
            qh = q_ref[rows, :]
            gates.append((logf, kk.astype(BF16), (qh * _sigmoid(qh)).astype(BF16), i_ref[rows, :].astype(BF16)))
        decays = [_chunk_decays(g[0], dr["d"] == 1) for (dr, _), g in zip(work, gates)]
        amat = [dr["k_ref"][nl] * _dot_nt(g[2], g[1]) for (dr, _), g in zip(work, gates)]
        for lev in range(nl):
            for j, ((dr, _), g) in enumerate(zip(work, gates)):
                gl = decays[j][0][lev].astype(BF16)
                amat[j] = amat[j] + dr["k_ref"][lev] * _dot_nt(g[2] * gl, g[1] * gl)
        intra = [_dot(amat[j].astype(BF16), g[3]) for j, g in enumerate(gates)]
        upd = [_dot_tn(g[3], g[1] * decays[j][2].astype(BF16)) for j, g in enumerate(gates)]
        st = [st_scr[0], st_scr[1]]
        for j, ((dr, rows), g) in enumerate(zip(work, gates)):
            eb = decays[j][1]
            d = dr["d"]
            dr["o_scr"][rows, :] = _dot_nt(g[2] * eb.astype(BF16), st[d].astype(BF16)) + intra[j]
            st[d] = st[d] * eb[dr["last_row"]:dr["last_row"] + 1, :] + upd[j]
        st_scr[0] = st[0]
        st_scr[1] = st[1]
        return carry

    lax.fori_loop(0, nc // unroll, body, 0)

    ep = min(256, seq)
    gn = gn_ref[...]

    def epilogue(j, carry):
        rows = pl.ds(pl.multiple_of(j * ep, ep), ep)
        o = of_scr[rows, :] + ob_scr[rows, :]
        y = o * lax.rsqrt(jnp.mean(o * o, axis=-1, keepdims=True) + EPS) * gn
        g = g_ref[rows, :]
        o_ref[rows, :] = (y * (g * _sigmoid(g))).astype(BF16)
        return carry

    lax.fori_loop(0, seq // ep, epilogue, 0)


def _hgrn(hg, lower, hg_norm3, consts, l, batch, seq):
    t = hg.shape[0]
    kf, kb = consts
    hd = HG_HEAD_DIM

    def col(g):
        return pl.BlockSpec((seq, hd), lambda b, h: (b, g * HG_HEADS + h))

    full3 = lambda a: pl.BlockSpec(a.shape, lambda b, h: (0, 0, 0))
    return pl.pallas_call(
        functools.partial(_hgrn_kernel, seq=seq),
        out_shape=jax.ShapeDtypeStruct((t, HG_WIDTH), BF16),
        grid=(batch, HG_HEADS),
        in_specs=[
            col(0), col(1), col(2), col(3), col(4),
            pl.BlockSpec((None, 2, hd), lambda b, h: (l, 0, h)),
            pl.BlockSpec((None, 1, hd), lambda b, h: (l, 0, 0)),
            full3(kf), full3(kb),
        ],
        out_specs=pl.BlockSpec((seq, hd), lambda b, h: (b, h)),
        scratch_shapes=[
            pltpu.VMEM((seq, hd), F32),
            pltpu.VMEM((seq, hd), F32),
            pltpu.VMEM((2, hd, hd), F32),
        ],
        compiler_params=_params(("arbitrary", "arbitrary")),
        name="hgrn2_scan",
    )(hg, hg, hg, hg, hg, lower, hg_norm3, kf, kb)


def _outproj_kernel(a_ref, o_ref, x_ref, mod_ref, n2_ref, w_ref, wr_ref, br_ref,
                    xo_ref, h2_ref, ti_ref, gt_ref, *, d):
    mod = mod_ref[...]
    y = _dot(a_ref[...], w_ref[0:ATT_WIDTH, :]) + _dot(o_ref[...], w_ref[ATT_WIDTH:ATT_WIDTH + HG_WIDTH, :])
    xn = x_ref[...] + mod[:, 2 * d:3 * d] * y
    xo_ref[...] = xn
    h2 = _rms_mod(xn, n2_ref[...], mod[:, 4 * d:5 * d], mod[:, 3 * d:4 * d])
    h2_ref[...] = _pack_bf16_pairs(h2)
    lg = _dot_f32_nt(wr_ref[...], h2) + br_ref[...]
    eidx = lax.broadcasted_iota(I32, lg.shape, 0)
    vals, idxs = [], []
    for _ in range(TOP_K):
        mx = jnp.max(lg, axis=0, keepdims=True)
        sel = jnp.min(jnp.where(lg == mx, eidx, N_EXPERTS), axis=0, keepdims=True)
        vals.append(mx)
        idxs.append(sel)
        lg = jnp.where(eidx == sel, -jnp.inf, lg)
    ex = [jnp.exp(v - vals[0]) for v in vals]
    den = ex[0]
    for e in ex[1:]:
        den = den + e
    inv = 1.0 / den
    ti_ref[...] = jnp.concatenate(idxs, axis=0)
    gt_ref[...] = jnp.concatenate([e * inv for e in ex], axis=0)


def _outproj(a, o, x2, mod3, norm2, w_out_bf, w_router_t, b_router3, l, seq):
    t, d = x2.shape
    tm = min(PROJ_TILE, seq)
    nb = mod3.shape[0] // norm2.shape[0]
    tok = lambda i: (i, 0)
    lane_tok = lambda i: (0, i)
    return pl.pallas_call(
        functools.partial(_outproj_kernel, d=d),
        out_shape=(
            jax.ShapeDtypeStruct((t, d), F32),
            jax.ShapeDtypeStruct((t, d // 2), U32),
            jax.ShapeDtypeStruct((TOP_K, t), I32),
            jax.ShapeDtypeStruct((TOP_K, t), F32),
        ),
        grid=(t // tm,),
        in_specs=[
            pl.BlockSpec((tm, ATT_WIDTH), tok),
            pl.BlockSpec((tm, HG_WIDTH), tok),
            pl.BlockSpec((tm, d), tok),
            pl.BlockSpec((None, 1, N_MOD * d), lambda i: (l * nb + (i * tm) // seq, 0, 0)),
            pl.BlockSpec((None, 1, d), lambda i: (l, 0, 0)),
            pl.BlockSpec((None, ATT_WIDTH + HG_WIDTH, d), lambda i: (l, 0, 0)),
            pl.BlockSpec((None, N_EXPERTS, d), lambda i: (l, 0, 0)),
            pl.BlockSpec((None, N_EXPERTS, 1), lambda i: (l, 0, 0)),
        ],
        out_specs=(
            pl.BlockSpec((tm, d), tok),
            pl.BlockSpec((tm, d // 2), tok),
            pl.BlockSpec((TOP_K, tm), lane_tok),
            pl.BlockSpec((TOP_K, tm), lane_tok),
        ),
        compiler_params=_params(("arbitrary",)),
        name="out_proj_router",
    )(a, o, x2, mod3, norm2, w_out_bf, w_router_t, b_router3)


def _rank_kernel(ti_ref, tri_ref, rank_ref, cnt_ref, carry_scr):
    @pl.when(pl.program_id(0) == 0)
    def _():
        carry_scr[...] = jnp.zeros(carry_scr.shape, F32)

    ti = ti_ref[...]
    tl = ti.shape[1]
    eidx = lax.broadcasted_iota(I32, (N_EXPERTS, tl), 0)
    carry = carry_scr[...]
    rows = []
    for k in range(TOP_K):
        oh = eidx == ti[k:k + 1, :]
        ohf = jnp.where(oh, 1.0, 0.0)
        pre = _dot(ohf.astype(BF16), tri_ref[...])
        rows.append(jnp.sum(jnp.where(oh, carry + pre, 0.0), axis=0, keepdims=True))
        carry = carry + jnp.sum(ohf, axis=1, keepdims=True)
    carry_scr[...] = carry
    rank_ref[...] = jnp.concatenate(rows, axis=0).astype(I32)
    cnt_ref[...] = jnp.broadcast_to(carry, cnt_ref.shape)


def _ranks(topi):
    k, t = topi.shape
    tl = min(RANK_TILE, t)
    tri = np.triu(np.ones((tl, tl), np.float32), 1)
    return pl.pallas_call(
        _rank_kernel,
        out_shape=(jax.ShapeDtypeStruct((k, t), I32), jax.ShapeDtypeStruct((N_EXPERTS, LANES), F32)),
        grid=(t // tl,),
        in_specs=[pl.BlockSpec((k, tl), lambda i: (0, i)), pl.BlockSpec((tl, tl), lambda i: (0, 0))],
        out_specs=(pl.BlockSpec((k, tl), lambda i: (0, i)), pl.BlockSpec((N_EXPERTS, LANES), lambda i: (0, 0))),
        scratch_shapes=[pltpu.VMEM((N_EXPERTS, 1), F32)],
        compiler_params=_params(("arbitrary",)),
        name="route_rank",
    )(topi, jnp.asarray(tri, BF16))


def _dest_kernel(cnt_ref, ltri_ref, ti_ref, rank_ref, dest_ref, eblk_ref):
    cnt = cnt_ref[...]
    nblk = jnp.floor((cnt + (MOE_ROWS - 1)) * (1.0 / MOE_ROWS))
    pstart_b = _dot(ltri_ref[...], nblk.astype(BF16))
    pstart = (pstart_b[:, 0:1] * MOE_ROWS).astype(I32)
    ti = ti_ref[...]
    tl = ti.shape[1]
    eidx = lax.broadcasted_iota(I32, (N_EXPERTS, tl), 0)
    rows = []
    for k in range(TOP_K):
        oh = eidx == ti[k:k + 1, :]
        rows.append(jnp.sum(jnp.where(oh, pstart, 0), axis=0, keepdims=True))
    dest_ref[...] = jnp.concatenate(rows, axis=0) + rank_ref[...]
    eblk_ref[0] = pstart_b.astype(I32)
    eblk_ref[1] = nblk.astype(I32)
    eblk_ref[2] = cnt.astype(I32)


def _destinations(counts, topi, rank):
    k, t = topi.shape
    tl = min(2048, t)
    ltri = np.tril(np.ones((N_EXPERTS, N_EXPERTS), np.float32), -1)
    return pl.pallas_call(
        _dest_kernel,
        out_shape=(
            jax.ShapeDtypeStruct((k, t), I32),
            jax.ShapeDtypeStruct((3, N_EXPERTS, LANES), I32),
        ),
        grid=(t // tl,),
        in_specs=[
            pl.BlockSpec((N_EXPERTS, LANES), lambda i: (0, 0)),
            pl.BlockSpec((N_EXPERTS, N_EXPERTS), lambda i: (0, 0)),
            pl.BlockSpec((k, tl), lambda i: (0, i)),
            pl.BlockSpec((k, tl), lambda i: (0, i)),
        ],
        out_specs=(
            pl.BlockSpec((k, tl), lambda i: (0, i)),
            pl.BlockSpec((3, N_EXPERTS, LANES), lambda i: (0, 0, 0)),
        ),
        compiler_params=_params(("arbitrary",)),
        name="route_dest",
    )(counts, jnp.asarray(ltri, BF16), topi, rank)


def _pack_bf16_pairs(x):
    n = x.shape[1] // 2
    lo = lax.bitcast_convert_type(x[:, :n].astype(BF16).astype(F32), U32)
    hi = lax.bitcast_convert_type(x[:, n:].astype(BF16).astype(F32), U32)
    return hi | (lo >> 16)


def _unpack_bf16_pairs(w):
    lo = lax.bitcast_convert_type(w << 16, F32)
    hi = lax.bitcast_convert_type(w & jnp.uint32(0xFFFF0000), F32)
    return lo, hi


SC_WINDOW = 128


def _sc_mesh():
    return plsc.VectorSubcoreMesh(core_axis_name="core", subcore_axis_name="subcore")


def _sc_worker(mesh):
    return lax.axis_index("core") * mesh.num_subcores + lax.axis_index("subcore"), mesh.num_cores * mesh.num_subcores


def _sc_scatter_rows(src, dest_rows, n_out):
    n, width = src.shape
    nk = dest_rows.shape[0]
    mesh = _sc_mesh()
    half = SC_WINDOW // 2

    @pl.kernel(out_type=jax.ShapeDtypeStruct((n_out, width), src.dtype), mesh=mesh,
               scratch_types=[pltpu.VMEM((nk, SC_WINDOW), I32), pltpu.VMEM((2, half, width), src.dtype),
                              pltpu.SemaphoreType.DMA((2,)), pltpu.SemaphoreType.DMA((nk,))],
               name="sc_scatter_rows")
    def scatter_kernel(src_hbm, idx_hbm, out_hbm, idx_v, rows_v, sem, psem):
        wid, nw = _sc_worker(mesh)
        per = n // SC_WINDOW // nw
        row0 = wid * per * SC_WINDOW

        def load(s, buf):
            return pltpu.make_async_copy(src_hbm.at[pl.ds(row0 + s * half, half)], rows_v.at[buf], sem.at[buf])

        load(0, 0).start()

        @pl.loop(0, per)
        def _(j):
            pltpu.sync_copy(idx_hbm.at[:, pl.ds(row0 + j * SC_WINDOW, SC_WINDOW)], idx_v)
            for h in range(2):
                if h == 0:
                    load(2 * j + 1, 1).start()
                else:
                    @pl.when(j + 1 < per)
                    def _():
                        load(2 * j + 2, 0).start()

                load(2 * j + h, h).wait()
                puts = [pltpu.make_async_copy(rows_v.at[h], out_hbm.at[idx_v.at[k, pl.ds(h * half, half)]], psem.at[k])
                        for k in range(nk)]
                for cp in puts:
                    cp.start()
                for cp in puts:
                    cp.wait()

    return scatter_kernel(src, dest_rows)


def _sc_gather_rows(src, rows):
    n = rows.shape[0]
    width = src.shape[1]
    mesh = _sc_mesh()
    half = SC_WINDOW // 2

    @pl.kernel(out_type=jax.ShapeDtypeStruct((n, width), src.dtype), mesh=mesh,
               scratch_types=[pltpu.VMEM((1, SC_WINDOW), I32), pltpu.VMEM((2, half, width), src.dtype),
                              pltpu.SemaphoreType.DMA((2,)), pltpu.SemaphoreType.DMA((2,))],
               name="sc_gather_rows")
    def gather_kernel(src_hbm, idx_hbm, out_hbm, idx_v, rows_v, sem, gsem):
        wid, nw = _sc_worker(mesh)
        per = n // SC_WINDOW // nw
        row0 = wid * per * SC_WINDOW

        def store(s, buf):
            return pltpu.make_async_copy(rows_v.at[buf], out_hbm.at[pl.ds(row0 + s * half, half)], sem.at[buf])

        def fetch(buf):
            return pltpu.make_async_copy(src_hbm.at[idx_v.at[0, pl.ds(buf * half, half)]], rows_v.at[buf], gsem.at[buf])

        @pl.loop(0, per)
        def _(j):
            pltpu.sync_copy(idx_hbm.at[:, pl.ds(row0 + j * SC_WINDOW, SC_WINDOW)], idx_v)
            for h in range(2):
                @pl.when(j >= 1)
                def _():
                    store(2 * j + h - 2, h).wait()

                fetch(h).start()
            for h in range(2):
                fetch(h).wait()
                store(2 * j + h, h).start()

        for h in range(2):
            store(2 * per - 2 + h, h).wait()

    return gather_kernel(src, rows.reshape(1, n))


def _expert_group_kernel(eb_ref, xs_hbm, wgu_hbm, bgu_ref, wd_hbm, bd_ref, ys_hbm,
                         xbuf, ybuf, wgu_f, wd_f, wgu_bf, wd_bf, act_scr, xsem, ysem, wsem, *, dff, l):
    e = pl.program_id(0)
    ne = pl.num_programs(0)
    b0 = eb_ref[e]
    nb = eb_ref[ne + e]
    nrow = eb_ref[2 * ne + e]
    rows, half = xbuf.shape[1], xbuf.shape[2]
    group = rows // MOE_ROWS
    nch = (nb + group - 1) // group
    wslot = e % 2

    def w_copies(ex, slot):
        cps = []
        for src, dst, first in ((wgu_hbm, wgu_f, 0), (wd_hbm, wd_f, WEIGHT_CHUNKS)):
            step = dst.shape[1] // WEIGHT_CHUNKS
            for c in range(WEIGHT_CHUNKS):
                cps.append(pltpu.make_async_copy(src.at[l, ex, pl.ds(c * step, step), :],
                                                 dst.at[slot, pl.ds(c * step, step), :], wsem.at[slot, first + c]))
        return cps

    @pl.when(e == 0)
    def _():
        for c, cp in enumerate(w_copies(0, 0)):
            cp.start(priority=c % 2)

    @pl.when(e + 1 < ne)
    def _():
        for c, cp in enumerate(w_copies(e + 1, 1 - wslot)):
            cp.start(priority=c % 2)

    for cp in w_copies(e, wslot):
        cp.wait()

    def x_copy(j, slot):
        start = (b0 + j * group) * MOE_ROWS
        return pltpu.make_async_copy(xs_hbm.at[pl.ds(start, rows), :], xbuf.at[slot], xsem.at[slot])

    def y_copy(j, slot, g):
        start = (b0 + j * group + g) * MOE_ROWS
        return pltpu.make_async_copy(ybuf.at[slot, pl.ds(g * MOE_ROWS, MOE_ROWS), :],
                                     ys_hbm.at[pl.ds(start, MOE_ROWS), :], ysem.at[slot, g])

    def y_each(j, slot, fn):
        for g in range(group):
            @pl.when(j * group + g < nb)
            def _():
                fn(y_copy(j, slot, g))

    @pl.when(nb > 0)
    def _():
        x_copy(0, 0).start(priority=1)
        wgu_bf[:, 0:2 * dff] = wgu_f[wslot].astype(BF16)
        wd_bf[:, 0:2 * half] = wd_f[wslot].astype(BF16)

        def chunk(j, carry):
            slot = j % 2

            @pl.when(j + 1 < nch)
            def _():
                x_copy(j + 1, 1 - slot).start(priority=1)

            x_copy(j, slot).wait()

            @pl.when(j >= 2)
            def _():
                y_each(j - 2, slot, lambda cp: cp.wait())

            def compute(nr):
                lo, hi = _unpack_bf16_pairs(xbuf[slot, 0:nr, :])
                xb = jnp.concatenate([lo.astype(BF16), hi.astype(BF16)], axis=1)
                for c0 in range(0, dff, EXPERT_COLS):
                    gate = _dot(xb, wgu_bf[:, c0:c0 + EXPERT_COLS]) + bgu_ref[:, c0:c0 + EXPERT_COLS]
                    up = (_dot(xb, wgu_bf[:, dff + c0:dff + c0 + EXPERT_COLS])
                          + bgu_ref[:, dff + c0:dff + c0 + EXPERT_COLS])
                    glu = jnp.minimum(gate, SWIGLU_LIMIT)
                    lin = jnp.clip(up, -SWIGLU_LIMIT, SWIGLU_LIMIT)
                    act_scr[0:nr, c0:c0 + EXPERT_COLS] = (
                        glu * _sigmoid(SWIGLU_ALPHA * glu) * (lin + 1.0)).astype(BF16)
                ybuf[slot, 0:nr, :] = _pack_bf16_pairs(_dot(act_scr[0:nr, :], wd_bf[:, 0:2 * half]) + bd_ref[...])

            few = jnp.logical_and(j == nch - 1, nrow - j * rows <= rows // 2)

            @pl.when(few)
            def _():
                compute(rows // 2)

            @pl.when(jnp.logical_not(few))
            def _():
                compute(rows)

            y_each(j, slot, lambda cp: cp.start(priority=1))
            return carry

        lax.fori_loop(0, nch, chunk, 0)

        @pl.when(nch >= 2)
        def _():
            y_each(nch - 2, nch % 2, lambda cp: cp.wait())

        y_each(nch - 1, (nch - 1) % 2, lambda cp: cp.wait())


def _block_experts(eblk, xs, w_gu, b_gu4, w_down, b_down4, l):
    cap, half = xs.shape
    d = 2 * half
    dff = w_down.shape[2]
    wsel = lambda e, eb: (l, e, 0, 0)
    grid_spec = pltpu.PrefetchScalarGridSpec(
        num_scalar_prefetch=1,
        grid=(N_EXPERTS,),
        in_specs=[
            pl.BlockSpec(memory_space=pl.ANY),
            pl.BlockSpec(memory_space=pl.ANY),
            pl.BlockSpec((None, None, 1, 2 * dff), wsel),
            pl.BlockSpec(memory_space=pl.ANY),
            pl.BlockSpec((None, None, 1, d), wsel),
        ],
        out_specs=pl.BlockSpec(memory_space=pl.ANY),
        scratch_shapes=[
            pltpu.VMEM((2, EXPERT_GROUP * MOE_ROWS, half), U32),
            pltpu.VMEM((2, EXPERT_GROUP * MOE_ROWS, half), U32),
            pltpu.VMEM((2, d, 2 * dff), F32),
            pltpu.VMEM((2, dff, d), F32),
            pltpu.VMEM((d, 2 * dff + LANES), BF16),
            pltpu.VMEM((dff, d + LANES), BF16),
            pltpu.VMEM((EXPERT_GROUP * MOE_ROWS, dff), BF16),
            pltpu.SemaphoreType.DMA((2,)),
            pltpu.SemaphoreType.DMA((2, EXPERT_GROUP)),
            pltpu.SemaphoreType.DMA((2, 2 * WEIGHT_CHUNKS)),
        ],
    )
    return pl.pallas_call(
        functools.partial(_expert_group_kernel, dff=dff, l=l),
        out_shape=jax.ShapeDtypeStruct((cap - (EXPERT_GROUP - 1) * MOE_ROWS, half), U32),
        grid_spec=grid_spec,
        compiler_params=_params(("arbitrary",)),
        name="moe_expert_groups",
    )(eblk, xs, w_gu, b_gu4, w_down, b_down4)


def _combine_kernel(*refs, d, final):
    y_refs = refs[:TOP_K]
    gt_ref, x_ref, mod_ref = refs[TOP_K:TOP_K + 3]
    xo_ref = refs[-1]
    half = d // 2
    gt = gt_ref[...]
    acc_lo = acc_hi = None
    for k in range(TOP_K):
        lo, hi = _unpack_bf16_pairs(y_refs[k][...])
        g = gt[:, k:k + 1]
        acc_lo = g * lo if acc_lo is None else acc_lo + g * lo
        acc_hi = g * hi if acc_hi is None else acc_hi + g * hi
    x_lo = x_ref[:, 0:half] + mod_ref[:, 5 * d:5 * d + half] * acc_lo
    x_hi = x_ref[:, half:d] + mod_ref[:, 5 * d + half:6 * d] * acc_hi
    if final:
        gain = refs[TOP_K + 3][...]
        ssq = jnp.sum(x_lo * x_lo, axis=-1, keepdims=True) + jnp.sum(x_hi * x_hi, axis=-1, keepdims=True)
        inv = lax.rsqrt(ssq * (1.0 / d) + EPS)
        x_lo = x_lo * inv * gain[:, 0:half]
        x_hi = x_hi * inv * gain[:, half:d]
    xo_ref[:, 0:half] = x_lo
    xo_ref[:, half:d] = x_hi


def _combine(y4p, gates_t, x2, mod3, l, seq, depth, row0, prev, final_gain=None):
    t, d = x2.shape
    n = gates_t.shape[0]
    tm = min(PROJ_TILE, seq)
    nb = mod3.shape[0] // depth
    nt = n // tm
    t0 = row0 // tm
    final = final_gain is not None

    def slot(k):
        return pl.BlockSpec((tm, d // 2), lambda i: (k * nt + i, 0))

    specs = [slot(k) for k in range(TOP_K)] + [
        pl.BlockSpec((tm, TOP_K), lambda i: (i, 0)),
        pl.BlockSpec((tm, d), lambda i: (t0 + i, 0)),
        pl.BlockSpec((None, 1, N_MOD * d), lambda i: (l * nb + (row0 + i * tm) // seq, 0, 0)),
    ]
    args = [y4p] * TOP_K + [gates_t, x2, mod3]
    if final:
        specs.append(pl.BlockSpec((1, d), lambda i: (0, 0)))
        args.append(final_gain.reshape(1, d))
    aliases = {}
    if prev is not None:
        specs.append(pl.BlockSpec(memory_space=pl.ANY))
        aliases = {len(args): 0}
        args.append(prev)
    return pl.pallas_call(
        functools.partial(_combine_kernel, d=d, final=final),
        out_shape=jax.ShapeDtypeStruct((t, d), F32),
        grid=(nt,),
        in_specs=specs,
        out_specs=pl.BlockSpec((tm, d), lambda i: (t0 + i, 0)),
        input_output_aliases=aliases,
        compiler_params=_params(("arbitrary",)),
        name="moe_combine",
    )(*args)


def _mixer_layer(x2, mod3, tables, lower, consts, p, l, batch, seq):
    ra, rb, rc = tables
    qa, ka2, va2, hg = _inproj(x2, mod3, p["norm1"], p["w_in"], ra, rb, rc, l, seq)
    a = _attention(qa, ka2, va2, p["attn_sink"], p["attn_norm"], l, batch, seq)
    o = _hgrn(hg, lower, p["hg_norm"], consts, l, batch, seq)
    return a, o


def _moe_layer(a, o, x2, mod3, p, l, seq, depth, final_gain=None):
    t, d = x2.shape
    xn, h2p, topi, gates = _outproj(a, o, x2, mod3, p["norm2"], p["w_out"], p["w_router_t"], p["b_router"], l, seq)
    rank, counts = _ranks(topi)
    nblocks = (t * TOP_K + N_EXPERTS * MOE_ROWS) // MOE_ROWS
    dest, eblk = _destinations(counts, topi, rank)
    xs = _sc_scatter_rows(h2p, dest, (nblocks + EXPERT_GROUP - 1) * MOE_ROWS)
    ys = _block_experts(eblk[:, :, 0].reshape(-1), xs, p["w_gu"], p["b_gu"], p["w_down"], p["b_down"], l)
    gates_t = gates.T
    out = None
    for part in range(MOE_COMBINE_PARTS):
        rows = slice(part * (t // MOE_COMBINE_PARTS), (part + 1) * (t // MOE_COMBINE_PARTS))
        y4p = _sc_gather_rows(ys, dest[:, rows].reshape(-1))
        out = _combine(y4p, gates_t[rows], xn, mod3, l, seq, depth, rows.start, out, final_gain)
    return out


def kernel(x, c, positions, w_ada, b_ada, norm1, w_in, attn_sink, attn_norm, hg_lb_logits, hg_norm, w_out, norm2,
           w_router, b_router, w_gu, b_gu, w_down, b_down, final_norm):
    batch, seq, d = x.shape
    depth = w_ada.shape[0]
    t = batch * seq
    p = {
        "norm1": norm1.reshape(depth, 1, d),
        "w_in": w_in.astype(BF16),
        "attn_sink": attn_sink.astype(F32),
        "attn_norm": attn_norm.reshape(depth, 1, ATT_WIDTH),
        "hg_norm": hg_norm.reshape(depth, 1, HG_HEAD_DIM),
        "w_out": w_out.astype(BF16),
        "norm2": norm2.reshape(depth, 1, d),
        "w_router_t": jnp.swapaxes(w_router, 1, 2),
        "b_router": b_router.reshape(depth, N_EXPERTS, 1),
        "w_gu": w_gu,
        "b_gu": b_gu.reshape(depth, N_EXPERTS, 1, b_gu.shape[-1]),
        "w_down": w_down,
        "b_down": b_down.reshape(depth, N_EXPERTS, 1, d),
    }
    mod3 = _ada_all(c, w_ada, b_ada).reshape(depth * batch, 1, N_MOD * d)
    lower = _lower_bounds(hg_lb_logits)
    tables = _rope_tables(positions)
    consts = tuple(jnp.asarray(m) for m in _hgrn_constants())
    x2 = x.reshape(t, d)
    for l in range(depth):
        a, o = _mixer_layer(x2, mod3, tables, lower, consts, p, l, batch, seq)
        x2 = _moe_layer(a, o, x2, mod3, p, l, seq, depth, final_norm if l == depth - 1 else None)
    return x2.reshape(batch, seq, d)
```

```python
import functools

import numpy as np
import jax
import jax.numpy as jnp
from jax import lax
from jax.experimental import pallas as pl
from jax.experimental.pallas import tpu as pltpu
from jax.experimental.pallas import tpu_sc as plsc

F32 = jnp.float32
BF16 = jnp.bfloat16
I32 = jnp.int32
U32 = jnp.uint32

ATT_HEADS = 8
ATT_KV_HEADS = 2
ATT_HEAD_DIM = 64
ATT_WIDTH = ATT_HEADS * ATT_HEAD_DIM
KV_WIDTH = ATT_KV_HEADS * ATT_HEAD_DIM
WINDOW = 128
ATT_BLOCK = 128
ROPE_THETA = 500000.0
ROPE_DIM = ATT_HEAD_DIM // 4
HG_HEADS = 4
HG_HEAD_DIM = 128
HG_WIDTH = HG_HEADS * HG_HEAD_DIM
N_EXPERTS = 32
TOP_K = 4
SWIGLU_ALPHA = 1.702
SWIGLU_LIMIT = 7.0
N_MOD = 6
EPS = 1e-6
NEG_INF = -1e30
LB_FLOOR = 1e-30

LANES = 128
HG_CHUNK = 64
HG_LEVELS = 6
HG_UNROLL = 8
MOE_ROWS = 256
EXPERT_GROUP = 2
WEIGHT_CHUNKS = 4
EXPERT_COLS = 256
MOE_COMBINE_PARTS = 2
PROJ_TILE = 512
RANK_TILE = 512
VMEM_LIMIT = 56 * 1024 * 1024


def _dot(a, b):
    return jnp.dot(a, b, preferred_element_type=F32)


def _dot_nt(a, b):
    return lax.dot_general(a, b, (((1,), (1,)), ((), ())), preferred_element_type=F32)


def _dot_tn(a, b):
    return lax.dot_general(a, b, (((0,), (0,)), ((), ())), preferred_element_type=F32)


def _split3(x):
    hi = x.astype(BF16)
    r1 = x - hi.astype(F32)
    mid = r1.astype(BF16)
    lo = (r1 - mid.astype(F32)).astype(BF16)
    return hi, mid, lo


def _dot_f32_nt(a, b):
    ah, am, _ = _split3(a)
    bh, bm, _ = _split3(b)
    return _dot_nt(ah, bh) + _dot_nt(ah, bm) + _dot_nt(am, bh)


def _dot_f32(a, b):
    ah, am, _ = _split3(a)
    bh, bm, _ = _split3(b)
    return _dot(ah, bh) + _dot(ah, bm) + _dot(am, bh)


def _sigmoid(x):
    return 1.0 / (1.0 + jnp.exp(-x))


def _params(sem=None):
    return pltpu.CompilerParams(dimension_semantics=sem, vmem_limit_bytes=VMEM_LIMIT)


def _ada_kernel(c_ref, w_ref, b_ref, o_ref):
    c = c_ref[...]
    cond = c * _sigmoid(c)
    o_ref[...] = _dot_f32(cond, w_ref[...]) + b_ref[...]


def _ada_all(c, w_ada, b_ada):
    depth, d, n = w_ada.shape
    b = c.shape[0]
    nt = n // d
    return pl.pallas_call(
        _ada_kernel,
        out_shape=jax.ShapeDtypeStruct((depth, b, n), F32),
        grid=(depth, nt),
        in_specs=[
            pl.BlockSpec((b, d), lambda l, j: (0, 0)),
            pl.BlockSpec((None, d, d), lambda l, j: (l, 0, j)),
            pl.BlockSpec((None, 1, d), lambda l, j: (l, 0, j)),
        ],
        out_specs=pl.BlockSpec((None, b, d), lambda l, j: (l, 0, j)),
        compiler_params=_params(("arbitrary", "arbitrary")),
        name="ada_mod",
    )(c, w_ada, b_ada.reshape(depth, 1, n))


def _lb_kernel(x_ref, o_ref):
    depth = x_ref.shape[0]
    xs = [x_ref[l] for l in range(depth)]
    m = xs[0]
    for l in range(1, depth):
        m = jnp.maximum(m, xs[l])
    es = [jnp.exp(v - m) for v in xs]
    den = es[0]
    for l in range(1, depth):
        den = den + es[l]
    ps = [e / den for e in es]
    run = ps[0]
    o_ref[0] = run - ps[0]
    for l in range(1, depth):
        run = run + ps[l]
        o_ref[l] = run - ps[0]


def _lower_bounds(hg_lb_logits):
    return pl.pallas_call(
        _lb_kernel,
        out_shape=jax.ShapeDtypeStruct(hg_lb_logits.shape, F32),
        name="hg_lower_bounds",
    )(hg_lb_logits.astype(F32))


def _rope_kernel(pos_ref, invf_ref, a_ref, b_ref, c_ref):
    pos = pos_ref[...].astype(F32)
    ang = pos * invf_ref[...]
    cs = jnp.cos(ang)
    sn = jnp.sin(ang)
    lane = lax.broadcasted_iota(I32, ang.shape, 1) & (ATT_HEAD_DIM - 1)
    half = ROPE_DIM // 2
    first = lane < half
    second = (lane >= half) & (lane < ROPE_DIM)
    a_ref[...] = jnp.where(first | second, cs, 1.0)
    b_ref[...] = jnp.where(first, -sn, 0.0)
    c_ref[...] = jnp.where(second, sn, 0.0)


def _rope_tables(positions):
    t = positions.size
    half = ROPE_DIM // 2
    inv = (np.float32(ROPE_THETA) ** (-(np.arange(half, dtype=np.float32) * np.float32(2.0) / np.float32(ROPE_DIM)))).astype(np.float32)
    lane = np.arange(LANES) % ATT_HEAD_DIM
    pat = np.where(lane < ROPE_DIM, inv[lane % half], 0.0).astype(np.float32).reshape(1, LANES)
    tm = min(t, 2048)
    shp = jax.ShapeDtypeStruct((t, LANES), F32)
    spec = pl.BlockSpec((tm, LANES), lambda i: (i, 0))
    return pl.pallas_call(
        _rope_kernel,
        out_shape=(shp, shp, shp),
        grid=(t // tm,),
        in_specs=[pl.BlockSpec((tm, 1), lambda i: (i, 0)), pl.BlockSpec((1, LANES), lambda i: (0, 0))],
        out_specs=(spec, spec, spec),
        compiler_params=_params(("arbitrary",)),
        name="rope_tables",
    )(positions.reshape(t, 1).astype(I32), jnp.asarray(pat))


def _rms_mod(x, gain, scale, shift):
    ms = jnp.mean(x * x, axis=-1, keepdims=True)
    return (x * lax.rsqrt(ms + EPS) * gain) * (1.0 + scale) + shift


def _rope_apply(x, a, b, c):
    half = ROPE_DIM // 2
    return x * a + pltpu.roll(x, LANES - half, 1) * b + pltpu.roll(x, half, 1) * c


def _inproj_kernel(x_ref, mod_ref, n1_ref, w_ref, ra_ref, rb_ref, rc_ref, qa_ref, ka_ref, va_ref, hg_ref, *, d):
    mod = mod_ref[...]
    h = _rms_mod(x_ref[...], n1_ref[...], mod[:, d:2 * d], mod[:, 0:d]).astype(BF16)
    a, b, c = ra_ref[...], rb_ref[...], rc_ref[...]
    kvw = ATT_WIDTH + 2 * KV_WIDTH
    pa = _dot(h, w_ref[:, 0:kvw])
    scale = ATT_HEAD_DIM ** -0.5
    for g in range(ATT_WIDTH // LANES):
        qg = _rope_apply(pa[:, g * LANES:(g + 1) * LANES], a, b, c)
        qa_ref[:, g * LANES:(g + 1) * LANES] = (qg * scale).astype(BF16)
    k = _rope_apply(pa[:, ATT_WIDTH:ATT_WIDTH + KV_WIDTH], a, b, c)
    v = pa[:, ATT_WIDTH + KV_WIDTH:kvw]
    ka_ref[:, 0:LANES] = k.astype(BF16)
    ka_ref[:, LANES:2 * LANES] = pltpu.roll(k, ATT_HEAD_DIM, 1).astype(BF16)
    va_ref[:, 0:LANES] = v.astype(BF16)
    va_ref[:, LANES:2 * LANES] = pltpu.roll(v, ATT_HEAD_DIM, 1).astype(BF16)
    for g in range(5):
        lo = kvw + g * HG_WIDTH
        hg_ref[:, g * HG_WIDTH:(g + 1) * HG_WIDTH] = _dot(h, w_ref[:, lo:lo + HG_WIDTH])


def _inproj(x2, mod3, norm1, w_in_bf, ra, rb, rc, l, seq):
    t, d = x2.shape
    n_in = w_in_bf.shape[-1]
    tm = min(PROJ_TILE, seq)
    nb = mod3.shape[0] // norm1.shape[0]
    tok = lambda i: (i, 0)
    return pl.pallas_call(
        functools.partial(_inproj_kernel, d=d),
        out_shape=(
            jax.ShapeDtypeStruct((t, ATT_WIDTH), BF16),
            jax.ShapeDtypeStruct((t, 2 * KV_WIDTH), BF16),
            jax.ShapeDtypeStruct((t, 2 * KV_WIDTH), BF16),
            jax.ShapeDtypeStruct((t, 5 * HG_WIDTH), F32),
        ),
        grid=(t // tm,),
        in_specs=[
            pl.BlockSpec((tm, d), tok),
            pl.BlockSpec((None, 1, N_MOD * d), lambda i: (l * nb + (i * tm) // seq, 0, 0)),
            pl.BlockSpec((None, 1, d), lambda i: (l, 0, 0)),
            pl.BlockSpec((None, d, n_in), lambda i: (l, 0, 0)),
            pl.BlockSpec((tm, LANES), tok),
            pl.BlockSpec((tm, LANES), tok),
            pl.BlockSpec((tm, LANES), tok),
        ],
        out_specs=(
            pl.BlockSpec((tm, ATT_WIDTH), tok),
            pl.BlockSpec((tm, 2 * KV_WIDTH), tok),
            pl.BlockSpec((tm, 2 * KV_WIDTH), tok),
            pl.BlockSpec((tm, 5 * HG_WIDTH), tok),
        ),
        compiler_params=_params(("arbitrary",)),
        name="in_proj",
    )(x2, mod3, norm1, w_in_bf, ra, rb, rc)


def _attn_kernel(sink_ref, q_ref, kp_ref, kc_ref, kn_ref, vp_ref, vc_ref, vn_ref, gain_ref, o_ref, *, l, seq):
    n = pl.program_id(1)
    blk = ATT_BLOCK
    k2 = jnp.concatenate([kp_ref[...], kc_ref[...], kn_ref[...]], axis=0)
    v2 = jnp.concatenate([vp_ref[...], vc_ref[...], vn_ref[...]], axis=0)
    lane = lax.broadcasted_iota(I32, (3 * blk, LANES), 1)
    lo_half = lane < ATT_HEAD_DIM
    zero = jnp.zeros((3 * blk, LANES), BF16)
    ka, kb = k2[:, 0:LANES], k2[:, LANES:2 * LANES]
    va, vb = v2[:, 0:LANES], v2[:, LANES:2 * LANES]
    kz = [[jnp.where(lo_half, ka, zero), jnp.where(lo_half, zero, kb)],
          [jnp.where(lo_half, kb, zero), jnp.where(lo_half, zero, ka)]]
    vz = [[jnp.where(lo_half, va, zero), jnp.where(lo_half, zero, vb)],
          [jnp.where(lo_half, vb, zero), jnp.where(lo_half, zero, va)]]
    qpos = n * blk + lax.broadcasted_iota(I32, (blk, 3 * blk), 0)
    kpos = (n - 1) * blk + lax.broadcasted_iota(I32, (blk, 3 * blk), 1)
    valid = (jnp.abs(qpos - kpos) <= WINDOW) & (kpos >= 0) & (kpos < seq)
    valid2 = jnp.concatenate([valid, valid], axis=0)
    upper = lax.broadcasted_iota(I32, (2 * blk, 1), 0) < blk
    outs = []
    for j in range(ATT_KV_HEADS):
        qs = jnp.concatenate([q_ref[:, 2 * j * LANES:(2 * j + 1) * LANES],
                              q_ref[:, (2 * j + 1) * LANES:(2 * j + 2) * LANES]], axis=0)
        acc = None
        for half in range(2):
            s = _dot_nt(qs, kz[j][half])
            s = jnp.where(valid2, s, NEG_INF)
            sink = jnp.where(upper, sink_ref[l, 4 * j + half], sink_ref[l, 4 * j + 2 + half])
            mx = jnp.maximum(jnp.max(s, axis=-1, keepdims=True), sink)
            p = jnp.exp(s - mx)
            den = jnp.sum(p, axis=-1, keepdims=True) + jnp.exp(sink - mx)
            p = (p * (1.0 / den)).astype(BF16)
            pv = _dot(p, vz[j][half])
            acc = pv if acc is None else acc + pv
        outs.append(acc[0:blk])
        outs.append(acc[blk:2 * blk])
    o = jnp.concatenate(outs, axis=-1)
    ms = jnp.mean(o * o, axis=-1, keepdims=True)
    o_ref[...] = (o * lax.rsqrt(ms + EPS) * gain_ref[...]).astype(BF16)


def _attention(qa, ka2, va2, attn_sink, attn_norm3, l, batch, seq):
    t = qa.shape[0]
    blk = ATT_BLOCK
    nb = seq // blk
    cur = lambda b, n: (b * nb + n, 0)
    prev = lambda b, n: (b * nb + jnp.maximum(n - 1, 0), 0)
    nxt = lambda b, n: (b * nb + jnp.minimum(n + 1, nb - 1), 0)
    kvspec = lambda f: pl.BlockSpec((blk, 2 * KV_WIDTH), f)
    return pl.pallas_call(
        functools.partial(_attn_kernel, l=l, seq=seq),
        out_shape=jax.ShapeDtypeStruct((t, ATT_WIDTH), BF16),
        grid=(batch, nb),
        in_specs=[
            pl.BlockSpec(memory_space=pltpu.SMEM),
            pl.BlockSpec((blk, ATT_WIDTH), cur),
            kvspec(prev), kvspec(cur), kvspec(nxt),
            kvspec(prev), kvspec(cur), kvspec(nxt),
            pl.BlockSpec((None, 1, ATT_WIDTH), lambda b, n: (l, 0, 0)),
        ],
        out_specs=pl.BlockSpec((blk, ATT_WIDTH), cur),
        compiler_params=_params(("arbitrary", "arbitrary")),
        name="window_attn",
    )(attn_sink, qa, ka2, ka2, ka2, va2, va2, va2, attn_norm3)


def _hgrn_constants():
    c, nl = HG_CHUNK, HG_LEVELS
    r = np.arange(c)
    masks = []
    for lev in range(nl):
        m = 1 << lev
        parent = r // (2 * m)
        upper = r >= parent * 2 * m + m
        masks.append((parent[:, None] == parent[None, :]) & upper[:, None] & (~upper)[None, :])
    masks.append(np.eye(c, dtype=bool))
    kf = np.stack(masks).astype(np.float32)
    kb = np.stack([mk[::-1, ::-1] for mk in masks]).astype(np.float32)
    return kf, kb


def _chunk_decays(logf, reverse):
    c = logf.shape[0]
    sub = 8
    nv = c // sub
    row = lax.broadcasted_iota(I32, (sub, LANES), 0)
    parts = []
    for v in range(nv):
        x = logf[sub * v:sub * (v + 1), :]
        for s in (1, 2, 4):
            if reverse:
                x = x + jnp.where(row < sub - s, pltpu.roll(x, sub - s, 0), 0.0)
            else:
                x = x + jnp.where(row >= s, pltpu.roll(x, s, 0), 0.0)
        parts.append(x)
    b = [None] * nv
    order = list(reversed(range(nv))) if reverse else list(range(nv))
    edge = 0 if reverse else sub - 1
    carry = None
    for v in order:
        b[v] = parts[v] if carry is None else parts[v] + carry
        carry = b[v][edge:edge + 1, :]
    b_last = carry

    def anchor_row(v, r):
        return jnp.broadcast_to(b[v][r:r + 1, :], (sub, LANES))

    odd = (row & 1) == 1
    levels = []
    for lev in range(HG_LEVELS):
        m = 1 << lev
        pieces = []
        for v in range(nv):
            if m == 1:
                a = jnp.where(odd, pltpu.roll(b[v], 1, 0), b[v]) if reverse else jnp.where(odd, b[v], pltpu.roll(b[v], sub - 1, 0))
            elif m == 2:
                lo, hi = (1, 5) if reverse else (2, 6)
                a = jnp.where(row < 4, anchor_row(v, lo), anchor_row(v, hi))
            elif m == 4:
                a = anchor_row(v, 3 if reverse else 4)
            else:
                mv = m // sub
                first = (v // (2 * mv)) * 2 * mv
                a = anchor_row(first + mv - 1, sub - 1) if reverse else anchor_row(first + mv, 0)
            pieces.append(jnp.exp(-jnp.abs(b[v] - a)))
        levels.append(jnp.concatenate(pieces, axis=0))
    eb = jnp.concatenate([jnp.exp(bv) for bv in b], axis=0)
    erem = jnp.concatenate([jnp.exp(b_last - bv) for bv in b], axis=0)
    return levels, eb, erem


def _hgrn_kernel(q_ref, ff_ref, fb_ref, i_ref, g_ref, lb_ref, gn_ref, kf_ref, kb_ref,
                 o_ref, of_scr, ob_scr, st_scr, *, seq):
    c, nl = HG_CHUNK, HG_LEVELS
    nc = seq // c
    lb = lb_ref[...]
    st_scr[...] = jnp.zeros(st_scr.shape, F32)

    unroll = HG_UNROLL if nc % HG_UNROLL == 0 else 1
    dirs = (
        dict(f_ref=ff_ref, lbrow=lb[0:1, :], k_ref=kf_ref, o_scr=of_scr, last_row=c - 1, d=0),
        dict(f_ref=fb_ref, lbrow=lb[1:2, :], k_ref=kb_ref, o_scr=ob_scr, last_row=0, d=1),
    )

    def body(i, carry):
        work = []
        for u in range(unroll):
            cf = i * unroll + u
            work.append((dirs[0], pl.ds(pl.multiple_of(cf * c, c), c)))
            work.append((dirs[1], pl.ds(pl.multiple_of((nc - 1 - cf) * c, c), c)))
        gates = []
        for dr, rows in work:
            lbf = jnp.maximum(dr["lbrow"], LB_FLOOR)
            oml = 1.0 - dr["lbrow"]
            f = dr["f_ref"][rows, :]
            e = jnp.exp(-jnp.abs(f))
            r = 1.0 / (1.0 + e)
            er = e * r
            pos = f >= 0.0
            logf = jnp.log(lbf + oml * jnp.where(pos, r, er))
            kk = oml * jnp.where(pos, er, r)
            qh = q_ref[rows, :]
            gates.append((logf, kk.astype(BF16), (qh * _sigmoid(qh)).astype(BF16), i_ref[rows, :].astype(BF16)))
        decays = [_chunk_decays(g[0], dr["d"] == 1) for (dr, _), g in zip(work, gates)]
        amat = [dr["k_ref"][nl] * _dot_nt(g[2], g[1]) for (dr, _), g in zip(work, gates)]
        for lev in range(nl):
            for j, ((dr, _), g) in enumerate(zip(work, gates)):
                gl = decays[j][0][lev].astype(BF16)
                amat[j] = amat[j] + dr["k_ref"][lev] * _dot_nt(g[2] * gl, g[1] * gl)
        intra = [_dot(amat[j].astype(BF16), g[3]) for j, g in enumerate(gates)]
        upd = [_dot_tn(g[3], g[1] * decays[j][2].astype(BF16)) for j, g in enumerate(gates)]
        st = [st_scr[0], st_scr[1]]
        for j, ((dr, rows), g) in enumerate(zip(work, gates)):
            eb = decays[j][1]
            d = dr["d"]
            dr["o_scr"][rows, :] = _dot_nt(g[2] * eb.astype(BF16), st[d].astype(BF16)) + intra[j]
            st[d] = st[d] * eb[dr["last_row"]:dr["last_row"] + 1, :] + upd[j]
        st_scr[0] = st[0]
        st_scr[1] = st[1]
        return carry

    lax.fori_loop(0, nc // unroll, body, 0)

    ep = min(256, seq)
    gn = gn_ref[...]

    def epilogue(j, carry):
        rows = pl.ds(pl.multiple_of(j * ep, ep), ep)
        o = of_scr[rows, :] + ob_scr[rows, :]
        y = o * lax.rsqrt(jnp.mean(o * o, axis=-1, keepdims=True) + EPS) * gn
        g = g_ref[rows, :]
        o_ref[rows, :] = (y * (g * _sigmoid(g))).astype(BF16)
        return carry

    lax.fori_loop(0, seq // ep, epilogue, 0)


def _hgrn(hg, lower, hg_norm3, consts, l, batch, seq):
    t = hg.shape[0]
    kf, kb = consts
    hd = HG_HEAD_DIM

    def col(g):
        return pl.BlockSpec((seq, hd), lambda b, h: (b, g * HG_HEADS + h))

    full3 = lambda a: pl.BlockSpec(a.shape, lambda b, h: (0, 0, 0))
    return pl.pallas_call(
        functools.partial(_hgrn_kernel, seq=seq),
        out_shape=jax.ShapeDtypeStruct((t, HG_WIDTH), BF16),
        grid=(batch, HG_HEADS),
        in_specs=[
            col(0), col(1), col(2), col(3), col(4),
            pl.BlockSpec((None, 2, hd), lambda b, h: (l, 0, h)),
            pl.BlockSpec((None, 1, hd), lambda b, h: (l, 0, 0)),
            full3(kf), full3(kb),
        ],
        out_specs=pl.BlockSpec((seq, hd), lambda b, h: (b, h)),
        scratch_shapes=[
            pltpu.VMEM((seq, hd), F32),
            pltpu.VMEM((seq, hd), F32),
            pltpu.VMEM((2, hd, hd), F32),
        ],
        compiler_params=_params(("arbitrary", "arbitrary")),
        name="hgrn2_scan",
    )(hg, hg, hg, hg, hg, lower, hg_norm3, kf, kb)


def _outproj_kernel(a_ref, o_ref, x_ref, mod_ref, n2_ref, w_ref, wr_ref, br_ref,
                    xo_ref, h2_ref, ti_ref, gt_ref, *, d):
    mod = mod_ref[...]
    y = _dot(a_ref[...], w_ref[0:ATT_WIDTH, :]) + _dot(o_ref[...], w_ref[ATT_WIDTH:ATT_WIDTH + HG_WIDTH, :])
    xn = x_ref[...] + mod[:, 2 * d:3 * d] * y
    xo_ref[...] = xn
    h2 = _rms_mod(xn, n2_ref[...], mod[:, 4 * d:5 * d], mod[:, 3 * d:4 * d])
    h2_ref[...] = _pack_bf16_pairs(h2)
    lg = _dot_f32_nt(wr_ref[...], h2) + br_ref[...]
    eidx = lax.broadcasted_iota(I32, lg.shape, 0)
    vals, idxs = [], []
    for _ in range(TOP_K):
        mx = jnp.max(lg, axis=0, keepdims=True)
        sel = jnp.min(jnp.where(lg == mx, eidx, N_EXPERTS), axis=0, keepdims=True)
        vals.append(mx)
        idxs.append(sel)
        lg = jnp.where(eidx == sel, -jnp.inf, lg)
    ex = [jnp.exp(v - vals[0]) for v in vals]
    den = ex[0]
    for e in ex[1:]:
        den = den + e
    inv = 1.0 / den
    ti_ref[...] = jnp.concatenate(idxs, axis=0)
    gt_ref[...] = jnp.concatenate([e * inv for e in ex], axis=0)


def _outproj(a, o, x2, mod3, norm2, w_out_bf, w_router_t, b_router3, l, seq):
    t, d = x2.shape
    tm = min(PROJ_TILE, seq)
    nb = mod3.shape[0] // norm2.shape[0]
    tok = lambda i: (i, 0)
    lane_tok = lambda i: (0, i)
    return pl.pallas_call(
        functools.partial(_outproj_kernel, d=d),
        out_shape=(
            jax.ShapeDtypeStruct((t, d), F32),
            jax.ShapeDtypeStruct((t, d // 2), U32),
            jax.ShapeDtypeStruct((TOP_K, t), I32),
            jax.ShapeDtypeStruct((TOP_K, t), F32),
        ),
        grid=(t // tm,),
        in_specs=[
            pl.BlockSpec((tm, ATT_WIDTH), tok),
            pl.BlockSpec((tm, HG_WIDTH), tok),
            pl.BlockSpec((tm, d), tok),
            pl.BlockSpec((None, 1, N_MOD * d), lambda i: (l * nb + (i * tm) // seq, 0, 0)),
            pl.BlockSpec((None, 1, d), lambda i: (l, 0, 0)),
            pl.BlockSpec((None, ATT_WIDTH + HG_WIDTH, d), lambda i: (l, 0, 0)),
            pl.BlockSpec((None, N_EXPERTS, d), lambda i: (l, 0, 0)),
            pl.BlockSpec((None, N_EXPERTS, 1), lambda i: (l, 0, 0)),
        ],
        out_specs=(
            pl.BlockSpec((tm, d), tok),
            pl.BlockSpec((tm, d // 2), tok),
            pl.BlockSpec((TOP_K, tm), lane_tok),
            pl.BlockSpec((TOP_K, tm), lane_tok),
        ),
        compiler_params=_params(("arbitrary",)),
        name="out_proj_router",
    )(a, o, x2, mod3, norm2, w_out_bf, w_router_t, b_router3)


def _rank_kernel(ti_ref, tri_ref, rank_ref, cnt_ref, carry_scr):
    @pl.when(pl.program_id(0) == 0)
    def _():
        carry_scr[...] = jnp.zeros(carry_scr.shape, F32)

    ti = ti_ref[...]
    tl = ti.shape[1]
    eidx = lax.broadcasted_iota(I32, (N_EXPERTS, tl), 0)
    carry = carry_scr[...]
    rows = []
    for k in range(TOP_K):
        oh = eidx == ti[k:k + 1, :]
        ohf = jnp.where(oh, 1.0, 0.0)
        pre = _dot(ohf.astype(BF16), tri_ref[...])
        rows.append(jnp.sum(jnp.where(oh, carry + pre, 0.0), axis=0, keepdims=True))
        carry = carry + jnp.sum(ohf, axis=1, keepdims=True)
    carry_scr[...] = carry
    rank_ref[...] = jnp.concatenate(rows, axis=0).astype(I32)
    cnt_ref[...] = jnp.broadcast_to(carry, cnt_ref.shape)


def _ranks(topi):
    k, t = topi.shape
    tl = min(RANK_TILE, t)
    tri = np.triu(np.ones((tl, tl), np.float32), 1)
    return pl.pallas_call(
        _rank_kernel,
        out_shape=(jax.ShapeDtypeStruct((k, t), I32), jax.ShapeDtypeStruct((N_EXPERTS, LANES), F32)),
        grid=(t // tl,),
        in_specs=[pl.BlockSpec((k, tl), lambda i: (0, i)), pl.BlockSpec((tl, tl), lambda i: (0, 0))],
        out_specs=(pl.BlockSpec((k, tl), lambda i: (0, i)), pl.BlockSpec((N_EXPERTS, LANES), lambda i: (0, 0))),
        scratch_shapes=[pltpu.VMEM((N_EXPERTS, 1), F32)],
        compiler_params=_params(("arbitrary",)),
        name="route_rank",
    )(topi, jnp.asarray(tri, BF16))


def _dest_kernel(cnt_ref, ltri_ref, ti_ref, rank_ref, dest_ref, eblk_ref):
    cnt = cnt_ref[...]
    nblk = jnp.floor((cnt + (MOE_ROWS - 1)) * (1.0 / MOE_ROWS))
    pstart_b = _dot(ltri_ref[...], nblk.astype(BF16))
    pstart = (pstart_b[:, 0:1] * MOE_ROWS).astype(I32)
    ti = ti_ref[...]
    tl = ti.shape[1]
    eidx = lax.broadcasted_iota(I32, (N_EXPERTS, tl), 0)
    rows = []
    for k in range(TOP_K):
        oh = eidx == ti[k:k + 1, :]
        rows.append(jnp.sum(jnp.where(oh, pstart, 0), axis=0, keepdims=True))
    dest_ref[...] = jnp.concatenate(rows, axis=0) + rank_ref[...]
    eblk_ref[0] = pstart_b.astype(I32)
    eblk_ref[1] = nblk.astype(I32)


def _destinations(counts, topi, rank):
    k, t = topi.shape
    tl = min(2048, t)
    ltri = np.tril(np.ones((N_EXPERTS, N_EXPERTS), np.float32), -1)
    return pl.pallas_call(
        _dest_kernel,
        out_shape=(
            jax.ShapeDtypeStruct((k, t), I32),
            jax.ShapeDtypeStruct((2, N_EXPERTS, LANES), I32),
        ),
        grid=(t // tl,),
        in_specs=[
            pl.BlockSpec((N_EXPERTS, LANES), lambda i: (0, 0)),
            pl.BlockSpec((N_EXPERTS, N_EXPERTS), lambda i: (0, 0)),
            pl.BlockSpec((k, tl), lambda i: (0, i)),
            pl.BlockSpec((k, tl), lambda i: (0, i)),
        ],
        out_specs=(
            pl.BlockSpec((k, tl), lambda i: (0, i)),
            pl.BlockSpec((2, N_EXPERTS, LANES), lambda i: (0, 0, 0)),
        ),
        compiler_params=_params(("arbitrary",)),
        name="route_dest",
    )(counts, jnp.asarray(ltri, BF16), topi, rank)


def _pack_bf16_pairs(x):
    n = x.shape[1] // 2
    lo = lax.bitcast_convert_type(x[:, :n].astype(BF16).astype(F32), U32)
    hi = lax.bitcast_convert_type(x[:, n:].astype(BF16).astype(F32), U32)
    return hi | (lo >> 16)


def _unpack_bf16_pairs(w):
    lo = lax.bitcast_convert_type(w << 16, F32)
    hi = lax.bitcast_convert_type(w & jnp.uint32(0xFFFF0000), F32)
    return lo, hi


SC_WINDOW = 128


def _sc_mesh():
    return plsc.VectorSubcoreMesh(core_axis_name="core", subcore_axis_name="subcore")


def _sc_worker(mesh):
    return lax.axis_index("core") * mesh.num_subcores + lax.axis_index("subcore"), mesh.num_cores * mesh.num_subcores


def _sc_scatter_rows(src, dest_rows, n_out):
    n, width = src.shape
    nk = dest_rows.shape[0]
    mesh = _sc_mesh()
    half = SC_WINDOW // 2

    @pl.kernel(out_type=jax.ShapeDtypeStruct((n_out, width), src.dtype), mesh=mesh,
               scratch_types=[pltpu.VMEM((nk, SC_WINDOW), I32), pltpu.VMEM((2, half, width), src.dtype),
                              pltpu.SemaphoreType.DMA((2,)), pltpu.SemaphoreType.DMA((nk,))],
               name="sc_scatter_rows")
    def scatter_kernel(src_hbm, idx_hbm, out_hbm, idx_v, rows_v, sem, psem):
        wid, nw = _sc_worker(mesh)
        per = n // SC_WINDOW // nw
        row0 = wid * per * SC_WINDOW

        def load(s, buf):
            return pltpu.make_async_copy(src_hbm.at[pl.ds(row0 + s * half, half)], rows_v.at[buf], sem.at[buf])

        load(0, 0).start()

        @pl.loop(0, per)
        def _(j):
            pltpu.sync_copy(idx_hbm.at[:, pl.ds(row0 + j * SC_WINDOW, SC_WINDOW)], idx_v)
            for h in range(2):
                if h == 0:
                    load(2 * j + 1, 1).start()
                else:
                    @pl.when(j + 1 < per)
                    def _():
                        load(2 * j + 2, 0).start()

                load(2 * j + h, h).wait()
                puts = [pltpu.make_async_copy(rows_v.at[h], out_hbm.at[idx_v.at[k, pl.ds(h * half, half)]], psem.at[k])
                        for k in range(nk)]
                for cp in puts:
                    cp.start()
                for cp in puts:
                    cp.wait()

    return scatter_kernel(src, dest_rows)


def _sc_gather_rows(src, rows):
    n = rows.shape[0]
    width = src.shape[1]
    mesh = _sc_mesh()
    half = SC_WINDOW // 2

    @pl.kernel(out_type=jax.ShapeDtypeStruct((n, width), src.dtype), mesh=mesh,
               scratch_types=[pltpu.VMEM((1, SC_WINDOW), I32), pltpu.VMEM((2, half, width), src.dtype),
                              pltpu.SemaphoreType.DMA((2,)), pltpu.SemaphoreType.DMA((2,))],
               name="sc_gather_rows")
    def gather_kernel(src_hbm, idx_hbm, out_hbm, idx_v, rows_v, sem, gsem):
        wid, nw = _sc_worker(mesh)
        per = n // SC_WINDOW // nw
        row0 = wid * per * SC_WINDOW

        def store(s, buf):
            return pltpu.make_async_copy(rows_v.at[buf], out_hbm.at[pl.ds(row0 + s * half, half)], sem.at[buf])

        def fetch(buf):
            return pltpu.make_async_copy(src_hbm.at[idx_v.at[0, pl.ds(buf * half, half)]], rows_v.at[buf], gsem.at[buf])

        @pl.loop(0, per)
        def _(j):
            pltpu.sync_copy(idx_hbm.at[:, pl.ds(row0 + j * SC_WINDOW, SC_WINDOW)], idx_v)
            for h in range(2):
                @pl.when(j >= 1)
                def _():
                    store(2 * j + h - 2, h).wait()

                fetch(h).start()
            for h in range(2):
                fetch(h).wait()
                store(2 * j + h, h).start()

        for h in range(2):
            store(2 * per - 2 + h, h).wait()

    return gather_kernel(src, rows.reshape(1, n))


def _expert_group_kernel(eb_ref, xs_hbm, wgu_hbm, bgu_ref, wd_hbm, bd_ref, ys_hbm,
                         xbuf, ybuf, wgu_f, wd_f, wgu_bf, wd_bf, act_scr, xsem, ysem, wsem, *, dff, l):
    e = pl.program_id(0)
    ne = pl.num_programs(0)
    b0 = eb_ref[e]
    nb = eb_ref[ne + e]
    rows, half = xbuf.shape[1], xbuf.shape[2]
    group = rows // MOE_ROWS
    nch = (nb + group - 1) // group
    wslot = e % 2

    def w_copies(ex, slot):
        cps = []
        for src, dst, first in ((wgu_hbm, wgu_f, 0), (wd_hbm, wd_f, WEIGHT_CHUNKS)):
            step = dst.shape[1] // WEIGHT_CHUNKS
            for c in range(WEIGHT_CHUNKS):
                cps.append(pltpu.make_async_copy(src.at[l, ex, pl.ds(c * step, step), :],
                                                 dst.at[slot, pl.ds(c * step, step), :], wsem.at[slot, first + c]))
        return cps

    @pl.when(e == 0)
    def _():
        for c, cp in enumerate(w_copies(0, 0)):
            cp.start(priority=c % 2)

    @pl.when(e + 1 < ne)
    def _():
        for c, cp in enumerate(w_copies(e + 1, 1 - wslot)):
            cp.start(priority=c % 2)

    for cp in w_copies(e, wslot):
        cp.wait()

    def x_copy(j, slot):
        start = (b0 + j * group) * MOE_ROWS
        return pltpu.make_async_copy(xs_hbm.at[pl.ds(start, rows), :], xbuf.at[slot], xsem.at[slot])

    def y_copy(j, slot, g):
        start = (b0 + j * group + g) * MOE_ROWS
        return pltpu.make_async_copy(ybuf.at[slot, pl.ds(g * MOE_ROWS, MOE_ROWS), :],
                                     ys_hbm.at[pl.ds(start, MOE_ROWS), :], ysem.at[slot, g])

    def y_each(j, slot, fn):
        for g in range(group):
            @pl.when(j * group + g < nb)
            def _():
                fn(y_copy(j, slot, g))

    @pl.when(nb > 0)
    def _():
        x_copy(0, 0).start(priority=1)
        wgu_bf[:, 0:2 * dff] = wgu_f[wslot].astype(BF16)
        wd_bf[:, 0:2 * half] = wd_f[wslot].astype(BF16)

        def chunk(j, carry):
            slot = j % 2

            @pl.when(j + 1 < nch)
            def _():
                x_copy(j + 1, 1 - slot).start(priority=1)

            x_copy(j, slot).wait()

            @pl.when(j >= 2)
            def _():
                y_each(j - 2, slot, lambda cp: cp.wait())

            lo, hi = _unpack_bf16_pairs(xbuf[slot])
            xb = jnp.concatenate([lo.astype(BF16), hi.astype(BF16)], axis=1)
            for c0 in range(0, dff, EXPERT_COLS):
                gate = _dot(xb, wgu_bf[:, c0:c0 + EXPERT_COLS]) + bgu_ref[:, c0:c0 + EXPERT_COLS]
                up = (_dot(xb, wgu_bf[:, dff + c0:dff + c0 + EXPERT_COLS])
                      + bgu_ref[:, dff + c0:dff + c0 + EXPERT_COLS])
                glu = jnp.minimum(gate, SWIGLU_LIMIT)
                lin = jnp.clip(up, -SWIGLU_LIMIT, SWIGLU_LIMIT)
                act_scr[:, c0:c0 + EXPERT_COLS] = (glu * _sigmoid(SWIGLU_ALPHA * glu) * (lin + 1.0)).astype(BF16)
            ybuf[slot] = _pack_bf16_pairs(_dot(act_scr[...], wd_bf[:, 0:2 * half]) + bd_ref[...])
            y_each(j, slot, lambda cp: cp.start(priority=1))
            return carry

        lax.fori_loop(0, nch, chunk, 0)

        @pl.when(nch >= 2)
        def _():
            y_each(nch - 2, nch % 2, lambda cp: cp.wait())

        y_each(nch - 1, (nch - 1) % 2, lambda cp: cp.wait())


def _block_experts(eblk, xs, w_gu, b_gu4, w_down, b_down4, l):
    cap, half = xs.shape
    d = 2 * half
    dff = w_down.shape[2]
    wsel = lambda e, eb: (l, e, 0, 0)
    grid_spec = pltpu.PrefetchScalarGridSpec(
        num_scalar_prefetch=1,
        grid=(N_EXPERTS,),
        in_specs=[
            pl.BlockSpec(memory_space=pl.ANY),
            pl.BlockSpec(memory_space=pl.ANY),
            pl.BlockSpec((None, None, 1, 2 * dff), wsel),
            pl.BlockSpec(memory_space=pl.ANY),
            pl.BlockSpec((None, None, 1, d), wsel),
        ],
        out_specs=pl.BlockSpec(memory_space=pl.ANY),
        scratch_shapes=[
            pltpu.VMEM((2, EXPERT_GROUP * MOE_ROWS, half), U32),
            pltpu.VMEM((2, EXPERT_GROUP * MOE_ROWS, half), U32),
            pltpu.VMEM((2, d, 2 * dff), F32),
            pltpu.VMEM((2, dff, d), F32),
            pltpu.VMEM((d, 2 * dff + LANES), BF16),
            pltpu.VMEM((dff, d + LANES), BF16),
            pltpu.VMEM((EXPERT_GROUP * MOE_ROWS, dff), BF16),
            pltpu.SemaphoreType.DMA((2,)),
            pltpu.SemaphoreType.DMA((2, EXPERT_GROUP)),
            pltpu.SemaphoreType.DMA((2, 2 * WEIGHT_CHUNKS)),
        ],
    )
    return pl.pallas_call(
        functools.partial(_expert_group_kernel, dff=dff, l=l),
        out_shape=jax.ShapeDtypeStruct((cap - (EXPERT_GROUP - 1) * MOE_ROWS, half), U32),
        grid_spec=grid_spec,
        compiler_params=_params(("arbitrary",)),
        name="moe_expert_groups",
    )(eblk, xs, w_gu, b_gu4, w_down, b_down4)


def _combine_kernel(*refs, d, final):
    y_refs = refs[:TOP_K]
    gt_ref, x_ref, mod_ref = refs[TOP_K:TOP_K + 3]
    xo_ref = refs[-1]
    half = d // 2
    gt = gt_ref[...]
    acc_lo = acc_hi = None
    for k in range(TOP_K):
        lo, hi = _unpack_bf16_pairs(y_refs[k][...])
        g = gt[:, k:k + 1]
        acc_lo = g * lo if acc_lo is None else acc_lo + g * lo
        acc_hi = g * hi if acc_hi is None else acc_hi + g * hi
    x_lo = x_ref[:, 0:half] + mod_ref[:, 5 * d:5 * d + half] * acc_lo
    x_hi = x_ref[:, half:d] + mod_ref[:, 5 * d + half:6 * d] * acc_hi
    if final:
        gain = refs[TOP_K + 3][...]
        ssq = jnp.sum(x_lo * x_lo, axis=-1, keepdims=True) + jnp.sum(x_hi * x_hi, axis=-1, keepdims=True)
        inv = lax.rsqrt(ssq * (1.0 / d) + EPS)
        x_lo = x_lo * inv * gain[:, 0:half]
        x_hi = x_hi * inv * gain[:, half:d]
    xo_ref[:, 0:half] = x_lo
    xo_ref[:, half:d] = x_hi


def _combine(y4p, gates_t, x2, mod3, l, seq, depth, row0, prev, final_gain=None):
    t, d = x2.shape
    n = gates_t.shape[0]
    tm = min(PROJ_TILE, seq)
    nb = mod3.shape[0] // depth
    nt = n // tm
    t0 = row0 // tm
    final = final_gain is not None

    def slot(k):
        return pl.BlockSpec((tm, d // 2), lambda i: (k * nt + i, 0))

    specs = [slot(k) for k in range(TOP_K)] + [
        pl.BlockSpec((tm, TOP_K), lambda i: (i, 0)),
        pl.BlockSpec((tm, d), lambda i: (t0 + i, 0)),
        pl.BlockSpec((None, 1, N_MOD * d), lambda i: (l * nb + (row0 + i * tm) // seq, 0, 0)),
    ]
    args = [y4p] * TOP_K + [gates_t, x2, mod3]
    if final:
        specs.append(pl.BlockSpec((1, d), lambda i: (0, 0)))
        args.append(final_gain.reshape(1, d))
    aliases = {}
    if prev is not None:
        specs.append(pl.BlockSpec(memory_space=pl.ANY))
        aliases = {len(args): 0}
        args.append(prev)
    return pl.pallas_call(
        functools.partial(_combine_kernel, d=d, final=final),
        out_shape=jax.ShapeDtypeStruct((t, d), F32),
        grid=(nt,),
        in_specs=specs,
        out_specs=pl.BlockSpec((tm, d), lambda i: (t0 + i, 0)),
        input_output_aliases=aliases,
        compiler_params=_params(("arbitrary",)),
        name="moe_combine",
    )(*args)


def _mixer_layer(x2, mod3, tables, lower, consts, p, l, batch, seq):
    ra, rb, rc = tables
    qa, ka2, va2, hg = _inproj(x2, mod3, p["norm1"], p["w_in"], ra, rb, rc, l, seq)
    a = _attention(qa, ka2, va2, p["attn_sink"], p["attn_norm"], l, batch, seq)
    o = _hgrn(hg, lower, p["hg_norm"], consts, l, batch, seq)
    return a, o


def _moe_layer(a, o, x2, mod3, p, l, seq, depth, final_gain=None):
    t, d = x2.shape
    xn, h2p, topi, gates = _outproj(a, o, x2, mod3, p["norm2"], p["w_out"], p["w_router_t"], p["b_router"], l, seq)
    rank, counts = _ranks(topi)
    nblocks = (t * TOP_K + N_EXPERTS * MOE_ROWS) // MOE_ROWS
    dest, eblk = _destinations(counts, topi, rank)
    xs = _sc_scatter_rows(h2p, dest, (nblocks + EXPERT_GROUP - 1) * MOE_ROWS)
    ys = _block_experts(eblk[:, :, 0].reshape(-1), xs, p["w_gu"], p["b_gu"], p["w_down"], p["b_down"], l)
    gates_t = gates.T
    out = None
    for part in range(MOE_COMBINE_PARTS):
        rows = slice(part * (t // MOE_COMBINE_PARTS), (part + 1) * (t // MOE_COMBINE_PARTS))
        y4p = _sc_gather_rows(ys, dest[:, rows].reshape(-1))
        out = _combine(y4p, gates_t[rows], xn, mod3, l, seq, depth, rows.start, out, final_gain)
    return out


def kernel(x, c, positions, w_ada, b_ada, norm1, w_in, attn_sink, attn_norm, hg_lb_logits, hg_norm, w_out, norm2,
           w_router, b_router, w_gu, b_gu, w_down, b_down, final_norm):
    batch, seq, d = x.shape
    depth = w_ada.shape[0]
    t = batch * seq
    p = {
        "norm1": norm1.reshape(depth, 1, d),
        "w_in": w_in.astype(BF16),
        "attn_sink": attn_sink.astype(F32),
        "attn_norm": attn_norm.reshape(depth, 1, ATT_WIDTH),
        "hg_norm": hg_norm.reshape(depth, 1, HG_HEAD_DIM),
        "w_out": w_out.astype(BF16),
        "norm2": norm2.reshape(depth, 1, d),
        "w_router_t": jnp.swapaxes(w_router, 1, 2),
        "b_router": b_router.reshape(depth, N_EXPERTS, 1),
        "w_gu": w_gu,
        "b_gu": b_gu.reshape(depth, N_EXPERTS, 1, b_gu.shape[-1]),
        "w_down": w_down,
        "b_down": b_down.reshape(depth, N_EXPERTS, 1, d),
    }
    mod3 = _ada_all(c, w_ada, b_ada).reshape(depth * batch, 1, N_MOD * d)
    lower = _lower_bounds(hg_lb_logits)
    tables = _rope_tables(positions)
    consts = tuple(jnp.asarray(m) for m in _hgrn_constants())
    x2 = x.reshape(t, d)
    for l in range(depth):
        a, o = _mixer_layer(x2, mod3, tables, lower, consts, p, l, batch, seq)
        x2 = _moe_layer(a, o, x2, mod3, p, l, seq, depth, final_norm if l == depth - 1 else None)
    return x2.reshape(batch, seq, d)
```

```python
import functools

import numpy as np
import jax
import jax.numpy as jnp
from jax import lax
from jax.experimental import pallas as pl
from jax.experimental.pallas import tpu as pltpu
from jax.experimental.pallas import tpu_sc as plsc

F32 = jnp.float32
BF16 = jnp.bfloat16
I32 = jnp.int32
U32 = jnp.uint32

ATT_HEADS = 8
ATT_KV_HEADS = 2
ATT_HEAD_DIM = 64
ATT_WIDTH = ATT_HEADS * ATT_HEAD_DIM
KV_WIDTH = ATT_KV_HEADS * ATT_HEAD_DIM
WINDOW = 128
ATT_BLOCK = 128
ROPE_THETA = 500000.0
ROPE_DIM = ATT_HEAD_DIM // 4
HG_HEADS = 4
HG_HEAD_DIM = 128
HG_WIDTH = HG_HEADS * HG_HEAD_DIM
N_EXPERTS = 32
TOP_K = 4
SWIGLU_ALPHA = 1.702
SWIGLU_LIMIT = 7.0
N_MOD = 6
EPS = 1e-6
NEG_INF = -1e30
LB_FLOOR = 1e-30

LANES = 128
HG_CHUNK = 64
HG_LEVELS = 6
HG_UNROLL = 8
MOE_ROWS = 256
EXPERT_GROUP = 2
WEIGHT_CHUNKS = 4
EXPERT_COLS = 256
MOE_COMBINE_PARTS = 2
PROJ_TILE = 512
RANK_TILE = 512
VMEM_LIMIT = 56 * 1024 * 1024


def _dot(a, b):
    return jnp.dot(a, b, preferred_element_type=F32)


def _dot_nt(a, b):
    return lax.dot_general(a, b, (((1,), (1,)), ((), ())), preferred_element_type=F32)


def _dot_tn(a, b):
    return lax.dot_general(a, b, (((0,), (0,)), ((), ())), preferred_element_type=F32)


def _split3(x):
    hi = x.astype(BF16)
    r1 = x - hi.astype(F32)
    mid = r1.astype(BF16)
    lo = (r1 - mid.astype(F32)).astype(BF16)
    return hi, mid, lo


def _dot_f32_nt(a, b):
    ah, am, _ = _split3(a)
    bh, bm, _ = _split3(b)
    return _dot_nt(ah, bh) + _dot_nt(ah, bm) + _dot_nt(am, bh)


def _dot_f32(a, b):
    ah, am, _ = _split3(a)
    bh, bm, _ = _split3(b)
    return _dot(ah, bh) + _dot(ah, bm) + _dot(am, bh)


def _sigmoid(x):
    return 1.0 / (1.0 + jnp.exp(-x))


def _params(sem=None):
    return pltpu.CompilerParams(dimension_semantics=sem, vmem_limit_bytes=VMEM_LIMIT)


def _ada_kernel(c_ref, w_ref, b_ref, o_ref):
    c = c_ref[...]
    cond = c * _sigmoid(c)
    o_ref[...] = _dot_f32(cond, w_ref[...]) + b_ref[...]


def _ada_all(c, w_ada, b_ada):
    depth, d, n = w_ada.shape
    b = c.shape[0]
    nt = n // d
    return pl.pallas_call(
        _ada_kernel,
        out_shape=jax.ShapeDtypeStruct((depth, b, n), F32),
        grid=(depth, nt),
        in_specs=[
            pl.BlockSpec((b, d), lambda l, j: (0, 0)),
            pl.BlockSpec((None, d, d), lambda l, j: (l, 0, j)),
            pl.BlockSpec((None, 1, d), lambda l, j: (l, 0, j)),
        ],
        out_specs=pl.BlockSpec((None, b, d), lambda l, j: (l, 0, j)),
        compiler_params=_params(("arbitrary", "arbitrary")),
        name="ada_mod",
    )(c, w_ada, b_ada.reshape(depth, 1, n))


def _lb_kernel(x_ref, o_ref):
    depth = x_ref.shape[0]
    xs = [x_ref[l] for l in range(depth)]
    m = xs[0]
    for l in range(1, depth):
        m = jnp.maximum(m, xs[l])
    es = [jnp.exp(v - m) for v in xs]
    den = es[0]
    for l in range(1, depth):
        den = den + es[l]
    ps = [e / den for e in es]
    run = ps[0]
    o_ref[0] = run - ps[0]
    for l in range(1, depth):
        run = run + ps[l]
        o_ref[l] = run - ps[0]


def _lower_bounds(hg_lb_logits):
    return pl.pallas_call(
        _lb_kernel,
        out_shape=jax.ShapeDtypeStruct(hg_lb_logits.shape, F32),
        name="hg_lower_bounds",
    )(hg_lb_logits.astype(F32))


def _rope_kernel(pos_ref, invf_ref, a_ref, b_ref, c_ref):
    pos = pos_ref[...].astype(F32)
    ang = pos * invf_ref[...]
    cs = jnp.cos(ang)
    sn = jnp.sin(ang)
    lane = lax.broadcasted_iota(I32, ang.shape, 1) & (ATT_HEAD_DIM - 1)
    half = ROPE_DIM // 2
    first = lane < half
    second = (lane >= half) & (lane < ROPE_DIM)
    a_ref[...] = jnp.where(first | second, cs, 1.0)
    b_ref[...] = jnp.where(first, -sn, 0.0)
    c_ref[...] = jnp.where(second, sn, 0.0)


def _rope_tables(positions):
    t = positions.size
    half = ROPE_DIM // 2
    inv = (np.float32(ROPE_THETA) ** (-(np.arange(half, dtype=np.float32) * np.float32(2.0) / np.float32(ROPE_DIM)))).astype(np.float32)
    lane = np.arange(LANES) % ATT_HEAD_DIM
    pat = np.where(lane < ROPE_DIM, inv[lane % half], 0.0).astype(np.float32).reshape(1, LANES)
    tm = min(t, 2048)
    shp = jax.ShapeDtypeStruct((t, LANES), F32)
    spec = pl.BlockSpec((tm, LANES), lambda i: (i, 0))
    return pl.pallas_call(
        _rope_kernel,
        out_shape=(shp, shp, shp),
        grid=(t // tm,),
        in_specs=[pl.BlockSpec((tm, 1), lambda i: (i, 0)), pl.BlockSpec((1, LANES), lambda i: (0, 0))],
        out_specs=(spec, spec, spec),
        compiler_params=_params(("arbitrary",)),
        name="rope_tables",
    )(positions.reshape(t, 1).astype(I32), jnp.asarray(pat))


def _rms_mod(x, gain, scale, shift):
    ms = jnp.mean(x * x, axis=-1, keepdims=True)
    return (x * lax.rsqrt(ms + EPS) * gain) * (1.0 + scale) + shift


def _rope_apply(x, a, b, c):
    half = ROPE_DIM // 2
    return x * a + pltpu.roll(x, LANES - half, 1) * b + pltpu.roll(x, half, 1) * c


def _inproj_kernel(x_ref, mod_ref, n1_ref, w_ref, ra_ref, rb_ref, rc_ref, qa_ref, ka_ref, va_ref, hg_ref, *, d):
    mod = mod_ref[...]
    h = _rms_mod(x_ref[...], n1_ref[...], mod[:, d:2 * d], mod[:, 0:d]).astype(BF16)
    a, b, c = ra_ref[...], rb_ref[...], rc_ref[...]
    kvw = ATT_WIDTH + 2 * KV_WIDTH
    pa = _dot(h, w_ref[:, 0:kvw])
    scale = ATT_HEAD_DIM ** -0.5
    for g in range(ATT_WIDTH // LANES):
        qg = _rope_apply(pa[:, g * LANES:(g + 1) * LANES], a, b, c)
        qa_ref[:, g * LANES:(g + 1) * LANES] = (qg * scale).astype(BF16)
    k = _rope_apply(pa[:, ATT_WIDTH:ATT_WIDTH + KV_WIDTH], a, b, c)
    v = pa[:, ATT_WIDTH + KV_WIDTH:kvw]
    ka_ref[:, 0:LANES] = k.astype(BF16)
    ka_ref[:, LANES:2 * LANES] = pltpu.roll(k, ATT_HEAD_DIM, 1).astype(BF16)
    va_ref[:, 0:LANES] = v.astype(BF16)
    va_ref[:, LANES:2 * LANES] = pltpu.roll(v, ATT_HEAD_DIM, 1).astype(BF16)
    for g in range(5):
        lo = kvw + g * HG_WIDTH
        hg_ref[:, g * HG_WIDTH:(g + 1) * HG_WIDTH] = _dot(h, w_ref[:, lo:lo + HG_WIDTH])


def _inproj(x2, mod3, norm1, w_in_bf, ra, rb, rc, l, seq):
    t, d = x2.shape
    n_in = w_in_bf.shape[-1]
    tm = min(PROJ_TILE, seq)
    nb = mod3.shape[0] // norm1.shape[0]
    tok = lambda i: (i, 0)
    return pl.pallas_call(
        functools.partial(_inproj_kernel, d=d),
        out_shape=(
            jax.ShapeDtypeStruct((t, ATT_WIDTH), BF16),
            jax.ShapeDtypeStruct((t, 2 * KV_WIDTH), BF16),
            jax.ShapeDtypeStruct((t, 2 * KV_WIDTH), BF16),
            jax.ShapeDtypeStruct((t, 5 * HG_WIDTH), F32),
        ),
        grid=(t // tm,),
        in_specs=[
            pl.BlockSpec((tm, d), tok),
            pl.BlockSpec((None, 1, N_MOD * d), lambda i: (l * nb + (i * tm) // seq, 0, 0)),
            pl.BlockSpec((None, 1, d), lambda i: (l, 0, 0)),
            pl.BlockSpec((None, d, n_in), lambda i: (l, 0, 0)),
            pl.BlockSpec((tm, LANES), tok),
            pl.BlockSpec((tm, LANES), tok),
            pl.BlockSpec((tm, LANES), tok),
        ],
        out_specs=(
            pl.BlockSpec((tm, ATT_WIDTH), tok),
            pl.BlockSpec((tm, 2 * KV_WIDTH), tok),
            pl.BlockSpec((tm, 2 * KV_WIDTH), tok),
            pl.BlockSpec((tm, 5 * HG_WIDTH), tok),
        ),
        compiler_params=_params(("arbitrary",)),
        name="in_proj",
    )(x2, mod3, norm1, w_in_bf, ra, rb, rc)


def _attn_kernel(sink_ref, q_ref, kp_ref, kc_ref, kn_ref, vp_ref, vc_ref, vn_ref, gain_ref, o_ref, *, l, seq):
    n = pl.program_id(1)
    blk = ATT_BLOCK
    k2 = jnp.concatenate([kp_ref[...], kc_ref[...], kn_ref[...]], axis=0)
    v2 = jnp.concatenate([vp_ref[...], vc_ref[...], vn_ref[...]], axis=0)
    lane = lax.broadcasted_iota(I32, (3 * blk, LANES), 1)
    lo_half = lane < ATT_HEAD_DIM
    zero = jnp.zeros((3 * blk, LANES), BF16)
    ka, kb = k2[:, 0:LANES], k2[:, LANES:2 * LANES]
    va, vb = v2[:, 0:LANES], v2[:, LANES:2 * LANES]
    kz = [[jnp.where(lo_half, ka, zero), jnp.where(lo_half, zero, kb)],
          [jnp.where(lo_half, kb, zero), jnp.where(lo_half, zero, ka)]]
    vz = [[jnp.where(lo_half, va, zero), jnp.where(lo_half, zero, vb)],
          [jnp.where(lo_half, vb, zero), jnp.where(lo_half, zero, va)]]
    qpos = n * blk + lax.broadcasted_iota(I32, (blk, 3 * blk), 0)
    kpos = (n - 1) * blk + lax.broadcasted_iota(I32, (blk, 3 * blk), 1)
    valid = (jnp.abs(qpos - kpos) <= WINDOW) & (kpos >= 0) & (kpos < seq)
    valid2 = jnp.concatenate([valid, valid], axis=0)
    upper = lax.broadcasted_iota(I32, (2 * blk, 1), 0) < blk
    outs = []
    for j in range(ATT_KV_HEADS):
        qs = jnp.concatenate([q_ref[:, 2 * j * LANES:(2 * j + 1) * LANES],
                              q_ref[:, (2 * j + 1) * LANES:(2 * j + 2) * LANES]], axis=0)
        acc = None
        for half in range(2):
            s = _dot_nt(qs, kz[j][half])
            s = jnp.where(valid2, s, NEG_INF)
            sink = jnp.where(upper, sink_ref[l, 4 * j + half], sink_ref[l, 4 * j + 2 + half])
            mx = jnp.maximum(jnp.max(s, axis=-1, keepdims=True), sink)
            p = jnp.exp(s - mx)
            den = jnp.sum(p, axis=-1, keepdims=True) + jnp.exp(sink - mx)
            p = (p * (1.0 / den)).astype(BF16)
            pv = _dot(p, vz[j][half])
            acc = pv if acc is None else acc + pv
        outs.append(acc[0:blk])
        outs.append(acc[blk:2 * blk])
    o = jnp.concatenate(outs, axis=-1)
    ms = jnp.mean(o * o, axis=-1, keepdims=True)
    o_ref[...] = (o * lax.rsqrt(ms + EPS) * gain_ref[...]).astype(BF16)


def _attention(qa, ka2, va2, attn_sink, attn_norm3, l, batch, seq):
    t = qa.shape[0]
    blk = ATT_BLOCK
    nb = seq // blk
    cur = lambda b, n: (b * nb + n, 0)
    prev = lambda b, n: (b * nb + jnp.maximum(n - 1, 0), 0)
    nxt = lambda b, n: (b * nb + jnp.minimum(n + 1, nb - 1), 0)
    kvspec = lambda f: pl.BlockSpec((blk, 2 * KV_WIDTH), f)
    return pl.pallas_call(
        functools.partial(_attn_kernel, l=l, seq=seq),
        out_shape=jax.ShapeDtypeStruct((t, ATT_WIDTH), BF16),
        grid=(batch, nb),
        in_specs=[
            pl.BlockSpec(memory_space=pltpu.SMEM),
            pl.BlockSpec((blk, ATT_WIDTH), cur),
            kvspec(prev), kvspec(cur), kvspec(nxt),
            kvspec(prev), kvspec(cur), kvspec(nxt),
            pl.BlockSpec((None, 1, ATT_WIDTH), lambda b, n: (l, 0, 0)),
        ],
        out_specs=pl.BlockSpec((blk, ATT_WIDTH), cur),
        compiler_params=_params(("arbitrary", "arbitrary")),
        name="window_attn",
    )(attn_sink, qa, ka2, ka2, ka2, va2, va2, va2, attn_norm3)


def _hgrn_constants():
    c, nl = HG_CHUNK, HG_LEVELS
    r = np.arange(c)
    masks = []
    for lev in range(nl):
        m = 1 << lev
        parent = r // (2 * m)
        upper = r >= parent * 2 * m + m
        masks.append((parent[:, None] == parent[None, :]) & upper[:, None] & (~upper)[None, :])
    masks.append(np.eye(c, dtype=bool))
    kf = np.stack(masks).astype(np.float32)
    kb = np.stack([mk[::-1, ::-1] for mk in masks]).astype(np.float32)
    return kf, kb


def _chunk_decays(logf, reverse):
    c = logf.shape[0]
    sub = 8
    nv = c // sub
    row = lax.broadcasted_iota(I32, (sub, LANES), 0)
    parts = []
    for v in range(nv):
        x = logf[sub * v:sub * (v + 1), :]
        for s in (1, 2, 4):
            if reverse:
                x = x + jnp.where(row < sub - s, pltpu.roll(x, sub - s, 0), 0.0)
            else:
                x = x + jnp.where(row >= s, pltpu.roll(x, s, 0), 0.0)
        parts.append(x)
    b = [None] * nv
    order = list(reversed(range(nv))) if reverse else list(range(nv))
    edge = 0 if reverse else sub - 1
    carry = None
    for v in order:
        b[v] = parts[v] if carry is None else parts[v] + carry
        carry = b[v][edge:edge + 1, :]
    b_last = carry

    def anchor_row(v, r):
        return jnp.broadcast_to(b[v][r:r + 1, :], (sub, LANES))

    odd = (row & 1) == 1
    levels = []
    for lev in range(HG_LEVELS):
        m = 1 << lev
        pieces = []
        for v in range(nv):
            if m == 1:
                a = jnp.where(odd, pltpu.roll(b[v], 1, 0), b[v]) if reverse else jnp.where(odd, b[v], pltpu.roll(b[v], sub - 1, 0))
            elif m == 2:
                lo, hi = (1, 5) if reverse else (2, 6)
                a = jnp.where(row < 4, anchor_row(v, lo), anchor_row(v, hi))
            elif m == 4:
                a = anchor_row(v, 3 if reverse else 4)
            else:
                mv = m // sub
                first = (v // (2 * mv)) * 2 * mv
                a = anchor_row(first + mv - 1, sub - 1) if reverse else anchor_row(first + mv, 0)
            pieces.append(jnp.exp(-jnp.abs(b[v] - a)))
        levels.append(jnp.concatenate(pieces, axis=0))
    eb = jnp.concatenate([jnp.exp(bv) for bv in b], axis=0)
    erem = jnp.concatenate([jnp.exp(b_last - bv) for bv in b], axis=0)
    return levels, eb, erem


def _hgrn_kernel(q_ref, ff_ref, fb_ref, i_ref, g_ref, lb_ref, gn_ref, kf_ref, kb_ref,
                 o_ref, of_scr, ob_scr, st_scr, *, seq):
    c, nl = HG_CHUNK, HG_LEVELS
    nc = seq // c
    lb = lb_ref[...]
    st_scr[...] = jnp.zeros(st_scr.shape, F32)

    unroll = HG_UNROLL if nc % HG_UNROLL == 0 else 1
    dirs = (
        dict(f_ref=ff_ref, lbrow=lb[0:1, :], k_ref=kf_ref, o_scr=of_scr, last_row=c - 1, d=0),
        dict(f_ref=fb_ref, lbrow=lb[1:2, :], k_ref=kb_ref, o_scr=ob_scr, last_row=0, d=1),
    )

    def body(i, carry):
        work = []
        for u in range(unroll):
            cf = i * unroll + u
            work.append((dirs[0], pl.ds(pl.multiple_of(cf * c, c), c)))
            work.append((dirs[1], pl.ds(pl.multiple_of((nc - 1 - cf) * c, c), c)))
        gates = []
        for dr, rows in work:
            lbf = jnp.maximum(dr["lbrow"], LB_FLOOR)
            oml = 1.0 - dr["lbrow"]
            f = dr["f_ref"][rows, :]
            e = jnp.exp(-jnp.abs(f))
            r = 1.0 / (1.0 + e)
            er = e * r
            pos = f >= 0.0
            logf = jnp.log(lbf + oml * jnp.where(pos, r, er))
            kk = oml * jnp.where(pos, er, r)
            qh = q_ref[rows, :]
            gates.append((logf, kk.astype(BF16), (qh * _sigmoid(qh)).astype(BF16), i_ref[rows, :].astype(BF16)))
        decays = [_chunk_decays(g[0], dr["d"] == 1) for (dr, _), g in zip(work, gates)]
        amat = [dr["k_ref"][nl] * _dot_nt(g[2], g[1]) for (dr, _), g in zip(work, gates)]
        for lev in range(nl):
            for j, ((dr, _), g) in enumerate(zip(work, gates)):
                gl = decays[j][0][lev].astype(BF16)
                amat[j] = amat[j] + dr["k_ref"][lev] * _dot_nt(g[2] * gl, g[1] * gl)
        intra = [_dot(amat[j].astype(BF16), g[3]) for j, g in enumerate(gates)]
        upd = [_dot_tn(g[3], g[1] * decays[j][2].astype(BF16)) for j, g in enumerate(gates)]
        st = [st_scr[0], st_scr[1]]
        for j, ((dr, rows), g) in enumerate(zip(work, gates)):
            eb = decays[j][1]
            d = dr["d"]
            dr["o_scr"][rows, :] = _dot_nt(g[2] * eb.astype(BF16), st[d].astype(BF16)) + intra[j]
            st[d] = st[d] * eb[dr["last_row"]:dr["last_row"] + 1, :] + upd[j]
        st_scr[0] = st[0]
        st_scr[1] = st[1]
        return carry

    lax.fori_loop(0, nc // unroll, body, 0)

    ep = min(256, seq)
    gn = gn_ref[...]

    def epilogue(j, carry):
        rows = pl.ds(pl.multiple_of(j * ep, ep), ep)
        o = of_scr[rows, :] + ob_scr[rows, :]
        y = o * lax.rsqrt(jnp.mean(o * o, axis=-1, keepdims=True) + EPS) * gn
        g = g_ref[rows, :]
        o_ref[rows, :] = (y * (g * _sigmoid(g))).astype(BF16)
        return carry

    lax.fori_loop(0, seq // ep, epilogue, 0)


def _hgrn(hg, lower, hg_norm3, consts, l, batch, seq):
    t = hg.shape[0]
    kf, kb = consts
    hd = HG_HEAD_DIM

    def col(g):
        return pl.BlockSpec((seq, hd), lambda b, h: (b, g * HG_HEADS + h))

    full3 = lambda a: pl.BlockSpec(a.shape, lambda b, h: (0, 0, 0))
    return pl.pallas_call(
        functools.partial(_hgrn_kernel, seq=seq),
        out_shape=jax.ShapeDtypeStruct((t, HG_WIDTH), BF16),
        grid=(batch, HG_HEADS),
        in_specs=[
            col(0), col(1), col(2), col(3), col(4),
            pl.BlockSpec((None, 2, hd), lambda b, h: (l, 0, h)),
            pl.BlockSpec((None, 1, hd), lambda b, h: (l, 0, 0)),
            full3(kf), full3(kb),
        ],
        out_specs=pl.BlockSpec((seq, hd), lambda b, h: (b, h)),
        scratch_shapes=[
            pltpu.VMEM((seq, hd), F32),
            pltpu.VMEM((seq, hd), F32),
            pltpu.VMEM((2, hd, hd), F32),
        ],
        compiler_params=_params(("arbitrary", "arbitrary")),
        name="hgrn2_scan",
    )(hg, hg, hg, hg, hg, lower, hg_norm3, kf, kb)


def _outproj_kernel(a_ref, o_ref, x_ref, mod_ref, n2_ref, w_ref, wr_ref, br_ref,
                    xo_ref, h2_ref, ti_ref, gt_ref, *, d):
    mod = mod_ref[...]
    y = _dot(a_ref[...], w_ref[0:ATT_WIDTH, :]) + _dot(o_ref[...], w_ref[ATT_WIDTH:ATT_WIDTH + HG_WIDTH, :])
    xn = x_ref[...] + mod[:, 2 * d:3 * d] * y
    xo_ref[...] = xn
    h2 = _rms_mod(xn, n2_ref[...], mod[:, 4 * d:5 * d], mod[:, 3 * d:4 * d])
    h2_ref[...] = _pack_bf16_pairs(h2)
    lg = _dot_f32_nt(wr_ref[...], h2) + br_ref[...]
    eidx = lax.broadcasted_iota(I32, lg.shape, 0)
    vals, idxs = [], []
    for _ in range(TOP_K):
        mx = jnp.max(lg, axis=0, keepdims=True)
        sel = jnp.min(jnp.where(lg == mx, eidx, N_EXPERTS), axis=0, keepdims=True)
        vals.append(mx)
        idxs.append(sel)
        lg = jnp.where(eidx == sel, -jnp.inf, lg)
    ex = [jnp.exp(v - vals[0]) for v in vals]
    den = ex[0]
    for e in ex[1:]:
        den = den + e
    inv = 1.0 / den
    ti_ref[...] = jnp.concatenate(idxs, axis=0)
    gt_ref[...] = jnp.concatenate([e * inv for e in ex], axis=0)


def _outproj(a, o, x2, mod3, norm2, w_out_bf, w_router_t, b_router3, l, seq):
    t, d = x2.shape
    tm = min(PROJ_TILE, seq)
    nb = mod3.shape[0] // norm2.shape[0]
    tok = lambda i: (i, 0)
    lane_tok = lambda i: (0, i)
    return pl.pallas_call(
        functools.partial(_outproj_kernel, d=d),
        out_shape=(
            jax.ShapeDtypeStruct((t, d), F32),
            jax.ShapeDtypeStruct((t, d // 2), U32),
            jax.ShapeDtypeStruct((TOP_K, t), I32),
            jax.ShapeDtypeStruct((TOP_K, t), F32),
        ),
        grid=(t // tm,),
        in_specs=[
            pl.BlockSpec((tm, ATT_WIDTH), tok),
            pl.BlockSpec((tm, HG_WIDTH), tok),
            pl.BlockSpec((tm, d), tok),
            pl.BlockSpec((None, 1, N_MOD * d), lambda i: (l * nb + (i * tm) // seq, 0, 0)),
            pl.BlockSpec((None, 1, d), lambda i: (l, 0, 0)),
            pl.BlockSpec((None, ATT_WIDTH + HG_WIDTH, d), lambda i: (l, 0, 0)),
            pl.BlockSpec((None, N_EXPERTS, d), lambda i: (l, 0, 0)),
            pl.BlockSpec((None, N_EXPERTS, 1), lambda i: (l, 0, 0)),
        ],
        out_specs=(
            pl.BlockSpec((tm, d), tok),
            pl.BlockSpec((tm, d // 2), tok),
            pl.BlockSpec((TOP_K, tm), lane_tok),
            pl.BlockSpec((TOP_K, tm), lane_tok),
        ),
        compiler_params=_params(("arbitrary",)),
        name="out_proj_router",
    )(a, o, x2, mod3, norm2, w_out_bf, w_router_t, b_router3)


def _rank_kernel(ti_ref, tri_ref, rank_ref, cnt_ref, carry_scr):
    @pl.when(pl.program_id(0) == 0)
    def _():
        carry_scr[...] = jnp.zeros(carry_scr.shape, F32)

    ti = ti_ref[...]
    tl = ti.shape[1]
    eidx = lax.broadcasted_iota(I32, (N_EXPERTS, tl), 0)
    carry = carry_scr[...]
    rows = []
    for k in range(TOP_K):
        oh = eidx == ti[k:k + 1, :]
        ohf = jnp.where(oh, 1.0, 0.0)
        pre = _dot(ohf.astype(BF16), tri_ref[...])
        rows.append(jnp.sum(jnp.where(oh, carry + pre, 0.0), axis=0, keepdims=True))
        carry = carry + jnp.sum(ohf, axis=1, keepdims=True)
    carry_scr[...] = carry
    rank_ref[...] = jnp.concatenate(rows, axis=0).astype(I32)
    cnt_ref[...] = jnp.broadcast_to(carry, cnt_ref.shape)


def _ranks(topi):
    k, t = topi.shape
    tl = min(RANK_TILE, t)
    tri = np.triu(np.ones((tl, tl), np.float32), 1)
    return pl.pallas_call(
        _rank_kernel,
        out_shape=(jax.ShapeDtypeStruct((k, t), I32), jax.ShapeDtypeStruct((N_EXPERTS, LANES), F32)),
        grid=(t // tl,),
        in_specs=[pl.BlockSpec((k, tl), lambda i: (0, i)), pl.BlockSpec((tl, tl), lambda i: (0, 0))],
        out_specs=(pl.BlockSpec((k, tl), lambda i: (0, i)), pl.BlockSpec((N_EXPERTS, LANES), lambda i: (0, 0))),
        scratch_shapes=[pltpu.VMEM((N_EXPERTS, 1), F32)],
        compiler_params=_params(("arbitrary",)),
        name="route_rank",
    )(topi, jnp.asarray(tri, BF16))


def _dest_kernel(cnt_ref, ltri_ref, ti_ref, rank_ref, dest_ref, eblk_ref):
    cnt = cnt_ref[...]
    nblk = jnp.floor((cnt + (MOE_ROWS - 1)) * (1.0 / MOE_ROWS))
    pstart_b = _dot(ltri_ref[...], nblk.astype(BF16))
    pstart = (pstart_b[:, 0:1] * MOE_ROWS).astype(I32)
    ti = ti_ref[...]
    tl = ti.shape[1]
    eidx = lax.broadcasted_iota(I32, (N_EXPERTS, tl), 0)
    rows = []
    for k in range(TOP_K):
        oh = eidx == ti[k:k + 1, :]
        rows.append(jnp.sum(jnp.where(oh, pstart, 0), axis=0, keepdims=True))
    dest_ref[...] = jnp.concatenate(rows, axis=0) + rank_ref[...]
    eblk_ref[0] = pstart_b.astype(I32)
    eblk_ref[1] = nblk.astype(I32)


def _destinations(counts, topi, rank):
    k, t = topi.shape
    tl = min(2048, t)
    ltri = np.tril(np.ones((N_EXPERTS, N_EXPERTS), np.float32), -1)
    return pl.pallas_call(
        _dest_kernel,
        out_shape=(
            jax.ShapeDtypeStruct((k, t), I32),
            jax.ShapeDtypeStruct((2, N_EXPERTS, LANES), I32),
        ),
        grid=(t // tl,),
        in_specs=[
            pl.BlockSpec((N_EXPERTS, LANES), lambda i: (0, 0)),
            pl.BlockSpec((N_EXPERTS, N_EXPERTS), lambda i: (0, 0)),
            pl.BlockSpec((k, tl), lambda i: (0, i)),
            pl.BlockSpec((k, tl), lambda i: (0, i)),
        ],
        out_specs=(
            pl.BlockSpec((k, tl), lambda i: (0, i)),
            pl.BlockSpec((2, N_EXPERTS, LANES), lambda i: (0, 0, 0)),
        ),
        compiler_params=_params(("arbitrary",)),
        name="route_dest",
    )(counts, jnp.asarray(ltri, BF16), topi, rank)


def _pack_bf16_pairs(x):
    n = x.shape[1] // 2
    lo = lax.bitcast_convert_type(x[:, :n].astype(BF16).astype(F32), U32)
    hi = lax.bitcast_convert_type(x[:, n:].astype(BF16).astype(F32), U32)
    return hi | (lo >> 16)


def _unpack_bf16_pairs(w):
    lo = lax.bitcast_convert_type(w << 16, F32)
    hi = lax.bitcast_convert_type(w & jnp.uint32(0xFFFF0000), F32)
    return lo, hi


SC_WINDOW = 128


def _sc_mesh():
    return plsc.VectorSubcoreMesh(core_axis_name="core", subcore_axis_name="subcore")


def _sc_worker(mesh):
    return lax.axis_index("core") * mesh.num_subcores + lax.axis_index("subcore"), mesh.num_cores * mesh.num_subcores


def _sc_scatter_rows(src, dest_rows, n_out):
    n, width = src.shape
    nk = dest_rows.shape[0]
    mesh = _sc_mesh()
    half = SC_WINDOW // 2

    @pl.kernel(out_type=jax.ShapeDtypeStruct((n_out, width), src.dtype), mesh=mesh,
               scratch_types=[pltpu.VMEM((nk, SC_WINDOW), I32), pltpu.VMEM((2, half, width), src.dtype),
                              pltpu.SemaphoreType.DMA((2,)), pltpu.SemaphoreType.DMA((nk,))],
               name="sc_scatter_rows")
    def scatter_kernel(src_hbm, idx_hbm, out_hbm, idx_v, rows_v, sem, psem):
        wid, nw = _sc_worker(mesh)
        per = n // SC_WINDOW // nw
        row0 = wid * per * SC_WINDOW

        def load(s, buf):
            return pltpu.make_async_copy(src_hbm.at[pl.ds(row0 + s * half, half)], rows_v.at[buf], sem.at[buf])

        load(0, 0).start()

        @pl.loop(0, per)
        def _(j):
            pltpu.sync_copy(idx_hbm.at[:, pl.ds(row0 + j * SC_WINDOW, SC_WINDOW)], idx_v)
            for h in range(2):
                if h == 0:
                    load(2 * j + 1, 1).start()
                else:
                    @pl.when(j + 1 < per)
                    def _():
                        load(2 * j + 2, 0).start()

                load(2 * j + h, h).wait()
                puts = [pltpu.make_async_copy(rows_v.at[h], out_hbm.at[idx_v.at[k, pl.ds(h * half, half)]], psem.at[k])
                        for k in range(nk)]
                for cp in puts:
                    cp.start()
                for cp in puts:
                    cp.wait()

    return scatter_kernel(src, dest_rows)


def _sc_gather_rows(src, rows):
    n = rows.shape[0]
    width = src.shape[1]
    mesh = _sc_mesh()
    half = SC_WINDOW // 2

    @pl.kernel(out_type=jax.ShapeDtypeStruct((n, width), src.dtype), mesh=mesh,
               scratch_types=[pltpu.VMEM((1, SC_WINDOW), I32), pltpu.VMEM((2, half, width), src.dtype),
                              pltpu.SemaphoreType.DMA((2,)), pltpu.SemaphoreType.DMA((2,))],
               name="sc_gather_rows")
    def gather_kernel(src_hbm, idx_hbm, out_hbm, idx_v, rows_v, sem, gsem):
        wid, nw = _sc_worker(mesh)
        per = n // SC_WINDOW // nw
        row0 = wid * per * SC_WINDOW

        def store(s, buf):
            return pltpu.make_async_copy(rows_v.at[buf], out_hbm.at[pl.ds(row0 + s * half, half)], sem.at[buf])

        def fetch(buf):
            return pltpu.make_async_copy(src_hbm.at[idx_v.at[0, pl.ds(buf * half, half)]], rows_v.at[buf], gsem.at[buf])

        @pl.loop(0, per)
        def _(j):
            pltpu.sync_copy(idx_hbm.at[:, pl.ds(row0 + j * SC_WINDOW, SC_WINDOW)], idx_v)
            for h in range(2):
                @pl.when(j >= 1)
                def _():
                    store(2 * j + h - 2, h).wait()

                fetch(h).start()
            for h in range(2):
                fetch(h).wait()
                store(2 * j + h, h).start()

        for h in range(2):
            store(2 * per - 2 + h, h).wait()

    return gather_kernel(src, rows.reshape(1, n))


def _expert_group_kernel(eb_ref, xs_hbm, wgu_hbm, bgu_ref, wd_hbm, bd_ref, ys_hbm,
                         xbuf, ybuf, wgu_f, wd_f, wgu_bf, wd_bf, act_scr, xsem, ysem, wsem, *, dff, l):
    e = pl.program_id(0)
    ne = pl.num_programs(0)
    b0 = eb_ref[e]
    nb = eb_ref[ne + e]
    rows, half = xbuf.shape[1], xbuf.shape[2]
    group = rows // MOE_ROWS
    nch = (nb + group - 1) // group
    wslot = e % 2

    def w_copies(ex, slot):
        cps = []
        for src, dst, first in ((wgu_hbm, wgu_f, 0), (wd_hbm, wd_f, WEIGHT_CHUNKS)):
            step = dst.shape[1] // WEIGHT_CHUNKS
            for c in range(WEIGHT_CHUNKS):
                cps.append(pltpu.make_async_copy(src.at[l, ex, pl.ds(c * step, step), :],
                                                 dst.at[slot, pl.ds(c * step, step), :], wsem.at[slot, first + c]))
        return cps

    @pl.when(e == 0)
    def _():
        for c, cp in enumerate(w_copies(0, 0)):
            cp.start(priority=c % 2)

    @pl.when(e + 1 < ne)
    def _():
        for c, cp in enumerate(w_copies(e + 1, 1 - wslot)):
            cp.start(priority=c % 2)

    for cp in w_copies(e, wslot):
        cp.wait()

    def x_copy(j, slot):
        start = (b0 + j * group) * MOE_ROWS
        return pltpu.make_async_copy(xs_hbm.at[pl.ds(start, rows), :], xbuf.at[slot], xsem.at[slot])

    def y_copy(j, slot, g):
        start = (b0 + j * group + g) * MOE_ROWS
        return pltpu.make_async_copy(ybuf.at[slot, pl.ds(g * MOE_ROWS, MOE_ROWS), :],
                                     ys_hbm.at[pl.ds(start, MOE_ROWS), :], ysem.at[slot, g])

    def y_each(j, slot, fn):
        for g in range(group):
            @pl.when(j * group + g < nb)
            def _():
                fn(y_copy(j, slot, g))

    @pl.when(nb > 0)
    def _():
        x_copy(0, 0).start(priority=1)
        wgu_bf[:, 0:2 * dff] = wgu_f[wslot].astype(BF16)
        wd_bf[:, 0:2 * half] = wd_f[wslot].astype(BF16)

        def chunk(j, carry):
            slot = j % 2

            @pl.when(j + 1 < nch)
            def _():
                x_copy(j + 1, 1 - slot).start(priority=1)

            x_copy(j, slot).wait()

            @pl.when(j >= 2)
            def _():
                y_each(j - 2, slot, lambda cp: cp.wait())

            def compute(nr):
                lo, hi = _unpack_bf16_pairs(xbuf[slot, 0:nr, :])
                xb = jnp.concatenate([lo.astype(BF16), hi.astype(BF16)], axis=1)
                for c0 in range(0, dff, EXPERT_COLS):
                    gate = _dot(xb, wgu_bf[:, c0:c0 + EXPERT_COLS]) + bgu_ref[:, c0:c0 + EXPERT_COLS]
                    up = (_dot(xb, wgu_bf[:, dff + c0:dff + c0 + EXPERT_COLS])
                          + bgu_ref[:, dff + c0:dff + c0 + EXPERT_COLS])
                    glu = jnp.minimum(gate, SWIGLU_LIMIT)
                    lin = jnp.clip(up, -SWIGLU_LIMIT, SWIGLU_LIMIT)
                    act_scr[0:nr, c0:c0 + EXPERT_COLS] = (
                        glu * _sigmoid(SWIGLU_ALPHA * glu) * (lin + 1.0)).astype(BF16)
                ybuf[slot, 0:nr, :] = _pack_bf16_pairs(_dot(act_scr[0:nr, :], wd_bf[:, 0:2 * half]) + bd_ref[...])

            few = (nb - j * group) * 2 <= group

            @pl.when(few)
            def _():
                compute(rows // 2)

            @pl.when(jnp.logical_not(few))
            def _():
                compute(rows)

            y_each(j, slot, lambda cp: cp.start(priority=1))
            return carry

        lax.fori_loop(0, nch, chunk, 0)

        @pl.when(nch >= 2)
        def _():
            y_each(nch - 2, nch % 2, lambda cp: cp.wait())

        y_each(nch - 1, (nch - 1) % 2, lambda cp: cp.wait())


def _block_experts(eblk, xs, w_gu, b_gu4, w_down, b_down4, l):
    cap, half = xs.shape
    d = 2 * half
    dff = w_down.shape[2]
    wsel = lambda e, eb: (l, e, 0, 0)
    grid_spec = pltpu.PrefetchScalarGridSpec(
        num_scalar_prefetch=1,
        grid=(N_EXPERTS,),
        in_specs=[
            pl.BlockSpec(memory_space=pl.ANY),
            pl.BlockSpec(memory_space=pl.ANY),
            pl.BlockSpec((None, None, 1, 2 * dff), wsel),
            pl.BlockSpec(memory_space=pl.ANY),
            pl.BlockSpec((None, None, 1, d), wsel),
        ],
        out_specs=pl.BlockSpec(memory_space=pl.ANY),
        scratch_shapes=[
            pltpu.VMEM((2, EXPERT_GROUP * MOE_ROWS, half), U32),
            pltpu.VMEM((2, EXPERT_GROUP * MOE_ROWS, half), U32),
            pltpu.VMEM((2, d, 2 * dff), F32),
            pltpu.VMEM((2, dff, d), F32),
            pltpu.VMEM((d, 2 * dff + LANES), BF16),
            pltpu.VMEM((dff, d + LANES), BF16),
            pltpu.VMEM((EXPERT_GROUP * MOE_ROWS, dff), BF16),
            pltpu.SemaphoreType.DMA((2,)),
            pltpu.SemaphoreType.DMA((2, EXPERT_GROUP)),
            pltpu.SemaphoreType.DMA((2, 2 * WEIGHT_CHUNKS)),
        ],
    )
    return pl.pallas_call(
        functools.partial(_expert_group_kernel, dff=dff, l=l),
        out_shape=jax.ShapeDtypeStruct((cap - (EXPERT_GROUP - 1) * MOE_ROWS, half), U32),
        grid_spec=grid_spec,
        compiler_params=_params(("arbitrary",)),
        name="moe_expert_groups",
    )(eblk, xs, w_gu, b_gu4, w_down, b_down4)


def _combine_kernel(*refs, d, final):
    y_refs = refs[:TOP_K]
    gt_ref, x_ref, mod_ref = refs[TOP_K:TOP_K + 3]
    xo_ref = refs[-1]
    half = d // 2
    gt = gt_ref[...]
    acc_lo = acc_hi = None
    for k in range(TOP_K):
        lo, hi = _unpack_bf16_pairs(y_refs[k][...])
        g = gt[:, k:k + 1]
        acc_lo = g * lo if acc_lo is None else acc_lo + g * lo
        acc_hi = g * hi if acc_hi is None else acc_hi + g * hi
    x_lo = x_ref[:, 0:half] + mod_ref[:, 5 * d:5 * d + half] * acc_lo
    x_hi = x_ref[:, half:d] + mod_ref[:, 5 * d + half:6 * d] * acc_hi
    if final:
        gain = refs[TOP_K + 3][...]
        ssq = jnp.sum(x_lo * x_lo, axis=-1, keepdims=True) + jnp.sum(x_hi * x_hi, axis=-1, keepdims=True)
        inv = lax.rsqrt(ssq * (1.0 / d) + EPS)
        x_lo = x_lo * inv * gain[:, 0:half]
        x_hi = x_hi * inv * gain[:, half:d]
    xo_ref[:, 0:half] = x_lo
    xo_ref[:, half:d] = x_hi


def _combine(y4p, gates_t, x2, mod3, l, seq, depth, row0, prev, final_gain=None):
    t, d = x2.shape
    n = gates_t.shape[0]
    tm = min(PROJ_TILE, seq)
    nb = mod3.shape[0] // depth
    nt = n // tm
    t0 = row0 // tm
    final = final_gain is not None

    def slot(k):
        return pl.BlockSpec((tm, d // 2), lambda i: (k * nt + i, 0))

    specs = [slot(k) for k in range(TOP_K)] + [
        pl.BlockSpec((tm, TOP_K), lambda i: (i, 0)),
        pl.BlockSpec((tm, d), lambda i: (t0 + i, 0)),
        pl.BlockSpec((None, 1, N_MOD * d), lambda i: (l * nb + (row0 + i * tm) // seq, 0, 0)),
    ]
    args = [y4p] * TOP_K + [gates_t, x2, mod3]
    if final:
        specs.append(pl.BlockSpec((1, d), lambda i: (0, 0)))
        args.append(final_gain.reshape(1, d))
    aliases = {}
    if prev is not None:
        specs.append(pl.BlockSpec(memory_space=pl.ANY))
        aliases = {len(args): 0}
        args.append(prev)
    return pl.pallas_call(
        functools.partial(_combine_kernel, d=d, final=final),
        out_shape=jax.ShapeDtypeStruct((t, d), F32),
        grid=(nt,),
        in_specs=specs,
        out_specs=pl.BlockSpec((tm, d), lambda i: (t0 + i, 0)),
        input_output_aliases=aliases,
        compiler_params=_params(("arbitrary",)),
        name="moe_combine",
    )(*args)


def _mixer_layer(x2, mod3, tables, lower, consts, p, l, batch, seq):
    ra, rb, rc = tables
    qa, ka2, va2, hg = _inproj(x2, mod3, p["norm1"], p["w_in"], ra, rb, rc, l, seq)
    a = _attention(qa, ka2, va2, p["attn_sink"], p["attn_norm"], l, batch, seq)
    o = _hgrn(hg, lower, p["hg_norm"], consts, l, batch, seq)
    return a, o


def _moe_layer(a, o, x2, mod3, p, l, seq, depth, final_gain=None):
    t, d = x2.shape
    xn, h2p, topi, gates = _outproj(a, o, x2, mod3, p["norm2"], p["w_out"], p["w_router_t"], p["b_router"], l, seq)
    rank, counts = _ranks(topi)
    nblocks = (t * TOP_K + N_EXPERTS * MOE_ROWS) // MOE_ROWS
    dest, eblk = _destinations(counts, topi, rank)
    xs = _sc_scatter_rows(h2p, dest, (nblocks + EXPERT_GROUP - 1) * MOE_ROWS)
    ys = _block_experts(eblk[:, :, 0].reshape(-1), xs, p["w_gu"], p["b_gu"], p["w_down"], p["b_down"], l)
    gates_t = gates.T
    out = None
    for part in range(MOE_COMBINE_PARTS):
        rows = slice(part * (t // MOE_COMBINE_PARTS), (part + 1) * (t // MOE_COMBINE_PARTS))
        y4p = _sc_gather_rows(ys, dest[:, rows].reshape(-1))
        out = _combine(y4p, gates_t[rows], xn, mod3, l, seq, depth, rows.start, out, final_gain)
    return out


def kernel(x, c, positions, w_ada, b_ada, norm1, w_in, attn_sink, attn_norm, hg_lb_logits, hg_norm, w_out, norm2,
           w_router, b_router, w_gu, b_gu, w_down, b_down, final_norm):
    batch, seq, d = x.shape
    depth = w_ada.shape[0]
    t = batch * seq
    p = {
        "norm1": norm1.reshape(depth, 1, d),
        "w_in": w_in.astype(BF16),
        "attn_sink": attn_sink.astype(F32),
        "attn_norm": attn_norm.reshape(depth, 1, ATT_WIDTH),
        "hg_norm": hg_norm.reshape(depth, 1, HG_HEAD_DIM),
        "w_out": w_out.astype(BF16),
        "norm2": norm2.reshape(depth, 1, d),
        "w_router_t": jnp.swapaxes(w_router, 1, 2),
        "b_router": b_router.reshape(depth, N_EXPERTS, 1),
        "w_gu": w_gu,
        "b_gu": b_gu.reshape(depth, N_EXPERTS, 1, b_gu.shape[-1]),
        "w_down": w_down,
        "b_down": b_down.reshape(depth, N_EXPERTS, 1, d),
    }
    mod3 = _ada_all(c, w_ada, b_ada).reshape(depth * batch, 1, N_MOD * d)
    lower = _lower_bounds(hg_lb_logits)
    tables = _rope_tables(positions)
    consts = tuple(jnp.asarray(m) for m in _hgrn_constants())
    x2 = x.reshape(t, d)
    for l in range(depth):
        a, o = _mixer_layer(x2, mod3, tables, lower, consts, p, l, batch, seq)
        x2 = _moe_layer(a, o, x2, mod3, p, l, seq, depth, final_norm if l == depth - 1 else None)
    return x2.reshape(batch, seq, d)
```
